```python
import math
import jax, jax.numpy as jnp
from jax import lax
import numpy as np

D_MODEL = 1024
BATCH = 1
SEQ = 16384
DEPTH = 1

CHUNK = 64
EPS = 1e-6
GLA_HEADS = 4
GLA_DK = 128
GLA_DV = 256
GLA_LOWRANK = 16
GLA_TAU = 16.0
DIFF_HEADS = 8
DIFF_DH = 64
DIFF_DV = 2 * DIFF_DH
ROPE_THETA = 500000.0
ROT_DIM = DIFF_DH // 4
Q_BLOCK = 128
N_EXPERTS = 256
TOP_K = 8
N_GROUPS = 8
TOPK_GROUPS = 4
D_EXPERT = 256
D_SHARED = 256
ROUTED_SCALE = 2.5
MOE_BLOCK = 128

GLA_QK = GLA_HEADS * GLA_DK
GLA_V = GLA_HEADS * GLA_DV
DIFF_QK = DIFF_HEADS * 2 * DIFF_DH
DIFF_V = DIFF_HEADS * DIFF_DV
SPLITS = (GLA_QK, GLA_QK, GLA_V, GLA_LOWRANK, GLA_V, DIFF_QK, DIFF_QK, DIFF_V, D_MODEL, D_MODEL)
IN_WIDTH = GLA_QK * 2 + GLA_V * 2 + GLA_LOWRANK + DIFF_QK * 2 + DIFF_V + 2 * D_MODEL

kernel_name = "hybrid_gla_diffattn_moe_adaln_block"


def rms_norm(x, g):
    xf = x.astype(jnp.float32)
    y = xf * lax.rsqrt(jnp.mean(xf * xf, axis=-1, keepdims=True) + EPS)
    return (y * g.astype(jnp.float32)).astype(x.dtype)


def rope_tables(positions):
    pos = positions.astype(jnp.float32)
    inv_freq = ROPE_THETA ** (-jnp.arange(0, ROT_DIM, 2, dtype=jnp.float32) / ROT_DIM)
    ang = pos[..., None] * inv_freq
    return jnp.cos(ang)[:, None, None], jnp.sin(ang)[:, None, None]


def apply_partial_rope(t, cos, sin):
    tf = t.astype(jnp.float32)
    half = ROT_DIM // 2
    t1, t2, rest = tf[..., :half], tf[..., half:ROT_DIM], tf[..., ROT_DIM:]
    out = jnp.concatenate([t1 * cos - t2 * sin, t2 * cos + t1 * sin, rest], axis=-1)
    return out.astype(t.dtype)


def gla_mixer(q, k, v, a_low, g, w_a2, b_a, onorm_g):
    B, S, _ = q.shape
    N = S // CHUNK
    log_a = jax.nn.log_sigmoid((a_low @ w_a2 + b_a).astype(jnp.float32)) / GLA_TAU

    def to_chunks(t, d):
        return t.astype(jnp.float32).reshape(B, N, CHUNK, GLA_HEADS, d).transpose(1, 0, 3, 2, 4)

    qh = to_chunks(q, GLA_DK) * (GLA_DK ** -0.5)
    kh = to_chunks(k, GLA_DK)
    vh = to_chunks(v, GLA_DV)
    lah = to_chunks(log_a, GLA_DK)
    cum = jnp.cumsum(lah, axis=3)
    tot = cum[:, :, :, -1]
    kd = kh * jnp.exp(tot[:, :, :, None, :] - cum)

    def step(state, inp):
        qc, kc, vc, tc = inp
        state = jnp.exp(tc)[..., None] * state + jnp.einsum('bhcd,bhce->bhde', kc, vc)
        o = jnp.einsum('bhcd,bhde->bhce', qc, state)
        return state, o

    state0 = jnp.zeros((B, GLA_HEADS, GLA_DK, GLA_DV), jnp.float32)
    _, o = lax.scan(step, state0, (qh, kd, vh, tot))
    o = o.transpose(1, 0, 3, 2, 4).reshape(B, S, GLA_HEADS, GLA_DV)
    o = rms_norm(o, onorm_g) * jax.nn.silu(g.astype(jnp.float32).reshape(B, S, GLA_HEADS, GLA_DV))
    return o.reshape(B, S, GLA_V).astype(q.dtype)


def diff_mixer(q, k, v, cos, sin, qn_g, kn_g, lq1, lk1, lq2, lk2, subln_g, lambda_init):
    B, S, _ = q.shape
    qh = q.reshape(B, S, DIFF_HEADS, 2, DIFF_DH).transpose(0, 2, 3, 1, 4)
    kh = k.reshape(B, S, DIFF_HEADS, 2, DIFF_DH).transpose(0, 2, 3, 1, 4)
    vh = v.reshape(B, S, DIFF_HEADS, DIFF_DV).transpose(0, 2, 1, 3)
    qh = apply_partial_rope(rms_norm(qh, qn_g), cos, sin)
    kh = apply_partial_rope(rms_norm(kh, kn_g), cos, sin)
    f32 = jnp.float32
    lam = (jnp.exp(jnp.sum(lq1.astype(f32) * lk1.astype(f32)))
           - jnp.exp(jnp.sum(lq2.astype(f32) * lk2.astype(f32))) + lambda_init)

    nb = S // Q_BLOCK
    q_blocks = qh.reshape(B, DIFF_HEADS, 2, nb, Q_BLOCK, DIFF_DH).transpose(3, 0, 1, 2, 4, 5)
    key_chunk = jnp.arange(S) // CHUNK
    q_chunk = (jnp.arange(S) // CHUNK).reshape(nb, Q_BLOCK)
    scale = DIFF_DH ** -0.5

    def attend(args):
        qb, qc = args
        s = jnp.einsum('bhmqd,bhmkd->bhmqk', qb, kh).astype(f32) * scale
        mask = key_chunk[None, :] <= qc[:, None]
        s = jnp.where(mask, s, -jnp.inf)
        p = jax.nn.softmax(s, axis=-1)
        attn = p[:, :, 0] - lam * p[:, :, 1]
        return jnp.einsum('bhqk,bhke->bhqe', attn.astype(vh.dtype), vh)

    o = lax.map(attend, (q_blocks, q_chunk))
    o = o.transpose(1, 0, 3, 2, 4).reshape(B, S, DIFF_HEADS, DIFF_DV)
    o = rms_norm(o, subln_g) * (1.0 - lambda_init)
    return o.reshape(B, S, DIFF_V).astype(q.dtype)


def swiglu(h, wg, wu, wd):
    return (jax.nn.silu(h @ wg) * (h @ wu)) @ wd


def routed_moe(hf, w_router, router_bias, wg, wu, wd):
    T, D = hf.shape
    E = N_EXPERTS
    scores = jax.nn.sigmoid((hf @ w_router).astype(jnp.float32))
    biased = scores + router_bias.astype(jnp.float32)
    grp = biased.reshape(T, N_GROUPS, E // N_GROUPS)
    grp_score = jnp.sum(lax.top_k(grp, 2)[0], axis=-1)
    _, top_grp = lax.top_k(grp_score, TOPK_GROUPS)
    grp_mask = jnp.sum(jax.nn.one_hot(top_grp, N_GROUPS, dtype=jnp.float32), axis=-2) > 0
    expert_mask = jnp.repeat(grp_mask, E // N_GROUPS, axis=-1)
    masked = jnp.where(expert_mask, biased, -jnp.inf)
    _, idx = lax.top_k(masked, TOP_K)
    sel = jnp.take_along_axis(scores, idx, axis=-1)
    wts = sel / jnp.sum(sel, axis=-1, keepdims=True) * ROUTED_SCALE

    M = T * TOP_K
    flat_e = idx.reshape(M).astype(jnp.int32)
    flat_t = jnp.repeat(jnp.arange(T, dtype=jnp.int32), TOP_K)
    flat_w = wts.reshape(M)
    order = jnp.argsort(flat_e)
    se = flat_e[order]
    counts = jnp.bincount(flat_e, length=E).astype(jnp.int32)
    pcounts = ((counts + MOE_BLOCK - 1) // MOE_BLOCK) * MOE_BLOCK
    pend = jnp.cumsum(pcounts)
    pstart = pend - pcounts
    start = jnp.cumsum(counts) - counts
    dest = pstart[se] + jnp.arange(M, dtype=jnp.int32) - start[se]
    n_blocks = -(-M // MOE_BLOCK) + E
    m_pad = n_blocks * MOE_BLOCK
    tok_pad = jnp.zeros((m_pad,), jnp.int32).at[dest].set(flat_t[order])
    w_pad = jnp.zeros((m_pad,), hf.dtype).at[dest].set(flat_w[order].astype(hf.dtype))
    blk_e = jnp.minimum(jnp.searchsorted(pend, jnp.arange(n_blocks, dtype=jnp.int32) * MOE_BLOCK,
                                         side='right'), E - 1).astype(jnp.int32)

    def body(y, blk):
        tok, e, wt = blk
        out = swiglu(hf[tok], wg[e], wu[e], wd[e])
        return y.at[tok].add(out * wt[:, None]), None

    y, _ = lax.scan(body, jnp.zeros((T, D), hf.dtype),
                    (tok_pad.reshape(n_blocks, MOE_BLOCK), blk_e, w_pad.reshape(n_blocks, MOE_BLOCK)))
    return y


def setup_inputs(seed: int = 0) -> dict:
    key = jax.random.key(seed)
    ks = iter(jax.random.split(key, 40))
    L, D, E, F = DEPTH, D_MODEL, N_EXPERTS, D_EXPERT

    def nrm(shape, scale):
        return jax.random.normal(next(ks), shape, jnp.float32) * scale

    def gain(shape):
        return 1.0 + nrm(shape, 0.02)

    return {
        "x": nrm((BATCH, SEQ, D), 1.0),
        "c": nrm((BATCH, D), 1.0),
        "positions": (jax.random.randint(next(ks), (BATCH, 1), 0, 65536, jnp.int32)
                      + jnp.arange(SEQ, dtype=jnp.int32)[None, :]),
        "w_ada": nrm((L, D, 6 * D), 0.3 * D ** -0.5),
        "b_ada": nrm((L, 6 * D), 0.02),
        "norm1_g": gain((L, D)),
        "w_in": nrm((L, D, IN_WIDTH), D ** -0.5),
        "gla_w_a2": nrm((L, GLA_LOWRANK, GLA_QK), GLA_LOWRANK ** -0.5),
        "gla_b_a": nrm((L, GLA_QK), 0.1),
        "gla_onorm_g": gain((L, GLA_DV)),
        "diff_qnorm_g": gain((L, DIFF_DH)),
        "diff_knorm_g": gain((L, DIFF_DH)),
        "diff_lq1": nrm((L, DIFF_DH), 0.1),
        "diff_lk1": nrm((L, DIFF_DH), 0.1),
        "diff_lq2": nrm((L, DIFF_DH), 0.1),
        "diff_lk2": nrm((L, DIFF_DH), 0.1),
        "diff_subln_g": gain((L, DIFF_DV)),
        "w_branch_gla": nrm((L, GLA_V, D), GLA_V ** -0.5),
        "w_branch_diff": nrm((L, DIFF_V, D), DIFF_V ** -0.5),
        "w_out": nrm((L, D, D), D ** -0.5),
        "norm2_g": gain((L, D)),
        "w_router": nrm((L, D, E), D ** -0.5),
        "router_bias": nrm((L, E), 0.01),
        "w_exp_gate": nrm((L, E, D, F), D ** -0.5),
        "w_exp_up": nrm((L, E, D, F), D ** -0.5),
        "w_exp_down": nrm((L, E, F, D), F ** -0.5),
        "w_sh_gate": nrm((L, D, D_SHARED), D ** -0.5),
        "w_sh_up": nrm((L, D, D_SHARED), D ** -0.5),
        "w_sh_down": nrm((L, D_SHARED, D), D_SHARED ** -0.5),
    }


def reference(x, c, positions, w_ada, b_ada, norm1_g, w_in, gla_w_a2, gla_b_a, gla_onorm_g,
              diff_qnorm_g, diff_knorm_g, diff_lq1, diff_lk1, diff_lq2, diff_lk2, diff_subln_g,
              w_branch_gla, w_branch_diff, w_out, norm2_g, w_router, router_bias,
              w_exp_gate, w_exp_up, w_exp_down, w_sh_gate, w_sh_up, w_sh_down):
    B, S, D = x.shape
    cos, sin = rope_tables(positions)
    c_act = jax.nn.silu(c)
    split_points = [int(p) for p in np.cumsum(SPLITS)[:-1]]
    for l in range(DEPTH):
        lambda_init = 0.8 - 0.6 * math.exp(-0.3 * l)
        mod = c_act @ w_ada[l] + b_ada[l]
        sh_a, sc_a, gt_a, sh_f, sc_f, gt_f = [m[:, None, :] for m in jnp.split(mod, 6, axis=-1)]

        h = rms_norm(x, norm1_g[l]) * (1.0 + sc_a) + sh_a
        proj = h @ w_in[l]
        gq, gk, gv, ga, gg, dq, dk, dv, mg_gla, mg_diff = jnp.split(proj, split_points, axis=-1)
        o_gla = gla_mixer(gq, gk, gv, ga, gg, gla_w_a2[l], gla_b_a[l], gla_onorm_g[l])
        o_diff = diff_mixer(dq, dk, dv, cos, sin, diff_qnorm_g[l], diff_knorm_g[l],
                            diff_lq1[l], diff_lk1[l], diff_lq2[l], diff_lk2[l],
                            diff_subln_g[l], lambda_init)
        merged = (jax.nn.sigmoid(mg_gla) * (o_gla @ w_branch_gla[l])
                  + jax.nn.sigmoid(mg_diff) * (o_diff @ w_branch_diff[l]))
        x = x + gt_a * (merged @ w_out[l])

        h2 = rms_norm(x, norm2_g[l]) * (1.0 + sc_f) + sh_f
        hf = h2.reshape(B * S, D)
        shared = swiglu(hf, w_sh_gate[l], w_sh_up[l], w_sh_down[l])
        routed = routed_moe(hf, w_router[l], router_bias[l], w_exp_gate[l], w_exp_up[l], w_exp_down[l])
        x = x + gt_f * (shared + routed).reshape(B, S, D)
    return x
```

```python
import functools
import math

import jax
import jax.numpy as jnp
from jax import lax
from jax.experimental import pallas as pl
from jax.experimental.pallas import tpu as pltpu

CHUNK = 64
EPS = 1e-6
GLA_HEADS = 4
GLA_DK = 128
GLA_DV = 256
GLA_TAU = 16.0
DIFF_HEADS = 8
DIFF_DH = 64
DIFF_DV = 2 * DIFF_DH
ROPE_THETA = 500000.0
ROT_DIM = DIFF_DH // 4
N_GROUPS = 8
TOPK_GROUPS = 4
TOP_K = 8
ROUTED_SCALE = 2.5

MOE_ROWS = 128
VMEM_LIMIT = 56 * 1024 * 1024
NEG_BIG = -1e30
HIGHEST = lax.Precision.HIGHEST
F32 = jnp.float32
BF16 = jnp.bfloat16


def _cparams(sem):
    return pltpu.CompilerParams(dimension_semantics=sem, vmem_limit_bytes=VMEM_LIMIT)


def _nt_dot(a, b, precision=None):
    return lax.dot_general(a, b, (((1,), (1,)), ((), ())), precision=precision,
                           preferred_element_type=F32)


def _rms_mod(x, g, sc, sh):
    xn = x * lax.rsqrt(jnp.mean(x * x, axis=-1, keepdims=True) + EPS)
    return (xn * g) * (1.0 + sc) + sh


def _ada_kernel(c_ref, w_ref, b_ref, o_ref):
    c = c_ref[...]
    ca = c * jax.nn.sigmoid(c)
    o_ref[...] = jnp.sum(ca * w_ref[...], axis=0, keepdims=True) + b_ref[...]


def _ada(c_col, w, b):
    d, n = w.shape
    tn = min(1024, n)
    return pl.pallas_call(
        _ada_kernel,
        grid=(n // tn,),
        in_specs=[pl.BlockSpec((d, 1), lambda j: (0, 0)),
                  pl.BlockSpec((d, tn), lambda j: (0, j)),
                  pl.BlockSpec((1, tn), lambda j: (0, j))],
        out_specs=pl.BlockSpec((1, tn), lambda j: (0, j)),
        out_shape=jax.ShapeDtypeStruct((1, n), F32),
        compiler_params=_cparams(("arbitrary",)),
        name="ada",
    )(c_col, w, b)


def _inproj_kernel(x_ref, g_ref, sc_ref, sh_ref, w_ref, wga_ref, o_ref, ga_ref, h_scr):
    @pl.when(pl.program_id(1) == 0)
    def _():
        h = _rms_mod(x_ref[...], g_ref[...], sc_ref[...], sh_ref[...]).astype(BF16)
        h_scr[...] = h
        ga_ref[...] = jnp.dot(h, wga_ref[...], preferred_element_type=F32)

    o_ref[...] = jnp.dot(h_scr[...], w_ref[...], preferred_element_type=F32).astype(BF16)


def _inproj(x, g, sc, sh, w, wga, tm, tn):
    s, d = x.shape
    n = w.shape[1]
    vec = pl.BlockSpec((1, d), lambda i, j: (0, 0))
    return pl.pallas_call(
        _inproj_kernel,
        grid=(s // tm, n // tn),
        in_specs=[pl.BlockSpec((tm, d), lambda i, j: (i, 0)), vec, vec, vec,
                  pl.BlockSpec((d, tn), lambda i, j: (0, j)),
                  pl.BlockSpec((d, 128), lambda i, j: (0, 0))],
        out_specs=[pl.BlockSpec((tm, tn), lambda i, j: (i, j)),
                   pl.BlockSpec((tm, 128), lambda i, j: (i, 0))],
        out_shape=[jax.ShapeDtypeStruct((s, n), BF16), jax.ShapeDtypeStruct((s, 128), F32)],
        scratch_shapes=[pltpu.VMEM((tm, d), BF16)],
        compiler_params=_cparams(("arbitrary", "arbitrary")),
        name="inproj",
    )(x, g, sc, sh, w, wga)


def _inproj_t_kernel(x_ref, g_ref, sc_ref, sh_ref, wt_ref, o_ref, h_scr):
    @pl.when(pl.program_id(1) == 0)
    def _():
        h_scr[...] = _rms_mod(x_ref[...], g_ref[...], sc_ref[...], sh_ref[...]).astype(BF16)

    o_ref[...] = _nt_dot(wt_ref[...], h_scr[...]).astype(BF16)


def _inproj_t(x, g, sc, sh, wt, tm, tn):
    s, d = x.shape
    n = wt.shape[0]
    vec = pl.BlockSpec((1, d), lambda i, j: (0, 0))
    return pl.pallas_call(
        _inproj_t_kernel,
        grid=(s // tm, n // tn),
        in_specs=[pl.BlockSpec((tm, d), lambda i, j: (i, 0)), vec, vec, vec,
                  pl.BlockSpec((tn, d), lambda i, j: (j, 0))],
        out_specs=pl.BlockSpec((tn, tm), lambda i, j: (j, i)),
        out_shape=jax.ShapeDtypeStruct((n, s), BF16),
        scratch_shapes=[pltpu.VMEM((tm, d), BF16)],
        compiler_params=_cparams(("arbitrary", "arbitrary")),
        name="inproj_t",
    )(x, g, sc, sh, wt)


def _gla_kernel(q_ref, kt_ref, v_ref, gg_ref, ga_ref, wa2t_ref, ba_ref, on_ref, o_ref,
                state_ref, o_scr):
    tt = q_ref.shape[0]
    nchunk = tt // CHUNK

    @pl.when(pl.program_id(0) == 0)
    def _():
        state_ref[...] = jnp.zeros_like(state_ref)

    zt = _nt_dot(wa2t_ref[...], ga_ref[...], precision=HIGHEST) + ba_ref[...]
    lat = (jnp.minimum(zt, 0.0) - jnp.log1p(jnp.exp(-jnp.abs(zt)))) * (1.0 / GLA_TAU)
    row = lax.broadcasted_iota(jnp.int32, (tt, tt), 0)
    col = lax.broadcasted_iota(jnp.int32, (tt, tt), 1)
    same = (row // CHUNK) == (col // CHUNK)
    incl = jnp.where(same & (row <= col), 1.0, 0.0).astype(F32)
    full = jnp.where(same, 1.0, 0.0).astype(F32)
    cumt = jnp.dot(lat, incl, precision=HIGHEST, preferred_element_type=F32)
    tott = jnp.dot(lat, full, precision=HIGHEST, preferred_element_type=F32)
    kdt = kt_ref[...].astype(F32) * jnp.exp(tott - cumt)
    dec = jnp.exp(tott)

    lane = lax.broadcasted_iota(jnp.int32, (GLA_DK, 2 * CHUNK), 1)
    for c in range(nchunk):
        pair = (c // 2) * 2 * CHUNK
        if nchunk > 1:
            keep = (lane // CHUNK) == (c % 2)
        for h in range(GLA_HEADS):
            rows = slice(h * GLA_DK, (h + 1) * GLA_DK)
            vcols = slice(h * GLA_DV, (h + 1) * GLA_DV)
            if nchunk > 1:
                a = jnp.where(keep, kdt[rows, pair:pair + 2 * CHUNK], 0.0).astype(BF16)
                vp = v_ref[pair:pair + 2 * CHUNK, vcols]
            else:
                a = kdt[rows, :].astype(BF16)
                vp = v_ref[:, vcols]
            upd = jnp.dot(a, vp, preferred_element_type=F32)
            dcol = dec[rows, c * CHUNK:c * CHUNK + 1]
            st = state_ref[h] * dcol + upd
            state_ref[h] = st
            qc = q_ref[c * CHUNK:(c + 1) * CHUNK, rows]
            o_scr[c * CHUNK:(c + 1) * CHUNK, vcols] = jnp.dot(
                qc, st.astype(BF16), preferred_element_type=F32)

    for h in range(GLA_HEADS):
        vcols = slice(h * GLA_DV, (h + 1) * GLA_DV)
        o = o_scr[:, vcols] * (GLA_DK ** -0.5)
        o = o * lax.rsqrt(jnp.mean(o * o, axis=-1, keepdims=True) + EPS) * on_ref[...]
        g = gg_ref[:, vcols].astype(F32)
        o_ref[:, vcols] = (o * (g * jax.nn.sigmoid(g))).astype(BF16)


def _gla(proj, projt, ga, wa2t, ba_col, on_g, tt, col_q, col_v, col_g, row_k):
    s = proj.shape[0]
    qk = GLA_HEADS * GLA_DK
    vw = GLA_HEADS * GLA_DV
    return pl.pallas_call(
        _gla_kernel,
        grid=(s // tt,),
        in_specs=[pl.BlockSpec((tt, qk), lambda i: (i, col_q // qk)),
                  pl.BlockSpec((qk, tt), lambda i: (row_k // qk, i)),
                  pl.BlockSpec((tt, vw), lambda i: (i, col_v // vw)),
                  pl.BlockSpec((tt, vw), lambda i: (i, col_g // vw)),
                  pl.BlockSpec((tt, 128), lambda i: (i, 0)),
                  pl.BlockSpec((qk, 128), lambda i: (0, 0)),
                  pl.BlockSpec((qk, 1), lambda i: (0, 0)),
                  pl.BlockSpec((1, GLA_DV), lambda i: (0, 0))],
        out_specs=pl.BlockSpec((tt, vw), lambda i: (i, 0)),
        out_shape=jax.ShapeDtypeStruct((s, vw), BF16),
        scratch_shapes=[pltpu.VMEM((GLA_HEADS, GLA_DK, GLA_DV), F32),
                        pltpu.VMEM((tt, vw), F32)],
        compiler_params=_cparams(("arbitrary",)),
        name="gla",
    )(proj, projt, proj, proj, ga, wa2t, ba_col, on_g)


def _qknorm_rope_t(xt, g_col, cos, sin):
    n, tm = xt.shape
    x3 = xt.reshape(n // DIFF_DH, DIFF_DH, tm)
    r = lax.rsqrt(jnp.mean(x3 * x3, axis=1, keepdims=True) + EPS)
    y = x3 * r * g_col[None]
    half = ROT_DIM // 2
    y1, y2, rest = y[:, :half], y[:, half:ROT_DIM], y[:, ROT_DIM:]
    o1 = y1 * cos[None] - y2 * sin[None]
    o2 = y2 * cos[None] + y1 * sin[None]
    return jnp.concatenate([o1, o2, rest], axis=1)


def _qkprep_kernel(qt_ref, kt_ref, pos_ref, invf_ref, qg_ref, kg_ref, qa_ref, qb_ref, ko_ref):
    tm = qt_ref.shape[1]
    ang = pos_ref[...].astype(F32) * invf_ref[...]
    cos, sin = jnp.cos(ang), jnp.sin(ang)
    k3 = _qknorm_rope_t(kt_ref[...].astype(F32), kg_ref[...], cos, sin)
    ko_ref[...] = k3.reshape(-1, tm).astype(BF16)
    q3 = _qknorm_rope_t(qt_ref[...].astype(F32), qg_ref[...], cos, sin) * (DIFF_DH ** -0.5)
    seg = lax.broadcasted_iota(jnp.int32, q3.shape, 0)
    qa = jnp.where(seg % 2 == 0, q3, 0.0).reshape(-1, tm)
    qb = jnp.where(seg % 2 == 1, q3, 0.0).reshape(-1, tm)
    qa_ref[...] = qa.T.astype(BF16)
    qb_ref[...] = qb.T.astype(BF16)


def _qkprep(projt, pos_row, invf_col, qg_col, kg_col, tm, row_q, row_k):
    s = projt.shape[1]
    n = DIFF_HEADS * 2 * DIFF_DH
    col = pl.BlockSpec((DIFF_DH, 1), lambda i: (0, 0))
    return pl.pallas_call(
        _qkprep_kernel,
        grid=(s // tm,),
        in_specs=[pl.BlockSpec((n, tm), lambda i: (row_q // n, i)),
                  pl.BlockSpec((n, tm), lambda i: (row_k // n, i)),
                  pl.BlockSpec((1, tm), lambda i: (0, i)),
                  pl.BlockSpec((ROT_DIM // 2, 1), lambda i: (0, 0)), col, col],
        out_specs=[pl.BlockSpec((tm, n), lambda i: (i, 0)),
                   pl.BlockSpec((tm, n), lambda i: (i, 0)),
                   pl.BlockSpec((n, tm), lambda i: (0, i))],
        out_shape=[jax.ShapeDtypeStruct((s, n), BF16), jax.ShapeDtypeStruct((s, n), BF16),
                   jax.ShapeDtypeStruct((n, s), BF16)],
        compiler_params=_cparams(("arbitrary",)),
        name="qkprep",
    )(projt, projt, pos_row, invf_col, qg_col, kg_col)


def _diffattn_kernel(qa_ref, qb_ref, kt_ref, v_ref, lq1_ref, lk1_ref, lq2_ref, lk2_ref, sg_ref,
                     o_ref, qs_scr, m_scr, l_scr, acc_scr, *, lambda_init, tk):
    tq = qa_ref.shape[0]
    i = pl.program_id(1)
    qs_scr[:tq] = qa_ref[...]
    qs_scr[tq:] = qb_ref[...]
    m_scr[...] = jnp.full_like(m_scr, NEG_BIG)
    l_scr[...] = jnp.zeros_like(l_scr)
    acc_scr[...] = jnp.zeros_like(acc_scr)

    def step(j, masked):
        start = pl.multiple_of(j * tk, tk)
        s = jnp.dot(qs_scr[...], kt_ref[:, pl.ds(start, tk)], preferred_element_type=F32)
        if masked:
            qrow = lax.broadcasted_iota(jnp.int32, (2 * tq, tk), 0) % tq
            kcol = lax.broadcasted_iota(jnp.int32, (2 * tq, tk), 1)
            vis = (start + kcol) // CHUNK <= (i * tq + qrow) // CHUNK
            s = jnp.where(vis, s, NEG_BIG)
        m_prev = m_scr[...]
        m_new = jnp.maximum(m_prev, jnp.max(s, axis=1, keepdims=True))
        alpha = jnp.exp(m_prev - m_new)
        p = jnp.exp(s - m_new)
        l_scr[...] = alpha * l_scr[...] + jnp.sum(p, axis=1, keepdims=True)
        acc_scr[...] = alpha * acc_scr[...] + jnp.dot(
            p.astype(BF16), v_ref[pl.ds(start, tk), :], preferred_element_type=F32)
        m_scr[...] = m_new

    n_full = (i * tq) // tk

    def body(j, carry):
        step(j, False)
        return carry

    lax.fori_loop(0, n_full, body, 0)
    n_all = ((i + 1) * tq + tk - 1) // tk
    lax.fori_loop(n_full, n_all, lambda j, c: (step(j, True), c)[1], 0)

    o = acc_scr[...] / l_scr[...]
    lam = (jnp.exp(jnp.sum(lq1_ref[...] * lk1_ref[...]))
           - jnp.exp(jnp.sum(lq2_ref[...] * lk2_ref[...])) + lambda_init)
    o = o[:tq] - lam * o[tq:]
    o = o * lax.rsqrt(jnp.mean(o * o, axis=-1, keepdims=True) + EPS) * sg_ref[...]
    o_ref[...] = (o * (1.0 - lambda_init)).astype(BF16)


def _diffattn(qa, qb, kt, proj, lq1, lk1, lq2, lk2, sg, lambda_init, tq, tk, col_v):
    s = qa.shape[0]
    hd = 2 * DIFF_DH
    vec = pl.BlockSpec((1, DIFF_DH), lambda h, i: (0, 0))
    kern = functools.partial(_diffattn_kernel, lambda_init=lambda_init, tk=tk)
    return pl.pallas_call(
        kern,
        grid=(DIFF_HEADS, s // tq),
        in_specs=[pl.BlockSpec((tq, hd), lambda h, i: (i, h)),
                  pl.BlockSpec((tq, hd), lambda h, i: (i, h)),
                  pl.BlockSpec((hd, s), lambda h, i: (h, 0)),
                  pl.BlockSpec((s, DIFF_DV), lambda h, i: (0, col_v // DIFF_DV + h)),
                  vec, vec, vec, vec,
                  pl.BlockSpec((1, DIFF_DV), lambda h, i: (0, 0))],
        out_specs=pl.BlockSpec((tq, DIFF_DV), lambda h, i: (i, h)),
        out_shape=jax.ShapeDtypeStruct((s, DIFF_HEADS * DIFF_DV), BF16),
        scratch_shapes=[pltpu.VMEM((2 * tq, hd), BF16),
                        pltpu.VMEM((2 * tq, 1), F32),
                        pltpu.VMEM((2 * tq, 1), F32),
                        pltpu.VMEM((2 * tq, DIFF_DV), F32)],
        compiler_params=_cparams(("arbitrary", "arbitrary")),
        name="diffattn",
    )(qa, qb, kt, proj, lq1, lk1, lq2, lk2, sg)


def _mergeout_kernel(og_ref, od_ref, mg_ref, md_ref, x_ref, wbg_ref, wbd_ref, wo_ref, gt_ref,
                     g2_ref, sc_ref, sh_ref, x1_ref, hf_ref):
    bg = jnp.dot(og_ref[...], wbg_ref[...], preferred_element_type=F32)
    bd = jnp.dot(od_ref[...], wbd_ref[...], preferred_element_type=F32)
    merged = (jax.nn.sigmoid(mg_ref[...].astype(F32)) * bg
              + jax.nn.sigmoid(md_ref[...].astype(F32)) * bd)
    x1 = x_ref[...] + gt_ref[...] * jnp.dot(merged.astype(BF16), wo_ref[...],
                                             preferred_element_type=F32)
    x1_ref[...] = x1
    hf_ref[...] = _rms_mod(x1, g2_ref[...], sc_ref[...], sh_ref[...])


def _mergeout(og, od, proj, x, wbg, wbd, wo, gt, g2, sc, sh, tm, col_mg, col_md):
    s, d = x.shape
    vec = pl.BlockSpec((1, d), lambda i: (0, 0))
    wspec = pl.BlockSpec((d, d), lambda i: (0, 0))
    row = pl.BlockSpec((tm, d), lambda i: (i, 0))
    return pl.pallas_call(
        _mergeout_kernel,
        grid=(s // tm,),
        in_specs=[row, row,
                  pl.BlockSpec((tm, d), lambda i: (i, col_mg // d)),
                  pl.BlockSpec((tm, d), lambda i: (i, col_md // d)),
                  row, wspec, wspec, wspec, vec, vec, vec, vec],
        out_specs=[row, row],
        out_shape=[jax.ShapeDtypeStruct((s, d), F32), jax.ShapeDtypeStruct((s, d), F32)],
        compiler_params=_cparams(("arbitrary",)),
        name="mergeout",
    )(og, od, proj, proj, x, wbg, wbd, wo, gt, g2, sc, sh)


def _route_kernel(hf_ref, wrt_ref, bias_ref, idx_ref, wts_ref, rnk_ref, cnt_ref, run_scr):
    tr = hf_ref.shape[0]
    e = wrt_ref.shape[0]
    gsz = e // N_GROUPS

    @pl.when(pl.program_id(0) == 0)
    def _():
        run_scr[...] = jnp.zeros_like(run_scr)

    logits = _nt_dot(wrt_ref[...], hf_ref[...], precision=HIGHEST)
    scores = jax.nn.sigmoid(logits)
    biased = scores + bias_ref[...]
    g3 = biased.reshape(N_GROUPS, gsz, tr)
    m1 = jnp.max(g3, axis=1, keepdims=True)
    n_top = jnp.sum(jnp.where(g3 == m1, 1.0, 0.0), axis=1, keepdims=True)
    m2 = jnp.max(jnp.where(g3 < m1, g3, -jnp.inf), axis=1, keepdims=True)
    gs = (m1 + jnp.where(n_top >= 2.0, m1, m2)).reshape(N_GROUPS, tr)
    gi = lax.broadcasted_iota(jnp.int32, (N_GROUPS, tr), 0)
    beaten = jnp.zeros((N_GROUPS, tr), F32)
    for g in range(N_GROUPS):
        other = gs[g:g + 1, :]
        beaten = beaten + jnp.where((other > gs) | ((other == gs) & (g < gi)), 1.0, 0.0)
    gsel = (beaten < float(TOPK_GROUPS)).reshape(N_GROUPS, 1, tr)
    masked = jnp.where(gsel, g3, -jnp.inf).reshape(e, tr)

    ids = lax.broadcasted_iota(jnp.int32, (e, tr), 0)
    chosen = jnp.zeros((e, tr), F32)
    sel_idx, sel_score = [], []
    for _ in range(TOP_K):
        mx = jnp.max(masked, axis=0, keepdims=True)
        ix = jnp.min(jnp.where(masked == mx, ids, e), axis=0, keepdims=True)
        hit = ids == ix
        sel_idx.append(ix)
        sel_score.append(jnp.sum(jnp.where(hit, scores, 0.0), axis=0, keepdims=True))
        chosen = jnp.where(hit, 1.0, chosen)
        masked = jnp.where(hit, -jnp.inf, masked)
    idx = jnp.concatenate(sel_idx, axis=0)
    sc = jnp.concatenate(sel_score, axis=0)
    idx_ref[...] = idx
    wts_ref[...] = sc / jnp.sum(sc, axis=0, keepdims=True) * ROUTED_SCALE

    row = lax.broadcasted_iota(jnp.int32, (tr, tr), 0)
    col = lax.broadcasted_iota(jnp.int32, (tr, tr), 1)
    before = jnp.where(row < col, 1.0, 0.0).astype(BF16)
    prior = jnp.dot(chosen.astype(BF16), before, preferred_element_type=F32) + run_scr[:, 0:1]
    rnk_ref[...] = jnp.concatenate(
        [jnp.sum(jnp.where(ids == sel_idx[k], prior, 0.0), axis=0, keepdims=True)
         for k in range(TOP_K)], axis=0).astype(jnp.int32)
    run_scr[...] = run_scr[...] + jnp.sum(chosen, axis=1, keepdims=True)
    cnt_ref[...] = run_scr[...].astype(jnp.int32)


def _route(hf, wrt, bias_col, tr):
    s, d = hf.shape
    e = wrt.shape[0]
    tok = pl.BlockSpec((TOP_K, tr), lambda i: (0, i))
    return pl.pallas_call(
        _route_kernel,
        grid=(s // tr,),
        in_specs=[pl.BlockSpec((tr, d), lambda i: (i, 0)),
                  pl.BlockSpec((e, d), lambda i: (0, 0)),
                  pl.BlockSpec((e, 1), lambda i: (0, 0))],
        out_specs=[tok, tok, tok, pl.BlockSpec((e, 128), lambda i: (0, 0))],
        out_shape=[jax.ShapeDtypeStruct((TOP_K, s), jnp.int32),
                   jax.ShapeDtypeStruct((TOP_K, s), F32),
                   jax.ShapeDtypeStruct((TOP_K, s), jnp.int32),
                   jax.ShapeDtypeStruct((e, 128), jnp.int32)],
        scratch_shapes=[pltpu.VMEM((e, 128), F32)],
        compiler_params=_cparams(("arbitrary",)),
        name="route",
    )(hf, wrt, bias_col)


def _row_copy(src_ref, src_row, dst_ref, dst_row, sem):
    return pltpu.make_async_copy(src_ref.at[pl.ds(src_row, 1), :],
                                 dst_ref.at[pl.ds(dst_row, 1), :], sem)


def _dispatch_kernel(pos_ref, hf_ref, xs_in_ref, xs_ref, sem):
    del xs_in_ref
    td = hf_ref.shape[0]

    def start(t, c):
        for k in range(TOP_K):
            _row_copy(hf_ref, t, xs_ref, pos_ref[k, t], sem).start()
        return c

    def wait(t, c):
        for k in range(TOP_K):
            _row_copy(hf_ref, t, xs_ref, pos_ref[k, t], sem).wait()
        return c

    lax.fori_loop(0, td, start, 0)
    lax.fori_loop(0, td, wait, 0)


def _dispatch(pos3, hf, xs_zero, td):
    s, d = hf.shape
    return pl.pallas_call(
        _dispatch_kernel,
        grid=(s // td,),
        in_specs=[pl.BlockSpec((None, TOP_K, td), lambda i: (i, 0, 0), memory_space=pltpu.SMEM),
                  pl.BlockSpec((td, d), lambda i: (i, 0)),
                  pl.BlockSpec(memory_space=pl.ANY)],
        out_specs=pl.BlockSpec(memory_space=pl.ANY),
        out_shape=jax.ShapeDtypeStruct(xs_zero.shape, xs_zero.dtype),
        scratch_shapes=[pltpu.SemaphoreType.DMA(())],
        input_output_aliases={2: 0},
        compiler_params=_cparams(("arbitrary",)),
        name="dispatch",
    )(pos3, hf, xs_zero)


def _moe_kernel(ie_ref, ib_ref, nv_ref, xs_ref, wg_ref, wu_ref, wd_ref, ys_ref):
    del ie_ref, ib_ref

    @pl.when(pl.program_id(0) < nv_ref[0])
    def _():
        xb = xs_ref[...].astype(BF16)
        g = jnp.dot(xb, wg_ref[...].astype(BF16), preferred_element_type=F32)
        u = jnp.dot(xb, wu_ref[...].astype(BF16), preferred_element_type=F32)
        h = (g * jax.nn.sigmoid(g)) * u
        ys_ref[...] = jnp.dot(h.astype(BF16), wd_ref[...].astype(BF16),
                              preferred_element_type=F32)


def _moe(item_e, item_b, n_valid, xs, wg, wu, wd):
    m_pad, d = xs.shape
    f = wg.shape[2]
    n_items = item_e.shape[0]
    return pl.pallas_call(
        _moe_kernel,
        grid_spec=pltpu.PrefetchScalarGridSpec(
            num_scalar_prefetch=3,
            grid=(n_items,),
            in_specs=[pl.BlockSpec((MOE_ROWS, d), lambda i, ie, ib, nv: (ib[i], 0)),
                      pl.BlockSpec((None, d, f), lambda i, ie, ib, nv: (ie[i], 0, 0)),
                      pl.BlockSpec((None, d, f), lambda i, ie, ib, nv: (ie[i], 0, 0)),
                      pl.BlockSpec((None, f, d), lambda i, ie, ib, nv: (ie[i], 0, 0))],
            out_specs=pl.BlockSpec((MOE_ROWS, d), lambda i, ie, ib, nv: (ib[i], 0)),
        ),
        out_shape=jax.ShapeDtypeStruct((m_pad, d), F32),
        compiler_params=_cparams(("arbitrary",)),
        name="moe",
    )(item_e, item_b, n_valid, xs, wg, wu, wd)


def _combine_kernel(pos_ref, wt_ref, hf_ref, x1_ref, gt_ref, sg_ref, su_ref, sd_ref, ys_ref,
                    o_ref, gbuf, sem):
    tc = hf_ref.shape[0]

    def copy(t, k):
        return pltpu.make_async_copy(ys_ref.at[pl.ds(pos_ref[k, t], 1), :],
                                     gbuf.at[k, pl.ds(t, 1), :], sem)

    def start(t, c):
        for k in range(TOP_K):
            copy(t, k).start()
        return c

    def wait(t, c):
        for k in range(TOP_K):
            copy(t, k).wait()
        return c

    lax.fori_loop(0, tc, start, 0)
    hb = hf_ref[...].astype(BF16)
    g = jnp.dot(hb, sg_ref[...], preferred_element_type=F32)
    u = jnp.dot(hb, su_ref[...], preferred_element_type=F32)
    y = jnp.dot(((g * jax.nn.sigmoid(g)) * u).astype(BF16), sd_ref[...],
                preferred_element_type=F32)
    lax.fori_loop(0, tc, wait, 0)
    wt = wt_ref[...]
    for k in range(TOP_K):
        y = y + gbuf[k] * wt[:, k:k + 1]
    o_ref[...] = x1_ref[...] + gt_ref[...] * y


def _combine(pos3, wts_t, hf, x1, gt, sg, su, sd, ys, tc):
    s, d = hf.shape
    f = sg.shape[1]
    row = pl.BlockSpec((tc, d), lambda i: (i, 0))
    return pl.pallas_call(
        _combine_kernel,
        grid=(s // tc,),
        in_specs=[pl.BlockSpec((None, TOP_K, tc), lambda i: (i, 0, 0), memory_space=pltpu.SMEM),
                  pl.BlockSpec((tc, TOP_K), lambda i: (i, 0)),
                  row, row,
                  pl.BlockSpec((1, d), lambda i: (0, 0)),
                  pl.BlockSpec((d, f), lambda i: (0, 0)),
                  pl.BlockSpec((d, f), lambda i: (0, 0)),
                  pl.BlockSpec((f, d), lambda i: (0, 0)),
                  pl.BlockSpec(memory_space=pl.ANY)],
        out_specs=row,
        out_shape=jax.ShapeDtypeStruct((s, d), F32),
        scratch_shapes=[pltpu.VMEM((TOP_K, tc, d), F32), pltpu.SemaphoreType.DMA(())],
        compiler_params=_cparams(("arbitrary",)),
        name="combine",
    )(pos3, wts_t, hf, x1, gt, sg, su, sd, ys)


def _tile(n, want):
    t = min(n, want)
    assert n % t == 0, (n, t)
    return t


def _layer(l, x, c_col, pos_row, p):
    s, d = x.shape
    lambda_init = 0.8 - 0.6 * math.exp(-0.3 * l)
    gqk, gv = GLA_HEADS * GLA_DK, GLA_HEADS * GLA_DV
    dqk, dvw = DIFF_HEADS * 2 * DIFF_DH, DIFF_HEADS * DIFF_DV
    lowrank = p["gla_w_a2"].shape[0]

    mod = _ada(c_col, p["w_ada"], p["b_ada"][None, :])
    sh_a, sc_a, gt_a, sh_f, sc_f, gt_f = [mod[:, j * d:(j + 1) * d] for j in range(6)]

    w_in = p["w_in"]
    o = 0
    cols = {}
    for name, wdt in (("gq", gqk), ("gk", gqk), ("gv", gv), ("ga", lowrank), ("gg", gv),
                      ("dq", dqk), ("dk", dqk), ("dv", dvw), ("mg", d), ("md", d)):
        cols[name] = w_in[:, o:o + wdt]
        o += wdt
    row_names = ("gv", "gg", "dv", "mg", "md", "gq")
    w_row = jnp.concatenate([cols[n] for n in row_names], axis=1).astype(BF16)
    col_of, o = {}, 0
    for n in row_names:
        col_of[n] = o
        o += cols[n].shape[1]
    t_names = ("dq", "dk", "gk")
    w_t = jnp.concatenate([cols[n] for n in t_names], axis=1).T.astype(BF16)
    row_of, o = {}, 0
    for n in t_names:
        row_of[n] = o
        o += cols[n].shape[1]
    w_ga = jnp.pad(cols["ga"], ((0, 0), (0, 128 - lowrank))).astype(BF16)

    g1 = p["norm1_g"][None, :]
    tm = _tile(s, 1024)
    proj, ga = _inproj(x, g1, sc_a, sh_a, w_row, w_ga, tm, 512)
    projt = _inproj_t(x, g1, sc_a, sh_a, w_t, tm, 512)

    wa2t = jnp.pad(p["gla_w_a2"].T, ((0, 0), (0, 128 - lowrank)))
    o_gla = _gla(proj, projt, ga, wa2t, p["gla_b_a"][:, None], p["gla_onorm_g"][None, :],
                 _tile(s, 512), col_of["gq"], col_of["gv"], col_of["gg"], row_of["gk"])

    invf = ROPE_THETA ** (-jnp.arange(0, ROT_DIM, 2, dtype=F32) / ROT_DIM)
    qa, qb, kt = _qkprep(projt, pos_row, invf[:, None], p["diff_qnorm_g"][:, None],
                         p["diff_knorm_g"][:, None], _tile(s, 512), row_of["dq"], row_of["dk"])
    tq = _tile(s, 512)
    o_diff = _diffattn(qa, qb, kt, proj, p["diff_lq1"][None, :], p["diff_lk1"][None, :],
                       p["diff_lq2"][None, :], p["diff_lk2"][None, :],
                       p["diff_subln_g"][None, :], lambda_init, tq, tq, col_of["dv"])

    x1, hf = _mergeout(o_gla, o_diff, proj, x, p["w_branch_gla"].astype(BF16),
                       p["w_branch_diff"].astype(BF16), p["w_out"].astype(BF16), gt_a,
                       p["norm2_g"][None, :], sc_f, sh_f, _tile(s, 512),
                       col_of["mg"], col_of["md"])

    e = p["w_router"].shape[1]
    idx, wts, rnk, cnt = _route(hf, p["w_router"].T, p["router_bias"][:, None], _tile(s, 512))

    counts = cnt[:, 0]
    pcounts = ((counts + MOE_ROWS - 1) // MOE_ROWS) * MOE_ROWS
    pend = jnp.cumsum(pcounts)
    pstart = pend - pcounts
    pos = pstart[idx] + rnk
    n_items = (s * TOP_K) // MOE_ROWS + e
    n_valid = (pend[-1] // MOE_ROWS).astype(jnp.int32)
    item_b = jnp.minimum(jnp.arange(n_items, dtype=jnp.int32), n_valid - 1)
    item_e = jnp.minimum(jnp.searchsorted(pend, item_b * MOE_ROWS, side="right"),
                         e - 1).astype(jnp.int32)

    td = _tile(s, 256)
    pos_d = pos.reshape(TOP_K, s // td, td).transpose(1, 0, 2)
    xs = _dispatch(pos_d, hf, jnp.zeros((n_items * MOE_ROWS, d), F32), td)
    ys = _moe(item_e, item_b, n_valid[None], xs, p["w_exp_gate"], p["w_exp_up"], p["w_exp_down"])
    tc = _tile(s, 128)
    pos_c = pos.reshape(TOP_K, s // tc, tc).transpose(1, 0, 2)
    return _combine(pos_c, wts.T, hf, x1, gt_f, p["w_sh_gate"].astype(BF16),
                    p["w_sh_up"].astype(BF16), p["w_sh_down"].astype(BF16), ys, tc)


_LAYER_PARAMS = ("w_ada", "b_ada", "norm1_g", "w_in", "gla_w_a2", "gla_b_a", "gla_onorm_g",
                 "diff_qnorm_g", "diff_knorm_g", "diff_lq1", "diff_lk1", "diff_lq2", "diff_lk2",
                 "diff_subln_g", "w_branch_gla", "w_branch_diff", "w_out", "norm2_g", "w_router",
                 "router_bias", "w_exp_gate", "w_exp_up", "w_exp_down", "w_sh_gate", "w_sh_up",
                 "w_sh_down")


def kernel(x, c, positions, w_ada, b_ada, norm1_g, w_in, gla_w_a2, gla_b_a, gla_onorm_g, diff_qnorm_g, diff_knorm_g, diff_lq1, diff_lk1, diff_lq2, diff_lk2, diff_subln_g, w_branch_gla, w_branch_diff, w_out, norm2_g, w_router, router_bias, w_exp_gate, w_exp_up, w_exp_down, w_sh_gate, w_sh_up, w_sh_down):
    stacked = dict(zip(_LAYER_PARAMS, (
        w_ada, b_ada, norm1_g, w_in, gla_w_a2, gla_b_a, gla_onorm_g, diff_qnorm_g, diff_knorm_g,
        diff_lq1, diff_lk1, diff_lq2, diff_lk2, diff_subln_g, w_branch_gla, w_branch_diff, w_out,
        norm2_g, w_router, router_bias, w_exp_gate, w_exp_up, w_exp_down, w_sh_gate, w_sh_up,
        w_sh_down)))
    b, s, d = x.shape
    assert b == 1, "single-sequence kernel"
    xl = x[0]
    c_col = c[0][:, None]
    pos_row = positions.astype(jnp.int32)
    for l in range(w_ada.shape[0]):
        xl = _layer(l, xl, c_col, pos_row, {k: v[l] for k, v in stacked.items()})
    return xl[None]
```

```python
import functools
import math

import jax
import jax.numpy as jnp
from jax import lax
from jax.experimental import pallas as pl
from jax.experimental.pallas import tpu as pltpu

CHUNK = 64
EPS = 1e-6
GLA_HEADS = 4
GLA_DK = 128
GLA_DV = 256
GLA_TAU = 16.0
DIFF_HEADS = 8
DIFF_DH = 64
DIFF_DV = 2 * DIFF_DH
ROPE_THETA = 500000.0
ROT_DIM = DIFF_DH // 4
N_GROUPS = 8
TOPK_GROUPS = 4
TOP_K = 8
ROUTED_SCALE = 2.5

MOE_ROWS = 128
VMEM_LIMIT = 56 * 1024 * 1024
NEG_BIG = -1e30
LOG2E = 1.4426950408889634
HIGHEST = lax.Precision.HIGHEST
F32 = jnp.float32
BF16 = jnp.bfloat16


def _cparams(sem):
    return pltpu.CompilerParams(dimension_semantics=sem, vmem_limit_bytes=VMEM_LIMIT)


def _nt_dot(a, b, precision=None):
    return lax.dot_general(a, b, (((1,), (1,)), ((), ())), precision=precision,
                           preferred_element_type=F32)


def _rms_mod(x, g, sc, sh):
    xn = x * lax.rsqrt(jnp.mean(x * x, axis=-1, keepdims=True) + EPS)
    return (xn * g) * (1.0 + sc) + sh


def _ada_kernel(c_ref, w_ref, b_ref, o_ref):
    c = c_ref[...]
    ca = c * jax.nn.sigmoid(c)
    o_ref[...] = jnp.sum(ca * w_ref[...], axis=0, keepdims=True) + b_ref[...]


def _ada(c_col, w, b):
    d, n = w.shape
    tn = min(1024, n)
    return pl.pallas_call(
        _ada_kernel,
        grid=(n // tn,),
        in_specs=[pl.BlockSpec((d, 1), lambda j: (0, 0)),
                  pl.BlockSpec((d, tn), lambda j: (0, j)),
                  pl.BlockSpec((1, tn), lambda j: (0, j))],
        out_specs=pl.BlockSpec((1, tn), lambda j: (0, j)),
        out_shape=jax.ShapeDtypeStruct((1, n), F32),
        compiler_params=_cparams(("arbitrary",)),
        name="ada",
    )(c_col, w, b)


def _inproj_kernel(x_ref, g_ref, sc_ref, sh_ref, w_ref, wga_ref, o_ref, ga_ref, h_scr):
    @pl.when(pl.program_id(1) == 0)
    def _():
        h = _rms_mod(x_ref[...], g_ref[...], sc_ref[...], sh_ref[...]).astype(BF16)
        h_scr[...] = h
        ga_ref[...] = jnp.dot(h, wga_ref[...], preferred_element_type=F32)

    o_ref[...] = jnp.dot(h_scr[...], w_ref[...], preferred_element_type=F32).astype(BF16)


def _inproj(x, g, sc, sh, w, wga, tm, tn):
    s, d = x.shape
    n = w.shape[1]
    vec = pl.BlockSpec((1, d), lambda i, j: (0, 0))
    return pl.pallas_call(
        _inproj_kernel,
        grid=(s // tm, n // tn),
        in_specs=[pl.BlockSpec((tm, d), lambda i, j: (i, 0)), vec, vec, vec,
                  pl.BlockSpec((d, tn), lambda i, j: (0, j)),
                  pl.BlockSpec((d, 128), lambda i, j: (0, 0))],
        out_specs=[pl.BlockSpec((tm, tn), lambda i, j: (i, j)),
                   pl.BlockSpec((tm, 128), lambda i, j: (i, 0))],
        out_shape=[jax.ShapeDtypeStruct((s, n), BF16), jax.ShapeDtypeStruct((s, 128), F32)],
        scratch_shapes=[pltpu.VMEM((tm, d), BF16)],
        compiler_params=_cparams(("arbitrary", "arbitrary")),
        name="inproj",
    )(x, g, sc, sh, w, wga)


def _inproj_t_kernel(x_ref, g_ref, sc_ref, sh_ref, wt_ref, o_ref, h_scr):
    @pl.when(pl.program_id(1) == 0)
    def _():
        h_scr[...] = _rms_mod(x_ref[...], g_ref[...], sc_ref[...], sh_ref[...]).astype(BF16)

    o_ref[...] = _nt_dot(wt_ref[...], h_scr[...]).astype(BF16)


def _inproj_t(x, g, sc, sh, wt, tm, tn):
    s, d = x.shape
    n = wt.shape[0]
    vec = pl.BlockSpec((1, d), lambda i, j: (0, 0))
    return pl.pallas_call(
        _inproj_t_kernel,
        grid=(s // tm, n // tn),
        in_specs=[pl.BlockSpec((tm, d), lambda i, j: (i, 0)), vec, vec, vec,
                  pl.BlockSpec((tn, d), lambda i, j: (j, 0))],
        out_specs=pl.BlockSpec((tn, tm), lambda i, j: (j, i)),
        out_shape=jax.ShapeDtypeStruct((n, s), BF16),
        scratch_shapes=[pltpu.VMEM((tm, d), BF16)],
        compiler_params=_cparams(("arbitrary", "arbitrary")),
        name="inproj_t",
    )(x, g, sc, sh, wt)


def _gla_kernel(q_ref, kt_ref, v_ref, gg_ref, ga_ref, wa2t_ref, ba_ref, on_ref, o_ref,
                state_ref, o_scr):
    tt = q_ref.shape[0]
    nchunk = tt // CHUNK

    @pl.when(pl.program_id(0) == 0)
    def _():
        state_ref[...] = jnp.zeros_like(state_ref)

    zt = _nt_dot(wa2t_ref[...], ga_ref[...], precision=HIGHEST) + ba_ref[...]
    lat = (jnp.minimum(zt, 0.0) - jnp.log1p(jnp.exp(-jnp.abs(zt)))) * (1.0 / GLA_TAU)
    row = lax.broadcasted_iota(jnp.int32, (tt, tt), 0)
    col = lax.broadcasted_iota(jnp.int32, (tt, tt), 1)
    same = (row // CHUNK) == (col // CHUNK)
    incl = jnp.where(same & (row <= col), 1.0, 0.0).astype(F32)
    full = jnp.where(same, 1.0, 0.0).astype(F32)
    cumt = jnp.dot(lat, incl, precision=HIGHEST, preferred_element_type=F32)
    tott = jnp.dot(lat, full, precision=HIGHEST, preferred_element_type=F32)
    kdt = kt_ref[...].astype(F32) * jnp.exp(tott - cumt)
    dec = jnp.exp(tott)

    lane = lax.broadcasted_iota(jnp.int32, (GLA_DK, 2 * CHUNK), 1)
    for c in range(nchunk):
        pair = (c // 2) * 2 * CHUNK
        if nchunk > 1:
            keep = (lane // CHUNK) == (c % 2)
        for h in range(GLA_HEADS):
            rows = slice(h * GLA_DK, (h + 1) * GLA_DK)
            vcols = slice(h * GLA_DV, (h + 1) * GLA_DV)
            if nchunk > 1:
                a = jnp.where(keep, kdt[rows, pair:pair + 2 * CHUNK], 0.0).astype(BF16)
                vp = v_ref[pair:pair + 2 * CHUNK, vcols]
            else:
                a = kdt[rows, :].astype(BF16)
                vp = v_ref[:, vcols]
            upd = jnp.dot(a, vp, preferred_element_type=F32)
            dcol = dec[rows, c * CHUNK:c * CHUNK + 1]
            st = state_ref[h] * dcol + upd
            state_ref[h] = st
            qc = q_ref[c * CHUNK:(c + 1) * CHUNK, rows]
            o_scr[c * CHUNK:(c + 1) * CHUNK, vcols] = jnp.dot(
                qc, st.astype(BF16), preferred_element_type=F32)

    for h in range(GLA_HEADS):
        vcols = slice(h * GLA_DV, (h + 1) * GLA_DV)
        o = o_scr[:, vcols] * (GLA_DK ** -0.5)
        o = o * lax.rsqrt(jnp.mean(o * o, axis=-1, keepdims=True) + EPS) * on_ref[...]
        g = gg_ref[:, vcols].astype(F32)
        o_ref[:, vcols] = (o * (g * jax.nn.sigmoid(g))).astype(BF16)


def _gla(proj, projt, ga, wa2t, ba_col, on_g, tt, col_q, col_v, col_g, row_k):
    s = proj.shape[0]
    qk = GLA_HEADS * GLA_DK
    vw = GLA_HEADS * GLA_DV
    return pl.pallas_call(
        _gla_kernel,
        grid=(s // tt,),
        in_specs=[pl.BlockSpec((tt, qk), lambda i: (i, col_q // qk)),
                  pl.BlockSpec((qk, tt), lambda i: (row_k // qk, i)),
                  pl.BlockSpec((tt, vw), lambda i: (i, col_v // vw)),
                  pl.BlockSpec((tt, vw), lambda i: (i, col_g // vw)),
                  pl.BlockSpec((tt, 128), lambda i: (i, 0)),
                  pl.BlockSpec((qk, 128), lambda i: (0, 0)),
                  pl.BlockSpec((qk, 1), lambda i: (0, 0)),
                  pl.BlockSpec((1, GLA_DV), lambda i: (0, 0))],
        out_specs=pl.BlockSpec((tt, vw), lambda i: (i, 0)),
        out_shape=jax.ShapeDtypeStruct((s, vw), BF16),
        scratch_shapes=[pltpu.VMEM((GLA_HEADS, GLA_DK, GLA_DV), F32),
                        pltpu.VMEM((tt, vw), F32)],
        compiler_params=_cparams(("arbitrary",)),
        name="gla",
    )(proj, projt, proj, proj, ga, wa2t, ba_col, on_g)


def _qknorm_rope_t(xt, g_col, cos, sin):
    n, tm = xt.shape
    x3 = xt.reshape(n // DIFF_DH, DIFF_DH, tm)
    r = lax.rsqrt(jnp.mean(x3 * x3, axis=1, keepdims=True) + EPS)
    y = x3 * r * g_col[None]
    half = ROT_DIM // 2
    y1, y2, rest = y[:, :half], y[:, half:ROT_DIM], y[:, ROT_DIM:]
    o1 = y1 * cos[None] - y2 * sin[None]
    o2 = y2 * cos[None] + y1 * sin[None]
    return jnp.concatenate([o1, o2, rest], axis=1)


def _qkprep_kernel(qt_ref, kt_ref, pos_ref, invf_ref, qg_ref, kg_ref, qa_ref, qb_ref, ko_ref):
    tm = qt_ref.shape[1]
    ang = pos_ref[...].astype(F32) * invf_ref[...]
    cos, sin = jnp.cos(ang), jnp.sin(ang)
    k3 = _qknorm_rope_t(kt_ref[...].astype(F32), kg_ref[...], cos, sin)
    ko_ref[...] = k3.reshape(-1, tm).T.astype(BF16)
    q3 = _qknorm_rope_t(qt_ref[...].astype(F32), qg_ref[...], cos, sin) * (
        DIFF_DH ** -0.5 * LOG2E)
    seg = lax.broadcasted_iota(jnp.int32, q3.shape, 0)
    qa_ref[...] = jnp.where(seg % 2 == 0, q3, 0.0).reshape(-1, tm).astype(BF16)
    qb_ref[...] = jnp.where(seg % 2 == 1, q3, 0.0).reshape(-1, tm).astype(BF16)


def _qkprep(projt, pos_row, invf_col, qg_col, kg_col, tm, row_q, row_k):
    s = projt.shape[1]
    n = DIFF_HEADS * 2 * DIFF_DH
    col = pl.BlockSpec((DIFF_DH, 1), lambda i: (0, 0))
    return pl.pallas_call(
        _qkprep_kernel,
        grid=(s // tm,),
        in_specs=[pl.BlockSpec((n, tm), lambda i: (row_q // n, i)),
                  pl.BlockSpec((n, tm), lambda i: (row_k // n, i)),
                  pl.BlockSpec((1, tm), lambda i: (0, i)),
                  pl.BlockSpec((ROT_DIM // 2, 1), lambda i: (0, 0)), col, col],
        out_specs=[pl.BlockSpec((n, tm), lambda i: (0, i)),
                   pl.BlockSpec((n, tm), lambda i: (0, i)),
                   pl.BlockSpec((tm, n), lambda i: (i, 0))],
        out_shape=[jax.ShapeDtypeStruct((n, s), BF16), jax.ShapeDtypeStruct((n, s), BF16),
                   jax.ShapeDtypeStruct((s, n), BF16)],
        compiler_params=_cparams(("arbitrary",)),
        name="qkprep",
    )(projt, projt, pos_row, invf_col, qg_col, kg_col)


ATT_COLS = 256


def _diffattn_kernel(qa_ref, qb_ref, k_ref, vt_ref, lq1_ref, lk1_ref, lq2_ref, lk2_ref, sg_ref,
                     o_ref, *scr, lambda_init, tk, cols):
    tq = qa_ref.shape[1]
    nblk = 2 * tq // cols
    q_scr, m_scr, l_scr, acc_scr = (scr[b * nblk:(b + 1) * nblk] for b in range(4))
    i = pl.program_id(1)
    for c in range(nblk):
        src = qa_ref if c * cols < tq else qb_ref
        off = (c * cols) % tq
        q_scr[c][...] = src[:, off:off + cols]
        m_scr[c][...] = jnp.full_like(m_scr[c], NEG_BIG)
        l_scr[c][...] = jnp.zeros_like(l_scr[c])
        acc_scr[c][...] = jnp.zeros_like(acc_scr[c])

    def scores(j, c):
        start = pl.multiple_of(j * tk, tk)
        return jnp.dot(k_ref[pl.ds(start, tk), :], q_scr[c][...], preferred_element_type=F32)

    def steps(tiles, masked):
        items = [(j, c) for j in tiles for c in range(nblk)]
        nxt = scores(*items[0])
        for n, (j, c) in enumerate(items):
            s = nxt
            if n + 1 < len(items):
                nxt = scores(*items[n + 1])
            start = pl.multiple_of(j * tk, tk)
            if masked:
                krow = lax.broadcasted_iota(jnp.int32, (tk, cols), 0)
                qcol = lax.broadcasted_iota(jnp.int32, (tk, cols), 1)
                qpos = i * tq + (c * cols) % tq + qcol
                s = jnp.where((start + krow) // CHUNK <= qpos // CHUNK, s, NEG_BIG)
            m_prev = m_scr[c][...]
            m_new = jnp.maximum(m_prev, jnp.max(s, axis=0, keepdims=True))
            alpha = jnp.exp2(m_prev - m_new)
            p = jnp.exp2(s - m_new)
            l_scr[c][...] = alpha * l_scr[c][...] + jnp.sum(p, axis=0, keepdims=True)
            acc_scr[c][...] = alpha * acc_scr[c][...] + jnp.dot(
                vt_ref[:, pl.ds(start, tk)], p.astype(BF16), preferred_element_type=F32)
            m_scr[c][...] = m_new

    n_full = (i * tq) // tk
    lax.fori_loop(0, n_full // 2, lambda t, c: (steps((2 * t, 2 * t + 1), False), c)[1], 0)

    @pl.when(n_full % 2 == 1)
    def _():
        steps((n_full - 1,), False)

    n_all = ((i + 1) * tq + tk - 1) // tk
    lax.fori_loop(n_full, n_all, lambda j, c: (steps((j,), True), c)[1], 0)

    o = jnp.concatenate([acc_scr[c][...] / l_scr[c][...] for c in range(nblk)], axis=1)
    lam = (jnp.exp(jnp.sum(lq1_ref[...] * lk1_ref[...]))
           - jnp.exp(jnp.sum(lq2_ref[...] * lk2_ref[...])) + lambda_init)
    o = o[:, :tq] - lam * o[:, tq:]
    o = o * lax.rsqrt(jnp.mean(o * o, axis=0, keepdims=True) + EPS) * sg_ref[...]
    o_ref[...] = (o * (1.0 - lambda_init)).T.astype(BF16)


def _diffattn(qat, qbt, kr, projt, lq1, lk1, lq2, lk2, sg_col, lambda_init, tq, tk, row_v):
    s = kr.shape[0]
    hd = 2 * DIFF_DH
    vec = pl.BlockSpec((1, DIFF_DH), lambda h, i: (0, 0))
    cols = min(ATT_COLS, tq)
    nblk = 2 * tq // cols
    kern = functools.partial(_diffattn_kernel, lambda_init=lambda_init, tk=tk, cols=cols)
    scratch = ([pltpu.VMEM((hd, cols), BF16)] * nblk + [pltpu.VMEM((1, cols), F32)] * (2 * nblk)
               + [pltpu.VMEM((DIFF_DV, cols), F32)] * nblk)
    return pl.pallas_call(
        kern,
        grid=(DIFF_HEADS, s // tq),
        in_specs=[pl.BlockSpec((hd, tq), lambda h, i: (h, i)),
                  pl.BlockSpec((hd, tq), lambda h, i: (h, i)),
                  pl.BlockSpec((s, hd), lambda h, i: (0, h)),
                  pl.BlockSpec((DIFF_DV, s), lambda h, i: (row_v // DIFF_DV + h, 0)),
                  vec, vec, vec, vec,
                  pl.BlockSpec((DIFF_DV, 1), lambda h, i: (0, 0))],
        out_specs=pl.BlockSpec((tq, DIFF_DV), lambda h, i: (i, h)),
        out_shape=jax.ShapeDtypeStruct((s, DIFF_HEADS * DIFF_DV), BF16),
        scratch_shapes=scratch,
        compiler_params=_cparams(("arbitrary", "arbitrary")),
        name="diffattn",
    )(qat, qbt, kr, projt, lq1, lk1, lq2, lk2, sg_col)


def _mergeout_kernel(og_ref, od_ref, mg_ref, md_ref, x_ref, wbg_ref, wbd_ref, wo_ref, gt_ref,
                     g2_ref, sc_ref, sh_ref, x1_ref, hf_ref):
    bg = jnp.dot(og_ref[...], wbg_ref[...], preferred_element_type=F32)
    bd = jnp.dot(od_ref[...], wbd_ref[...], preferred_element_type=F32)
    merged = (jax.nn.sigmoid(mg_ref[...].astype(F32)) * bg
              + jax.nn.sigmoid(md_ref[...].astype(F32)) * bd)
    x1 = x_ref[...] + gt_ref[...] * jnp.dot(merged.astype(BF16), wo_ref[...],
                                             preferred_element_type=F32)
    x1_ref[...] = x1
    hf_ref[...] = _rms_mod(x1, g2_ref[...], sc_ref[...], sh_ref[...])


def _mergeout(og, od, proj, x, wbg, wbd, wo, gt, g2, sc, sh, tm, col_mg, col_md):
    s, d = x.shape
    vec = pl.BlockSpec((1, d), lambda i: (0, 0))
    wspec = pl.BlockSpec((d, d), lambda i: (0, 0))
    row = pl.BlockSpec((tm, d), lambda i: (i, 0))
    return pl.pallas_call(
        _mergeout_kernel,
        grid=(s // tm,),
        in_specs=[row, row,
                  pl.BlockSpec((tm, d), lambda i: (i, col_mg // d)),
                  pl.BlockSpec((tm, d), lambda i: (i, col_md // d)),
                  row, wspec, wspec, wspec, vec, vec, vec, vec],
        out_specs=[row, row],
        out_shape=[jax.ShapeDtypeStruct((s, d), F32), jax.ShapeDtypeStruct((s, d), F32)],
        compiler_params=_cparams(("arbitrary",)),
        name="mergeout",
    )(og, od, proj, proj, x, wbg, wbd, wo, gt, g2, sc, sh)


def _route_kernel(hf_ref, wrt_ref, bias_ref, idx_ref, wts_ref, rnk_ref, cnt_ref, run_scr):
    tr = hf_ref.shape[0]
    e = wrt_ref.shape[0]
    gsz = e // N_GROUPS

    @pl.when(pl.program_id(0) == 0)
    def _():
        run_scr[...] = jnp.zeros_like(run_scr)

    logits = _nt_dot(wrt_ref[...], hf_ref[...], precision=HIGHEST)
    scores = jax.nn.sigmoid(logits)
    biased = scores + bias_ref[...]
    g3 = biased.reshape(N_GROUPS, gsz, tr)
    m1 = jnp.max(g3, axis=1, keepdims=True)
    n_top = jnp.sum(jnp.where(g3 == m1, 1.0, 0.0), axis=1, keepdims=True)
    m2 = jnp.max(jnp.where(g3 < m1, g3, -jnp.inf), axis=1, keepdims=True)
    gs = (m1 + jnp.where(n_top >= 2.0, m1, m2)).reshape(N_GROUPS, tr)
    gi = lax.broadcasted_iota(jnp.int32, (N_GROUPS, tr), 0)
    beaten = jnp.zeros((N_GROUPS, tr), F32)
    for g in range(N_GROUPS):
        other = gs[g:g + 1, :]
        beaten = beaten + jnp.where((other > gs) | ((other == gs) & (g < gi)), 1.0, 0.0)
    gsel = (beaten < float(TOPK_GROUPS)).reshape(N_GROUPS, 1, tr)
    masked = jnp.where(gsel, g3, -jnp.inf).reshape(e, tr)

    ids = lax.broadcasted_iota(jnp.int32, (e, tr), 0)
    chosen = jnp.zeros((e, tr), F32)
    sel_idx, sel_score = [], []
    for _ in range(TOP_K):
        mx = jnp.max(masked, axis=0, keepdims=True)
        ix = jnp.min(jnp.where(masked == mx, ids, e), axis=0, keepdims=True)
        hit = ids == ix
        sel_idx.append(ix)
        sel_score.append(jnp.sum(jnp.where(hit, scores, 0.0), axis=0, keepdims=True))
        chosen = jnp.where(hit, 1.0, chosen)
        masked = jnp.where(hit, -jnp.inf, masked)
    idx = jnp.concatenate(sel_idx, axis=0)
    sc = jnp.concatenate(sel_score, axis=0)
    idx_ref[...] = idx
    wts_ref[...] = sc / jnp.sum(sc, axis=0, keepdims=True) * ROUTED_SCALE

    row = lax.broadcasted_iota(jnp.int32, (tr, tr), 0)
    col = lax.broadcasted_iota(jnp.int32, (tr, tr), 1)
    before = jnp.where(row < col, 1.0, 0.0).astype(BF16)
    prior = jnp.dot(chosen.astype(BF16), before, preferred_element_type=F32) + run_scr[:, 0:1]
    rnk_ref[...] = jnp.concatenate(
        [jnp.sum(jnp.where(ids == sel_idx[k], prior, 0.0), axis=0, keepdims=True)
         for k in range(TOP_K)], axis=0).astype(jnp.int32)
    run_scr[...] = run_scr[...] + jnp.sum(chosen, axis=1, keepdims=True)
    cnt_ref[...] = run_scr[...].astype(jnp.int32)


def _route(hf, wrt, bias_col, tr):
    s, d = hf.shape
    e = wrt.shape[0]
    tok = pl.BlockSpec((TOP_K, tr), lambda i: (0, i))
    return pl.pallas_call(
        _route_kernel,
        grid=(s // tr,),
        in_specs=[pl.BlockSpec((tr, d), lambda i: (i, 0)),
                  pl.BlockSpec((e, d), lambda i: (0, 0)),
                  pl.BlockSpec((e, 1), lambda i: (0, 0))],
        out_specs=[tok, tok, tok, pl.BlockSpec((e, 128), lambda i: (0, 0))],
        out_shape=[jax.ShapeDtypeStruct((TOP_K, s), jnp.int32),
                   jax.ShapeDtypeStruct((TOP_K, s), F32),
                   jax.ShapeDtypeStruct((TOP_K, s), jnp.int32),
                   jax.ShapeDtypeStruct((e, 128), jnp.int32)],
        scratch_shapes=[pltpu.VMEM((e, 128), F32)],
        compiler_params=_cparams(("arbitrary",)),
        name="route",
    )(hf, wrt, bias_col)


def _positions_kernel(idx_ref, rnk_ref, pstart_ref, pos_ref):
    e = pstart_ref.shape[0]
    ts = idx_ref.shape[1]
    ids = lax.broadcasted_iota(jnp.int32, (e, ts), 0)
    idx = idx_ref[...]
    pos_ref[...] = rnk_ref[...] + jnp.concatenate(
        [jnp.sum(jnp.where(ids == idx[k:k + 1, :], pstart_ref[...], 0), axis=0, keepdims=True)
         for k in range(TOP_K)], axis=0)


def _positions(idx, rnk, pstart_col, ts):
    s = idx.shape[1]
    e = pstart_col.shape[0]
    tok = pl.BlockSpec((TOP_K, ts), lambda i: (0, i))
    return pl.pallas_call(
        _positions_kernel,
        grid=(s // ts,),
        in_specs=[tok, tok, pl.BlockSpec((e, 1), lambda i: (0, 0))],
        out_specs=tok,
        out_shape=jax.ShapeDtypeStruct((TOP_K, s), jnp.int32),
        compiler_params=_cparams(("arbitrary",)),
        name="positions",
    )(idx, rnk, pstart_col)


def _row_copy(src_ref, src_row, dst_ref, dst_row, sem):
    return pltpu.make_async_copy(src_ref.at[pl.ds(src_row, 1), :],
                                 dst_ref.at[pl.ds(dst_row, 1), :], sem)


def _dispatch_kernel(pos_ref, hf_ref, xs_in_ref, xs_ref, sem):
    del xs_in_ref
    td = hf_ref.shape[0]

    def start(t, c):
        for k in range(TOP_K):
            _row_copy(hf_ref, t, xs_ref, pos_ref[k, t], sem).start()
        return c

    def wait(t, c):
        for k in range(TOP_K):
            _row_copy(hf_ref, t, xs_ref, pos_ref[k, t], sem).wait()
        return c

    lax.fori_loop(0, td, start, 0)
    lax.fori_loop(0, td, wait, 0)


def _dispatch(pos3, hf, xs_zero, td):
    s, d = hf.shape
    return pl.pallas_call(
        _dispatch_kernel,
        grid=(s // td,),
        in_specs=[pl.BlockSpec((None, TOP_K, td), lambda i: (i, 0, 0), memory_space=pltpu.SMEM),
                  pl.BlockSpec((td, d), lambda i: (i, 0)),
                  pl.BlockSpec(memory_space=pl.ANY)],
        out_specs=pl.BlockSpec(memory_space=pl.ANY),
        out_shape=jax.ShapeDtypeStruct(xs_zero.shape, xs_zero.dtype),
        scratch_shapes=[pltpu.SemaphoreType.DMA(())],
        input_output_aliases={2: 0},
        compiler_params=_cparams(("arbitrary",)),
        name="dispatch",
    )(pos3, hf, xs_zero)


def _moe_kernel(ie_ref, ib_ref, nv_ref, xs_ref, wg_ref, wu_ref, wd_ref, ys_ref):
    del ie_ref, ib_ref

    @pl.when(pl.program_id(0) < nv_ref[0])
    def _():
        xb = xs_ref[...].astype(BF16)
        g = jnp.dot(xb, wg_ref[...].astype(BF16), preferred_element_type=F32)
        u = jnp.dot(xb, wu_ref[...].astype(BF16), preferred_element_type=F32)
        h = (g * jax.nn.sigmoid(g)) * u
        ys_ref[...] = jnp.dot(h.astype(BF16), wd_ref[...].astype(BF16),
                              preferred_element_type=F32)


def _moe(item_e, item_b, n_valid, xs, wg, wu, wd):
    m_pad, d = xs.shape
    f = wg.shape[2]
    n_items = item_e.shape[0]
    return pl.pallas_call(
        _moe_kernel,
        grid_spec=pltpu.PrefetchScalarGridSpec(
            num_scalar_prefetch=3,
            grid=(n_items,),
            in_specs=[pl.BlockSpec((MOE_ROWS, d), lambda i, ie, ib, nv: (ib[i], 0)),
                      pl.BlockSpec((None, d, f), lambda i, ie, ib, nv: (ie[i], 0, 0)),
                      pl.BlockSpec((None, d, f), lambda i, ie, ib, nv: (ie[i], 0, 0)),
                      pl.BlockSpec((None, f, d), lambda i, ie, ib, nv: (ie[i], 0, 0))],
            out_specs=pl.BlockSpec((MOE_ROWS, d), lambda i, ie, ib, nv: (ib[i], 0)),
        ),
        out_shape=jax.ShapeDtypeStruct((m_pad, d), F32),
        compiler_params=_cparams(("arbitrary",)),
        name="moe",
    )(item_e, item_b, n_valid, xs, wg, wu, wd)


def _combine_kernel(pos_ref, wt_ref, hf_ref, x1_ref, gt_ref, sg_ref, su_ref, sd_ref, ys_ref,
                    o_ref, gbuf, sem):
    tc = hf_ref.shape[0]

    def copy(t, k):
        return pltpu.make_async_copy(ys_ref.at[pl.ds(pos_ref[k, t], 1), :],
                                     gbuf.at[k, pl.ds(t, 1), :], sem)

    def start(t, c):
        for k in range(TOP_K):
            copy(t, k).start()
        return c

    def wait(t, c):
        for k in range(TOP_K):
            copy(t, k).wait()
        return c

    lax.fori_loop(0, tc, start, 0)
    hb = hf_ref[...].astype(BF16)
    g = jnp.dot(hb, sg_ref[...], preferred_element_type=F32)
    u = jnp.dot(hb, su_ref[...], preferred_element_type=F32)
    y = jnp.dot(((g * jax.nn.sigmoid(g)) * u).astype(BF16), sd_ref[...],
                preferred_element_type=F32)
    lax.fori_loop(0, tc, wait, 0)
    wt = wt_ref[...]
    for k in range(TOP_K):
        y = y + gbuf[k] * wt[:, k:k + 1]
    o_ref[...] = x1_ref[...] + gt_ref[...] * y


def _combine(pos3, wts_t, hf, x1, gt, sg, su, sd, ys, tc):
    s, d = hf.shape
    f = sg.shape[1]
    row = pl.BlockSpec((tc, d), lambda i: (i, 0))
    return pl.pallas_call(
        _combine_kernel,
        grid=(s // tc,),
        in_specs=[pl.BlockSpec((None, TOP_K, tc), lambda i: (i, 0, 0), memory_space=pltpu.SMEM),
                  pl.BlockSpec((tc, TOP_K), lambda i: (i, 0)),
                  row, row,
                  pl.BlockSpec((1, d), lambda i: (0, 0)),
                  pl.BlockSpec((d, f), lambda i: (0, 0)),
                  pl.BlockSpec((d, f), lambda i: (0, 0)),
                  pl.BlockSpec((f, d), lambda i: (0, 0)),
                  pl.BlockSpec(memory_space=pl.ANY)],
        out_specs=row,
        out_shape=jax.ShapeDtypeStruct((s, d), F32),
        scratch_shapes=[pltpu.VMEM((TOP_K, tc, d), F32), pltpu.SemaphoreType.DMA(())],
        compiler_params=_cparams(("arbitrary",)),
        name="combine",
    )(pos3, wts_t, hf, x1, gt, sg, su, sd, ys)


def _tile(n, want):
    t = min(n, want)
    assert n % t == 0, (n, t)
    return t


def _layer(l, x, c_col, pos_row, p):
    s, d = x.shape
    lambda_init = 0.8 - 0.6 * math.exp(-0.3 * l)
    gqk, gv = GLA_HEADS * GLA_DK, GLA_HEADS * GLA_DV
    dqk, dvw = DIFF_HEADS * 2 * DIFF_DH, DIFF_HEADS * DIFF_DV
    lowrank = p["gla_w_a2"].shape[0]

    mod = _ada(c_col, p["w_ada"], p["b_ada"][None, :])
    sh_a, sc_a, gt_a, sh_f, sc_f, gt_f = [mod[:, j * d:(j + 1) * d] for j in range(6)]

    w_in = p["w_in"]
    o = 0
    cols = {}
    for name, wdt in (("gq", gqk), ("gk", gqk), ("gv", gv), ("ga", lowrank), ("gg", gv),
                      ("dq", dqk), ("dk", dqk), ("dv", dvw), ("mg", d), ("md", d)):
        cols[name] = w_in[:, o:o + wdt]
        o += wdt
    row_names = ("gv", "gg", "mg", "md", "gq")
    w_row = jnp.concatenate([cols[n] for n in row_names], axis=1).astype(BF16)
    col_of, o = {}, 0
    for n in row_names:
        col_of[n] = o
        o += cols[n].shape[1]
    t_names = ("dq", "dk", "dv", "gk")
    w_t = jnp.concatenate([cols[n] for n in t_names], axis=1).T.astype(BF16)
    row_of, o = {}, 0
    for n in t_names:
        row_of[n] = o
        o += cols[n].shape[1]
    w_ga = jnp.pad(cols["ga"], ((0, 0), (0, 128 - lowrank))).astype(BF16)

    g1 = p["norm1_g"][None, :]
    tm = _tile(s, 1024)
    proj, ga = _inproj(x, g1, sc_a, sh_a, w_row, w_ga, tm, 512)
    projt = _inproj_t(x, g1, sc_a, sh_a, w_t, tm, 512)

    wa2t = jnp.pad(p["gla_w_a2"].T, ((0, 0), (0, 128 - lowrank)))
    o_gla = _gla(proj, projt, ga, wa2t, p["gla_b_a"][:, None], p["gla_onorm_g"][None, :],
                 _tile(s, 512), col_of["gq"], col_of["gv"], col_of["gg"], row_of["gk"])

    invf = ROPE_THETA ** (-jnp.arange(0, ROT_DIM, 2, dtype=F32) / ROT_DIM)
    qat, qbt, kr = _qkprep(projt, pos_row, invf[:, None], p["diff_qnorm_g"][:, None],
                           p["diff_knorm_g"][:, None], _tile(s, 512), row_of["dq"], row_of["dk"])
    tq = _tile(s, 512)
    o_diff = _diffattn(qat, qbt, kr, projt, p["diff_lq1"][None, :], p["diff_lk1"][None, :],
                       p["diff_lq2"][None, :], p["diff_lk2"][None, :],
                       p["diff_subln_g"][:, None], lambda_init, tq, tq, row_of["dv"])

    x1, hf = _mergeout(o_gla, o_diff, proj, x, p["w_branch_gla"].astype(BF16),
                       p["w_branch_diff"].astype(BF16), p["w_out"].astype(BF16), gt_a,
                       p["norm2_g"][None, :], sc_f, sh_f, _tile(s, 512),
                       col_of["mg"], col_of["md"])

    e = p["w_router"].shape[1]
    idx, wts, rnk, cnt = _route(hf, p["w_router"].T, p["router_bias"][:, None], _tile(s, 512))

    counts = cnt[:, 0]
    pcounts = ((counts + MOE_ROWS - 1) // MOE_ROWS) * MOE_ROWS
    pend = jnp.cumsum(pcounts)
    pstart = pend - pcounts
    pos = _positions(idx, rnk, pstart[:, None], _tile(s, 512))
    n_items = (s * TOP_K) // MOE_ROWS + e
    n_valid = (pend[-1] // MOE_ROWS).astype(jnp.int32)
    item_b = jnp.minimum(jnp.arange(n_items, dtype=jnp.int32), n_valid - 1)
    item_e = jnp.minimum(jnp.sum(pend[None, :] <= (item_b * MOE_ROWS)[:, None], axis=1),
                         e - 1).astype(jnp.int32)

    td = _tile(s, 256)
    pos_d = pos.reshape(TOP_K, s // td, td).transpose(1, 0, 2)
    xs = _dispatch(pos_d, hf, jnp.zeros((n_items * MOE_ROWS, d), F32), td)
    ys = _moe(item_e, item_b, n_valid[None], xs, p["w_exp_gate"], p["w_exp_up"], p["w_exp_down"])
    tc = _tile(s, 128)
    pos_c = pos.reshape(TOP_K, s // tc, tc).transpose(1, 0, 2)
    return _combine(pos_c, wts.T, hf, x1, gt_f, p["w_sh_gate"].astype(BF16),
                    p["w_sh_up"].astype(BF16), p["w_sh_down"].astype(BF16), ys, tc)


_LAYER_PARAMS = ("w_ada", "b_ada", "norm1_g", "w_in", "gla_w_a2", "gla_b_a", "gla_onorm_g",
                 "diff_qnorm_g", "diff_knorm_g", "diff_lq1", "diff_lk1", "diff_lq2", "diff_lk2",
                 "diff_subln_g", "w_branch_gla", "w_branch_diff", "w_out", "norm2_g", "w_router",
                 "router_bias", "w_exp_gate", "w_exp_up", "w_exp_down", "w_sh_gate", "w_sh_up",
                 "w_sh_down")


def kernel(x, c, positions, w_ada, b_ada, norm1_g, w_in, gla_w_a2, gla_b_a, gla_onorm_g, diff_qnorm_g, diff_knorm_g, diff_lq1, diff_lk1, diff_lq2, diff_lk2, diff_subln_g, w_branch_gla, w_branch_diff, w_out, norm2_g, w_router, router_bias, w_exp_gate, w_exp_up, w_exp_down, w_sh_gate, w_sh_up, w_sh_down):
    stacked = dict(zip(_LAYER_PARAMS, (
        w_ada, b_ada, norm1_g, w_in, gla_w_a2, gla_b_a, gla_onorm_g, diff_qnorm_g, diff_knorm_g,
        diff_lq1, diff_lk1, diff_lq2, diff_lk2, diff_subln_g, w_branch_gla, w_branch_diff, w_out,
        norm2_g, w_router, router_bias, w_exp_gate, w_exp_up, w_exp_down, w_sh_gate, w_sh_up,
        w_sh_down)))
    b, s, d = x.shape
    assert b == 1, "single-sequence kernel"
    xl = x[0]
    c_col = c[0][:, None]
    pos_row = positions.astype(jnp.int32)
    for l in range(w_ada.shape[0]):
        xl = _layer(l, xl, c_col, pos_row, {k: v[l] for k, v in stacked.items()})
    return xl[None]
```

```python
import functools
import math

import jax
import jax.numpy as jnp
from jax import lax
from jax.experimental import pallas as pl
from jax.experimental.pallas import tpu as pltpu

CHUNK = 64
EPS = 1e-6
GLA_HEADS = 4
GLA_DK = 128
GLA_DV = 256
GLA_TAU = 16.0
DIFF_HEADS = 8
DIFF_DH = 64
DIFF_DV = 2 * DIFF_DH
ROPE_THETA = 500000.0
ROT_DIM = DIFF_DH // 4
N_GROUPS = 8
TOPK_GROUPS = 4
TOP_K = 8
ROUTED_SCALE = 2.5

MOE_ROWS = 256
VMEM_LIMIT = 56 * 1024 * 1024
NEG_BIG = -1e30
LOG2E = 1.4426950408889634
HIGHEST = lax.Precision.HIGHEST
F32 = jnp.float32
BF16 = jnp.bfloat16


def _cparams(sem):
    return pltpu.CompilerParams(dimension_semantics=sem, vmem_limit_bytes=VMEM_LIMIT)


def _nt_dot(a, b, precision=None):
    return lax.dot_general(a, b, (((1,), (1,)), ((), ())), precision=precision,
                           preferred_element_type=F32)


def _pack_halves(x):
    n = x.shape[1] // 2
    lo = pltpu.bitcast(x[:, :n].astype(BF16).astype(F32), jnp.uint32) >> 16
    hi = pltpu.bitcast(x[:, n:].astype(BF16).astype(F32), jnp.uint32) & jnp.uint32(0xFFFF0000)
    return lo | hi


def _unpack_halves(w):
    return (pltpu.bitcast(w << 16, F32), pltpu.bitcast(w & jnp.uint32(0xFFFF0000), F32))


def _rms_mod(x, g, sc, sh):
    xn = x * lax.rsqrt(jnp.mean(x * x, axis=-1, keepdims=True) + EPS)
    return (xn * g) * (1.0 + sc) + sh


def _ada_kernel(c_ref, w_ref, b_ref, o_ref):
    c = c_ref[...]
    ca = c * jax.nn.sigmoid(c)
    o_ref[...] = jnp.sum(ca * w_ref[...], axis=0, keepdims=True) + b_ref[...]


def _ada(c_col, w, b):
    d, n = w.shape
    tn = min(1024, n)
    return pl.pallas_call(
        _ada_kernel,
        grid=(n // tn,),
        in_specs=[pl.BlockSpec((d, 1), lambda j: (0, 0)),
                  pl.BlockSpec((d, tn), lambda j: (0, j)),
                  pl.BlockSpec((1, tn), lambda j: (0, j))],
        out_specs=pl.BlockSpec((1, tn), lambda j: (0, j)),
        out_shape=jax.ShapeDtypeStruct((1, n), F32),
        compiler_params=_cparams(("arbitrary",)),
        name="ada",
    )(c_col, w, b)


def _inproj_kernel(x_ref, g_ref, sc_ref, sh_ref, w_ref, wga_ref, o_ref, ga_ref, h_scr):
    @pl.when(pl.program_id(1) == 0)
    def _():
        h = _rms_mod(x_ref[...], g_ref[...], sc_ref[...], sh_ref[...]).astype(BF16)
        h_scr[...] = h
        ga_ref[...] = jnp.dot(h, wga_ref[...], preferred_element_type=F32)

    o_ref[...] = jnp.dot(h_scr[...], w_ref[...], preferred_element_type=F32).astype(BF16)


def _inproj(x, g, sc, sh, w, wga, tm, tn):
    s, d = x.shape
    n = w.shape[1]
    vec = pl.BlockSpec((1, d), lambda i, j: (0, 0))
    return pl.pallas_call(
        _inproj_kernel,
        grid=(s // tm, n // tn),
        in_specs=[pl.BlockSpec((tm, d), lambda i, j: (i, 0)), vec, vec, vec,
                  pl.BlockSpec((d, tn), lambda i, j: (0, j)),
                  pl.BlockSpec((d, 128), lambda i, j: (0, 0))],
        out_specs=[pl.BlockSpec((tm, tn), lambda i, j: (i, j)),
                   pl.BlockSpec((tm, 128), lambda i, j: (i, 0))],
        out_shape=[jax.ShapeDtypeStruct((s, n), BF16), jax.ShapeDtypeStruct((s, 128), F32)],
        scratch_shapes=[pltpu.VMEM((tm, d), BF16)],
        compiler_params=_cparams(("arbitrary", "arbitrary")),
        name="inproj",
    )(x, g, sc, sh, w, wga)


def _inproj_t_kernel(x_ref, g_ref, sc_ref, sh_ref, wt_ref, o_ref, h_scr):
    @pl.when(pl.program_id(1) == 0)
    def _():
        h_scr[...] = _rms_mod(x_ref[...], g_ref[...], sc_ref[...], sh_ref[...]).astype(BF16)

    o_ref[...] = _nt_dot(wt_ref[...], h_scr[...]).astype(BF16)


def _inproj_t(x, g, sc, sh, wt, tm, tn):
    s, d = x.shape
    n = wt.shape[0]
    vec = pl.BlockSpec((1, d), lambda i, j: (0, 0))
    return pl.pallas_call(
        _inproj_t_kernel,
        grid=(s // tm, n // tn),
        in_specs=[pl.BlockSpec((tm, d), lambda i, j: (i, 0)), vec, vec, vec,
                  pl.BlockSpec((tn, d), lambda i, j: (j, 0))],
        out_specs=pl.BlockSpec((tn, tm), lambda i, j: (j, i)),
        out_shape=jax.ShapeDtypeStruct((n, s), BF16),
        scratch_shapes=[pltpu.VMEM((tm, d), BF16)],
        compiler_params=_cparams(("arbitrary", "arbitrary")),
        name="inproj_t",
    )(x, g, sc, sh, wt)


def _gla_kernel(q_ref, kt_ref, v_ref, gg_ref, ga_ref, wa2t_ref, ba_ref, on_ref, o_ref,
                state_ref, o_scr):
    tt = q_ref.shape[0]
    nchunk = tt // CHUNK

    @pl.when(pl.program_id(0) == 0)
    def _():
        state_ref[...] = jnp.zeros_like(state_ref)

    zt = _nt_dot(wa2t_ref[...], ga_ref[...], precision=HIGHEST) + ba_ref[...]
    lat = (jnp.minimum(zt, 0.0) - jnp.log1p(jnp.exp(-jnp.abs(zt)))) * (1.0 / GLA_TAU)
    row = lax.broadcasted_iota(jnp.int32, (tt, tt), 0)
    col = lax.broadcasted_iota(jnp.int32, (tt, tt), 1)
    same = (row // CHUNK) == (col // CHUNK)
    incl = jnp.where(same & (row <= col), 1.0, 0.0).astype(F32)
    full = jnp.where(same, 1.0, 0.0).astype(F32)
    cumt = jnp.dot(lat, incl, precision=HIGHEST, preferred_element_type=F32)
    tott = jnp.dot(lat, full, precision=HIGHEST, preferred_element_type=F32)
    kdt = kt_ref[...].astype(F32) * jnp.exp(tott - cumt)
    dec = jnp.exp(tott)

    lane = lax.broadcasted_iota(jnp.int32, (GLA_DK, 2 * CHUNK), 1)
    for c in range(nchunk):
        pair = (c // 2) * 2 * CHUNK
        if nchunk > 1:
            keep = (lane // CHUNK) == (c % 2)
        for h in range(GLA_HEADS):
            rows = slice(h * GLA_DK, (h + 1) * GLA_DK)
            vcols = slice(h * GLA_DV, (h + 1) * GLA_DV)
            if nchunk > 1:
                a = jnp.where(keep, kdt[rows, pair:pair + 2 * CHUNK], 0.0).astype(BF16)
                vp = v_ref[pair:pair + 2 * CHUNK, vcols]
            else:
                a = kdt[rows, :].astype(BF16)
                vp = v_ref[:, vcols]
            upd = jnp.dot(a, vp, preferred_element_type=F32)
            dcol = dec[rows, c * CHUNK:c * CHUNK + 1]
            st = state_ref[h] * dcol + upd
            state_ref[h] = st
            qc = q_ref[c * CHUNK:(c + 1) * CHUNK, rows]
            o_scr[c * CHUNK:(c + 1) * CHUNK, vcols] = jnp.dot(
                qc, st.astype(BF16), preferred_element_type=F32)

    for h in range(GLA_HEADS):
        vcols = slice(h * GLA_DV, (h + 1) * GLA_DV)
        o = o_scr[:, vcols] * (GLA_DK ** -0.5)
        o = o * lax.rsqrt(jnp.mean(o * o, axis=-1, keepdims=True) + EPS) * on_ref[...]
        g = gg_ref[:, vcols].astype(F32)
        o_ref[:, vcols] = (o * (g * jax.nn.sigmoid(g))).astype(BF16)


def _gla(proj, projt, ga, wa2t, ba_col, on_g, tt, col_q, col_v, col_g, row_k):
    s = proj.shape[0]
    qk = GLA_HEADS * GLA_DK
    vw = GLA_HEADS * GLA_DV
    return pl.pallas_call(
        _gla_kernel,
        grid=(s // tt,),
        in_specs=[pl.BlockSpec((tt, qk), lambda i: (i, col_q // qk)),
                  pl.BlockSpec((qk, tt), lambda i: (row_k // qk, i)),
                  pl.BlockSpec((tt, vw), lambda i: (i, col_v // vw)),
                  pl.BlockSpec((tt, vw), lambda i: (i, col_g // vw)),
                  pl.BlockSpec((tt, 128), lambda i: (i, 0)),
                  pl.BlockSpec((qk, 128), lambda i: (0, 0)),
                  pl.BlockSpec((qk, 1), lambda i: (0, 0)),
                  pl.BlockSpec((1, GLA_DV), lambda i: (0, 0))],
        out_specs=pl.BlockSpec((tt, vw), lambda i: (i, 0)),
        out_shape=jax.ShapeDtypeStruct((s, vw), BF16),
        scratch_shapes=[pltpu.VMEM((GLA_HEADS, GLA_DK, GLA_DV), F32),
                        pltpu.VMEM((tt, vw), F32)],
        compiler_params=_cparams(("arbitrary",)),
        name="gla",
    )(proj, projt, proj, proj, ga, wa2t, ba_col, on_g)


def _qknorm_rope_t(xt, g_col, cos, sin):
    n, tm = xt.shape
    x3 = xt.reshape(n // DIFF_DH, DIFF_DH, tm)
    r = lax.rsqrt(jnp.mean(x3 * x3, axis=1, keepdims=True) + EPS)
    y = x3 * r * g_col[None]
    half = ROT_DIM // 2
    y1, y2, rest = y[:, :half], y[:, half:ROT_DIM], y[:, ROT_DIM:]
    o1 = y1 * cos[None] - y2 * sin[None]
    o2 = y2 * cos[None] + y1 * sin[None]
    return jnp.concatenate([o1, o2, rest], axis=1)


def _qkprep_kernel(qt_ref, kt_ref, pos_ref, invf_ref, qg_ref, kg_ref, qa_ref, qb_ref, ko_ref):
    tm = qt_ref.shape[1]
    ang = pos_ref[...].astype(F32) * invf_ref[...]
    cos, sin = jnp.cos(ang), jnp.sin(ang)
    k3 = _qknorm_rope_t(kt_ref[...].astype(F32), kg_ref[...], cos, sin)
    ko_ref[...] = k3.reshape(-1, tm).T.astype(BF16)
    q3 = _qknorm_rope_t(qt_ref[...].astype(F32), qg_ref[...], cos, sin) * (
        DIFF_DH ** -0.5 * LOG2E)
    seg = lax.broadcasted_iota(jnp.int32, q3.shape, 0)
    qa_ref[...] = jnp.where(seg % 2 == 0, q3, 0.0).reshape(-1, tm).astype(BF16)
    qb_ref[...] = jnp.where(seg % 2 == 1, q3, 0.0).reshape(-1, tm).astype(BF16)


def _qkprep(projt, pos_row, invf_col, qg_col, kg_col, tm, row_q, row_k):
    s = projt.shape[1]
    n = DIFF_HEADS * 2 * DIFF_DH
    col = pl.BlockSpec((DIFF_DH, 1), lambda i: (0, 0))
    return pl.pallas_call(
        _qkprep_kernel,
        grid=(s // tm,),
        in_specs=[pl.BlockSpec((n, tm), lambda i: (row_q // n, i)),
                  pl.BlockSpec((n, tm), lambda i: (row_k // n, i)),
                  pl.BlockSpec((1, tm), lambda i: (0, i)),
                  pl.BlockSpec((ROT_DIM // 2, 1), lambda i: (0, 0)), col, col],
        out_specs=[pl.BlockSpec((n, tm), lambda i: (0, i)),
                   pl.BlockSpec((n, tm), lambda i: (0, i)),
                   pl.BlockSpec((tm, n), lambda i: (i, 0))],
        out_shape=[jax.ShapeDtypeStruct((n, s), BF16), jax.ShapeDtypeStruct((n, s), BF16),
                   jax.ShapeDtypeStruct((s, n), BF16)],
        compiler_params=_cparams(("arbitrary",)),
        name="qkprep",
    )(projt, projt, pos_row, invf_col, qg_col, kg_col)


ATT_COLS = 256


def _diffattn_kernel(qa_ref, qb_ref, k_ref, vt_ref, lq1_ref, lk1_ref, lq2_ref, lk2_ref, sg_ref,
                     o_ref, *scr, lambda_init, tk, cols):
    tq = qa_ref.shape[1]
    nblk = 2 * tq // cols
    q_scr, m_scr, l_scr, acc_scr = (scr[b * nblk:(b + 1) * nblk] for b in range(4))
    i = pl.program_id(1)
    for c in range(nblk):
        src = qa_ref if c * cols < tq else qb_ref
        off = (c * cols) % tq
        q_scr[c][...] = src[:, off:off + cols]
        m_scr[c][...] = jnp.full_like(m_scr[c], NEG_BIG)
        l_scr[c][...] = jnp.zeros_like(l_scr[c])
        acc_scr[c][...] = jnp.zeros_like(acc_scr[c])

    def scores(j, c):
        start = pl.multiple_of(j * tk, tk)
        return jnp.dot(k_ref[pl.ds(start, tk), :], q_scr[c][...], preferred_element_type=F32)

    def steps(tiles, masked):
        items = [(j, c) for j in tiles for c in range(nblk)]
        nxt = scores(*items[0])
        for n, (j, c) in enumerate(items):
            s = nxt
            if n + 1 < len(items):
                nxt = scores(*items[n + 1])
            start = pl.multiple_of(j * tk, tk)
            if masked:
                krow = lax.broadcasted_iota(jnp.int32, (tk, cols), 0)
                qcol = lax.broadcasted_iota(jnp.int32, (tk, cols), 1)
                qpos = i * tq + (c * cols) % tq + qcol
                s = jnp.where((start + krow) // CHUNK <= qpos // CHUNK, s, NEG_BIG)
            m_prev = m_scr[c][...]
            m_new = jnp.maximum(m_prev, jnp.max(s, axis=0, keepdims=True))
            alpha = jnp.exp2(m_prev - m_new)
            p = jnp.exp2(s - m_new)
            l_scr[c][...] = alpha * l_scr[c][...] + jnp.sum(p, axis=0, keepdims=True)
            acc_scr[c][...] = alpha * acc_scr[c][...] + jnp.dot(
                vt_ref[:, pl.ds(start, tk)], p.astype(BF16), preferred_element_type=F32)
            m_scr[c][...] = m_new

    n_full = (i * tq) // tk
    lax.fori_loop(0, n_full // 2, lambda t, c: (steps((2 * t, 2 * t + 1), False), c)[1], 0)

    @pl.when(n_full % 2 == 1)
    def _():
        steps((n_full - 1,), False)

    n_all = ((i + 1) * tq + tk - 1) // tk
    lax.fori_loop(n_full, n_all, lambda j, c: (steps((j,), True), c)[1], 0)

    o = jnp.concatenate([acc_scr[c][...] / l_scr[c][...] for c in range(nblk)], axis=1)
    lam = (jnp.exp(jnp.sum(lq1_ref[...] * lk1_ref[...]))
           - jnp.exp(jnp.sum(lq2_ref[...] * lk2_ref[...])) + lambda_init)
    o = o[:, :tq] - lam * o[:, tq:]
    o = o * lax.rsqrt(jnp.mean(o * o, axis=0, keepdims=True) + EPS) * sg_ref[...]
    o_ref[...] = (o * (1.0 - lambda_init)).T.astype(BF16)


def _diffattn(qat, qbt, kr, projt, lq1, lk1, lq2, lk2, sg_col, lambda_init, tq, tk, row_v):
    s = kr.shape[0]
    hd = 2 * DIFF_DH
    vec = pl.BlockSpec((1, DIFF_DH), lambda h, i: (0, 0))
    cols = min(ATT_COLS, tq)
    nblk = 2 * tq // cols
    kern = functools.partial(_diffattn_kernel, lambda_init=lambda_init, tk=tk, cols=cols)
    scratch = ([pltpu.VMEM((hd, cols), BF16)] * nblk + [pltpu.VMEM((1, cols), F32)] * (2 * nblk)
               + [pltpu.VMEM((DIFF_DV, cols), F32)] * nblk)
    return pl.pallas_call(
        kern,
        grid=(DIFF_HEADS, s // tq),
        in_specs=[pl.BlockSpec((hd, tq), lambda h, i: (h, i)),
                  pl.BlockSpec((hd, tq), lambda h, i: (h, i)),
                  pl.BlockSpec((s, hd), lambda h, i: (0, h)),
                  pl.BlockSpec((DIFF_DV, s), lambda h, i: (row_v // DIFF_DV + h, 0)),
                  vec, vec, vec, vec,
                  pl.BlockSpec((DIFF_DV, 1), lambda h, i: (0, 0))],
        out_specs=pl.BlockSpec((tq, DIFF_DV), lambda h, i: (i, h)),
        out_shape=jax.ShapeDtypeStruct((s, DIFF_HEADS * DIFF_DV), BF16),
        scratch_shapes=scratch,
        compiler_params=_cparams(("arbitrary", "arbitrary")),
        name="diffattn",
    )(qat, qbt, kr, projt, lq1, lk1, lq2, lk2, sg_col)


def _mergeout_kernel(og_ref, od_ref, mg_ref, md_ref, x_ref, wbg_ref, wbd_ref, wo_ref, gt_ref,
                     g2_ref, sc_ref, sh_ref, x1_ref, hf_ref, hfp_ref):
    bg = jnp.dot(og_ref[...], wbg_ref[...], preferred_element_type=F32)
    bd = jnp.dot(od_ref[...], wbd_ref[...], preferred_element_type=F32)
    merged = (jax.nn.sigmoid(mg_ref[...].astype(F32)) * bg
              + jax.nn.sigmoid(md_ref[...].astype(F32)) * bd)
    x1 = x_ref[...] + gt_ref[...] * jnp.dot(merged.astype(BF16), wo_ref[...],
                                             preferred_element_type=F32)
    x1_ref[...] = x1
    hf = _rms_mod(x1, g2_ref[...], sc_ref[...], sh_ref[...])
    hf_ref[...] = hf
    hfp_ref[...] = _pack_halves(hf)


def _mergeout(og, od, proj, x, wbg, wbd, wo, gt, g2, sc, sh, tm, col_mg, col_md):
    s, d = x.shape
    vec = pl.BlockSpec((1, d), lambda i: (0, 0))
    wspec = pl.BlockSpec((d, d), lambda i: (0, 0))
    row = pl.BlockSpec((tm, d), lambda i: (i, 0))
    return pl.pallas_call(
        _mergeout_kernel,
        grid=(s // tm,),
        in_specs=[row, row,
                  pl.BlockSpec((tm, d), lambda i: (i, col_mg // d)),
                  pl.BlockSpec((tm, d), lambda i: (i, col_md // d)),
                  row, wspec, wspec, wspec, vec, vec, vec, vec],
        out_specs=[row, row, pl.BlockSpec((tm, d // 2), lambda i: (i, 0))],
        out_shape=[jax.ShapeDtypeStruct((s, d), F32), jax.ShapeDtypeStruct((s, d), F32),
                   jax.ShapeDtypeStruct((s, d // 2), jnp.uint32)],
        compiler_params=_cparams(("arbitrary",)),
        name="mergeout",
    )(og, od, proj, proj, x, wbg, wbd, wo, gt, g2, sc, sh)


def _route_kernel(hf_ref, wrt_ref, bias_ref, idx_ref, wts_ref, rnk_ref, cnt_ref, run_scr):
    tr = hf_ref.shape[0]
    e = wrt_ref.shape[0]
    gsz = e // N_GROUPS

    @pl.when(pl.program_id(0) == 0)
    def _():
        run_scr[...] = jnp.zeros_like(run_scr)

    logits = _nt_dot(wrt_ref[...], hf_ref[...], precision=HIGHEST)
    scores = jax.nn.sigmoid(logits)
    biased = scores + bias_ref[...]
    g3 = biased.reshape(N_GROUPS, gsz, tr)
    m1 = jnp.max(g3, axis=1, keepdims=True)
    n_top = jnp.sum(jnp.where(g3 == m1, 1.0, 0.0), axis=1, keepdims=True)
    m2 = jnp.max(jnp.where(g3 < m1, g3, -jnp.inf), axis=1, keepdims=True)
    gs = (m1 + jnp.where(n_top >= 2.0, m1, m2)).reshape(N_GROUPS, tr)
    gi = lax.broadcasted_iota(jnp.int32, (N_GROUPS, tr), 0)
    beaten = jnp.zeros((N_GROUPS, tr), F32)
    for g in range(N_GROUPS):
        other = gs[g:g + 1, :]
        beaten = beaten + jnp.where((other > gs) | ((other == gs) & (g < gi)), 1.0, 0.0)
    gsel = (beaten < float(TOPK_GROUPS)).reshape(N_GROUPS, 1, tr)
    masked = jnp.where(gsel, g3, -jnp.inf).reshape(e, tr)

    ids = lax.broadcasted_iota(jnp.int32, (e, tr), 0)
    chosen = jnp.zeros((e, tr), F32)
    sel_idx, sel_score = [], []
    for _ in range(TOP_K):
        mx = jnp.max(masked, axis=0, keepdims=True)
        ix = jnp.min(jnp.where(masked == mx, ids, e), axis=0, keepdims=True)
        hit = ids == ix
        sel_idx.append(ix)
        sel_score.append(jnp.sum(jnp.where(hit, scores, 0.0), axis=0, keepdims=True))
        chosen = jnp.where(hit, 1.0, chosen)
        masked = jnp.where(hit, -jnp.inf, masked)
    idx = jnp.concatenate(sel_idx, axis=0)
    sc = jnp.concatenate(sel_score, axis=0)
    idx_ref[...] = idx
    wts_ref[...] = sc / jnp.sum(sc, axis=0, keepdims=True) * ROUTED_SCALE

    row = lax.broadcasted_iota(jnp.int32, (tr, tr), 0)
    col = lax.broadcasted_iota(jnp.int32, (tr, tr), 1)
    before = jnp.where(row < col, 1.0, 0.0).astype(BF16)
    prior = jnp.dot(chosen.astype(BF16), before, preferred_element_type=F32) + run_scr[:, 0:1]
    rnk_ref[...] = jnp.concatenate(
        [jnp.sum(jnp.where(ids == sel_idx[k], prior, 0.0), axis=0, keepdims=True)
         for k in range(TOP_K)], axis=0).astype(jnp.int32)
    run_scr[...] = run_scr[...] + jnp.sum(chosen, axis=1, keepdims=True)
    cnt_ref[...] = run_scr[...].astype(jnp.int32)


def _route(hf, wrt, bias_col, tr):
    s, d = hf.shape
    e = wrt.shape[0]
    tok = pl.BlockSpec((TOP_K, tr), lambda i: (0, i))
    return pl.pallas_call(
        _route_kernel,
        grid=(s // tr,),
        in_specs=[pl.BlockSpec((tr, d), lambda i: (i, 0)),
                  pl.BlockSpec((e, d), lambda i: (0, 0)),
                  pl.BlockSpec((e, 1), lambda i: (0, 0))],
        out_specs=[tok, tok, tok, pl.BlockSpec((e, 128), lambda i: (0, 0))],
        out_shape=[jax.ShapeDtypeStruct((TOP_K, s), jnp.int32),
                   jax.ShapeDtypeStruct((TOP_K, s), F32),
                   jax.ShapeDtypeStruct((TOP_K, s), jnp.int32),
                   jax.ShapeDtypeStruct((e, 128), jnp.int32)],
        scratch_shapes=[pltpu.VMEM((e, 128), F32)],
        compiler_params=_cparams(("arbitrary",)),
        name="route",
    )(hf, wrt, bias_col)


def _positions_kernel(idx_ref, rnk_ref, pstart_ref, pos_ref):
    e = pstart_ref.shape[0]
    ts = idx_ref.shape[1]
    ids = lax.broadcasted_iota(jnp.int32, (e, ts), 0)
    idx = idx_ref[...]
    pos_ref[...] = rnk_ref[...] + jnp.concatenate(
        [jnp.sum(jnp.where(ids == idx[k:k + 1, :], pstart_ref[...], 0), axis=0, keepdims=True)
         for k in range(TOP_K)], axis=0)


def _positions(idx, rnk, pstart_col, ts):
    s = idx.shape[1]
    e = pstart_col.shape[0]
    tok = pl.BlockSpec((TOP_K, ts), lambda i: (0, i))
    return pl.pallas_call(
        _positions_kernel,
        grid=(s // ts,),
        in_specs=[tok, tok, pl.BlockSpec((e, 1), lambda i: (0, 0))],
        out_specs=tok,
        out_shape=jax.ShapeDtypeStruct((TOP_K, s), jnp.int32),
        compiler_params=_cparams(("arbitrary",)),
        name="positions",
    )(idx, rnk, pstart_col)


def _row_copy(src_ref, src_row, dst_ref, dst_row, sem):
    return pltpu.make_async_copy(src_ref.at[pl.ds(src_row, 1), :],
                                 dst_ref.at[pl.ds(dst_row, 1), :], sem)


def _dispatch_kernel(pos_ref, hf_ref, xs_in_ref, xs_ref, sem):
    del xs_in_ref
    td = hf_ref.shape[0]

    def start(t, c):
        for k in range(TOP_K):
            _row_copy(hf_ref, t, xs_ref, pos_ref[k, t], sem).start()
        return c

    def wait(t, c):
        for k in range(TOP_K):
            _row_copy(hf_ref, t, xs_ref, pos_ref[k, t], sem).wait()
        return c

    lax.fori_loop(0, td, start, 0)
    lax.fori_loop(0, td, wait, 0)


def _dispatch(pos3, hf, xs_zero, td):
    s, d = hf.shape
    return pl.pallas_call(
        _dispatch_kernel,
        grid=(s // td,),
        in_specs=[pl.BlockSpec((None, TOP_K, td), lambda i: (i, 0, 0), memory_space=pltpu.SMEM),
                  pl.BlockSpec((td, d), lambda i: (i, 0)),
                  pl.BlockSpec(memory_space=pl.ANY)],
        out_specs=pl.BlockSpec(memory_space=pl.ANY),
        out_shape=jax.ShapeDtypeStruct(xs_zero.shape, xs_zero.dtype),
        scratch_shapes=[pltpu.SemaphoreType.DMA(())],
        input_output_aliases={2: 0},
        compiler_params=_cparams(("arbitrary",)),
        name="dispatch",
    )(pos3, hf, xs_zero)


def _swiglu_packed(xp, wg, wu, wd):
    lo, hi = _unpack_halves(xp)
    lo, hi = lo.astype(BF16), hi.astype(BF16)
    n = lo.shape[1]
    g = (jnp.dot(lo, wg[:n], preferred_element_type=F32)
         + jnp.dot(hi, wg[n:], preferred_element_type=F32))
    u = (jnp.dot(lo, wu[:n], preferred_element_type=F32)
         + jnp.dot(hi, wu[n:], preferred_element_type=F32))
    h = (g * jax.nn.sigmoid(g)) * u
    return jnp.dot(h.astype(BF16), wd[...], preferred_element_type=F32)


def _moe_kernel(ie_ref, ib_ref, nv_ref, xs_ref, wg_ref, wu_ref, wd_ref, ys_ref,
                wg_b, wu_b, wd_b):
    del ib_ref
    i = pl.program_id(0)

    @pl.when(i < nv_ref[0])
    def _():
        @pl.when((i == 0) | (ie_ref[i] != ie_ref[jnp.maximum(i - 1, 0)]))
        def _():
            wg_b[...] = wg_ref[...].astype(BF16)
            wu_b[...] = wu_ref[...].astype(BF16)
            wd_b[...] = wd_ref[...].astype(BF16)

        ys_ref[...] = _pack_halves(_swiglu_packed(xs_ref[...], wg_b, wu_b, wd_b))


def _moe(item_e, item_b, n_valid, xs, wg, wu, wd):
    m_pad, dh = xs.shape
    _, d, f = wg.shape
    n_items = item_e.shape[0]
    return pl.pallas_call(
        _moe_kernel,
        grid_spec=pltpu.PrefetchScalarGridSpec(
            num_scalar_prefetch=3,
            grid=(n_items,),
            in_specs=[pl.BlockSpec((MOE_ROWS, dh), lambda i, ie, ib, nv: (ib[i], 0)),
                      pl.BlockSpec((None, d, f), lambda i, ie, ib, nv: (ie[i], 0, 0)),
                      pl.BlockSpec((None, d, f), lambda i, ie, ib, nv: (ie[i], 0, 0)),
                      pl.BlockSpec((None, f, d), lambda i, ie, ib, nv: (ie[i], 0, 0))],
            out_specs=pl.BlockSpec((MOE_ROWS, dh), lambda i, ie, ib, nv: (ib[i], 0)),
            scratch_shapes=[pltpu.VMEM((d, f), BF16), pltpu.VMEM((d, f), BF16),
                            pltpu.VMEM((f, d), BF16)],
        ),
        out_shape=jax.ShapeDtypeStruct((m_pad, dh), jnp.uint32),
        compiler_params=_cparams(("arbitrary",)),
        name="moe",
    )(item_e, item_b, n_valid, xs, wg, wu, wd)


def _combine_kernel(pos_ref, wt_ref, hf_ref, x1_ref, gt_ref, sg_ref, su_ref, sd_ref, ys_ref,
                    o_ref, gbuf, sem):
    tc = x1_ref.shape[0]

    def copy(t, k):
        return pltpu.make_async_copy(ys_ref.at[pl.ds(pos_ref[k, t], 1), :],
                                     gbuf.at[k, pl.ds(t, 1), :], sem)

    def start(t, c):
        for k in range(TOP_K):
            copy(t, k).start()
        return c

    def wait(t, c):
        for k in range(TOP_K):
            copy(t, k).wait()
        return c

    lax.fori_loop(0, tc, start, 0)
    y = _swiglu_packed(hf_ref[...], sg_ref, su_ref, sd_ref)
    lax.fori_loop(0, tc, wait, 0)
    wt = wt_ref[...]
    n = gbuf.shape[2]
    r_lo = jnp.zeros((tc, n), F32)
    r_hi = jnp.zeros((tc, n), F32)
    for k in range(TOP_K):
        lo, hi = _unpack_halves(gbuf[k])
        r_lo = r_lo + lo * wt[:, k:k + 1]
        r_hi = r_hi + hi * wt[:, k:k + 1]
    y = y + jnp.concatenate([r_lo, r_hi], axis=1)
    o_ref[...] = x1_ref[...] + gt_ref[...] * y


def _combine(pos3, wts_t, hfp, x1, gt, sg, su, sd, ys, tc):
    s, d = x1.shape
    f = sg.shape[1]
    row = pl.BlockSpec((tc, d), lambda i: (i, 0))
    return pl.pallas_call(
        _combine_kernel,
        grid=(s // tc,),
        in_specs=[pl.BlockSpec((None, TOP_K, tc), lambda i: (i, 0, 0), memory_space=pltpu.SMEM),
                  pl.BlockSpec((tc, TOP_K), lambda i: (i, 0)),
                  pl.BlockSpec((tc, d // 2), lambda i: (i, 0)), row,
                  pl.BlockSpec((1, d), lambda i: (0, 0)),
                  pl.BlockSpec((d, f), lambda i: (0, 0)),
                  pl.BlockSpec((d, f), lambda i: (0, 0)),
                  pl.BlockSpec((f, d), lambda i: (0, 0)),
                  pl.BlockSpec(memory_space=pl.ANY)],
        out_specs=row,
        out_shape=jax.ShapeDtypeStruct((s, d), F32),
        scratch_shapes=[pltpu.VMEM((TOP_K, tc, d // 2), jnp.uint32),
                        pltpu.SemaphoreType.DMA(())],
        compiler_params=_cparams(("arbitrary",)),
        name="combine",
    )(pos3, wts_t, hfp, x1, gt, sg, su, sd, ys)


def _tile(n, want):
    t = min(n, want)
    assert n % t == 0, (n, t)
    return t


def _layer(l, x, c_col, pos_row, p):
    s, d = x.shape
    lambda_init = 0.8 - 0.6 * math.exp(-0.3 * l)
    gqk, gv = GLA_HEADS * GLA_DK, GLA_HEADS * GLA_DV
    dqk, dvw = DIFF_HEADS * 2 * DIFF_DH, DIFF_HEADS * DIFF_DV
    lowrank = p["gla_w_a2"].shape[0]

    mod = _ada(c_col, p["w_ada"], p["b_ada"][None, :])
    sh_a, sc_a, gt_a, sh_f, sc_f, gt_f = [mod[:, j * d:(j + 1) * d] for j in range(6)]

    w_in = p["w_in"]
    o = 0
    cols = {}
    for name, wdt in (("gq", gqk), ("gk", gqk), ("gv", gv), ("ga", lowrank), ("gg", gv),
                      ("dq", dqk), ("dk", dqk), ("dv", dvw), ("mg", d), ("md", d)):
        cols[name] = w_in[:, o:o + wdt]
        o += wdt
    row_names = ("gv", "gg", "mg", "md", "gq")
    w_row = jnp.concatenate([cols[n] for n in row_names], axis=1).astype(BF16)
    col_of, o = {}, 0
    for n in row_names:
        col_of[n] = o
        o += cols[n].shape[1]
    t_names = ("dq", "dk", "dv", "gk")
    w_t = jnp.concatenate([cols[n] for n in t_names], axis=1).T.astype(BF16)
    row_of, o = {}, 0
    for n in t_names:
        row_of[n] = o
        o += cols[n].shape[1]
    w_ga = jnp.pad(cols["ga"], ((0, 0), (0, 128 - lowrank))).astype(BF16)

    g1 = p["norm1_g"][None, :]
    tm = _tile(s, 1024)
    proj, ga = _inproj(x, g1, sc_a, sh_a, w_row, w_ga, tm, 512)
    projt = _inproj_t(x, g1, sc_a, sh_a, w_t, tm, 512)

    wa2t = jnp.pad(p["gla_w_a2"].T, ((0, 0), (0, 128 - lowrank)))
    o_gla = _gla(proj, projt, ga, wa2t, p["gla_b_a"][:, None], p["gla_onorm_g"][None, :],
                 _tile(s, 512), col_of["gq"], col_of["gv"], col_of["gg"], row_of["gk"])

    invf = ROPE_THETA ** (-jnp.arange(0, ROT_DIM, 2, dtype=F32) / ROT_DIM)
    qat, qbt, kr = _qkprep(projt, pos_row, invf[:, None], p["diff_qnorm_g"][:, None],
                           p["diff_knorm_g"][:, None], _tile(s, 512), row_of["dq"], row_of["dk"])
    tq = _tile(s, 512)
    o_diff = _diffattn(qat, qbt, kr, projt, p["diff_lq1"][None, :], p["diff_lk1"][None, :],
                       p["diff_lq2"][None, :], p["diff_lk2"][None, :],
                       p["diff_subln_g"][:, None], lambda_init, tq, tq, row_of["dv"])

    x1, hf, hfp = _mergeout(o_gla, o_diff, proj, x, p["w_branch_gla"].astype(BF16),
                       p["w_branch_diff"].astype(BF16), p["w_out"].astype(BF16), gt_a,
                       p["norm2_g"][None, :], sc_f, sh_f, _tile(s, 512),
                       col_of["mg"], col_of["md"])

    e = p["w_router"].shape[1]
    idx, wts, rnk, cnt = _route(hf, p["w_router"].T, p["router_bias"][:, None], _tile(s, 512))

    counts = cnt[:, 0]
    pcounts = ((counts + MOE_ROWS - 1) // MOE_ROWS) * MOE_ROWS
    pend = jnp.cumsum(pcounts)
    pstart = pend - pcounts
    pos = _positions(idx, rnk, pstart[:, None], _tile(s, 512))
    n_items = (s * TOP_K) // MOE_ROWS + e
    n_valid = (pend[-1] // MOE_ROWS).astype(jnp.int32)
    item_b = jnp.minimum(jnp.arange(n_items, dtype=jnp.int32), n_valid - 1)
    item_e = jnp.minimum(jnp.sum(pend[None, :] <= (item_b * MOE_ROWS)[:, None], axis=1),
                         e - 1).astype(jnp.int32)

    td = _tile(s, 256)
    pos_d = pos.reshape(TOP_K, s // td, td).transpose(1, 0, 2)
    xs = _dispatch(pos_d, hfp, jnp.zeros((n_items * MOE_ROWS, d // 2), jnp.uint32), td)
    ys = _moe(item_e, item_b, n_valid[None], xs, p["w_exp_gate"], p["w_exp_up"], p["w_exp_down"])
    tc = _tile(s, 128)
    pos_c = pos.reshape(TOP_K, s // tc, tc).transpose(1, 0, 2)
    return _combine(pos_c, wts.T, hfp, x1, gt_f, p["w_sh_gate"].astype(BF16),
                    p["w_sh_up"].astype(BF16), p["w_sh_down"].astype(BF16), ys, tc)


_LAYER_PARAMS = ("w_ada", "b_ada", "norm1_g", "w_in", "gla_w_a2", "gla_b_a", "gla_onorm_g",
                 "diff_qnorm_g", "diff_knorm_g", "diff_lq1", "diff_lk1", "diff_lq2", "diff_lk2",
                 "diff_subln_g", "w_branch_gla", "w_branch_diff", "w_out", "norm2_g", "w_router",
                 "router_bias", "w_exp_gate", "w_exp_up", "w_exp_down", "w_sh_gate", "w_sh_up",
                 "w_sh_down")


def kernel(x, c, positions, w_ada, b_ada, norm1_g, w_in, gla_w_a2, gla_b_a, gla_onorm_g, diff_qnorm_g, diff_knorm_g, diff_lq1, diff_lk1, diff_lq2, diff_lk2, diff_subln_g, w_branch_gla, w_branch_diff, w_out, norm2_g, w_router, router_bias, w_exp_gate, w_exp_up, w_exp_down, w_sh_gate, w_sh_up, w_sh_down):
    stacked = dict(zip(_LAYER_PARAMS, (
        w_ada, b_ada, norm1_g, w_in, gla_w_a2, gla_b_a, gla_onorm_g, diff_qnorm_g, diff_knorm_g,
        diff_lq1, diff_lk1, diff_lq2, diff_lk2, diff_subln_g, w_branch_gla, w_branch_diff, w_out,
        norm2_g, w_router, router_bias, w_exp_gate, w_exp_up, w_exp_down, w_sh_gate, w_sh_up,
        w_sh_down)))
    b, s, d = x.shape
    assert b == 1, "single-sequence kernel"
    xl = x[0]
    c_col = c[0][:, None]
    pos_row = positions.astype(jnp.int32)
    for l in range(w_ada.shape[0]):
        xl = _layer(l, xl, c_col, pos_row, {k: v[l] for k, v in stacked.items()})
    return xl[None]
```

```python
import functools
import math

import jax
import jax.numpy as jnp
from jax import lax
from jax.experimental import pallas as pl
from jax.experimental.pallas import tpu as pltpu

CHUNK = 64
EPS = 1e-6
GLA_HEADS = 4
GLA_DK = 128
GLA_DV = 256
GLA_TAU = 16.0
DIFF_HEADS = 8
DIFF_DH = 64
DIFF_DV = 2 * DIFF_DH
ROPE_THETA = 500000.0
ROT_DIM = DIFF_DH // 4
N_GROUPS = 8
TOPK_GROUPS = 4
TOP_K = 8
ROUTED_SCALE = 2.5

MOE_ROWS = 256
VMEM_LIMIT = 56 * 1024 * 1024
NEG_BIG = -1e30
LOG2E = 1.4426950408889634
HIGHEST = lax.Precision.HIGHEST
F32 = jnp.float32
BF16 = jnp.bfloat16


def _cparams(sem):
    return pltpu.CompilerParams(dimension_semantics=sem, vmem_limit_bytes=VMEM_LIMIT)


def _nt_dot(a, b, precision=None):
    return lax.dot_general(a, b, (((1,), (1,)), ((), ())), precision=precision,
                           preferred_element_type=F32)


def _pack_halves(x):
    n = x.shape[1] // 2
    lo = pltpu.bitcast(x[:, :n].astype(BF16).astype(F32), jnp.uint32) >> 16
    hi = pltpu.bitcast(x[:, n:].astype(BF16).astype(F32), jnp.uint32) & jnp.uint32(0xFFFF0000)
    return lo | hi


def _unpack_halves(w):
    return (pltpu.bitcast(w << 16, F32), pltpu.bitcast(w & jnp.uint32(0xFFFF0000), F32))


def _rms_mod(x, g, sc, sh):
    xn = x * lax.rsqrt(jnp.mean(x * x, axis=-1, keepdims=True) + EPS)
    return (xn * g) * (1.0 + sc) + sh


def _ada_kernel(c_ref, w_ref, b_ref, o_ref):
    c = c_ref[...]
    ca = c * jax.nn.sigmoid(c)
    o_ref[...] = jnp.sum(ca * w_ref[...], axis=0, keepdims=True) + b_ref[...]


def _ada(c_col, w, b):
    d, n = w.shape
    tn = min(1024, n)
    return pl.pallas_call(
        _ada_kernel,
        grid=(n // tn,),
        in_specs=[pl.BlockSpec((d, 1), lambda j: (0, 0)),
                  pl.BlockSpec((d, tn), lambda j: (0, j)),
                  pl.BlockSpec((1, tn), lambda j: (0, j))],
        out_specs=pl.BlockSpec((1, tn), lambda j: (0, j)),
        out_shape=jax.ShapeDtypeStruct((1, n), F32),
        compiler_params=_cparams(("arbitrary",)),
        name="ada",
    )(c_col, w, b)


def _inproj_kernel(x_ref, g_ref, sc_ref, sh_ref, w_ref, wga_ref, o_ref, ga_ref, h_scr):
    @pl.when(pl.program_id(1) == 0)
    def _():
        h = _rms_mod(x_ref[...], g_ref[...], sc_ref[...], sh_ref[...]).astype(BF16)
        h_scr[...] = h
        ga_ref[...] = jnp.dot(h, wga_ref[...], preferred_element_type=F32)

    o_ref[...] = jnp.dot(h_scr[...], w_ref[...], preferred_element_type=F32).astype(BF16)


def _inproj(x, g, sc, sh, w, wga, tm, tn):
    s, d = x.shape
    n = w.shape[1]
    vec = pl.BlockSpec((1, d), lambda i, j: (0, 0))
    return pl.pallas_call(
        _inproj_kernel,
        grid=(s // tm, n // tn),
        in_specs=[pl.BlockSpec((tm, d), lambda i, j: (i, 0)), vec, vec, vec,
                  pl.BlockSpec((d, tn), lambda i, j: (0, j)),
                  pl.BlockSpec((d, 128), lambda i, j: (0, 0))],
        out_specs=[pl.BlockSpec((tm, tn), lambda i, j: (i, j)),
                   pl.BlockSpec((tm, 128), lambda i, j: (i, 0))],
        out_shape=[jax.ShapeDtypeStruct((s, n), BF16), jax.ShapeDtypeStruct((s, 128), F32)],
        scratch_shapes=[pltpu.VMEM((tm, d), BF16)],
        compiler_params=_cparams(("arbitrary", "arbitrary")),
        name="inproj",
    )(x, g, sc, sh, w, wga)


def _inproj_t_kernel(x_ref, g_ref, sc_ref, sh_ref, wt_ref, o_ref, h_scr):
    @pl.when(pl.program_id(1) == 0)
    def _():
        h_scr[...] = _rms_mod(x_ref[...], g_ref[...], sc_ref[...], sh_ref[...]).astype(BF16)

    o_ref[...] = _nt_dot(wt_ref[...], h_scr[...]).astype(BF16)


def _inproj_t(x, g, sc, sh, wt, tm, tn):
    s, d = x.shape
    n = wt.shape[0]
    vec = pl.BlockSpec((1, d), lambda i, j: (0, 0))
    return pl.pallas_call(
        _inproj_t_kernel,
        grid=(s // tm, n // tn),
        in_specs=[pl.BlockSpec((tm, d), lambda i, j: (i, 0)), vec, vec, vec,
                  pl.BlockSpec((tn, d), lambda i, j: (j, 0))],
        out_specs=pl.BlockSpec((tn, tm), lambda i, j: (j, i)),
        out_shape=jax.ShapeDtypeStruct((n, s), BF16),
        scratch_shapes=[pltpu.VMEM((tm, d), BF16)],
        compiler_params=_cparams(("arbitrary", "arbitrary")),
        name="inproj_t",
    )(x, g, sc, sh, wt)


def _gla_kernel(q_ref, kt_ref, v_ref, gg_ref, ga_ref, wa2t_ref, ba_ref, on_ref, o_ref,
                state_ref, o_scr):
    tt = q_ref.shape[0]
    nchunk = tt // CHUNK

    @pl.when(pl.program_id(0) == 0)
    def _():
        state_ref[...] = jnp.zeros_like(state_ref)

    zt = _nt_dot(wa2t_ref[...], ga_ref[...], precision=HIGHEST) + ba_ref[...]
    lat = (jnp.minimum(zt, 0.0) - jnp.log1p(jnp.exp(-jnp.abs(zt)))) * (1.0 / GLA_TAU)
    row = lax.broadcasted_iota(jnp.int32, (tt, tt), 0)
    col = lax.broadcasted_iota(jnp.int32, (tt, tt), 1)
    same = (row // CHUNK) == (col // CHUNK)
    incl = jnp.where(same & (row <= col), 1.0, 0.0).astype(F32)
    full = jnp.where(same, 1.0, 0.0).astype(F32)
    cumt = jnp.dot(lat, incl, precision=HIGHEST, preferred_element_type=F32)
    tott = jnp.dot(lat, full, precision=HIGHEST, preferred_element_type=F32)
    kdt = kt_ref[...].astype(F32) * jnp.exp(tott - cumt)
    dec = jnp.exp(tott)

    lane = lax.broadcasted_iota(jnp.int32, (GLA_DK, 2 * CHUNK), 1)
    for c in range(nchunk):
        pair = (c // 2) * 2 * CHUNK
        if nchunk > 1:
            keep = (lane // CHUNK) == (c % 2)
        for h in range(GLA_HEADS):
            rows = slice(h * GLA_DK, (h + 1) * GLA_DK)
            vcols = slice(h * GLA_DV, (h + 1) * GLA_DV)
            if nchunk > 1:
                a = jnp.where(keep, kdt[rows, pair:pair + 2 * CHUNK], 0.0).astype(BF16)
                vp = v_ref[pair:pair + 2 * CHUNK, vcols]
            else:
                a = kdt[rows, :].astype(BF16)
                vp = v_ref[:, vcols]
            upd = jnp.dot(a, vp, preferred_element_type=F32)
            dcol = dec[rows, c * CHUNK:c * CHUNK + 1]
            st = state_ref[h] * dcol + upd
            state_ref[h] = st
            qc = q_ref[c * CHUNK:(c + 1) * CHUNK, rows]
            o_scr[c * CHUNK:(c + 1) * CHUNK, vcols] = jnp.dot(
                qc, st.astype(BF16), preferred_element_type=F32)

    for h in range(GLA_HEADS):
        vcols = slice(h * GLA_DV, (h + 1) * GLA_DV)
        o = o_scr[:, vcols] * (GLA_DK ** -0.5)
        o = o * lax.rsqrt(jnp.mean(o * o, axis=-1, keepdims=True) + EPS) * on_ref[...]
        g = gg_ref[:, vcols].astype(F32)
        o_ref[:, vcols] = (o * (g * jax.nn.sigmoid(g))).astype(BF16)


def _gla(proj, projt, ga, wa2t, ba_col, on_g, tt, col_q, col_v, col_g, row_k):
    s = proj.shape[0]
    qk = GLA_HEADS * GLA_DK
    vw = GLA_HEADS * GLA_DV
    return pl.pallas_call(
        _gla_kernel,
        grid=(s // tt,),
        in_specs=[pl.BlockSpec((tt, qk), lambda i: (i, col_q // qk)),
                  pl.BlockSpec((qk, tt), lambda i: (row_k // qk, i)),
                  pl.BlockSpec((tt, vw), lambda i: (i, col_v // vw)),
                  pl.BlockSpec((tt, vw), lambda i: (i, col_g // vw)),
                  pl.BlockSpec((tt, 128), lambda i: (i, 0)),
                  pl.BlockSpec((qk, 128), lambda i: (0, 0)),
                  pl.BlockSpec((qk, 1), lambda i: (0, 0)),
                  pl.BlockSpec((1, GLA_DV), lambda i: (0, 0))],
        out_specs=pl.BlockSpec((tt, vw), lambda i: (i, 0)),
        out_shape=jax.ShapeDtypeStruct((s, vw), BF16),
        scratch_shapes=[pltpu.VMEM((GLA_HEADS, GLA_DK, GLA_DV), F32),
                        pltpu.VMEM((tt, vw), F32)],
        compiler_params=_cparams(("arbitrary",)),
        name="gla",
    )(proj, projt, proj, proj, ga, wa2t, ba_col, on_g)


def _qknorm_rope_t(xt, g_col, cos, sin):
    n, tm = xt.shape
    x3 = xt.reshape(n // DIFF_DH, DIFF_DH, tm)
    r = lax.rsqrt(jnp.mean(x3 * x3, axis=1, keepdims=True) + EPS)
    y = x3 * r * g_col[None]
    half = ROT_DIM // 2
    y1, y2, rest = y[:, :half], y[:, half:ROT_DIM], y[:, ROT_DIM:]
    o1 = y1 * cos[None] - y2 * sin[None]
    o2 = y2 * cos[None] + y1 * sin[None]
    return jnp.concatenate([o1, o2, rest], axis=1)


def _qkprep_kernel(qt_ref, kt_ref, vt_ref, pos_ref, invf_ref, qg_ref, kg_ref, qa_ref, qb_ref,
                   ko_ref, ve_ref):
    tm = qt_ref.shape[1]
    v3 = vt_ref[...].reshape(DIFF_HEADS, DIFF_DV, tm)
    ones = jnp.ones((DIFF_HEADS, ATT_SUM_ROWS, tm), BF16)
    ve_ref[...] = jnp.concatenate([v3, ones], axis=1).reshape(-1, tm)
    ang = pos_ref[...].astype(F32) * invf_ref[...]
    cos, sin = jnp.cos(ang), jnp.sin(ang)
    k3 = _qknorm_rope_t(kt_ref[...].astype(F32), kg_ref[...], cos, sin)
    ko_ref[...] = k3.reshape(-1, tm).T.astype(BF16)
    q3 = _qknorm_rope_t(qt_ref[...].astype(F32), qg_ref[...], cos, sin) * (
        DIFF_DH ** -0.5 * LOG2E)
    seg = lax.broadcasted_iota(jnp.int32, q3.shape, 0)
    qa_ref[...] = jnp.where(seg % 2 == 0, q3, 0.0).reshape(-1, tm).astype(BF16)
    qb_ref[...] = jnp.where(seg % 2 == 1, q3, 0.0).reshape(-1, tm).astype(BF16)


def _qkprep(projt, pos_row, invf_col, qg_col, kg_col, tm, row_q, row_k, row_v):
    s = projt.shape[1]
    n = DIFF_HEADS * 2 * DIFF_DH
    ne = DIFF_HEADS * (DIFF_DV + ATT_SUM_ROWS)
    col = pl.BlockSpec((DIFF_DH, 1), lambda i: (0, 0))
    return pl.pallas_call(
        _qkprep_kernel,
        grid=(s // tm,),
        in_specs=[pl.BlockSpec((n, tm), lambda i: (row_q // n, i)),
                  pl.BlockSpec((n, tm), lambda i: (row_k // n, i)),
                  pl.BlockSpec((n, tm), lambda i: (row_v // n, i)),
                  pl.BlockSpec((1, tm), lambda i: (0, i)),
                  pl.BlockSpec((ROT_DIM // 2, 1), lambda i: (0, 0)), col, col],
        out_specs=[pl.BlockSpec((n, tm), lambda i: (0, i)),
                   pl.BlockSpec((n, tm), lambda i: (0, i)),
                   pl.BlockSpec((tm, n), lambda i: (i, 0)),
                   pl.BlockSpec((ne, tm), lambda i: (0, i))],
        out_shape=[jax.ShapeDtypeStruct((n, s), BF16), jax.ShapeDtypeStruct((n, s), BF16),
                   jax.ShapeDtypeStruct((s, n), BF16), jax.ShapeDtypeStruct((ne, s), BF16)],
        compiler_params=_cparams(("arbitrary",)),
        name="qkprep",
    )(projt, projt, projt, pos_row, invf_col, qg_col, kg_col)


ATT_COLS = 256
ATT_LOOKAHEAD = 3
ATT_SUM_ROWS = 16


def _diffattn_kernel(qa_ref, qb_ref, k_ref, vt_ref, lq1_ref, lk1_ref, lq2_ref, lk2_ref, sg_ref,
                     o_ref, *scr, lambda_init, tk, cols):
    tq = qa_ref.shape[1]
    nblk = 2 * tq // cols
    q_scr, m_scr, acc_scr = (scr[b * nblk:(b + 1) * nblk] for b in range(3))
    i = pl.program_id(1)
    for c in range(nblk):
        src = qa_ref if c * cols < tq else qb_ref
        off = (c * cols) % tq
        q_scr[c][...] = src[:, off:off + cols]
        m_scr[c][...] = jnp.full_like(m_scr[c], NEG_BIG)
        acc_scr[c][...] = jnp.zeros_like(acc_scr[c])

    def scores(j, c):
        start = pl.multiple_of(j * tk, tk)
        return jnp.dot(k_ref[pl.ds(start, tk), :], q_scr[c][...], preferred_element_type=F32)

    def steps(tiles, masked):
        items = [(j, c) for j in tiles for c in range(nblk)]
        pending = [scores(*it) for it in items[:ATT_LOOKAHEAD]]
        for n, (j, c) in enumerate(items):
            s = pending.pop(0)
            if n + ATT_LOOKAHEAD < len(items):
                pending.append(scores(*items[n + ATT_LOOKAHEAD]))
            start = pl.multiple_of(j * tk, tk)
            if masked:
                krow = lax.broadcasted_iota(jnp.int32, (tk, cols), 0)
                qcol = lax.broadcasted_iota(jnp.int32, (tk, cols), 1)
                qpos = i * tq + (c * cols) % tq + qcol
                s = jnp.where((start + krow) // CHUNK <= qpos // CHUNK, s, NEG_BIG)
            m_prev = m_scr[c][...]
            m_new = jnp.maximum(m_prev, jnp.max(s, axis=0, keepdims=True))
            alpha = jnp.exp2(m_prev - m_new)
            p = jnp.exp2((s - m_new).astype(BF16))
            acc_scr[c][...] = alpha * acc_scr[c][...] + jnp.dot(
                vt_ref[:, pl.ds(start, tk)], p, preferred_element_type=F32)
            m_scr[c][...] = m_new

    n_full = (i * tq) // tk
    lax.fori_loop(0, n_full // 2, lambda t, c: (steps((2 * t, 2 * t + 1), False), c)[1], 0)

    @pl.when(n_full % 2 == 1)
    def _():
        steps((n_full - 1,), False)

    n_all = ((i + 1) * tq + tk - 1) // tk
    lax.fori_loop(n_full, n_all, lambda j, c: (steps((j,), True), c)[1], 0)

    o = jnp.concatenate([acc_scr[c][:DIFF_DV] / acc_scr[c][DIFF_DV:DIFF_DV + 1]
                         for c in range(nblk)], axis=1)
    lam = (jnp.exp(jnp.sum(lq1_ref[...] * lk1_ref[...]))
           - jnp.exp(jnp.sum(lq2_ref[...] * lk2_ref[...])) + lambda_init)
    o = o[:, :tq] - lam * o[:, tq:]
    o = o * lax.rsqrt(jnp.mean(o * o, axis=0, keepdims=True) + EPS) * sg_ref[...]
    o_ref[...] = (o * (1.0 - lambda_init)).T.astype(BF16)


def _diffattn(qat, qbt, kr, vte, lq1, lk1, lq2, lk2, sg_col, lambda_init, tq, tk):
    s = kr.shape[0]
    hd = 2 * DIFF_DH
    vec = pl.BlockSpec((1, DIFF_DH), lambda h, i: (0, 0))
    cols = min(ATT_COLS, tq)
    nblk = 2 * tq // cols
    kern = functools.partial(_diffattn_kernel, lambda_init=lambda_init, tk=tk, cols=cols)
    dve = DIFF_DV + ATT_SUM_ROWS
    scratch = ([pltpu.VMEM((hd, cols), BF16)] * nblk + [pltpu.VMEM((1, cols), F32)] * nblk
               + [pltpu.VMEM((dve, cols), F32)] * nblk)
    return pl.pallas_call(
        kern,
        grid=(DIFF_HEADS, s // tq),
        in_specs=[pl.BlockSpec((hd, tq), lambda h, i: (h, i)),
                  pl.BlockSpec((hd, tq), lambda h, i: (h, i)),
                  pl.BlockSpec((s, hd), lambda h, i: (0, h)),
                  pl.BlockSpec((dve, s), lambda h, i: (h, 0)),
                  vec, vec, vec, vec,
                  pl.BlockSpec((DIFF_DV, 1), lambda h, i: (0, 0))],
        out_specs=pl.BlockSpec((tq, DIFF_DV), lambda h, i: (i, h)),
        out_shape=jax.ShapeDtypeStruct((s, DIFF_HEADS * DIFF_DV), BF16),
        scratch_shapes=scratch,
        compiler_params=_cparams(("arbitrary", "arbitrary")),
        name="diffattn",
    )(qat, qbt, kr, vte, lq1, lk1, lq2, lk2, sg_col)


def _mergeout_kernel(og_ref, od_ref, mg_ref, md_ref, x_ref, wbg_ref, wbd_ref, wo_ref, gt_ref,
                     g2_ref, sc_ref, sh_ref, x1_ref, hf_ref, hfp_ref):
    bg = jnp.dot(og_ref[...], wbg_ref[...], preferred_element_type=F32)
    bd = jnp.dot(od_ref[...], wbd_ref[...], preferred_element_type=F32)
    merged = (jax.nn.sigmoid(mg_ref[...].astype(F32)) * bg
              + jax.nn.sigmoid(md_ref[...].astype(F32)) * bd)
    x1 = x_ref[...] + gt_ref[...] * jnp.dot(merged.astype(BF16), wo_ref[...],
                                             preferred_element_type=F32)
    x1_ref[...] = x1
    hf = _rms_mod(x1, g2_ref[...], sc_ref[...], sh_ref[...])
    hf_ref[...] = hf
    hfp_ref[...] = _pack_halves(hf)


def _mergeout(og, od, proj, x, wbg, wbd, wo, gt, g2, sc, sh, tm, col_mg, col_md):
    s, d = x.shape
    vec = pl.BlockSpec((1, d), lambda i: (0, 0))
    wspec = pl.BlockSpec((d, d), lambda i: (0, 0))
    row = pl.BlockSpec((tm, d), lambda i: (i, 0))
    return pl.pallas_call(
        _mergeout_kernel,
        grid=(s // tm,),
        in_specs=[row, row,
                  pl.BlockSpec((tm, d), lambda i: (i, col_mg // d)),
                  pl.BlockSpec((tm, d), lambda i: (i, col_md // d)),
                  row, wspec, wspec, wspec, vec, vec, vec, vec],
        out_specs=[row, row, pl.BlockSpec((tm, d // 2), lambda i: (i, 0))],
        out_shape=[jax.ShapeDtypeStruct((s, d), F32), jax.ShapeDtypeStruct((s, d), F32),
                   jax.ShapeDtypeStruct((s, d // 2), jnp.uint32)],
        compiler_params=_cparams(("arbitrary",)),
        name="mergeout",
    )(og, od, proj, proj, x, wbg, wbd, wo, gt, g2, sc, sh)


def _route_kernel(hf_ref, wrt_ref, bias_ref, idx_ref, wts_ref, rnk_ref, cnt_ref, run_scr):
    tr = hf_ref.shape[0]
    e = wrt_ref.shape[0]
    gsz = e // N_GROUPS

    @pl.when(pl.program_id(0) == 0)
    def _():
        run_scr[...] = jnp.zeros_like(run_scr)

    logits = _nt_dot(wrt_ref[...], hf_ref[...], precision=HIGHEST)
    scores = jax.nn.sigmoid(logits)
    biased = scores + bias_ref[...]
    g3 = biased.reshape(N_GROUPS, gsz, tr)
    m1 = jnp.max(g3, axis=1, keepdims=True)
    n_top = jnp.sum(jnp.where(g3 == m1, 1.0, 0.0), axis=1, keepdims=True)
    m2 = jnp.max(jnp.where(g3 < m1, g3, -jnp.inf), axis=1, keepdims=True)
    gs = (m1 + jnp.where(n_top >= 2.0, m1, m2)).reshape(N_GROUPS, tr)
    gi = lax.broadcasted_iota(jnp.int32, (N_GROUPS, tr), 0)
    beaten = jnp.zeros((N_GROUPS, tr), F32)
    for g in range(N_GROUPS):
        other = gs[g:g + 1, :]
        beaten = beaten + jnp.where((other > gs) | ((other == gs) & (g < gi)), 1.0, 0.0)
    gsel = (beaten < float(TOPK_GROUPS)).reshape(N_GROUPS, 1, tr)
    masked = jnp.where(gsel, g3, -jnp.inf).reshape(e, tr)

    ids = lax.broadcasted_iota(jnp.int32, (e, tr), 0)
    chosen = jnp.zeros((e, tr), F32)
    sel_idx, sel_score = [], []
    for _ in range(TOP_K):
        mx = jnp.max(masked, axis=0, keepdims=True)
        ix = jnp.min(jnp.where(masked == mx, ids, e), axis=0, keepdims=True)
        hit = ids == ix
        sel_idx.append(ix)
        sel_score.append(jnp.sum(jnp.where(hit, scores, 0.0), axis=0, keepdims=True))
        chosen = jnp.where(hit, 1.0, chosen)
        masked = jnp.where(hit, -jnp.inf, masked)
    idx = jnp.concatenate(sel_idx, axis=0)
    sc = jnp.concatenate(sel_score, axis=0)
    idx_ref[...] = idx
    wts_ref[...] = sc / jnp.sum(sc, axis=0, keepdims=True) * ROUTED_SCALE

    row = lax.broadcasted_iota(jnp.int32, (tr, tr), 0)
    col = lax.broadcasted_iota(jnp.int32, (tr, tr), 1)
    before = jnp.where(row < col, 1.0, 0.0).astype(BF16)
    prior = jnp.dot(chosen.astype(BF16), before, preferred_element_type=F32) + run_scr[:, 0:1]
    rnk_ref[...] = jnp.concatenate(
        [jnp.sum(jnp.where(ids == sel_idx[k], prior, 0.0), axis=0, keepdims=True)
         for k in range(TOP_K)], axis=0).astype(jnp.int32)
    run_scr[...] = run_scr[...] + jnp.sum(chosen, axis=1, keepdims=True)
    cnt_ref[...] = run_scr[...].astype(jnp.int32)


def _route(hf, wrt, bias_col, tr):
    s, d = hf.shape
    e = wrt.shape[0]
    tok = pl.BlockSpec((TOP_K, tr), lambda i: (0, i))
    return pl.pallas_call(
        _route_kernel,
        grid=(s // tr,),
        in_specs=[pl.BlockSpec((tr, d), lambda i: (i, 0)),
                  pl.BlockSpec((e, d), lambda i: (0, 0)),
                  pl.BlockSpec((e, 1), lambda i: (0, 0))],
        out_specs=[tok, tok, tok, pl.BlockSpec((e, 128), lambda i: (0, 0))],
        out_shape=[jax.ShapeDtypeStruct((TOP_K, s), jnp.int32),
                   jax.ShapeDtypeStruct((TOP_K, s), F32),
                   jax.ShapeDtypeStruct((TOP_K, s), jnp.int32),
                   jax.ShapeDtypeStruct((e, 128), jnp.int32)],
        scratch_shapes=[pltpu.VMEM((e, 128), F32)],
        compiler_params=_cparams(("arbitrary",)),
        name="route",
    )(hf, wrt, bias_col)


def _positions_kernel(idx_ref, rnk_ref, pstart_ref, pos_ref):
    e = pstart_ref.shape[0]
    ts = idx_ref.shape[1]
    ids = lax.broadcasted_iota(jnp.int32, (e, ts), 0)
    idx = idx_ref[...]
    pos_ref[...] = rnk_ref[...] + jnp.concatenate(
        [jnp.sum(jnp.where(ids == idx[k:k + 1, :], pstart_ref[...], 0), axis=0, keepdims=True)
         for k in range(TOP_K)], axis=0)


def _positions(idx, rnk, pstart_col, ts):
    s = idx.shape[1]
    e = pstart_col.shape[0]
    tok = pl.BlockSpec((TOP_K, ts), lambda i: (0, i))
    return pl.pallas_call(
        _positions_kernel,
        grid=(s // ts,),
        in_specs=[tok, tok, pl.BlockSpec((e, 1), lambda i: (0, 0))],
        out_specs=tok,
        out_shape=jax.ShapeDtypeStruct((TOP_K, s), jnp.int32),
        compiler_params=_cparams(("arbitrary",)),
        name="positions",
    )(idx, rnk, pstart_col)


def _row_copy(src_ref, src_row, dst_ref, dst_row, sem):
    return pltpu.make_async_copy(src_ref.at[pl.ds(src_row, 1), :],
                                 dst_ref.at[pl.ds(dst_row, 1), :], sem)


def _dispatch_kernel(pos_ref, hf_ref, xs_in_ref, xs_ref, sem):
    del xs_in_ref
    td = hf_ref.shape[0]

    def start(t, c):
        for k in range(TOP_K):
            _row_copy(hf_ref, t, xs_ref, pos_ref[k, t], sem).start()
        return c

    def wait(t, c):
        for k in range(TOP_K):
            _row_copy(hf_ref, t, xs_ref, pos_ref[k, t], sem).wait()
        return c

    lax.fori_loop(0, td, start, 0)
    lax.fori_loop(0, td, wait, 0)


def _dispatch(pos3, hf, xs_zero, td):
    s, d = hf.shape
    return pl.pallas_call(
        _dispatch_kernel,
        grid=(s // td,),
        in_specs=[pl.BlockSpec((None, TOP_K, td), lambda i: (i, 0, 0), memory_space=pltpu.SMEM),
                  pl.BlockSpec((td, d), lambda i: (i, 0)),
                  pl.BlockSpec(memory_space=pl.ANY)],
        out_specs=pl.BlockSpec(memory_space=pl.ANY),
        out_shape=jax.ShapeDtypeStruct(xs_zero.shape, xs_zero.dtype),
        scratch_shapes=[pltpu.SemaphoreType.DMA(())],
        input_output_aliases={2: 0},
        compiler_params=_cparams(("arbitrary",)),
        name="dispatch",
    )(pos3, hf, xs_zero)


def _swiglu_packed(xp, wg, wu, wd):
    lo, hi = _unpack_halves(xp)
    lo, hi = lo.astype(BF16), hi.astype(BF16)
    n = lo.shape[1]
    g = (jnp.dot(lo, wg[:n], preferred_element_type=F32)
         + jnp.dot(hi, wg[n:], preferred_element_type=F32))
    u = (jnp.dot(lo, wu[:n], preferred_element_type=F32)
         + jnp.dot(hi, wu[n:], preferred_element_type=F32))
    h = (g * jax.nn.sigmoid(g)) * u
    return jnp.dot(h.astype(BF16), wd[...], preferred_element_type=F32)


def _moe_kernel(ie_ref, ib_ref, nv_ref, xs_ref, wg_ref, wu_ref, wd_ref, ys_ref,
                wg_b, wu_b, wd_b):
    del ib_ref
    i = pl.program_id(0)

    @pl.when(i < nv_ref[0])
    def _():
        @pl.when((i == 0) | (ie_ref[i] != ie_ref[jnp.maximum(i - 1, 0)]))
        def _():
            wg_b[...] = wg_ref[...].astype(BF16)
            wu_b[...] = wu_ref[...].astype(BF16)
            wd_b[...] = wd_ref[...].astype(BF16)

        ys_ref[...] = _pack_halves(_swiglu_packed(xs_ref[...], wg_b, wu_b, wd_b))


def _moe(item_e, item_b, n_valid, xs, wg, wu, wd):
    m_pad, dh = xs.shape
    _, d, f = wg.shape
    n_items = item_e.shape[0]
    return pl.pallas_call(
        _moe_kernel,
        grid_spec=pltpu.PrefetchScalarGridSpec(
            num_scalar_prefetch=3,
            grid=(n_items,),
            in_specs=[pl.BlockSpec((MOE_ROWS, dh), lambda i, ie, ib, nv: (ib[i], 0)),
                      pl.BlockSpec((None, d, f), lambda i, ie, ib, nv: (ie[i], 0, 0)),
                      pl.BlockSpec((None, d, f), lambda i, ie, ib, nv: (ie[i], 0, 0)),
                      pl.BlockSpec((None, f, d), lambda i, ie, ib, nv: (ie[i], 0, 0))],
            out_specs=pl.BlockSpec((MOE_ROWS, dh), lambda i, ie, ib, nv: (ib[i], 0)),
            scratch_shapes=[pltpu.VMEM((d, f), BF16), pltpu.VMEM((d, f), BF16),
                            pltpu.VMEM((f, d), BF16)],
        ),
        out_shape=jax.ShapeDtypeStruct((m_pad, dh), jnp.uint32),
        compiler_params=_cparams(("arbitrary",)),
        name="moe",
    )(item_e, item_b, n_valid, xs, wg, wu, wd)


def _combine_kernel(pos_ref, wt_ref, hf_ref, x1_ref, gt_ref, sg_ref, su_ref, sd_ref, ys_ref,
                    o_ref, gbuf, sem):
    tc = x1_ref.shape[0]

    def copy(t, k):
        return pltpu.make_async_copy(ys_ref.at[pl.ds(pos_ref[k, t], 1), :],
                                     gbuf.at[k, pl.ds(t, 1), :], sem)

    def start(t, c):
        for k in range(TOP_K):
            copy(t, k).start()
        return c

    def wait(t, c):
        for k in range(TOP_K):
            copy(t, k).wait()
        return c

    lax.fori_loop(0, tc, start, 0)
    y = _swiglu_packed(hf_ref[...], sg_ref, su_ref, sd_ref)
    lax.fori_loop(0, tc, wait, 0)
    wt = wt_ref[...]
    n = gbuf.shape[2]
    r_lo = jnp.zeros((tc, n), F32)
    r_hi = jnp.zeros((tc, n), F32)
    for k in range(TOP_K):
        lo, hi = _unpack_halves(gbuf[k])
        r_lo = r_lo + lo * wt[:, k:k + 1]
        r_hi = r_hi + hi * wt[:, k:k + 1]
    y = y + jnp.concatenate([r_lo, r_hi], axis=1)
    o_ref[...] = x1_ref[...] + gt_ref[...] * y


def _combine(pos3, wts_t, hfp, x1, gt, sg, su, sd, ys, tc):
    s, d = x1.shape
    f = sg.shape[1]
    row = pl.BlockSpec((tc, d), lambda i: (i, 0))
    return pl.pallas_call(
        _combine_kernel,
        grid=(s // tc,),
        in_specs=[pl.BlockSpec((None, TOP_K, tc), lambda i: (i, 0, 0), memory_space=pltpu.SMEM),
                  pl.BlockSpec((tc, TOP_K), lambda i: (i, 0)),
                  pl.BlockSpec((tc, d // 2), lambda i: (i, 0)), row,
                  pl.BlockSpec((1, d), lambda i: (0, 0)),
                  pl.BlockSpec((d, f), lambda i: (0, 0)),
                  pl.BlockSpec((d, f), lambda i: (0, 0)),
                  pl.BlockSpec((f, d), lambda i: (0, 0)),
                  pl.BlockSpec(memory_space=pl.ANY)],
        out_specs=row,
        out_shape=jax.ShapeDtypeStruct((s, d), F32),
        scratch_shapes=[pltpu.VMEM((TOP_K, tc, d // 2), jnp.uint32),
                        pltpu.SemaphoreType.DMA(())],
        compiler_params=_cparams(("arbitrary",)),
        name="combine",
    )(pos3, wts_t, hfp, x1, gt, sg, su, sd, ys)


def _tile(n, want):
    t = min(n, want)
    assert n % t == 0, (n, t)
    return t


def _layer(l, x, c_col, pos_row, p):
    s, d = x.shape
    lambda_init = 0.8 - 0.6 * math.exp(-0.3 * l)
    gqk, gv = GLA_HEADS * GLA_DK, GLA_HEADS * GLA_DV
    dqk, dvw = DIFF_HEADS * 2 * DIFF_DH, DIFF_HEADS * DIFF_DV
    lowrank = p["gla_w_a2"].shape[0]

    mod = _ada(c_col, p["w_ada"], p["b_ada"][None, :])
    sh_a, sc_a, gt_a, sh_f, sc_f, gt_f = [mod[:, j * d:(j + 1) * d] for j in range(6)]

    w_in = p["w_in"]
    o = 0
    cols = {}
    for name, wdt in (("gq", gqk), ("gk", gqk), ("gv", gv), ("ga", lowrank), ("gg", gv),
                      ("dq", dqk), ("dk", dqk), ("dv", dvw), ("mg", d), ("md", d)):
        cols[name] = w_in[:, o:o + wdt]
        o += wdt
    row_names = ("gv", "gg", "mg", "md", "gq")
    w_row = jnp.concatenate([cols[n] for n in row_names], axis=1).astype(BF16)
    col_of, o = {}, 0
    for n in row_names:
        col_of[n] = o
        o += cols[n].shape[1]
    t_names = ("dq", "dk", "dv", "gk")
    w_t = jnp.concatenate([cols[n] for n in t_names], axis=1).T.astype(BF16)
    row_of, o = {}, 0
    for n in t_names:
        row_of[n] = o
        o += cols[n].shape[1]
    w_ga = jnp.pad(cols["ga"], ((0, 0), (0, 128 - lowrank))).astype(BF16)

    g1 = p["norm1_g"][None, :]
    tm = _tile(s, 1024)
    proj, ga = _inproj(x, g1, sc_a, sh_a, w_row, w_ga, tm, 512)
    projt = _inproj_t(x, g1, sc_a, sh_a, w_t, tm, 512)

    wa2t = jnp.pad(p["gla_w_a2"].T, ((0, 0), (0, 128 - lowrank)))
    o_gla = _gla(proj, projt, ga, wa2t, p["gla_b_a"][:, None], p["gla_onorm_g"][None, :],
                 _tile(s, 512), col_of["gq"], col_of["gv"], col_of["gg"], row_of["gk"])

    invf = ROPE_THETA ** (-jnp.arange(0, ROT_DIM, 2, dtype=F32) / ROT_DIM)
    qat, qbt, kr, vte = _qkprep(projt, pos_row, invf[:, None], p["diff_qnorm_g"][:, None],
                                p["diff_knorm_g"][:, None], _tile(s, 512), row_of["dq"],
                                row_of["dk"], row_of["dv"])
    tq = _tile(s, 512)
    o_diff = _diffattn(qat, qbt, kr, vte, p["diff_lq1"][None, :], p["diff_lk1"][None, :],
                       p["diff_lq2"][None, :], p["diff_lk2"][None, :],
                       p["diff_subln_g"][:, None], lambda_init, tq, tq)

    x1, hf, hfp = _mergeout(o_gla, o_diff, proj, x, p["w_branch_gla"].astype(BF16),
                       p["w_branch_diff"].astype(BF16), p["w_out"].astype(BF16), gt_a,
                       p["norm2_g"][None, :], sc_f, sh_f, _tile(s, 512),
                       col_of["mg"], col_of["md"])

    e = p["w_router"].shape[1]
    idx, wts, rnk, cnt = _route(hf, p["w_router"].T, p["router_bias"][:, None], _tile(s, 512))

    counts = cnt[:, 0]
    pcounts = ((counts + MOE_ROWS - 1) // MOE_ROWS) * MOE_ROWS
    pend = jnp.cumsum(pcounts)
    pstart = pend - pcounts
    pos = _positions(idx, rnk, pstart[:, None], _tile(s, 512))
    n_items = (s * TOP_K) // MOE_ROWS + e
    n_valid = (pend[-1] // MOE_ROWS).astype(jnp.int32)
    item_b = jnp.minimum(jnp.arange(n_items, dtype=jnp.int32), n_valid - 1)
    item_e = jnp.minimum(jnp.sum(pend[None, :] <= (item_b * MOE_ROWS)[:, None], axis=1),
                         e - 1).astype(jnp.int32)

    td = _tile(s, 256)
    pos_d = pos.reshape(TOP_K, s // td, td).transpose(1, 0, 2)
    xs = _dispatch(pos_d, hfp, jnp.zeros((n_items * MOE_ROWS, d // 2), jnp.uint32), td)
    ys = _moe(item_e, item_b, n_valid[None], xs, p["w_exp_gate"], p["w_exp_up"], p["w_exp_down"])
    tc = _tile(s, 128)
    pos_c = pos.reshape(TOP_K, s // tc, tc).transpose(1, 0, 2)
    return _combine(pos_c, wts.T, hfp, x1, gt_f, p["w_sh_gate"].astype(BF16),
                    p["w_sh_up"].astype(BF16), p["w_sh_down"].astype(BF16), ys, tc)


_LAYER_PARAMS = ("w_ada", "b_ada", "norm1_g", "w_in", "gla_w_a2", "gla_b_a", "gla_onorm_g",
                 "diff_qnorm_g", "diff_knorm_g", "diff_lq1", "diff_lk1", "diff_lq2", "diff_lk2",
                 "diff_subln_g", "w_branch_gla", "w_branch_diff", "w_out", "norm2_g", "w_router",
                 "router_bias", "w_exp_gate", "w_exp_up", "w_exp_down", "w_sh_gate", "w_sh_up",
                 "w_sh_down")


def kernel(x, c, positions, w_ada, b_ada, norm1_g, w_in, gla_w_a2, gla_b_a, gla_onorm_g, diff_qnorm_g, diff_knorm_g, diff_lq1, diff_lk1, diff_lq2, diff_lk2, diff_subln_g, w_branch_gla, w_branch_diff, w_out, norm2_g, w_router, router_bias, w_exp_gate, w_exp_up, w_exp_down, w_sh_gate, w_sh_up, w_sh_down):
    stacked = dict(zip(_LAYER_PARAMS, (
        w_ada, b_ada, norm1_g, w_in, gla_w_a2, gla_b_a, gla_onorm_g, diff_qnorm_g, diff_knorm_g,
        diff_lq1, diff_lk1, diff_lq2, diff_lk2, diff_subln_g, w_branch_gla, w_branch_diff, w_out,
        norm2_g, w_router, router_bias, w_exp_gate, w_exp_up, w_exp_down, w_sh_gate, w_sh_up,
        w_sh_down)))
    b, s, d = x.shape
    assert b == 1, "single-sequence kernel"
    xl = x[0]
    c_col = c[0][:, None]
    pos_row = positions.astype(jnp.int32)
    for l in range(w_ada.shape[0]):
        xl = _layer(l, xl, c_col, pos_row, {k: v[l] for k, v in stacked.items()})
    return xl[None]
```

```python
import functools
import math

import jax
import jax.numpy as jnp
from jax import lax
from jax.experimental import pallas as pl
from jax.experimental.pallas import tpu as pltpu

CHUNK = 64
EPS = 1e-6
GLA_HEADS = 4
GLA_DK = 128
GLA_DV = 256
GLA_TAU = 16.0
DIFF_HEADS = 8
DIFF_DH = 64
DIFF_DV = 2 * DIFF_DH
ROPE_THETA = 500000.0
ROT_DIM = DIFF_DH // 4
N_GROUPS = 8
TOPK_GROUPS = 4
TOP_K = 8
ROUTED_SCALE = 2.5

MOE_ROWS = 256
VMEM_LIMIT = 56 * 1024 * 1024
NEG_BIG = -1e30
LOG2E = 1.4426950408889634
HIGHEST = lax.Precision.HIGHEST
F32 = jnp.float32
BF16 = jnp.bfloat16


def _cparams(sem):
    return pltpu.CompilerParams(dimension_semantics=sem, vmem_limit_bytes=VMEM_LIMIT)


def _nt_dot(a, b, precision=None):
    return lax.dot_general(a, b, (((1,), (1,)), ((), ())), precision=precision,
                           preferred_element_type=F32)


def _pack_halves(x):
    n = x.shape[1] // 2
    lo = pltpu.bitcast(x[:, :n].astype(BF16).astype(F32), jnp.uint32) >> 16
    hi = pltpu.bitcast(x[:, n:].astype(BF16).astype(F32), jnp.uint32) & jnp.uint32(0xFFFF0000)
    return lo | hi


def _unpack_halves(w):
    return (pltpu.bitcast(w << 16, F32), pltpu.bitcast(w & jnp.uint32(0xFFFF0000), F32))


def _rms_mod(x, g, sc, sh):
    xn = x * lax.rsqrt(jnp.mean(x * x, axis=-1, keepdims=True) + EPS)
    return (xn * g) * (1.0 + sc) + sh


def _ada_kernel(c_ref, w_ref, b_ref, o_ref):
    c = c_ref[...]
    ca = c * jax.nn.sigmoid(c)
    o_ref[...] = jnp.sum(ca * w_ref[...], axis=0, keepdims=True) + b_ref[...]


def _ada(c_col, w, b):
    d, n = w.shape
    tn = min(1024, n)
    return pl.pallas_call(
        _ada_kernel,
        grid=(n // tn,),
        in_specs=[pl.BlockSpec((d, 1), lambda j: (0, 0)),
                  pl.BlockSpec((d, tn), lambda j: (0, j)),
                  pl.BlockSpec((1, tn), lambda j: (0, j))],
        out_specs=pl.BlockSpec((1, tn), lambda j: (0, j)),
        out_shape=jax.ShapeDtypeStruct((1, n), F32),
        compiler_params=_cparams(("arbitrary",)),
        name="ada",
    )(c_col, w, b)


def _inproj_kernel(x_ref, g_ref, sc_ref, sh_ref, w_ref, wga_ref, o_ref, ga_ref, h_scr):
    @pl.when(pl.program_id(1) == 0)
    def _():
        h = _rms_mod(x_ref[...], g_ref[...], sc_ref[...], sh_ref[...]).astype(BF16)
        h_scr[...] = h
        ga_ref[...] = jnp.dot(h, wga_ref[...], preferred_element_type=F32)

    o_ref[...] = jnp.dot(h_scr[...], w_ref[...], preferred_element_type=F32).astype(BF16)


def _inproj(x, g, sc, sh, w, wga, tm, tn):
    s, d = x.shape
    n = w.shape[1]
    vec = pl.BlockSpec((1, d), lambda i, j: (0, 0))
    return pl.pallas_call(
        _inproj_kernel,
        grid=(s // tm, n // tn),
        in_specs=[pl.BlockSpec((tm, d), lambda i, j: (i, 0)), vec, vec, vec,
                  pl.BlockSpec((d, tn), lambda i, j: (0, j)),
                  pl.BlockSpec((d, 128), lambda i, j: (0, 0))],
        out_specs=[pl.BlockSpec((tm, tn), lambda i, j: (i, j)),
                   pl.BlockSpec((tm, 128), lambda i, j: (i, 0))],
        out_shape=[jax.ShapeDtypeStruct((s, n), BF16), jax.ShapeDtypeStruct((s, 128), F32)],
        scratch_shapes=[pltpu.VMEM((tm, d), BF16)],
        compiler_params=_cparams(("arbitrary", "arbitrary")),
        name="inproj",
    )(x, g, sc, sh, w, wga)


def _inproj_t_kernel(x_ref, g_ref, sc_ref, sh_ref, wt_ref, o_ref, h_scr):
    @pl.when(pl.program_id(1) == 0)
    def _():
        h_scr[...] = _rms_mod(x_ref[...], g_ref[...], sc_ref[...], sh_ref[...]).astype(BF16)

    o_ref[...] = _nt_dot(wt_ref[...], h_scr[...]).astype(BF16)


def _inproj_t(x, g, sc, sh, wt, tm, tn):
    s, d = x.shape
    n = wt.shape[0]
    vec = pl.BlockSpec((1, d), lambda i, j: (0, 0))
    return pl.pallas_call(
        _inproj_t_kernel,
        grid=(s // tm, n // tn),
        in_specs=[pl.BlockSpec((tm, d), lambda i, j: (i, 0)), vec, vec, vec,
                  pl.BlockSpec((tn, d), lambda i, j: (j, 0))],
        out_specs=pl.BlockSpec((tn, tm), lambda i, j: (j, i)),
        out_shape=jax.ShapeDtypeStruct((n, s), BF16),
        scratch_shapes=[pltpu.VMEM((tm, d), BF16)],
        compiler_params=_cparams(("arbitrary", "arbitrary")),
        name="inproj_t",
    )(x, g, sc, sh, wt)


def _gla_kernel(q_ref, kt_ref, v_ref, gg_ref, ga_ref, wa2t_ref, ba_ref, on_ref, o_ref,
                state_ref, o_scr):
    tt = q_ref.shape[0]
    nchunk = tt // CHUNK

    @pl.when(pl.program_id(0) == 0)
    def _():
        state_ref[...] = jnp.zeros_like(state_ref)

    zt = _nt_dot(wa2t_ref[...], ga_ref[...], precision=HIGHEST) + ba_ref[...]
    lat = (jnp.minimum(zt, 0.0) - jnp.log1p(jnp.exp(-jnp.abs(zt)))) * (1.0 / GLA_TAU)
    row = lax.broadcasted_iota(jnp.int32, (tt, tt), 0)
    col = lax.broadcasted_iota(jnp.int32, (tt, tt), 1)
    same = (row // CHUNK) == (col // CHUNK)
    incl = jnp.where(same & (row <= col), 1.0, 0.0).astype(F32)
    full = jnp.where(same, 1.0, 0.0).astype(F32)
    cumt = jnp.dot(lat, incl, precision=HIGHEST, preferred_element_type=F32)
    tott = jnp.dot(lat, full, precision=HIGHEST, preferred_element_type=F32)
    kdt = kt_ref[...].astype(F32) * jnp.exp(tott - cumt)
    dec = jnp.exp(tott)

    lane = lax.broadcasted_iota(jnp.int32, (GLA_DK, 2 * CHUNK), 1)
    for c in range(nchunk):
        pair = (c // 2) * 2 * CHUNK
        if nchunk > 1:
            keep = (lane // CHUNK) == (c % 2)
        for h in range(GLA_HEADS):
            rows = slice(h * GLA_DK, (h + 1) * GLA_DK)
            vcols = slice(h * GLA_DV, (h + 1) * GLA_DV)
            if nchunk > 1:
                a = jnp.where(keep, kdt[rows, pair:pair + 2 * CHUNK], 0.0).astype(BF16)
                vp = v_ref[pair:pair + 2 * CHUNK, vcols]
            else:
                a = kdt[rows, :].astype(BF16)
                vp = v_ref[:, vcols]
            upd = jnp.dot(a, vp, preferred_element_type=F32)
            dcol = dec[rows, c * CHUNK:c * CHUNK + 1]
            st = state_ref[h] * dcol + upd
            state_ref[h] = st
            qc = q_ref[c * CHUNK:(c + 1) * CHUNK, rows]
            o_scr[c * CHUNK:(c + 1) * CHUNK, vcols] = jnp.dot(
                qc, st.astype(BF16), preferred_element_type=F32)

    for h in range(GLA_HEADS):
        vcols = slice(h * GLA_DV, (h + 1) * GLA_DV)
        o = o_scr[:, vcols] * (GLA_DK ** -0.5)
        o = o * lax.rsqrt(jnp.mean(o * o, axis=-1, keepdims=True) + EPS) * on_ref[...]
        g = gg_ref[:, vcols].astype(F32)
        o_ref[:, vcols] = (o * (g * jax.nn.sigmoid(g))).astype(BF16)


def _gla(proj, projt, ga, wa2t, ba_col, on_g, tt, col_q, col_v, col_g, row_k):
    s = proj.shape[0]
    qk = GLA_HEADS * GLA_DK
    vw = GLA_HEADS * GLA_DV
    return pl.pallas_call(
        _gla_kernel,
        grid=(s // tt,),
        in_specs=[pl.BlockSpec((tt, qk), lambda i: (i, col_q // qk)),
                  pl.BlockSpec((qk, tt), lambda i: (row_k // qk, i)),
                  pl.BlockSpec((tt, vw), lambda i: (i, col_v // vw)),
                  pl.BlockSpec((tt, vw), lambda i: (i, col_g // vw)),
                  pl.BlockSpec((tt, 128), lambda i: (i, 0)),
                  pl.BlockSpec((qk, 128), lambda i: (0, 0)),
                  pl.BlockSpec((qk, 1), lambda i: (0, 0)),
                  pl.BlockSpec((1, GLA_DV), lambda i: (0, 0))],
        out_specs=pl.BlockSpec((tt, vw), lambda i: (i, 0)),
        out_shape=jax.ShapeDtypeStruct((s, vw), BF16),
        scratch_shapes=[pltpu.VMEM((GLA_HEADS, GLA_DK, GLA_DV), F32),
                        pltpu.VMEM((tt, vw), F32)],
        compiler_params=_cparams(("arbitrary",)),
        name="gla",
    )(proj, projt, proj, proj, ga, wa2t, ba_col, on_g)


def _qknorm_rope_t(xt, g_col, cos, sin):
    n, tm = xt.shape
    x3 = xt.reshape(n // DIFF_DH, DIFF_DH, tm)
    r = lax.rsqrt(jnp.mean(x3 * x3, axis=1, keepdims=True) + EPS)
    y = x3 * r * g_col[None]
    half = ROT_DIM // 2
    y1, y2, rest = y[:, :half], y[:, half:ROT_DIM], y[:, ROT_DIM:]
    o1 = y1 * cos[None] - y2 * sin[None]
    o2 = y2 * cos[None] + y1 * sin[None]
    return jnp.concatenate([o1, o2, rest], axis=1)


def _qkprep_kernel(qt_ref, kt_ref, vt_ref, pos_ref, invf_ref, qg_ref, kg_ref, qa_ref, qb_ref,
                   ko_ref, ve_ref):
    tm = qt_ref.shape[1]
    v3 = vt_ref[...].reshape(DIFF_HEADS, DIFF_DV, tm)
    ones = jnp.ones((DIFF_HEADS, ATT_SUM_ROWS, tm), BF16)
    ve_ref[...] = jnp.concatenate([v3, ones], axis=1).reshape(-1, tm)
    ang = pos_ref[...].astype(F32) * invf_ref[...]
    cos, sin = jnp.cos(ang), jnp.sin(ang)
    k3 = _qknorm_rope_t(kt_ref[...].astype(F32), kg_ref[...], cos, sin)
    ko_ref[...] = k3.reshape(-1, tm).T.astype(BF16)
    q3 = _qknorm_rope_t(qt_ref[...].astype(F32), qg_ref[...], cos, sin) * (
        DIFF_DH ** -0.5 * LOG2E)
    seg = lax.broadcasted_iota(jnp.int32, q3.shape, 0)
    qa_ref[...] = jnp.where(seg % 2 == 0, q3, 0.0).reshape(-1, tm).astype(BF16)
    qb_ref[...] = jnp.where(seg % 2 == 1, q3, 0.0).reshape(-1, tm).astype(BF16)


def _qkprep(projt, pos_row, invf_col, qg_col, kg_col, tm, row_q, row_k, row_v):
    s = projt.shape[1]
    n = DIFF_HEADS * 2 * DIFF_DH
    ne = DIFF_HEADS * (DIFF_DV + ATT_SUM_ROWS)
    col = pl.BlockSpec((DIFF_DH, 1), lambda i: (0, 0))
    return pl.pallas_call(
        _qkprep_kernel,
        grid=(s // tm,),
        in_specs=[pl.BlockSpec((n, tm), lambda i: (row_q // n, i)),
                  pl.BlockSpec((n, tm), lambda i: (row_k // n, i)),
                  pl.BlockSpec((n, tm), lambda i: (row_v // n, i)),
                  pl.BlockSpec((1, tm), lambda i: (0, i)),
                  pl.BlockSpec((ROT_DIM // 2, 1), lambda i: (0, 0)), col, col],
        out_specs=[pl.BlockSpec((n, tm), lambda i: (0, i)),
                   pl.BlockSpec((n, tm), lambda i: (0, i)),
                   pl.BlockSpec((tm, n), lambda i: (i, 0)),
                   pl.BlockSpec((ne, tm), lambda i: (0, i))],
        out_shape=[jax.ShapeDtypeStruct((n, s), BF16), jax.ShapeDtypeStruct((n, s), BF16),
                   jax.ShapeDtypeStruct((s, n), BF16), jax.ShapeDtypeStruct((ne, s), BF16)],
        compiler_params=_cparams(("arbitrary",)),
        name="qkprep",
    )(projt, projt, projt, pos_row, invf_col, qg_col, kg_col)


ATT_COLS = 256
ATT_LOOKAHEAD = 3
ATT_SUM_ROWS = 16


def _diffattn_kernel(qa_ref, qb_ref, k_ref, vt_ref, lq1_ref, lk1_ref, lq2_ref, lk2_ref, sg_ref,
                     o_ref, *scr, lambda_init, tk, cols):
    tq = qa_ref.shape[1]
    nblk = 2 * tq // cols
    q_scr, m_scr, acc_scr = (scr[b * nblk:(b + 1) * nblk] for b in range(3))
    i = pl.program_id(1)
    for c in range(nblk):
        src = qa_ref if c * cols < tq else qb_ref
        off = (c * cols) % tq
        q_scr[c][...] = src[:, off:off + cols]
        m_scr[c][...] = jnp.full_like(m_scr[c], NEG_BIG)
        acc_scr[c][...] = jnp.zeros_like(acc_scr[c])

    def scores(j, c):
        start = pl.multiple_of(j * tk, tk)
        return jnp.dot(k_ref[pl.ds(start, tk), :], q_scr[c][...], preferred_element_type=F32)

    def steps(tiles, masked):
        items = [(j, c) for j in tiles for c in range(nblk)]
        pending = [scores(*it) for it in items[:ATT_LOOKAHEAD]]
        for n, (j, c) in enumerate(items):
            s = pending.pop(0)
            if n + ATT_LOOKAHEAD < len(items):
                pending.append(scores(*items[n + ATT_LOOKAHEAD]))
            start = pl.multiple_of(j * tk, tk)
            if masked:
                krow = lax.broadcasted_iota(jnp.int32, (tk, cols), 0)
                qcol = lax.broadcasted_iota(jnp.int32, (tk, cols), 1)
                qpos = i * tq + (c * cols) % tq + qcol
                s = jnp.where((start + krow) // CHUNK <= qpos // CHUNK, s, NEG_BIG)
            m_prev = m_scr[c][...]
            m_new = jnp.maximum(m_prev, jnp.max(s, axis=0, keepdims=True))
            alpha = jnp.exp2(m_prev - m_new)
            p = jnp.exp2((s - m_new).astype(BF16))
            acc_scr[c][...] = alpha * acc_scr[c][...] + jnp.dot(
                vt_ref[:, pl.ds(start, tk)], p, preferred_element_type=F32)
            m_scr[c][...] = m_new

    n_full = (i * tq) // tk
    lax.fori_loop(0, n_full // 2, lambda t, c: (steps((2 * t, 2 * t + 1), False), c)[1], 0)

    @pl.when(n_full % 2 == 1)
    def _():
        steps((n_full - 1,), False)

    n_all = ((i + 1) * tq + tk - 1) // tk
    lax.fori_loop(n_full, n_all, lambda j, c: (steps((j,), True), c)[1], 0)

    o = jnp.concatenate([acc_scr[c][:DIFF_DV] / acc_scr[c][DIFF_DV:DIFF_DV + 1]
                         for c in range(nblk)], axis=1)
    lam = (jnp.exp(jnp.sum(lq1_ref[...] * lk1_ref[...]))
           - jnp.exp(jnp.sum(lq2_ref[...] * lk2_ref[...])) + lambda_init)
    o = o[:, :tq] - lam * o[:, tq:]
    o = o * lax.rsqrt(jnp.mean(o * o, axis=0, keepdims=True) + EPS) * sg_ref[...]
    o_ref[...] = (o * (1.0 - lambda_init)).T.astype(BF16)


def _diffattn(qat, qbt, kr, vte, lq1, lk1, lq2, lk2, sg_col, lambda_init, tq, tk):
    s = kr.shape[0]
    hd = 2 * DIFF_DH
    vec = pl.BlockSpec((1, DIFF_DH), lambda h, i: (0, 0))
    cols = min(ATT_COLS, tq)
    nblk = 2 * tq // cols
    kern = functools.partial(_diffattn_kernel, lambda_init=lambda_init, tk=tk, cols=cols)
    dve = DIFF_DV + ATT_SUM_ROWS
    scratch = ([pltpu.VMEM((hd, cols), BF16)] * nblk + [pltpu.VMEM((1, cols), F32)] * nblk
               + [pltpu.VMEM((dve, cols), F32)] * nblk)
    return pl.pallas_call(
        kern,
        grid=(DIFF_HEADS, s // tq),
        in_specs=[pl.BlockSpec((hd, tq), lambda h, i: (h, i)),
                  pl.BlockSpec((hd, tq), lambda h, i: (h, i)),
                  pl.BlockSpec((s, hd), lambda h, i: (0, h)),
                  pl.BlockSpec((dve, s), lambda h, i: (h, 0)),
                  vec, vec, vec, vec,
                  pl.BlockSpec((DIFF_DV, 1), lambda h, i: (0, 0))],
        out_specs=pl.BlockSpec((tq, DIFF_DV), lambda h, i: (i, h)),
        out_shape=jax.ShapeDtypeStruct((s, DIFF_HEADS * DIFF_DV), BF16),
        scratch_shapes=scratch,
        compiler_params=_cparams(("arbitrary", "arbitrary")),
        name="diffattn",
    )(qat, qbt, kr, vte, lq1, lk1, lq2, lk2, sg_col)


def _mergeout_kernel(og_ref, od_ref, mg_ref, md_ref, x_ref, wbg_ref, wbd_ref, wo_ref, gt_ref,
                     g2_ref, sc_ref, sh_ref, x1_ref, hf_ref, hfp_ref):
    bg = jnp.dot(og_ref[...], wbg_ref[...], preferred_element_type=F32)
    bd = jnp.dot(od_ref[...], wbd_ref[...], preferred_element_type=F32)
    merged = (jax.nn.sigmoid(mg_ref[...].astype(F32)) * bg
              + jax.nn.sigmoid(md_ref[...].astype(F32)) * bd)
    x1 = x_ref[...] + gt_ref[...] * jnp.dot(merged.astype(BF16), wo_ref[...],
                                             preferred_element_type=F32)
    x1_ref[...] = x1
    hf = _rms_mod(x1, g2_ref[...], sc_ref[...], sh_ref[...])
    hf_ref[...] = hf
    hfp_ref[...] = _pack_halves(hf)


def _mergeout(og, od, proj, x, wbg, wbd, wo, gt, g2, sc, sh, tm, col_mg, col_md):
    s, d = x.shape
    vec = pl.BlockSpec((1, d), lambda i: (0, 0))
    wspec = pl.BlockSpec((d, d), lambda i: (0, 0))
    row = pl.BlockSpec((tm, d), lambda i: (i, 0))
    return pl.pallas_call(
        _mergeout_kernel,
        grid=(s // tm,),
        in_specs=[row, row,
                  pl.BlockSpec((tm, d), lambda i: (i, col_mg // d)),
                  pl.BlockSpec((tm, d), lambda i: (i, col_md // d)),
                  row, wspec, wspec, wspec, vec, vec, vec, vec],
        out_specs=[row, row, pl.BlockSpec((tm, d // 2), lambda i: (i, 0))],
        out_shape=[jax.ShapeDtypeStruct((s, d), F32), jax.ShapeDtypeStruct((s, d), F32),
                   jax.ShapeDtypeStruct((s, d // 2), jnp.uint32)],
        compiler_params=_cparams(("arbitrary",)),
        name="mergeout",
    )(og, od, proj, proj, x, wbg, wbd, wo, gt, g2, sc, sh)


def _route_kernel(hf_ref, wrt_ref, bias_ref, idx_ref, wts_ref, rnk_ref, cnt_ref, run_scr):
    tr = hf_ref.shape[0]
    e = wrt_ref.shape[0]
    gsz = e // N_GROUPS

    @pl.when(pl.program_id(0) == 0)
    def _():
        run_scr[...] = jnp.zeros_like(run_scr)

    logits = _nt_dot(wrt_ref[...], hf_ref[...], precision=HIGHEST)
    scores = jax.nn.sigmoid(logits)
    biased = scores + bias_ref[...]
    g3 = biased.reshape(N_GROUPS, gsz, tr)
    m1 = jnp.max(g3, axis=1, keepdims=True)
    n_top = jnp.sum(jnp.where(g3 == m1, 1.0, 0.0), axis=1, keepdims=True)
    m2 = jnp.max(jnp.where(g3 < m1, g3, -jnp.inf), axis=1, keepdims=True)
    gs = (m1 + jnp.where(n_top >= 2.0, m1, m2)).reshape(N_GROUPS, tr)
    gi = lax.broadcasted_iota(jnp.int32, (N_GROUPS, tr), 0)
    beaten = jnp.zeros((N_GROUPS, tr), F32)
    for g in range(N_GROUPS):
        other = gs[g:g + 1, :]
        beaten = beaten + jnp.where((other > gs) | ((other == gs) & (g < gi)), 1.0, 0.0)
    gsel = (beaten < float(TOPK_GROUPS)).reshape(N_GROUPS, 1, tr)
    masked = jnp.where(gsel, g3, -jnp.inf).reshape(e, tr)

    ids = lax.broadcasted_iota(jnp.int32, (e, tr), 0)
    chosen = jnp.zeros((e, tr), F32)
    sel_idx, sel_score = [], []
    for _ in range(TOP_K):
        mx = jnp.max(masked, axis=0, keepdims=True)
        ix = jnp.min(jnp.where(masked == mx, ids, e), axis=0, keepdims=True)
        hit = ids == ix
        sel_idx.append(ix)
        sel_score.append(jnp.sum(jnp.where(hit, scores, 0.0), axis=0, keepdims=True))
        chosen = jnp.where(hit, 1.0, chosen)
        masked = jnp.where(hit, -jnp.inf, masked)
    idx = jnp.concatenate(sel_idx, axis=0)
    sc = jnp.concatenate(sel_score, axis=0)
    idx_ref[...] = idx
    wts_ref[...] = sc / jnp.sum(sc, axis=0, keepdims=True) * ROUTED_SCALE

    row = lax.broadcasted_iota(jnp.int32, (tr, tr), 0)
    col = lax.broadcasted_iota(jnp.int32, (tr, tr), 1)
    before = jnp.where(row < col, 1.0, 0.0).astype(BF16)
    prior = jnp.dot(chosen.astype(BF16), before, preferred_element_type=F32) + run_scr[:, 0:1]
    rnk_ref[...] = jnp.concatenate(
        [jnp.sum(jnp.where(ids == sel_idx[k], prior, 0.0), axis=0, keepdims=True)
         for k in range(TOP_K)], axis=0).astype(jnp.int32)
    run_scr[...] = run_scr[...] + jnp.sum(chosen, axis=1, keepdims=True)
    cnt_ref[...] = run_scr[...].astype(jnp.int32)


def _route(hf, wrt, bias_col, tr):
    s, d = hf.shape
    e = wrt.shape[0]
    tok = pl.BlockSpec((TOP_K, tr), lambda i: (0, i))
    return pl.pallas_call(
        _route_kernel,
        grid=(s // tr,),
        in_specs=[pl.BlockSpec((tr, d), lambda i: (i, 0)),
                  pl.BlockSpec((e, d), lambda i: (0, 0)),
                  pl.BlockSpec((e, 1), lambda i: (0, 0))],
        out_specs=[tok, tok, tok, pl.BlockSpec((e, 128), lambda i: (0, 0))],
        out_shape=[jax.ShapeDtypeStruct((TOP_K, s), jnp.int32),
                   jax.ShapeDtypeStruct((TOP_K, s), F32),
                   jax.ShapeDtypeStruct((TOP_K, s), jnp.int32),
                   jax.ShapeDtypeStruct((e, 128), jnp.int32)],
        scratch_shapes=[pltpu.VMEM((e, 128), F32)],
        compiler_params=_cparams(("arbitrary",)),
        name="route",
    )(hf, wrt, bias_col)


def _positions_kernel(idx_ref, rnk_ref, pstart_ref, pos_ref):
    e = pstart_ref.shape[0]
    ts = idx_ref.shape[1]
    ids = lax.broadcasted_iota(jnp.int32, (e, ts), 0)
    idx = idx_ref[...]
    pos_ref[...] = rnk_ref[...] + jnp.concatenate(
        [jnp.sum(jnp.where(ids == idx[k:k + 1, :], pstart_ref[...], 0), axis=0, keepdims=True)
         for k in range(TOP_K)], axis=0)


def _positions(idx, rnk, pstart_col, ts):
    s = idx.shape[1]
    e = pstart_col.shape[0]
    tok = pl.BlockSpec((TOP_K, ts), lambda i: (0, i))
    return pl.pallas_call(
        _positions_kernel,
        grid=(s // ts,),
        in_specs=[tok, tok, pl.BlockSpec((e, 1), lambda i: (0, 0))],
        out_specs=tok,
        out_shape=jax.ShapeDtypeStruct((TOP_K, s), jnp.int32),
        compiler_params=_cparams(("arbitrary",)),
        name="positions",
    )(idx, rnk, pstart_col)


def _row_copy(src_ref, src_row, dst_ref, dst_row, sem):
    return pltpu.make_async_copy(src_ref.at[pl.ds(src_row, 1), :],
                                 dst_ref.at[pl.ds(dst_row, 1), :], sem)


def _dispatch_kernel(pos_ref, hf_ref, xs_in_ref, xs_ref, sem):
    del xs_in_ref
    td = hf_ref.shape[0]

    def start(t, c):
        for k in range(TOP_K):
            _row_copy(hf_ref, t, xs_ref, pos_ref[k, t], sem).start()
        return c

    def wait(t, c):
        for k in range(TOP_K):
            _row_copy(hf_ref, t, xs_ref, pos_ref[k, t], sem).wait()
        return c

    lax.fori_loop(0, td, start, 0)
    lax.fori_loop(0, td, wait, 0)


def _dispatch(pos3, hf, xs_zero, td):
    s, d = hf.shape
    return pl.pallas_call(
        _dispatch_kernel,
        grid=(s // td,),
        in_specs=[pl.BlockSpec((None, TOP_K, td), lambda i: (i, 0, 0), memory_space=pltpu.SMEM),
                  pl.BlockSpec((td, d), lambda i: (i, 0)),
                  pl.BlockSpec(memory_space=pl.ANY)],
        out_specs=pl.BlockSpec(memory_space=pl.ANY),
        out_shape=jax.ShapeDtypeStruct(xs_zero.shape, xs_zero.dtype),
        scratch_shapes=[pltpu.SemaphoreType.DMA(())],
        input_output_aliases={2: 0},
        compiler_params=_cparams(("arbitrary",)),
        name="dispatch",
    )(pos3, hf, xs_zero)


def _swiglu_packed(xp, wg, wu, wd):
    lo, hi = _unpack_halves(xp)
    lo, hi = lo.astype(BF16), hi.astype(BF16)
    n = lo.shape[1]
    g = (jnp.dot(lo, wg[:n], preferred_element_type=F32)
         + jnp.dot(hi, wg[n:], preferred_element_type=F32))
    u = (jnp.dot(lo, wu[:n], preferred_element_type=F32)
         + jnp.dot(hi, wu[n:], preferred_element_type=F32))
    h = (g * jax.nn.sigmoid(g)) * u
    return jnp.dot(h.astype(BF16), wd[...], preferred_element_type=F32)


def _moe_kernel(ie_ref, ib_ref, first_ref, slot_ref, ne_ref, nv_ref, xs_ref, wg_hbm, wu_hbm,
                wd_hbm, ys_ref, wg_f, wu_f, wd_f, wg_b, wu_b, wd_b, sem):
    del ib_ref
    i = pl.program_id(0)

    def fetch(e, slot):
        return [pltpu.make_async_copy(src.at[e], dst.at[slot], sem.at[slot, n])
                for n, (src, dst) in enumerate(((wg_hbm, wg_f), (wu_hbm, wu_f), (wd_hbm, wd_f)))]

    @pl.when(i == 0)
    def _():
        for cp in fetch(ie_ref[0], 0):
            cp.start()

    @pl.when(i < nv_ref[0])
    def _():
        for slot in range(2):
            @pl.when((first_ref[i] == 1) & (slot_ref[i] == slot))
            def _():
                for cp in fetch(ie_ref[i], slot):
                    cp.wait()
                wg_b[...] = wg_f[slot].astype(BF16)
                wu_b[...] = wu_f[slot].astype(BF16)
                wd_b[...] = wd_f[slot].astype(BF16)

                @pl.when(ne_ref[i] >= 0)
                def _():
                    for cp in fetch(ne_ref[i], 1 - slot):
                        cp.start()

        ys_ref[...] = _pack_halves(_swiglu_packed(xs_ref[...], wg_b, wu_b, wd_b))


def _moe(item_e, item_b, item_first, item_slot, item_next, n_valid, xs, wg, wu, wd):
    m_pad, dh = xs.shape
    _, d, f = wg.shape
    n_items = item_e.shape[0]
    blk = lambda i, ie, ib, fi, sl, ne, nv: (ib[i], 0)
    hbm = pl.BlockSpec(memory_space=pl.ANY)
    return pl.pallas_call(
        _moe_kernel,
        grid_spec=pltpu.PrefetchScalarGridSpec(
            num_scalar_prefetch=6,
            grid=(n_items,),
            in_specs=[pl.BlockSpec((MOE_ROWS, dh), blk), hbm, hbm, hbm],
            out_specs=pl.BlockSpec((MOE_ROWS, dh), blk),
            scratch_shapes=[pltpu.VMEM((2, d, f), F32), pltpu.VMEM((2, d, f), F32),
                            pltpu.VMEM((2, f, d), F32),
                            pltpu.VMEM((d, f), BF16), pltpu.VMEM((d, f), BF16),
                            pltpu.VMEM((f, d), BF16),
                            pltpu.SemaphoreType.DMA((2, 3))],
        ),
        out_shape=jax.ShapeDtypeStruct((m_pad, dh), jnp.uint32),
        compiler_params=_cparams(("arbitrary",)),
        name="moe",
    )(item_e, item_b, item_first, item_slot, item_next, n_valid, xs, wg, wu, wd)


def _combine_kernel(pos_ref, wt_ref, hf_ref, x1_ref, gt_ref, sg_ref, su_ref, sd_ref, ys_ref,
                    o_ref, gbuf, sem):
    tc = x1_ref.shape[0]

    def copy(t, k):
        return pltpu.make_async_copy(ys_ref.at[pl.ds(pos_ref[k, t], 1), :],
                                     gbuf.at[k, pl.ds(t, 1), :], sem)

    def start(t, c):
        for k in range(TOP_K):
            copy(t, k).start()
        return c

    def wait(t, c):
        for k in range(TOP_K):
            copy(t, k).wait()
        return c

    lax.fori_loop(0, tc, start, 0)
    y = _swiglu_packed(hf_ref[...], sg_ref, su_ref, sd_ref)
    lax.fori_loop(0, tc, wait, 0)
    wt = wt_ref[...]
    n = gbuf.shape[2]
    r_lo = jnp.zeros((tc, n), F32)
    r_hi = jnp.zeros((tc, n), F32)
    for k in range(TOP_K):
        lo, hi = _unpack_halves(gbuf[k])
        r_lo = r_lo + lo * wt[:, k:k + 1]
        r_hi = r_hi + hi * wt[:, k:k + 1]
    y = y + jnp.concatenate([r_lo, r_hi], axis=1)
    o_ref[...] = x1_ref[...] + gt_ref[...] * y


def _combine(pos3, wts_t, hfp, x1, gt, sg, su, sd, ys, tc):
    s, d = x1.shape
    f = sg.shape[1]
    row = pl.BlockSpec((tc, d), lambda i: (i, 0))
    return pl.pallas_call(
        _combine_kernel,
        grid=(s // tc,),
        in_specs=[pl.BlockSpec((None, TOP_K, tc), lambda i: (i, 0, 0), memory_space=pltpu.SMEM),
                  pl.BlockSpec((tc, TOP_K), lambda i: (i, 0)),
                  pl.BlockSpec((tc, d // 2), lambda i: (i, 0)), row,
                  pl.BlockSpec((1, d), lambda i: (0, 0)),
                  pl.BlockSpec((d, f), lambda i: (0, 0)),
                  pl.BlockSpec((d, f), lambda i: (0, 0)),
                  pl.BlockSpec((f, d), lambda i: (0, 0)),
                  pl.BlockSpec(memory_space=pl.ANY)],
        out_specs=row,
        out_shape=jax.ShapeDtypeStruct((s, d), F32),
        scratch_shapes=[pltpu.VMEM((TOP_K, tc, d // 2), jnp.uint32),
                        pltpu.SemaphoreType.DMA(())],
        compiler_params=_cparams(("arbitrary",)),
        name="combine",
    )(pos3, wts_t, hfp, x1, gt, sg, su, sd, ys)


def _tile(n, want):
    t = min(n, want)
    assert n % t == 0, (n, t)
    return t


def _layer(l, x, c_col, pos_row, p):
    s, d = x.shape
    lambda_init = 0.8 - 0.6 * math.exp(-0.3 * l)
    gqk, gv = GLA_HEADS * GLA_DK, GLA_HEADS * GLA_DV
    dqk, dvw = DIFF_HEADS * 2 * DIFF_DH, DIFF_HEADS * DIFF_DV
    lowrank = p["gla_w_a2"].shape[0]

    mod = _ada(c_col, p["w_ada"], p["b_ada"][None, :])
    sh_a, sc_a, gt_a, sh_f, sc_f, gt_f = [mod[:, j * d:(j + 1) * d] for j in range(6)]

    w_in = p["w_in"]
    o = 0
    cols = {}
    for name, wdt in (("gq", gqk), ("gk", gqk), ("gv", gv), ("ga", lowrank), ("gg", gv),
                      ("dq", dqk), ("dk", dqk), ("dv", dvw), ("mg", d), ("md", d)):
        cols[name] = w_in[:, o:o + wdt]
        o += wdt
    row_names = ("gv", "gg", "mg", "md", "gq")
    w_row = jnp.concatenate([cols[n] for n in row_names], axis=1).astype(BF16)
    col_of, o = {}, 0
    for n in row_names:
        col_of[n] = o
        o += cols[n].shape[1]
    t_names = ("dq", "dk", "dv", "gk")
    w_t = jnp.concatenate([cols[n] for n in t_names], axis=1).T.astype(BF16)
    row_of, o = {}, 0
    for n in t_names:
        row_of[n] = o
        o += cols[n].shape[1]
    w_ga = jnp.pad(cols["ga"], ((0, 0), (0, 128 - lowrank))).astype(BF16)

    g1 = p["norm1_g"][None, :]
    tm = _tile(s, 1024)
    proj, ga = _inproj(x, g1, sc_a, sh_a, w_row, w_ga, tm, 512)
    projt = _inproj_t(x, g1, sc_a, sh_a, w_t, tm, 512)

    wa2t = jnp.pad(p["gla_w_a2"].T, ((0, 0), (0, 128 - lowrank)))
    o_gla = _gla(proj, projt, ga, wa2t, p["gla_b_a"][:, None], p["gla_onorm_g"][None, :],
                 _tile(s, 512), col_of["gq"], col_of["gv"], col_of["gg"], row_of["gk"])

    invf = ROPE_THETA ** (-jnp.arange(0, ROT_DIM, 2, dtype=F32) / ROT_DIM)
    qat, qbt, kr, vte = _qkprep(projt, pos_row, invf[:, None], p["diff_qnorm_g"][:, None],
                                p["diff_knorm_g"][:, None], _tile(s, 512), row_of["dq"],
                                row_of["dk"], row_of["dv"])
    tq = _tile(s, 512)
    o_diff = _diffattn(qat, qbt, kr, vte, p["diff_lq1"][None, :], p["diff_lk1"][None, :],
                       p["diff_lq2"][None, :], p["diff_lk2"][None, :],
                       p["diff_subln_g"][:, None], lambda_init, tq, tq)

    x1, hf, hfp = _mergeout(o_gla, o_diff, proj, x, p["w_branch_gla"].astype(BF16),
                       p["w_branch_diff"].astype(BF16), p["w_out"].astype(BF16), gt_a,
                       p["norm2_g"][None, :], sc_f, sh_f, _tile(s, 512),
                       col_of["mg"], col_of["md"])

    e = p["w_router"].shape[1]
    idx, wts, rnk, cnt = _route(hf, p["w_router"].T, p["router_bias"][:, None], _tile(s, 512))

    counts = cnt[:, 0]
    pcounts = ((counts + MOE_ROWS - 1) // MOE_ROWS) * MOE_ROWS
    pend = jnp.cumsum(pcounts)
    pstart = pend - pcounts
    pos = _positions(idx, rnk, pstart[:, None], _tile(s, 512))
    n_items = (s * TOP_K) // MOE_ROWS + e
    n_valid = (pend[-1] // MOE_ROWS).astype(jnp.int32)
    item_b = jnp.minimum(jnp.arange(n_items, dtype=jnp.int32), n_valid - 1)
    item_e = jnp.minimum(jnp.sum(pend[None, :] <= (item_b * MOE_ROWS)[:, None], axis=1),
                         e - 1).astype(jnp.int32)

    td = _tile(s, 256)
    pos_d = pos.reshape(TOP_K, s // td, td).transpose(1, 0, 2)
    xs = _dispatch(pos_d, hfp, jnp.zeros((n_items * MOE_ROWS, d // 2), jnp.uint32), td)
    prev_e = jnp.concatenate([jnp.full((1,), -1, jnp.int32), item_e[:-1]])
    item_first = ((jnp.arange(n_items) < n_valid) & (item_e != prev_e)).astype(jnp.int32)
    item_slot = ((jnp.cumsum(item_first) - 1) % 2).astype(jnp.int32)
    cand = jnp.where(pcounts > 0, jnp.arange(e, dtype=jnp.int32), e)
    following = jnp.concatenate([lax.cummin(cand[::-1])[::-1][1:], jnp.full((1,), e, jnp.int32)])
    item_next = jnp.where(following[item_e] < e, following[item_e], -1).astype(jnp.int32)
    ys = _moe(item_e, item_b, item_first, item_slot, item_next, n_valid[None], xs,
              p["w_exp_gate"], p["w_exp_up"], p["w_exp_down"])
    tc = _tile(s, 128)
    pos_c = pos.reshape(TOP_K, s // tc, tc).transpose(1, 0, 2)
    return _combine(pos_c, wts.T, hfp, x1, gt_f, p["w_sh_gate"].astype(BF16),
                    p["w_sh_up"].astype(BF16), p["w_sh_down"].astype(BF16), ys, tc)


_LAYER_PARAMS = ("w_ada", "b_ada", "norm1_g", "w_in", "gla_w_a2", "gla_b_a", "gla_onorm_g",
                 "diff_qnorm_g", "diff_knorm_g", "diff_lq1", "diff_lk1", "diff_lq2", "diff_lk2",
                 "diff_subln_g", "w_branch_gla", "w_branch_diff", "w_out", "norm2_g", "w_router",
                 "router_bias", "w_exp_gate", "w_exp_up", "w_exp_down", "w_sh_gate", "w_sh_up",
                 "w_sh_down")


def kernel(x, c, positions, w_ada, b_ada, norm1_g, w_in, gla_w_a2, gla_b_a, gla_onorm_g, diff_qnorm_g, diff_knorm_g, diff_lq1, diff_lk1, diff_lq2, diff_lk2, diff_subln_g, w_branch_gla, w_branch_diff, w_out, norm2_g, w_router, router_bias, w_exp_gate, w_exp_up, w_exp_down, w_sh_gate, w_sh_up, w_sh_down):
    stacked = dict(zip(_LAYER_PARAMS, (
        w_ada, b_ada, norm1_g, w_in, gla_w_a2, gla_b_a, gla_onorm_g, diff_qnorm_g, diff_knorm_g,
        diff_lq1, diff_lk1, diff_lq2, diff_lk2, diff_subln_g, w_branch_gla, w_branch_diff, w_out,
        norm2_g, w_router, router_bias, w_exp_gate, w_exp_up, w_exp_down, w_sh_gate, w_sh_up,
        w_sh_down)))
    b, s, d = x.shape
    assert b == 1, "single-sequence kernel"
    xl = x[0]
    c_col = c[0][:, None]
    pos_row = positions.astype(jnp.int32)
    for l in range(w_ada.shape[0]):
        xl = _layer(l, xl, c_col, pos_row, {k: v[l] for k, v in stacked.items()})
    return xl[None]
```

```python
import functools
import math

import jax
import jax.numpy as jnp
from jax import lax
from jax.experimental import pallas as pl
from jax.experimental.pallas import tpu as pltpu
from jax.experimental.pallas import tpu_sc as plsc

CHUNK = 64
EPS = 1e-6
GLA_HEADS = 4
GLA_DK = 128
GLA_DV = 256
GLA_TAU = 16.0
DIFF_HEADS = 8
DIFF_DH = 64
DIFF_DV = 2 * DIFF_DH
ROPE_THETA = 500000.0
ROT_DIM = DIFF_DH // 4
N_GROUPS = 8
TOPK_GROUPS = 4
TOP_K = 8
ROUTED_SCALE = 2.5

MOE_ROWS = 256
SC_GATHER_WINDOW = 64
VMEM_LIMIT = 56 * 1024 * 1024
NEG_BIG = -1e30
LOG2E = 1.4426950408889634
HIGHEST = lax.Precision.HIGHEST
F32 = jnp.float32
BF16 = jnp.bfloat16


def _cparams(sem):
    return pltpu.CompilerParams(dimension_semantics=sem, vmem_limit_bytes=VMEM_LIMIT)


def _nt_dot(a, b, precision=None):
    return lax.dot_general(a, b, (((1,), (1,)), ((), ())), precision=precision,
                           preferred_element_type=F32)


def _pack_halves(x):
    n = x.shape[1] // 2
    lo = pltpu.bitcast(x[:, :n].astype(BF16).astype(F32), jnp.uint32) >> 16
    hi = pltpu.bitcast(x[:, n:].astype(BF16).astype(F32), jnp.uint32) & jnp.uint32(0xFFFF0000)
    return lo | hi


def _unpack_halves(w):
    return (pltpu.bitcast(w << 16, F32), pltpu.bitcast(w & jnp.uint32(0xFFFF0000), F32))


def _rms_mod(x, g, sc, sh):
    xn = x * lax.rsqrt(jnp.mean(x * x, axis=-1, keepdims=True) + EPS)
    return (xn * g) * (1.0 + sc) + sh


def _ada_kernel(c_ref, w_ref, b_ref, o_ref):
    c = c_ref[...]
    ca = c * jax.nn.sigmoid(c)
    o_ref[...] = jnp.sum(ca * w_ref[...], axis=0, keepdims=True) + b_ref[...]


def _ada(c_col, w, b):
    d, n = w.shape
    tn = min(1024, n)
    return pl.pallas_call(
        _ada_kernel,
        grid=(n // tn,),
        in_specs=[pl.BlockSpec((d, 1), lambda j: (0, 0)),
                  pl.BlockSpec((d, tn), lambda j: (0, j)),
                  pl.BlockSpec((1, tn), lambda j: (0, j))],
        out_specs=pl.BlockSpec((1, tn), lambda j: (0, j)),
        out_shape=jax.ShapeDtypeStruct((1, n), F32),
        compiler_params=_cparams(("arbitrary",)),
        name="ada",
    )(c_col, w, b)


def _inproj_kernel(x_ref, g_ref, sc_ref, sh_ref, w_ref, wga_ref, o_ref, ga_ref, h_scr):
    @pl.when(pl.program_id(1) == 0)
    def _():
        h = _rms_mod(x_ref[...], g_ref[...], sc_ref[...], sh_ref[...]).astype(BF16)
        h_scr[...] = h
        ga_ref[...] = jnp.dot(h, wga_ref[...], preferred_element_type=F32)

    o_ref[...] = jnp.dot(h_scr[...], w_ref[...], preferred_element_type=F32).astype(BF16)


def _inproj(x, g, sc, sh, w, wga, tm, tn):
    s, d = x.shape
    n = w.shape[1]
    vec = pl.BlockSpec((1, d), lambda i, j: (0, 0))
    return pl.pallas_call(
        _inproj_kernel,
        grid=(s // tm, n // tn),
        in_specs=[pl.BlockSpec((tm, d), lambda i, j: (i, 0)), vec, vec, vec,
                  pl.BlockSpec((d, tn), lambda i, j: (0, j)),
                  pl.BlockSpec((d, 128), lambda i, j: (0, 0))],
        out_specs=[pl.BlockSpec((tm, tn), lambda i, j: (i, j)),
                   pl.BlockSpec((tm, 128), lambda i, j: (i, 0))],
        out_shape=[jax.ShapeDtypeStruct((s, n), BF16), jax.ShapeDtypeStruct((s, 128), F32)],
        scratch_shapes=[pltpu.VMEM((tm, d), BF16)],
        compiler_params=_cparams(("arbitrary", "arbitrary")),
        name="inproj",
    )(x, g, sc, sh, w, wga)


def _inproj_t_kernel(x_ref, g_ref, sc_ref, sh_ref, wt_ref, o_ref, h_scr):
    @pl.when(pl.program_id(1) == 0)
    def _():
        h_scr[...] = _rms_mod(x_ref[...], g_ref[...], sc_ref[...], sh_ref[...]).astype(BF16)

    o_ref[...] = _nt_dot(wt_ref[...], h_scr[...]).astype(BF16)


def _inproj_t(x, g, sc, sh, wt, tm, tn):
    s, d = x.shape
    n = wt.shape[0]
    vec = pl.BlockSpec((1, d), lambda i, j: (0, 0))
    return pl.pallas_call(
        _inproj_t_kernel,
        grid=(s // tm, n // tn),
        in_specs=[pl.BlockSpec((tm, d), lambda i, j: (i, 0)), vec, vec, vec,
                  pl.BlockSpec((tn, d), lambda i, j: (j, 0))],
        out_specs=pl.BlockSpec((tn, tm), lambda i, j: (j, i)),
        out_shape=jax.ShapeDtypeStruct((n, s), BF16),
        scratch_shapes=[pltpu.VMEM((tm, d), BF16)],
        compiler_params=_cparams(("arbitrary", "arbitrary")),
        name="inproj_t",
    )(x, g, sc, sh, wt)


def _gla_kernel(q_ref, kt_ref, v_ref, gg_ref, ga_ref, wa2t_ref, ba_ref, on_ref, o_ref,
                state_ref, o_scr):
    tt = q_ref.shape[0]
    nchunk = tt // CHUNK

    @pl.when(pl.program_id(0) == 0)
    def _():
        state_ref[...] = jnp.zeros_like(state_ref)

    zt = _nt_dot(wa2t_ref[...], ga_ref[...], precision=HIGHEST) + ba_ref[...]
    lat = (jnp.minimum(zt, 0.0) - jnp.log1p(jnp.exp(-jnp.abs(zt)))) * (1.0 / GLA_TAU)
    row = lax.broadcasted_iota(jnp.int32, (tt, tt), 0)
    col = lax.broadcasted_iota(jnp.int32, (tt, tt), 1)
    same = (row // CHUNK) == (col // CHUNK)
    incl = jnp.where(same & (row <= col), 1.0, 0.0).astype(F32)
    full = jnp.where(same, 1.0, 0.0).astype(F32)
    cumt = jnp.dot(lat, incl, precision=HIGHEST, preferred_element_type=F32)
    tott = jnp.dot(lat, full, precision=HIGHEST, preferred_element_type=F32)
    kdt = kt_ref[...].astype(F32) * jnp.exp(tott - cumt)
    dec = jnp.exp(tott)

    lane = lax.broadcasted_iota(jnp.int32, (GLA_DK, 2 * CHUNK), 1)
    for c in range(nchunk):
        pair = (c // 2) * 2 * CHUNK
        if nchunk > 1:
            keep = (lane // CHUNK) == (c % 2)
        for h in range(GLA_HEADS):
            rows = slice(h * GLA_DK, (h + 1) * GLA_DK)
            vcols = slice(h * GLA_DV, (h + 1) * GLA_DV)
            if nchunk > 1:
                a = jnp.where(keep, kdt[rows, pair:pair + 2 * CHUNK], 0.0).astype(BF16)
                vp = v_ref[pair:pair + 2 * CHUNK, vcols]
            else:
                a = kdt[rows, :].astype(BF16)
                vp = v_ref[:, vcols]
            upd = jnp.dot(a, vp, preferred_element_type=F32)
            dcol = dec[rows, c * CHUNK:c * CHUNK + 1]
            st = state_ref[h] * dcol + upd
            state_ref[h] = st
            qc = q_ref[c * CHUNK:(c + 1) * CHUNK, rows]
            o_scr[c * CHUNK:(c + 1) * CHUNK, vcols] = jnp.dot(
                qc, st.astype(BF16), preferred_element_type=F32)

    for h in range(GLA_HEADS):
        vcols = slice(h * GLA_DV, (h + 1) * GLA_DV)
        o = o_scr[:, vcols] * (GLA_DK ** -0.5)
        o = o * lax.rsqrt(jnp.mean(o * o, axis=-1, keepdims=True) + EPS) * on_ref[...]
        g = gg_ref[:, vcols].astype(F32)
        o_ref[:, vcols] = (o * (g * jax.nn.sigmoid(g))).astype(BF16)


def _gla(proj, projt, ga, wa2t, ba_col, on_g, tt, col_q, col_v, col_g, row_k):
    s = proj.shape[0]
    qk = GLA_HEADS * GLA_DK
    vw = GLA_HEADS * GLA_DV
    return pl.pallas_call(
        _gla_kernel,
        grid=(s // tt,),
        in_specs=[pl.BlockSpec((tt, qk), lambda i: (i, col_q // qk)),
                  pl.BlockSpec((qk, tt), lambda i: (row_k // qk, i)),
                  pl.BlockSpec((tt, vw), lambda i: (i, col_v // vw)),
                  pl.BlockSpec((tt, vw), lambda i: (i, col_g // vw)),
                  pl.BlockSpec((tt, 128), lambda i: (i, 0)),
                  pl.BlockSpec((qk, 128), lambda i: (0, 0)),
                  pl.BlockSpec((qk, 1), lambda i: (0, 0)),
                  pl.BlockSpec((1, GLA_DV), lambda i: (0, 0))],
        out_specs=pl.BlockSpec((tt, vw), lambda i: (i, 0)),
        out_shape=jax.ShapeDtypeStruct((s, vw), BF16),
        scratch_shapes=[pltpu.VMEM((GLA_HEADS, GLA_DK, GLA_DV), F32),
                        pltpu.VMEM((tt, vw), F32)],
        compiler_params=_cparams(("arbitrary",)),
        name="gla",
    )(proj, projt, proj, proj, ga, wa2t, ba_col, on_g)


def _qknorm_rope_t(xt, g_col, cos, sin):
    n, tm = xt.shape
    x3 = xt.reshape(n // DIFF_DH, DIFF_DH, tm)
    r = lax.rsqrt(jnp.mean(x3 * x3, axis=1, keepdims=True) + EPS)
    y = x3 * r * g_col[None]
    half = ROT_DIM // 2
    y1, y2, rest = y[:, :half], y[:, half:ROT_DIM], y[:, ROT_DIM:]
    o1 = y1 * cos[None] - y2 * sin[None]
    o2 = y2 * cos[None] + y1 * sin[None]
    return jnp.concatenate([o1, o2, rest], axis=1)


def _qkprep_kernel(qt_ref, kt_ref, vt_ref, pos_ref, invf_ref, qg_ref, kg_ref, qa_ref, qb_ref,
                   ko_ref, ve_ref):
    tm = qt_ref.shape[1]
    v3 = vt_ref[...].reshape(DIFF_HEADS, DIFF_DV, tm)
    ones = jnp.ones((DIFF_HEADS, ATT_SUM_ROWS, tm), BF16)
    ve_ref[...] = jnp.concatenate([v3, ones], axis=1).reshape(-1, tm)
    ang = pos_ref[...].astype(F32) * invf_ref[...]
    cos, sin = jnp.cos(ang), jnp.sin(ang)
    k3 = _qknorm_rope_t(kt_ref[...].astype(F32), kg_ref[...], cos, sin)
    ko_ref[...] = k3.reshape(-1, tm).T.astype(BF16)
    q3 = _qknorm_rope_t(qt_ref[...].astype(F32), qg_ref[...], cos, sin) * (
        DIFF_DH ** -0.5 * LOG2E)
    seg = lax.broadcasted_iota(jnp.int32, q3.shape, 0)
    qa_ref[...] = jnp.where(seg % 2 == 0, q3, 0.0).reshape(-1, tm).astype(BF16)
    qb_ref[...] = jnp.where(seg % 2 == 1, q3, 0.0).reshape(-1, tm).astype(BF16)


def _qkprep(projt, pos_row, invf_col, qg_col, kg_col, tm, row_q, row_k, row_v):
    s = projt.shape[1]
    n = DIFF_HEADS * 2 * DIFF_DH
    ne = DIFF_HEADS * (DIFF_DV + ATT_SUM_ROWS)
    col = pl.BlockSpec((DIFF_DH, 1), lambda i: (0, 0))
    return pl.pallas_call(
        _qkprep_kernel,
        grid=(s // tm,),
        in_specs=[pl.BlockSpec((n, tm), lambda i: (row_q // n, i)),
                  pl.BlockSpec((n, tm), lambda i: (row_k // n, i)),
                  pl.BlockSpec((n, tm), lambda i: (row_v // n, i)),
                  pl.BlockSpec((1, tm), lambda i: (0, i)),
                  pl.BlockSpec((ROT_DIM // 2, 1), lambda i: (0, 0)), col, col],
        out_specs=[pl.BlockSpec((n, tm), lambda i: (0, i)),
                   pl.BlockSpec((n, tm), lambda i: (0, i)),
                   pl.BlockSpec((tm, n), lambda i: (i, 0)),
                   pl.BlockSpec((ne, tm), lambda i: (0, i))],
        out_shape=[jax.ShapeDtypeStruct((n, s), BF16), jax.ShapeDtypeStruct((n, s), BF16),
                   jax.ShapeDtypeStruct((s, n), BF16), jax.ShapeDtypeStruct((ne, s), BF16)],
        compiler_params=_cparams(("arbitrary",)),
        name="qkprep",
    )(projt, projt, projt, pos_row, invf_col, qg_col, kg_col)


ATT_COLS = 256
ATT_LOOKAHEAD = 3
ATT_SUM_ROWS = 16


def _diffattn_kernel(qa_ref, qb_ref, k_ref, vt_ref, lq1_ref, lk1_ref, lq2_ref, lk2_ref, sg_ref,
                     o_ref, *scr, lambda_init, tk, cols):
    tq = qa_ref.shape[1]
    nblk = 2 * tq // cols
    q_scr, m_scr, acc_scr = (scr[b * nblk:(b + 1) * nblk] for b in range(3))
    i = pl.program_id(1)
    for c in range(nblk):
        src = qa_ref if c * cols < tq else qb_ref
        off = (c * cols) % tq
        q_scr[c][...] = src[:, off:off + cols]
        m_scr[c][...] = jnp.full_like(m_scr[c], NEG_BIG)
        acc_scr[c][...] = jnp.zeros_like(acc_scr[c])

    def scores(j, c):
        start = pl.multiple_of(j * tk, tk)
        return jnp.dot(k_ref[pl.ds(start, tk), :], q_scr[c][...], preferred_element_type=F32)

    def steps(tiles, masked):
        items = [(j, c) for j in tiles for c in range(nblk)]
        pending = [scores(*it) for it in items[:ATT_LOOKAHEAD]]
        for n, (j, c) in enumerate(items):
            s = pending.pop(0)
            if n + ATT_LOOKAHEAD < len(items):
                pending.append(scores(*items[n + ATT_LOOKAHEAD]))
            start = pl.multiple_of(j * tk, tk)
            if masked:
                krow = lax.broadcasted_iota(jnp.int32, (tk, cols), 0)
                qcol = lax.broadcasted_iota(jnp.int32, (tk, cols), 1)
                qpos = i * tq + (c * cols) % tq + qcol
                s = jnp.where((start + krow) // CHUNK <= qpos // CHUNK, s, NEG_BIG)
            m_prev = m_scr[c][...]
            m_new = jnp.maximum(m_prev, jnp.max(s, axis=0, keepdims=True))
            alpha = jnp.exp2(m_prev - m_new)
            p = jnp.exp2((s - m_new).astype(BF16))
            acc_scr[c][...] = alpha * acc_scr[c][...] + jnp.dot(
                vt_ref[:, pl.ds(start, tk)], p, preferred_element_type=F32)
            m_scr[c][...] = m_new

    n_full = (i * tq) // tk
    lax.fori_loop(0, n_full // 2, lambda t, c: (steps((2 * t, 2 * t + 1), False), c)[1], 0)

    @pl.when(n_full % 2 == 1)
    def _():
        steps((n_full - 1,), False)

    n_all = ((i + 1) * tq + tk - 1) // tk
    lax.fori_loop(n_full, n_all, lambda j, c: (steps((j,), True), c)[1], 0)

    o = jnp.concatenate([acc_scr[c][:DIFF_DV] / acc_scr[c][DIFF_DV:DIFF_DV + 1]
                         for c in range(nblk)], axis=1)
    lam = (jnp.exp(jnp.sum(lq1_ref[...] * lk1_ref[...]))
           - jnp.exp(jnp.sum(lq2_ref[...] * lk2_ref[...])) + lambda_init)
    o = o[:, :tq] - lam * o[:, tq:]
    o = o * lax.rsqrt(jnp.mean(o * o, axis=0, keepdims=True) + EPS) * sg_ref[...]
    o_ref[...] = (o * (1.0 - lambda_init)).T.astype(BF16)


def _diffattn(qat, qbt, kr, vte, lq1, lk1, lq2, lk2, sg_col, lambda_init, tq, tk):
    s = kr.shape[0]
    hd = 2 * DIFF_DH
    vec = pl.BlockSpec((1, DIFF_DH), lambda h, i: (0, 0))
    cols = min(ATT_COLS, tq)
    nblk = 2 * tq // cols
    kern = functools.partial(_diffattn_kernel, lambda_init=lambda_init, tk=tk, cols=cols)
    dve = DIFF_DV + ATT_SUM_ROWS
    scratch = ([pltpu.VMEM((hd, cols), BF16)] * nblk + [pltpu.VMEM((1, cols), F32)] * nblk
               + [pltpu.VMEM((dve, cols), F32)] * nblk)
    return pl.pallas_call(
        kern,
        grid=(DIFF_HEADS, s // tq),
        in_specs=[pl.BlockSpec((hd, tq), lambda h, i: (h, i)),
                  pl.BlockSpec((hd, tq), lambda h, i: (h, i)),
                  pl.BlockSpec((s, hd), lambda h, i: (0, h)),
                  pl.BlockSpec((dve, s), lambda h, i: (h, 0)),
                  vec, vec, vec, vec,
                  pl.BlockSpec((DIFF_DV, 1), lambda h, i: (0, 0))],
        out_specs=pl.BlockSpec((tq, DIFF_DV), lambda h, i: (i, h)),
        out_shape=jax.ShapeDtypeStruct((s, DIFF_HEADS * DIFF_DV), BF16),
        scratch_shapes=scratch,
        compiler_params=_cparams(("arbitrary", "arbitrary")),
        name="diffattn",
    )(qat, qbt, kr, vte, lq1, lk1, lq2, lk2, sg_col)


def _mergeout_kernel(og_ref, od_ref, mg_ref, md_ref, x_ref, wbg_ref, wbd_ref, wo_ref, gt_ref,
                     g2_ref, sc_ref, sh_ref, x1_ref, hf_ref, hfp_ref):
    bg = jnp.dot(og_ref[...], wbg_ref[...], preferred_element_type=F32)
    bd = jnp.dot(od_ref[...], wbd_ref[...], preferred_element_type=F32)
    merged = (jax.nn.sigmoid(mg_ref[...].astype(F32)) * bg
              + jax.nn.sigmoid(md_ref[...].astype(F32)) * bd)
    x1 = x_ref[...] + gt_ref[...] * jnp.dot(merged.astype(BF16), wo_ref[...],
                                             preferred_element_type=F32)
    x1_ref[...] = x1
    hf = _rms_mod(x1, g2_ref[...], sc_ref[...], sh_ref[...])
    hf_ref[...] = hf
    hfp_ref[...] = _pack_halves(hf)


def _mergeout(og, od, proj, x, wbg, wbd, wo, gt, g2, sc, sh, tm, col_mg, col_md):
    s, d = x.shape
    vec = pl.BlockSpec((1, d), lambda i: (0, 0))
    wspec = pl.BlockSpec((d, d), lambda i: (0, 0))
    row = pl.BlockSpec((tm, d), lambda i: (i, 0))
    return pl.pallas_call(
        _mergeout_kernel,
        grid=(s // tm,),
        in_specs=[row, row,
                  pl.BlockSpec((tm, d), lambda i: (i, col_mg // d)),
                  pl.BlockSpec((tm, d), lambda i: (i, col_md // d)),
                  row, wspec, wspec, wspec, vec, vec, vec, vec],
        out_specs=[row, row, pl.BlockSpec((tm, d // 2), lambda i: (i, 0))],
        out_shape=[jax.ShapeDtypeStruct((s, d), F32), jax.ShapeDtypeStruct((s, d), F32),
                   jax.ShapeDtypeStruct((s, d // 2), jnp.uint32)],
        compiler_params=_cparams(("arbitrary",)),
        name="mergeout",
    )(og, od, proj, proj, x, wbg, wbd, wo, gt, g2, sc, sh)


def _route_kernel(hf_ref, wrt_ref, bias_ref, idx_ref, wts_ref, rnk_ref, cnt_ref, run_scr):
    tr = hf_ref.shape[0]
    e = wrt_ref.shape[0]
    gsz = e // N_GROUPS

    @pl.when(pl.program_id(0) == 0)
    def _():
        run_scr[...] = jnp.zeros_like(run_scr)

    logits = _nt_dot(wrt_ref[...], hf_ref[...], precision=HIGHEST)
    scores = jax.nn.sigmoid(logits)
    biased = scores + bias_ref[...]
    g3 = biased.reshape(N_GROUPS, gsz, tr)
    m1 = jnp.max(g3, axis=1, keepdims=True)
    n_top = jnp.sum(jnp.where(g3 == m1, 1.0, 0.0), axis=1, keepdims=True)
    m2 = jnp.max(jnp.where(g3 < m1, g3, -jnp.inf), axis=1, keepdims=True)
    gs = (m1 + jnp.where(n_top >= 2.0, m1, m2)).reshape(N_GROUPS, tr)
    gi = lax.broadcasted_iota(jnp.int32, (N_GROUPS, tr), 0)
    beaten = jnp.zeros((N_GROUPS, tr), F32)
    for g in range(N_GROUPS):
        other = gs[g:g + 1, :]
        beaten = beaten + jnp.where((other > gs) | ((other == gs) & (g < gi)), 1.0, 0.0)
    gsel = (beaten < float(TOPK_GROUPS)).reshape(N_GROUPS, 1, tr)
    masked = jnp.where(gsel, g3, -jnp.inf).reshape(e, tr)

    ids = lax.broadcasted_iota(jnp.int32, (e, tr), 0)
    chosen = jnp.zeros((e, tr), F32)
    sel_idx, sel_score = [], []
    for _ in range(TOP_K):
        mx = jnp.max(masked, axis=0, keepdims=True)
        ix = jnp.min(jnp.where(masked == mx, ids, e), axis=0, keepdims=True)
        hit = ids == ix
        sel_idx.append(ix)
        sel_score.append(jnp.sum(jnp.where(hit, scores, 0.0), axis=0, keepdims=True))
        chosen = jnp.where(hit, 1.0, chosen)
        masked = jnp.where(hit, -jnp.inf, masked)
    idx = jnp.concatenate(sel_idx, axis=0)
    sc = jnp.concatenate(sel_score, axis=0)
    idx_ref[...] = idx
    wts_ref[...] = sc / jnp.sum(sc, axis=0, keepdims=True) * ROUTED_SCALE

    row = lax.broadcasted_iota(jnp.int32, (tr, tr), 0)
    col = lax.broadcasted_iota(jnp.int32, (tr, tr), 1)
    before = jnp.where(row < col, 1.0, 0.0).astype(BF16)
    prior = jnp.dot(chosen.astype(BF16), before, preferred_element_type=F32) + run_scr[:, 0:1]
    rnk_ref[...] = jnp.concatenate(
        [jnp.sum(jnp.where(ids == sel_idx[k], prior, 0.0), axis=0, keepdims=True)
         for k in range(TOP_K)], axis=0).astype(jnp.int32)
    run_scr[...] = run_scr[...] + jnp.sum(chosen, axis=1, keepdims=True)
    cnt_ref[...] = run_scr[...].astype(jnp.int32)


def _route(hf, wrt, bias_col, tr):
    s, d = hf.shape
    e = wrt.shape[0]
    tok = pl.BlockSpec((TOP_K, tr), lambda i: (0, i))
    return pl.pallas_call(
        _route_kernel,
        grid=(s // tr,),
        in_specs=[pl.BlockSpec((tr, d), lambda i: (i, 0)),
                  pl.BlockSpec((e, d), lambda i: (0, 0)),
                  pl.BlockSpec((e, 1), lambda i: (0, 0))],
        out_specs=[tok, tok, tok, pl.BlockSpec((e, 128), lambda i: (0, 0))],
        out_shape=[jax.ShapeDtypeStruct((TOP_K, s), jnp.int32),
                   jax.ShapeDtypeStruct((TOP_K, s), F32),
                   jax.ShapeDtypeStruct((TOP_K, s), jnp.int32),
                   jax.ShapeDtypeStruct((e, 128), jnp.int32)],
        scratch_shapes=[pltpu.VMEM((e, 128), F32)],
        compiler_params=_cparams(("arbitrary",)),
        name="route",
    )(hf, wrt, bias_col)


def _positions_kernel(idx_ref, rnk_ref, pstart_ref, pos_ref):
    e = pstart_ref.shape[0]
    ts = idx_ref.shape[1]
    ids = lax.broadcasted_iota(jnp.int32, (e, ts), 0)
    idx = idx_ref[...]
    pos_ref[...] = rnk_ref[...] + jnp.concatenate(
        [jnp.sum(jnp.where(ids == idx[k:k + 1, :], pstart_ref[...], 0), axis=0, keepdims=True)
         for k in range(TOP_K)], axis=0)


def _positions(idx, rnk, pstart_col, ts):
    s = idx.shape[1]
    e = pstart_col.shape[0]
    tok = pl.BlockSpec((TOP_K, ts), lambda i: (0, i))
    return pl.pallas_call(
        _positions_kernel,
        grid=(s // ts,),
        in_specs=[tok, tok, pl.BlockSpec((e, 1), lambda i: (0, 0))],
        out_specs=tok,
        out_shape=jax.ShapeDtypeStruct((TOP_K, s), jnp.int32),
        compiler_params=_cparams(("arbitrary",)),
        name="positions",
    )(idx, rnk, pstart_col)


def _row_copy(src_ref, src_row, dst_ref, dst_row, sem):
    return pltpu.make_async_copy(src_ref.at[pl.ds(src_row, 1), :],
                                 dst_ref.at[pl.ds(dst_row, 1), :], sem)


def _dispatch_kernel(pos_ref, hf_ref, xs_in_ref, xs_ref, sem):
    del xs_in_ref
    td = hf_ref.shape[0]

    def start(t, c):
        for k in range(TOP_K):
            _row_copy(hf_ref, t, xs_ref, pos_ref[k, t], sem).start()
        return c

    def wait(t, c):
        for k in range(TOP_K):
            _row_copy(hf_ref, t, xs_ref, pos_ref[k, t], sem).wait()
        return c

    lax.fori_loop(0, td, start, 0)
    lax.fori_loop(0, td, wait, 0)


def _dispatch(pos3, hf, xs_zero, td):
    s, d = hf.shape
    return pl.pallas_call(
        _dispatch_kernel,
        grid=(s // td,),
        in_specs=[pl.BlockSpec((None, TOP_K, td), lambda i: (i, 0, 0), memory_space=pltpu.SMEM),
                  pl.BlockSpec((td, d), lambda i: (i, 0)),
                  pl.BlockSpec(memory_space=pl.ANY)],
        out_specs=pl.BlockSpec(memory_space=pl.ANY),
        out_shape=jax.ShapeDtypeStruct(xs_zero.shape, xs_zero.dtype),
        scratch_shapes=[pltpu.SemaphoreType.DMA(())],
        input_output_aliases={2: 0},
        compiler_params=_cparams(("arbitrary",)),
        name="dispatch",
    )(pos3, hf, xs_zero)


def _swiglu_packed(xp, wg, wu, wd):
    lo, hi = _unpack_halves(xp)
    lo, hi = lo.astype(BF16), hi.astype(BF16)
    n = lo.shape[1]
    g = (jnp.dot(lo, wg[:n], preferred_element_type=F32)
         + jnp.dot(hi, wg[n:], preferred_element_type=F32))
    u = (jnp.dot(lo, wu[:n], preferred_element_type=F32)
         + jnp.dot(hi, wu[n:], preferred_element_type=F32))
    h = (g * jax.nn.sigmoid(g)) * u
    return jnp.dot(h.astype(BF16), wd[...], preferred_element_type=F32)


def _moe_kernel(ie_ref, ib_ref, first_ref, slot_ref, ne_ref, nv_ref, xs_ref, wg_hbm, wu_hbm,
                wd_hbm, ys_ref, wg_f, wu_f, wd_f, wg_b, wu_b, wd_b, sem):
    del ib_ref
    i = pl.program_id(0)

    def fetch(e, slot):
        return [pltpu.make_async_copy(src.at[e], dst.at[slot], sem.at[slot, n])
                for n, (src, dst) in enumerate(((wg_hbm, wg_f), (wu_hbm, wu_f), (wd_hbm, wd_f)))]

    @pl.when(i == 0)
    def _():
        for cp in fetch(ie_ref[0], 0):
            cp.start()

    @pl.when(i < nv_ref[0])
    def _():
        for slot in range(2):
            @pl.when((first_ref[i] == 1) & (slot_ref[i] == slot))
            def _():
                for cp in fetch(ie_ref[i], slot):
                    cp.wait()
                wg_b[...] = wg_f[slot].astype(BF16)
                wu_b[...] = wu_f[slot].astype(BF16)
                wd_b[...] = wd_f[slot].astype(BF16)

                @pl.when(ne_ref[i] >= 0)
                def _():
                    for cp in fetch(ne_ref[i], 1 - slot):
                        cp.start()

        ys_ref[...] = _pack_halves(_swiglu_packed(xs_ref[...], wg_b, wu_b, wd_b))


def _moe(item_e, item_b, item_first, item_slot, item_next, n_valid, xs, wg, wu, wd):
    m_pad, dh = xs.shape
    _, d, f = wg.shape
    n_items = item_e.shape[0]
    blk = lambda i, ie, ib, fi, sl, ne, nv: (ib[i], 0)
    hbm = pl.BlockSpec(memory_space=pl.ANY)
    return pl.pallas_call(
        _moe_kernel,
        grid_spec=pltpu.PrefetchScalarGridSpec(
            num_scalar_prefetch=6,
            grid=(n_items,),
            in_specs=[pl.BlockSpec((MOE_ROWS, dh), blk), hbm, hbm, hbm],
            out_specs=pl.BlockSpec((MOE_ROWS, dh), blk),
            scratch_shapes=[pltpu.VMEM((2, d, f), F32), pltpu.VMEM((2, d, f), F32),
                            pltpu.VMEM((2, f, d), F32),
                            pltpu.VMEM((d, f), BF16), pltpu.VMEM((d, f), BF16),
                            pltpu.VMEM((f, d), BF16),
                            pltpu.SemaphoreType.DMA((2, 3))],
        ),
        out_shape=jax.ShapeDtypeStruct((m_pad, dh), jnp.uint32),
        compiler_params=_cparams(("arbitrary",)),
        name="moe",
    )(item_e, item_b, item_first, item_slot, item_next, n_valid, xs, wg, wu, wd)


def _sc_gather_rows(table, idx_row):
    m = idx_row.shape[1]
    w = table.shape[1]
    idx_row = idx_row.reshape(m // SC_GATHER_WINDOW, SC_GATHER_WINDOW)
    mesh = plsc.VectorSubcoreMesh(core_axis_name="c", subcore_axis_name="s")

    @functools.partial(pl.kernel, mesh=mesh,
                       out_type=jax.ShapeDtypeStruct((m, w), table.dtype))
    def gather(table_hbm, idx_hbm, out_hbm):
        def body(idx_vmem, out_vmem):
            pltpu.sync_copy(table_hbm.at[idx_vmem.at[0]], out_vmem)

        pltpu.emit_pipeline(
            body,
            grid=(m // SC_GATHER_WINDOW,),
            in_specs=[pl.BlockSpec((1, SC_GATHER_WINDOW), lambda i: (i, 0))],
            out_specs=[pl.BlockSpec((SC_GATHER_WINDOW, w), lambda i: (i, 0))],
            core_axis_name=("c", "s"),
            dimension_semantics=(pltpu.PARALLEL,),
        )(idx_hbm, out_hbm)

    return gather(table, idx_row)


def _combine_kernel(wt_ref, hf_ref, x1_ref, gt_ref, sg_ref, su_ref, sd_ref, g_ref, o_ref):
    tc = x1_ref.shape[0]
    y = _swiglu_packed(hf_ref[...], sg_ref, su_ref, sd_ref)
    wt = wt_ref[...]
    n = g_ref.shape[2]
    r_lo = jnp.zeros((tc, n), F32)
    r_hi = jnp.zeros((tc, n), F32)
    for k in range(TOP_K):
        lo, hi = _unpack_halves(g_ref[k])
        r_lo = r_lo + lo * wt[:, k:k + 1]
        r_hi = r_hi + hi * wt[:, k:k + 1]
    y = y + jnp.concatenate([r_lo, r_hi], axis=1)
    o_ref[...] = x1_ref[...] + gt_ref[...] * y


def _combine(wts_t, hfp, x1, gt, sg, su, sd, gathered, tc):
    s, d = x1.shape
    f = sg.shape[1]
    row = pl.BlockSpec((tc, d), lambda i: (i, 0))
    return pl.pallas_call(
        _combine_kernel,
        grid=(s // tc,),
        in_specs=[pl.BlockSpec((tc, TOP_K), lambda i: (i, 0)),
                  pl.BlockSpec((tc, d // 2), lambda i: (i, 0)), row,
                  pl.BlockSpec((1, d), lambda i: (0, 0)),
                  pl.BlockSpec((d, f), lambda i: (0, 0)),
                  pl.BlockSpec((d, f), lambda i: (0, 0)),
                  pl.BlockSpec((f, d), lambda i: (0, 0)),
                  pl.BlockSpec((TOP_K, tc, d // 2), lambda i: (0, i, 0))],
        out_specs=row,
        out_shape=jax.ShapeDtypeStruct((s, d), F32),
        compiler_params=_cparams(("arbitrary",)),
        name="combine",
    )(wts_t, hfp, x1, gt, sg, su, sd, gathered)


def _tile(n, want):
    t = min(n, want)
    assert n % t == 0, (n, t)
    return t


def _layer(l, x, c_col, pos_row, p):
    s, d = x.shape
    lambda_init = 0.8 - 0.6 * math.exp(-0.3 * l)
    gqk, gv = GLA_HEADS * GLA_DK, GLA_HEADS * GLA_DV
    dqk, dvw = DIFF_HEADS * 2 * DIFF_DH, DIFF_HEADS * DIFF_DV
    lowrank = p["gla_w_a2"].shape[0]

    mod = _ada(c_col, p["w_ada"], p["b_ada"][None, :])
    sh_a, sc_a, gt_a, sh_f, sc_f, gt_f = [mod[:, j * d:(j + 1) * d] for j in range(6)]

    w_in = p["w_in"]
    o = 0
    cols = {}
    for name, wdt in (("gq", gqk), ("gk", gqk), ("gv", gv), ("ga", lowrank), ("gg", gv),
                      ("dq", dqk), ("dk", dqk), ("dv", dvw), ("mg", d), ("md", d)):
        cols[name] = w_in[:, o:o + wdt]
        o += wdt
    row_names = ("gv", "gg", "mg", "md", "gq")
    w_row = jnp.concatenate([cols[n] for n in row_names], axis=1).astype(BF16)
    col_of, o = {}, 0
    for n in row_names:
        col_of[n] = o
        o += cols[n].shape[1]
    t_names = ("dq", "dk", "dv", "gk")
    w_t = jnp.concatenate([cols[n] for n in t_names], axis=1).T.astype(BF16)
    row_of, o = {}, 0
    for n in t_names:
        row_of[n] = o
        o += cols[n].shape[1]
    w_ga = jnp.pad(cols["ga"], ((0, 0), (0, 128 - lowrank))).astype(BF16)

    g1 = p["norm1_g"][None, :]
    tm = _tile(s, 1024)
    proj, ga = _inproj(x, g1, sc_a, sh_a, w_row, w_ga, tm, 512)
    projt = _inproj_t(x, g1, sc_a, sh_a, w_t, tm, 512)

    wa2t = jnp.pad(p["gla_w_a2"].T, ((0, 0), (0, 128 - lowrank)))
    o_gla = _gla(proj, projt, ga, wa2t, p["gla_b_a"][:, None], p["gla_onorm_g"][None, :],
                 _tile(s, 512), col_of["gq"], col_of["gv"], col_of["gg"], row_of["gk"])

    invf = ROPE_THETA ** (-jnp.arange(0, ROT_DIM, 2, dtype=F32) / ROT_DIM)
    qat, qbt, kr, vte = _qkprep(projt, pos_row, invf[:, None], p["diff_qnorm_g"][:, None],
                                p["diff_knorm_g"][:, None], _tile(s, 512), row_of["dq"],
                                row_of["dk"], row_of["dv"])
    tq = _tile(s, 512)
    o_diff = _diffattn(qat, qbt, kr, vte, p["diff_lq1"][None, :], p["diff_lk1"][None, :],
                       p["diff_lq2"][None, :], p["diff_lk2"][None, :],
                       p["diff_subln_g"][:, None], lambda_init, tq, tq)

    x1, hf, hfp = _mergeout(o_gla, o_diff, proj, x, p["w_branch_gla"].astype(BF16),
                       p["w_branch_diff"].astype(BF16), p["w_out"].astype(BF16), gt_a,
                       p["norm2_g"][None, :], sc_f, sh_f, _tile(s, 512),
                       col_of["mg"], col_of["md"])

    e = p["w_router"].shape[1]
    idx, wts, rnk, cnt = _route(hf, p["w_router"].T, p["router_bias"][:, None], _tile(s, 512))

    counts = cnt[:, 0]
    pcounts = ((counts + MOE_ROWS - 1) // MOE_ROWS) * MOE_ROWS
    pend = jnp.cumsum(pcounts)
    pstart = pend - pcounts
    pos = _positions(idx, rnk, pstart[:, None], _tile(s, 512))
    n_items = (s * TOP_K) // MOE_ROWS + e
    n_valid = (pend[-1] // MOE_ROWS).astype(jnp.int32)
    item_b = jnp.minimum(jnp.arange(n_items, dtype=jnp.int32), n_valid - 1)
    item_e = jnp.minimum(jnp.sum(pend[None, :] <= (item_b * MOE_ROWS)[:, None], axis=1),
                         e - 1).astype(jnp.int32)

    td = _tile(s, 256)
    pos_d = pos.reshape(TOP_K, s // td, td).transpose(1, 0, 2)
    xs = _dispatch(pos_d, hfp, jnp.zeros((n_items * MOE_ROWS, d // 2), jnp.uint32), td)
    prev_e = jnp.concatenate([jnp.full((1,), -1, jnp.int32), item_e[:-1]])
    item_first = ((jnp.arange(n_items) < n_valid) & (item_e != prev_e)).astype(jnp.int32)
    item_slot = ((jnp.cumsum(item_first) - 1) % 2).astype(jnp.int32)
    cand = jnp.where(pcounts > 0, jnp.arange(e, dtype=jnp.int32), e)
    following = jnp.concatenate([lax.cummin(cand[::-1])[::-1][1:], jnp.full((1,), e, jnp.int32)])
    item_next = jnp.where(following[item_e] < e, following[item_e], -1).astype(jnp.int32)
    ys = _moe(item_e, item_b, item_first, item_slot, item_next, n_valid[None], xs,
              p["w_exp_gate"], p["w_exp_up"], p["w_exp_down"])
    gathered = _sc_gather_rows(ys, pos.reshape(1, TOP_K * s)).reshape(TOP_K, s, d // 2)
    return _combine(wts.T, hfp, x1, gt_f, p["w_sh_gate"].astype(BF16),
                    p["w_sh_up"].astype(BF16), p["w_sh_down"].astype(BF16), gathered,
                    _tile(s, 256))


_LAYER_PARAMS = ("w_ada", "b_ada", "norm1_g", "w_in", "gla_w_a2", "gla_b_a", "gla_onorm_g",
                 "diff_qnorm_g", "diff_knorm_g", "diff_lq1", "diff_lk1", "diff_lq2", "diff_lk2",
                 "diff_subln_g", "w_branch_gla", "w_branch_diff", "w_out", "norm2_g", "w_router",
                 "router_bias", "w_exp_gate", "w_exp_up", "w_exp_down", "w_sh_gate", "w_sh_up",
                 "w_sh_down")


def kernel(x, c, positions, w_ada, b_ada, norm1_g, w_in, gla_w_a2, gla_b_a, gla_onorm_g, diff_qnorm_g, diff_knorm_g, diff_lq1, diff_lk1, diff_lq2, diff_lk2, diff_subln_g, w_branch_gla, w_branch_diff, w_out, norm2_g, w_router, router_bias, w_exp_gate, w_exp_up, w_exp_down, w_sh_gate, w_sh_up, w_sh_down):
    stacked = dict(zip(_LAYER_PARAMS, (
        w_ada, b_ada, norm1_g, w_in, gla_w_a2, gla_b_a, gla_onorm_g, diff_qnorm_g, diff_knorm_g,
        diff_lq1, diff_lk1, diff_lq2, diff_lk2, diff_subln_g, w_branch_gla, w_branch_diff, w_out,
        norm2_g, w_router, router_bias, w_exp_gate, w_exp_up, w_exp_down, w_sh_gate, w_sh_up,
        w_sh_down)))
    b, s, d = x.shape
    assert b == 1, "single-sequence kernel"
    xl = x[0]
    c_col = c[0][:, None]
    pos_row = positions.astype(jnp.int32)
    for l in range(w_ada.shape[0]):
        xl = _layer(l, xl, c_col, pos_row, {k: v[l] for k, v in stacked.items()})
    return xl[None]
```

```python
import functools
import math

import jax
import jax.numpy as jnp
from jax import lax
from jax.experimental import pallas as pl
from jax.experimental.pallas import tpu as pltpu
from jax.experimental.pallas import tpu_sc as plsc

CHUNK = 64
EPS = 1e-6
GLA_HEADS = 4
GLA_DK = 128
GLA_DV = 256
GLA_TAU = 16.0
DIFF_HEADS = 8
DIFF_DH = 64
DIFF_DV = 2 * DIFF_DH
ROPE_THETA = 500000.0
ROT_DIM = DIFF_DH // 4
N_GROUPS = 8
TOPK_GROUPS = 4
TOP_K = 8
ROUTED_SCALE = 2.5

MOE_ROWS = 256
SC_GATHER_WINDOW = 64
VMEM_LIMIT = 56 * 1024 * 1024
NEG_BIG = -1e30
LOG2E = 1.4426950408889634
HIGHEST = lax.Precision.HIGHEST
F32 = jnp.float32
BF16 = jnp.bfloat16


def _cparams(sem):
    return pltpu.CompilerParams(dimension_semantics=sem, vmem_limit_bytes=VMEM_LIMIT)


def _nt_dot(a, b, precision=None):
    return lax.dot_general(a, b, (((1,), (1,)), ((), ())), precision=precision,
                           preferred_element_type=F32)


def _pack_halves(x):
    n = x.shape[1] // 2
    lo = pltpu.bitcast(x[:, :n].astype(BF16).astype(F32), jnp.uint32) >> 16
    hi = pltpu.bitcast(x[:, n:].astype(BF16).astype(F32), jnp.uint32) & jnp.uint32(0xFFFF0000)
    return lo | hi


def _unpack_halves(w):
    return (pltpu.bitcast(w << 16, F32), pltpu.bitcast(w & jnp.uint32(0xFFFF0000), F32))


def _rms_mod(x, g, sc, sh):
    xn = x * lax.rsqrt(jnp.mean(x * x, axis=-1, keepdims=True) + EPS)
    return (xn * g) * (1.0 + sc) + sh


def _ada_kernel(c_ref, w_ref, b_ref, o_ref):
    c = c_ref[...]
    ca = c * jax.nn.sigmoid(c)
    o_ref[...] = jnp.sum(ca * w_ref[...], axis=0, keepdims=True) + b_ref[...]


def _ada(c_col, w, b):
    d, n = w.shape
    tn = min(1024, n)
    return pl.pallas_call(
        _ada_kernel,
        grid=(n // tn,),
        in_specs=[pl.BlockSpec((d, 1), lambda j: (0, 0)),
                  pl.BlockSpec((d, tn), lambda j: (0, j)),
                  pl.BlockSpec((1, tn), lambda j: (0, j))],
        out_specs=pl.BlockSpec((1, tn), lambda j: (0, j)),
        out_shape=jax.ShapeDtypeStruct((1, n), F32),
        compiler_params=_cparams(("arbitrary",)),
        name="ada",
    )(c_col, w, b)


def _inproj_kernel(x_ref, g_ref, sc_ref, sh_ref, w_ref, wga_ref, o_ref, ga_ref, h_scr):
    @pl.when(pl.program_id(1) == 0)
    def _():
        h = _rms_mod(x_ref[...], g_ref[...], sc_ref[...], sh_ref[...]).astype(BF16)
        h_scr[...] = h
        ga_ref[...] = jnp.dot(h, wga_ref[...], preferred_element_type=F32)

    o_ref[...] = jnp.dot(h_scr[...], w_ref[...], preferred_element_type=F32).astype(BF16)


def _inproj(x, g, sc, sh, w, wga, tm, tn):
    s, d = x.shape
    n = w.shape[1]
    vec = pl.BlockSpec((1, d), lambda i, j: (0, 0))
    return pl.pallas_call(
        _inproj_kernel,
        grid=(s // tm, n // tn),
        in_specs=[pl.BlockSpec((tm, d), lambda i, j: (i, 0)), vec, vec, vec,
                  pl.BlockSpec((d, tn), lambda i, j: (0, j)),
                  pl.BlockSpec((d, 128), lambda i, j: (0, 0))],
        out_specs=[pl.BlockSpec((tm, tn), lambda i, j: (i, j)),
                   pl.BlockSpec((tm, 128), lambda i, j: (i, 0))],
        out_shape=[jax.ShapeDtypeStruct((s, n), BF16), jax.ShapeDtypeStruct((s, 128), F32)],
        scratch_shapes=[pltpu.VMEM((tm, d), BF16)],
        compiler_params=_cparams(("arbitrary", "arbitrary")),
        name="inproj",
    )(x, g, sc, sh, w, wga)


def _inproj_t_kernel(x_ref, g_ref, sc_ref, sh_ref, wt_ref, o_ref, h_scr):
    @pl.when(pl.program_id(1) == 0)
    def _():
        h_scr[...] = _rms_mod(x_ref[...], g_ref[...], sc_ref[...], sh_ref[...]).astype(BF16)

    o_ref[...] = _nt_dot(wt_ref[...], h_scr[...]).astype(BF16)


def _inproj_t(x, g, sc, sh, wt, tm, tn):
    s, d = x.shape
    n = wt.shape[0]
    vec = pl.BlockSpec((1, d), lambda i, j: (0, 0))
    return pl.pallas_call(
        _inproj_t_kernel,
        grid=(s // tm, n // tn),
        in_specs=[pl.BlockSpec((tm, d), lambda i, j: (i, 0)), vec, vec, vec,
                  pl.BlockSpec((tn, d), lambda i, j: (j, 0))],
        out_specs=pl.BlockSpec((tn, tm), lambda i, j: (j, i)),
        out_shape=jax.ShapeDtypeStruct((n, s), BF16),
        scratch_shapes=[pltpu.VMEM((tm, d), BF16)],
        compiler_params=_cparams(("arbitrary", "arbitrary")),
        name="inproj_t",
    )(x, g, sc, sh, wt)


def _gla_kernel(q_ref, kt_ref, v_ref, gg_ref, ga_ref, wa2t_ref, ba_ref, on_ref, o_ref,
                state_ref, o_scr):
    tt = q_ref.shape[0]
    nchunk = tt // CHUNK

    @pl.when(pl.program_id(0) == 0)
    def _():
        state_ref[...] = jnp.zeros_like(state_ref)

    zt = _nt_dot(wa2t_ref[...], ga_ref[...], precision=HIGHEST) + ba_ref[...]
    lat = (jnp.minimum(zt, 0.0) - jnp.log1p(jnp.exp(-jnp.abs(zt)))) * (1.0 / GLA_TAU)
    row = lax.broadcasted_iota(jnp.int32, (tt, tt), 0)
    col = lax.broadcasted_iota(jnp.int32, (tt, tt), 1)
    same = (row // CHUNK) == (col // CHUNK)
    incl = jnp.where(same & (row <= col), 1.0, 0.0).astype(F32)
    full = jnp.where(same, 1.0, 0.0).astype(F32)
    cumt = jnp.dot(lat, incl, precision=HIGHEST, preferred_element_type=F32)
    tott = jnp.dot(lat, full, precision=HIGHEST, preferred_element_type=F32)
    kdt = kt_ref[...].astype(F32) * jnp.exp(tott - cumt)
    dec = jnp.exp(tott)

    lane = lax.broadcasted_iota(jnp.int32, (GLA_DK, 2 * CHUNK), 1)
    for c in range(nchunk):
        pair = (c // 2) * 2 * CHUNK
        if nchunk > 1:
            keep = (lane // CHUNK) == (c % 2)
        for h in range(GLA_HEADS):
            rows = slice(h * GLA_DK, (h + 1) * GLA_DK)
            vcols = slice(h * GLA_DV, (h + 1) * GLA_DV)
            if nchunk > 1:
                a = jnp.where(keep, kdt[rows, pair:pair + 2 * CHUNK], 0.0).astype(BF16)
                vp = v_ref[pair:pair + 2 * CHUNK, vcols]
            else:
                a = kdt[rows, :].astype(BF16)
                vp = v_ref[:, vcols]
            upd = jnp.dot(a, vp, preferred_element_type=F32)
            dcol = dec[rows, c * CHUNK:c * CHUNK + 1]
            st = state_ref[h] * dcol + upd
            state_ref[h] = st
            qc = q_ref[c * CHUNK:(c + 1) * CHUNK, rows]
            o_scr[c * CHUNK:(c + 1) * CHUNK, vcols] = jnp.dot(
                qc, st.astype(BF16), preferred_element_type=F32)

    for h in range(GLA_HEADS):
        vcols = slice(h * GLA_DV, (h + 1) * GLA_DV)
        o = o_scr[:, vcols] * (GLA_DK ** -0.5)
        o = o * lax.rsqrt(jnp.mean(o * o, axis=-1, keepdims=True) + EPS) * on_ref[...]
        g = gg_ref[:, vcols].astype(F32)
        o_ref[:, vcols] = (o * (g * jax.nn.sigmoid(g))).astype(BF16)


def _gla(proj, projt, ga, wa2t, ba_col, on_g, tt, col_q, col_v, col_g, row_k):
    s = proj.shape[0]
    qk = GLA_HEADS * GLA_DK
    vw = GLA_HEADS * GLA_DV
    return pl.pallas_call(
        _gla_kernel,
        grid=(s // tt,),
        in_specs=[pl.BlockSpec((tt, qk), lambda i: (i, col_q // qk)),
                  pl.BlockSpec((qk, tt), lambda i: (row_k // qk, i)),
                  pl.BlockSpec((tt, vw), lambda i: (i, col_v // vw)),
                  pl.BlockSpec((tt, vw), lambda i: (i, col_g // vw)),
                  pl.BlockSpec((tt, 128), lambda i: (i, 0)),
                  pl.BlockSpec((qk, 128), lambda i: (0, 0)),
                  pl.BlockSpec((qk, 1), lambda i: (0, 0)),
                  pl.BlockSpec((1, GLA_DV), lambda i: (0, 0))],
        out_specs=pl.BlockSpec((tt, vw), lambda i: (i, 0)),
        out_shape=jax.ShapeDtypeStruct((s, vw), BF16),
        scratch_shapes=[pltpu.VMEM((GLA_HEADS, GLA_DK, GLA_DV), F32),
                        pltpu.VMEM((tt, vw), F32)],
        compiler_params=_cparams(("arbitrary",)),
        name="gla",
    )(proj, projt, proj, proj, ga, wa2t, ba_col, on_g)


def _qknorm_rope_t(xt, g_col, cos, sin):
    n, tm = xt.shape
    x3 = xt.reshape(n // DIFF_DH, DIFF_DH, tm)
    r = lax.rsqrt(jnp.mean(x3 * x3, axis=1, keepdims=True) + EPS)
    y = x3 * r * g_col[None]
    half = ROT_DIM // 2
    y1, y2, rest = y[:, :half], y[:, half:ROT_DIM], y[:, ROT_DIM:]
    o1 = y1 * cos[None] - y2 * sin[None]
    o2 = y2 * cos[None] + y1 * sin[None]
    return jnp.concatenate([o1, o2, rest], axis=1)


def _qkprep_kernel(qt_ref, kt_ref, vt_ref, pos_ref, invf_ref, qg_ref, kg_ref, qa_ref, qb_ref,
                   ko_ref, ve_ref):
    tm = qt_ref.shape[1]
    v3 = vt_ref[...].reshape(DIFF_HEADS, DIFF_DV, tm)
    ones = jnp.ones((DIFF_HEADS, ATT_SUM_ROWS, tm), BF16)
    ve_ref[...] = jnp.concatenate([v3, ones], axis=1).reshape(-1, tm)
    ang = pos_ref[...].astype(F32) * invf_ref[...]
    cos, sin = jnp.cos(ang), jnp.sin(ang)
    k3 = _qknorm_rope_t(kt_ref[...].astype(F32), kg_ref[...], cos, sin)
    ko_ref[...] = k3.reshape(-1, tm).T.astype(BF16)
    q3 = _qknorm_rope_t(qt_ref[...].astype(F32), qg_ref[...], cos, sin) * (
        DIFF_DH ** -0.5 * LOG2E)
    seg = lax.broadcasted_iota(jnp.int32, q3.shape, 0)
    qa_ref[...] = jnp.where(seg % 2 == 0, q3, 0.0).reshape(-1, tm).astype(BF16)
    qb_ref[...] = jnp.where(seg % 2 == 1, q3, 0.0).reshape(-1, tm).astype(BF16)


def _qkprep(projt, pos_row, invf_col, qg_col, kg_col, tm, row_q, row_k, row_v):
    s = projt.shape[1]
    n = DIFF_HEADS * 2 * DIFF_DH
    ne = DIFF_HEADS * (DIFF_DV + ATT_SUM_ROWS)
    col = pl.BlockSpec((DIFF_DH, 1), lambda i: (0, 0))
    return pl.pallas_call(
        _qkprep_kernel,
        grid=(s // tm,),
        in_specs=[pl.BlockSpec((n, tm), lambda i: (row_q // n, i)),
                  pl.BlockSpec((n, tm), lambda i: (row_k // n, i)),
                  pl.BlockSpec((n, tm), lambda i: (row_v // n, i)),
                  pl.BlockSpec((1, tm), lambda i: (0, i)),
                  pl.BlockSpec((ROT_DIM // 2, 1), lambda i: (0, 0)), col, col],
        out_specs=[pl.BlockSpec((n, tm), lambda i: (0, i)),
                   pl.BlockSpec((n, tm), lambda i: (0, i)),
                   pl.BlockSpec((tm, n), lambda i: (i, 0)),
                   pl.BlockSpec((ne, tm), lambda i: (0, i))],
        out_shape=[jax.ShapeDtypeStruct((n, s), BF16), jax.ShapeDtypeStruct((n, s), BF16),
                   jax.ShapeDtypeStruct((s, n), BF16), jax.ShapeDtypeStruct((ne, s), BF16)],
        compiler_params=_cparams(("arbitrary",)),
        name="qkprep",
    )(projt, projt, projt, pos_row, invf_col, qg_col, kg_col)


ATT_COLS = 256
ATT_LOOKAHEAD = 3
ATT_SUM_ROWS = 16


def _diffattn_kernel(qa_ref, qb_ref, k_ref, vt_ref, lq1_ref, lk1_ref, lq2_ref, lk2_ref, sg_ref,
                     o_ref, *scr, lambda_init, tk, cols):
    tq = qa_ref.shape[1]
    nblk = 2 * tq // cols
    q_scr, m_scr, acc_scr = (scr[b * nblk:(b + 1) * nblk] for b in range(3))
    i = pl.program_id(1)
    for c in range(nblk):
        src = qa_ref if c * cols < tq else qb_ref
        off = (c * cols) % tq
        q_scr[c][...] = src[:, off:off + cols]
        m_scr[c][...] = jnp.full_like(m_scr[c], NEG_BIG)
        acc_scr[c][...] = jnp.zeros_like(acc_scr[c])

    def scores(j, c):
        start = pl.multiple_of(j * tk, tk)
        return jnp.dot(k_ref[pl.ds(start, tk), :], q_scr[c][...], preferred_element_type=F32)

    def steps(tiles, masked):
        items = [(j, c) for j in tiles for c in range(nblk)]
        pending = [scores(*it) for it in items[:ATT_LOOKAHEAD]]
        for n, (j, c) in enumerate(items):
            s = pending.pop(0)
            if n + ATT_LOOKAHEAD < len(items):
                pending.append(scores(*items[n + ATT_LOOKAHEAD]))
            start = pl.multiple_of(j * tk, tk)
            if masked:
                krow = lax.broadcasted_iota(jnp.int32, (tk, cols), 0)
                qcol = lax.broadcasted_iota(jnp.int32, (tk, cols), 1)
                qpos = i * tq + (c * cols) % tq + qcol
                s = jnp.where((start + krow) // CHUNK <= qpos // CHUNK, s, NEG_BIG)
            m_prev = m_scr[c][...]
            m_new = jnp.maximum(m_prev, jnp.max(s, axis=0, keepdims=True))
            alpha = jnp.exp2(m_prev - m_new)
            p = jnp.exp2((s - m_new).astype(BF16))
            acc_scr[c][...] = alpha * acc_scr[c][...] + jnp.dot(
                vt_ref[:, pl.ds(start, tk)], p, preferred_element_type=F32)
            m_scr[c][...] = m_new

    n_full = (i * tq) // tk
    lax.fori_loop(0, n_full // 2, lambda t, c: (steps((2 * t, 2 * t + 1), False), c)[1], 0)

    @pl.when(n_full % 2 == 1)
    def _():
        steps((n_full - 1,), False)

    n_all = ((i + 1) * tq + tk - 1) // tk
    lax.fori_loop(n_full, n_all, lambda j, c: (steps((j,), True), c)[1], 0)

    o = jnp.concatenate([acc_scr[c][:DIFF_DV] / acc_scr[c][DIFF_DV:DIFF_DV + 1]
                         for c in range(nblk)], axis=1)
    lam = (jnp.exp(jnp.sum(lq1_ref[...] * lk1_ref[...]))
           - jnp.exp(jnp.sum(lq2_ref[...] * lk2_ref[...])) + lambda_init)
    o = o[:, :tq] - lam * o[:, tq:]
    o = o * lax.rsqrt(jnp.mean(o * o, axis=0, keepdims=True) + EPS) * sg_ref[...]
    o_ref[...] = (o * (1.0 - lambda_init)).T.astype(BF16)


def _diffattn(qat, qbt, kr, vte, lq1, lk1, lq2, lk2, sg_col, lambda_init, tq, tk):
    s = kr.shape[0]
    hd = 2 * DIFF_DH
    vec = pl.BlockSpec((1, DIFF_DH), lambda h, i: (0, 0))
    cols = min(ATT_COLS, tq)
    nblk = 2 * tq // cols
    kern = functools.partial(_diffattn_kernel, lambda_init=lambda_init, tk=tk, cols=cols)
    dve = DIFF_DV + ATT_SUM_ROWS
    scratch = ([pltpu.VMEM((hd, cols), BF16)] * nblk + [pltpu.VMEM((1, cols), F32)] * nblk
               + [pltpu.VMEM((dve, cols), F32)] * nblk)
    return pl.pallas_call(
        kern,
        grid=(DIFF_HEADS, s // tq),
        in_specs=[pl.BlockSpec((hd, tq), lambda h, i: (h, i)),
                  pl.BlockSpec((hd, tq), lambda h, i: (h, i)),
                  pl.BlockSpec((s, hd), lambda h, i: (0, h)),
                  pl.BlockSpec((dve, s), lambda h, i: (h, 0)),
                  vec, vec, vec, vec,
                  pl.BlockSpec((DIFF_DV, 1), lambda h, i: (0, 0))],
        out_specs=pl.BlockSpec((tq, DIFF_DV), lambda h, i: (i, h)),
        out_shape=jax.ShapeDtypeStruct((s, DIFF_HEADS * DIFF_DV), BF16),
        scratch_shapes=scratch,
        compiler_params=_cparams(("arbitrary", "arbitrary")),
        name="diffattn",
    )(qat, qbt, kr, vte, lq1, lk1, lq2, lk2, sg_col)


def _mergeout_kernel(og_ref, od_ref, mg_ref, md_ref, x_ref, wbg_ref, wbd_ref, wo_ref, gt_ref,
                     g2_ref, sc_ref, sh_ref, x1_ref, hf_ref, hfp_ref):
    bg = jnp.dot(og_ref[...], wbg_ref[...], preferred_element_type=F32)
    bd = jnp.dot(od_ref[...], wbd_ref[...], preferred_element_type=F32)
    merged = (jax.nn.sigmoid(mg_ref[...].astype(F32)) * bg
              + jax.nn.sigmoid(md_ref[...].astype(F32)) * bd)
    x1 = x_ref[...] + gt_ref[...] * jnp.dot(merged.astype(BF16), wo_ref[...],
                                             preferred_element_type=F32)
    x1_ref[...] = x1
    hf = _rms_mod(x1, g2_ref[...], sc_ref[...], sh_ref[...])
    hf_ref[...] = hf
    hfp_ref[...] = _pack_halves(hf)


def _mergeout(og, od, proj, x, wbg, wbd, wo, gt, g2, sc, sh, tm, col_mg, col_md):
    s, d = x.shape
    vec = pl.BlockSpec((1, d), lambda i: (0, 0))
    wspec = pl.BlockSpec((d, d), lambda i: (0, 0))
    row = pl.BlockSpec((tm, d), lambda i: (i, 0))
    return pl.pallas_call(
        _mergeout_kernel,
        grid=(s // tm,),
        in_specs=[row, row,
                  pl.BlockSpec((tm, d), lambda i: (i, col_mg // d)),
                  pl.BlockSpec((tm, d), lambda i: (i, col_md // d)),
                  row, wspec, wspec, wspec, vec, vec, vec, vec],
        out_specs=[row, row, pl.BlockSpec((tm, d // 2), lambda i: (i, 0))],
        out_shape=[jax.ShapeDtypeStruct((s, d), F32), jax.ShapeDtypeStruct((s, d), F32),
                   jax.ShapeDtypeStruct((s, d // 2), jnp.uint32)],
        compiler_params=_cparams(("arbitrary",)),
        name="mergeout",
    )(og, od, proj, proj, x, wbg, wbd, wo, gt, g2, sc, sh)


def _route_kernel(hf_ref, wrt_ref, bias_ref, idx_ref, wts_ref, rnk_ref, cnt_ref, run_scr):
    tr = hf_ref.shape[0]
    e = wrt_ref.shape[0]
    gsz = e // N_GROUPS

    @pl.when(pl.program_id(0) == 0)
    def _():
        run_scr[...] = jnp.zeros_like(run_scr)

    logits = _nt_dot(wrt_ref[...], hf_ref[...], precision=HIGHEST)
    scores = jax.nn.sigmoid(logits)
    biased = scores + bias_ref[...]
    g3 = biased.reshape(N_GROUPS, gsz, tr)
    m1 = jnp.max(g3, axis=1, keepdims=True)
    n_top = jnp.sum(jnp.where(g3 == m1, 1.0, 0.0), axis=1, keepdims=True)
    m2 = jnp.max(jnp.where(g3 < m1, g3, -jnp.inf), axis=1, keepdims=True)
    gs = (m1 + jnp.where(n_top >= 2.0, m1, m2)).reshape(N_GROUPS, tr)
    gi = lax.broadcasted_iota(jnp.int32, (N_GROUPS, tr), 0)
    beaten = jnp.zeros((N_GROUPS, tr), F32)
    for g in range(N_GROUPS):
        other = gs[g:g + 1, :]
        beaten = beaten + jnp.where((other > gs) | ((other == gs) & (g < gi)), 1.0, 0.0)
    gsel = (beaten < float(TOPK_GROUPS)).reshape(N_GROUPS, 1, tr)
    masked = jnp.where(gsel, g3, -jnp.inf).reshape(e, tr)

    ids = lax.broadcasted_iota(jnp.int32, (e, tr), 0)
    chosen = jnp.zeros((e, tr), F32)
    sel_idx, sel_score = [], []
    for _ in range(TOP_K):
        mx = jnp.max(masked, axis=0, keepdims=True)
        ix = jnp.min(jnp.where(masked == mx, ids, e), axis=0, keepdims=True)
        hit = ids == ix
        sel_idx.append(ix)
        sel_score.append(jnp.sum(jnp.where(hit, scores, 0.0), axis=0, keepdims=True))
        chosen = jnp.where(hit, 1.0, chosen)
        masked = jnp.where(hit, -jnp.inf, masked)
    idx = jnp.concatenate(sel_idx, axis=0)
    sc = jnp.concatenate(sel_score, axis=0)
    idx_ref[...] = idx
    wts_ref[...] = sc / jnp.sum(sc, axis=0, keepdims=True) * ROUTED_SCALE

    row = lax.broadcasted_iota(jnp.int32, (tr, tr), 0)
    col = lax.broadcasted_iota(jnp.int32, (tr, tr), 1)
    before = jnp.where(row < col, 1.0, 0.0).astype(BF16)
    prior = jnp.dot(chosen.astype(BF16), before, preferred_element_type=F32) + run_scr[:, 0:1]
    rnk_ref[...] = jnp.concatenate(
        [jnp.sum(jnp.where(ids == sel_idx[k], prior, 0.0), axis=0, keepdims=True)
         for k in range(TOP_K)], axis=0).astype(jnp.int32)
    run_scr[...] = run_scr[...] + jnp.sum(chosen, axis=1, keepdims=True)
    cnt_ref[...] = run_scr[...].astype(jnp.int32)


def _route(hf, wrt, bias_col, tr):
    s, d = hf.shape
    e = wrt.shape[0]
    tok = pl.BlockSpec((TOP_K, tr), lambda i: (0, i))
    return pl.pallas_call(
        _route_kernel,
        grid=(s // tr,),
        in_specs=[pl.BlockSpec((tr, d), lambda i: (i, 0)),
                  pl.BlockSpec((e, d), lambda i: (0, 0)),
                  pl.BlockSpec((e, 1), lambda i: (0, 0))],
        out_specs=[tok, tok, tok, pl.BlockSpec((e, 128), lambda i: (0, 0))],
        out_shape=[jax.ShapeDtypeStruct((TOP_K, s), jnp.int32),
                   jax.ShapeDtypeStruct((TOP_K, s), F32),
                   jax.ShapeDtypeStruct((TOP_K, s), jnp.int32),
                   jax.ShapeDtypeStruct((e, 128), jnp.int32)],
        scratch_shapes=[pltpu.VMEM((e, 128), F32)],
        compiler_params=_cparams(("arbitrary",)),
        name="route",
    )(hf, wrt, bias_col)


def _positions_kernel(idx_ref, rnk_ref, pstart_ref, pos_ref):
    e = pstart_ref.shape[0]
    ts = idx_ref.shape[1]
    ids = lax.broadcasted_iota(jnp.int32, (e, ts), 0)
    idx = idx_ref[...]
    pos_ref[...] = rnk_ref[...] + jnp.concatenate(
        [jnp.sum(jnp.where(ids == idx[k:k + 1, :], pstart_ref[...], 0), axis=0, keepdims=True)
         for k in range(TOP_K)], axis=0)


def _positions(idx, rnk, pstart_col, ts):
    s = idx.shape[1]
    e = pstart_col.shape[0]
    tok = pl.BlockSpec((TOP_K, ts), lambda i: (0, i))
    return pl.pallas_call(
        _positions_kernel,
        grid=(s // ts,),
        in_specs=[tok, tok, pl.BlockSpec((e, 1), lambda i: (0, 0))],
        out_specs=tok,
        out_shape=jax.ShapeDtypeStruct((TOP_K, s), jnp.int32),
        compiler_params=_cparams(("arbitrary",)),
        name="positions",
    )(idx, rnk, pstart_col)


def _swiglu_packed(xp, wg, wu, wd):
    lo, hi = _unpack_halves(xp)
    lo, hi = lo.astype(BF16), hi.astype(BF16)
    n = lo.shape[1]
    g = (jnp.dot(lo, wg[:n], preferred_element_type=F32)
         + jnp.dot(hi, wg[n:], preferred_element_type=F32))
    u = (jnp.dot(lo, wu[:n], preferred_element_type=F32)
         + jnp.dot(hi, wu[n:], preferred_element_type=F32))
    h = (g * jax.nn.sigmoid(g)) * u
    return jnp.dot(h.astype(BF16), wd[...], preferred_element_type=F32)


def _moe_kernel(ie_ref, ib_ref, first_ref, slot_ref, ne_ref, rows_ref, nv_ref, xs_ref, wg_hbm,
                wu_hbm, wd_hbm, ys_ref, wg_f, wu_f, wd_f, wg_b, wu_b, wd_b, sem):
    del ib_ref
    i = pl.program_id(0)

    def fetch(e, slot):
        return [pltpu.make_async_copy(src.at[e], dst.at[slot], sem.at[slot, n])
                for n, (src, dst) in enumerate(((wg_hbm, wg_f), (wu_hbm, wu_f), (wd_hbm, wd_f)))]

    @pl.when(i == 0)
    def _():
        for cp in fetch(ie_ref[0], 0):
            cp.start()

    @pl.when(i < nv_ref[0])
    def _():
        for slot in range(2):
            @pl.when((first_ref[i] == 1) & (slot_ref[i] == slot))
            def _():
                for cp in fetch(ie_ref[i], slot):
                    cp.wait()
                wg_b[...] = wg_f[slot].astype(BF16)
                wu_b[...] = wu_f[slot].astype(BF16)
                wd_b[...] = wd_f[slot].astype(BF16)

                @pl.when(ne_ref[i] >= 0)
                def _():
                    for cp in fetch(ne_ref[i], 1 - slot):
                        cp.start()

        row = lax.broadcasted_iota(jnp.int32, xs_ref.shape, 0)
        xp = jnp.where(row < rows_ref[i], xs_ref[...], jnp.uint32(0))
        ys_ref[...] = _pack_halves(_swiglu_packed(xp, wg_b, wu_b, wd_b))


def _moe(item_e, item_b, item_first, item_slot, item_next, item_rows, n_valid, xs, wg, wu, wd):
    m_pad, dh = xs.shape
    _, d, f = wg.shape
    n_items = item_e.shape[0]
    blk = lambda i, ie, ib, fi, sl, ne, nr, nv: (ib[i], 0)
    hbm = pl.BlockSpec(memory_space=pl.ANY)
    return pl.pallas_call(
        _moe_kernel,
        grid_spec=pltpu.PrefetchScalarGridSpec(
            num_scalar_prefetch=7,
            grid=(n_items,),
            in_specs=[pl.BlockSpec((MOE_ROWS, dh), blk), hbm, hbm, hbm],
            out_specs=pl.BlockSpec((MOE_ROWS, dh), blk),
            scratch_shapes=[pltpu.VMEM((2, d, f), F32), pltpu.VMEM((2, d, f), F32),
                            pltpu.VMEM((2, f, d), F32),
                            pltpu.VMEM((d, f), BF16), pltpu.VMEM((d, f), BF16),
                            pltpu.VMEM((f, d), BF16),
                            pltpu.SemaphoreType.DMA((2, 3))],
        ),
        out_shape=jax.ShapeDtypeStruct((m_pad, dh), jnp.uint32),
        compiler_params=_cparams(("arbitrary",)),
        name="moe",
    )(item_e, item_b, item_first, item_slot, item_next, item_rows, n_valid, xs, wg, wu, wd)


def _sc_gather_rows(table, idx_row):
    m = idx_row.shape[1]
    w = table.shape[1]
    idx_row = idx_row.reshape(m // SC_GATHER_WINDOW, SC_GATHER_WINDOW)
    mesh = plsc.VectorSubcoreMesh(core_axis_name="c", subcore_axis_name="s")

    @functools.partial(pl.kernel, mesh=mesh,
                       out_type=jax.ShapeDtypeStruct((m, w), table.dtype))
    def gather(table_hbm, idx_hbm, out_hbm):
        def body(idx_vmem, out_vmem):
            pltpu.sync_copy(table_hbm.at[idx_vmem.at[0]], out_vmem)

        pltpu.emit_pipeline(
            body,
            grid=(m // SC_GATHER_WINDOW,),
            in_specs=[pl.BlockSpec((1, SC_GATHER_WINDOW), lambda i: (i, 0))],
            out_specs=[pl.BlockSpec((SC_GATHER_WINDOW, w), lambda i: (i, 0))],
            core_axis_name=("c", "s"),
            dimension_semantics=(pltpu.PARALLEL,),
        )(idx_hbm, out_hbm)

    return gather(table, idx_row)


def _sc_scatter_rows(rows, idx_blocks, m_out):
    s, w = rows.shape
    mesh = plsc.VectorSubcoreMesh(core_axis_name="c", subcore_axis_name="s")

    @functools.partial(pl.kernel, mesh=mesh,
                       out_type=jax.ShapeDtypeStruct((m_out, w), rows.dtype))
    def scatter(rows_hbm, idx_hbm, out_hbm):
        def body(rows_vmem, idx_vmem):
            for k in range(TOP_K):
                pltpu.sync_copy(rows_vmem, out_hbm.at[idx_vmem.at[k]])

        pltpu.emit_pipeline(
            body,
            grid=(s // SC_GATHER_WINDOW,),
            in_specs=[pl.BlockSpec((SC_GATHER_WINDOW, w), lambda i: (i, 0)),
                      pl.BlockSpec((TOP_K, SC_GATHER_WINDOW), lambda i: (i, 0))],
            out_specs=[],
            core_axis_name=("c", "s"),
            dimension_semantics=(pltpu.PARALLEL,),
        )(rows_hbm, idx_hbm)

    return scatter(rows, idx_blocks)


def _combine_kernel(wt_ref, hf_ref, x1_ref, gt_ref, sg_ref, su_ref, sd_ref, g_ref, o_ref):
    tc = x1_ref.shape[0]
    y = _swiglu_packed(hf_ref[...], sg_ref, su_ref, sd_ref)
    wt = wt_ref[...]
    n = g_ref.shape[2]
    r_lo = jnp.zeros((tc, n), F32)
    r_hi = jnp.zeros((tc, n), F32)
    for k in range(TOP_K):
        lo, hi = _unpack_halves(g_ref[k])
        r_lo = r_lo + lo * wt[:, k:k + 1]
        r_hi = r_hi + hi * wt[:, k:k + 1]
    y = y + jnp.concatenate([r_lo, r_hi], axis=1)
    o_ref[...] = x1_ref[...] + gt_ref[...] * y


def _combine(wts_t, hfp, x1, gt, sg, su, sd, gathered, tc):
    s, d = x1.shape
    f = sg.shape[1]
    row = pl.BlockSpec((tc, d), lambda i: (i, 0))
    return pl.pallas_call(
        _combine_kernel,
        grid=(s // tc,),
        in_specs=[pl.BlockSpec((tc, TOP_K), lambda i: (i, 0)),
                  pl.BlockSpec((tc, d // 2), lambda i: (i, 0)), row,
                  pl.BlockSpec((1, d), lambda i: (0, 0)),
                  pl.BlockSpec((d, f), lambda i: (0, 0)),
                  pl.BlockSpec((d, f), lambda i: (0, 0)),
                  pl.BlockSpec((f, d), lambda i: (0, 0)),
                  pl.BlockSpec((TOP_K, tc, d // 2), lambda i: (0, i, 0))],
        out_specs=row,
        out_shape=jax.ShapeDtypeStruct((s, d), F32),
        compiler_params=_cparams(("arbitrary",)),
        name="combine",
    )(wts_t, hfp, x1, gt, sg, su, sd, gathered)


def _tile(n, want):
    t = min(n, want)
    assert n % t == 0, (n, t)
    return t


def _layer(l, x, c_col, pos_row, p):
    s, d = x.shape
    lambda_init = 0.8 - 0.6 * math.exp(-0.3 * l)
    gqk, gv = GLA_HEADS * GLA_DK, GLA_HEADS * GLA_DV
    dqk, dvw = DIFF_HEADS * 2 * DIFF_DH, DIFF_HEADS * DIFF_DV
    lowrank = p["gla_w_a2"].shape[0]

    mod = _ada(c_col, p["w_ada"], p["b_ada"][None, :])
    sh_a, sc_a, gt_a, sh_f, sc_f, gt_f = [mod[:, j * d:(j + 1) * d] for j in range(6)]

    w_in = p["w_in"]
    o = 0
    cols = {}
    for name, wdt in (("gq", gqk), ("gk", gqk), ("gv", gv), ("ga", lowrank), ("gg", gv),
                      ("dq", dqk), ("dk", dqk), ("dv", dvw), ("mg", d), ("md", d)):
        cols[name] = w_in[:, o:o + wdt]
        o += wdt
    row_names = ("gv", "gg", "mg", "md", "gq")
    w_row = jnp.concatenate([cols[n] for n in row_names], axis=1).astype(BF16)
    col_of, o = {}, 0
    for n in row_names:
        col_of[n] = o
        o += cols[n].shape[1]
    t_names = ("dq", "dk", "dv", "gk")
    w_t = jnp.concatenate([cols[n] for n in t_names], axis=1).T.astype(BF16)
    row_of, o = {}, 0
    for n in t_names:
        row_of[n] = o
        o += cols[n].shape[1]
    w_ga = jnp.pad(cols["ga"], ((0, 0), (0, 128 - lowrank))).astype(BF16)

    g1 = p["norm1_g"][None, :]
    tm = _tile(s, 1024)
    proj, ga = _inproj(x, g1, sc_a, sh_a, w_row, w_ga, tm, 512)
    projt = _inproj_t(x, g1, sc_a, sh_a, w_t, tm, 512)

    wa2t = jnp.pad(p["gla_w_a2"].T, ((0, 0), (0, 128 - lowrank)))
    o_gla = _gla(proj, projt, ga, wa2t, p["gla_b_a"][:, None], p["gla_onorm_g"][None, :],
                 _tile(s, 512), col_of["gq"], col_of["gv"], col_of["gg"], row_of["gk"])

    invf = ROPE_THETA ** (-jnp.arange(0, ROT_DIM, 2, dtype=F32) / ROT_DIM)
    qat, qbt, kr, vte = _qkprep(projt, pos_row, invf[:, None], p["diff_qnorm_g"][:, None],
                                p["diff_knorm_g"][:, None], _tile(s, 512), row_of["dq"],
                                row_of["dk"], row_of["dv"])
    tq = _tile(s, 512)
    o_diff = _diffattn(qat, qbt, kr, vte, p["diff_lq1"][None, :], p["diff_lk1"][None, :],
                       p["diff_lq2"][None, :], p["diff_lk2"][None, :],
                       p["diff_subln_g"][:, None], lambda_init, tq, tq)

    x1, hf, hfp = _mergeout(o_gla, o_diff, proj, x, p["w_branch_gla"].astype(BF16),
                       p["w_branch_diff"].astype(BF16), p["w_out"].astype(BF16), gt_a,
                       p["norm2_g"][None, :], sc_f, sh_f, _tile(s, 512),
                       col_of["mg"], col_of["md"])

    e = p["w_router"].shape[1]
    idx, wts, rnk, cnt = _route(hf, p["w_router"].T, p["router_bias"][:, None], _tile(s, 512))

    counts = cnt[:, 0]
    pcounts = ((counts + MOE_ROWS - 1) // MOE_ROWS) * MOE_ROWS
    pend = jnp.cumsum(pcounts)
    pstart = pend - pcounts
    pos = _positions(idx, rnk, pstart[:, None], _tile(s, 512))
    n_items = (s * TOP_K) // MOE_ROWS + e
    n_valid = (pend[-1] // MOE_ROWS).astype(jnp.int32)
    item_b = jnp.minimum(jnp.arange(n_items, dtype=jnp.int32), n_valid - 1)
    item_e = jnp.minimum(jnp.sum(pend[None, :] <= (item_b * MOE_ROWS)[:, None], axis=1),
                         e - 1).astype(jnp.int32)

    wn = SC_GATHER_WINDOW
    pos_w = pos.reshape(TOP_K, s // wn, wn).transpose(1, 0, 2).reshape(s // wn * TOP_K, wn)
    xs = _sc_scatter_rows(hfp, pos_w, n_items * MOE_ROWS)
    item_rows = jnp.clip(pstart[item_e] + counts[item_e] - item_b * MOE_ROWS, 0,
                         MOE_ROWS).astype(jnp.int32)
    prev_e = jnp.concatenate([jnp.full((1,), -1, jnp.int32), item_e[:-1]])
    item_first = ((jnp.arange(n_items) < n_valid) & (item_e != prev_e)).astype(jnp.int32)
    item_slot = ((jnp.cumsum(item_first) - 1) % 2).astype(jnp.int32)
    cand = jnp.where(pcounts > 0, jnp.arange(e, dtype=jnp.int32), e)
    following = jnp.concatenate([lax.cummin(cand[::-1])[::-1][1:], jnp.full((1,), e, jnp.int32)])
    item_next = jnp.where(following[item_e] < e, following[item_e], -1).astype(jnp.int32)
    ys = _moe(item_e, item_b, item_first, item_slot, item_next, item_rows, n_valid[None], xs,
              p["w_exp_gate"], p["w_exp_up"], p["w_exp_down"])
    gathered = _sc_gather_rows(ys, pos.reshape(1, TOP_K * s)).reshape(TOP_K, s, d // 2)
    return _combine(wts.T, hfp, x1, gt_f, p["w_sh_gate"].astype(BF16),
                    p["w_sh_up"].astype(BF16), p["w_sh_down"].astype(BF16), gathered,
                    _tile(s, 256))


_LAYER_PARAMS = ("w_ada", "b_ada", "norm1_g", "w_in", "gla_w_a2", "gla_b_a", "gla_onorm_g",
                 "diff_qnorm_g", "diff_knorm_g", "diff_lq1", "diff_lk1", "diff_lq2", "diff_lk2",
                 "diff_subln_g", "w_branch_gla", "w_branch_diff", "w_out", "norm2_g", "w_router",
                 "router_bias", "w_exp_gate", "w_exp_up", "w_exp_down", "w_sh_gate", "w_sh_up",
                 "w_sh_down")


def kernel(x, c, positions, w_ada, b_ada, norm1_g, w_in, gla_w_a2, gla_b_a, gla_onorm_g, diff_qnorm_g, diff_knorm_g, diff_lq1, diff_lk1, diff_lq2, diff_lk2, diff_subln_g, w_branch_gla, w_branch_diff, w_out, norm2_g, w_router, router_bias, w_exp_gate, w_exp_up, w_exp_down, w_sh_gate, w_sh_up, w_sh_down):
    stacked = dict(zip(_LAYER_PARAMS, (
        w_ada, b_ada, norm1_g, w_in, gla_w_a2, gla_b_a, gla_onorm_g, diff_qnorm_g, diff_knorm_g,
        diff_lq1, diff_lk1, diff_lq2, diff_lk2, diff_subln_g, w_branch_gla, w_branch_diff, w_out,
        norm2_g, w_router, router_bias, w_exp_gate, w_exp_up, w_exp_down, w_sh_gate, w_sh_up,
        w_sh_down)))
    b, s, d = x.shape
    assert b == 1, "single-sequence kernel"
    xl = x[0]
    c_col = c[0][:, None]
    pos_row = positions.astype(jnp.int32)
    for l in range(w_ada.shape[0]):
        xl = _layer(l, xl, c_col, pos_row, {k: v[l] for k, v in stacked.items()})
    return xl[None]
```

```python
import functools
import math

import jax
import jax.numpy as jnp
from jax import lax
from jax.experimental import pallas as pl
from jax.experimental.pallas import tpu as pltpu
from jax.experimental.pallas import tpu_sc as plsc

CHUNK = 64
EPS = 1e-6
GLA_HEADS = 4
GLA_DK = 128
GLA_DV = 256
GLA_TAU = 16.0
DIFF_HEADS = 8
DIFF_DH = 64
DIFF_DV = 2 * DIFF_DH
ROPE_THETA = 500000.0
ROT_DIM = DIFF_DH // 4
N_GROUPS = 8
TOPK_GROUPS = 4
TOP_K = 8
ROUTED_SCALE = 2.5

MOE_ROWS = 256
SC_GATHER_WINDOW = 64
VMEM_LIMIT = 56 * 1024 * 1024
NEG_BIG = -1e30
LOG2E = 1.4426950408889634
HIGHEST = lax.Precision.HIGHEST
F32 = jnp.float32
BF16 = jnp.bfloat16


def _cparams(sem):
    return pltpu.CompilerParams(dimension_semantics=sem, vmem_limit_bytes=VMEM_LIMIT)


def _nt_dot(a, b, precision=None):
    return lax.dot_general(a, b, (((1,), (1,)), ((), ())), precision=precision,
                           preferred_element_type=F32)


def _pack_halves(x):
    n = x.shape[1] // 2
    lo = pltpu.bitcast(x[:, :n].astype(BF16).astype(F32), jnp.uint32) >> 16
    hi = pltpu.bitcast(x[:, n:].astype(BF16).astype(F32), jnp.uint32) & jnp.uint32(0xFFFF0000)
    return lo | hi


def _unpack_halves(w):
    return (pltpu.bitcast(w << 16, F32), pltpu.bitcast(w & jnp.uint32(0xFFFF0000), F32))


def _rms_mod(x, g, sc, sh):
    xn = x * lax.rsqrt(jnp.mean(x * x, axis=-1, keepdims=True) + EPS)
    return (xn * g) * (1.0 + sc) + sh


def _ada_kernel(c_ref, w_ref, b_ref, o_ref):
    c = c_ref[...]
    ca = c * jax.nn.sigmoid(c)
    o_ref[...] = jnp.sum(ca * w_ref[...], axis=0, keepdims=True) + b_ref[...]


def _ada(c_col, w, b):
    d, n = w.shape
    tn = min(1024, n)
    return pl.pallas_call(
        _ada_kernel,
        grid=(n // tn,),
        in_specs=[pl.BlockSpec((d, 1), lambda j: (0, 0)),
                  pl.BlockSpec((d, tn), lambda j: (0, j)),
                  pl.BlockSpec((1, tn), lambda j: (0, j))],
        out_specs=pl.BlockSpec((1, tn), lambda j: (0, j)),
        out_shape=jax.ShapeDtypeStruct((1, n), F32),
        compiler_params=_cparams(("arbitrary",)),
        name="ada",
    )(c_col, w, b)


def _inproj_kernel(x_ref, g_ref, sc_ref, sh_ref, w_ref, wga_ref, o_ref, ga_ref, h_scr):
    @pl.when(pl.program_id(1) == 0)
    def _():
        h = _rms_mod(x_ref[...], g_ref[...], sc_ref[...], sh_ref[...]).astype(BF16)
        h_scr[...] = h
        ga_ref[...] = jnp.dot(h, wga_ref[...], preferred_element_type=F32)

    o_ref[...] = jnp.dot(h_scr[...], w_ref[...], preferred_element_type=F32).astype(BF16)


def _inproj(x, g, sc, sh, w, wga, tm, tn):
    s, d = x.shape
    n = w.shape[1]
    vec = pl.BlockSpec((1, d), lambda i, j: (0, 0))
    return pl.pallas_call(
        _inproj_kernel,
        grid=(s // tm, n // tn),
        in_specs=[pl.BlockSpec((tm, d), lambda i, j: (i, 0)), vec, vec, vec,
                  pl.BlockSpec((d, tn), lambda i, j: (0, j)),
                  pl.BlockSpec((d, 128), lambda i, j: (0, 0))],
        out_specs=[pl.BlockSpec((tm, tn), lambda i, j: (i, j)),
                   pl.BlockSpec((tm, 128), lambda i, j: (i, 0))],
        out_shape=[jax.ShapeDtypeStruct((s, n), BF16), jax.ShapeDtypeStruct((s, 128), F32)],
        scratch_shapes=[pltpu.VMEM((tm, d), BF16)],
        compiler_params=_cparams(("arbitrary", "arbitrary")),
        name="inproj",
    )(x, g, sc, sh, w, wga)


def _inproj_t_kernel(x_ref, g_ref, sc_ref, sh_ref, wt_ref, o_ref, h_scr):
    @pl.when(pl.program_id(1) == 0)
    def _():
        h_scr[...] = _rms_mod(x_ref[...], g_ref[...], sc_ref[...], sh_ref[...]).astype(BF16)

    o_ref[...] = _nt_dot(wt_ref[...], h_scr[...]).astype(BF16)


def _inproj_t(x, g, sc, sh, wt, tm, tn):
    s, d = x.shape
    n = wt.shape[0]
    vec = pl.BlockSpec((1, d), lambda i, j: (0, 0))
    return pl.pallas_call(
        _inproj_t_kernel,
        grid=(s // tm, n // tn),
        in_specs=[pl.BlockSpec((tm, d), lambda i, j: (i, 0)), vec, vec, vec,
                  pl.BlockSpec((tn, d), lambda i, j: (j, 0))],
        out_specs=pl.BlockSpec((tn, tm), lambda i, j: (j, i)),
        out_shape=jax.ShapeDtypeStruct((n, s), BF16),
        scratch_shapes=[pltpu.VMEM((tm, d), BF16)],
        compiler_params=_cparams(("arbitrary", "arbitrary")),
        name="inproj_t",
    )(x, g, sc, sh, wt)


def _gla_kernel(q_ref, kt_ref, v_ref, gg_ref, ga_ref, wa2t_ref, ba_ref, on_ref, o_ref,
                state_ref, o_scr):
    tt = q_ref.shape[0]
    nchunk = tt // CHUNK

    @pl.when(pl.program_id(0) == 0)
    def _():
        state_ref[...] = jnp.zeros_like(state_ref)

    zt = _nt_dot(wa2t_ref[...], ga_ref[...], precision=HIGHEST) + ba_ref[...]
    lat = (jnp.minimum(zt, 0.0) - jnp.log1p(jnp.exp(-jnp.abs(zt)))) * (1.0 / GLA_TAU)
    row = lax.broadcasted_iota(jnp.int32, (tt, tt), 0)
    col = lax.broadcasted_iota(jnp.int32, (tt, tt), 1)
    same = (row // CHUNK) == (col // CHUNK)
    incl = jnp.where(same & (row <= col), 1.0, 0.0).astype(F32)
    full = jnp.where(same, 1.0, 0.0).astype(F32)
    cumt = jnp.dot(lat, incl, precision=HIGHEST, preferred_element_type=F32)
    tott = jnp.dot(lat, full, precision=HIGHEST, preferred_element_type=F32)
    kdt = kt_ref[...].astype(F32) * jnp.exp(tott - cumt)
    dec = jnp.exp(tott)

    lane = lax.broadcasted_iota(jnp.int32, (GLA_DK, 2 * CHUNK), 1)
    for c in range(nchunk):
        pair = (c // 2) * 2 * CHUNK
        if nchunk > 1:
            keep = (lane // CHUNK) == (c % 2)
        for h in range(GLA_HEADS):
            rows = slice(h * GLA_DK, (h + 1) * GLA_DK)
            vcols = slice(h * GLA_DV, (h + 1) * GLA_DV)
            if nchunk > 1:
                a = jnp.where(keep, kdt[rows, pair:pair + 2 * CHUNK], 0.0).astype(BF16)
                vp = v_ref[pair:pair + 2 * CHUNK, vcols]
            else:
                a = kdt[rows, :].astype(BF16)
                vp = v_ref[:, vcols]
            upd = jnp.dot(a, vp, preferred_element_type=F32)
            dcol = dec[rows, c * CHUNK:c * CHUNK + 1]
            st = state_ref[h] * dcol + upd
            state_ref[h] = st
            qc = q_ref[c * CHUNK:(c + 1) * CHUNK, rows]
            o_scr[c * CHUNK:(c + 1) * CHUNK, vcols] = jnp.dot(
                qc, st.astype(BF16), preferred_element_type=F32)

    for h in range(GLA_HEADS):
        vcols = slice(h * GLA_DV, (h + 1) * GLA_DV)
        o = o_scr[:, vcols] * (GLA_DK ** -0.5)
        o = o * lax.rsqrt(jnp.mean(o * o, axis=-1, keepdims=True) + EPS) * on_ref[...]
        g = gg_ref[:, vcols].astype(F32)
        o_ref[:, vcols] = (o * (g * jax.nn.sigmoid(g))).astype(BF16)


def _gla(proj, projt, ga, wa2t, ba_col, on_g, tt, col_q, col_v, col_g, row_k):
    s = proj.shape[0]
    qk = GLA_HEADS * GLA_DK
    vw = GLA_HEADS * GLA_DV
    return pl.pallas_call(
        _gla_kernel,
        grid=(s // tt,),
        in_specs=[pl.BlockSpec((tt, qk), lambda i: (i, col_q // qk)),
                  pl.BlockSpec((qk, tt), lambda i: (row_k // qk, i)),
                  pl.BlockSpec((tt, vw), lambda i: (i, col_v // vw)),
                  pl.BlockSpec((tt, vw), lambda i: (i, col_g // vw)),
                  pl.BlockSpec((tt, 128), lambda i: (i, 0)),
                  pl.BlockSpec((qk, 128), lambda i: (0, 0)),
                  pl.BlockSpec((qk, 1), lambda i: (0, 0)),
                  pl.BlockSpec((1, GLA_DV), lambda i: (0, 0))],
        out_specs=pl.BlockSpec((tt, vw), lambda i: (i, 0)),
        out_shape=jax.ShapeDtypeStruct((s, vw), BF16),
        scratch_shapes=[pltpu.VMEM((GLA_HEADS, GLA_DK, GLA_DV), F32),
                        pltpu.VMEM((tt, vw), F32)],
        compiler_params=_cparams(("arbitrary",)),
        name="gla",
    )(proj, projt, proj, proj, ga, wa2t, ba_col, on_g)


def _qknorm_rope_t(xt, g_col, cos, sin):
    n, tm = xt.shape
    x3 = xt.reshape(n // DIFF_DH, DIFF_DH, tm)
    r = lax.rsqrt(jnp.mean(x3 * x3, axis=1, keepdims=True) + EPS)
    y = x3 * r * g_col[None]
    half = ROT_DIM // 2
    y1, y2, rest = y[:, :half], y[:, half:ROT_DIM], y[:, ROT_DIM:]
    o1 = y1 * cos[None] - y2 * sin[None]
    o2 = y2 * cos[None] + y1 * sin[None]
    return jnp.concatenate([o1, o2, rest], axis=1)


def _qkprep_kernel(qt_ref, kt_ref, vt_ref, pos_ref, invf_ref, qg_ref, kg_ref, qa_ref, qb_ref,
                   ko_ref, ve_ref):
    tm = qt_ref.shape[1]
    v3 = vt_ref[...].reshape(DIFF_HEADS, DIFF_DV, tm)
    ones = jnp.ones((DIFF_HEADS, ATT_SUM_ROWS, tm), BF16)
    ve_ref[...] = jnp.concatenate([v3, ones], axis=1).reshape(-1, tm)
    ang = pos_ref[...].astype(F32) * invf_ref[...]
    cos, sin = jnp.cos(ang), jnp.sin(ang)
    k3 = _qknorm_rope_t(kt_ref[...].astype(F32), kg_ref[...], cos, sin)
    ko_ref[...] = k3.reshape(-1, tm).T.astype(BF16)
    q3 = _qknorm_rope_t(qt_ref[...].astype(F32), qg_ref[...], cos, sin) * (
        DIFF_DH ** -0.5 * LOG2E)
    seg = lax.broadcasted_iota(jnp.int32, q3.shape, 0)
    qa_ref[...] = jnp.where(seg % 2 == 0, q3, 0.0).reshape(-1, tm).astype(BF16)
    qb_ref[...] = jnp.where(seg % 2 == 1, q3, 0.0).reshape(-1, tm).astype(BF16)


def _qkprep(projt, pos_row, invf_col, qg_col, kg_col, tm, row_q, row_k, row_v):
    s = projt.shape[1]
    n = DIFF_HEADS * 2 * DIFF_DH
    ne = DIFF_HEADS * (DIFF_DV + ATT_SUM_ROWS)
    col = pl.BlockSpec((DIFF_DH, 1), lambda i: (0, 0))
    return pl.pallas_call(
        _qkprep_kernel,
        grid=(s // tm,),
        in_specs=[pl.BlockSpec((n, tm), lambda i: (row_q // n, i)),
                  pl.BlockSpec((n, tm), lambda i: (row_k // n, i)),
                  pl.BlockSpec((n, tm), lambda i: (row_v // n, i)),
                  pl.BlockSpec((1, tm), lambda i: (0, i)),
                  pl.BlockSpec((ROT_DIM // 2, 1), lambda i: (0, 0)), col, col],
        out_specs=[pl.BlockSpec((n, tm), lambda i: (0, i)),
                   pl.BlockSpec((n, tm), lambda i: (0, i)),
                   pl.BlockSpec((tm, n), lambda i: (i, 0)),
                   pl.BlockSpec((ne, tm), lambda i: (0, i))],
        out_shape=[jax.ShapeDtypeStruct((n, s), BF16), jax.ShapeDtypeStruct((n, s), BF16),
                   jax.ShapeDtypeStruct((s, n), BF16), jax.ShapeDtypeStruct((ne, s), BF16)],
        compiler_params=_cparams(("arbitrary",)),
        name="qkprep",
    )(projt, projt, projt, pos_row, invf_col, qg_col, kg_col)


ATT_COLS = 256
ATT_LOOKAHEAD = 3
ATT_SUM_ROWS = 16


def _diffattn_kernel(qa_ref, qb_ref, k_ref, vt_ref, lq1_ref, lk1_ref, lq2_ref, lk2_ref, sg_ref,
                     o_ref, *scr, lambda_init, tk, cols):
    tq = qa_ref.shape[1]
    nblk = 2 * tq // cols
    q_scr, m_scr, acc_scr = (scr[b * nblk:(b + 1) * nblk] for b in range(3))
    s_scr = scr[3 * nblk:]
    i = pl.program_id(1)
    for c in range(nblk):
        src = qa_ref if c * cols < tq else qb_ref
        off = (c * cols) % tq
        q_scr[c][...] = src[:, off:off + cols]
        m_scr[c][...] = jnp.full_like(m_scr[c], NEG_BIG)
        acc_scr[c][...] = jnp.zeros_like(acc_scr[c])

    def scores(j, c):
        start = pl.multiple_of(j * tk, tk)
        return jnp.dot(k_ref[pl.ds(start, tk), :], q_scr[c][...], preferred_element_type=F32)

    def steps(tiles, masked, next_tile):
        items = [(j, c) for j in tiles for c in range(nblk)]
        pending = []
        for n, (j, c) in enumerate(items):
            s = s_scr[n][...] if n < ATT_LOOKAHEAD else pending.pop(0)
            ahead = n + ATT_LOOKAHEAD
            if ahead < len(items):
                pending.append(scores(*items[ahead]))
            elif next_tile is not None:
                s_scr[ahead - len(items)][...] = scores(next_tile, ahead - len(items))
            start = pl.multiple_of(j * tk, tk)
            if masked:
                krow = lax.broadcasted_iota(jnp.int32, (tk, cols), 0)
                qcol = lax.broadcasted_iota(jnp.int32, (tk, cols), 1)
                qpos = i * tq + (c * cols) % tq + qcol
                s = jnp.where((start + krow) // CHUNK <= qpos // CHUNK, s, NEG_BIG)
            m_prev = m_scr[c][...]
            m_new = jnp.maximum(m_prev, jnp.max(s, axis=0, keepdims=True))
            alpha = jnp.exp2(m_prev - m_new)
            p = jnp.exp2((s - m_new).astype(BF16))
            acc_scr[c][...] = alpha * acc_scr[c][...] + jnp.dot(
                vt_ref[:, pl.ds(start, tk)], p, preferred_element_type=F32)
            m_scr[c][...] = m_new

    for c in range(ATT_LOOKAHEAD):
        s_scr[c][...] = scores(0, c)
    lax.fori_loop(0, i // 2,
                  lambda t, c: (steps((2 * t, 2 * t + 1), False, 2 * t + 2), c)[1], 0)

    @pl.when(i % 2 == 1)
    def _():
        steps((i - 1,), False, i)

    steps((i,), True, None)

    o = jnp.concatenate([acc_scr[c][:DIFF_DV] / acc_scr[c][DIFF_DV:DIFF_DV + 1]
                         for c in range(nblk)], axis=1)
    lam = (jnp.exp(jnp.sum(lq1_ref[...] * lk1_ref[...]))
           - jnp.exp(jnp.sum(lq2_ref[...] * lk2_ref[...])) + lambda_init)
    o = o[:, :tq] - lam * o[:, tq:]
    o = o * lax.rsqrt(jnp.mean(o * o, axis=0, keepdims=True) + EPS) * sg_ref[...]
    o_ref[...] = (o * (1.0 - lambda_init)).T.astype(BF16)


def _diffattn(qat, qbt, kr, vte, lq1, lk1, lq2, lk2, sg_col, lambda_init, tq, tk):
    s = kr.shape[0]
    hd = 2 * DIFF_DH
    vec = pl.BlockSpec((1, DIFF_DH), lambda h, i: (0, 0))
    cols = min(ATT_COLS, tq)
    nblk = 2 * tq // cols
    kern = functools.partial(_diffattn_kernel, lambda_init=lambda_init, tk=tk, cols=cols)
    dve = DIFF_DV + ATT_SUM_ROWS
    assert tq == tk and nblk >= ATT_LOOKAHEAD
    scratch = ([pltpu.VMEM((hd, cols), BF16)] * nblk + [pltpu.VMEM((1, cols), F32)] * nblk
               + [pltpu.VMEM((dve, cols), F32)] * nblk
               + [pltpu.VMEM((tk, cols), F32)] * ATT_LOOKAHEAD)
    return pl.pallas_call(
        kern,
        grid=(DIFF_HEADS, s // tq),
        in_specs=[pl.BlockSpec((hd, tq), lambda h, i: (h, i)),
                  pl.BlockSpec((hd, tq), lambda h, i: (h, i)),
                  pl.BlockSpec((s, hd), lambda h, i: (0, h)),
                  pl.BlockSpec((dve, s), lambda h, i: (h, 0)),
                  vec, vec, vec, vec,
                  pl.BlockSpec((DIFF_DV, 1), lambda h, i: (0, 0))],
        out_specs=pl.BlockSpec((tq, DIFF_DV), lambda h, i: (i, h)),
        out_shape=jax.ShapeDtypeStruct((s, DIFF_HEADS * DIFF_DV), BF16),
        scratch_shapes=scratch,
        compiler_params=_cparams(("arbitrary", "arbitrary")),
        name="diffattn",
    )(qat, qbt, kr, vte, lq1, lk1, lq2, lk2, sg_col)


def _mergeout_kernel(og_ref, od_ref, mg_ref, md_ref, x_ref, wbg_ref, wbd_ref, wo_ref, gt_ref,
                     g2_ref, sc_ref, sh_ref, x1_ref, hf_ref, hfp_ref):
    bg = jnp.dot(og_ref[...], wbg_ref[...], preferred_element_type=F32)
    bd = jnp.dot(od_ref[...], wbd_ref[...], preferred_element_type=F32)
    merged = (jax.nn.sigmoid(mg_ref[...].astype(F32)) * bg
              + jax.nn.sigmoid(md_ref[...].astype(F32)) * bd)
    x1 = x_ref[...] + gt_ref[...] * jnp.dot(merged.astype(BF16), wo_ref[...],
                                             preferred_element_type=F32)
    x1_ref[...] = x1
    hf = _rms_mod(x1, g2_ref[...], sc_ref[...], sh_ref[...])
    hf_ref[...] = hf
    hfp_ref[...] = _pack_halves(hf)


def _mergeout(og, od, proj, x, wbg, wbd, wo, gt, g2, sc, sh, tm, col_mg, col_md):
    s, d = x.shape
    vec = pl.BlockSpec((1, d), lambda i: (0, 0))
    wspec = pl.BlockSpec((d, d), lambda i: (0, 0))
    row = pl.BlockSpec((tm, d), lambda i: (i, 0))
    return pl.pallas_call(
        _mergeout_kernel,
        grid=(s // tm,),
        in_specs=[row, row,
                  pl.BlockSpec((tm, d), lambda i: (i, col_mg // d)),
                  pl.BlockSpec((tm, d), lambda i: (i, col_md // d)),
                  row, wspec, wspec, wspec, vec, vec, vec, vec],
        out_specs=[row, row, pl.BlockSpec((tm, d // 2), lambda i: (i, 0))],
        out_shape=[jax.ShapeDtypeStruct((s, d), F32), jax.ShapeDtypeStruct((s, d), F32),
                   jax.ShapeDtypeStruct((s, d // 2), jnp.uint32)],
        compiler_params=_cparams(("arbitrary",)),
        name="mergeout",
    )(og, od, proj, proj, x, wbg, wbd, wo, gt, g2, sc, sh)


def _route_kernel(hf_ref, wrt_ref, bias_ref, idx_ref, wts_ref, rnk_ref, cnt_ref, run_scr):
    tr = hf_ref.shape[0]
    e = wrt_ref.shape[0]
    gsz = e // N_GROUPS

    @pl.when(pl.program_id(0) == 0)
    def _():
        run_scr[...] = jnp.zeros_like(run_scr)

    logits = _nt_dot(wrt_ref[...], hf_ref[...], precision=HIGHEST)
    scores = jax.nn.sigmoid(logits)
    biased = scores + bias_ref[...]
    g3 = biased.reshape(N_GROUPS, gsz, tr)
    m1 = jnp.max(g3, axis=1, keepdims=True)
    n_top = jnp.sum(jnp.where(g3 == m1, 1.0, 0.0), axis=1, keepdims=True)
    m2 = jnp.max(jnp.where(g3 < m1, g3, -jnp.inf), axis=1, keepdims=True)
    gs = (m1 + jnp.where(n_top >= 2.0, m1, m2)).reshape(N_GROUPS, tr)
    gi = lax.broadcasted_iota(jnp.int32, (N_GROUPS, tr), 0)
    beaten = jnp.zeros((N_GROUPS, tr), F32)
    for g in range(N_GROUPS):
        other = gs[g:g + 1, :]
        beaten = beaten + jnp.where((other > gs) | ((other == gs) & (g < gi)), 1.0, 0.0)
    gsel = (beaten < float(TOPK_GROUPS)).reshape(N_GROUPS, 1, tr)
    masked = jnp.where(gsel, g3, -jnp.inf).reshape(e, tr)

    ids = lax.broadcasted_iota(jnp.int32, (e, tr), 0)
    chosen = jnp.zeros((e, tr), F32)
    sel_idx, sel_score = [], []
    for _ in range(TOP_K):
        mx = jnp.max(masked, axis=0, keepdims=True)
        ix = jnp.min(jnp.where(masked == mx, ids, e), axis=0, keepdims=True)
        hit = ids == ix
        sel_idx.append(ix)
        sel_score.append(jnp.sum(jnp.where(hit, scores, 0.0), axis=0, keepdims=True))
        chosen = jnp.where(hit, 1.0, chosen)
        masked = jnp.where(hit, -jnp.inf, masked)
    idx = jnp.concatenate(sel_idx, axis=0)
    sc = jnp.concatenate(sel_score, axis=0)
    idx_ref[...] = idx
    wts_ref[...] = sc / jnp.sum(sc, axis=0, keepdims=True) * ROUTED_SCALE

    row = lax.broadcasted_iota(jnp.int32, (tr, tr), 0)
    col = lax.broadcasted_iota(jnp.int32, (tr, tr), 1)
    before = jnp.where(row < col, 1.0, 0.0).astype(BF16)
    prior = jnp.dot(chosen.astype(BF16), before, preferred_element_type=F32) + run_scr[:, 0:1]
    rnk_ref[...] = jnp.concatenate(
        [jnp.sum(jnp.where(ids == sel_idx[k], prior, 0.0), axis=0, keepdims=True)
         for k in range(TOP_K)], axis=0).astype(jnp.int32)
    run_scr[...] = run_scr[...] + jnp.sum(chosen, axis=1, keepdims=True)
    cnt_ref[...] = run_scr[...].astype(jnp.int32)


def _route(hf, wrt, bias_col, tr):
    s, d = hf.shape
    e = wrt.shape[0]
    tok = pl.BlockSpec((TOP_K, tr), lambda i: (0, i))
    return pl.pallas_call(
        _route_kernel,
        grid=(s // tr,),
        in_specs=[pl.BlockSpec((tr, d), lambda i: (i, 0)),
                  pl.BlockSpec((e, d), lambda i: (0, 0)),
                  pl.BlockSpec((e, 1), lambda i: (0, 0))],
        out_specs=[tok, tok, tok, pl.BlockSpec((e, 128), lambda i: (0, 0))],
        out_shape=[jax.ShapeDtypeStruct((TOP_K, s), jnp.int32),
                   jax.ShapeDtypeStruct((TOP_K, s), F32),
                   jax.ShapeDtypeStruct((TOP_K, s), jnp.int32),
                   jax.ShapeDtypeStruct((e, 128), jnp.int32)],
        scratch_shapes=[pltpu.VMEM((e, 128), F32)],
        compiler_params=_cparams(("arbitrary",)),
        name="route",
    )(hf, wrt, bias_col)


def _positions_kernel(idx_ref, rnk_ref, pstart_ref, pos_ref):
    e = pstart_ref.shape[0]
    ts = idx_ref.shape[1]
    ids = lax.broadcasted_iota(jnp.int32, (e, ts), 0)
    idx = idx_ref[...]
    pos_ref[...] = rnk_ref[...] + jnp.concatenate(
        [jnp.sum(jnp.where(ids == idx[k:k + 1, :], pstart_ref[...], 0), axis=0, keepdims=True)
         for k in range(TOP_K)], axis=0)


def _positions(idx, rnk, pstart_col, ts):
    s = idx.shape[1]
    e = pstart_col.shape[0]
    tok = pl.BlockSpec((TOP_K, ts), lambda i: (0, i))
    return pl.pallas_call(
        _positions_kernel,
        grid=(s // ts,),
        in_specs=[tok, tok, pl.BlockSpec((e, 1), lambda i: (0, 0))],
        out_specs=tok,
        out_shape=jax.ShapeDtypeStruct((TOP_K, s), jnp.int32),
        compiler_params=_cparams(("arbitrary",)),
        name="positions",
    )(idx, rnk, pstart_col)


def _swiglu_packed(xp, wg, wu, wd):
    lo, hi = _unpack_halves(xp)
    lo, hi = lo.astype(BF16), hi.astype(BF16)
    n = lo.shape[1]
    g = (jnp.dot(lo, wg[:n], preferred_element_type=F32)
         + jnp.dot(hi, wg[n:], preferred_element_type=F32))
    u = (jnp.dot(lo, wu[:n], preferred_element_type=F32)
         + jnp.dot(hi, wu[n:], preferred_element_type=F32))
    h = (g * jax.nn.sigmoid(g)) * u
    return jnp.dot(h.astype(BF16), wd[...], preferred_element_type=F32)


def _moe_kernel(ie_ref, ib_ref, first_ref, slot_ref, ne_ref, rows_ref, nv_ref, xs_ref, wg_hbm,
                wu_hbm, wd_hbm, ys_ref, wg_f, wu_f, wd_f, wg_b, wu_b, wd_b, sem):
    del ib_ref
    i = pl.program_id(0)

    def fetch(e, slot):
        return [pltpu.make_async_copy(src.at[e], dst.at[slot], sem.at[slot, n])
                for n, (src, dst) in enumerate(((wg_hbm, wg_f), (wu_hbm, wu_f), (wd_hbm, wd_f)))]

    @pl.when(i == 0)
    def _():
        for cp in fetch(ie_ref[0], 0):
            cp.start()

    @pl.when(i < nv_ref[0])
    def _():
        for slot in range(2):
            @pl.when((first_ref[i] == 1) & (slot_ref[i] == slot))
            def _():
                for cp in fetch(ie_ref[i], slot):
                    cp.wait()
                wg_b[...] = wg_f[slot].astype(BF16)
                wu_b[...] = wu_f[slot].astype(BF16)
                wd_b[...] = wd_f[slot].astype(BF16)

                @pl.when(ne_ref[i] >= 0)
                def _():
                    for cp in fetch(ne_ref[i], 1 - slot):
                        cp.start()

        row = lax.broadcasted_iota(jnp.int32, xs_ref.shape, 0)
        xp = jnp.where(row < rows_ref[i], xs_ref[...], jnp.uint32(0))
        ys_ref[...] = _pack_halves(_swiglu_packed(xp, wg_b, wu_b, wd_b))


def _moe(item_e, item_b, item_first, item_slot, item_next, item_rows, n_valid, xs, wg, wu, wd):
    m_pad, dh = xs.shape
    _, d, f = wg.shape
    n_items = item_e.shape[0]
    blk = lambda i, ie, ib, fi, sl, ne, nr, nv: (ib[i], 0)
    hbm = pl.BlockSpec(memory_space=pl.ANY)
    return pl.pallas_call(
        _moe_kernel,
        grid_spec=pltpu.PrefetchScalarGridSpec(
            num_scalar_prefetch=7,
            grid=(n_items,),
            in_specs=[pl.BlockSpec((MOE_ROWS, dh), blk), hbm, hbm, hbm],
            out_specs=pl.BlockSpec((MOE_ROWS, dh), blk),
            scratch_shapes=[pltpu.VMEM((2, d, f), F32), pltpu.VMEM((2, d, f), F32),
                            pltpu.VMEM((2, f, d), F32),
                            pltpu.VMEM((d, f), BF16), pltpu.VMEM((d, f), BF16),
                            pltpu.VMEM((f, d), BF16),
                            pltpu.SemaphoreType.DMA((2, 3))],
        ),
        out_shape=jax.ShapeDtypeStruct((m_pad, dh), jnp.uint32),
        compiler_params=_cparams(("arbitrary",)),
        name="moe",
    )(item_e, item_b, item_first, item_slot, item_next, item_rows, n_valid, xs, wg, wu, wd)


def _sc_gather_rows(table, idx_row):
    m = idx_row.shape[1]
    w = table.shape[1]
    idx_row = idx_row.reshape(m // SC_GATHER_WINDOW, SC_GATHER_WINDOW)
    mesh = plsc.VectorSubcoreMesh(core_axis_name="c", subcore_axis_name="s")

    @functools.partial(pl.kernel, mesh=mesh,
                       out_type=jax.ShapeDtypeStruct((m, w), table.dtype))
    def gather(table_hbm, idx_hbm, out_hbm):
        def body(idx_vmem, out_vmem):
            pltpu.sync_copy(table_hbm.at[idx_vmem.at[0]], out_vmem)

        pltpu.emit_pipeline(
            body,
            grid=(m // SC_GATHER_WINDOW,),
            in_specs=[pl.BlockSpec((1, SC_GATHER_WINDOW), lambda i: (i, 0))],
            out_specs=[pl.BlockSpec((SC_GATHER_WINDOW, w), lambda i: (i, 0))],
            core_axis_name=("c", "s"),
            dimension_semantics=(pltpu.PARALLEL,),
        )(idx_hbm, out_hbm)

    return gather(table, idx_row)


def _sc_scatter_rows(rows, idx_blocks, m_out):
    s, w = rows.shape
    mesh = plsc.VectorSubcoreMesh(core_axis_name="c", subcore_axis_name="s")

    @functools.partial(pl.kernel, mesh=mesh,
                       out_type=jax.ShapeDtypeStruct((m_out, w), rows.dtype))
    def scatter(rows_hbm, idx_hbm, out_hbm):
        def body(rows_vmem, idx_vmem):
            for k in range(TOP_K):
                pltpu.sync_copy(rows_vmem, out_hbm.at[idx_vmem.at[k]])

        pltpu.emit_pipeline(
            body,
            grid=(s // SC_GATHER_WINDOW,),
            in_specs=[pl.BlockSpec((SC_GATHER_WINDOW, w), lambda i: (i, 0)),
                      pl.BlockSpec((TOP_K, SC_GATHER_WINDOW), lambda i: (i, 0))],
            out_specs=[],
            core_axis_name=("c", "s"),
            dimension_semantics=(pltpu.PARALLEL,),
        )(rows_hbm, idx_hbm)

    return scatter(rows, idx_blocks)


def _combine_kernel(wt_ref, hf_ref, x1_ref, gt_ref, sg_ref, su_ref, sd_ref, g_ref, o_ref):
    tc = x1_ref.shape[0]
    y = _swiglu_packed(hf_ref[...], sg_ref, su_ref, sd_ref)
    wt = wt_ref[...]
    n = g_ref.shape[2]
    r_lo = jnp.zeros((tc, n), F32)
    r_hi = jnp.zeros((tc, n), F32)
    for k in range(TOP_K):
        lo, hi = _unpack_halves(g_ref[k])
        r_lo = r_lo + lo * wt[:, k:k + 1]
        r_hi = r_hi + hi * wt[:, k:k + 1]
    y = y + jnp.concatenate([r_lo, r_hi], axis=1)
    o_ref[...] = x1_ref[...] + gt_ref[...] * y


def _combine(wts_t, hfp, x1, gt, sg, su, sd, gathered, tc):
    s, d = x1.shape
    f = sg.shape[1]
    row = pl.BlockSpec((tc, d), lambda i: (i, 0))
    return pl.pallas_call(
        _combine_kernel,
        grid=(s // tc,),
        in_specs=[pl.BlockSpec((tc, TOP_K), lambda i: (i, 0)),
                  pl.BlockSpec((tc, d // 2), lambda i: (i, 0)), row,
                  pl.BlockSpec((1, d), lambda i: (0, 0)),
                  pl.BlockSpec((d, f), lambda i: (0, 0)),
                  pl.BlockSpec((d, f), lambda i: (0, 0)),
                  pl.BlockSpec((f, d), lambda i: (0, 0)),
                  pl.BlockSpec((TOP_K, tc, d // 2), lambda i: (0, i, 0))],
        out_specs=row,
        out_shape=jax.ShapeDtypeStruct((s, d), F32),
        compiler_params=_cparams(("arbitrary",)),
        name="combine",
    )(wts_t, hfp, x1, gt, sg, su, sd, gathered)


def _tile(n, want):
    t = min(n, want)
    assert n % t == 0, (n, t)
    return t


def _layer(l, x, c_col, pos_row, p):
    s, d = x.shape
    lambda_init = 0.8 - 0.6 * math.exp(-0.3 * l)
    gqk, gv = GLA_HEADS * GLA_DK, GLA_HEADS * GLA_DV
    dqk, dvw = DIFF_HEADS * 2 * DIFF_DH, DIFF_HEADS * DIFF_DV
    lowrank = p["gla_w_a2"].shape[0]

    mod = _ada(c_col, p["w_ada"], p["b_ada"][None, :])
    sh_a, sc_a, gt_a, sh_f, sc_f, gt_f = [mod[:, j * d:(j + 1) * d] for j in range(6)]

    w_in = p["w_in"]
    o = 0
    cols = {}
    for name, wdt in (("gq", gqk), ("gk", gqk), ("gv", gv), ("ga", lowrank), ("gg", gv),
                      ("dq", dqk), ("dk", dqk), ("dv", dvw), ("mg", d), ("md", d)):
        cols[name] = w_in[:, o:o + wdt]
        o += wdt
    row_names = ("gv", "gg", "mg", "md", "gq")
    w_row = jnp.concatenate([cols[n] for n in row_names], axis=1).astype(BF16)
    col_of, o = {}, 0
    for n in row_names:
        col_of[n] = o
        o += cols[n].shape[1]
    t_names = ("dq", "dk", "dv", "gk")
    w_t = jnp.concatenate([cols[n] for n in t_names], axis=1).T.astype(BF16)
    row_of, o = {}, 0
    for n in t_names:
        row_of[n] = o
        o += cols[n].shape[1]
    w_ga = jnp.pad(cols["ga"], ((0, 0), (0, 128 - lowrank))).astype(BF16)

    g1 = p["norm1_g"][None, :]
    tm = _tile(s, 1024)
    proj, ga = _inproj(x, g1, sc_a, sh_a, w_row, w_ga, tm, 512)
    projt = _inproj_t(x, g1, sc_a, sh_a, w_t, tm, 512)

    wa2t = jnp.pad(p["gla_w_a2"].T, ((0, 0), (0, 128 - lowrank)))
    o_gla = _gla(proj, projt, ga, wa2t, p["gla_b_a"][:, None], p["gla_onorm_g"][None, :],
                 _tile(s, 512), col_of["gq"], col_of["gv"], col_of["gg"], row_of["gk"])

    invf = ROPE_THETA ** (-jnp.arange(0, ROT_DIM, 2, dtype=F32) / ROT_DIM)
    qat, qbt, kr, vte = _qkprep(projt, pos_row, invf[:, None], p["diff_qnorm_g"][:, None],
                                p["diff_knorm_g"][:, None], _tile(s, 512), row_of["dq"],
                                row_of["dk"], row_of["dv"])
    tq = _tile(s, 512)
    o_diff = _diffattn(qat, qbt, kr, vte, p["diff_lq1"][None, :], p["diff_lk1"][None, :],
                       p["diff_lq2"][None, :], p["diff_lk2"][None, :],
                       p["diff_subln_g"][:, None], lambda_init, tq, tq)

    x1, hf, hfp = _mergeout(o_gla, o_diff, proj, x, p["w_branch_gla"].astype(BF16),
                       p["w_branch_diff"].astype(BF16), p["w_out"].astype(BF16), gt_a,
                       p["norm2_g"][None, :], sc_f, sh_f, _tile(s, 512),
                       col_of["mg"], col_of["md"])

    e = p["w_router"].shape[1]
    idx, wts, rnk, cnt = _route(hf, p["w_router"].T, p["router_bias"][:, None], _tile(s, 512))

    counts = cnt[:, 0]
    pcounts = ((counts + MOE_ROWS - 1) // MOE_ROWS) * MOE_ROWS
    pend = jnp.cumsum(pcounts)
    pstart = pend - pcounts
    pos = _positions(idx, rnk, pstart[:, None], _tile(s, 512))
    n_items = (s * TOP_K) // MOE_ROWS + e
    n_valid = (pend[-1] // MOE_ROWS).astype(jnp.int32)
    item_b = jnp.minimum(jnp.arange(n_items, dtype=jnp.int32), n_valid - 1)
    item_e = jnp.minimum(jnp.sum(pend[None, :] <= (item_b * MOE_ROWS)[:, None], axis=1),
                         e - 1).astype(jnp.int32)

    wn = SC_GATHER_WINDOW
    pos_w = pos.reshape(TOP_K, s // wn, wn).transpose(1, 0, 2).reshape(s // wn * TOP_K, wn)
    xs = _sc_scatter_rows(hfp, pos_w, n_items * MOE_ROWS)
    item_rows = jnp.clip(pstart[item_e] + counts[item_e] - item_b * MOE_ROWS, 0,
                         MOE_ROWS).astype(jnp.int32)
    prev_e = jnp.concatenate([jnp.full((1,), -1, jnp.int32), item_e[:-1]])
    item_first = ((jnp.arange(n_items) < n_valid) & (item_e != prev_e)).astype(jnp.int32)
    item_slot = ((jnp.cumsum(item_first) - 1) % 2).astype(jnp.int32)
    cand = jnp.where(pcounts > 0, jnp.arange(e, dtype=jnp.int32), e)
    following = jnp.concatenate([lax.cummin(cand[::-1])[::-1][1:], jnp.full((1,), e, jnp.int32)])
    item_next = jnp.where(following[item_e] < e, following[item_e], -1).astype(jnp.int32)
    ys = _moe(item_e, item_b, item_first, item_slot, item_next, item_rows, n_valid[None], xs,
              p["w_exp_gate"], p["w_exp_up"], p["w_exp_down"])
    gathered = _sc_gather_rows(ys, pos.reshape(1, TOP_K * s)).reshape(TOP_K, s, d // 2)
    return _combine(wts.T, hfp, x1, gt_f, p["w_sh_gate"].astype(BF16),
                    p["w_sh_up"].astype(BF16), p["w_sh_down"].astype(BF16), gathered,
                    _tile(s, 256))


_LAYER_PARAMS = ("w_ada", "b_ada", "norm1_g", "w_in", "gla_w_a2", "gla_b_a", "gla_onorm_g",
                 "diff_qnorm_g", "diff_knorm_g", "diff_lq1", "diff_lk1", "diff_lq2", "diff_lk2",
                 "diff_subln_g", "w_branch_gla", "w_branch_diff", "w_out", "norm2_g", "w_router",
                 "router_bias", "w_exp_gate", "w_exp_up", "w_exp_down", "w_sh_gate", "w_sh_up",
                 "w_sh_down")


def kernel(x, c, positions, w_ada, b_ada, norm1_g, w_in, gla_w_a2, gla_b_a, gla_onorm_g, diff_qnorm_g, diff_knorm_g, diff_lq1, diff_lk1, diff_lq2, diff_lk2, diff_subln_g, w_branch_gla, w_branch_diff, w_out, norm2_g, w_router, router_bias, w_exp_gate, w_exp_up, w_exp_down, w_sh_gate, w_sh_up, w_sh_down):
    stacked = dict(zip(_LAYER_PARAMS, (
        w_ada, b_ada, norm1_g, w_in, gla_w_a2, gla_b_a, gla_onorm_g, diff_qnorm_g, diff_knorm_g,
        diff_lq1, diff_lk1, diff_lq2, diff_lk2, diff_subln_g, w_branch_gla, w_branch_diff, w_out,
        norm2_g, w_router, router_bias, w_exp_gate, w_exp_up, w_exp_down, w_sh_gate, w_sh_up,
        w_sh_down)))
    b, s, d = x.shape
    assert b == 1, "single-sequence kernel"
    xl = x[0]
    c_col = c[0][:, None]
    pos_row = positions.astype(jnp.int32)
    for l in range(w_ada.shape[0]):
        xl = _layer(l, xl, c_col, pos_row, {k: v[l] for k, v in stacked.items()})
    return xl[None]
```

```python
import functools
import math

import jax
import jax.numpy as jnp
from jax import lax
from jax.experimental import pallas as pl
from jax.experimental.pallas import tpu as pltpu
from jax.experimental.pallas import tpu_sc as plsc

CHUNK = 64
EPS = 1e-6
GLA_HEADS = 4
GLA_DK = 128
GLA_DV = 256
GLA_TAU = 16.0
DIFF_HEADS = 8
DIFF_DH = 64
DIFF_DV = 2 * DIFF_DH
ROPE_THETA = 500000.0
ROT_DIM = DIFF_DH // 4
N_GROUPS = 8
TOPK_GROUPS = 4
TOP_K = 8
ROUTED_SCALE = 2.5

MOE_ROWS = 256
SC_GATHER_WINDOW = 64
VMEM_LIMIT = 56 * 1024 * 1024
NEG_BIG = -1e30
LOG2E = 1.4426950408889634
HIGHEST = lax.Precision.HIGHEST
F32 = jnp.float32
BF16 = jnp.bfloat16


def _cparams(sem):
    return pltpu.CompilerParams(dimension_semantics=sem, vmem_limit_bytes=VMEM_LIMIT)


def _nt_dot(a, b, precision=None):
    return lax.dot_general(a, b, (((1,), (1,)), ((), ())), precision=precision,
                           preferred_element_type=F32)


def _pack_halves(x):
    n = x.shape[1] // 2
    lo = pltpu.bitcast(x[:, :n].astype(BF16).astype(F32), jnp.uint32) >> 16
    hi = pltpu.bitcast(x[:, n:].astype(BF16).astype(F32), jnp.uint32) & jnp.uint32(0xFFFF0000)
    return lo | hi


def _unpack_halves(w):
    return (pltpu.bitcast(w << 16, F32), pltpu.bitcast(w & jnp.uint32(0xFFFF0000), F32))


def _rms_mod(x, g, sc, sh):
    xn = x * lax.rsqrt(jnp.mean(x * x, axis=-1, keepdims=True) + EPS)
    return (xn * g) * (1.0 + sc) + sh


def _ada_kernel(c_ref, w_ref, b_ref, o_ref):
    c = c_ref[...]
    ca = c * jax.nn.sigmoid(c)
    o_ref[...] = jnp.sum(ca * w_ref[...], axis=0, keepdims=True) + b_ref[...]


def _ada(c_col, w, b):
    d, n = w.shape
    tn = min(1024, n)
    return pl.pallas_call(
        _ada_kernel,
        grid=(n // tn,),
        in_specs=[pl.BlockSpec((d, 1), lambda j: (0, 0)),
                  pl.BlockSpec((d, tn), lambda j: (0, j)),
                  pl.BlockSpec((1, tn), lambda j: (0, j))],
        out_specs=pl.BlockSpec((1, tn), lambda j: (0, j)),
        out_shape=jax.ShapeDtypeStruct((1, n), F32),
        compiler_params=_cparams(("arbitrary",)),
        name="ada",
    )(c_col, w, b)


def _inproj_kernel(x_ref, g_ref, sc_ref, sh_ref, w_ref, wga_ref, o_ref, ga_ref, h_scr):
    @pl.when(pl.program_id(1) == 0)
    def _():
        h = _rms_mod(x_ref[...], g_ref[...], sc_ref[...], sh_ref[...]).astype(BF16)
        h_scr[...] = h
        ga_ref[...] = jnp.dot(h, wga_ref[...], preferred_element_type=F32)

    o_ref[...] = jnp.dot(h_scr[...], w_ref[...], preferred_element_type=F32).astype(BF16)


def _inproj(x, g, sc, sh, w, wga, tm, tn):
    s, d = x.shape
    n = w.shape[1]
    vec = pl.BlockSpec((1, d), lambda i, j: (0, 0))
    return pl.pallas_call(
        _inproj_kernel,
        grid=(s // tm, n // tn),
        in_specs=[pl.BlockSpec((tm, d), lambda i, j: (i, 0)), vec, vec, vec,
                  pl.BlockSpec((d, tn), lambda i, j: (0, j)),
                  pl.BlockSpec((d, 128), lambda i, j: (0, 0))],
        out_specs=[pl.BlockSpec((tm, tn), lambda i, j: (i, j)),
                   pl.BlockSpec((tm, 128), lambda i, j: (i, 0))],
        out_shape=[jax.ShapeDtypeStruct((s, n), BF16), jax.ShapeDtypeStruct((s, 128), F32)],
        scratch_shapes=[pltpu.VMEM((tm, d), BF16)],
        compiler_params=_cparams(("arbitrary", "arbitrary")),
        name="inproj",
    )(x, g, sc, sh, w, wga)


def _inproj_t_kernel(x_ref, g_ref, sc_ref, sh_ref, wt_ref, o_ref, h_scr):
    @pl.when(pl.program_id(1) == 0)
    def _():
        h_scr[...] = _rms_mod(x_ref[...], g_ref[...], sc_ref[...], sh_ref[...]).astype(BF16)

    o_ref[...] = _nt_dot(wt_ref[...], h_scr[...]).astype(BF16)


def _inproj_t(x, g, sc, sh, wt, tm, tn):
    s, d = x.shape
    n = wt.shape[0]
    vec = pl.BlockSpec((1, d), lambda i, j: (0, 0))
    return pl.pallas_call(
        _inproj_t_kernel,
        grid=(s // tm, n // tn),
        in_specs=[pl.BlockSpec((tm, d), lambda i, j: (i, 0)), vec, vec, vec,
                  pl.BlockSpec((tn, d), lambda i, j: (j, 0))],
        out_specs=pl.BlockSpec((tn, tm), lambda i, j: (j, i)),
        out_shape=jax.ShapeDtypeStruct((n, s), BF16),
        scratch_shapes=[pltpu.VMEM((tm, d), BF16)],
        compiler_params=_cparams(("arbitrary", "arbitrary")),
        name="inproj_t",
    )(x, g, sc, sh, wt)


def _gla_kernel(q_ref, kt_ref, v_ref, gg_ref, ga_ref, wa2t_ref, ba_ref, on_ref, o_ref,
                state_ref, o_scr):
    tt = q_ref.shape[0]
    nchunk = tt // CHUNK

    @pl.when(pl.program_id(0) == 0)
    def _():
        state_ref[...] = jnp.zeros_like(state_ref)

    zt = _nt_dot(wa2t_ref[...], ga_ref[...], precision=HIGHEST) + ba_ref[...]
    lat = (jnp.minimum(zt, 0.0) - jnp.log1p(jnp.exp(-jnp.abs(zt)))) * (1.0 / GLA_TAU)
    row = lax.broadcasted_iota(jnp.int32, (tt, tt), 0)
    col = lax.broadcasted_iota(jnp.int32, (tt, tt), 1)
    same = (row // CHUNK) == (col // CHUNK)
    incl = jnp.where(same & (row <= col), 1.0, 0.0).astype(F32)
    full = jnp.where(same, 1.0, 0.0).astype(F32)
    cumt = jnp.dot(lat, incl, precision=HIGHEST, preferred_element_type=F32)
    tott = jnp.dot(lat, full, precision=HIGHEST, preferred_element_type=F32)
    kdt = kt_ref[...].astype(F32) * jnp.exp(tott - cumt)
    dec = jnp.exp(tott)

    lane = lax.broadcasted_iota(jnp.int32, (GLA_DK, 2 * CHUNK), 1)
    for c in range(nchunk):
        pair = (c // 2) * 2 * CHUNK
        if nchunk > 1:
            keep = (lane // CHUNK) == (c % 2)
        for h in range(GLA_HEADS):
            rows = slice(h * GLA_DK, (h + 1) * GLA_DK)
            vcols = slice(h * GLA_DV, (h + 1) * GLA_DV)
            if nchunk > 1:
                a = jnp.where(keep, kdt[rows, pair:pair + 2 * CHUNK], 0.0).astype(BF16)
                vp = v_ref[pair:pair + 2 * CHUNK, vcols]
            else:
                a = kdt[rows, :].astype(BF16)
                vp = v_ref[:, vcols]
            upd = jnp.dot(a, vp, preferred_element_type=F32)
            dcol = dec[rows, c * CHUNK:c * CHUNK + 1]
            st = state_ref[h] * dcol + upd
            state_ref[h] = st
            qc = q_ref[c * CHUNK:(c + 1) * CHUNK, rows]
            o_scr[c * CHUNK:(c + 1) * CHUNK, vcols] = jnp.dot(
                qc, st.astype(BF16), preferred_element_type=F32)

    for h in range(GLA_HEADS):
        vcols = slice(h * GLA_DV, (h + 1) * GLA_DV)
        o = o_scr[:, vcols] * (GLA_DK ** -0.5)
        o = o * lax.rsqrt(jnp.mean(o * o, axis=-1, keepdims=True) + EPS) * on_ref[...]
        g = gg_ref[:, vcols].astype(F32)
        o_ref[:, vcols] = (o * (g * jax.nn.sigmoid(g))).astype(BF16)


def _gla(proj, projt, ga, wa2t, ba_col, on_g, tt, col_q, col_v, col_g, row_k):
    s = proj.shape[0]
    qk = GLA_HEADS * GLA_DK
    vw = GLA_HEADS * GLA_DV
    return pl.pallas_call(
        _gla_kernel,
        grid=(s // tt,),
        in_specs=[pl.BlockSpec((tt, qk), lambda i: (i, col_q // qk)),
                  pl.BlockSpec((qk, tt), lambda i: (row_k // qk, i)),
                  pl.BlockSpec((tt, vw), lambda i: (i, col_v // vw)),
                  pl.BlockSpec((tt, vw), lambda i: (i, col_g // vw)),
                  pl.BlockSpec((tt, 128), lambda i: (i, 0)),
                  pl.BlockSpec((qk, 128), lambda i: (0, 0)),
                  pl.BlockSpec((qk, 1), lambda i: (0, 0)),
                  pl.BlockSpec((1, GLA_DV), lambda i: (0, 0))],
        out_specs=pl.BlockSpec((tt, vw), lambda i: (i, 0)),
        out_shape=jax.ShapeDtypeStruct((s, vw), BF16),
        scratch_shapes=[pltpu.VMEM((GLA_HEADS, GLA_DK, GLA_DV), F32),
                        pltpu.VMEM((tt, vw), F32)],
        compiler_params=_cparams(("arbitrary",)),
        name="gla",
    )(proj, projt, proj, proj, ga, wa2t, ba_col, on_g)


def _qknorm_rope_t(xt, g_col, cos, sin):
    n, tm = xt.shape
    x3 = xt.reshape(n // DIFF_DH, DIFF_DH, tm)
    r = lax.rsqrt(jnp.mean(x3 * x3, axis=1, keepdims=True) + EPS)
    y = x3 * r * g_col[None]
    half = ROT_DIM // 2
    y1, y2, rest = y[:, :half], y[:, half:ROT_DIM], y[:, ROT_DIM:]
    o1 = y1 * cos[None] - y2 * sin[None]
    o2 = y2 * cos[None] + y1 * sin[None]
    return jnp.concatenate([o1, o2, rest], axis=1)


def _seg_norms(x3):
    return jnp.sqrt(jnp.sum(x3 * x3, axis=1)).reshape(DIFF_HEADS, 2, x3.shape[2])


def _qkprep_kernel(qt_ref, kt_ref, vt_ref, pos_ref, invf_ref, qg_ref, kg_ref, qa_ref, qb_ref,
                   ko_ref, ve_ref, qn_ref, kn_ref):
    tm = qt_ref.shape[1]
    v3 = vt_ref[...].reshape(DIFF_HEADS, DIFF_DV, tm)
    ones = jnp.ones((DIFF_HEADS, ATT_SUM_ROWS, tm), BF16)
    ve_ref[...] = jnp.concatenate([v3, ones], axis=1).reshape(-1, tm)
    ang = pos_ref[...].astype(F32) * invf_ref[...]
    cos, sin = jnp.cos(ang), jnp.sin(ang)
    k3 = _qknorm_rope_t(kt_ref[...].astype(F32), kg_ref[...], cos, sin)
    ko_ref[...] = k3.reshape(-1, tm).T.astype(BF16)
    kn_ref[...] = _seg_norms(k3)
    q3 = _qknorm_rope_t(qt_ref[...].astype(F32), qg_ref[...], cos, sin) * (
        DIFF_DH ** -0.5 * LOG2E)
    qn_ref[...] = _seg_norms(q3)
    seg = lax.broadcasted_iota(jnp.int32, q3.shape, 0)
    qa_ref[...] = jnp.where(seg % 2 == 0, q3, 0.0).reshape(-1, tm).astype(BF16)
    qb_ref[...] = jnp.where(seg % 2 == 1, q3, 0.0).reshape(-1, tm).astype(BF16)


def _qkprep(projt, pos_row, invf_col, qg_col, kg_col, tm, row_q, row_k, row_v):
    s = projt.shape[1]
    n = DIFF_HEADS * 2 * DIFF_DH
    ne = DIFF_HEADS * (DIFF_DV + ATT_SUM_ROWS)
    col = pl.BlockSpec((DIFF_DH, 1), lambda i: (0, 0))
    return pl.pallas_call(
        _qkprep_kernel,
        grid=(s // tm,),
        in_specs=[pl.BlockSpec((n, tm), lambda i: (row_q // n, i)),
                  pl.BlockSpec((n, tm), lambda i: (row_k // n, i)),
                  pl.BlockSpec((n, tm), lambda i: (row_v // n, i)),
                  pl.BlockSpec((1, tm), lambda i: (0, i)),
                  pl.BlockSpec((ROT_DIM // 2, 1), lambda i: (0, 0)), col, col],
        out_specs=[pl.BlockSpec((n, tm), lambda i: (0, i)),
                   pl.BlockSpec((n, tm), lambda i: (0, i)),
                   pl.BlockSpec((tm, n), lambda i: (i, 0)),
                   pl.BlockSpec((ne, tm), lambda i: (0, i)),
                   pl.BlockSpec((DIFF_HEADS, 2, tm), lambda i: (0, 0, i)),
                   pl.BlockSpec((DIFF_HEADS, 2, tm), lambda i: (0, 0, i))],
        out_shape=[jax.ShapeDtypeStruct((n, s), BF16), jax.ShapeDtypeStruct((n, s), BF16),
                   jax.ShapeDtypeStruct((s, n), BF16), jax.ShapeDtypeStruct((ne, s), BF16),
                   jax.ShapeDtypeStruct((DIFF_HEADS, 2, s), F32),
                   jax.ShapeDtypeStruct((DIFF_HEADS, 2, s), F32)],
        compiler_params=_cparams(("arbitrary",)),
        name="qkprep",
    )(projt, projt, projt, pos_row, invf_col, qg_col, kg_col)


ATT_COLS = 256
ATT_LOOKAHEAD = 4
ATT_SUM_ROWS = 16
ATT_BOUND_SLACK = 1.02
ATT_BOUND_LIMIT = 50.0


def _diffattn_kernel(qa_ref, qb_ref, k_ref, vt_ref, qn_ref, kn_ref, lq1_ref, lk1_ref, lq2_ref,
                     lk2_ref, sg_ref, o_ref, *scr, lambda_init, tk, cols):
    tq = qa_ref.shape[1]
    nblk = 2 * tq // cols
    q_scr, m_scr, acc_scr = (scr[b * nblk:(b + 1) * nblk] for b in range(3))
    s_scr = scr[3 * nblk:]
    i = pl.program_id(1)
    kmax = jnp.max(kn_ref[...], axis=1, keepdims=True)
    bound = qn_ref[...] * kmax * ATT_BOUND_SLACK
    bound = jnp.concatenate([bound[0:1], bound[1:2]], axis=1)
    bounded = jnp.max(bound) < ATT_BOUND_LIMIT
    for c in range(nblk):
        src = qa_ref if c * cols < tq else qb_ref
        off = (c * cols) % tq
        q_scr[c][...] = src[:, off:off + cols]
        m_scr[c][...] = jnp.where(bounded, bound[:, c * cols:(c + 1) * cols], NEG_BIG)
        acc_scr[c][...] = jnp.zeros_like(acc_scr[c])

    def scores(j, c):
        start = pl.multiple_of(j * tk, tk)
        return jnp.dot(k_ref[pl.ds(start, tk), :], q_scr[c][...], preferred_element_type=F32)

    def steps(tiles, masked, next_tile, fixed):
        items = [(j, c) for j in tiles for c in range(nblk)]
        pending = []
        for n, (j, c) in enumerate(items):
            s = s_scr[n][...] if n < ATT_LOOKAHEAD else pending.pop(0)
            ahead = n + ATT_LOOKAHEAD
            if ahead < len(items):
                pending.append(scores(*items[ahead]))
            elif next_tile is not None:
                s_scr[ahead - len(items)][...] = scores(next_tile, ahead - len(items))
            start = pl.multiple_of(j * tk, tk)
            if masked:
                krow = lax.broadcasted_iota(jnp.int32, (tk, cols), 0)
                qcol = lax.broadcasted_iota(jnp.int32, (tk, cols), 1)
                qpos = i * tq + (c * cols) % tq + qcol
                s = jnp.where((start + krow) // CHUNK <= qpos // CHUNK, s, NEG_BIG)
            vj = vt_ref[:, pl.ds(start, tk)]
            if fixed:
                p = jnp.exp2(s - m_scr[c][...]).astype(BF16)
                acc_scr[c][...] += jnp.dot(vj, p, preferred_element_type=F32)
            else:
                m_prev = m_scr[c][...]
                m_new = jnp.maximum(m_prev, jnp.max(s, axis=0, keepdims=True))
                alpha = jnp.exp2(m_prev - m_new)
                p = jnp.exp2((s - m_new).astype(BF16))
                acc_scr[c][...] = alpha * acc_scr[c][...] + jnp.dot(
                    vj, p, preferred_element_type=F32)
                m_scr[c][...] = m_new

    def attend(fixed):
        for c in range(ATT_LOOKAHEAD):
            s_scr[c][...] = scores(0, c)
        lax.fori_loop(0, i // 2,
                      lambda t, c: (steps((2 * t, 2 * t + 1), False, 2 * t + 2, fixed), c)[1], 0)

        @pl.when(i % 2 == 1)
        def _():
            steps((i - 1,), False, i, fixed)

        steps((i,), True, None, fixed)

    @pl.when(bounded)
    def _():
        attend(True)

    @pl.when(jnp.logical_not(bounded))
    def _():
        attend(False)

    o = jnp.concatenate([acc_scr[c][:DIFF_DV] / acc_scr[c][DIFF_DV:DIFF_DV + 1]
                         for c in range(nblk)], axis=1)
    lam = (jnp.exp(jnp.sum(lq1_ref[...] * lk1_ref[...]))
           - jnp.exp(jnp.sum(lq2_ref[...] * lk2_ref[...])) + lambda_init)
    o = o[:, :tq] - lam * o[:, tq:]
    o = o * lax.rsqrt(jnp.mean(o * o, axis=0, keepdims=True) + EPS) * sg_ref[...]
    o_ref[...] = (o * (1.0 - lambda_init)).T.astype(BF16)


def _diffattn(qat, qbt, kr, vte, qn, kn, lq1, lk1, lq2, lk2, sg_col, lambda_init, tq, tk):
    s = kr.shape[0]
    hd = 2 * DIFF_DH
    vec = pl.BlockSpec((1, DIFF_DH), lambda h, i: (0, 0))
    cols = min(ATT_COLS, tq)
    nblk = 2 * tq // cols
    kern = functools.partial(_diffattn_kernel, lambda_init=lambda_init, tk=tk, cols=cols)
    dve = DIFF_DV + ATT_SUM_ROWS
    assert tq == tk and nblk >= ATT_LOOKAHEAD
    scratch = ([pltpu.VMEM((hd, cols), BF16)] * nblk + [pltpu.VMEM((1, cols), F32)] * nblk
               + [pltpu.VMEM((dve, cols), F32)] * nblk
               + [pltpu.VMEM((tk, cols), F32)] * ATT_LOOKAHEAD)
    return pl.pallas_call(
        kern,
        grid=(DIFF_HEADS, s // tq),
        in_specs=[pl.BlockSpec((hd, tq), lambda h, i: (h, i)),
                  pl.BlockSpec((hd, tq), lambda h, i: (h, i)),
                  pl.BlockSpec((s, hd), lambda h, i: (0, h)),
                  pl.BlockSpec((dve, s), lambda h, i: (h, 0)),
                  pl.BlockSpec((None, 2, tq), lambda h, i: (h, 0, i)),
                  pl.BlockSpec((None, 2, s), lambda h, i: (h, 0, 0)),
                  vec, vec, vec, vec,
                  pl.BlockSpec((DIFF_DV, 1), lambda h, i: (0, 0))],
        out_specs=pl.BlockSpec((tq, DIFF_DV), lambda h, i: (i, h)),
        out_shape=jax.ShapeDtypeStruct((s, DIFF_HEADS * DIFF_DV), BF16),
        scratch_shapes=scratch,
        compiler_params=_cparams(("arbitrary", "arbitrary")),
        name="diffattn",
    )(qat, qbt, kr, vte, qn, kn, lq1, lk1, lq2, lk2, sg_col)


def _mergeout_kernel(og_ref, od_ref, mg_ref, md_ref, x_ref, wbg_ref, wbd_ref, wo_ref, gt_ref,
                     g2_ref, sc_ref, sh_ref, x1_ref, hf_ref, hfp_ref):
    bg = jnp.dot(og_ref[...], wbg_ref[...], preferred_element_type=F32)
    bd = jnp.dot(od_ref[...], wbd_ref[...], preferred_element_type=F32)
    merged = (jax.nn.sigmoid(mg_ref[...].astype(F32)) * bg
              + jax.nn.sigmoid(md_ref[...].astype(F32)) * bd)
    x1 = x_ref[...] + gt_ref[...] * jnp.dot(merged.astype(BF16), wo_ref[...],
                                             preferred_element_type=F32)
    x1_ref[...] = x1
    hf = _rms_mod(x1, g2_ref[...], sc_ref[...], sh_ref[...])
    hf_ref[...] = hf
    hfp_ref[...] = _pack_halves(hf)


def _mergeout(og, od, proj, x, wbg, wbd, wo, gt, g2, sc, sh, tm, col_mg, col_md):
    s, d = x.shape
    vec = pl.BlockSpec((1, d), lambda i: (0, 0))
    wspec = pl.BlockSpec((d, d), lambda i: (0, 0))
    row = pl.BlockSpec((tm, d), lambda i: (i, 0))
    return pl.pallas_call(
        _mergeout_kernel,
        grid=(s // tm,),
        in_specs=[row, row,
                  pl.BlockSpec((tm, d), lambda i: (i, col_mg // d)),
                  pl.BlockSpec((tm, d), lambda i: (i, col_md // d)),
                  row, wspec, wspec, wspec, vec, vec, vec, vec],
        out_specs=[row, row, pl.BlockSpec((tm, d // 2), lambda i: (i, 0))],
        out_shape=[jax.ShapeDtypeStruct((s, d), F32), jax.ShapeDtypeStruct((s, d), F32),
                   jax.ShapeDtypeStruct((s, d // 2), jnp.uint32)],
        compiler_params=_cparams(("arbitrary",)),
        name="mergeout",
    )(og, od, proj, proj, x, wbg, wbd, wo, gt, g2, sc, sh)


def _route_kernel(hf_ref, wrt_ref, bias_ref, idx_ref, wts_ref, rnk_ref, cnt_ref, run_scr):
    tr = hf_ref.shape[0]
    e = wrt_ref.shape[0]
    gsz = e // N_GROUPS

    @pl.when(pl.program_id(0) == 0)
    def _():
        run_scr[...] = jnp.zeros_like(run_scr)

    logits = _nt_dot(wrt_ref[...], hf_ref[...], precision=HIGHEST)
    scores = jax.nn.sigmoid(logits)
    biased = scores + bias_ref[...]
    g3 = biased.reshape(N_GROUPS, gsz, tr)
    m1 = jnp.max(g3, axis=1, keepdims=True)
    n_top = jnp.sum(jnp.where(g3 == m1, 1.0, 0.0), axis=1, keepdims=True)
    m2 = jnp.max(jnp.where(g3 < m1, g3, -jnp.inf), axis=1, keepdims=True)
    gs = (m1 + jnp.where(n_top >= 2.0, m1, m2)).reshape(N_GROUPS, tr)
    gi = lax.broadcasted_iota(jnp.int32, (N_GROUPS, tr), 0)
    beaten = jnp.zeros((N_GROUPS, tr), F32)
    for g in range(N_GROUPS):
        other = gs[g:g + 1, :]
        beaten = beaten + jnp.where((other > gs) | ((other == gs) & (g < gi)), 1.0, 0.0)
    gsel = (beaten < float(TOPK_GROUPS)).reshape(N_GROUPS, 1, tr)
    masked = jnp.where(gsel, g3, -jnp.inf).reshape(e, tr)

    ids = lax.broadcasted_iota(jnp.int32, (e, tr), 0)
    chosen = jnp.zeros((e, tr), F32)
    sel_idx, sel_score = [], []
    for _ in range(TOP_K):
        mx = jnp.max(masked, axis=0, keepdims=True)
        ix = jnp.min(jnp.where(masked == mx, ids, e), axis=0, keepdims=True)
        hit = ids == ix
        sel_idx.append(ix)
        sel_score.append(jnp.sum(jnp.where(hit, scores, 0.0), axis=0, keepdims=True))
        chosen = jnp.where(hit, 1.0, chosen)
        masked = jnp.where(hit, -jnp.inf, masked)
    idx = jnp.concatenate(sel_idx, axis=0)
    sc = jnp.concatenate(sel_score, axis=0)
    idx_ref[...] = idx
    wts_ref[...] = sc / jnp.sum(sc, axis=0, keepdims=True) * ROUTED_SCALE

    row = lax.broadcasted_iota(jnp.int32, (tr, tr), 0)
    col = lax.broadcasted_iota(jnp.int32, (tr, tr), 1)
    before = jnp.where(row < col, 1.0, 0.0).astype(BF16)
    prior = jnp.dot(chosen.astype(BF16), before, preferred_element_type=F32) + run_scr[:, 0:1]
    rnk_ref[...] = jnp.concatenate(
        [jnp.sum(jnp.where(ids == sel_idx[k], prior, 0.0), axis=0, keepdims=True)
         for k in range(TOP_K)], axis=0).astype(jnp.int32)
    run_scr[...] = run_scr[...] + jnp.sum(chosen, axis=1, keepdims=True)
    cnt_ref[...] = run_scr[...].astype(jnp.int32)


def _route(hf, wrt, bias_col, tr):
    s, d = hf.shape
    e = wrt.shape[0]
    tok = pl.BlockSpec((TOP_K, tr), lambda i: (0, i))
    return pl.pallas_call(
        _route_kernel,
        grid=(s // tr,),
        in_specs=[pl.BlockSpec((tr, d), lambda i: (i, 0)),
                  pl.BlockSpec((e, d), lambda i: (0, 0)),
                  pl.BlockSpec((e, 1), lambda i: (0, 0))],
        out_specs=[tok, tok, tok, pl.BlockSpec((e, 128), lambda i: (0, 0))],
        out_shape=[jax.ShapeDtypeStruct((TOP_K, s), jnp.int32),
                   jax.ShapeDtypeStruct((TOP_K, s), F32),
                   jax.ShapeDtypeStruct((TOP_K, s), jnp.int32),
                   jax.ShapeDtypeStruct((e, 128), jnp.int32)],
        scratch_shapes=[pltpu.VMEM((e, 128), F32)],
        compiler_params=_cparams(("arbitrary",)),
        name="route",
    )(hf, wrt, bias_col)


def _positions_kernel(idx_ref, rnk_ref, pstart_ref, pos_ref):
    e = pstart_ref.shape[0]
    ts = idx_ref.shape[1]
    ids = lax.broadcasted_iota(jnp.int32, (e, ts), 0)
    idx = idx_ref[...]
    pos_ref[...] = rnk_ref[...] + jnp.concatenate(
        [jnp.sum(jnp.where(ids == idx[k:k + 1, :], pstart_ref[...], 0), axis=0, keepdims=True)
         for k in range(TOP_K)], axis=0)


def _positions(idx, rnk, pstart_col, ts):
    s = idx.shape[1]
    e = pstart_col.shape[0]
    tok = pl.BlockSpec((TOP_K, ts), lambda i: (0, i))
    return pl.pallas_call(
        _positions_kernel,
        grid=(s // ts,),
        in_specs=[tok, tok, pl.BlockSpec((e, 1), lambda i: (0, 0))],
        out_specs=tok,
        out_shape=jax.ShapeDtypeStruct((TOP_K, s), jnp.int32),
        compiler_params=_cparams(("arbitrary",)),
        name="positions",
    )(idx, rnk, pstart_col)


def _swiglu_packed(xp, wg, wu, wd):
    lo, hi = _unpack_halves(xp)
    lo, hi = lo.astype(BF16), hi.astype(BF16)
    n = lo.shape[1]
    g = (jnp.dot(lo, wg[:n], preferred_element_type=F32)
         + jnp.dot(hi, wg[n:], preferred_element_type=F32))
    u = (jnp.dot(lo, wu[:n], preferred_element_type=F32)
         + jnp.dot(hi, wu[n:], preferred_element_type=F32))
    h = (g * jax.nn.sigmoid(g)) * u
    return jnp.dot(h.astype(BF16), wd[...], preferred_element_type=F32)


def _moe_kernel(ie_ref, ib_ref, first_ref, slot_ref, ne_ref, rows_ref, nv_ref, xs_ref, wg_hbm,
                wu_hbm, wd_hbm, ys_ref, wg_f, wu_f, wd_f, wg_b, wu_b, wd_b, sem):
    del ib_ref
    i = pl.program_id(0)

    def fetch(e, slot):
        return [pltpu.make_async_copy(src.at[e], dst.at[slot], sem.at[slot, n])
                for n, (src, dst) in enumerate(((wg_hbm, wg_f), (wu_hbm, wu_f), (wd_hbm, wd_f)))]

    @pl.when(i == 0)
    def _():
        for cp in fetch(ie_ref[0], 0):
            cp.start()

    @pl.when(i < nv_ref[0])
    def _():
        for slot in range(2):
            @pl.when((first_ref[i] == 1) & (slot_ref[i] == slot))
            def _():
                for cp in fetch(ie_ref[i], slot):
                    cp.wait()
                wg_b[...] = wg_f[slot].astype(BF16)
                wu_b[...] = wu_f[slot].astype(BF16)
                wd_b[...] = wd_f[slot].astype(BF16)

                @pl.when(ne_ref[i] >= 0)
                def _():
                    for cp in fetch(ne_ref[i], 1 - slot):
                        cp.start()

        row = lax.broadcasted_iota(jnp.int32, xs_ref.shape, 0)
        xp = jnp.where(row < rows_ref[i], xs_ref[...], jnp.uint32(0))
        ys_ref[...] = _pack_halves(_swiglu_packed(xp, wg_b, wu_b, wd_b))


def _moe(item_e, item_b, item_first, item_slot, item_next, item_rows, n_valid, xs, wg, wu, wd):
    m_pad, dh = xs.shape
    _, d, f = wg.shape
    n_items = item_e.shape[0]
    blk = lambda i, ie, ib, fi, sl, ne, nr, nv: (ib[i], 0)
    hbm = pl.BlockSpec(memory_space=pl.ANY)
    return pl.pallas_call(
        _moe_kernel,
        grid_spec=pltpu.PrefetchScalarGridSpec(
            num_scalar_prefetch=7,
            grid=(n_items,),
            in_specs=[pl.BlockSpec((MOE_ROWS, dh), blk), hbm, hbm, hbm],
            out_specs=pl.BlockSpec((MOE_ROWS, dh), blk),
            scratch_shapes=[pltpu.VMEM((2, d, f), F32), pltpu.VMEM((2, d, f), F32),
                            pltpu.VMEM((2, f, d), F32),
                            pltpu.VMEM((d, f), BF16), pltpu.VMEM((d, f), BF16),
                            pltpu.VMEM((f, d), BF16),
                            pltpu.SemaphoreType.DMA((2, 3))],
        ),
        out_shape=jax.ShapeDtypeStruct((m_pad, dh), jnp.uint32),
        compiler_params=_cparams(("arbitrary",)),
        name="moe",
    )(item_e, item_b, item_first, item_slot, item_next, item_rows, n_valid, xs, wg, wu, wd)


def _sc_gather_rows(table, idx_row):
    m = idx_row.shape[1]
    w = table.shape[1]
    idx_row = idx_row.reshape(m // SC_GATHER_WINDOW, SC_GATHER_WINDOW)
    mesh = plsc.VectorSubcoreMesh(core_axis_name="c", subcore_axis_name="s")

    @functools.partial(pl.kernel, mesh=mesh,
                       out_type=jax.ShapeDtypeStruct((m, w), table.dtype))
    def gather(table_hbm, idx_hbm, out_hbm):
        def body(idx_vmem, out_vmem):
            pltpu.sync_copy(table_hbm.at[idx_vmem.at[0]], out_vmem)

        pltpu.emit_pipeline(
            body,
            grid=(m // SC_GATHER_WINDOW,),
            in_specs=[pl.BlockSpec((1, SC_GATHER_WINDOW), lambda i: (i, 0))],
            out_specs=[pl.BlockSpec((SC_GATHER_WINDOW, w), lambda i: (i, 0))],
            core_axis_name=("c", "s"),
            dimension_semantics=(pltpu.PARALLEL,),
        )(idx_hbm, out_hbm)

    return gather(table, idx_row)


def _sc_scatter_rows(rows, idx_blocks, m_out):
    s, w = rows.shape
    mesh = plsc.VectorSubcoreMesh(core_axis_name="c", subcore_axis_name="s")

    @functools.partial(pl.kernel, mesh=mesh,
                       out_type=jax.ShapeDtypeStruct((m_out, w), rows.dtype))
    def scatter(rows_hbm, idx_hbm, out_hbm):
        def body(rows_vmem, idx_vmem):
            for k in range(TOP_K):
                pltpu.sync_copy(rows_vmem, out_hbm.at[idx_vmem.at[k]])

        pltpu.emit_pipeline(
            body,
            grid=(s // SC_GATHER_WINDOW,),
            in_specs=[pl.BlockSpec((SC_GATHER_WINDOW, w), lambda i: (i, 0)),
                      pl.BlockSpec((TOP_K, SC_GATHER_WINDOW), lambda i: (i, 0))],
            out_specs=[],
            core_axis_name=("c", "s"),
            dimension_semantics=(pltpu.PARALLEL,),
        )(rows_hbm, idx_hbm)

    return scatter(rows, idx_blocks)


def _combine_kernel(wt_ref, hf_ref, x1_ref, gt_ref, sg_ref, su_ref, sd_ref, g_ref, o_ref):
    tc = x1_ref.shape[0]
    y = _swiglu_packed(hf_ref[...], sg_ref, su_ref, sd_ref)
    wt = wt_ref[...]
    n = g_ref.shape[2]
    r_lo = jnp.zeros((tc, n), F32)
    r_hi = jnp.zeros((tc, n), F32)
    for k in range(TOP_K):
        lo, hi = _unpack_halves(g_ref[k])
        r_lo = r_lo + lo * wt[:, k:k + 1]
        r_hi = r_hi + hi * wt[:, k:k + 1]
    y = y + jnp.concatenate([r_lo, r_hi], axis=1)
    o_ref[...] = x1_ref[...] + gt_ref[...] * y


def _combine(wts_t, hfp, x1, gt, sg, su, sd, gathered, tc):
    s, d = x1.shape
    f = sg.shape[1]
    row = pl.BlockSpec((tc, d), lambda i: (i, 0))
    return pl.pallas_call(
        _combine_kernel,
        grid=(s // tc,),
        in_specs=[pl.BlockSpec((tc, TOP_K), lambda i: (i, 0)),
                  pl.BlockSpec((tc, d // 2), lambda i: (i, 0)), row,
                  pl.BlockSpec((1, d), lambda i: (0, 0)),
                  pl.BlockSpec((d, f), lambda i: (0, 0)),
                  pl.BlockSpec((d, f), lambda i: (0, 0)),
                  pl.BlockSpec((f, d), lambda i: (0, 0)),
                  pl.BlockSpec((TOP_K, tc, d // 2), lambda i: (0, i, 0))],
        out_specs=row,
        out_shape=jax.ShapeDtypeStruct((s, d), F32),
        compiler_params=_cparams(("arbitrary",)),
        name="combine",
    )(wts_t, hfp, x1, gt, sg, su, sd, gathered)


def _tile(n, want):
    t = min(n, want)
    assert n % t == 0, (n, t)
    return t


def _layer(l, x, c_col, pos_row, p):
    s, d = x.shape
    lambda_init = 0.8 - 0.6 * math.exp(-0.3 * l)
    gqk, gv = GLA_HEADS * GLA_DK, GLA_HEADS * GLA_DV
    dqk, dvw = DIFF_HEADS * 2 * DIFF_DH, DIFF_HEADS * DIFF_DV
    lowrank = p["gla_w_a2"].shape[0]

    mod = _ada(c_col, p["w_ada"], p["b_ada"][None, :])
    sh_a, sc_a, gt_a, sh_f, sc_f, gt_f = [mod[:, j * d:(j + 1) * d] for j in range(6)]

    w_in = p["w_in"]
    o = 0
    cols = {}
    for name, wdt in (("gq", gqk), ("gk", gqk), ("gv", gv), ("ga", lowrank), ("gg", gv),
                      ("dq", dqk), ("dk", dqk), ("dv", dvw), ("mg", d), ("md", d)):
        cols[name] = w_in[:, o:o + wdt]
        o += wdt
    row_names = ("gv", "gg", "mg", "md", "gq")
    w_row = jnp.concatenate([cols[n] for n in row_names], axis=1).astype(BF16)
    col_of, o = {}, 0
    for n in row_names:
        col_of[n] = o
        o += cols[n].shape[1]
    t_names = ("dq", "dk", "dv", "gk")
    w_t = jnp.concatenate([cols[n] for n in t_names], axis=1).T.astype(BF16)
    row_of, o = {}, 0
    for n in t_names:
        row_of[n] = o
        o += cols[n].shape[1]
    w_ga = jnp.pad(cols["ga"], ((0, 0), (0, 128 - lowrank))).astype(BF16)

    g1 = p["norm1_g"][None, :]
    tm = _tile(s, 1024)
    proj, ga = _inproj(x, g1, sc_a, sh_a, w_row, w_ga, tm, 512)
    projt = _inproj_t(x, g1, sc_a, sh_a, w_t, tm, 512)

    wa2t = jnp.pad(p["gla_w_a2"].T, ((0, 0), (0, 128 - lowrank)))
    o_gla = _gla(proj, projt, ga, wa2t, p["gla_b_a"][:, None], p["gla_onorm_g"][None, :],
                 _tile(s, 512), col_of["gq"], col_of["gv"], col_of["gg"], row_of["gk"])

    invf = ROPE_THETA ** (-jnp.arange(0, ROT_DIM, 2, dtype=F32) / ROT_DIM)
    qat, qbt, kr, vte, qn, kn = _qkprep(projt, pos_row, invf[:, None], p["diff_qnorm_g"][:, None],
                                p["diff_knorm_g"][:, None], _tile(s, 512), row_of["dq"],
                                row_of["dk"], row_of["dv"])
    tq = _tile(s, 512)
    o_diff = _diffattn(qat, qbt, kr, vte, qn, kn, p["diff_lq1"][None, :], p["diff_lk1"][None, :],
                       p["diff_lq2"][None, :], p["diff_lk2"][None, :],
                       p["diff_subln_g"][:, None], lambda_init, tq, tq)

    x1, hf, hfp = _mergeout(o_gla, o_diff, proj, x, p["w_branch_gla"].astype(BF16),
                       p["w_branch_diff"].astype(BF16), p["w_out"].astype(BF16), gt_a,
                       p["norm2_g"][None, :], sc_f, sh_f, _tile(s, 512),
                       col_of["mg"], col_of["md"])

    e = p["w_router"].shape[1]
    idx, wts, rnk, cnt = _route(hf, p["w_router"].T, p["router_bias"][:, None], _tile(s, 512))

    counts = cnt[:, 0]
    pcounts = ((counts + MOE_ROWS - 1) // MOE_ROWS) * MOE_ROWS
    pend = jnp.cumsum(pcounts)
    pstart = pend - pcounts
    pos = _positions(idx, rnk, pstart[:, None], _tile(s, 512))
    n_items = (s * TOP_K) // MOE_ROWS + e
    n_valid = (pend[-1] // MOE_ROWS).astype(jnp.int32)
    item_b = jnp.minimum(jnp.arange(n_items, dtype=jnp.int32), n_valid - 1)
    item_e = jnp.minimum(jnp.sum(pend[None, :] <= (item_b * MOE_ROWS)[:, None], axis=1),
                         e - 1).astype(jnp.int32)

    wn = SC_GATHER_WINDOW
    pos_w = pos.reshape(TOP_K, s // wn, wn).transpose(1, 0, 2).reshape(s // wn * TOP_K, wn)
    xs = _sc_scatter_rows(hfp, pos_w, n_items * MOE_ROWS)
    item_rows = jnp.clip(pstart[item_e] + counts[item_e] - item_b * MOE_ROWS, 0,
                         MOE_ROWS).astype(jnp.int32)
    prev_e = jnp.concatenate([jnp.full((1,), -1, jnp.int32), item_e[:-1]])
    item_first = ((jnp.arange(n_items) < n_valid) & (item_e != prev_e)).astype(jnp.int32)
    item_slot = ((jnp.cumsum(item_first) - 1) % 2).astype(jnp.int32)
    cand = jnp.where(pcounts > 0, jnp.arange(e, dtype=jnp.int32), e)
    following = jnp.concatenate([lax.cummin(cand[::-1])[::-1][1:], jnp.full((1,), e, jnp.int32)])
    item_next = jnp.where(following[item_e] < e, following[item_e], -1).astype(jnp.int32)
    ys = _moe(item_e, item_b, item_first, item_slot, item_next, item_rows, n_valid[None], xs,
              p["w_exp_gate"], p["w_exp_up"], p["w_exp_down"])
    gathered = _sc_gather_rows(ys, pos.reshape(1, TOP_K * s)).reshape(TOP_K, s, d // 2)
    return _combine(wts.T, hfp, x1, gt_f, p["w_sh_gate"].astype(BF16),
                    p["w_sh_up"].astype(BF16), p["w_sh_down"].astype(BF16), gathered,
                    _tile(s, 256))


_LAYER_PARAMS = ("w_ada", "b_ada", "norm1_g", "w_in", "gla_w_a2", "gla_b_a", "gla_onorm_g",
                 "diff_qnorm_g", "diff_knorm_g", "diff_lq1", "diff_lk1", "diff_lq2", "diff_lk2",
                 "diff_subln_g", "w_branch_gla", "w_branch_diff", "w_out", "norm2_g", "w_router",
                 "router_bias", "w_exp_gate", "w_exp_up", "w_exp_down", "w_sh_gate", "w_sh_up",
                 "w_sh_down")


def kernel(x, c, positions, w_ada, b_ada, norm1_g, w_in, gla_w_a2, gla_b_a, gla_onorm_g, diff_qnorm_g, diff_knorm_g, diff_lq1, diff_lk1, diff_lq2, diff_lk2, diff_subln_g, w_branch_gla, w_branch_diff, w_out, norm2_g, w_router, router_bias, w_exp_gate, w_exp_up, w_exp_down, w_sh_gate, w_sh_up, w_sh_down):
    stacked = dict(zip(_LAYER_PARAMS, (
        w_ada, b_ada, norm1_g, w_in, gla_w_a2, gla_b_a, gla_onorm_g, diff_qnorm_g, diff_knorm_g,
        diff_lq1, diff_lk1, diff_lq2, diff_lk2, diff_subln_g, w_branch_gla, w_branch_diff, w_out,
        norm2_g, w_router, router_bias, w_exp_gate, w_exp_up, w_exp_down, w_sh_gate, w_sh_up,
        w_sh_down)))
    b, s, d = x.shape
    assert b == 1, "single-sequence kernel"
    xl = x[0]
    c_col = c[0][:, None]
    pos_row = positions.astype(jnp.int32)
    for l in range(w_ada.shape[0]):
        xl = _layer(l, xl, c_col, pos_row, {k: v[l] for k, v in stacked.items()})
    return xl[None]
```

```python
import functools
import math

import jax
import jax.numpy as jnp
from jax import lax
from jax.experimental import pallas as pl
from jax.experimental.pallas import tpu as pltpu
from jax.experimental.pallas import tpu_sc as plsc

CHUNK = 64
EPS = 1e-6
GLA_HEADS = 4
GLA_DK = 128
GLA_DV = 256
GLA_TAU = 16.0
DIFF_HEADS = 8
DIFF_DH = 64
DIFF_DV = 2 * DIFF_DH
ROPE_THETA = 500000.0
ROT_DIM = DIFF_DH // 4
N_GROUPS = 8
TOPK_GROUPS = 4
TOP_K = 8
ROUTED_SCALE = 2.5

MOE_ROWS = 512
SC_GATHER_WINDOW = 64
VMEM_LIMIT = 56 * 1024 * 1024
NEG_BIG = -1e30
LOG2E = 1.4426950408889634
HIGHEST = lax.Precision.HIGHEST
F32 = jnp.float32
BF16 = jnp.bfloat16


def _cparams(sem):
    return pltpu.CompilerParams(dimension_semantics=sem, vmem_limit_bytes=VMEM_LIMIT)


def _nt_dot(a, b, precision=None):
    return lax.dot_general(a, b, (((1,), (1,)), ((), ())), precision=precision,
                           preferred_element_type=F32)


def _pack_halves(x):
    n = x.shape[1] // 2
    lo = pltpu.bitcast(x[:, :n].astype(BF16).astype(F32), jnp.uint32) >> 16
    hi = pltpu.bitcast(x[:, n:].astype(BF16).astype(F32), jnp.uint32) & jnp.uint32(0xFFFF0000)
    return lo | hi


def _unpack_halves(w):
    return (pltpu.bitcast(w << 16, F32), pltpu.bitcast(w & jnp.uint32(0xFFFF0000), F32))


def _rms_mod(x, g, sc, sh):
    xn = x * lax.rsqrt(jnp.mean(x * x, axis=-1, keepdims=True) + EPS)
    return (xn * g) * (1.0 + sc) + sh


def _ada_kernel(c_ref, w_ref, b_ref, o_ref):
    c = c_ref[...]
    ca = c * jax.nn.sigmoid(c)
    o_ref[...] = jnp.sum(ca * w_ref[...], axis=0, keepdims=True) + b_ref[...]


def _ada(c_col, w, b):
    d, n = w.shape
    tn = min(1024, n)
    return pl.pallas_call(
        _ada_kernel,
        grid=(n // tn,),
        in_specs=[pl.BlockSpec((d, 1), lambda j: (0, 0)),
                  pl.BlockSpec((d, tn), lambda j: (0, j)),
                  pl.BlockSpec((1, tn), lambda j: (0, j))],
        out_specs=pl.BlockSpec((1, tn), lambda j: (0, j)),
        out_shape=jax.ShapeDtypeStruct((1, n), F32),
        compiler_params=_cparams(("arbitrary",)),
        name="ada",
    )(c_col, w, b)


def _inproj_kernel(x_ref, g_ref, sc_ref, sh_ref, w_ref, wga_ref, o_ref, ga_ref, h_scr):
    @pl.when(pl.program_id(1) == 0)
    def _():
        h = _rms_mod(x_ref[...], g_ref[...], sc_ref[...], sh_ref[...]).astype(BF16)
        h_scr[...] = h
        ga_ref[...] = jnp.dot(h, wga_ref[...], preferred_element_type=F32)

    o_ref[...] = jnp.dot(h_scr[...], w_ref[...], preferred_element_type=F32).astype(BF16)


def _inproj(x, g, sc, sh, w, wga, tm, tn):
    s, d = x.shape
    n = w.shape[1]
    vec = pl.BlockSpec((1, d), lambda i, j: (0, 0))
    return pl.pallas_call(
        _inproj_kernel,
        grid=(s // tm, n // tn),
        in_specs=[pl.BlockSpec((tm, d), lambda i, j: (i, 0)), vec, vec, vec,
                  pl.BlockSpec((d, tn), lambda i, j: (0, j)),
                  pl.BlockSpec((d, 128), lambda i, j: (0, 0))],
        out_specs=[pl.BlockSpec((tm, tn), lambda i, j: (i, j)),
                   pl.BlockSpec((tm, 128), lambda i, j: (i, 0))],
        out_shape=[jax.ShapeDtypeStruct((s, n), BF16), jax.ShapeDtypeStruct((s, 128), F32)],
        scratch_shapes=[pltpu.VMEM((tm, d), BF16)],
        compiler_params=_cparams(("arbitrary", "arbitrary")),
        name="inproj",
    )(x, g, sc, sh, w, wga)


def _inproj_t_kernel(x_ref, g_ref, sc_ref, sh_ref, wt_ref, o_ref, h_scr):
    @pl.when(pl.program_id(1) == 0)
    def _():
        h_scr[...] = _rms_mod(x_ref[...], g_ref[...], sc_ref[...], sh_ref[...]).astype(BF16)

    o_ref[...] = _nt_dot(wt_ref[...], h_scr[...]).astype(BF16)


def _inproj_t(x, g, sc, sh, wt, tm, tn):
    s, d = x.shape
    n = wt.shape[0]
    vec = pl.BlockSpec((1, d), lambda i, j: (0, 0))
    return pl.pallas_call(
        _inproj_t_kernel,
        grid=(s // tm, n // tn),
        in_specs=[pl.BlockSpec((tm, d), lambda i, j: (i, 0)), vec, vec, vec,
                  pl.BlockSpec((tn, d), lambda i, j: (j, 0))],
        out_specs=pl.BlockSpec((tn, tm), lambda i, j: (j, i)),
        out_shape=jax.ShapeDtypeStruct((n, s), BF16),
        scratch_shapes=[pltpu.VMEM((tm, d), BF16)],
        compiler_params=_cparams(("arbitrary", "arbitrary")),
        name="inproj_t",
    )(x, g, sc, sh, wt)


def _gla_kernel(q_ref, kt_ref, v_ref, gg_ref, ga_ref, wa2t_ref, ba_ref, on_ref, o_ref,
                state_ref, o_scr):
    tt = q_ref.shape[0]
    nchunk = tt // CHUNK

    @pl.when(pl.program_id(0) == 0)
    def _():
        state_ref[...] = jnp.zeros_like(state_ref)

    zt = _nt_dot(wa2t_ref[...], ga_ref[...], precision=HIGHEST) + ba_ref[...]
    lat = (jnp.minimum(zt, 0.0) - jnp.log1p(jnp.exp(-jnp.abs(zt)))) * (1.0 / GLA_TAU)
    row = lax.broadcasted_iota(jnp.int32, (tt, tt), 0)
    col = lax.broadcasted_iota(jnp.int32, (tt, tt), 1)
    same = (row // CHUNK) == (col // CHUNK)
    incl = jnp.where(same & (row <= col), 1.0, 0.0).astype(BF16)
    full = jnp.where(same, 1.0, 0.0).astype(BF16)
    lat_hi = lat.astype(BF16)
    lat_lo = (lat - lat_hi.astype(F32)).astype(BF16)
    cumt = (jnp.dot(lat_hi, incl, preferred_element_type=F32)
            + jnp.dot(lat_lo, incl, preferred_element_type=F32))
    tott = (jnp.dot(lat_hi, full, preferred_element_type=F32)
            + jnp.dot(lat_lo, full, preferred_element_type=F32))
    kdt = kt_ref[...].astype(F32) * jnp.exp(tott - cumt)
    dec = jnp.exp(tott)

    lane = lax.broadcasted_iota(jnp.int32, (GLA_DK, 2 * CHUNK), 1)
    for c in range(nchunk):
        pair = (c // 2) * 2 * CHUNK
        if nchunk > 1:
            keep = (lane // CHUNK) == (c % 2)
        for h in range(GLA_HEADS):
            rows = slice(h * GLA_DK, (h + 1) * GLA_DK)
            vcols = slice(h * GLA_DV, (h + 1) * GLA_DV)
            if nchunk > 1:
                a = jnp.where(keep, kdt[rows, pair:pair + 2 * CHUNK], 0.0).astype(BF16)
                vp = v_ref[pair:pair + 2 * CHUNK, vcols]
            else:
                a = kdt[rows, :].astype(BF16)
                vp = v_ref[:, vcols]
            upd = jnp.dot(a, vp, preferred_element_type=F32)
            dcol = dec[rows, c * CHUNK:c * CHUNK + 1]
            st = state_ref[h] * dcol + upd
            state_ref[h] = st
            qc = q_ref[c * CHUNK:(c + 1) * CHUNK, rows]
            o_scr[c * CHUNK:(c + 1) * CHUNK, vcols] = jnp.dot(
                qc, st.astype(BF16), preferred_element_type=F32)

    for h in range(GLA_HEADS):
        vcols = slice(h * GLA_DV, (h + 1) * GLA_DV)
        o = o_scr[:, vcols] * (GLA_DK ** -0.5)
        o = o * lax.rsqrt(jnp.mean(o * o, axis=-1, keepdims=True) + EPS) * on_ref[...]
        g = gg_ref[:, vcols].astype(F32)
        o_ref[:, vcols] = (o * (g * jax.nn.sigmoid(g))).astype(BF16)


def _gla(proj, projt, ga, wa2t, ba_col, on_g, tt, col_q, col_v, col_g, row_k):
    s = proj.shape[0]
    qk = GLA_HEADS * GLA_DK
    vw = GLA_HEADS * GLA_DV
    return pl.pallas_call(
        _gla_kernel,
        grid=(s // tt,),
        in_specs=[pl.BlockSpec((tt, qk), lambda i: (i, col_q // qk)),
                  pl.BlockSpec((qk, tt), lambda i: (row_k // qk, i)),
                  pl.BlockSpec((tt, vw), lambda i: (i, col_v // vw)),
                  pl.BlockSpec((tt, vw), lambda i: (i, col_g // vw)),
                  pl.BlockSpec((tt, 128), lambda i: (i, 0)),
                  pl.BlockSpec((qk, 128), lambda i: (0, 0)),
                  pl.BlockSpec((qk, 1), lambda i: (0, 0)),
                  pl.BlockSpec((1, GLA_DV), lambda i: (0, 0))],
        out_specs=pl.BlockSpec((tt, vw), lambda i: (i, 0)),
        out_shape=jax.ShapeDtypeStruct((s, vw), BF16),
        scratch_shapes=[pltpu.VMEM((GLA_HEADS, GLA_DK, GLA_DV), F32),
                        pltpu.VMEM((tt, vw), F32)],
        compiler_params=_cparams(("arbitrary",)),
        name="gla",
    )(proj, projt, proj, proj, ga, wa2t, ba_col, on_g)


def _qknorm_rope_t(xt, g_col, cos, sin):
    n, tm = xt.shape
    x3 = xt.reshape(n // DIFF_DH, DIFF_DH, tm)
    r = lax.rsqrt(jnp.mean(x3 * x3, axis=1, keepdims=True) + EPS)
    y = x3 * r * g_col[None]
    half = ROT_DIM // 2
    y1, y2, rest = y[:, :half], y[:, half:ROT_DIM], y[:, ROT_DIM:]
    o1 = y1 * cos[None] - y2 * sin[None]
    o2 = y2 * cos[None] + y1 * sin[None]
    return jnp.concatenate([o1, o2, rest], axis=1)


def _seg_norms(x3):
    return jnp.sqrt(jnp.sum(x3 * x3, axis=1)).reshape(DIFF_HEADS, 2, x3.shape[2])


def _qkprep_kernel(qt_ref, kt_ref, vt_ref, pos_ref, invf_ref, qg_ref, kg_ref, qa_ref, qb_ref,
                   ko_ref, ve_ref, qn_ref, kn_ref):
    tm = qt_ref.shape[1]
    v3 = vt_ref[...].reshape(DIFF_HEADS, DIFF_DV, tm)
    ones = jnp.ones((DIFF_HEADS, ATT_SUM_ROWS, tm), BF16)
    ve_ref[...] = jnp.concatenate([v3, ones], axis=1).reshape(-1, tm)
    ang = pos_ref[...].astype(F32) * invf_ref[...]
    cos, sin = jnp.cos(ang), jnp.sin(ang)
    k3 = _qknorm_rope_t(kt_ref[...].astype(F32), kg_ref[...], cos, sin)
    ko_ref[...] = k3.reshape(-1, tm).T.astype(BF16)
    kn_ref[...] = _seg_norms(k3)
    q3 = _qknorm_rope_t(qt_ref[...].astype(F32), qg_ref[...], cos, sin) * (
        DIFF_DH ** -0.5 * LOG2E)
    qn_ref[...] = _seg_norms(q3)
    seg = lax.broadcasted_iota(jnp.int32, q3.shape, 0)
    qa_ref[...] = jnp.where(seg % 2 == 0, q3, 0.0).reshape(-1, tm).astype(BF16)
    qb_ref[...] = jnp.where(seg % 2 == 1, q3, 0.0).reshape(-1, tm).astype(BF16)


def _qkprep(projt, pos_row, invf_col, qg_col, kg_col, tm, row_q, row_k, row_v):
    s = projt.shape[1]
    n = DIFF_HEADS * 2 * DIFF_DH
    ne = DIFF_HEADS * (DIFF_DV + ATT_SUM_ROWS)
    col = pl.BlockSpec((DIFF_DH, 1), lambda i: (0, 0))
    return pl.pallas_call(
        _qkprep_kernel,
        grid=(s // tm,),
        in_specs=[pl.BlockSpec((n, tm), lambda i: (row_q // n, i)),
                  pl.BlockSpec((n, tm), lambda i: (row_k // n, i)),
                  pl.BlockSpec((n, tm), lambda i: (row_v // n, i)),
                  pl.BlockSpec((1, tm), lambda i: (0, i)),
                  pl.BlockSpec((ROT_DIM // 2, 1), lambda i: (0, 0)), col, col],
        out_specs=[pl.BlockSpec((n, tm), lambda i: (0, i)),
                   pl.BlockSpec((n, tm), lambda i: (0, i)),
                   pl.BlockSpec((tm, n), lambda i: (i, 0)),
                   pl.BlockSpec((ne, tm), lambda i: (0, i)),
                   pl.BlockSpec((DIFF_HEADS, 2, tm), lambda i: (0, 0, i)),
                   pl.BlockSpec((DIFF_HEADS, 2, tm), lambda i: (0, 0, i))],
        out_shape=[jax.ShapeDtypeStruct((n, s), BF16), jax.ShapeDtypeStruct((n, s), BF16),
                   jax.ShapeDtypeStruct((s, n), BF16), jax.ShapeDtypeStruct((ne, s), BF16),
                   jax.ShapeDtypeStruct((DIFF_HEADS, 2, s), F32),
                   jax.ShapeDtypeStruct((DIFF_HEADS, 2, s), F32)],
        compiler_params=_cparams(("arbitrary",)),
        name="qkprep",
    )(projt, projt, projt, pos_row, invf_col, qg_col, kg_col)


ATT_COLS = 256
ATT_LOOKAHEAD = {True: 2, False: 4}
ATT_SUM_ROWS = 16
ATT_BOUND_SLACK = 1.02
ATT_BOUND_LIMIT = 50.0


def _diffattn_kernel(qa_ref, qb_ref, k_ref, vt_ref, qn_ref, kn_ref, lq1_ref, lk1_ref, lq2_ref,
                     lk2_ref, sg_ref, o_ref, *scr, lambda_init, tk, cols):
    tq = qa_ref.shape[1]
    nblk = 2 * tq // cols
    q_scr, m_scr, acc_scr = (scr[b * nblk:(b + 1) * nblk] for b in range(3))
    s_scr = scr[3 * nblk:]
    i = pl.program_id(1)
    kmax = jnp.max(kn_ref[...], axis=1, keepdims=True)
    bound = qn_ref[...] * kmax * ATT_BOUND_SLACK
    bound = jnp.concatenate([bound[0:1], bound[1:2]], axis=1)
    bounded = jnp.max(bound) < ATT_BOUND_LIMIT
    for c in range(nblk):
        src = qa_ref if c * cols < tq else qb_ref
        off = (c * cols) % tq
        q_scr[c][...] = src[:, off:off + cols]
        m_scr[c][...] = jnp.where(bounded, bound[:, c * cols:(c + 1) * cols], NEG_BIG)
        acc_scr[c][...] = jnp.zeros_like(acc_scr[c])

    def scores(j, c):
        start = pl.multiple_of(j * tk, tk)
        return jnp.dot(k_ref[pl.ds(start, tk), :], q_scr[c][...], preferred_element_type=F32)

    def steps(tiles, masked, next_tile, fixed):
        look = ATT_LOOKAHEAD[fixed]
        items = [(j, c) for j in tiles for c in range(nblk)]
        pending = []
        for n, (j, c) in enumerate(items):
            s = s_scr[n][...] if n < look else pending.pop(0)
            ahead = n + look
            if ahead < len(items):
                pending.append(scores(*items[ahead]))
            elif next_tile is not None:
                s_scr[ahead - len(items)][...] = scores(next_tile, ahead - len(items))
            start = pl.multiple_of(j * tk, tk)
            if masked:
                krow = lax.broadcasted_iota(jnp.int32, (tk, cols), 0)
                qcol = lax.broadcasted_iota(jnp.int32, (tk, cols), 1)
                qpos = i * tq + (c * cols) % tq + qcol
                s = jnp.where((start + krow) // CHUNK <= qpos // CHUNK, s, NEG_BIG)
            vj = vt_ref[:, pl.ds(start, tk)]
            if fixed:
                p = jnp.exp2(s - m_scr[c][...]).astype(BF16)
                acc_scr[c][...] += jnp.dot(vj, p, preferred_element_type=F32)
            else:
                m_prev = m_scr[c][...]
                m_new = jnp.maximum(m_prev, jnp.max(s, axis=0, keepdims=True))
                alpha = jnp.exp2(m_prev - m_new)
                p = jnp.exp2((s - m_new).astype(BF16))
                acc_scr[c][...] = alpha * acc_scr[c][...] + jnp.dot(
                    vj, p, preferred_element_type=F32)
                m_scr[c][...] = m_new

    def attend(fixed):
        for c in range(ATT_LOOKAHEAD[fixed]):
            s_scr[c][...] = scores(0, c)
        lax.fori_loop(0, i // 2,
                      lambda t, c: (steps((2 * t, 2 * t + 1), False, 2 * t + 2, fixed), c)[1], 0)

        @pl.when(i % 2 == 1)
        def _():
            steps((i - 1,), False, i, fixed)

        steps((i,), True, None, fixed)

    @pl.when(bounded)
    def _():
        attend(True)

    @pl.when(jnp.logical_not(bounded))
    def _():
        attend(False)

    o = jnp.concatenate([acc_scr[c][:DIFF_DV] / acc_scr[c][DIFF_DV:DIFF_DV + 1]
                         for c in range(nblk)], axis=1)
    lam = (jnp.exp(jnp.sum(lq1_ref[...] * lk1_ref[...]))
           - jnp.exp(jnp.sum(lq2_ref[...] * lk2_ref[...])) + lambda_init)
    o = o[:, :tq] - lam * o[:, tq:]
    o = o * lax.rsqrt(jnp.mean(o * o, axis=0, keepdims=True) + EPS) * sg_ref[...]
    o_ref[...] = (o * (1.0 - lambda_init)).T.astype(BF16)


def _diffattn(qat, qbt, kr, vte, qn, kn, lq1, lk1, lq2, lk2, sg_col, lambda_init, tq, tk):
    s = kr.shape[0]
    hd = 2 * DIFF_DH
    vec = pl.BlockSpec((1, DIFF_DH), lambda h, i: (0, 0))
    cols = min(ATT_COLS, tq)
    nblk = 2 * tq // cols
    kern = functools.partial(_diffattn_kernel, lambda_init=lambda_init, tk=tk, cols=cols)
    dve = DIFF_DV + ATT_SUM_ROWS
    assert tq == tk and nblk >= max(ATT_LOOKAHEAD.values())
    scratch = ([pltpu.VMEM((hd, cols), BF16)] * nblk + [pltpu.VMEM((1, cols), F32)] * nblk
               + [pltpu.VMEM((dve, cols), F32)] * nblk
               + [pltpu.VMEM((tk, cols), F32)] * max(ATT_LOOKAHEAD.values()))
    return pl.pallas_call(
        kern,
        grid=(DIFF_HEADS, s // tq),
        in_specs=[pl.BlockSpec((hd, tq), lambda h, i: (h, i)),
                  pl.BlockSpec((hd, tq), lambda h, i: (h, i)),
                  pl.BlockSpec((s, hd), lambda h, i: (0, h)),
                  pl.BlockSpec((dve, s), lambda h, i: (h, 0)),
                  pl.BlockSpec((None, 2, tq), lambda h, i: (h, 0, i)),
                  pl.BlockSpec((None, 2, s), lambda h, i: (h, 0, 0)),
                  vec, vec, vec, vec,
                  pl.BlockSpec((DIFF_DV, 1), lambda h, i: (0, 0))],
        out_specs=pl.BlockSpec((tq, DIFF_DV), lambda h, i: (i, h)),
        out_shape=jax.ShapeDtypeStruct((s, DIFF_HEADS * DIFF_DV), BF16),
        scratch_shapes=scratch,
        compiler_params=_cparams(("arbitrary", "arbitrary")),
        name="diffattn",
    )(qat, qbt, kr, vte, qn, kn, lq1, lk1, lq2, lk2, sg_col)


def _mergeout_kernel(og_ref, od_ref, mg_ref, md_ref, x_ref, wbg_ref, wbd_ref, wo_ref, gt_ref,
                     g2_ref, sc_ref, sh_ref, x1_ref, hf_ref, hfp_ref):
    bg = jnp.dot(og_ref[...], wbg_ref[...], preferred_element_type=F32)
    bd = jnp.dot(od_ref[...], wbd_ref[...], preferred_element_type=F32)
    merged = (jax.nn.sigmoid(mg_ref[...].astype(F32)) * bg
              + jax.nn.sigmoid(md_ref[...].astype(F32)) * bd)
    x1 = x_ref[...] + gt_ref[...] * jnp.dot(merged.astype(BF16), wo_ref[...],
                                             preferred_element_type=F32)
    x1_ref[...] = x1
    hf = _rms_mod(x1, g2_ref[...], sc_ref[...], sh_ref[...])
    hf_ref[...] = hf
    hfp_ref[...] = _pack_halves(hf)


def _mergeout(og, od, proj, x, wbg, wbd, wo, gt, g2, sc, sh, tm, col_mg, col_md):
    s, d = x.shape
    vec = pl.BlockSpec((1, d), lambda i: (0, 0))
    wspec = pl.BlockSpec((d, d), lambda i: (0, 0))
    row = pl.BlockSpec((tm, d), lambda i: (i, 0))
    return pl.pallas_call(
        _mergeout_kernel,
        grid=(s // tm,),
        in_specs=[row, row,
                  pl.BlockSpec((tm, d), lambda i: (i, col_mg // d)),
                  pl.BlockSpec((tm, d), lambda i: (i, col_md // d)),
                  row, wspec, wspec, wspec, vec, vec, vec, vec],
        out_specs=[row, row, pl.BlockSpec((tm, d // 2), lambda i: (i, 0))],
        out_shape=[jax.ShapeDtypeStruct((s, d), F32), jax.ShapeDtypeStruct((s, d), F32),
                   jax.ShapeDtypeStruct((s, d // 2), jnp.uint32)],
        compiler_params=_cparams(("arbitrary",)),
        name="mergeout",
    )(og, od, proj, proj, x, wbg, wbd, wo, gt, g2, sc, sh)


def _route_kernel(hf_ref, wrt_ref, bias_ref, idx_ref, wts_ref, rnk_ref, cnt_ref, run_scr):
    tr = hf_ref.shape[0]
    e = wrt_ref.shape[0]
    gsz = e // N_GROUPS

    @pl.when(pl.program_id(0) == 0)
    def _():
        run_scr[...] = jnp.zeros_like(run_scr)

    logits = _nt_dot(wrt_ref[...], hf_ref[...], precision=HIGHEST)
    scores = jax.nn.sigmoid(logits)
    biased = scores + bias_ref[...]
    g3 = biased.reshape(N_GROUPS, gsz, tr)
    m1 = jnp.max(g3, axis=1, keepdims=True)
    n_top = jnp.sum(jnp.where(g3 == m1, 1.0, 0.0), axis=1, keepdims=True)
    m2 = jnp.max(jnp.where(g3 < m1, g3, -jnp.inf), axis=1, keepdims=True)
    gs = (m1 + jnp.where(n_top >= 2.0, m1, m2)).reshape(N_GROUPS, tr)
    gi = lax.broadcasted_iota(jnp.int32, (N_GROUPS, tr), 0)
    beaten = jnp.zeros((N_GROUPS, tr), F32)
    for g in range(N_GROUPS):
        other = gs[g:g + 1, :]
        beaten = beaten + jnp.where((other > gs) | ((other == gs) & (g < gi)), 1.0, 0.0)
    gsel = (beaten < float(TOPK_GROUPS)).reshape(N_GROUPS, 1, tr)
    masked = jnp.where(gsel, g3, -jnp.inf).reshape(e, tr)

    ids = lax.broadcasted_iota(jnp.int32, (e, tr), 0)
    chosen = jnp.zeros((e, tr), F32)
    sel_idx, sel_score = [], []
    for _ in range(TOP_K):
        mx = jnp.max(masked, axis=0, keepdims=True)
        ix = jnp.min(jnp.where(masked == mx, ids, e), axis=0, keepdims=True)
        hit = ids == ix
        sel_idx.append(ix)
        sel_score.append(jnp.sum(jnp.where(hit, scores, 0.0), axis=0, keepdims=True))
        chosen = jnp.where(hit, 1.0, chosen)
        masked = jnp.where(hit, -jnp.inf, masked)
    idx = jnp.concatenate(sel_idx, axis=0)
    sc = jnp.concatenate(sel_score, axis=0)
    idx_ref[...] = idx
    wts_ref[...] = sc / jnp.sum(sc, axis=0, keepdims=True) * ROUTED_SCALE

    row = lax.broadcasted_iota(jnp.int32, (tr, tr), 0)
    col = lax.broadcasted_iota(jnp.int32, (tr, tr), 1)
    before = jnp.where(row < col, 1.0, 0.0).astype(BF16)
    prior = jnp.dot(chosen.astype(BF16), before, preferred_element_type=F32) + run_scr[:, 0:1]
    rnk_ref[...] = jnp.concatenate(
        [jnp.sum(jnp.where(ids == sel_idx[k], prior, 0.0), axis=0, keepdims=True)
         for k in range(TOP_K)], axis=0).astype(jnp.int32)
    run_scr[...] = run_scr[...] + jnp.sum(chosen, axis=1, keepdims=True)
    cnt_ref[...] = run_scr[...].astype(jnp.int32)


def _route(hf, wrt, bias_col, tr):
    s, d = hf.shape
    e = wrt.shape[0]
    tok = pl.BlockSpec((TOP_K, tr), lambda i: (0, i))
    return pl.pallas_call(
        _route_kernel,
        grid=(s // tr,),
        in_specs=[pl.BlockSpec((tr, d), lambda i: (i, 0)),
                  pl.BlockSpec((e, d), lambda i: (0, 0)),
                  pl.BlockSpec((e, 1), lambda i: (0, 0))],
        out_specs=[tok, tok, tok, pl.BlockSpec((e, 128), lambda i: (0, 0))],
        out_shape=[jax.ShapeDtypeStruct((TOP_K, s), jnp.int32),
                   jax.ShapeDtypeStruct((TOP_K, s), F32),
                   jax.ShapeDtypeStruct((TOP_K, s), jnp.int32),
                   jax.ShapeDtypeStruct((e, 128), jnp.int32)],
        scratch_shapes=[pltpu.VMEM((e, 128), F32)],
        compiler_params=_cparams(("arbitrary",)),
        name="route",
    )(hf, wrt, bias_col)


def _positions_kernel(idx_ref, rnk_ref, pstart_ref, pos_ref):
    e = pstart_ref.shape[0]
    ts = idx_ref.shape[1]
    ids = lax.broadcasted_iota(jnp.int32, (e, ts), 0)
    idx = idx_ref[...]
    pos_ref[...] = rnk_ref[...] + jnp.concatenate(
        [jnp.sum(jnp.where(ids == idx[k:k + 1, :], pstart_ref[...], 0), axis=0, keepdims=True)
         for k in range(TOP_K)], axis=0)


def _positions(idx, rnk, pstart_col, ts):
    s = idx.shape[1]
    e = pstart_col.shape[0]
    tok = pl.BlockSpec((TOP_K, ts), lambda i: (0, i))
    return pl.pallas_call(
        _positions_kernel,
        grid=(s // ts,),
        in_specs=[tok, tok, pl.BlockSpec((e, 1), lambda i: (0, 0))],
        out_specs=tok,
        out_shape=jax.ShapeDtypeStruct((TOP_K, s), jnp.int32),
        compiler_params=_cparams(("arbitrary",)),
        name="positions",
    )(idx, rnk, pstart_col)


def _swiglu_packed(xp, wg, wu, wd):
    lo, hi = _unpack_halves(xp)
    lo, hi = lo.astype(BF16), hi.astype(BF16)
    n = lo.shape[1]
    g = (jnp.dot(lo, wg[:n], preferred_element_type=F32)
         + jnp.dot(hi, wg[n:], preferred_element_type=F32))
    u = (jnp.dot(lo, wu[:n], preferred_element_type=F32)
         + jnp.dot(hi, wu[n:], preferred_element_type=F32))
    h = (g * jax.nn.sigmoid(g)) * u
    return jnp.dot(h.astype(BF16), wd[...], preferred_element_type=F32)


def _moe_kernel(ie_ref, ib_ref, first_ref, slot_ref, ne_ref, rows_ref, nv_ref, xs_ref, wg_hbm,
                wu_hbm, wd_hbm, ys_ref, wg_f, wu_f, wd_f, wg_b, wu_b, wd_b, sem):
    del ib_ref
    i = pl.program_id(0)

    def fetch(e, slot):
        return [pltpu.make_async_copy(src.at[e], dst.at[slot], sem.at[slot, n])
                for n, (src, dst) in enumerate(((wg_hbm, wg_f), (wu_hbm, wu_f), (wd_hbm, wd_f)))]

    @pl.when(i == 0)
    def _():
        for cp in fetch(ie_ref[0], 0):
            cp.start()

    @pl.when(i < nv_ref[0])
    def _():
        for slot in range(2):
            @pl.when((first_ref[i] == 1) & (slot_ref[i] == slot))
            def _():
                for cp in fetch(ie_ref[i], slot):
                    cp.wait()
                wg_b[...] = wg_f[slot].astype(BF16)
                wu_b[...] = wu_f[slot].astype(BF16)
                wd_b[...] = wd_f[slot].astype(BF16)

                @pl.when(ne_ref[i] >= 0)
                def _():
                    for cp in fetch(ne_ref[i], 1 - slot):
                        cp.start()

        row = lax.broadcasted_iota(jnp.int32, xs_ref.shape, 0)
        xp = jnp.where(row < rows_ref[i], xs_ref[...], jnp.uint32(0))
        ys_ref[...] = _pack_halves(_swiglu_packed(xp, wg_b, wu_b, wd_b))


def _moe(item_e, item_b, item_first, item_slot, item_next, item_rows, n_valid, xs, wg, wu, wd):
    m_pad, dh = xs.shape
    _, d, f = wg.shape
    n_items = item_e.shape[0]
    blk = lambda i, ie, ib, fi, sl, ne, nr, nv: (ib[i], 0)
    hbm = pl.BlockSpec(memory_space=pl.ANY)
    return pl.pallas_call(
        _moe_kernel,
        grid_spec=pltpu.PrefetchScalarGridSpec(
            num_scalar_prefetch=7,
            grid=(n_items,),
            in_specs=[pl.BlockSpec((MOE_ROWS, dh), blk), hbm, hbm, hbm],
            out_specs=pl.BlockSpec((MOE_ROWS, dh), blk),
            scratch_shapes=[pltpu.VMEM((2, d, f), F32), pltpu.VMEM((2, d, f), F32),
                            pltpu.VMEM((2, f, d), F32),
                            pltpu.VMEM((d, f), BF16), pltpu.VMEM((d, f), BF16),
                            pltpu.VMEM((f, d), BF16),
                            pltpu.SemaphoreType.DMA((2, 3))],
        ),
        out_shape=jax.ShapeDtypeStruct((m_pad, dh), jnp.uint32),
        compiler_params=_cparams(("arbitrary",)),
        name="moe",
    )(item_e, item_b, item_first, item_slot, item_next, item_rows, n_valid, xs, wg, wu, wd)


def _sc_gather_rows(table, idx_row):
    m = idx_row.shape[1]
    w = table.shape[1]
    idx_row = idx_row.reshape(m // SC_GATHER_WINDOW, SC_GATHER_WINDOW)
    mesh = plsc.VectorSubcoreMesh(core_axis_name="c", subcore_axis_name="s")

    @functools.partial(pl.kernel, mesh=mesh,
                       out_type=jax.ShapeDtypeStruct((m, w), table.dtype))
    def gather(table_hbm, idx_hbm, out_hbm):
        def body(idx_vmem, out_vmem):
            pltpu.sync_copy(table_hbm.at[idx_vmem.at[0]], out_vmem)

        pltpu.emit_pipeline(
            body,
            grid=(m // SC_GATHER_WINDOW,),
            in_specs=[pl.BlockSpec((1, SC_GATHER_WINDOW), lambda i: (i, 0))],
            out_specs=[pl.BlockSpec((SC_GATHER_WINDOW, w), lambda i: (i, 0))],
            core_axis_name=("c", "s"),
            dimension_semantics=(pltpu.PARALLEL,),
        )(idx_hbm, out_hbm)

    return gather(table, idx_row)


def _sc_scatter_rows(rows, idx_blocks, m_out):
    s, w = rows.shape
    mesh = plsc.VectorSubcoreMesh(core_axis_name="c", subcore_axis_name="s")

    @functools.partial(pl.kernel, mesh=mesh,
                       out_type=jax.ShapeDtypeStruct((m_out, w), rows.dtype))
    def scatter(rows_hbm, idx_hbm, out_hbm):
        def body(rows_vmem, idx_vmem):
            for k in range(TOP_K):
                pltpu.sync_copy(rows_vmem, out_hbm.at[idx_vmem.at[k]])

        pltpu.emit_pipeline(
            body,
            grid=(s // SC_GATHER_WINDOW,),
            in_specs=[pl.BlockSpec((SC_GATHER_WINDOW, w), lambda i: (i, 0)),
                      pl.BlockSpec((TOP_K, SC_GATHER_WINDOW), lambda i: (i, 0))],
            out_specs=[],
            core_axis_name=("c", "s"),
            dimension_semantics=(pltpu.PARALLEL,),
        )(rows_hbm, idx_hbm)

    return scatter(rows, idx_blocks)


def _combine_kernel(wt_ref, hf_ref, x1_ref, gt_ref, sg_ref, su_ref, sd_ref, g_ref, o_ref):
    tc = x1_ref.shape[0]
    y = _swiglu_packed(hf_ref[...], sg_ref, su_ref, sd_ref)
    wt = wt_ref[...]
    n = g_ref.shape[2]
    r_lo = jnp.zeros((tc, n), F32)
    r_hi = jnp.zeros((tc, n), F32)
    for k in range(TOP_K):
        lo, hi = _unpack_halves(g_ref[k])
        r_lo = r_lo + lo * wt[:, k:k + 1]
        r_hi = r_hi + hi * wt[:, k:k + 1]
    y = y + jnp.concatenate([r_lo, r_hi], axis=1)
    o_ref[...] = x1_ref[...] + gt_ref[...] * y


def _combine(wts_t, hfp, x1, gt, sg, su, sd, gathered, tc):
    s, d = x1.shape
    f = sg.shape[1]
    row = pl.BlockSpec((tc, d), lambda i: (i, 0))
    return pl.pallas_call(
        _combine_kernel,
        grid=(s // tc,),
        in_specs=[pl.BlockSpec((tc, TOP_K), lambda i: (i, 0)),
                  pl.BlockSpec((tc, d // 2), lambda i: (i, 0)), row,
                  pl.BlockSpec((1, d), lambda i: (0, 0)),
                  pl.BlockSpec((d, f), lambda i: (0, 0)),
                  pl.BlockSpec((d, f), lambda i: (0, 0)),
                  pl.BlockSpec((f, d), lambda i: (0, 0)),
                  pl.BlockSpec((TOP_K, tc, d // 2), lambda i: (0, i, 0))],
        out_specs=row,
        out_shape=jax.ShapeDtypeStruct((s, d), F32),
        compiler_params=_cparams(("arbitrary",)),
        name="combine",
    )(wts_t, hfp, x1, gt, sg, su, sd, gathered)


def _tile(n, want):
    t = min(n, want)
    assert n % t == 0, (n, t)
    return t


def _layer(l, x, c_col, pos_row, p):
    s, d = x.shape
    lambda_init = 0.8 - 0.6 * math.exp(-0.3 * l)
    gqk, gv = GLA_HEADS * GLA_DK, GLA_HEADS * GLA_DV
    dqk, dvw = DIFF_HEADS * 2 * DIFF_DH, DIFF_HEADS * DIFF_DV
    lowrank = p["gla_w_a2"].shape[0]

    mod = _ada(c_col, p["w_ada"], p["b_ada"][None, :])
    sh_a, sc_a, gt_a, sh_f, sc_f, gt_f = [mod[:, j * d:(j + 1) * d] for j in range(6)]

    w_in = p["w_in"]
    o = 0
    cols = {}
    for name, wdt in (("gq", gqk), ("gk", gqk), ("gv", gv), ("ga", lowrank), ("gg", gv),
                      ("dq", dqk), ("dk", dqk), ("dv", dvw), ("mg", d), ("md", d)):
        cols[name] = w_in[:, o:o + wdt]
        o += wdt
    row_names = ("gv", "gg", "mg", "md", "gq")
    w_row = jnp.concatenate([cols[n] for n in row_names], axis=1).astype(BF16)
    col_of, o = {}, 0
    for n in row_names:
        col_of[n] = o
        o += cols[n].shape[1]
    t_names = ("dq", "dk", "dv", "gk")
    w_t = jnp.concatenate([cols[n] for n in t_names], axis=1).T.astype(BF16)
    row_of, o = {}, 0
    for n in t_names:
        row_of[n] = o
        o += cols[n].shape[1]
    w_ga = jnp.pad(cols["ga"], ((0, 0), (0, 128 - lowrank))).astype(BF16)

    g1 = p["norm1_g"][None, :]
    tm = _tile(s, 1024)
    proj, ga = _inproj(x, g1, sc_a, sh_a, w_row, w_ga, tm, 512)
    projt = _inproj_t(x, g1, sc_a, sh_a, w_t, tm, w_t.shape[0] // 2)

    wa2t = jnp.pad(p["gla_w_a2"].T, ((0, 0), (0, 128 - lowrank)))
    o_gla = _gla(proj, projt, ga, wa2t, p["gla_b_a"][:, None], p["gla_onorm_g"][None, :],
                 _tile(s, 512), col_of["gq"], col_of["gv"], col_of["gg"], row_of["gk"])

    invf = ROPE_THETA ** (-jnp.arange(0, ROT_DIM, 2, dtype=F32) / ROT_DIM)
    qat, qbt, kr, vte, qn, kn = _qkprep(projt, pos_row, invf[:, None], p["diff_qnorm_g"][:, None],
                                p["diff_knorm_g"][:, None], _tile(s, 512), row_of["dq"],
                                row_of["dk"], row_of["dv"])
    tq = _tile(s, 512)
    o_diff = _diffattn(qat, qbt, kr, vte, qn, kn, p["diff_lq1"][None, :], p["diff_lk1"][None, :],
                       p["diff_lq2"][None, :], p["diff_lk2"][None, :],
                       p["diff_subln_g"][:, None], lambda_init, tq, tq)

    x1, hf, hfp = _mergeout(o_gla, o_diff, proj, x, p["w_branch_gla"].astype(BF16),
                       p["w_branch_diff"].astype(BF16), p["w_out"].astype(BF16), gt_a,
                       p["norm2_g"][None, :], sc_f, sh_f, _tile(s, 512),
                       col_of["mg"], col_of["md"])

    e = p["w_router"].shape[1]
    idx, wts, rnk, cnt = _route(hf, p["w_router"].T, p["router_bias"][:, None], _tile(s, 512))

    counts = cnt[:, 0]
    pcounts = ((counts + MOE_ROWS - 1) // MOE_ROWS) * MOE_ROWS
    pend = jnp.cumsum(pcounts)
    pstart = pend - pcounts
    pos = _positions(idx, rnk, pstart[:, None], _tile(s, 512))
    n_items = (s * TOP_K) // MOE_ROWS + e
    n_valid = (pend[-1] // MOE_ROWS).astype(jnp.int32)
    item_b = jnp.minimum(jnp.arange(n_items, dtype=jnp.int32), n_valid - 1)
    item_e = jnp.minimum(jnp.sum(pend[None, :] <= (item_b * MOE_ROWS)[:, None], axis=1),
                         e - 1).astype(jnp.int32)

    wn = SC_GATHER_WINDOW
    pos_w = pos.reshape(TOP_K, s // wn, wn).transpose(1, 0, 2).reshape(s // wn * TOP_K, wn)
    xs = _sc_scatter_rows(hfp, pos_w, n_items * MOE_ROWS)
    item_rows = jnp.clip(pstart[item_e] + counts[item_e] - item_b * MOE_ROWS, 0,
                         MOE_ROWS).astype(jnp.int32)
    prev_e = jnp.concatenate([jnp.full((1,), -1, jnp.int32), item_e[:-1]])
    item_first = ((jnp.arange(n_items) < n_valid) & (item_e != prev_e)).astype(jnp.int32)
    item_slot = ((jnp.cumsum(item_first) - 1) % 2).astype(jnp.int32)
    cand = jnp.where(pcounts > 0, jnp.arange(e, dtype=jnp.int32), e)
    following = jnp.concatenate([lax.cummin(cand[::-1])[::-1][1:], jnp.full((1,), e, jnp.int32)])
    item_next = jnp.where(following[item_e] < e, following[item_e], -1).astype(jnp.int32)
    ys = _moe(item_e, item_b, item_first, item_slot, item_next, item_rows, n_valid[None], xs,
              p["w_exp_gate"], p["w_exp_up"], p["w_exp_down"])
    gathered = _sc_gather_rows(ys, pos.reshape(1, TOP_K * s)).reshape(TOP_K, s, d // 2)
    return _combine(wts.T, hfp, x1, gt_f, p["w_sh_gate"].astype(BF16),
                    p["w_sh_up"].astype(BF16), p["w_sh_down"].astype(BF16), gathered,
                    _tile(s, 256))


_LAYER_PARAMS = ("w_ada", "b_ada", "norm1_g", "w_in", "gla_w_a2", "gla_b_a", "gla_onorm_g",
                 "diff_qnorm_g", "diff_knorm_g", "diff_lq1", "diff_lk1", "diff_lq2", "diff_lk2",
                 "diff_subln_g", "w_branch_gla", "w_branch_diff", "w_out", "norm2_g", "w_router",
                 "router_bias", "w_exp_gate", "w_exp_up", "w_exp_down", "w_sh_gate", "w_sh_up",
                 "w_sh_down")


def kernel(x, c, positions, w_ada, b_ada, norm1_g, w_in, gla_w_a2, gla_b_a, gla_onorm_g, diff_qnorm_g, diff_knorm_g, diff_lq1, diff_lk1, diff_lq2, diff_lk2, diff_subln_g, w_branch_gla, w_branch_diff, w_out, norm2_g, w_router, router_bias, w_exp_gate, w_exp_up, w_exp_down, w_sh_gate, w_sh_up, w_sh_down):
    stacked = dict(zip(_LAYER_PARAMS, (
        w_ada, b_ada, norm1_g, w_in, gla_w_a2, gla_b_a, gla_onorm_g, diff_qnorm_g, diff_knorm_g,
        diff_lq1, diff_lk1, diff_lq2, diff_lk2, diff_subln_g, w_branch_gla, w_branch_diff, w_out,
        norm2_g, w_router, router_bias, w_exp_gate, w_exp_up, w_exp_down, w_sh_gate, w_sh_up,
        w_sh_down)))
    b, s, d = x.shape
    assert b == 1, "single-sequence kernel"
    xl = x[0]
    c_col = c[0][:, None]
    pos_row = positions.astype(jnp.int32)
    for l in range(w_ada.shape[0]):
        xl = _layer(l, xl, c_col, pos_row, {k: v[l] for k, v in stacked.items()})
    return xl[None]
```

```python
import functools
import math

import jax
import jax.numpy as jnp
from jax import lax
from jax.experimental import pallas as pl
from jax.experimental.pallas import tpu as pltpu
from jax.experimental.pallas import tpu_sc as plsc

CHUNK = 64
EPS = 1e-6
GLA_HEADS = 4
GLA_DK = 128
GLA_DV = 256
GLA_TAU = 16.0
DIFF_HEADS = 8
DIFF_DH = 64
DIFF_DV = 2 * DIFF_DH
ROPE_THETA = 500000.0
ROT_DIM = DIFF_DH // 4
N_GROUPS = 8
TOPK_GROUPS = 4
TOP_K = 8
ROUTED_SCALE = 2.5

MOE_ROWS = 512
SC_GATHER_WINDOW = 64
VMEM_LIMIT = 56 * 1024 * 1024
NEG_BIG = -1e30
LOG2E = 1.4426950408889634
HIGHEST = lax.Precision.HIGHEST
F32 = jnp.float32
BF16 = jnp.bfloat16


def _cparams(sem):
    return pltpu.CompilerParams(dimension_semantics=sem, vmem_limit_bytes=VMEM_LIMIT)


def _nt_dot(a, b, precision=None):
    return lax.dot_general(a, b, (((1,), (1,)), ((), ())), precision=precision,
                           preferred_element_type=F32)


def _pack_halves(x):
    n = x.shape[1] // 2
    lo = pltpu.bitcast(x[:, :n].astype(BF16).astype(F32), jnp.uint32) >> 16
    hi = pltpu.bitcast(x[:, n:].astype(BF16).astype(F32), jnp.uint32) & jnp.uint32(0xFFFF0000)
    return lo | hi


def _unpack_halves(w):
    return (pltpu.bitcast(w << 16, F32), pltpu.bitcast(w & jnp.uint32(0xFFFF0000), F32))


def _rms_mod(x, g, sc, sh):
    xn = x * lax.rsqrt(jnp.mean(x * x, axis=-1, keepdims=True) + EPS)
    return (xn * g) * (1.0 + sc) + sh


def _ada_kernel(c_ref, w_ref, b_ref, o_ref):
    c = c_ref[...]
    ca = c * jax.nn.sigmoid(c)
    o_ref[...] = jnp.sum(ca * w_ref[...], axis=0, keepdims=True) + b_ref[...]


def _ada(c_col, w, b):
    d, n = w.shape
    tn = min(1024, n)
    return pl.pallas_call(
        _ada_kernel,
        grid=(n // tn,),
        in_specs=[pl.BlockSpec((d, 1), lambda j: (0, 0)),
                  pl.BlockSpec((d, tn), lambda j: (0, j)),
                  pl.BlockSpec((1, tn), lambda j: (0, j))],
        out_specs=pl.BlockSpec((1, tn), lambda j: (0, j)),
        out_shape=jax.ShapeDtypeStruct((1, n), F32),
        compiler_params=_cparams(("arbitrary",)),
        name="ada",
    )(c_col, w, b)


def _inproj_kernel(x_ref, g_ref, sc_ref, sh_ref, w_ref, wga_ref, o_ref, ga_ref, h_scr):
    @pl.when(pl.program_id(1) == 0)
    def _():
        h = _rms_mod(x_ref[...], g_ref[...], sc_ref[...], sh_ref[...]).astype(BF16)
        h_scr[...] = h
        ga_ref[...] = jnp.dot(h, wga_ref[...], preferred_element_type=F32)

    o_ref[...] = jnp.dot(h_scr[...], w_ref[...], preferred_element_type=F32).astype(BF16)


def _inproj(x, g, sc, sh, w, wga, tm, tn):
    s, d = x.shape
    n = w.shape[1]
    vec = pl.BlockSpec((1, d), lambda i, j: (0, 0))
    return pl.pallas_call(
        _inproj_kernel,
        grid=(s // tm, n // tn),
        in_specs=[pl.BlockSpec((tm, d), lambda i, j: (i, 0)), vec, vec, vec,
                  pl.BlockSpec((d, tn), lambda i, j: (0, j)),
                  pl.BlockSpec((d, 128), lambda i, j: (0, 0))],
        out_specs=[pl.BlockSpec((tm, tn), lambda i, j: (i, j)),
                   pl.BlockSpec((tm, 128), lambda i, j: (i, 0))],
        out_shape=[jax.ShapeDtypeStruct((s, n), BF16), jax.ShapeDtypeStruct((s, 128), F32)],
        scratch_shapes=[pltpu.VMEM((tm, d), BF16)],
        compiler_params=_cparams(("arbitrary", "arbitrary")),
        name="inproj",
    )(x, g, sc, sh, w, wga)


def _inproj_t_kernel(x_ref, g_ref, sc_ref, sh_ref, wt_ref, o_ref, h_scr):
    @pl.when(pl.program_id(1) == 0)
    def _():
        h_scr[...] = _rms_mod(x_ref[...], g_ref[...], sc_ref[...], sh_ref[...]).astype(BF16)

    o_ref[...] = _nt_dot(wt_ref[...], h_scr[...]).astype(BF16)


def _inproj_t(x, g, sc, sh, wt, tm, tn):
    s, d = x.shape
    n = wt.shape[0]
    vec = pl.BlockSpec((1, d), lambda i, j: (0, 0))
    return pl.pallas_call(
        _inproj_t_kernel,
        grid=(s // tm, n // tn),
        in_specs=[pl.BlockSpec((tm, d), lambda i, j: (i, 0)), vec, vec, vec,
                  pl.BlockSpec((tn, d), lambda i, j: (j, 0))],
        out_specs=pl.BlockSpec((tn, tm), lambda i, j: (j, i)),
        out_shape=jax.ShapeDtypeStruct((n, s), BF16),
        scratch_shapes=[pltpu.VMEM((tm, d), BF16)],
        compiler_params=_cparams(("arbitrary", "arbitrary")),
        name="inproj_t",
    )(x, g, sc, sh, wt)


def _gla_kernel(q_ref, kt_ref, v_ref, gg_ref, ga_ref, wa2t_ref, ba_ref, on_ref, o_ref,
                state_ref, o_scr):
    tt = q_ref.shape[0]
    nchunk = tt // CHUNK

    @pl.when(pl.program_id(0) == 0)
    def _():
        state_ref[...] = jnp.zeros_like(state_ref)

    zt = _nt_dot(wa2t_ref[...], ga_ref[...], precision=HIGHEST) + ba_ref[...]
    lat = (jnp.minimum(zt, 0.0) - jnp.log1p(jnp.exp(-jnp.abs(zt)))) * (1.0 / GLA_TAU)
    row = lax.broadcasted_iota(jnp.int32, (tt, tt), 0)
    col = lax.broadcasted_iota(jnp.int32, (tt, tt), 1)
    same = (row // CHUNK) == (col // CHUNK)
    incl = jnp.where(same & (row <= col), 1.0, 0.0).astype(BF16)
    full = jnp.where(same, 1.0, 0.0).astype(BF16)
    lat_hi = lat.astype(BF16)
    lat_lo = (lat - lat_hi.astype(F32)).astype(BF16)
    cumt = (jnp.dot(lat_hi, incl, preferred_element_type=F32)
            + jnp.dot(lat_lo, incl, preferred_element_type=F32))
    tott = (jnp.dot(lat_hi, full, preferred_element_type=F32)
            + jnp.dot(lat_lo, full, preferred_element_type=F32))
    kdt = kt_ref[...].astype(F32) * jnp.exp(tott - cumt)
    dec = jnp.exp(tott)

    lane = lax.broadcasted_iota(jnp.int32, (GLA_DK, 2 * CHUNK), 1)
    for c in range(nchunk):
        pair = (c // 2) * 2 * CHUNK
        if nchunk > 1:
            keep = (lane // CHUNK) == (c % 2)
        for h in range(GLA_HEADS):
            rows = slice(h * GLA_DK, (h + 1) * GLA_DK)
            vcols = slice(h * GLA_DV, (h + 1) * GLA_DV)
            if nchunk > 1:
                a = jnp.where(keep, kdt[rows, pair:pair + 2 * CHUNK], 0.0).astype(BF16)
                vp = v_ref[pair:pair + 2 * CHUNK, vcols]
            else:
                a = kdt[rows, :].astype(BF16)
                vp = v_ref[:, vcols]
            upd = jnp.dot(a, vp, preferred_element_type=F32)
            dcol = dec[rows, c * CHUNK:c * CHUNK + 1]
            st = state_ref[h] * dcol + upd
            state_ref[h] = st
            qc = q_ref[c * CHUNK:(c + 1) * CHUNK, rows]
            o_scr[c * CHUNK:(c + 1) * CHUNK, vcols] = jnp.dot(
                qc, st.astype(BF16), preferred_element_type=F32)

    for h in range(GLA_HEADS):
        vcols = slice(h * GLA_DV, (h + 1) * GLA_DV)
        o = o_scr[:, vcols] * (GLA_DK ** -0.5)
        o = o * lax.rsqrt(jnp.mean(o * o, axis=-1, keepdims=True) + EPS) * on_ref[...]
        g = gg_ref[:, vcols].astype(F32)
        o_ref[:, vcols] = (o * (g * jax.nn.sigmoid(g))).astype(BF16)


def _gla(proj, projt, ga, wa2t, ba_col, on_g, tt, col_q, col_v, col_g, row_k):
    s = proj.shape[0]
    qk = GLA_HEADS * GLA_DK
    vw = GLA_HEADS * GLA_DV
    return pl.pallas_call(
        _gla_kernel,
        grid=(s // tt,),
        in_specs=[pl.BlockSpec((tt, qk), lambda i: (i, col_q // qk)),
                  pl.BlockSpec((qk, tt), lambda i: (row_k // qk, i)),
                  pl.BlockSpec((tt, vw), lambda i: (i, col_v // vw)),
                  pl.BlockSpec((tt, vw), lambda i: (i, col_g // vw)),
                  pl.BlockSpec((tt, 128), lambda i: (i, 0)),
                  pl.BlockSpec((qk, 128), lambda i: (0, 0)),
                  pl.BlockSpec((qk, 1), lambda i: (0, 0)),
                  pl.BlockSpec((1, GLA_DV), lambda i: (0, 0))],
        out_specs=pl.BlockSpec((tt, vw), lambda i: (i, 0)),
        out_shape=jax.ShapeDtypeStruct((s, vw), BF16),
        scratch_shapes=[pltpu.VMEM((GLA_HEADS, GLA_DK, GLA_DV), F32),
                        pltpu.VMEM((tt, vw), F32)],
        compiler_params=_cparams(("arbitrary",)),
        name="gla",
    )(proj, projt, proj, proj, ga, wa2t, ba_col, on_g)


def _qknorm_rope_t(xt, g_col, cos, sin):
    n, tm = xt.shape
    x3 = xt.reshape(n // DIFF_DH, DIFF_DH, tm)
    r = lax.rsqrt(jnp.mean(x3 * x3, axis=1, keepdims=True) + EPS)
    y = x3 * r * g_col[None]
    half = ROT_DIM // 2
    y1, y2, rest = y[:, :half], y[:, half:ROT_DIM], y[:, ROT_DIM:]
    o1 = y1 * cos[None] - y2 * sin[None]
    o2 = y2 * cos[None] + y1 * sin[None]
    return jnp.concatenate([o1, o2, rest], axis=1)


def _seg_norms(x3):
    return jnp.sqrt(jnp.sum(x3 * x3, axis=1)).reshape(DIFF_HEADS, 2, x3.shape[2])


def _qkprep_kernel(qt_ref, kt_ref, vt_ref, pos_ref, invf_ref, qg_ref, kg_ref, qa_ref, qb_ref,
                   ko_ref, ve_ref, qn_ref, kn_ref):
    tm = qt_ref.shape[1]
    v3 = vt_ref[...].reshape(DIFF_HEADS, DIFF_DV, tm)
    ones = jnp.ones((DIFF_HEADS, ATT_SUM_ROWS, tm), BF16)
    ve_ref[...] = jnp.concatenate([v3, ones], axis=1).reshape(-1, tm)
    ang = pos_ref[...].astype(F32) * invf_ref[...]
    cos, sin = jnp.cos(ang), jnp.sin(ang)
    k3 = _qknorm_rope_t(kt_ref[...].astype(F32), kg_ref[...], cos, sin)
    ko_ref[...] = k3.reshape(-1, tm).T.astype(BF16)
    kn_ref[...] = _seg_norms(k3)
    q3 = _qknorm_rope_t(qt_ref[...].astype(F32), qg_ref[...], cos, sin) * (
        DIFF_DH ** -0.5 * LOG2E)
    qn_ref[...] = _seg_norms(q3)
    seg = lax.broadcasted_iota(jnp.int32, q3.shape, 0)
    qa_ref[...] = jnp.where(seg % 2 == 0, q3, 0.0).reshape(-1, tm).astype(BF16)
    qb_ref[...] = jnp.where(seg % 2 == 1, q3, 0.0).reshape(-1, tm).astype(BF16)


def _qkprep(projt, pos_row, invf_col, qg_col, kg_col, tm, row_q, row_k, row_v):
    s = projt.shape[1]
    n = DIFF_HEADS * 2 * DIFF_DH
    ne = DIFF_HEADS * (DIFF_DV + ATT_SUM_ROWS)
    col = pl.BlockSpec((DIFF_DH, 1), lambda i: (0, 0))
    return pl.pallas_call(
        _qkprep_kernel,
        grid=(s // tm,),
        in_specs=[pl.BlockSpec((n, tm), lambda i: (row_q // n, i)),
                  pl.BlockSpec((n, tm), lambda i: (row_k // n, i)),
                  pl.BlockSpec((n, tm), lambda i: (row_v // n, i)),
                  pl.BlockSpec((1, tm), lambda i: (0, i)),
                  pl.BlockSpec((ROT_DIM // 2, 1), lambda i: (0, 0)), col, col],
        out_specs=[pl.BlockSpec((n, tm), lambda i: (0, i)),
                   pl.BlockSpec((n, tm), lambda i: (0, i)),
                   pl.BlockSpec((tm, n), lambda i: (i, 0)),
                   pl.BlockSpec((ne, tm), lambda i: (0, i)),
                   pl.BlockSpec((DIFF_HEADS, 2, tm), lambda i: (0, 0, i)),
                   pl.BlockSpec((DIFF_HEADS, 2, tm), lambda i: (0, 0, i))],
        out_shape=[jax.ShapeDtypeStruct((n, s), BF16), jax.ShapeDtypeStruct((n, s), BF16),
                   jax.ShapeDtypeStruct((s, n), BF16), jax.ShapeDtypeStruct((ne, s), BF16),
                   jax.ShapeDtypeStruct((DIFF_HEADS, 2, s), F32),
                   jax.ShapeDtypeStruct((DIFF_HEADS, 2, s), F32)],
        compiler_params=_cparams(("arbitrary",)),
        name="qkprep",
    )(projt, projt, projt, pos_row, invf_col, qg_col, kg_col)


ATT_COLS = 256
ATT_LOOKAHEAD = {True: 2, False: 4}
ATT_SUM_ROWS = 16
ATT_BOUND_SLACK = 1.02
ATT_BOUND_LIMIT = 50.0


def _diffattn_kernel(qa_ref, qb_ref, k_ref, vt_ref, qn_ref, kn_ref, lq1_ref, lk1_ref, lq2_ref,
                     lk2_ref, sg_ref, o_ref, *scr, lambda_init, tk, cols):
    tq = qa_ref.shape[1]
    nblk = 2 * tq // cols
    m_scr, acc_scr = (scr[b * nblk:(b + 1) * nblk] for b in range(2))
    kmax_scr = scr[2 * nblk]
    s_scr = scr[2 * nblk + 1:]
    i = pl.program_id(1)

    @pl.when(i == 0)
    def _():
        kmax_scr[...] = jnp.max(kn_ref[...], axis=1, keepdims=True)

    bound = qn_ref[...] * kmax_scr[...] * ATT_BOUND_SLACK
    bound = jnp.concatenate([bound[0:1], bound[1:2]], axis=1)
    bounded = jnp.max(bound) < ATT_BOUND_LIMIT
    for c in range(nblk):
        m_scr[c][...] = jnp.where(bounded, bound[:, c * cols:(c + 1) * cols], NEG_BIG)
        acc_scr[c][...] = jnp.zeros_like(acc_scr[c])

    def scores(j, c):
        start = pl.multiple_of(j * tk, tk)
        q_ref = qa_ref if c * cols < tq else qb_ref
        off = (c * cols) % tq
        return jnp.dot(k_ref[pl.ds(start, tk), :], q_ref[:, off:off + cols],
                       preferred_element_type=F32)

    def steps(tiles, masked, next_tile, fixed):
        look = ATT_LOOKAHEAD[fixed]
        items = [(j, c) for j in tiles for c in range(nblk)]
        pending = []
        for n, (j, c) in enumerate(items):
            s = s_scr[n][...] if n < look else pending.pop(0)
            ahead = n + look
            if ahead < len(items):
                pending.append(scores(*items[ahead]))
            elif next_tile is not None:
                s_scr[ahead - len(items)][...] = scores(next_tile, ahead - len(items))
            start = pl.multiple_of(j * tk, tk)
            keys = tk
            if masked:
                keys = (c * cols) % tq + cols
                s = s[:keys]
                krow = lax.broadcasted_iota(jnp.int32, (keys, cols), 0)
                qcol = lax.broadcasted_iota(jnp.int32, (keys, cols), 1)
                qpos = i * tq + (c * cols) % tq + qcol
                s = jnp.where((start + krow) // CHUNK <= qpos // CHUNK, s, NEG_BIG)
            vj = vt_ref[:, pl.ds(start, keys)]
            if fixed:
                p = jnp.exp2(s - m_scr[c][...]).astype(BF16)
                acc_scr[c][...] += jnp.dot(vj, p, preferred_element_type=F32)
            else:
                m_prev = m_scr[c][...]
                m_new = jnp.maximum(m_prev, jnp.max(s, axis=0, keepdims=True))
                alpha = jnp.exp2(m_prev - m_new)
                p = jnp.exp2((s - m_new).astype(BF16))
                acc_scr[c][...] = alpha * acc_scr[c][...] + jnp.dot(
                    vj, p, preferred_element_type=F32)
                m_scr[c][...] = m_new

    def attend(fixed):
        for c in range(ATT_LOOKAHEAD[fixed]):
            s_scr[c][...] = scores(0, c)
        quads = i // 4
        lax.fori_loop(0, quads, lambda t, c: (steps(
            (4 * t, 4 * t + 1, 4 * t + 2, 4 * t + 3), False, 4 * t + 4, fixed), c)[1], 0)

        @pl.when(i % 4 >= 2)
        def _():
            steps((4 * quads, 4 * quads + 1), False, 4 * quads + 2, fixed)

        @pl.when(i % 2 == 1)
        def _():
            steps((i - 1,), False, i, fixed)

        steps((i,), True, None, fixed)

    @pl.when(bounded)
    def _():
        attend(True)

    @pl.when(jnp.logical_not(bounded))
    def _():
        attend(False)

    o = jnp.concatenate([acc_scr[c][:DIFF_DV] / acc_scr[c][DIFF_DV:DIFF_DV + 1]
                         for c in range(nblk)], axis=1)
    lam = (jnp.exp(jnp.sum(lq1_ref[...] * lk1_ref[...]))
           - jnp.exp(jnp.sum(lq2_ref[...] * lk2_ref[...])) + lambda_init)
    o = o[:, :tq] - lam * o[:, tq:]
    o = o * lax.rsqrt(jnp.mean(o * o, axis=0, keepdims=True) + EPS) * sg_ref[...]
    o_ref[...] = (o * (1.0 - lambda_init)).T.astype(BF16)


def _diffattn(qat, qbt, kr, vte, qn, kn, lq1, lk1, lq2, lk2, sg_col, lambda_init, tq, tk):
    s = kr.shape[0]
    hd = 2 * DIFF_DH
    vec = pl.BlockSpec((1, DIFF_DH), lambda h, i: (0, 0))
    cols = min(ATT_COLS, tq)
    nblk = 2 * tq // cols
    kern = functools.partial(_diffattn_kernel, lambda_init=lambda_init, tk=tk, cols=cols)
    dve = DIFF_DV + ATT_SUM_ROWS
    assert tq == tk and nblk >= max(ATT_LOOKAHEAD.values())
    scratch = ([pltpu.VMEM((1, cols), F32)] * nblk + [pltpu.VMEM((dve, cols), F32)] * nblk
               + [pltpu.VMEM((2, 1), F32)]
               + [pltpu.VMEM((tk, cols), F32)] * max(ATT_LOOKAHEAD.values()))
    return pl.pallas_call(
        kern,
        grid=(DIFF_HEADS, s // tq),
        in_specs=[pl.BlockSpec((hd, tq), lambda h, i: (h, i)),
                  pl.BlockSpec((hd, tq), lambda h, i: (h, i)),
                  pl.BlockSpec((s, hd), lambda h, i: (0, h)),
                  pl.BlockSpec((dve, s), lambda h, i: (h, 0)),
                  pl.BlockSpec((None, 2, tq), lambda h, i: (h, 0, i)),
                  pl.BlockSpec((None, 2, s), lambda h, i: (h, 0, 0)),
                  vec, vec, vec, vec,
                  pl.BlockSpec((DIFF_DV, 1), lambda h, i: (0, 0))],
        out_specs=pl.BlockSpec((tq, DIFF_DV), lambda h, i: (i, h)),
        out_shape=jax.ShapeDtypeStruct((s, DIFF_HEADS * DIFF_DV), BF16),
        scratch_shapes=scratch,
        compiler_params=_cparams(("arbitrary", "arbitrary")),
        name="diffattn",
    )(qat, qbt, kr, vte, qn, kn, lq1, lk1, lq2, lk2, sg_col)


def _mergeout_kernel(og_ref, od_ref, mg_ref, md_ref, x_ref, wbg_ref, wbd_ref, wo_ref, gt_ref,
                     g2_ref, sc_ref, sh_ref, x1_ref, hf_ref, hfp_ref):
    bg = jnp.dot(og_ref[...], wbg_ref[...], preferred_element_type=F32)
    bd = jnp.dot(od_ref[...], wbd_ref[...], preferred_element_type=F32)
    merged = (jax.nn.sigmoid(mg_ref[...].astype(F32)) * bg
              + jax.nn.sigmoid(md_ref[...].astype(F32)) * bd)
    x1 = x_ref[...] + gt_ref[...] * jnp.dot(merged.astype(BF16), wo_ref[...],
                                             preferred_element_type=F32)
    x1_ref[...] = x1
    hf = _rms_mod(x1, g2_ref[...], sc_ref[...], sh_ref[...])
    hf_ref[...] = hf
    hfp_ref[...] = _pack_halves(hf)


def _mergeout(og, od, proj, x, wbg, wbd, wo, gt, g2, sc, sh, tm, col_mg, col_md):
    s, d = x.shape
    vec = pl.BlockSpec((1, d), lambda i: (0, 0))
    wspec = pl.BlockSpec((d, d), lambda i: (0, 0))
    row = pl.BlockSpec((tm, d), lambda i: (i, 0))
    return pl.pallas_call(
        _mergeout_kernel,
        grid=(s // tm,),
        in_specs=[row, row,
                  pl.BlockSpec((tm, d), lambda i: (i, col_mg // d)),
                  pl.BlockSpec((tm, d), lambda i: (i, col_md // d)),
                  row, wspec, wspec, wspec, vec, vec, vec, vec],
        out_specs=[row, row, pl.BlockSpec((tm, d // 2), lambda i: (i, 0))],
        out_shape=[jax.ShapeDtypeStruct((s, d), F32), jax.ShapeDtypeStruct((s, d), F32),
                   jax.ShapeDtypeStruct((s, d // 2), jnp.uint32)],
        compiler_params=_cparams(("arbitrary",)),
        name="mergeout",
    )(og, od, proj, proj, x, wbg, wbd, wo, gt, g2, sc, sh)


def _route_kernel(hf_ref, wrt_ref, bias_ref, idx_ref, wts_ref, rnk_ref, cnt_ref, run_scr):
    tr = hf_ref.shape[0]
    e = wrt_ref.shape[0]
    gsz = e // N_GROUPS

    @pl.when(pl.program_id(0) == 0)
    def _():
        run_scr[...] = jnp.zeros_like(run_scr)

    logits = _nt_dot(wrt_ref[...], hf_ref[...], precision=HIGHEST)
    scores = jax.nn.sigmoid(logits)
    biased = scores + bias_ref[...]
    g3 = biased.reshape(N_GROUPS, gsz, tr)
    m1 = jnp.max(g3, axis=1, keepdims=True)
    n_top = jnp.sum(jnp.where(g3 == m1, 1.0, 0.0), axis=1, keepdims=True)
    m2 = jnp.max(jnp.where(g3 < m1, g3, -jnp.inf), axis=1, keepdims=True)
    gs = (m1 + jnp.where(n_top >= 2.0, m1, m2)).reshape(N_GROUPS, tr)
    gi = lax.broadcasted_iota(jnp.int32, (N_GROUPS, tr), 0)
    beaten = jnp.zeros((N_GROUPS, tr), F32)
    for g in range(N_GROUPS):
        other = gs[g:g + 1, :]
        beaten = beaten + jnp.where((other > gs) | ((other == gs) & (g < gi)), 1.0, 0.0)
    gsel = (beaten < float(TOPK_GROUPS)).reshape(N_GROUPS, 1, tr)
    masked = jnp.where(gsel, g3, -jnp.inf).reshape(e, tr)

    ids = lax.broadcasted_iota(jnp.int32, (e, tr), 0)
    chosen = jnp.zeros((e, tr), F32)
    sel_idx, sel_score = [], []
    for _ in range(TOP_K):
        mx = jnp.max(masked, axis=0, keepdims=True)
        ix = jnp.min(jnp.where(masked == mx, ids, e), axis=0, keepdims=True)
        hit = ids == ix
        sel_idx.append(ix)
        sel_score.append(jnp.sum(jnp.where(hit, scores, 0.0), axis=0, keepdims=True))
        chosen = jnp.where(hit, 1.0, chosen)
        masked = jnp.where(hit, -jnp.inf, masked)
    idx = jnp.concatenate(sel_idx, axis=0)
    sc = jnp.concatenate(sel_score, axis=0)
    idx_ref[...] = idx
    wts_ref[...] = sc / jnp.sum(sc, axis=0, keepdims=True) * ROUTED_SCALE

    row = lax.broadcasted_iota(jnp.int32, (tr, tr), 0)
    col = lax.broadcasted_iota(jnp.int32, (tr, tr), 1)
    before = jnp.where(row < col, 1.0, 0.0).astype(BF16)
    prior = jnp.dot(chosen.astype(BF16), before, preferred_element_type=F32) + run_scr[:, 0:1]
    rnk_ref[...] = jnp.concatenate(
        [jnp.sum(jnp.where(ids == sel_idx[k], prior, 0.0), axis=0, keepdims=True)
         for k in range(TOP_K)], axis=0).astype(jnp.int32)
    run_scr[...] = run_scr[...] + jnp.sum(chosen, axis=1, keepdims=True)
    cnt_ref[...] = run_scr[...].astype(jnp.int32)


def _route(hf, wrt, bias_col, tr):
    s, d = hf.shape
    e = wrt.shape[0]
    tok = pl.BlockSpec((TOP_K, tr), lambda i: (0, i))
    return pl.pallas_call(
        _route_kernel,
        grid=(s // tr,),
        in_specs=[pl.BlockSpec((tr, d), lambda i: (i, 0)),
                  pl.BlockSpec((e, d), lambda i: (0, 0)),
                  pl.BlockSpec((e, 1), lambda i: (0, 0))],
        out_specs=[tok, tok, tok, pl.BlockSpec((e, 128), lambda i: (0, 0))],
        out_shape=[jax.ShapeDtypeStruct((TOP_K, s), jnp.int32),
                   jax.ShapeDtypeStruct((TOP_K, s), F32),
                   jax.ShapeDtypeStruct((TOP_K, s), jnp.int32),
                   jax.ShapeDtypeStruct((e, 128), jnp.int32)],
        scratch_shapes=[pltpu.VMEM((e, 128), F32)],
        compiler_params=_cparams(("arbitrary",)),
        name="route",
    )(hf, wrt, bias_col)


def _positions_kernel(idx_ref, rnk_ref, pstart_ref, pos_ref):
    e = pstart_ref.shape[0]
    ts = idx_ref.shape[1]
    ids = lax.broadcasted_iota(jnp.int32, (e, ts), 0)
    idx = idx_ref[...]
    pos_ref[...] = rnk_ref[...] + jnp.concatenate(
        [jnp.sum(jnp.where(ids == idx[k:k + 1, :], pstart_ref[...], 0), axis=0, keepdims=True)
         for k in range(TOP_K)], axis=0)


def _positions(idx, rnk, pstart_col, ts):
    s = idx.shape[1]
    e = pstart_col.shape[0]
    tok = pl.BlockSpec((TOP_K, ts), lambda i: (0, i))
    return pl.pallas_call(
        _positions_kernel,
        grid=(s // ts,),
        in_specs=[tok, tok, pl.BlockSpec((e, 1), lambda i: (0, 0))],
        out_specs=tok,
        out_shape=jax.ShapeDtypeStruct((TOP_K, s), jnp.int32),
        compiler_params=_cparams(("arbitrary",)),
        name="positions",
    )(idx, rnk, pstart_col)


def _swiglu_packed(xp, wg, wu, wd):
    lo, hi = _unpack_halves(xp)
    lo, hi = lo.astype(BF16), hi.astype(BF16)
    n = lo.shape[1]
    g = (jnp.dot(lo, wg[:n], preferred_element_type=F32)
         + jnp.dot(hi, wg[n:], preferred_element_type=F32))
    u = (jnp.dot(lo, wu[:n], preferred_element_type=F32)
         + jnp.dot(hi, wu[n:], preferred_element_type=F32))
    h = (g * jax.nn.sigmoid(g)) * u
    return jnp.dot(h.astype(BF16), wd[...], preferred_element_type=F32)


def _moe_kernel(ie_ref, ib_ref, first_ref, slot_ref, ne_ref, rows_ref, nv_ref, xs_ref, wg_hbm,
                wu_hbm, wd_hbm, ys_ref, wg_f, wu_f, wd_f, wg_b, wu_b, wd_b, sem):
    del ib_ref
    i = pl.program_id(0)

    def fetch(e, slot):
        return [pltpu.make_async_copy(src.at[e], dst.at[slot], sem.at[slot, n])
                for n, (src, dst) in enumerate(((wg_hbm, wg_f), (wu_hbm, wu_f), (wd_hbm, wd_f)))]

    @pl.when(i == 0)
    def _():
        for cp in fetch(ie_ref[0], 0):
            cp.start()

    @pl.when(i < nv_ref[0])
    def _():
        for slot in range(2):
            @pl.when((first_ref[i] == 1) & (slot_ref[i] == slot))
            def _():
                for cp in fetch(ie_ref[i], slot):
                    cp.wait()
                wg_b[...] = wg_f[slot].astype(BF16)
                wu_b[...] = wu_f[slot].astype(BF16)
                wd_b[...] = wd_f[slot].astype(BF16)

                @pl.when(ne_ref[i] >= 0)
                def _():
                    for cp in fetch(ne_ref[i], 1 - slot):
                        cp.start()

        row = lax.broadcasted_iota(jnp.int32, xs_ref.shape, 0)
        xp = jnp.where(row < rows_ref[i], xs_ref[...], jnp.uint32(0))
        ys_ref[...] = _pack_halves(_swiglu_packed(xp, wg_b, wu_b, wd_b))


def _moe(item_e, item_b, item_first, item_slot, item_next, item_rows, n_valid, xs, wg, wu, wd):
    m_pad, dh = xs.shape
    _, d, f = wg.shape
    n_items = item_e.shape[0]
    blk = lambda i, ie, ib, fi, sl, ne, nr, nv: (ib[i], 0)
    hbm = pl.BlockSpec(memory_space=pl.ANY)
    return pl.pallas_call(
        _moe_kernel,
        grid_spec=pltpu.PrefetchScalarGridSpec(
            num_scalar_prefetch=7,
            grid=(n_items,),
            in_specs=[pl.BlockSpec((MOE_ROWS, dh), blk), hbm, hbm, hbm],
            out_specs=pl.BlockSpec((MOE_ROWS, dh), blk),
            scratch_shapes=[pltpu.VMEM((2, d, f), F32), pltpu.VMEM((2, d, f), F32),
                            pltpu.VMEM((2, f, d), F32),
                            pltpu.VMEM((d, f), BF16), pltpu.VMEM((d, f), BF16),
                            pltpu.VMEM((f, d), BF16),
                            pltpu.SemaphoreType.DMA((2, 3))],
        ),
        out_shape=jax.ShapeDtypeStruct((m_pad, dh), jnp.uint32),
        compiler_params=_cparams(("arbitrary",)),
        name="moe",
    )(item_e, item_b, item_first, item_slot, item_next, item_rows, n_valid, xs, wg, wu, wd)


def _sc_gather_rows(table, idx_row):
    m = idx_row.shape[1]
    w = table.shape[1]
    idx_row = idx_row.reshape(m // SC_GATHER_WINDOW, SC_GATHER_WINDOW)
    mesh = plsc.VectorSubcoreMesh(core_axis_name="c", subcore_axis_name="s")

    @functools.partial(pl.kernel, mesh=mesh,
                       out_type=jax.ShapeDtypeStruct((m, w), table.dtype))
    def gather(table_hbm, idx_hbm, out_hbm):
        def body(idx_vmem, out_vmem):
            pltpu.sync_copy(table_hbm.at[idx_vmem.at[0]], out_vmem)

        pltpu.emit_pipeline(
            body,
            grid=(m // SC_GATHER_WINDOW,),
            in_specs=[pl.BlockSpec((1, SC_GATHER_WINDOW), lambda i: (i, 0))],
            out_specs=[pl.BlockSpec((SC_GATHER_WINDOW, w), lambda i: (i, 0))],
            core_axis_name=("c", "s"),
            dimension_semantics=(pltpu.PARALLEL,),
        )(idx_hbm, out_hbm)

    return gather(table, idx_row)


def _sc_scatter_rows(rows, idx_blocks, m_out):
    s, w = rows.shape
    mesh = plsc.VectorSubcoreMesh(core_axis_name="c", subcore_axis_name="s")

    @functools.partial(pl.kernel, mesh=mesh,
                       out_type=jax.ShapeDtypeStruct((m_out, w), rows.dtype))
    def scatter(rows_hbm, idx_hbm, out_hbm):
        def body(rows_vmem, idx_vmem):
            for k in range(TOP_K):
                pltpu.sync_copy(rows_vmem, out_hbm.at[idx_vmem.at[k]])

        pltpu.emit_pipeline(
            body,
            grid=(s // SC_GATHER_WINDOW,),
            in_specs=[pl.BlockSpec((SC_GATHER_WINDOW, w), lambda i: (i, 0)),
                      pl.BlockSpec((TOP_K, SC_GATHER_WINDOW), lambda i: (i, 0))],
            out_specs=[],
            core_axis_name=("c", "s"),
            dimension_semantics=(pltpu.PARALLEL,),
        )(rows_hbm, idx_hbm)

    return scatter(rows, idx_blocks)


def _combine_kernel(wt_ref, hf_ref, x1_ref, gt_ref, sg_ref, su_ref, sd_ref, g_ref, o_ref):
    tc = x1_ref.shape[0]
    y = _swiglu_packed(hf_ref[...], sg_ref, su_ref, sd_ref)
    wt = wt_ref[...]
    n = g_ref.shape[2]
    r_lo = jnp.zeros((tc, n), F32)
    r_hi = jnp.zeros((tc, n), F32)
    for k in range(TOP_K):
        lo, hi = _unpack_halves(g_ref[k])
        r_lo = r_lo + lo * wt[:, k:k + 1]
        r_hi = r_hi + hi * wt[:, k:k + 1]
    y = y + jnp.concatenate([r_lo, r_hi], axis=1)
    o_ref[...] = x1_ref[...] + gt_ref[...] * y


def _combine(wts_t, hfp, x1, gt, sg, su, sd, gathered, tc):
    s, d = x1.shape
    f = sg.shape[1]
    row = pl.BlockSpec((tc, d), lambda i: (i, 0))
    return pl.pallas_call(
        _combine_kernel,
        grid=(s // tc,),
        in_specs=[pl.BlockSpec((tc, TOP_K), lambda i: (i, 0)),
                  pl.BlockSpec((tc, d // 2), lambda i: (i, 0)), row,
                  pl.BlockSpec((1, d), lambda i: (0, 0)),
                  pl.BlockSpec((d, f), lambda i: (0, 0)),
                  pl.BlockSpec((d, f), lambda i: (0, 0)),
                  pl.BlockSpec((f, d), lambda i: (0, 0)),
                  pl.BlockSpec((TOP_K, tc, d // 2), lambda i: (0, i, 0))],
        out_specs=row,
        out_shape=jax.ShapeDtypeStruct((s, d), F32),
        compiler_params=_cparams(("arbitrary",)),
        name="combine",
    )(wts_t, hfp, x1, gt, sg, su, sd, gathered)


def _tile(n, want):
    t = min(n, want)
    assert n % t == 0, (n, t)
    return t


def _layer(l, x, c_col, pos_row, p):
    s, d = x.shape
    lambda_init = 0.8 - 0.6 * math.exp(-0.3 * l)
    gqk, gv = GLA_HEADS * GLA_DK, GLA_HEADS * GLA_DV
    dqk, dvw = DIFF_HEADS * 2 * DIFF_DH, DIFF_HEADS * DIFF_DV
    lowrank = p["gla_w_a2"].shape[0]

    mod = _ada(c_col, p["w_ada"], p["b_ada"][None, :])
    sh_a, sc_a, gt_a, sh_f, sc_f, gt_f = [mod[:, j * d:(j + 1) * d] for j in range(6)]

    w_in = p["w_in"]
    o = 0
    cols = {}
    for name, wdt in (("gq", gqk), ("gk", gqk), ("gv", gv), ("ga", lowrank), ("gg", gv),
                      ("dq", dqk), ("dk", dqk), ("dv", dvw), ("mg", d), ("md", d)):
        cols[name] = w_in[:, o:o + wdt]
        o += wdt
    row_names = ("gv", "gg", "mg", "md", "gq")
    w_row = jnp.concatenate([cols[n] for n in row_names], axis=1).astype(BF16)
    col_of, o = {}, 0
    for n in row_names:
        col_of[n] = o
        o += cols[n].shape[1]
    t_names = ("dq", "dk", "dv", "gk")
    w_t = jnp.concatenate([cols[n] for n in t_names], axis=1).T.astype(BF16)
    row_of, o = {}, 0
    for n in t_names:
        row_of[n] = o
        o += cols[n].shape[1]
    w_ga = jnp.pad(cols["ga"], ((0, 0), (0, 128 - lowrank))).astype(BF16)

    g1 = p["norm1_g"][None, :]
    tm = _tile(s, 1024)
    proj, ga = _inproj(x, g1, sc_a, sh_a, w_row, w_ga, tm, 512)
    projt = _inproj_t(x, g1, sc_a, sh_a, w_t, tm, w_t.shape[0] // 2)

    wa2t = jnp.pad(p["gla_w_a2"].T, ((0, 0), (0, 128 - lowrank)))
    o_gla = _gla(proj, projt, ga, wa2t, p["gla_b_a"][:, None], p["gla_onorm_g"][None, :],
                 _tile(s, 512), col_of["gq"], col_of["gv"], col_of["gg"], row_of["gk"])

    invf = ROPE_THETA ** (-jnp.arange(0, ROT_DIM, 2, dtype=F32) / ROT_DIM)
    qat, qbt, kr, vte, qn, kn = _qkprep(projt, pos_row, invf[:, None], p["diff_qnorm_g"][:, None],
                                p["diff_knorm_g"][:, None], _tile(s, 512), row_of["dq"],
                                row_of["dk"], row_of["dv"])
    tq = _tile(s, 512)
    o_diff = _diffattn(qat, qbt, kr, vte, qn, kn, p["diff_lq1"][None, :], p["diff_lk1"][None, :],
                       p["diff_lq2"][None, :], p["diff_lk2"][None, :],
                       p["diff_subln_g"][:, None], lambda_init, tq, tq)

    x1, hf, hfp = _mergeout(o_gla, o_diff, proj, x, p["w_branch_gla"].astype(BF16),
                       p["w_branch_diff"].astype(BF16), p["w_out"].astype(BF16), gt_a,
                       p["norm2_g"][None, :], sc_f, sh_f, _tile(s, 512),
                       col_of["mg"], col_of["md"])

    e = p["w_router"].shape[1]
    idx, wts, rnk, cnt = _route(hf, p["w_router"].T, p["router_bias"][:, None], _tile(s, 512))

    counts = cnt[:, 0]
    pcounts = ((counts + MOE_ROWS - 1) // MOE_ROWS) * MOE_ROWS
    pend = jnp.cumsum(pcounts)
    pstart = pend - pcounts
    pos = _positions(idx, rnk, pstart[:, None], _tile(s, 512))
    n_items = (s * TOP_K) // MOE_ROWS + e
    n_valid = (pend[-1] // MOE_ROWS).astype(jnp.int32)
    item_b = jnp.minimum(jnp.arange(n_items, dtype=jnp.int32), n_valid - 1)
    item_e = jnp.minimum(jnp.sum(pend[None, :] <= (item_b * MOE_ROWS)[:, None], axis=1),
                         e - 1).astype(jnp.int32)

    wn = SC_GATHER_WINDOW
    pos_w = pos.reshape(TOP_K, s // wn, wn).transpose(1, 0, 2).reshape(s // wn * TOP_K, wn)
    xs = _sc_scatter_rows(hfp, pos_w, n_items * MOE_ROWS)
    item_rows = jnp.clip(pstart[item_e] + counts[item_e] - item_b * MOE_ROWS, 0,
                         MOE_ROWS).astype(jnp.int32)
    prev_e = jnp.concatenate([jnp.full((1,), -1, jnp.int32), item_e[:-1]])
    item_first = ((jnp.arange(n_items) < n_valid) & (item_e != prev_e)).astype(jnp.int32)
    item_slot = ((jnp.cumsum(item_first) - 1) % 2).astype(jnp.int32)
    cand = jnp.where(pcounts > 0, jnp.arange(e, dtype=jnp.int32), e)
    following = jnp.concatenate([lax.cummin(cand[::-1])[::-1][1:], jnp.full((1,), e, jnp.int32)])
    item_next = jnp.where(following[item_e] < e, following[item_e], -1).astype(jnp.int32)
    ys = _moe(item_e, item_b, item_first, item_slot, item_next, item_rows, n_valid[None], xs,
              p["w_exp_gate"], p["w_exp_up"], p["w_exp_down"])
    gathered = _sc_gather_rows(ys, pos.reshape(1, TOP_K * s)).reshape(TOP_K, s, d // 2)
    return _combine(wts.T, hfp, x1, gt_f, p["w_sh_gate"].astype(BF16),
                    p["w_sh_up"].astype(BF16), p["w_sh_down"].astype(BF16), gathered,
                    _tile(s, 256))


_LAYER_PARAMS = ("w_ada", "b_ada", "norm1_g", "w_in", "gla_w_a2", "gla_b_a", "gla_onorm_g",
                 "diff_qnorm_g", "diff_knorm_g", "diff_lq1", "diff_lk1", "diff_lq2", "diff_lk2",
                 "diff_subln_g", "w_branch_gla", "w_branch_diff", "w_out", "norm2_g", "w_router",
                 "router_bias", "w_exp_gate", "w_exp_up", "w_exp_down", "w_sh_gate", "w_sh_up",
                 "w_sh_down")


def kernel(x, c, positions, w_ada, b_ada, norm1_g, w_in, gla_w_a2, gla_b_a, gla_onorm_g, diff_qnorm_g, diff_knorm_g, diff_lq1, diff_lk1, diff_lq2, diff_lk2, diff_subln_g, w_branch_gla, w_branch_diff, w_out, norm2_g, w_router, router_bias, w_exp_gate, w_exp_up, w_exp_down, w_sh_gate, w_sh_up, w_sh_down):
    stacked = dict(zip(_LAYER_PARAMS, (
        w_ada, b_ada, norm1_g, w_in, gla_w_a2, gla_b_a, gla_onorm_g, diff_qnorm_g, diff_knorm_g,
        diff_lq1, diff_lk1, diff_lq2, diff_lk2, diff_subln_g, w_branch_gla, w_branch_diff, w_out,
        norm2_g, w_router, router_bias, w_exp_gate, w_exp_up, w_exp_down, w_sh_gate, w_sh_up,
        w_sh_down)))
    b, s, d = x.shape
    assert b == 1, "single-sequence kernel"
    xl = x[0]
    c_col = c[0][:, None]
    pos_row = positions.astype(jnp.int32)
    for l in range(w_ada.shape[0]):
        xl = _layer(l, xl, c_col, pos_row, {k: v[l] for k, v in stacked.items()})
    return xl[None]
```

```python
import functools
import math

import jax
import jax.numpy as jnp
from jax import lax
from jax.experimental import pallas as pl
from jax.experimental.pallas import tpu as pltpu
from jax.experimental.pallas import tpu_sc as plsc

CHUNK = 64
EPS = 1e-6
GLA_HEADS = 4
GLA_DK = 128
GLA_DV = 256
GLA_TAU = 16.0
DIFF_HEADS = 8
DIFF_DH = 64
DIFF_DV = 2 * DIFF_DH
ROPE_THETA = 500000.0
ROT_DIM = DIFF_DH // 4
N_GROUPS = 8
TOPK_GROUPS = 4
TOP_K = 8
ROUTED_SCALE = 2.5

MOE_ROWS = 512
SC_GATHER_WINDOW = 64
VMEM_LIMIT = 56 * 1024 * 1024
NEG_BIG = -1e30
LOG2E = 1.4426950408889634
HIGHEST = lax.Precision.HIGHEST
F32 = jnp.float32
BF16 = jnp.bfloat16


def _cparams(sem):
    return pltpu.CompilerParams(dimension_semantics=sem, vmem_limit_bytes=VMEM_LIMIT)


def _nt_dot(a, b, precision=None):
    return lax.dot_general(a, b, (((1,), (1,)), ((), ())), precision=precision,
                           preferred_element_type=F32)


def _pack_halves(x):
    n = x.shape[1] // 2
    lo = pltpu.bitcast(x[:, :n].astype(BF16).astype(F32), jnp.uint32) >> 16
    hi = pltpu.bitcast(x[:, n:].astype(BF16).astype(F32), jnp.uint32) & jnp.uint32(0xFFFF0000)
    return lo | hi


def _unpack_halves(w):
    return (pltpu.bitcast(w << 16, F32), pltpu.bitcast(w & jnp.uint32(0xFFFF0000), F32))


def _rms_mod(x, g, sc, sh):
    xn = x * lax.rsqrt(jnp.mean(x * x, axis=-1, keepdims=True) + EPS)
    return (xn * g) * (1.0 + sc) + sh


def _ada_kernel(c_ref, w_ref, b_ref, o_ref):
    c = c_ref[...]
    ca = c * jax.nn.sigmoid(c)
    o_ref[...] = jnp.sum(ca * w_ref[...], axis=0, keepdims=True) + b_ref[...]


def _ada(c_col, w, b):
    d, n = w.shape
    tn = min(1024, n)
    return pl.pallas_call(
        _ada_kernel,
        grid=(n // tn,),
        in_specs=[pl.BlockSpec((d, 1), lambda j: (0, 0)),
                  pl.BlockSpec((d, tn), lambda j: (0, j)),
                  pl.BlockSpec((1, tn), lambda j: (0, j))],
        out_specs=pl.BlockSpec((1, tn), lambda j: (0, j)),
        out_shape=jax.ShapeDtypeStruct((1, n), F32),
        compiler_params=_cparams(("arbitrary",)),
        name="ada",
    )(c_col, w, b)


def _inproj_kernel(x_ref, g_ref, sc_ref, sh_ref, w_ref, wga_ref, o_ref, ga_ref, h_scr):
    @pl.when(pl.program_id(1) == 0)
    def _():
        h = _rms_mod(x_ref[...], g_ref[...], sc_ref[...], sh_ref[...]).astype(BF16)
        h_scr[...] = h
        ga_ref[...] = jnp.dot(h, wga_ref[...], preferred_element_type=F32)

    o_ref[...] = jnp.dot(h_scr[...], w_ref[...], preferred_element_type=F32).astype(BF16)


def _inproj(x, g, sc, sh, w, wga, tm, tn):
    s, d = x.shape
    n = w.shape[1]
    vec = pl.BlockSpec((1, d), lambda i, j: (0, 0))
    return pl.pallas_call(
        _inproj_kernel,
        grid=(s // tm, n // tn),
        in_specs=[pl.BlockSpec((tm, d), lambda i, j: (i, 0)), vec, vec, vec,
                  pl.BlockSpec((d, tn), lambda i, j: (0, j)),
                  pl.BlockSpec((d, 128), lambda i, j: (0, 0))],
        out_specs=[pl.BlockSpec((tm, tn), lambda i, j: (i, j)),
                   pl.BlockSpec((tm, 128), lambda i, j: (i, 0))],
        out_shape=[jax.ShapeDtypeStruct((s, n), BF16), jax.ShapeDtypeStruct((s, 128), F32)],
        scratch_shapes=[pltpu.VMEM((tm, d), BF16)],
        compiler_params=_cparams(("arbitrary", "arbitrary")),
        name="inproj",
    )(x, g, sc, sh, w, wga)


def _inproj_t_kernel(x_ref, g_ref, sc_ref, sh_ref, wt_ref, o_ref, h_scr):
    @pl.when(pl.program_id(1) == 0)
    def _():
        h_scr[...] = _rms_mod(x_ref[...], g_ref[...], sc_ref[...], sh_ref[...]).astype(BF16)

    o_ref[...] = _nt_dot(wt_ref[...], h_scr[...]).astype(BF16)


def _inproj_t(x, g, sc, sh, wt, tm, tn):
    s, d = x.shape
    n = wt.shape[0]
    vec = pl.BlockSpec((1, d), lambda i, j: (0, 0))
    return pl.pallas_call(
        _inproj_t_kernel,
        grid=(s // tm, n // tn),
        in_specs=[pl.BlockSpec((tm, d), lambda i, j: (i, 0)), vec, vec, vec,
                  pl.BlockSpec((tn, d), lambda i, j: (j, 0))],
        out_specs=pl.BlockSpec((tn, tm), lambda i, j: (j, i)),
        out_shape=jax.ShapeDtypeStruct((n, s), BF16),
        scratch_shapes=[pltpu.VMEM((tm, d), BF16)],
        compiler_params=_cparams(("arbitrary", "arbitrary")),
        name="inproj_t",
    )(x, g, sc, sh, wt)


def _gla_kernel(q_ref, kt_ref, v_ref, gg_ref, ga_ref, wa2t_ref, ba_ref, on_ref, o_ref,
                state_ref, o_scr):
    tt = q_ref.shape[0]
    nchunk = tt // CHUNK

    @pl.when(pl.program_id(0) == 0)
    def _():
        state_ref[...] = jnp.zeros_like(state_ref)

    zt = _nt_dot(wa2t_ref[...], ga_ref[...], precision=HIGHEST) + ba_ref[...]
    lat = (jnp.minimum(zt, 0.0) - jnp.log1p(jnp.exp(-jnp.abs(zt)))) * (1.0 / GLA_TAU)
    row = lax.broadcasted_iota(jnp.int32, (tt, tt), 0)
    col = lax.broadcasted_iota(jnp.int32, (tt, tt), 1)
    same = (row // CHUNK) == (col // CHUNK)
    incl = jnp.where(same & (row <= col), 1.0, 0.0).astype(BF16)
    full = jnp.where(same, 1.0, 0.0).astype(BF16)
    lat_hi = lat.astype(BF16)
    lat_lo = (lat - lat_hi.astype(F32)).astype(BF16)
    cumt = (jnp.dot(lat_hi, incl, preferred_element_type=F32)
            + jnp.dot(lat_lo, incl, preferred_element_type=F32))
    tott = (jnp.dot(lat_hi, full, preferred_element_type=F32)
            + jnp.dot(lat_lo, full, preferred_element_type=F32))
    kdt = kt_ref[...].astype(F32) * jnp.exp(tott - cumt)
    dec = jnp.exp(tott)

    lane = lax.broadcasted_iota(jnp.int32, (GLA_DK, 2 * CHUNK), 1)
    upd = {}
    for c in range(nchunk):
        pair = (c // 2) * 2 * CHUNK
        if nchunk > 1:
            keep = (lane // CHUNK) == (c % 2)
        for h in range(GLA_HEADS):
            rows = slice(h * GLA_DK, (h + 1) * GLA_DK)
            vcols = slice(h * GLA_DV, (h + 1) * GLA_DV)
            if nchunk > 1:
                a = jnp.where(keep, kdt[rows, pair:pair + 2 * CHUNK], 0.0).astype(BF16)
                vp = v_ref[pair:pair + 2 * CHUNK, vcols]
            else:
                a = kdt[rows, :].astype(BF16)
                vp = v_ref[:, vcols]
            upd[c, h] = jnp.dot(a, vp, preferred_element_type=F32)

    for h in range(GLA_HEADS):
        rows = slice(h * GLA_DK, (h + 1) * GLA_DK)
        vcols = slice(h * GLA_DV, (h + 1) * GLA_DV)
        st = state_ref[h]
        states = []
        for c in range(nchunk):
            st = st * dec[rows, c * CHUNK:c * CHUNK + 1] + upd[c, h]
            states.append(st.astype(BF16))
        state_ref[h] = st
        for c in range(nchunk):
            o_scr[c * CHUNK:(c + 1) * CHUNK, vcols] = jnp.dot(
                q_ref[c * CHUNK:(c + 1) * CHUNK, rows], states[c], preferred_element_type=F32)

    for h in range(GLA_HEADS):
        vcols = slice(h * GLA_DV, (h + 1) * GLA_DV)
        o = o_scr[:, vcols] * (GLA_DK ** -0.5)
        o = o * lax.rsqrt(jnp.mean(o * o, axis=-1, keepdims=True) + EPS) * on_ref[...]
        g = gg_ref[:, vcols].astype(F32)
        o_ref[:, vcols] = (o * (g * jax.nn.sigmoid(g))).astype(BF16)


def _gla(proj, projt, ga, wa2t, ba_col, on_g, tt, col_q, col_v, col_g, row_k):
    s = proj.shape[0]
    qk = GLA_HEADS * GLA_DK
    vw = GLA_HEADS * GLA_DV
    return pl.pallas_call(
        _gla_kernel,
        grid=(s // tt,),
        in_specs=[pl.BlockSpec((tt, qk), lambda i: (i, col_q // qk)),
                  pl.BlockSpec((qk, tt), lambda i: (row_k // qk, i)),
                  pl.BlockSpec((tt, vw), lambda i: (i, col_v // vw)),
                  pl.BlockSpec((tt, vw), lambda i: (i, col_g // vw)),
                  pl.BlockSpec((tt, 128), lambda i: (i, 0)),
                  pl.BlockSpec((qk, 128), lambda i: (0, 0)),
                  pl.BlockSpec((qk, 1), lambda i: (0, 0)),
                  pl.BlockSpec((1, GLA_DV), lambda i: (0, 0))],
        out_specs=pl.BlockSpec((tt, vw), lambda i: (i, 0)),
        out_shape=jax.ShapeDtypeStruct((s, vw), BF16),
        scratch_shapes=[pltpu.VMEM((GLA_HEADS, GLA_DK, GLA_DV), F32),
                        pltpu.VMEM((tt, vw), F32)],
        compiler_params=_cparams(("arbitrary",)),
        name="gla",
    )(proj, projt, proj, proj, ga, wa2t, ba_col, on_g)


def _qknorm_rope_t(xt, g_col, cos, sin):
    n, tm = xt.shape
    x3 = xt.reshape(n // DIFF_DH, DIFF_DH, tm)
    r = lax.rsqrt(jnp.mean(x3 * x3, axis=1, keepdims=True) + EPS)
    y = x3 * r * g_col[None]
    half = ROT_DIM // 2
    y1, y2, rest = y[:, :half], y[:, half:ROT_DIM], y[:, ROT_DIM:]
    o1 = y1 * cos[None] - y2 * sin[None]
    o2 = y2 * cos[None] + y1 * sin[None]
    return jnp.concatenate([o1, o2, rest], axis=1)


def _seg_norms(x3):
    return jnp.sqrt(jnp.sum(x3 * x3, axis=1)).reshape(DIFF_HEADS, 2, x3.shape[2])


def _qkprep_kernel(qt_ref, kt_ref, vt_ref, pos_ref, invf_ref, qg_ref, kg_ref, qa_ref, qb_ref,
                   ko_ref, ve_ref, qn_ref, kn_ref):
    tm = qt_ref.shape[1]
    v3 = vt_ref[...].reshape(DIFF_HEADS, DIFF_DV, tm)
    ones = jnp.ones((DIFF_HEADS, ATT_SUM_ROWS, tm), BF16)
    ve_ref[...] = jnp.concatenate([v3, ones], axis=1).reshape(-1, tm)
    ang = pos_ref[...].astype(F32) * invf_ref[...]
    cos, sin = jnp.cos(ang), jnp.sin(ang)
    k3 = _qknorm_rope_t(kt_ref[...].astype(F32), kg_ref[...], cos, sin)
    ko_ref[...] = k3.reshape(-1, tm).T.astype(BF16)
    kn_ref[...] = _seg_norms(k3)
    q3 = _qknorm_rope_t(qt_ref[...].astype(F32), qg_ref[...], cos, sin) * (
        DIFF_DH ** -0.5 * LOG2E)
    qn_ref[...] = _seg_norms(q3)
    seg = lax.broadcasted_iota(jnp.int32, q3.shape, 0)
    qa_ref[...] = jnp.where(seg % 2 == 0, q3, 0.0).reshape(-1, tm).astype(BF16)
    qb_ref[...] = jnp.where(seg % 2 == 1, q3, 0.0).reshape(-1, tm).astype(BF16)


def _qkprep(projt, pos_row, invf_col, qg_col, kg_col, tm, row_q, row_k, row_v):
    s = projt.shape[1]
    n = DIFF_HEADS * 2 * DIFF_DH
    ne = DIFF_HEADS * (DIFF_DV + ATT_SUM_ROWS)
    col = pl.BlockSpec((DIFF_DH, 1), lambda i: (0, 0))
    return pl.pallas_call(
        _qkprep_kernel,
        grid=(s // tm,),
        in_specs=[pl.BlockSpec((n, tm), lambda i: (row_q // n, i)),
                  pl.BlockSpec((n, tm), lambda i: (row_k // n, i)),
                  pl.BlockSpec((n, tm), lambda i: (row_v // n, i)),
                  pl.BlockSpec((1, tm), lambda i: (0, i)),
                  pl.BlockSpec((ROT_DIM // 2, 1), lambda i: (0, 0)), col, col],
        out_specs=[pl.BlockSpec((n, tm), lambda i: (0, i)),
                   pl.BlockSpec((n, tm), lambda i: (0, i)),
                   pl.BlockSpec((tm, n), lambda i: (i, 0)),
                   pl.BlockSpec((ne, tm), lambda i: (0, i)),
                   pl.BlockSpec((DIFF_HEADS, 2, tm), lambda i: (0, 0, i)),
                   pl.BlockSpec((DIFF_HEADS, 2, tm), lambda i: (0, 0, i))],
        out_shape=[jax.ShapeDtypeStruct((n, s), BF16), jax.ShapeDtypeStruct((n, s), BF16),
                   jax.ShapeDtypeStruct((s, n), BF16), jax.ShapeDtypeStruct((ne, s), BF16),
                   jax.ShapeDtypeStruct((DIFF_HEADS, 2, s), F32),
                   jax.ShapeDtypeStruct((DIFF_HEADS, 2, s), F32)],
        compiler_params=_cparams(("arbitrary",)),
        name="qkprep",
    )(projt, projt, projt, pos_row, invf_col, qg_col, kg_col)


ATT_COLS = 256
ATT_LOOKAHEAD = {True: 2, False: 4}
ATT_SUM_ROWS = 16
ATT_BOUND_SLACK = 1.02
ATT_BOUND_LIMIT = 50.0


def _diffattn_kernel(qa_ref, qb_ref, k_ref, vt_ref, qn_ref, kn_ref, lq1_ref, lk1_ref, lq2_ref,
                     lk2_ref, sg_ref, o_ref, *scr, lambda_init, tk, cols):
    tq = qa_ref.shape[1]
    nblk = 2 * tq // cols
    m_scr, acc_scr = (scr[b * nblk:(b + 1) * nblk] for b in range(2))
    kmax_scr = scr[2 * nblk]
    s_scr = scr[2 * nblk + 1:]
    i = pl.program_id(1)

    @pl.when(i == 0)
    def _():
        kmax_scr[...] = jnp.max(kn_ref[...], axis=1, keepdims=True)

    bound = qn_ref[...] * kmax_scr[...] * ATT_BOUND_SLACK
    bound = jnp.concatenate([bound[0:1], bound[1:2]], axis=1)
    bounded = jnp.max(bound) < ATT_BOUND_LIMIT
    for c in range(nblk):
        m_scr[c][...] = jnp.where(bounded, bound[:, c * cols:(c + 1) * cols], NEG_BIG)
        acc_scr[c][...] = jnp.zeros_like(acc_scr[c])

    def scores(j, c):
        start = pl.multiple_of(j * tk, tk)
        q_ref = qa_ref if c * cols < tq else qb_ref
        off = (c * cols) % tq
        return jnp.dot(k_ref[pl.ds(start, tk), :], q_ref[:, off:off + cols],
                       preferred_element_type=F32)

    def steps(tiles, masked, next_tile, fixed):
        look = ATT_LOOKAHEAD[fixed]
        items = [(j, c) for j in tiles for c in range(nblk)]
        pending = []
        for n, (j, c) in enumerate(items):
            s = s_scr[n][...] if n < look else pending.pop(0)
            ahead = n + look
            if ahead < len(items):
                pending.append(scores(*items[ahead]))
            elif next_tile is not None:
                s_scr[ahead - len(items)][...] = scores(next_tile, ahead - len(items))
            start = pl.multiple_of(j * tk, tk)
            keys = tk
            if masked:
                keys = (c * cols) % tq + cols
                s = s[:keys]
                krow = lax.broadcasted_iota(jnp.int32, (keys, cols), 0)
                qcol = lax.broadcasted_iota(jnp.int32, (keys, cols), 1)
                qpos = i * tq + (c * cols) % tq + qcol
                s = jnp.where((start + krow) // CHUNK <= qpos // CHUNK, s, NEG_BIG)
            vj = vt_ref[:, pl.ds(start, keys)]
            if fixed:
                p = jnp.exp2(s - m_scr[c][...]).astype(BF16)
                acc_scr[c][...] += jnp.dot(vj, p, preferred_element_type=F32)
            else:
                m_prev = m_scr[c][...]
                m_new = jnp.maximum(m_prev, jnp.max(s, axis=0, keepdims=True))
                alpha = jnp.exp2(m_prev - m_new)
                p = jnp.exp2((s - m_new).astype(BF16))
                acc_scr[c][...] = alpha * acc_scr[c][...] + jnp.dot(
                    vj, p, preferred_element_type=F32)
                m_scr[c][...] = m_new

    def attend(fixed):
        for c in range(ATT_LOOKAHEAD[fixed]):
            s_scr[c][...] = scores(0, c)
        quads = i // 4
        lax.fori_loop(0, quads, lambda t, c: (steps(
            (4 * t, 4 * t + 1, 4 * t + 2, 4 * t + 3), False, 4 * t + 4, fixed), c)[1], 0)

        @pl.when(i % 4 >= 2)
        def _():
            steps((4 * quads, 4 * quads + 1), False, 4 * quads + 2, fixed)

        @pl.when(i % 2 == 1)
        def _():
            steps((i - 1,), False, i, fixed)

        steps((i,), True, None, fixed)

    @pl.when(bounded)
    def _():
        attend(True)

    @pl.when(jnp.logical_not(bounded))
    def _():
        attend(False)

    o = jnp.concatenate([acc_scr[c][:DIFF_DV] / acc_scr[c][DIFF_DV:DIFF_DV + 1]
                         for c in range(nblk)], axis=1)
    lam = (jnp.exp(jnp.sum(lq1_ref[...] * lk1_ref[...]))
           - jnp.exp(jnp.sum(lq2_ref[...] * lk2_ref[...])) + lambda_init)
    o = o[:, :tq] - lam * o[:, tq:]
    o = o * lax.rsqrt(jnp.mean(o * o, axis=0, keepdims=True) + EPS) * sg_ref[...]
    o_ref[...] = (o * (1.0 - lambda_init)).T.astype(BF16)


def _diffattn(qat, qbt, kr, vte, qn, kn, lq1, lk1, lq2, lk2, sg_col, lambda_init, tq, tk):
    s = kr.shape[0]
    hd = 2 * DIFF_DH
    vec = pl.BlockSpec((1, DIFF_DH), lambda h, i: (0, 0))
    cols = min(ATT_COLS, tq)
    nblk = 2 * tq // cols
    kern = functools.partial(_diffattn_kernel, lambda_init=lambda_init, tk=tk, cols=cols)
    dve = DIFF_DV + ATT_SUM_ROWS
    assert tq == tk and nblk >= max(ATT_LOOKAHEAD.values())
    scratch = ([pltpu.VMEM((1, cols), F32)] * nblk + [pltpu.VMEM((dve, cols), F32)] * nblk
               + [pltpu.VMEM((2, 1), F32)]
               + [pltpu.VMEM((tk, cols), F32)] * max(ATT_LOOKAHEAD.values()))
    return pl.pallas_call(
        kern,
        grid=(DIFF_HEADS, s // tq),
        in_specs=[pl.BlockSpec((hd, tq), lambda h, i: (h, i)),
                  pl.BlockSpec((hd, tq), lambda h, i: (h, i)),
                  pl.BlockSpec((s, hd), lambda h, i: (0, h)),
                  pl.BlockSpec((dve, s), lambda h, i: (h, 0)),
                  pl.BlockSpec((None, 2, tq), lambda h, i: (h, 0, i)),
                  pl.BlockSpec((None, 2, s), lambda h, i: (h, 0, 0)),
                  vec, vec, vec, vec,
                  pl.BlockSpec((DIFF_DV, 1), lambda h, i: (0, 0))],
        out_specs=pl.BlockSpec((tq, DIFF_DV), lambda h, i: (i, h)),
        out_shape=jax.ShapeDtypeStruct((s, DIFF_HEADS * DIFF_DV), BF16),
        scratch_shapes=scratch,
        compiler_params=_cparams(("arbitrary", "arbitrary")),
        name="diffattn",
    )(qat, qbt, kr, vte, qn, kn, lq1, lk1, lq2, lk2, sg_col)


def _mergeout_kernel(og_ref, od_ref, mg_ref, md_ref, x_ref, wbg_ref, wbd_ref, wo_ref, gt_ref,
                     g2_ref, sc_ref, sh_ref, x1_ref, hf_ref, hfp_ref):
    bg = jnp.dot(og_ref[...], wbg_ref[...], preferred_element_type=F32)
    bd = jnp.dot(od_ref[...], wbd_ref[...], preferred_element_type=F32)
    merged = (jax.nn.sigmoid(mg_ref[...].astype(F32)) * bg
              + jax.nn.sigmoid(md_ref[...].astype(F32)) * bd)
    x1 = x_ref[...] + gt_ref[...] * jnp.dot(merged.astype(BF16), wo_ref[...],
                                             preferred_element_type=F32)
    x1_ref[...] = x1
    hf = _rms_mod(x1, g2_ref[...], sc_ref[...], sh_ref[...])
    hf_ref[...] = hf
    hfp_ref[...] = _pack_halves(hf)


def _mergeout(og, od, proj, x, wbg, wbd, wo, gt, g2, sc, sh, tm, col_mg, col_md):
    s, d = x.shape
    vec = pl.BlockSpec((1, d), lambda i: (0, 0))
    wspec = pl.BlockSpec((d, d), lambda i: (0, 0))
    row = pl.BlockSpec((tm, d), lambda i: (i, 0))
    return pl.pallas_call(
        _mergeout_kernel,
        grid=(s // tm,),
        in_specs=[row, row,
                  pl.BlockSpec((tm, d), lambda i: (i, col_mg // d)),
                  pl.BlockSpec((tm, d), lambda i: (i, col_md // d)),
                  row, wspec, wspec, wspec, vec, vec, vec, vec],
        out_specs=[row, row, pl.BlockSpec((tm, d // 2), lambda i: (i, 0))],
        out_shape=[jax.ShapeDtypeStruct((s, d), F32), jax.ShapeDtypeStruct((s, d), F32),
                   jax.ShapeDtypeStruct((s, d // 2), jnp.uint32)],
        compiler_params=_cparams(("arbitrary",)),
        name="mergeout",
    )(og, od, proj, proj, x, wbg, wbd, wo, gt, g2, sc, sh)


def _route_kernel(hf_ref, wrt_ref, bias_ref, idx_ref, wts_ref, rnk_ref, cnt_ref, run_scr):
    tr = hf_ref.shape[0]
    e = wrt_ref.shape[0]
    gsz = e // N_GROUPS

    @pl.when(pl.program_id(0) == 0)
    def _():
        run_scr[...] = jnp.zeros_like(run_scr)

    logits = _nt_dot(wrt_ref[...], hf_ref[...], precision=HIGHEST)
    scores = jax.nn.sigmoid(logits)
    biased = scores + bias_ref[...]
    g3 = biased.reshape(N_GROUPS, gsz, tr)
    m1 = jnp.max(g3, axis=1, keepdims=True)
    n_top = jnp.sum(jnp.where(g3 == m1, 1.0, 0.0), axis=1, keepdims=True)
    m2 = jnp.max(jnp.where(g3 < m1, g3, -jnp.inf), axis=1, keepdims=True)
    gs = (m1 + jnp.where(n_top >= 2.0, m1, m2)).reshape(N_GROUPS, tr)
    gi = lax.broadcasted_iota(jnp.int32, (N_GROUPS, tr), 0)
    beaten = jnp.zeros((N_GROUPS, tr), F32)
    for g in range(N_GROUPS):
        other = gs[g:g + 1, :]
        beaten = beaten + jnp.where((other > gs) | ((other == gs) & (g < gi)), 1.0, 0.0)
    gsel = (beaten < float(TOPK_GROUPS)).reshape(N_GROUPS, 1, tr)
    masked = jnp.where(gsel, g3, -jnp.inf).reshape(e, tr)

    ids = lax.broadcasted_iota(jnp.int32, (e, tr), 0)
    chosen = jnp.zeros((e, tr), F32)
    sel_idx, sel_score = [], []
    for _ in range(TOP_K):
        mx = jnp.max(masked, axis=0, keepdims=True)
        ix = jnp.min(jnp.where(masked == mx, ids, e), axis=0, keepdims=True)
        hit = ids == ix
        sel_idx.append(ix)
        sel_score.append(jnp.sum(jnp.where(hit, scores, 0.0), axis=0, keepdims=True))
        chosen = jnp.where(hit, 1.0, chosen)
        masked = jnp.where(hit, -jnp.inf, masked)
    idx = jnp.concatenate(sel_idx, axis=0)
    sc = jnp.concatenate(sel_score, axis=0)
    idx_ref[...] = idx
    wts_ref[...] = sc / jnp.sum(sc, axis=0, keepdims=True) * ROUTED_SCALE

    row = lax.broadcasted_iota(jnp.int32, (tr, tr), 0)
    col = lax.broadcasted_iota(jnp.int32, (tr, tr), 1)
    before = jnp.where(row < col, 1.0, 0.0).astype(BF16)
    prior = jnp.dot(chosen.astype(BF16), before, preferred_element_type=F32) + run_scr[:, 0:1]
    rnk_ref[...] = jnp.concatenate(
        [jnp.sum(jnp.where(ids == sel_idx[k], prior, 0.0), axis=0, keepdims=True)
         for k in range(TOP_K)], axis=0).astype(jnp.int32)
    run_scr[...] = run_scr[...] + jnp.sum(chosen, axis=1, keepdims=True)
    cnt_ref[...] = run_scr[...].astype(jnp.int32)


def _route(hf, wrt, bias_col, tr):
    s, d = hf.shape
    e = wrt.shape[0]
    tok = pl.BlockSpec((TOP_K, tr), lambda i: (0, i))
    return pl.pallas_call(
        _route_kernel,
        grid=(s // tr,),
        in_specs=[pl.BlockSpec((tr, d), lambda i: (i, 0)),
                  pl.BlockSpec((e, d), lambda i: (0, 0)),
                  pl.BlockSpec((e, 1), lambda i: (0, 0))],
        out_specs=[tok, tok, tok, pl.BlockSpec((e, 128), lambda i: (0, 0))],
        out_shape=[jax.ShapeDtypeStruct((TOP_K, s), jnp.int32),
                   jax.ShapeDtypeStruct((TOP_K, s), F32),
                   jax.ShapeDtypeStruct((TOP_K, s), jnp.int32),
                   jax.ShapeDtypeStruct((e, 128), jnp.int32)],
        scratch_shapes=[pltpu.VMEM((e, 128), F32)],
        compiler_params=_cparams(("arbitrary",)),
        name="route",
    )(hf, wrt, bias_col)


def _positions_kernel(idx_ref, rnk_ref, pstart_ref, pos_ref):
    e = pstart_ref.shape[0]
    ts = idx_ref.shape[1]
    ids = lax.broadcasted_iota(jnp.int32, (e, ts), 0)
    idx = idx_ref[...]
    pos_ref[...] = rnk_ref[...] + jnp.concatenate(
        [jnp.sum(jnp.where(ids == idx[k:k + 1, :], pstart_ref[...], 0), axis=0, keepdims=True)
         for k in range(TOP_K)], axis=0)


def _positions(idx, rnk, pstart_col, ts):
    s = idx.shape[1]
    e = pstart_col.shape[0]
    tok = pl.BlockSpec((TOP_K, ts), lambda i: (0, i))
    return pl.pallas_call(
        _positions_kernel,
        grid=(s // ts,),
        in_specs=[tok, tok, pl.BlockSpec((e, 1), lambda i: (0, 0))],
        out_specs=tok,
        out_shape=jax.ShapeDtypeStruct((TOP_K, s), jnp.int32),
        compiler_params=_cparams(("arbitrary",)),
        name="positions",
    )(idx, rnk, pstart_col)


def _swiglu_packed(xp, wg, wu, wd):
    lo, hi = _unpack_halves(xp)
    lo, hi = lo.astype(BF16), hi.astype(BF16)
    n = lo.shape[1]
    g = (jnp.dot(lo, wg[:n], preferred_element_type=F32)
         + jnp.dot(hi, wg[n:], preferred_element_type=F32))
    u = (jnp.dot(lo, wu[:n], preferred_element_type=F32)
         + jnp.dot(hi, wu[n:], preferred_element_type=F32))
    h = (g * jax.nn.sigmoid(g)) * u
    return jnp.dot(h.astype(BF16), wd[...], preferred_element_type=F32)


def _moe_kernel(ie_ref, ib_ref, first_ref, slot_ref, ne_ref, rows_ref, nv_ref, xs_ref, wg_hbm,
                wu_hbm, wd_hbm, ys_ref, wg_f, wu_f, wd_f, wg_b, wu_b, wd_b, sem):
    del ib_ref
    i = pl.program_id(0)

    def fetch(e, slot):
        return [pltpu.make_async_copy(src.at[e], dst.at[slot], sem.at[slot, n])
                for n, (src, dst) in enumerate(((wg_hbm, wg_f), (wu_hbm, wu_f), (wd_hbm, wd_f)))]

    @pl.when(i == 0)
    def _():
        for cp in fetch(ie_ref[0], 0):
            cp.start()

    @pl.when(i < nv_ref[0])
    def _():
        for slot in range(2):
            @pl.when((first_ref[i] == 1) & (slot_ref[i] == slot))
            def _():
                for cp in fetch(ie_ref[i], slot):
                    cp.wait()
                wg_b[...] = wg_f[slot].astype(BF16)
                wu_b[...] = wu_f[slot].astype(BF16)
                wd_b[...] = wd_f[slot].astype(BF16)

                @pl.when(ne_ref[i] >= 0)
                def _():
                    for cp in fetch(ne_ref[i], 1 - slot):
                        cp.start()

        row = lax.broadcasted_iota(jnp.int32, xs_ref.shape, 0)
        xp = jnp.where(row < rows_ref[i], xs_ref[...], jnp.uint32(0))
        ys_ref[...] = _pack_halves(_swiglu_packed(xp, wg_b, wu_b, wd_b))


def _moe(item_e, item_b, item_first, item_slot, item_next, item_rows, n_valid, xs, wg, wu, wd):
    m_pad, dh = xs.shape
    _, d, f = wg.shape
    n_items = item_e.shape[0]
    blk = lambda i, ie, ib, fi, sl, ne, nr, nv: (ib[i], 0)
    hbm = pl.BlockSpec(memory_space=pl.ANY)
    return pl.pallas_call(
        _moe_kernel,
        grid_spec=pltpu.PrefetchScalarGridSpec(
            num_scalar_prefetch=7,
            grid=(n_items,),
            in_specs=[pl.BlockSpec((MOE_ROWS, dh), blk), hbm, hbm, hbm],
            out_specs=pl.BlockSpec((MOE_ROWS, dh), blk),
            scratch_shapes=[pltpu.VMEM((2, d, f), F32), pltpu.VMEM((2, d, f), F32),
                            pltpu.VMEM((2, f, d), F32),
                            pltpu.VMEM((d, f), BF16), pltpu.VMEM((d, f), BF16),
                            pltpu.VMEM((f, d), BF16),
                            pltpu.SemaphoreType.DMA((2, 3))],
        ),
        out_shape=jax.ShapeDtypeStruct((m_pad, dh), jnp.uint32),
        compiler_params=_cparams(("arbitrary",)),
        name="moe",
    )(item_e, item_b, item_first, item_slot, item_next, item_rows, n_valid, xs, wg, wu, wd)


def _sc_gather_rows(table, idx_row):
    m = idx_row.shape[1]
    w = table.shape[1]
    idx_row = idx_row.reshape(m // SC_GATHER_WINDOW, SC_GATHER_WINDOW)
    mesh = plsc.VectorSubcoreMesh(core_axis_name="c", subcore_axis_name="s")

    @functools.partial(pl.kernel, mesh=mesh,
                       out_type=jax.ShapeDtypeStruct((m, w), table.dtype))
    def gather(table_hbm, idx_hbm, out_hbm):
        def body(idx_vmem, out_vmem):
            pltpu.sync_copy(table_hbm.at[idx_vmem.at[0]], out_vmem)

        pltpu.emit_pipeline(
            body,
            grid=(m // SC_GATHER_WINDOW,),
            in_specs=[pl.BlockSpec((1, SC_GATHER_WINDOW), lambda i: (i, 0))],
            out_specs=[pl.BlockSpec((SC_GATHER_WINDOW, w), lambda i: (i, 0))],
            core_axis_name=("c", "s"),
            dimension_semantics=(pltpu.PARALLEL,),
        )(idx_hbm, out_hbm)

    return gather(table, idx_row)


def _sc_scatter_rows(rows, idx_blocks, m_out):
    s, w = rows.shape
    mesh = plsc.VectorSubcoreMesh(core_axis_name="c", subcore_axis_name="s")

    @functools.partial(pl.kernel, mesh=mesh,
                       out_type=jax.ShapeDtypeStruct((m_out, w), rows.dtype))
    def scatter(rows_hbm, idx_hbm, out_hbm):
        def body(rows_vmem, idx_vmem):
            for k in range(TOP_K):
                pltpu.sync_copy(rows_vmem, out_hbm.at[idx_vmem.at[k]])

        pltpu.emit_pipeline(
            body,
            grid=(s // SC_GATHER_WINDOW,),
            in_specs=[pl.BlockSpec((SC_GATHER_WINDOW, w), lambda i: (i, 0)),
                      pl.BlockSpec((TOP_K, SC_GATHER_WINDOW), lambda i: (i, 0))],
            out_specs=[],
            core_axis_name=("c", "s"),
            dimension_semantics=(pltpu.PARALLEL,),
        )(rows_hbm, idx_hbm)

    return scatter(rows, idx_blocks)


def _combine_kernel(wt_ref, hf_ref, x1_ref, gt_ref, sg_ref, su_ref, sd_ref, g_ref, o_ref):
    tc = x1_ref.shape[0]
    y = _swiglu_packed(hf_ref[...], sg_ref, su_ref, sd_ref)
    wt = wt_ref[...]
    n = g_ref.shape[2]
    r_lo = jnp.zeros((tc, n), F32)
    r_hi = jnp.zeros((tc, n), F32)
    for k in range(TOP_K):
        lo, hi = _unpack_halves(g_ref[k])
        r_lo = r_lo + lo * wt[:, k:k + 1]
        r_hi = r_hi + hi * wt[:, k:k + 1]
    y = y + jnp.concatenate([r_lo, r_hi], axis=1)
    o_ref[...] = x1_ref[...] + gt_ref[...] * y


def _combine(wts_t, hfp, x1, gt, sg, su, sd, gathered, tc):
    s, d = x1.shape
    f = sg.shape[1]
    row = pl.BlockSpec((tc, d), lambda i: (i, 0))
    return pl.pallas_call(
        _combine_kernel,
        grid=(s // tc,),
        in_specs=[pl.BlockSpec((tc, TOP_K), lambda i: (i, 0)),
                  pl.BlockSpec((tc, d // 2), lambda i: (i, 0)), row,
                  pl.BlockSpec((1, d), lambda i: (0, 0)),
                  pl.BlockSpec((d, f), lambda i: (0, 0)),
                  pl.BlockSpec((d, f), lambda i: (0, 0)),
                  pl.BlockSpec((f, d), lambda i: (0, 0)),
                  pl.BlockSpec((TOP_K, tc, d // 2), lambda i: (0, i, 0))],
        out_specs=row,
        out_shape=jax.ShapeDtypeStruct((s, d), F32),
        compiler_params=_cparams(("arbitrary",)),
        name="combine",
    )(wts_t, hfp, x1, gt, sg, su, sd, gathered)


def _tile(n, want):
    t = min(n, want)
    assert n % t == 0, (n, t)
    return t


def _layer(l, x, c_col, pos_row, p):
    s, d = x.shape
    lambda_init = 0.8 - 0.6 * math.exp(-0.3 * l)
    gqk, gv = GLA_HEADS * GLA_DK, GLA_HEADS * GLA_DV
    dqk, dvw = DIFF_HEADS * 2 * DIFF_DH, DIFF_HEADS * DIFF_DV
    lowrank = p["gla_w_a2"].shape[0]

    mod = _ada(c_col, p["w_ada"], p["b_ada"][None, :])
    sh_a, sc_a, gt_a, sh_f, sc_f, gt_f = [mod[:, j * d:(j + 1) * d] for j in range(6)]

    w_in = p["w_in"]
    o = 0
    cols = {}
    for name, wdt in (("gq", gqk), ("gk", gqk), ("gv", gv), ("ga", lowrank), ("gg", gv),
                      ("dq", dqk), ("dk", dqk), ("dv", dvw), ("mg", d), ("md", d)):
        cols[name] = w_in[:, o:o + wdt]
        o += wdt
    row_names = ("gv", "gg", "mg", "md", "gq")
    w_row = jnp.concatenate([cols[n] for n in row_names], axis=1).astype(BF16)
    col_of, o = {}, 0
    for n in row_names:
        col_of[n] = o
        o += cols[n].shape[1]
    t_names = ("dq", "dk", "dv", "gk")
    w_t = jnp.concatenate([cols[n] for n in t_names], axis=1).T.astype(BF16)
    row_of, o = {}, 0
    for n in t_names:
        row_of[n] = o
        o += cols[n].shape[1]
    w_ga = jnp.pad(cols["ga"], ((0, 0), (0, 128 - lowrank))).astype(BF16)

    g1 = p["norm1_g"][None, :]
    tm = _tile(s, 1024)
    proj, ga = _inproj(x, g1, sc_a, sh_a, w_row, w_ga, tm, w_row.shape[1] // 3)
    projt = _inproj_t(x, g1, sc_a, sh_a, w_t, tm, w_t.shape[0] // 2)

    wa2t = jnp.pad(p["gla_w_a2"].T, ((0, 0), (0, 128 - lowrank)))
    o_gla = _gla(proj, projt, ga, wa2t, p["gla_b_a"][:, None], p["gla_onorm_g"][None, :],
                 _tile(s, 512), col_of["gq"], col_of["gv"], col_of["gg"], row_of["gk"])

    invf = ROPE_THETA ** (-jnp.arange(0, ROT_DIM, 2, dtype=F32) / ROT_DIM)
    qat, qbt, kr, vte, qn, kn = _qkprep(projt, pos_row, invf[:, None], p["diff_qnorm_g"][:, None],
                                p["diff_knorm_g"][:, None], _tile(s, 512), row_of["dq"],
                                row_of["dk"], row_of["dv"])
    tq = _tile(s, 512)
    o_diff = _diffattn(qat, qbt, kr, vte, qn, kn, p["diff_lq1"][None, :], p["diff_lk1"][None, :],
                       p["diff_lq2"][None, :], p["diff_lk2"][None, :],
                       p["diff_subln_g"][:, None], lambda_init, tq, tq)

    x1, hf, hfp = _mergeout(o_gla, o_diff, proj, x, p["w_branch_gla"].astype(BF16),
                       p["w_branch_diff"].astype(BF16), p["w_out"].astype(BF16), gt_a,
                       p["norm2_g"][None, :], sc_f, sh_f, _tile(s, 512),
                       col_of["mg"], col_of["md"])

    e = p["w_router"].shape[1]
    idx, wts, rnk, cnt = _route(hf, p["w_router"].T, p["router_bias"][:, None], _tile(s, 512))

    counts = cnt[:, 0]
    pcounts = ((counts + MOE_ROWS - 1) // MOE_ROWS) * MOE_ROWS
    pend = jnp.cumsum(pcounts)
    pstart = pend - pcounts
    pos = _positions(idx, rnk, pstart[:, None], _tile(s, 512))
    n_items = (s * TOP_K) // MOE_ROWS + e
    n_valid = (pend[-1] // MOE_ROWS).astype(jnp.int32)
    item_b = jnp.minimum(jnp.arange(n_items, dtype=jnp.int32), n_valid - 1)
    item_e = jnp.minimum(jnp.sum(pend[None, :] <= (item_b * MOE_ROWS)[:, None], axis=1),
                         e - 1).astype(jnp.int32)

    wn = SC_GATHER_WINDOW
    pos_w = pos.reshape(TOP_K, s // wn, wn).transpose(1, 0, 2).reshape(s // wn * TOP_K, wn)
    xs = _sc_scatter_rows(hfp, pos_w, n_items * MOE_ROWS)
    item_rows = jnp.clip(pstart[item_e] + counts[item_e] - item_b * MOE_ROWS, 0,
                         MOE_ROWS).astype(jnp.int32)
    prev_e = jnp.concatenate([jnp.full((1,), -1, jnp.int32), item_e[:-1]])
    item_first = ((jnp.arange(n_items) < n_valid) & (item_e != prev_e)).astype(jnp.int32)
    item_slot = ((jnp.cumsum(item_first) - 1) % 2).astype(jnp.int32)
    cand = jnp.where(pcounts > 0, jnp.arange(e, dtype=jnp.int32), e)
    following = jnp.concatenate([lax.cummin(cand[::-1])[::-1][1:], jnp.full((1,), e, jnp.int32)])
    item_next = jnp.where(following[item_e] < e, following[item_e], -1).astype(jnp.int32)
    ys = _moe(item_e, item_b, item_first, item_slot, item_next, item_rows, n_valid[None], xs,
              p["w_exp_gate"], p["w_exp_up"], p["w_exp_down"])
    gathered = _sc_gather_rows(ys, pos.reshape(1, TOP_K * s)).reshape(TOP_K, s, d // 2)
    return _combine(wts.T, hfp, x1, gt_f, p["w_sh_gate"].astype(BF16),
                    p["w_sh_up"].astype(BF16), p["w_sh_down"].astype(BF16), gathered,
                    _tile(s, 256))


_LAYER_PARAMS = ("w_ada", "b_ada", "norm1_g", "w_in", "gla_w_a2", "gla_b_a", "gla_onorm_g",
                 "diff_qnorm_g", "diff_knorm_g", "diff_lq1", "diff_lk1", "diff_lq2", "diff_lk2",
                 "diff_subln_g", "w_branch_gla", "w_branch_diff", "w_out", "norm2_g", "w_router",
                 "router_bias", "w_exp_gate", "w_exp_up", "w_exp_down", "w_sh_gate", "w_sh_up",
                 "w_sh_down")


def kernel(x, c, positions, w_ada, b_ada, norm1_g, w_in, gla_w_a2, gla_b_a, gla_onorm_g, diff_qnorm_g, diff_knorm_g, diff_lq1, diff_lk1, diff_lq2, diff_lk2, diff_subln_g, w_branch_gla, w_branch_diff, w_out, norm2_g, w_router, router_bias, w_exp_gate, w_exp_up, w_exp_down, w_sh_gate, w_sh_up, w_sh_down):
    stacked = dict(zip(_LAYER_PARAMS, (
        w_ada, b_ada, norm1_g, w_in, gla_w_a2, gla_b_a, gla_onorm_g, diff_qnorm_g, diff_knorm_g,
        diff_lq1, diff_lk1, diff_lq2, diff_lk2, diff_subln_g, w_branch_gla, w_branch_diff, w_out,
        norm2_g, w_router, router_bias, w_exp_gate, w_exp_up, w_exp_down, w_sh_gate, w_sh_up,
        w_sh_down)))
    b, s, d = x.shape
    assert b == 1, "single-sequence kernel"
    xl = x[0]
    c_col = c[0][:, None]
    pos_row = positions.astype(jnp.int32)
    for l in range(w_ada.shape[0]):
        xl = _layer(l, xl, c_col, pos_row, {k: v[l] for k, v in stacked.items()})
    return xl[None]
```

```python
import functools
import math

import jax
import jax.numpy as jnp
from jax import lax
from jax.experimental import pallas as pl
from jax.experimental.pallas import tpu as pltpu
from jax.experimental.pallas import tpu_sc as plsc

CHUNK = 64
EPS = 1e-6
GLA_HEADS = 4
GLA_DK = 128
GLA_DV = 256
GLA_TAU = 16.0
DIFF_HEADS = 8
DIFF_DH = 64
DIFF_DV = 2 * DIFF_DH
ROPE_THETA = 500000.0
ROT_DIM = DIFF_DH // 4
N_GROUPS = 8
TOPK_GROUPS = 4
TOP_K = 8
ROUTED_SCALE = 2.5

MOE_ROWS = 512
SC_GATHER_WINDOW = 64
VMEM_LIMIT = 56 * 1024 * 1024
NEG_BIG = -1e30
LOG2E = 1.4426950408889634
HIGHEST = lax.Precision.HIGHEST
F32 = jnp.float32
BF16 = jnp.bfloat16


def _cparams(sem):
    return pltpu.CompilerParams(dimension_semantics=sem, vmem_limit_bytes=VMEM_LIMIT)


def _nt_dot(a, b, precision=None):
    return lax.dot_general(a, b, (((1,), (1,)), ((), ())), precision=precision,
                           preferred_element_type=F32)


def _pack_halves(x):
    n = x.shape[1] // 2
    lo = pltpu.bitcast(x[:, :n].astype(BF16).astype(F32), jnp.uint32) >> 16
    hi = pltpu.bitcast(x[:, n:].astype(BF16).astype(F32), jnp.uint32) & jnp.uint32(0xFFFF0000)
    return lo | hi


def _unpack_halves(w):
    return (pltpu.bitcast(w << 16, F32), pltpu.bitcast(w & jnp.uint32(0xFFFF0000), F32))


def _rms_mod(x, g, sc, sh):
    xn = x * lax.rsqrt(jnp.mean(x * x, axis=-1, keepdims=True) + EPS)
    return (xn * g) * (1.0 + sc) + sh


def _ada_kernel(c_ref, w_ref, b_ref, o_ref):
    c = c_ref[...]
    ca = c * jax.nn.sigmoid(c)
    o_ref[...] = jnp.sum(ca * w_ref[...], axis=0, keepdims=True) + b_ref[...]


def _ada(c_col, w, b):
    d, n = w.shape
    tn = min(1024, n)
    return pl.pallas_call(
        _ada_kernel,
        grid=(n // tn,),
        in_specs=[pl.BlockSpec((d, 1), lambda j: (0, 0)),
                  pl.BlockSpec((d, tn), lambda j: (0, j)),
                  pl.BlockSpec((1, tn), lambda j: (0, j))],
        out_specs=pl.BlockSpec((1, tn), lambda j: (0, j)),
        out_shape=jax.ShapeDtypeStruct((1, n), F32),
        compiler_params=_cparams(("arbitrary",)),
        name="ada",
    )(c_col, w, b)


def _inproj_kernel(x_ref, g_ref, sc_ref, sh_ref, w_ref, wga_ref, o_ref, ga_ref, h_scr):
    @pl.when(pl.program_id(1) == 0)
    def _():
        h = _rms_mod(x_ref[...], g_ref[...], sc_ref[...], sh_ref[...]).astype(BF16)
        h_scr[...] = h
        ga_ref[...] = jnp.dot(h, wga_ref[...], preferred_element_type=F32)

    o_ref[...] = jnp.dot(h_scr[...], w_ref[...], preferred_element_type=F32).astype(BF16)


def _inproj(x, g, sc, sh, w, wga, tm, tn):
    s, d = x.shape
    n = w.shape[1]
    vec = pl.BlockSpec((1, d), lambda i, j: (0, 0))
    return pl.pallas_call(
        _inproj_kernel,
        grid=(s // tm, n // tn),
        in_specs=[pl.BlockSpec((tm, d), lambda i, j: (i, 0)), vec, vec, vec,
                  pl.BlockSpec((d, tn), lambda i, j: (0, j)),
                  pl.BlockSpec((d, 128), lambda i, j: (0, 0))],
        out_specs=[pl.BlockSpec((tm, tn), lambda i, j: (i, j)),
                   pl.BlockSpec((tm, 128), lambda i, j: (i, 0))],
        out_shape=[jax.ShapeDtypeStruct((s, n), BF16), jax.ShapeDtypeStruct((s, 128), F32)],
        scratch_shapes=[pltpu.VMEM((tm, d), BF16)],
        compiler_params=_cparams(("arbitrary", "arbitrary")),
        name="inproj",
    )(x, g, sc, sh, w, wga)


def _inproj_t_kernel(x_ref, g_ref, sc_ref, sh_ref, wt_ref, o_ref, h_scr):
    @pl.when(pl.program_id(1) == 0)
    def _():
        h_scr[...] = _rms_mod(x_ref[...], g_ref[...], sc_ref[...], sh_ref[...]).astype(BF16)

    o_ref[...] = _nt_dot(wt_ref[...], h_scr[...]).astype(BF16)


def _inproj_t(x, g, sc, sh, wt, tm, tn):
    s, d = x.shape
    n = wt.shape[0]
    vec = pl.BlockSpec((1, d), lambda i, j: (0, 0))
    return pl.pallas_call(
        _inproj_t_kernel,
        grid=(s // tm, n // tn),
        in_specs=[pl.BlockSpec((tm, d), lambda i, j: (i, 0)), vec, vec, vec,
                  pl.BlockSpec((tn, d), lambda i, j: (j, 0))],
        out_specs=pl.BlockSpec((tn, tm), lambda i, j: (j, i)),
        out_shape=jax.ShapeDtypeStruct((n, s), BF16),
        scratch_shapes=[pltpu.VMEM((tm, d), BF16)],
        compiler_params=_cparams(("arbitrary", "arbitrary")),
        name="inproj_t",
    )(x, g, sc, sh, wt)


def _gla_kernel(q_ref, kt_ref, v_ref, gg_ref, ga_ref, wa2t_ref, ba_ref, on_ref, o_ref,
                state_ref, o_scr):
    tt = q_ref.shape[0]
    nchunk = tt // CHUNK

    @pl.when(pl.program_id(0) == 0)
    def _():
        state_ref[...] = jnp.zeros_like(state_ref)

    zt = _nt_dot(wa2t_ref[...], ga_ref[...], precision=HIGHEST) + ba_ref[...]
    lat = (jnp.minimum(zt, 0.0) - jnp.log1p(jnp.exp(-jnp.abs(zt)))) * (1.0 / GLA_TAU)
    row = lax.broadcasted_iota(jnp.int32, (tt, tt), 0)
    col = lax.broadcasted_iota(jnp.int32, (tt, tt), 1)
    same = (row // CHUNK) == (col // CHUNK)
    incl = jnp.where(same & (row <= col), 1.0, 0.0).astype(BF16)
    full = jnp.where(same, 1.0, 0.0).astype(BF16)
    lat_hi = lat.astype(BF16)
    lat_lo = (lat - lat_hi.astype(F32)).astype(BF16)
    cumt = (jnp.dot(lat_hi, incl, preferred_element_type=F32)
            + jnp.dot(lat_lo, incl, preferred_element_type=F32))
    tott = (jnp.dot(lat_hi, full, preferred_element_type=F32)
            + jnp.dot(lat_lo, full, preferred_element_type=F32))
    kdt = kt_ref[...].astype(F32) * jnp.exp(tott - cumt)
    dec = jnp.exp(tott)

    lane = lax.broadcasted_iota(jnp.int32, (GLA_DK, 2 * CHUNK), 1)
    upd = {}
    for c in range(nchunk):
        pair = (c // 2) * 2 * CHUNK
        if nchunk > 1:
            keep = (lane // CHUNK) == (c % 2)
        for h in range(GLA_HEADS):
            rows = slice(h * GLA_DK, (h + 1) * GLA_DK)
            vcols = slice(h * GLA_DV, (h + 1) * GLA_DV)
            if nchunk > 1:
                a = jnp.where(keep, kdt[rows, pair:pair + 2 * CHUNK], 0.0).astype(BF16)
                vp = v_ref[pair:pair + 2 * CHUNK, vcols]
            else:
                a = kdt[rows, :].astype(BF16)
                vp = v_ref[:, vcols]
            upd[c, h] = jnp.dot(a, vp, preferred_element_type=F32)

    for h in range(GLA_HEADS):
        rows = slice(h * GLA_DK, (h + 1) * GLA_DK)
        vcols = slice(h * GLA_DV, (h + 1) * GLA_DV)
        st = state_ref[h]
        states = []
        for c in range(nchunk):
            st = st * dec[rows, c * CHUNK:c * CHUNK + 1] + upd[c, h]
            states.append(st.astype(BF16))
        state_ref[h] = st
        for c in range(nchunk):
            o_scr[c * CHUNK:(c + 1) * CHUNK, vcols] = jnp.dot(
                q_ref[c * CHUNK:(c + 1) * CHUNK, rows], states[c], preferred_element_type=F32)

    for h in range(GLA_HEADS):
        vcols = slice(h * GLA_DV, (h + 1) * GLA_DV)
        o = o_scr[:, vcols] * (GLA_DK ** -0.5)
        o = o * lax.rsqrt(jnp.mean(o * o, axis=-1, keepdims=True) + EPS) * on_ref[...]
        g = gg_ref[:, vcols].astype(F32)
        o_ref[:, vcols] = (o * (g * jax.nn.sigmoid(g))).astype(BF16)


def _gla(proj, projt, ga, wa2t, ba_col, on_g, tt, col_q, col_v, col_g, row_k):
    s = proj.shape[0]
    qk = GLA_HEADS * GLA_DK
    vw = GLA_HEADS * GLA_DV
    return pl.pallas_call(
        _gla_kernel,
        grid=(s // tt,),
        in_specs=[pl.BlockSpec((tt, qk), lambda i: (i, col_q // qk)),
                  pl.BlockSpec((qk, tt), lambda i: (row_k // qk, i)),
                  pl.BlockSpec((tt, vw), lambda i: (i, col_v // vw)),
                  pl.BlockSpec((tt, vw), lambda i: (i, col_g // vw)),
                  pl.BlockSpec((tt, 128), lambda i: (i, 0)),
                  pl.BlockSpec((qk, 128), lambda i: (0, 0)),
                  pl.BlockSpec((qk, 1), lambda i: (0, 0)),
                  pl.BlockSpec((1, GLA_DV), lambda i: (0, 0))],
        out_specs=pl.BlockSpec((tt, vw), lambda i: (i, 0)),
        out_shape=jax.ShapeDtypeStruct((s, vw), BF16),
        scratch_shapes=[pltpu.VMEM((GLA_HEADS, GLA_DK, GLA_DV), F32),
                        pltpu.VMEM((tt, vw), F32)],
        compiler_params=_cparams(("arbitrary",)),
        name="gla",
    )(proj, projt, proj, proj, ga, wa2t, ba_col, on_g)


def _qknorm_rope_t(xt, g_col, cos, sin):
    n, tm = xt.shape
    x3 = xt.reshape(n // DIFF_DH, DIFF_DH, tm)
    r = lax.rsqrt(jnp.mean(x3 * x3, axis=1, keepdims=True) + EPS)
    y = x3 * r * g_col[None]
    half = ROT_DIM // 2
    y1, y2, rest = y[:, :half], y[:, half:ROT_DIM], y[:, ROT_DIM:]
    o1 = y1 * cos[None] - y2 * sin[None]
    o2 = y2 * cos[None] + y1 * sin[None]
    return jnp.concatenate([o1, o2, rest], axis=1)


def _seg_norms(x3):
    return jnp.sqrt(jnp.sum(x3 * x3, axis=1)).reshape(DIFF_HEADS, 2, x3.shape[2])


def _qkprep_kernel(qt_ref, kt_ref, vt_ref, pos_ref, invf_ref, qg_ref, kg_ref, qa_ref, qb_ref,
                   ko_ref, ve_ref, qn_ref, kn_ref):
    tm = qt_ref.shape[1]
    v3 = vt_ref[...].reshape(DIFF_HEADS, DIFF_DV, tm)
    ones = jnp.ones((DIFF_HEADS, ATT_SUM_ROWS, tm), BF16)
    ve_ref[...] = jnp.concatenate([v3, ones], axis=1).reshape(-1, tm)
    ang = pos_ref[...].astype(F32) * invf_ref[...]
    cos, sin = jnp.cos(ang), jnp.sin(ang)
    k3 = _qknorm_rope_t(kt_ref[...].astype(F32), kg_ref[...], cos, sin)
    ko_ref[...] = k3.reshape(-1, tm).T.astype(BF16)
    kn_ref[...] = _seg_norms(k3)
    q3 = _qknorm_rope_t(qt_ref[...].astype(F32), qg_ref[...], cos, sin) * (
        DIFF_DH ** -0.5 * LOG2E)
    qn_ref[...] = _seg_norms(q3)
    seg = lax.broadcasted_iota(jnp.int32, q3.shape, 0)
    qa_ref[...] = jnp.where(seg % 2 == 0, q3, 0.0).reshape(-1, tm).astype(BF16)
    qb_ref[...] = jnp.where(seg % 2 == 1, q3, 0.0).reshape(-1, tm).astype(BF16)


def _qkprep(projt, pos_row, invf_col, qg_col, kg_col, tm, row_q, row_k, row_v):
    s = projt.shape[1]
    n = DIFF_HEADS * 2 * DIFF_DH
    ne = DIFF_HEADS * (DIFF_DV + ATT_SUM_ROWS)
    col = pl.BlockSpec((DIFF_DH, 1), lambda i: (0, 0))
    return pl.pallas_call(
        _qkprep_kernel,
        grid=(s // tm,),
        in_specs=[pl.BlockSpec((n, tm), lambda i: (row_q // n, i)),
                  pl.BlockSpec((n, tm), lambda i: (row_k // n, i)),
                  pl.BlockSpec((n, tm), lambda i: (row_v // n, i)),
                  pl.BlockSpec((1, tm), lambda i: (0, i)),
                  pl.BlockSpec((ROT_DIM // 2, 1), lambda i: (0, 0)), col, col],
        out_specs=[pl.BlockSpec((n, tm), lambda i: (0, i)),
                   pl.BlockSpec((n, tm), lambda i: (0, i)),
                   pl.BlockSpec((tm, n), lambda i: (i, 0)),
                   pl.BlockSpec((ne, tm), lambda i: (0, i)),
                   pl.BlockSpec((DIFF_HEADS, 2, tm), lambda i: (0, 0, i)),
                   pl.BlockSpec((DIFF_HEADS, 2, tm), lambda i: (0, 0, i))],
        out_shape=[jax.ShapeDtypeStruct((n, s), BF16), jax.ShapeDtypeStruct((n, s), BF16),
                   jax.ShapeDtypeStruct((s, n), BF16), jax.ShapeDtypeStruct((ne, s), BF16),
                   jax.ShapeDtypeStruct((DIFF_HEADS, 2, s), F32),
                   jax.ShapeDtypeStruct((DIFF_HEADS, 2, s), F32)],
        compiler_params=_cparams(("arbitrary",)),
        name="qkprep",
    )(projt, projt, projt, pos_row, invf_col, qg_col, kg_col)


ATT_COLS = 256
ATT_LOOKAHEAD = {True: 2, False: 4}
ATT_BODY_TILES = {True: 8, False: 2}
ATT_SUM_ROWS = 16
ATT_BOUND_SLACK = 1.02
ATT_BOUND_LIMIT = 50.0


def _diffattn_kernel(qa_ref, qb_ref, k_ref, vt_ref, qn_ref, kn_ref, lq1_ref, lk1_ref, lq2_ref,
                     lk2_ref, sg_ref, o_ref, *scr, lambda_init, tk, cols):
    tq = qa_ref.shape[1]
    nblk = 2 * tq // cols
    m_scr, acc_scr = (scr[b * nblk:(b + 1) * nblk] for b in range(2))
    kmax_scr = scr[2 * nblk]
    s_scr = scr[2 * nblk + 1:]
    i = pl.program_id(1)

    @pl.when(i == 0)
    def _():
        kmax_scr[...] = jnp.max(kn_ref[...], axis=1, keepdims=True)

    bound = qn_ref[...] * kmax_scr[...] * ATT_BOUND_SLACK
    bound = jnp.concatenate([bound[0:1], bound[1:2]], axis=1)
    bounded = jnp.max(bound) < ATT_BOUND_LIMIT
    for c in range(nblk):
        m_scr[c][...] = jnp.where(bounded, bound[:, c * cols:(c + 1) * cols], NEG_BIG)
        acc_scr[c][...] = jnp.zeros_like(acc_scr[c])

    def scores(j, c):
        start = pl.multiple_of(j * tk, tk)
        q_ref = qa_ref if c * cols < tq else qb_ref
        off = (c * cols) % tq
        return jnp.dot(k_ref[pl.ds(start, tk), :], q_ref[:, off:off + cols],
                       preferred_element_type=F32)

    def steps(tiles, masked, next_tile, fixed):
        look = ATT_LOOKAHEAD[fixed]
        items = [(j, c) for j in tiles for c in range(nblk)]
        pending = []
        for n, (j, c) in enumerate(items):
            s = s_scr[n][...] if n < look else pending.pop(0)
            ahead = n + look
            if ahead < len(items):
                pending.append(scores(*items[ahead]))
            elif next_tile is not None:
                s_scr[ahead - len(items)][...] = scores(next_tile, ahead - len(items))
            start = pl.multiple_of(j * tk, tk)
            keys = tk
            if masked:
                keys = (c * cols) % tq + cols
                s = s[:keys]
                krow = lax.broadcasted_iota(jnp.int32, (keys, cols), 0)
                qcol = lax.broadcasted_iota(jnp.int32, (keys, cols), 1)
                qpos = i * tq + (c * cols) % tq + qcol
                s = jnp.where((start + krow) // CHUNK <= qpos // CHUNK, s, NEG_BIG)
            vj = vt_ref[:, pl.ds(start, keys)]
            if fixed:
                p = jnp.exp2(s - m_scr[c][...]).astype(BF16)
                acc_scr[c][...] += jnp.dot(vj, p, preferred_element_type=F32)
            else:
                m_prev = m_scr[c][...]
                m_new = jnp.maximum(m_prev, jnp.max(s, axis=0, keepdims=True))
                alpha = jnp.exp2(m_prev - m_new)
                p = jnp.exp2((s - m_new).astype(BF16))
                acc_scr[c][...] = alpha * acc_scr[c][...] + jnp.dot(
                    vj, p, preferred_element_type=F32)
                m_scr[c][...] = m_new

    def attend(fixed):
        for c in range(ATT_LOOKAHEAD[fixed]):
            s_scr[c][...] = scores(0, c)
        big = ATT_BODY_TILES[fixed]
        lax.fori_loop(0, i // big, lambda t, c: (steps(
            tuple(big * t + u for u in range(big)), False, big * t + big, fixed), c)[1], 0)
        size = big // 2
        while size >= 1:
            first = (i // (2 * size)) * (2 * size)

            @pl.when(i % (2 * size) >= size)
            def _(first=first, size=size):
                steps(tuple(first + u for u in range(size)), False, first + size, fixed)

            size //= 2
        steps((i,), True, None, fixed)

    @pl.when(bounded)
    def _():
        attend(True)

    @pl.when(jnp.logical_not(bounded))
    def _():
        attend(False)

    o = jnp.concatenate([acc_scr[c][:DIFF_DV] / acc_scr[c][DIFF_DV:DIFF_DV + 1]
                         for c in range(nblk)], axis=1)
    lam = (jnp.exp(jnp.sum(lq1_ref[...] * lk1_ref[...]))
           - jnp.exp(jnp.sum(lq2_ref[...] * lk2_ref[...])) + lambda_init)
    o = o[:, :tq] - lam * o[:, tq:]
    o = o * lax.rsqrt(jnp.mean(o * o, axis=0, keepdims=True) + EPS) * sg_ref[...]
    o_ref[...] = (o * (1.0 - lambda_init)).T.astype(BF16)


def _diffattn(qat, qbt, kr, vte, qn, kn, lq1, lk1, lq2, lk2, sg_col, lambda_init, tq, tk):
    s = kr.shape[0]
    hd = 2 * DIFF_DH
    vec = pl.BlockSpec((1, DIFF_DH), lambda h, i: (0, 0))
    cols = min(ATT_COLS, tq)
    nblk = 2 * tq // cols
    kern = functools.partial(_diffattn_kernel, lambda_init=lambda_init, tk=tk, cols=cols)
    dve = DIFF_DV + ATT_SUM_ROWS
    assert tq == tk and nblk >= max(ATT_LOOKAHEAD.values())
    scratch = ([pltpu.VMEM((1, cols), F32)] * nblk + [pltpu.VMEM((dve, cols), F32)] * nblk
               + [pltpu.VMEM((2, 1), F32)]
               + [pltpu.VMEM((tk, cols), F32)] * max(ATT_LOOKAHEAD.values()))
    return pl.pallas_call(
        kern,
        grid=(DIFF_HEADS, s // tq),
        in_specs=[pl.BlockSpec((hd, tq), lambda h, i: (h, i)),
                  pl.BlockSpec((hd, tq), lambda h, i: (h, i)),
                  pl.BlockSpec((s, hd), lambda h, i: (0, h)),
                  pl.BlockSpec((dve, s), lambda h, i: (h, 0)),
                  pl.BlockSpec((None, 2, tq), lambda h, i: (h, 0, i)),
                  pl.BlockSpec((None, 2, s), lambda h, i: (h, 0, 0)),
                  vec, vec, vec, vec,
                  pl.BlockSpec((DIFF_DV, 1), lambda h, i: (0, 0))],
        out_specs=pl.BlockSpec((tq, DIFF_DV), lambda h, i: (i, h)),
        out_shape=jax.ShapeDtypeStruct((s, DIFF_HEADS * DIFF_DV), BF16),
        scratch_shapes=scratch,
        compiler_params=_cparams(("arbitrary", "arbitrary")),
        name="diffattn",
    )(qat, qbt, kr, vte, qn, kn, lq1, lk1, lq2, lk2, sg_col)


def _mergeout_kernel(og_ref, od_ref, mg_ref, md_ref, x_ref, wbg_ref, wbd_ref, wo_ref, gt_ref,
                     g2_ref, sc_ref, sh_ref, x1_ref, hf_ref, hfp_ref):
    bg = jnp.dot(og_ref[...], wbg_ref[...], preferred_element_type=F32)
    bd = jnp.dot(od_ref[...], wbd_ref[...], preferred_element_type=F32)
    merged = (jax.nn.sigmoid(mg_ref[...].astype(F32)) * bg
              + jax.nn.sigmoid(md_ref[...].astype(F32)) * bd)
    x1 = x_ref[...] + gt_ref[...] * jnp.dot(merged.astype(BF16), wo_ref[...],
                                             preferred_element_type=F32)
    x1_ref[...] = x1
    hf = _rms_mod(x1, g2_ref[...], sc_ref[...], sh_ref[...])
    hf_ref[...] = hf
    hfp_ref[...] = _pack_halves(hf)


def _mergeout(og, od, proj, x, wbg, wbd, wo, gt, g2, sc, sh, tm, col_mg, col_md):
    s, d = x.shape
    vec = pl.BlockSpec((1, d), lambda i: (0, 0))
    wspec = pl.BlockSpec((d, d), lambda i: (0, 0))
    row = pl.BlockSpec((tm, d), lambda i: (i, 0))
    return pl.pallas_call(
        _mergeout_kernel,
        grid=(s // tm,),
        in_specs=[row, row,
                  pl.BlockSpec((tm, d), lambda i: (i, col_mg // d)),
                  pl.BlockSpec((tm, d), lambda i: (i, col_md // d)),
                  row, wspec, wspec, wspec, vec, vec, vec, vec],
        out_specs=[row, row, pl.BlockSpec((tm, d // 2), lambda i: (i, 0))],
        out_shape=[jax.ShapeDtypeStruct((s, d), F32), jax.ShapeDtypeStruct((s, d), F32),
                   jax.ShapeDtypeStruct((s, d // 2), jnp.uint32)],
        compiler_params=_cparams(("arbitrary",)),
        name="mergeout",
    )(og, od, proj, proj, x, wbg, wbd, wo, gt, g2, sc, sh)


def _route_kernel(hf_ref, wrt_ref, bias_ref, idx_ref, wts_ref, rnk_ref, cnt_ref, run_scr):
    tr = hf_ref.shape[0]
    e = wrt_ref.shape[0]
    gsz = e // N_GROUPS

    @pl.when(pl.program_id(0) == 0)
    def _():
        run_scr[...] = jnp.zeros_like(run_scr)

    logits = _nt_dot(wrt_ref[...], hf_ref[...], precision=HIGHEST)
    scores = jax.nn.sigmoid(logits)
    biased = scores + bias_ref[...]
    g3 = biased.reshape(N_GROUPS, gsz, tr)
    m1 = jnp.max(g3, axis=1, keepdims=True)
    n_top = jnp.sum(jnp.where(g3 == m1, 1.0, 0.0), axis=1, keepdims=True)
    m2 = jnp.max(jnp.where(g3 < m1, g3, -jnp.inf), axis=1, keepdims=True)
    gs = (m1 + jnp.where(n_top >= 2.0, m1, m2)).reshape(N_GROUPS, tr)
    gi = lax.broadcasted_iota(jnp.int32, (N_GROUPS, tr), 0)
    beaten = jnp.zeros((N_GROUPS, tr), F32)
    for g in range(N_GROUPS):
        other = gs[g:g + 1, :]
        beaten = beaten + jnp.where((other > gs) | ((other == gs) & (g < gi)), 1.0, 0.0)
    gsel = (beaten < float(TOPK_GROUPS)).reshape(N_GROUPS, 1, tr)
    masked = jnp.where(gsel, g3, -jnp.inf).reshape(e, tr)

    ids = lax.broadcasted_iota(jnp.int32, (e, tr), 0)
    chosen = jnp.zeros((e, tr), F32)
    sel_idx, sel_score = [], []
    for _ in range(TOP_K):
        mx = jnp.max(masked, axis=0, keepdims=True)
        ix = jnp.min(jnp.where(masked == mx, ids, e), axis=0, keepdims=True)
        hit = ids == ix
        sel_idx.append(ix)
        sel_score.append(jnp.sum(jnp.where(hit, scores, 0.0), axis=0, keepdims=True))
        chosen = jnp.where(hit, 1.0, chosen)
        masked = jnp.where(hit, -jnp.inf, masked)
    idx = jnp.concatenate(sel_idx, axis=0)
    sc = jnp.concatenate(sel_score, axis=0)
    idx_ref[...] = idx
    wts_ref[...] = sc / jnp.sum(sc, axis=0, keepdims=True) * ROUTED_SCALE

    row = lax.broadcasted_iota(jnp.int32, (tr, tr), 0)
    col = lax.broadcasted_iota(jnp.int32, (tr, tr), 1)
    before = jnp.where(row < col, 1.0, 0.0).astype(BF16)
    prior = jnp.dot(chosen.astype(BF16), before, preferred_element_type=F32) + run_scr[:, 0:1]
    rnk_ref[...] = jnp.concatenate(
        [jnp.sum(jnp.where(ids == sel_idx[k], prior, 0.0), axis=0, keepdims=True)
         for k in range(TOP_K)], axis=0).astype(jnp.int32)
    run_scr[...] = run_scr[...] + jnp.sum(chosen, axis=1, keepdims=True)
    cnt_ref[...] = run_scr[...].astype(jnp.int32)


def _route(hf, wrt, bias_col, tr):
    s, d = hf.shape
    e = wrt.shape[0]
    tok = pl.BlockSpec((TOP_K, tr), lambda i: (0, i))
    return pl.pallas_call(
        _route_kernel,
        grid=(s // tr,),
        in_specs=[pl.BlockSpec((tr, d), lambda i: (i, 0)),
                  pl.BlockSpec((e, d), lambda i: (0, 0)),
                  pl.BlockSpec((e, 1), lambda i: (0, 0))],
        out_specs=[tok, tok, tok, pl.BlockSpec((e, 128), lambda i: (0, 0))],
        out_shape=[jax.ShapeDtypeStruct((TOP_K, s), jnp.int32),
                   jax.ShapeDtypeStruct((TOP_K, s), F32),
                   jax.ShapeDtypeStruct((TOP_K, s), jnp.int32),
                   jax.ShapeDtypeStruct((e, 128), jnp.int32)],
        scratch_shapes=[pltpu.VMEM((e, 128), F32)],
        compiler_params=_cparams(("arbitrary",)),
        name="route",
    )(hf, wrt, bias_col)


def _positions_kernel(idx_ref, rnk_ref, pstart_ref, pos_ref):
    e = pstart_ref.shape[0]
    ts = idx_ref.shape[1]
    ids = lax.broadcasted_iota(jnp.int32, (e, ts), 0)
    idx = idx_ref[...]
    pos_ref[...] = rnk_ref[...] + jnp.concatenate(
        [jnp.sum(jnp.where(ids == idx[k:k + 1, :], pstart_ref[...], 0), axis=0, keepdims=True)
         for k in range(TOP_K)], axis=0)


def _positions(idx, rnk, pstart_col, ts):
    s = idx.shape[1]
    e = pstart_col.shape[0]
    tok = pl.BlockSpec((TOP_K, ts), lambda i: (0, i))
    return pl.pallas_call(
        _positions_kernel,
        grid=(s // ts,),
        in_specs=[tok, tok, pl.BlockSpec((e, 1), lambda i: (0, 0))],
        out_specs=tok,
        out_shape=jax.ShapeDtypeStruct((TOP_K, s), jnp.int32),
        compiler_params=_cparams(("arbitrary",)),
        name="positions",
    )(idx, rnk, pstart_col)


def _swiglu_packed(xp, wg, wu, wd):
    lo, hi = _unpack_halves(xp)
    lo, hi = lo.astype(BF16), hi.astype(BF16)
    n = lo.shape[1]
    g = (jnp.dot(lo, wg[:n], preferred_element_type=F32)
         + jnp.dot(hi, wg[n:], preferred_element_type=F32))
    u = (jnp.dot(lo, wu[:n], preferred_element_type=F32)
         + jnp.dot(hi, wu[n:], preferred_element_type=F32))
    h = (g * jax.nn.sigmoid(g)) * u
    return jnp.dot(h.astype(BF16), wd[...], preferred_element_type=F32)


def _moe_kernel(ie_ref, ib_ref, first_ref, slot_ref, ne_ref, rows_ref, nv_ref, xs_ref, wg_hbm,
                wu_hbm, wd_hbm, ys_ref, wg_f, wu_f, wd_f, wg_b, wu_b, wd_b, sem):
    del ib_ref
    i = pl.program_id(0)

    def fetch(e, slot):
        return [pltpu.make_async_copy(src.at[e], dst.at[slot], sem.at[slot, n])
                for n, (src, dst) in enumerate(((wg_hbm, wg_f), (wu_hbm, wu_f), (wd_hbm, wd_f)))]

    @pl.when(i == 0)
    def _():
        for cp in fetch(ie_ref[0], 0):
            cp.start()

    @pl.when(i < nv_ref[0])
    def _():
        for slot in range(2):
            @pl.when((first_ref[i] == 1) & (slot_ref[i] == slot))
            def _():
                for cp in fetch(ie_ref[i], slot):
                    cp.wait()
                wg_b[...] = wg_f[slot].astype(BF16)
                wu_b[...] = wu_f[slot].astype(BF16)
                wd_b[...] = wd_f[slot].astype(BF16)

                @pl.when(ne_ref[i] >= 0)
                def _():
                    for cp in fetch(ne_ref[i], 1 - slot):
                        cp.start()

        row = lax.broadcasted_iota(jnp.int32, xs_ref.shape, 0)
        xp = jnp.where(row < rows_ref[i], xs_ref[...], jnp.uint32(0))
        ys_ref[...] = _pack_halves(_swiglu_packed(xp, wg_b, wu_b, wd_b))


def _moe(item_e, item_b, item_first, item_slot, item_next, item_rows, n_valid, xs, wg, wu, wd):
    m_pad, dh = xs.shape
    _, d, f = wg.shape
    n_items = item_e.shape[0]
    blk = lambda i, ie, ib, fi, sl, ne, nr, nv: (ib[i], 0)
    hbm = pl.BlockSpec(memory_space=pl.ANY)
    return pl.pallas_call(
        _moe_kernel,
        grid_spec=pltpu.PrefetchScalarGridSpec(
            num_scalar_prefetch=7,
            grid=(n_items,),
            in_specs=[pl.BlockSpec((MOE_ROWS, dh), blk), hbm, hbm, hbm],
            out_specs=pl.BlockSpec((MOE_ROWS, dh), blk),
            scratch_shapes=[pltpu.VMEM((2, d, f), F32), pltpu.VMEM((2, d, f), F32),
                            pltpu.VMEM((2, f, d), F32),
                            pltpu.VMEM((d, f), BF16), pltpu.VMEM((d, f), BF16),
                            pltpu.VMEM((f, d), BF16),
                            pltpu.SemaphoreType.DMA((2, 3))],
        ),
        out_shape=jax.ShapeDtypeStruct((m_pad, dh), jnp.uint32),
        compiler_params=_cparams(("arbitrary",)),
        name="moe",
    )(item_e, item_b, item_first, item_slot, item_next, item_rows, n_valid, xs, wg, wu, wd)


def _sc_gather_rows(table, idx_row):
    m = idx_row.shape[1]
    w = table.shape[1]
    idx_row = idx_row.reshape(m // SC_GATHER_WINDOW, SC_GATHER_WINDOW)
    mesh = plsc.VectorSubcoreMesh(core_axis_name="c", subcore_axis_name="s")

    @functools.partial(pl.kernel, mesh=mesh,
                       out_type=jax.ShapeDtypeStruct((m, w), table.dtype))
    def gather(table_hbm, idx_hbm, out_hbm):
        def body(idx_vmem, out_vmem):
            pltpu.sync_copy(table_hbm.at[idx_vmem.at[0]], out_vmem)

        pltpu.emit_pipeline(
            body,
            grid=(m // SC_GATHER_WINDOW,),
            in_specs=[pl.BlockSpec((1, SC_GATHER_WINDOW), lambda i: (i, 0))],
            out_specs=[pl.BlockSpec((SC_GATHER_WINDOW, w), lambda i: (i, 0))],
            core_axis_name=("c", "s"),
            dimension_semantics=(pltpu.PARALLEL,),
        )(idx_hbm, out_hbm)

    return gather(table, idx_row)


def _sc_scatter_rows(rows, idx_blocks, m_out):
    s, w = rows.shape
    mesh = plsc.VectorSubcoreMesh(core_axis_name="c", subcore_axis_name="s")

    @functools.partial(pl.kernel, mesh=mesh,
                       out_type=jax.ShapeDtypeStruct((m_out, w), rows.dtype))
    def scatter(rows_hbm, idx_hbm, out_hbm):
        def body(rows_vmem, idx_vmem):
            for k in range(TOP_K):
                pltpu.sync_copy(rows_vmem, out_hbm.at[idx_vmem.at[k]])

        pltpu.emit_pipeline(
            body,
            grid=(s // SC_GATHER_WINDOW,),
            in_specs=[pl.BlockSpec((SC_GATHER_WINDOW, w), lambda i: (i, 0)),
                      pl.BlockSpec((TOP_K, SC_GATHER_WINDOW), lambda i: (i, 0))],
            out_specs=[],
            core_axis_name=("c", "s"),
            dimension_semantics=(pltpu.PARALLEL,),
        )(rows_hbm, idx_hbm)

    return scatter(rows, idx_blocks)


def _combine_kernel(wt_ref, hf_ref, x1_ref, gt_ref, sg_ref, su_ref, sd_ref, g_ref, o_ref):
    tc = x1_ref.shape[0]
    y = _swiglu_packed(hf_ref[...], sg_ref, su_ref, sd_ref)
    wt = wt_ref[...]
    n = g_ref.shape[2]
    r_lo = jnp.zeros((tc, n), F32)
    r_hi = jnp.zeros((tc, n), F32)
    for k in range(TOP_K):
        lo, hi = _unpack_halves(g_ref[k])
        r_lo = r_lo + lo * wt[:, k:k + 1]
        r_hi = r_hi + hi * wt[:, k:k + 1]
    y = y + jnp.concatenate([r_lo, r_hi], axis=1)
    o_ref[...] = x1_ref[...] + gt_ref[...] * y


def _combine(wts_t, hfp, x1, gt, sg, su, sd, gathered, tc):
    s, d = x1.shape
    f = sg.shape[1]
    row = pl.BlockSpec((tc, d), lambda i: (i, 0))
    return pl.pallas_call(
        _combine_kernel,
        grid=(s // tc,),
        in_specs=[pl.BlockSpec((tc, TOP_K), lambda i: (i, 0)),
                  pl.BlockSpec((tc, d // 2), lambda i: (i, 0)), row,
                  pl.BlockSpec((1, d), lambda i: (0, 0)),
                  pl.BlockSpec((d, f), lambda i: (0, 0)),
                  pl.BlockSpec((d, f), lambda i: (0, 0)),
                  pl.BlockSpec((f, d), lambda i: (0, 0)),
                  pl.BlockSpec((TOP_K, tc, d // 2), lambda i: (0, i, 0))],
        out_specs=row,
        out_shape=jax.ShapeDtypeStruct((s, d), F32),
        compiler_params=_cparams(("arbitrary",)),
        name="combine",
    )(wts_t, hfp, x1, gt, sg, su, sd, gathered)


def _tile(n, want):
    t = min(n, want)
    assert n % t == 0, (n, t)
    return t


def _layer(l, x, c_col, pos_row, p):
    s, d = x.shape
    lambda_init = 0.8 - 0.6 * math.exp(-0.3 * l)
    gqk, gv = GLA_HEADS * GLA_DK, GLA_HEADS * GLA_DV
    dqk, dvw = DIFF_HEADS * 2 * DIFF_DH, DIFF_HEADS * DIFF_DV
    lowrank = p["gla_w_a2"].shape[0]

    mod = _ada(c_col, p["w_ada"], p["b_ada"][None, :])
    sh_a, sc_a, gt_a, sh_f, sc_f, gt_f = [mod[:, j * d:(j + 1) * d] for j in range(6)]

    w_in = p["w_in"]
    o = 0
    cols = {}
    for name, wdt in (("gq", gqk), ("gk", gqk), ("gv", gv), ("ga", lowrank), ("gg", gv),
                      ("dq", dqk), ("dk", dqk), ("dv", dvw), ("mg", d), ("md", d)):
        cols[name] = w_in[:, o:o + wdt]
        o += wdt
    row_names = ("gv", "gg", "mg", "md", "gq")
    w_row = jnp.concatenate([cols[n] for n in row_names], axis=1).astype(BF16)
    col_of, o = {}, 0
    for n in row_names:
        col_of[n] = o
        o += cols[n].shape[1]
    t_names = ("dq", "dk", "dv", "gk")
    w_t = jnp.concatenate([cols[n] for n in t_names], axis=1).T.astype(BF16)
    row_of, o = {}, 0
    for n in t_names:
        row_of[n] = o
        o += cols[n].shape[1]
    w_ga = jnp.pad(cols["ga"], ((0, 0), (0, 128 - lowrank))).astype(BF16)

    g1 = p["norm1_g"][None, :]
    tm = _tile(s, 1024)
    proj, ga = _inproj(x, g1, sc_a, sh_a, w_row, w_ga, tm, w_row.shape[1] // 3)
    projt = _inproj_t(x, g1, sc_a, sh_a, w_t, tm, w_t.shape[0] // 2)

    wa2t = jnp.pad(p["gla_w_a2"].T, ((0, 0), (0, 128 - lowrank)))
    o_gla = _gla(proj, projt, ga, wa2t, p["gla_b_a"][:, None], p["gla_onorm_g"][None, :],
                 _tile(s, 512), col_of["gq"], col_of["gv"], col_of["gg"], row_of["gk"])

    invf = ROPE_THETA ** (-jnp.arange(0, ROT_DIM, 2, dtype=F32) / ROT_DIM)
    qat, qbt, kr, vte, qn, kn = _qkprep(projt, pos_row, invf[:, None], p["diff_qnorm_g"][:, None],
                                p["diff_knorm_g"][:, None], _tile(s, 512), row_of["dq"],
                                row_of["dk"], row_of["dv"])
    tq = _tile(s, 512)
    o_diff = _diffattn(qat, qbt, kr, vte, qn, kn, p["diff_lq1"][None, :], p["diff_lk1"][None, :],
                       p["diff_lq2"][None, :], p["diff_lk2"][None, :],
                       p["diff_subln_g"][:, None], lambda_init, tq, tq)

    x1, hf, hfp = _mergeout(o_gla, o_diff, proj, x, p["w_branch_gla"].astype(BF16),
                       p["w_branch_diff"].astype(BF16), p["w_out"].astype(BF16), gt_a,
                       p["norm2_g"][None, :], sc_f, sh_f, _tile(s, 512),
                       col_of["mg"], col_of["md"])

    e = p["w_router"].shape[1]
    idx, wts, rnk, cnt = _route(hf, p["w_router"].T, p["router_bias"][:, None], _tile(s, 512))

    counts = cnt[:, 0]
    pcounts = ((counts + MOE_ROWS - 1) // MOE_ROWS) * MOE_ROWS
    pend = jnp.cumsum(pcounts)
    pstart = pend - pcounts
    pos = _positions(idx, rnk, pstart[:, None], _tile(s, 512))
    n_items = (s * TOP_K) // MOE_ROWS + e
    n_valid = (pend[-1] // MOE_ROWS).astype(jnp.int32)
    item_b = jnp.minimum(jnp.arange(n_items, dtype=jnp.int32), n_valid - 1)
    item_e = jnp.minimum(jnp.sum(pend[None, :] <= (item_b * MOE_ROWS)[:, None], axis=1),
                         e - 1).astype(jnp.int32)

    wn = SC_GATHER_WINDOW
    pos_w = pos.reshape(TOP_K, s // wn, wn).transpose(1, 0, 2).reshape(s // wn * TOP_K, wn)
    xs = _sc_scatter_rows(hfp, pos_w, n_items * MOE_ROWS)
    item_rows = jnp.clip(pstart[item_e] + counts[item_e] - item_b * MOE_ROWS, 0,
                         MOE_ROWS).astype(jnp.int32)
    prev_e = jnp.concatenate([jnp.full((1,), -1, jnp.int32), item_e[:-1]])
    item_first = ((jnp.arange(n_items) < n_valid) & (item_e != prev_e)).astype(jnp.int32)
    item_slot = ((jnp.cumsum(item_first) - 1) % 2).astype(jnp.int32)
    cand = jnp.where(pcounts > 0, jnp.arange(e, dtype=jnp.int32), e)
    following = jnp.concatenate([lax.cummin(cand[::-1])[::-1][1:], jnp.full((1,), e, jnp.int32)])
    item_next = jnp.where(following[item_e] < e, following[item_e], -1).astype(jnp.int32)
    ys = _moe(item_e, item_b, item_first, item_slot, item_next, item_rows, n_valid[None], xs,
              p["w_exp_gate"], p["w_exp_up"], p["w_exp_down"])
    gathered = _sc_gather_rows(ys, pos.reshape(1, TOP_K * s)).reshape(TOP_K, s, d // 2)
    return _combine(wts.T, hfp, x1, gt_f, p["w_sh_gate"].astype(BF16),
                    p["w_sh_up"].astype(BF16), p["w_sh_down"].astype(BF16), gathered,
                    _tile(s, 256))


_LAYER_PARAMS = ("w_ada", "b_ada", "norm1_g", "w_in", "gla_w_a2", "gla_b_a", "gla_onorm_g",
                 "diff_qnorm_g", "diff_knorm_g", "diff_lq1", "diff_lk1", "diff_lq2", "diff_lk2",
                 "diff_subln_g", "w_branch_gla", "w_branch_diff", "w_out", "norm2_g", "w_router",
                 "router_bias", "w_exp_gate", "w_exp_up", "w_exp_down", "w_sh_gate", "w_sh_up",
                 "w_sh_down")


def kernel(x, c, positions, w_ada, b_ada, norm1_g, w_in, gla_w_a2, gla_b_a, gla_onorm_g, diff_qnorm_g, diff_knorm_g, diff_lq1, diff_lk1, diff_lq2, diff_lk2, diff_subln_g, w_branch_gla, w_branch_diff, w_out, norm2_g, w_router, router_bias, w_exp_gate, w_exp_up, w_exp_down, w_sh_gate, w_sh_up, w_sh_down):
    stacked = dict(zip(_LAYER_PARAMS, (
        w_ada, b_ada, norm1_g, w_in, gla_w_a2, gla_b_a, gla_onorm_g, diff_qnorm_g, diff_knorm_g,
        diff_lq1, diff_lk1, diff_lq2, diff_lk2, diff_subln_g, w_branch_gla, w_branch_diff, w_out,
        norm2_g, w_router, router_bias, w_exp_gate, w_exp_up, w_exp_down, w_sh_gate, w_sh_up,
        w_sh_down)))
    b, s, d = x.shape
    assert b == 1, "single-sequence kernel"
    xl = x[0]
    c_col = c[0][:, None]
    pos_row = positions.astype(jnp.int32)
    for l in range(w_ada.shape[0]):
        xl = _layer(l, xl, c_col, pos_row, {k: v[l] for k, v in stacked.items()})
    return xl[None]
```

```python
import functools
import math

import jax
import jax.numpy as jnp
from jax import lax
from jax.experimental import pallas as pl
from jax.experimental.pallas import tpu as pltpu
from jax.experimental.pallas import tpu_sc as plsc

CHUNK = 64
EPS = 1e-6
GLA_HEADS = 4
GLA_DK = 128
GLA_DV = 256
GLA_TAU = 16.0
DIFF_HEADS = 8
DIFF_DH = 64
DIFF_DV = 2 * DIFF_DH
ROPE_THETA = 500000.0
ROT_DIM = DIFF_DH // 4
N_GROUPS = 8
TOPK_GROUPS = 4
TOP_K = 8
ROUTED_SCALE = 2.5

MOE_ROWS = 512
SC_GATHER_WINDOW = 64
VMEM_LIMIT = 56 * 1024 * 1024
TILE_PROJ = 1024
TILE_SEQ = 512
TILE_COMBINE = 256
NEG_BIG = -1e30
LOG2E = 1.4426950408889634
HIGHEST = lax.Precision.HIGHEST
F32 = jnp.float32
BF16 = jnp.bfloat16


def _cparams(sem):
    return pltpu.CompilerParams(dimension_semantics=sem, vmem_limit_bytes=VMEM_LIMIT)


def _nt_dot(a, b, precision=None):
    return lax.dot_general(a, b, (((1,), (1,)), ((), ())), precision=precision,
                           preferred_element_type=F32)


def _pack_halves(x):
    n = x.shape[1] // 2
    lo = pltpu.bitcast(x[:, :n].astype(BF16).astype(F32), jnp.uint32) >> 16
    hi = pltpu.bitcast(x[:, n:].astype(BF16).astype(F32), jnp.uint32) & jnp.uint32(0xFFFF0000)
    return lo | hi


def _unpack_halves(w):
    return (pltpu.bitcast(w << 16, F32), pltpu.bitcast(w & jnp.uint32(0xFFFF0000), F32))


def _rms_mod(x, g, sc, sh):
    xn = x * lax.rsqrt(jnp.mean(x * x, axis=-1, keepdims=True) + EPS)
    return (xn * g) * (1.0 + sc) + sh


def _ada_kernel(c_ref, w_ref, b_ref, o_ref):
    c = c_ref[...]
    ca = c * jax.nn.sigmoid(c)
    o_ref[...] = jnp.sum(ca * w_ref[...], axis=0, keepdims=True) + b_ref[...]


def _ada(c_col, w, b):
    d, n = w.shape
    tn = min(1024, n)
    return pl.pallas_call(
        _ada_kernel,
        grid=(n // tn,),
        in_specs=[pl.BlockSpec((d, 1), lambda j: (0, 0)),
                  pl.BlockSpec((d, tn), lambda j: (0, j)),
                  pl.BlockSpec((1, tn), lambda j: (0, j))],
        out_specs=pl.BlockSpec((1, tn), lambda j: (0, j)),
        out_shape=jax.ShapeDtypeStruct((1, n), F32),
        compiler_params=_cparams(("arbitrary",)),
        name="ada",
    )(c_col, w, b)


def _inproj_kernel(x_ref, g_ref, sc_ref, sh_ref, w_ref, wga_ref, o_ref, ga_ref, h_scr):
    @pl.when(pl.program_id(1) == 0)
    def _():
        h = _rms_mod(x_ref[...], g_ref[...], sc_ref[...], sh_ref[...]).astype(BF16)
        h_scr[...] = h
        ga_ref[...] = jnp.dot(h, wga_ref[...], preferred_element_type=F32)

    o_ref[...] = jnp.dot(h_scr[...], w_ref[...], preferred_element_type=F32).astype(BF16)


def _inproj(x, g, sc, sh, w, wga, tm, tn):
    s, d = x.shape
    n = w.shape[1]
    vec = pl.BlockSpec((1, d), lambda i, j: (0, 0))
    return pl.pallas_call(
        _inproj_kernel,
        grid=(s // tm, n // tn),
        in_specs=[pl.BlockSpec((tm, d), lambda i, j: (i, 0)), vec, vec, vec,
                  pl.BlockSpec((d, tn), lambda i, j: (0, j)),
                  pl.BlockSpec((d, 128), lambda i, j: (0, 0))],
        out_specs=[pl.BlockSpec((tm, tn), lambda i, j: (i, j)),
                   pl.BlockSpec((tm, 128), lambda i, j: (i, 0))],
        out_shape=[jax.ShapeDtypeStruct((s, n), BF16), jax.ShapeDtypeStruct((s, 128), F32)],
        scratch_shapes=[pltpu.VMEM((tm, d), BF16)],
        compiler_params=_cparams(("arbitrary", "arbitrary")),
        name="inproj",
    )(x, g, sc, sh, w, wga)


def _inproj_t_kernel(x_ref, g_ref, sc_ref, sh_ref, wt_ref, o_ref, h_scr):
    @pl.when(pl.program_id(1) == 0)
    def _():
        h_scr[...] = _rms_mod(x_ref[...], g_ref[...], sc_ref[...], sh_ref[...]).astype(BF16)

    o_ref[...] = _nt_dot(wt_ref[...], h_scr[...]).astype(BF16)


def _inproj_t(x, g, sc, sh, wt, tm, tn):
    s, d = x.shape
    n = wt.shape[0]
    vec = pl.BlockSpec((1, d), lambda i, j: (0, 0))
    return pl.pallas_call(
        _inproj_t_kernel,
        grid=(s // tm, n // tn),
        in_specs=[pl.BlockSpec((tm, d), lambda i, j: (i, 0)), vec, vec, vec,
                  pl.BlockSpec((tn, d), lambda i, j: (j, 0))],
        out_specs=pl.BlockSpec((tn, tm), lambda i, j: (j, i)),
        out_shape=jax.ShapeDtypeStruct((n, s), BF16),
        scratch_shapes=[pltpu.VMEM((tm, d), BF16)],
        compiler_params=_cparams(("arbitrary", "arbitrary")),
        name="inproj_t",
    )(x, g, sc, sh, wt)


def _gla_kernel(q_ref, kt_ref, v_ref, gg_ref, ga_ref, wa2t_ref, ba_ref, on_ref, o_ref,
                state_ref, o_scr):
    tt = q_ref.shape[0]
    nchunk = tt // CHUNK

    @pl.when(pl.program_id(0) == 0)
    def _():
        state_ref[...] = jnp.zeros_like(state_ref)

    zt = _nt_dot(wa2t_ref[...], ga_ref[...], precision=HIGHEST) + ba_ref[...]
    lat = (jnp.minimum(zt, 0.0) - jnp.log1p(jnp.exp(-jnp.abs(zt)))) * (1.0 / GLA_TAU)
    row = lax.broadcasted_iota(jnp.int32, (tt, tt), 0)
    col = lax.broadcasted_iota(jnp.int32, (tt, tt), 1)
    same = (row // CHUNK) == (col // CHUNK)
    incl = jnp.where(same & (row <= col), 1.0, 0.0).astype(BF16)
    full = jnp.where(same, 1.0, 0.0).astype(BF16)
    lat_hi = lat.astype(BF16)
    lat_lo = (lat - lat_hi.astype(F32)).astype(BF16)
    cumt = (jnp.dot(lat_hi, incl, preferred_element_type=F32)
            + jnp.dot(lat_lo, incl, preferred_element_type=F32))
    tott = (jnp.dot(lat_hi, full, preferred_element_type=F32)
            + jnp.dot(lat_lo, full, preferred_element_type=F32))
    kdt = kt_ref[...].astype(F32) * jnp.exp(tott - cumt)
    dec = jnp.exp(tott)

    lane = lax.broadcasted_iota(jnp.int32, (GLA_DK, 2 * CHUNK), 1)
    upd = {}
    for c in range(nchunk):
        pair = (c // 2) * 2 * CHUNK
        if nchunk > 1:
            keep = (lane // CHUNK) == (c % 2)
        for h in range(GLA_HEADS):
            rows = slice(h * GLA_DK, (h + 1) * GLA_DK)
            vcols = slice(h * GLA_DV, (h + 1) * GLA_DV)
            if nchunk > 1:
                a = jnp.where(keep, kdt[rows, pair:pair + 2 * CHUNK], 0.0).astype(BF16)
                vp = v_ref[pair:pair + 2 * CHUNK, vcols]
            else:
                a = kdt[rows, :].astype(BF16)
                vp = v_ref[:, vcols]
            upd[c, h] = jnp.dot(a, vp, preferred_element_type=F32)

    for h in range(GLA_HEADS):
        rows = slice(h * GLA_DK, (h + 1) * GLA_DK)
        vcols = slice(h * GLA_DV, (h + 1) * GLA_DV)
        st = state_ref[h]
        states = []
        for c in range(nchunk):
            st = st * dec[rows, c * CHUNK:c * CHUNK + 1] + upd[c, h]
            states.append(st.astype(BF16))
        state_ref[h] = st
        for c in range(nchunk):
            o_scr[c * CHUNK:(c + 1) * CHUNK, vcols] = jnp.dot(
                q_ref[c * CHUNK:(c + 1) * CHUNK, rows], states[c], preferred_element_type=F32)

    for h in range(GLA_HEADS):
        vcols = slice(h * GLA_DV, (h + 1) * GLA_DV)
        o = o_scr[:, vcols] * (GLA_DK ** -0.5)
        o = o * lax.rsqrt(jnp.mean(o * o, axis=-1, keepdims=True) + EPS) * on_ref[...]
        g = gg_ref[:, vcols].astype(F32)
        o_ref[:, vcols] = (o * (g * jax.nn.sigmoid(g))).astype(BF16)


def _gla(proj, projt, ga, wa2t, ba_col, on_g, tt, col_q, col_v, col_g, row_k):
    s = proj.shape[0]
    qk = GLA_HEADS * GLA_DK
    vw = GLA_HEADS * GLA_DV
    return pl.pallas_call(
        _gla_kernel,
        grid=(s // tt,),
        in_specs=[pl.BlockSpec((tt, qk), lambda i: (i, col_q // qk)),
                  pl.BlockSpec((qk, tt), lambda i: (row_k // qk, i)),
                  pl.BlockSpec((tt, vw), lambda i: (i, col_v // vw)),
                  pl.BlockSpec((tt, vw), lambda i: (i, col_g // vw)),
                  pl.BlockSpec((tt, 128), lambda i: (i, 0)),
                  pl.BlockSpec((qk, 128), lambda i: (0, 0)),
                  pl.BlockSpec((qk, 1), lambda i: (0, 0)),
                  pl.BlockSpec((1, GLA_DV), lambda i: (0, 0))],
        out_specs=pl.BlockSpec((tt, vw), lambda i: (i, 0)),
        out_shape=jax.ShapeDtypeStruct((s, vw), BF16),
        scratch_shapes=[pltpu.VMEM((GLA_HEADS, GLA_DK, GLA_DV), F32),
                        pltpu.VMEM((tt, vw), F32)],
        compiler_params=_cparams(("arbitrary",)),
        name="gla",
    )(proj, projt, proj, proj, ga, wa2t, ba_col, on_g)


def _qknorm_rope_t(xt, g_col, cos, sin):
    n, tm = xt.shape
    x3 = xt.reshape(n // DIFF_DH, DIFF_DH, tm)
    r = lax.rsqrt(jnp.mean(x3 * x3, axis=1, keepdims=True) + EPS)
    y = x3 * r * g_col[None]
    half = ROT_DIM // 2
    y1, y2, rest = y[:, :half], y[:, half:ROT_DIM], y[:, ROT_DIM:]
    o1 = y1 * cos[None] - y2 * sin[None]
    o2 = y2 * cos[None] + y1 * sin[None]
    return jnp.concatenate([o1, o2, rest], axis=1)


def _seg_norms(x3):
    return jnp.sqrt(jnp.sum(x3 * x3, axis=1)).reshape(DIFF_HEADS, 2, x3.shape[2])


def _qkprep_kernel(qt_ref, kt_ref, vt_ref, pos_ref, invf_ref, qg_ref, kg_ref, qa_ref, qb_ref,
                   ko_ref, ve_ref, qn_ref, kn_ref):
    tm = qt_ref.shape[1]
    v3 = vt_ref[...].reshape(DIFF_HEADS, DIFF_DV, tm)
    ones = jnp.ones((DIFF_HEADS, ATT_SUM_ROWS, tm), BF16)
    ve_ref[...] = jnp.concatenate([v3, ones], axis=1).reshape(-1, tm)
    ang = pos_ref[...].astype(F32) * invf_ref[...]
    cos, sin = jnp.cos(ang), jnp.sin(ang)
    k3 = _qknorm_rope_t(kt_ref[...].astype(F32), kg_ref[...], cos, sin)
    ko_ref[...] = k3.reshape(-1, tm).T.astype(BF16)
    kn_ref[...] = _seg_norms(k3)
    q3 = _qknorm_rope_t(qt_ref[...].astype(F32), qg_ref[...], cos, sin) * (
        DIFF_DH ** -0.5 * LOG2E)
    qn_ref[...] = _seg_norms(q3)
    seg = lax.broadcasted_iota(jnp.int32, q3.shape, 0)
    qa_ref[...] = jnp.where(seg % 2 == 0, q3, 0.0).reshape(-1, tm).astype(BF16)
    qb_ref[...] = jnp.where(seg % 2 == 1, q3, 0.0).reshape(-1, tm).astype(BF16)


def _qkprep(projt, pos_row, invf_col, qg_col, kg_col, tm, row_q, row_k, row_v):
    s = projt.shape[1]
    n = DIFF_HEADS * 2 * DIFF_DH
    ne = DIFF_HEADS * (DIFF_DV + ATT_SUM_ROWS)
    col = pl.BlockSpec((DIFF_DH, 1), lambda i: (0, 0))
    return pl.pallas_call(
        _qkprep_kernel,
        grid=(s // tm,),
        in_specs=[pl.BlockSpec((n, tm), lambda i: (row_q // n, i)),
                  pl.BlockSpec((n, tm), lambda i: (row_k // n, i)),
                  pl.BlockSpec((n, tm), lambda i: (row_v // n, i)),
                  pl.BlockSpec((1, tm), lambda i: (0, i)),
                  pl.BlockSpec((ROT_DIM // 2, 1), lambda i: (0, 0)), col, col],
        out_specs=[pl.BlockSpec((n, tm), lambda i: (0, i)),
                   pl.BlockSpec((n, tm), lambda i: (0, i)),
                   pl.BlockSpec((tm, n), lambda i: (i, 0)),
                   pl.BlockSpec((ne, tm), lambda i: (0, i)),
                   pl.BlockSpec((DIFF_HEADS, 2, tm), lambda i: (0, 0, i)),
                   pl.BlockSpec((DIFF_HEADS, 2, tm), lambda i: (0, 0, i))],
        out_shape=[jax.ShapeDtypeStruct((n, s), BF16), jax.ShapeDtypeStruct((n, s), BF16),
                   jax.ShapeDtypeStruct((s, n), BF16), jax.ShapeDtypeStruct((ne, s), BF16),
                   jax.ShapeDtypeStruct((DIFF_HEADS, 2, s), F32),
                   jax.ShapeDtypeStruct((DIFF_HEADS, 2, s), F32)],
        compiler_params=_cparams(("arbitrary",)),
        name="qkprep",
    )(projt, projt, projt, pos_row, invf_col, qg_col, kg_col)


ATT_COLS = 256
ATT_LOOKAHEAD = {True: 2, False: 4}
ATT_BODY_TILES = {True: 8, False: 2}
ATT_SUM_ROWS = 16
ATT_BOUND_SLACK = 1.02
ATT_BOUND_LIMIT = 50.0


def _diffattn_kernel(qa_ref, qb_ref, k_ref, vt_ref, qn_ref, kn_ref, lq1_ref, lk1_ref, lq2_ref,
                     lk2_ref, sg_ref, o_ref, *scr, lambda_init, tk, cols):
    tq = qa_ref.shape[1]
    nblk = 2 * tq // cols
    m_scr, acc_scr = (scr[b * nblk:(b + 1) * nblk] for b in range(2))
    kmax_scr = scr[2 * nblk]
    s_scr = scr[2 * nblk + 1:]
    i = pl.program_id(1)

    @pl.when(i == 0)
    def _():
        kmax_scr[...] = jnp.max(kn_ref[...], axis=1, keepdims=True)

    bound = qn_ref[...] * kmax_scr[...] * ATT_BOUND_SLACK
    bound = jnp.concatenate([bound[0:1], bound[1:2]], axis=1)
    bounded = jnp.max(bound) < ATT_BOUND_LIMIT
    for c in range(nblk):
        m_scr[c][...] = jnp.where(bounded, bound[:, c * cols:(c + 1) * cols], NEG_BIG)
        acc_scr[c][...] = jnp.zeros_like(acc_scr[c])

    def scores(j, c):
        start = pl.multiple_of(j * tk, tk)
        q_ref = qa_ref if c * cols < tq else qb_ref
        off = (c * cols) % tq
        return jnp.dot(k_ref[pl.ds(start, tk), :], q_ref[:, off:off + cols],
                       preferred_element_type=F32)

    def steps(tiles, masked, next_tile, fixed):
        look = ATT_LOOKAHEAD[fixed]
        items = [(j, c) for j in tiles for c in range(nblk)]
        pending = []
        for n, (j, c) in enumerate(items):
            s = s_scr[n][...] if n < look else pending.pop(0)
            ahead = n + look
            if ahead < len(items):
                pending.append(scores(*items[ahead]))
            elif next_tile is not None:
                s_scr[ahead - len(items)][...] = scores(next_tile, ahead - len(items))
            start = pl.multiple_of(j * tk, tk)
            keys = tk
            if masked:
                keys = (c * cols) % tq + cols
                s = s[:keys]
                krow = lax.broadcasted_iota(jnp.int32, (keys, cols), 0)
                qcol = lax.broadcasted_iota(jnp.int32, (keys, cols), 1)
                qpos = i * tq + (c * cols) % tq + qcol
                s = jnp.where((start + krow) // CHUNK <= qpos // CHUNK, s, NEG_BIG)
            vj = vt_ref[:, pl.ds(start, keys)]
            if fixed:
                p = jnp.exp2(s - m_scr[c][...]).astype(BF16)
                acc_scr[c][...] += jnp.dot(vj, p, preferred_element_type=F32)
            else:
                m_prev = m_scr[c][...]
                m_new = jnp.maximum(m_prev, jnp.max(s, axis=0, keepdims=True))
                alpha = jnp.exp2(m_prev - m_new)
                p = jnp.exp2((s - m_new).astype(BF16))
                acc_scr[c][...] = alpha * acc_scr[c][...] + jnp.dot(
                    vj, p, preferred_element_type=F32)
                m_scr[c][...] = m_new

    def attend(fixed):
        for c in range(ATT_LOOKAHEAD[fixed]):
            s_scr[c][...] = scores(0, c)
        big = ATT_BODY_TILES[fixed]
        lax.fori_loop(0, i // big, lambda t, c: (steps(
            tuple(big * t + u for u in range(big)), False, big * t + big, fixed), c)[1], 0)
        size = big // 2
        while size >= 1:
            first = (i // (2 * size)) * (2 * size)

            @pl.when(i % (2 * size) >= size)
            def _(first=first, size=size):
                steps(tuple(first + u for u in range(size)), False, first + size, fixed)

            size //= 2
        steps((i,), True, None, fixed)

    @pl.when(bounded)
    def _():
        attend(True)

    @pl.when(jnp.logical_not(bounded))
    def _():
        attend(False)

    o = jnp.concatenate([acc_scr[c][:DIFF_DV] / acc_scr[c][DIFF_DV:DIFF_DV + 1]
                         for c in range(nblk)], axis=1)
    lam = (jnp.exp(jnp.sum(lq1_ref[...] * lk1_ref[...]))
           - jnp.exp(jnp.sum(lq2_ref[...] * lk2_ref[...])) + lambda_init)
    o = o[:, :tq] - lam * o[:, tq:]
    o = o * lax.rsqrt(jnp.mean(o * o, axis=0, keepdims=True) + EPS) * sg_ref[...]
    o_ref[...] = (o * (1.0 - lambda_init)).T.astype(BF16)


def _diffattn(qat, qbt, kr, vte, qn, kn, lq1, lk1, lq2, lk2, sg_col, lambda_init, tq, tk):
    s = kr.shape[0]
    hd = 2 * DIFF_DH
    vec = pl.BlockSpec((1, DIFF_DH), lambda h, i: (0, 0))
    cols = min(ATT_COLS, tq)
    nblk = 2 * tq // cols
    kern = functools.partial(_diffattn_kernel, lambda_init=lambda_init, tk=tk, cols=cols)
    dve = DIFF_DV + ATT_SUM_ROWS
    assert tq == tk and nblk >= max(ATT_LOOKAHEAD.values())
    scratch = ([pltpu.VMEM((1, cols), F32)] * nblk + [pltpu.VMEM((dve, cols), F32)] * nblk
               + [pltpu.VMEM((2, 1), F32)]
               + [pltpu.VMEM((tk, cols), F32)] * max(ATT_LOOKAHEAD.values()))
    return pl.pallas_call(
        kern,
        grid=(DIFF_HEADS, s // tq),
        in_specs=[pl.BlockSpec((hd, tq), lambda h, i: (h, i)),
                  pl.BlockSpec((hd, tq), lambda h, i: (h, i)),
                  pl.BlockSpec((s, hd), lambda h, i: (0, h)),
                  pl.BlockSpec((dve, s), lambda h, i: (h, 0)),
                  pl.BlockSpec((None, 2, tq), lambda h, i: (h, 0, i)),
                  pl.BlockSpec((None, 2, s), lambda h, i: (h, 0, 0)),
                  vec, vec, vec, vec,
                  pl.BlockSpec((DIFF_DV, 1), lambda h, i: (0, 0))],
        out_specs=pl.BlockSpec((tq, DIFF_DV), lambda h, i: (i, h)),
        out_shape=jax.ShapeDtypeStruct((s, DIFF_HEADS * DIFF_DV), BF16),
        scratch_shapes=scratch,
        compiler_params=_cparams(("arbitrary", "arbitrary")),
        name="diffattn",
    )(qat, qbt, kr, vte, qn, kn, lq1, lk1, lq2, lk2, sg_col)


def _mergeout_kernel(og_ref, od_ref, mg_ref, md_ref, x_ref, wbg_ref, wbd_ref, wo_ref, gt_ref,
                     g2_ref, sc_ref, sh_ref, x1_ref, hf_ref, hfp_ref):
    bg = jnp.dot(og_ref[...], wbg_ref[...], preferred_element_type=F32)
    bd = jnp.dot(od_ref[...], wbd_ref[...], preferred_element_type=F32)
    merged = (jax.nn.sigmoid(mg_ref[...].astype(F32)) * bg
              + jax.nn.sigmoid(md_ref[...].astype(F32)) * bd)
    x1 = x_ref[...] + gt_ref[...] * jnp.dot(merged.astype(BF16), wo_ref[...],
                                             preferred_element_type=F32)
    x1_ref[...] = x1
    hf = _rms_mod(x1, g2_ref[...], sc_ref[...], sh_ref[...])
    hf_ref[...] = hf
    hfp_ref[...] = _pack_halves(hf)


def _mergeout(og, od, proj, x, wbg, wbd, wo, gt, g2, sc, sh, tm, col_mg, col_md):
    s, d = x.shape
    vec = pl.BlockSpec((1, d), lambda i: (0, 0))
    wspec = pl.BlockSpec((d, d), lambda i: (0, 0))
    row = pl.BlockSpec((tm, d), lambda i: (i, 0))
    return pl.pallas_call(
        _mergeout_kernel,
        grid=(s // tm,),
        in_specs=[row, row,
                  pl.BlockSpec((tm, d), lambda i: (i, col_mg // d)),
                  pl.BlockSpec((tm, d), lambda i: (i, col_md // d)),
                  row, wspec, wspec, wspec, vec, vec, vec, vec],
        out_specs=[row, row, pl.BlockSpec((tm, d // 2), lambda i: (i, 0))],
        out_shape=[jax.ShapeDtypeStruct((s, d), F32), jax.ShapeDtypeStruct((s, d), F32),
                   jax.ShapeDtypeStruct((s, d // 2), jnp.uint32)],
        compiler_params=_cparams(("arbitrary",)),
        name="mergeout",
    )(og, od, proj, proj, x, wbg, wbd, wo, gt, g2, sc, sh)


def _route_kernel(hf_ref, wrt_ref, bias_ref, idx_ref, wts_ref, rnk_ref, cnt_ref, run_scr):
    tr = hf_ref.shape[0]
    e = wrt_ref.shape[0]
    gsz = e // N_GROUPS

    @pl.when(pl.program_id(0) == 0)
    def _():
        run_scr[...] = jnp.zeros_like(run_scr)

    logits = _nt_dot(wrt_ref[...], hf_ref[...], precision=HIGHEST)
    scores = jax.nn.sigmoid(logits)
    biased = scores + bias_ref[...]
    g3 = biased.reshape(N_GROUPS, gsz, tr)
    m1 = jnp.max(g3, axis=1, keepdims=True)
    n_top = jnp.sum(jnp.where(g3 == m1, 1.0, 0.0), axis=1, keepdims=True)
    m2 = jnp.max(jnp.where(g3 < m1, g3, -jnp.inf), axis=1, keepdims=True)
    gs = (m1 + jnp.where(n_top >= 2.0, m1, m2)).reshape(N_GROUPS, tr)
    gi = lax.broadcasted_iota(jnp.int32, (N_GROUPS, tr), 0)
    beaten = jnp.zeros((N_GROUPS, tr), F32)
    for g in range(N_GROUPS):
        other = gs[g:g + 1, :]
        beaten = beaten + jnp.where((other > gs) | ((other == gs) & (g < gi)), 1.0, 0.0)
    gsel = (beaten < float(TOPK_GROUPS)).reshape(N_GROUPS, 1, tr)
    masked = jnp.where(gsel, g3, -jnp.inf).reshape(e, tr)

    ids = lax.broadcasted_iota(jnp.int32, (e, tr), 0)
    chosen = jnp.zeros((e, tr), F32)
    sel_idx, sel_score = [], []
    for _ in range(TOP_K):
        mx = jnp.max(masked, axis=0, keepdims=True)
        ix = jnp.min(jnp.where(masked == mx, ids, e), axis=0, keepdims=True)
        hit = ids == ix
        sel_idx.append(ix)
        sel_score.append(jnp.sum(jnp.where(hit, scores, 0.0), axis=0, keepdims=True))
        chosen = jnp.where(hit, 1.0, chosen)
        masked = jnp.where(hit, -jnp.inf, masked)
    idx = jnp.concatenate(sel_idx, axis=0)
    sc = jnp.concatenate(sel_score, axis=0)
    idx_ref[...] = idx
    wts_ref[...] = sc / jnp.sum(sc, axis=0, keepdims=True) * ROUTED_SCALE

    row = lax.broadcasted_iota(jnp.int32, (tr, tr), 0)
    col = lax.broadcasted_iota(jnp.int32, (tr, tr), 1)
    before = jnp.where(row < col, 1.0, 0.0).astype(BF16)
    prior = jnp.dot(chosen.astype(BF16), before, preferred_element_type=F32) + run_scr[:, 0:1]
    rnk_ref[...] = jnp.concatenate(
        [jnp.sum(jnp.where(ids == sel_idx[k], prior, 0.0), axis=0, keepdims=True)
         for k in range(TOP_K)], axis=0).astype(jnp.int32)
    run_scr[...] = run_scr[...] + jnp.sum(chosen, axis=1, keepdims=True)
    cnt_ref[...] = run_scr[...].astype(jnp.int32)


def _route(hf, wrt, bias_col, tr):
    s, d = hf.shape
    e = wrt.shape[0]
    tok = pl.BlockSpec((TOP_K, tr), lambda i: (0, i))
    return pl.pallas_call(
        _route_kernel,
        grid=(s // tr,),
        in_specs=[pl.BlockSpec((tr, d), lambda i: (i, 0)),
                  pl.BlockSpec((e, d), lambda i: (0, 0)),
                  pl.BlockSpec((e, 1), lambda i: (0, 0))],
        out_specs=[tok, tok, tok, pl.BlockSpec((e, 128), lambda i: (0, 0))],
        out_shape=[jax.ShapeDtypeStruct((TOP_K, s), jnp.int32),
                   jax.ShapeDtypeStruct((TOP_K, s), F32),
                   jax.ShapeDtypeStruct((TOP_K, s), jnp.int32),
                   jax.ShapeDtypeStruct((e, 128), jnp.int32)],
        scratch_shapes=[pltpu.VMEM((e, 128), F32)],
        compiler_params=_cparams(("arbitrary",)),
        name="route",
    )(hf, wrt, bias_col)


def _positions_kernel(idx_ref, rnk_ref, pstart_ref, pos_ref):
    e = pstart_ref.shape[0]
    ts = idx_ref.shape[1]
    ids = lax.broadcasted_iota(jnp.int32, (e, ts), 0)
    idx = idx_ref[...]
    pos_ref[...] = rnk_ref[...] + jnp.concatenate(
        [jnp.sum(jnp.where(ids == idx[k:k + 1, :], pstart_ref[...], 0), axis=0, keepdims=True)
         for k in range(TOP_K)], axis=0)


def _positions(idx, rnk, pstart_col, ts):
    s = idx.shape[1]
    e = pstart_col.shape[0]
    tok = pl.BlockSpec((TOP_K, ts), lambda i: (0, i))
    return pl.pallas_call(
        _positions_kernel,
        grid=(s // ts,),
        in_specs=[tok, tok, pl.BlockSpec((e, 1), lambda i: (0, 0))],
        out_specs=tok,
        out_shape=jax.ShapeDtypeStruct((TOP_K, s), jnp.int32),
        compiler_params=_cparams(("arbitrary",)),
        name="positions",
    )(idx, rnk, pstart_col)


def _swiglu_packed(xp, wg, wu, wd):
    lo, hi = _unpack_halves(xp)
    lo, hi = lo.astype(BF16), hi.astype(BF16)
    n = lo.shape[1]
    g = (jnp.dot(lo, wg[:n], preferred_element_type=F32)
         + jnp.dot(hi, wg[n:], preferred_element_type=F32))
    u = (jnp.dot(lo, wu[:n], preferred_element_type=F32)
         + jnp.dot(hi, wu[n:], preferred_element_type=F32))
    h = (g * jax.nn.sigmoid(g)) * u
    return jnp.dot(h.astype(BF16), wd[...], preferred_element_type=F32)


def _moe_kernel(ie_ref, ib_ref, first_ref, slot_ref, ne_ref, rows_ref, nv_ref, xs_ref, wg_hbm,
                wu_hbm, wd_hbm, ys_ref, wg_f, wu_f, wd_f, sem):
    del ib_ref
    i = pl.program_id(0)

    def fetch(e, slot):
        return [pltpu.make_async_copy(src.at[e], dst.at[slot], sem.at[slot, n])
                for n, (src, dst) in enumerate(((wg_hbm, wg_f), (wu_hbm, wu_f), (wd_hbm, wd_f)))]

    @pl.when(i == 0)
    def _():
        for cp in fetch(ie_ref[0], 0):
            cp.start()

    @pl.when(i < nv_ref[0])
    def _():
        for slot in range(2):
            @pl.when((first_ref[i] == 1) & (slot_ref[i] == slot))
            def _():
                for cp in fetch(ie_ref[i], slot):
                    cp.wait()

                @pl.when(ne_ref[i] >= 0)
                def _():
                    for cp in fetch(ne_ref[i], 1 - slot):
                        cp.start()

        slot = slot_ref[i]
        row = lax.broadcasted_iota(jnp.int32, xs_ref.shape, 0)
        xp = jnp.where(row < rows_ref[i], xs_ref[...], jnp.uint32(0))
        ys_ref[...] = _pack_halves(_swiglu_packed(
            xp, wg_f[slot].astype(BF16), wu_f[slot].astype(BF16), wd_f[slot].astype(BF16)))


def _moe(item_e, item_b, item_first, item_slot, item_next, item_rows, n_valid, xs, wg, wu, wd):
    m_pad, dh = xs.shape
    _, d, f = wg.shape
    n_items = item_e.shape[0]
    blk = lambda i, ie, ib, fi, sl, ne, nr, nv: (ib[i], 0)
    hbm = pl.BlockSpec(memory_space=pl.ANY)
    return pl.pallas_call(
        _moe_kernel,
        grid_spec=pltpu.PrefetchScalarGridSpec(
            num_scalar_prefetch=7,
            grid=(n_items,),
            in_specs=[pl.BlockSpec((MOE_ROWS, dh), blk), hbm, hbm, hbm],
            out_specs=pl.BlockSpec((MOE_ROWS, dh), blk),
            scratch_shapes=[pltpu.VMEM((2, d, f), F32), pltpu.VMEM((2, d, f), F32),
                            pltpu.VMEM((2, f, d), F32),
                            pltpu.SemaphoreType.DMA((2, 3))],
        ),
        out_shape=jax.ShapeDtypeStruct((m_pad, dh), jnp.uint32),
        compiler_params=_cparams(("arbitrary",)),
        name="moe",
    )(item_e, item_b, item_first, item_slot, item_next, item_rows, n_valid, xs, wg, wu, wd)


def _sc_gather_rows(table, idx_row):
    m = idx_row.shape[1]
    w = table.shape[1]
    idx_row = idx_row.reshape(m // SC_GATHER_WINDOW, SC_GATHER_WINDOW)
    mesh = plsc.VectorSubcoreMesh(core_axis_name="c", subcore_axis_name="s")

    @functools.partial(pl.kernel, mesh=mesh,
                       out_type=jax.ShapeDtypeStruct((m, w), table.dtype))
    def gather(table_hbm, idx_hbm, out_hbm):
        def body(idx_vmem, out_vmem):
            pltpu.sync_copy(table_hbm.at[idx_vmem.at[0]], out_vmem)

        pltpu.emit_pipeline(
            body,
            grid=(m // SC_GATHER_WINDOW,),
            in_specs=[pl.BlockSpec((1, SC_GATHER_WINDOW), lambda i: (i, 0))],
            out_specs=[pl.BlockSpec((SC_GATHER_WINDOW, w), lambda i: (i, 0))],
            core_axis_name=("c", "s"),
            dimension_semantics=(pltpu.PARALLEL,),
        )(idx_hbm, out_hbm)

    return gather(table, idx_row)


def _sc_scatter_rows(rows, idx_blocks, m_out):
    s, w = rows.shape
    mesh = plsc.VectorSubcoreMesh(core_axis_name="c", subcore_axis_name="s")

    @functools.partial(pl.kernel, mesh=mesh,
                       out_type=jax.ShapeDtypeStruct((m_out, w), rows.dtype))
    def scatter(rows_hbm, idx_hbm, out_hbm):
        def body(rows_vmem, idx_vmem):
            for k in range(TOP_K):
                pltpu.sync_copy(rows_vmem, out_hbm.at[idx_vmem.at[k]])

        pltpu.emit_pipeline(
            body,
            grid=(s // SC_GATHER_WINDOW,),
            in_specs=[pl.BlockSpec((SC_GATHER_WINDOW, w), lambda i: (i, 0)),
                      pl.BlockSpec((TOP_K, SC_GATHER_WINDOW), lambda i: (i, 0))],
            out_specs=[],
            core_axis_name=("c", "s"),
            dimension_semantics=(pltpu.PARALLEL,),
        )(rows_hbm, idx_hbm)

    return scatter(rows, idx_blocks)


def _combine_kernel(wt_ref, hf_ref, x1_ref, gt_ref, sg_ref, su_ref, sd_ref, g_ref, o_ref):
    tc = x1_ref.shape[0]
    y = _swiglu_packed(hf_ref[...], sg_ref, su_ref, sd_ref)
    wt = wt_ref[...]
    n = g_ref.shape[2]
    r_lo = jnp.zeros((tc, n), F32)
    r_hi = jnp.zeros((tc, n), F32)
    for k in range(TOP_K):
        lo, hi = _unpack_halves(g_ref[k])
        r_lo = r_lo + lo * wt[:, k:k + 1]
        r_hi = r_hi + hi * wt[:, k:k + 1]
    y = y + jnp.concatenate([r_lo, r_hi], axis=1)
    o_ref[...] = x1_ref[...] + gt_ref[...] * y


def _combine(wts_t, hfp, x1, gt, sg, su, sd, gathered, tc):
    s, d = x1.shape
    f = sg.shape[1]
    row = pl.BlockSpec((tc, d), lambda i: (i, 0))
    return pl.pallas_call(
        _combine_kernel,
        grid=(s // tc,),
        in_specs=[pl.BlockSpec((tc, TOP_K), lambda i: (i, 0)),
                  pl.BlockSpec((tc, d // 2), lambda i: (i, 0)), row,
                  pl.BlockSpec((1, d), lambda i: (0, 0)),
                  pl.BlockSpec((d, f), lambda i: (0, 0)),
                  pl.BlockSpec((d, f), lambda i: (0, 0)),
                  pl.BlockSpec((f, d), lambda i: (0, 0)),
                  pl.BlockSpec((TOP_K, tc, d // 2), lambda i: (0, i, 0))],
        out_specs=row,
        out_shape=jax.ShapeDtypeStruct((s, d), F32),
        compiler_params=_cparams(("arbitrary",)),
        name="combine",
    )(wts_t, hfp, x1, gt, sg, su, sd, gathered)


def _tile(n, want):
    t = min(n, want)
    assert n % t == 0, (n, t)
    return t


def _layer(l, x, c_col, pos_row, p):
    s, d = x.shape
    lambda_init = 0.8 - 0.6 * math.exp(-0.3 * l)
    gqk, gv = GLA_HEADS * GLA_DK, GLA_HEADS * GLA_DV
    dqk, dvw = DIFF_HEADS * 2 * DIFF_DH, DIFF_HEADS * DIFF_DV
    lowrank = p["gla_w_a2"].shape[0]

    mod = _ada(c_col, p["w_ada"], p["b_ada"][None, :])
    sh_a, sc_a, gt_a, sh_f, sc_f, gt_f = [mod[:, j * d:(j + 1) * d] for j in range(6)]

    w_in = p["w_in"]
    o = 0
    cols = {}
    for name, wdt in (("gq", gqk), ("gk", gqk), ("gv", gv), ("ga", lowrank), ("gg", gv),
                      ("dq", dqk), ("dk", dqk), ("dv", dvw), ("mg", d), ("md", d)):
        cols[name] = w_in[:, o:o + wdt]
        o += wdt
    row_names = ("gv", "gg", "mg", "md", "gq")
    w_row = jnp.concatenate([cols[n] for n in row_names], axis=1).astype(BF16)
    col_of, o = {}, 0
    for n in row_names:
        col_of[n] = o
        o += cols[n].shape[1]
    t_names = ("dq", "dk", "dv", "gk")
    w_t = jnp.concatenate([cols[n] for n in t_names], axis=1).T.astype(BF16)
    row_of, o = {}, 0
    for n in t_names:
        row_of[n] = o
        o += cols[n].shape[1]
    w_ga = jnp.pad(cols["ga"], ((0, 0), (0, 128 - lowrank))).astype(BF16)

    g1 = p["norm1_g"][None, :]
    tm = _tile(s, TILE_PROJ)
    ts = _tile(s, TILE_SEQ)
    proj, ga = _inproj(x, g1, sc_a, sh_a, w_row, w_ga, tm, w_row.shape[1] // 3)
    projt = _inproj_t(x, g1, sc_a, sh_a, w_t, tm, w_t.shape[0] // 2)

    wa2t = jnp.pad(p["gla_w_a2"].T, ((0, 0), (0, 128 - lowrank)))
    o_gla = _gla(proj, projt, ga, wa2t, p["gla_b_a"][:, None], p["gla_onorm_g"][None, :],
                 ts, col_of["gq"], col_of["gv"], col_of["gg"], row_of["gk"])

    invf = ROPE_THETA ** (-jnp.arange(0, ROT_DIM, 2, dtype=F32) / ROT_DIM)
    qat, qbt, kr, vte, qn, kn = _qkprep(projt, pos_row, invf[:, None], p["diff_qnorm_g"][:, None],
                                p["diff_knorm_g"][:, None], ts, row_of["dq"],
                                row_of["dk"], row_of["dv"])
    tq = ts
    o_diff = _diffattn(qat, qbt, kr, vte, qn, kn, p["diff_lq1"][None, :], p["diff_lk1"][None, :],
                       p["diff_lq2"][None, :], p["diff_lk2"][None, :],
                       p["diff_subln_g"][:, None], lambda_init, tq, tq)

    x1, hf, hfp = _mergeout(o_gla, o_diff, proj, x, p["w_branch_gla"].astype(BF16),
                       p["w_branch_diff"].astype(BF16), p["w_out"].astype(BF16), gt_a,
                       p["norm2_g"][None, :], sc_f, sh_f, ts,
                       col_of["mg"], col_of["md"])

    e = p["w_router"].shape[1]
    idx, wts, rnk, cnt = _route(hf, p["w_router"].T, p["router_bias"][:, None], ts)

    counts = cnt[:, 0]
    pcounts = ((counts + MOE_ROWS - 1) // MOE_ROWS) * MOE_ROWS
    pend = jnp.cumsum(pcounts)
    pstart = pend - pcounts
    pos = _positions(idx, rnk, pstart[:, None], ts)
    n_items = (s * TOP_K) // MOE_ROWS + e
    n_valid = (pend[-1] // MOE_ROWS).astype(jnp.int32)
    item_b = jnp.minimum(jnp.arange(n_items, dtype=jnp.int32), n_valid - 1)
    item_e = jnp.minimum(jnp.sum(pend[None, :] <= (item_b * MOE_ROWS)[:, None], axis=1),
                         e - 1).astype(jnp.int32)

    wn = SC_GATHER_WINDOW
    pos_w = pos.reshape(TOP_K, s // wn, wn).transpose(1, 0, 2).reshape(s // wn * TOP_K, wn)
    xs = _sc_scatter_rows(hfp, pos_w, n_items * MOE_ROWS)
    item_rows = jnp.clip(pstart[item_e] + counts[item_e] - item_b * MOE_ROWS, 0,
                         MOE_ROWS).astype(jnp.int32)
    prev_e = jnp.concatenate([jnp.full((1,), -1, jnp.int32), item_e[:-1]])
    item_first = ((jnp.arange(n_items) < n_valid) & (item_e != prev_e)).astype(jnp.int32)
    item_slot = ((jnp.cumsum(item_first) - 1) % 2).astype(jnp.int32)
    cand = jnp.where(pcounts > 0, jnp.arange(e, dtype=jnp.int32), e)
    following = jnp.concatenate([lax.cummin(cand[::-1])[::-1][1:], jnp.full((1,), e, jnp.int32)])
    item_next = jnp.where(following[item_e] < e, following[item_e], -1).astype(jnp.int32)
    ys = _moe(item_e, item_b, item_first, item_slot, item_next, item_rows, n_valid[None], xs,
              p["w_exp_gate"], p["w_exp_up"], p["w_exp_down"])
    gathered = _sc_gather_rows(ys, pos.reshape(1, TOP_K * s)).reshape(TOP_K, s, d // 2)
    return _combine(wts.T, hfp, x1, gt_f, p["w_sh_gate"].astype(BF16),
                    p["w_sh_up"].astype(BF16), p["w_sh_down"].astype(BF16), gathered,
                    _tile(s, TILE_COMBINE))


_LAYER_PARAMS = ("w_ada", "b_ada", "norm1_g", "w_in", "gla_w_a2", "gla_b_a", "gla_onorm_g",
                 "diff_qnorm_g", "diff_knorm_g", "diff_lq1", "diff_lk1", "diff_lq2", "diff_lk2",
                 "diff_subln_g", "w_branch_gla", "w_branch_diff", "w_out", "norm2_g", "w_router",
                 "router_bias", "w_exp_gate", "w_exp_up", "w_exp_down", "w_sh_gate", "w_sh_up",
                 "w_sh_down")


def kernel(x, c, positions, w_ada, b_ada, norm1_g, w_in, gla_w_a2, gla_b_a, gla_onorm_g, diff_qnorm_g, diff_knorm_g, diff_lq1, diff_lk1, diff_lq2, diff_lk2, diff_subln_g, w_branch_gla, w_branch_diff, w_out, norm2_g, w_router, router_bias, w_exp_gate, w_exp_up, w_exp_down, w_sh_gate, w_sh_up, w_sh_down):
    stacked = dict(zip(_LAYER_PARAMS, (
        w_ada, b_ada, norm1_g, w_in, gla_w_a2, gla_b_a, gla_onorm_g, diff_qnorm_g, diff_knorm_g,
        diff_lq1, diff_lk1, diff_lq2, diff_lk2, diff_subln_g, w_branch_gla, w_branch_diff, w_out,
        norm2_g, w_router, router_bias, w_exp_gate, w_exp_up, w_exp_down, w_sh_gate, w_sh_up,
        w_sh_down)))
    b, s, d = x.shape
    assert b == 1, "single-sequence kernel"
    xl = x[0]
    c_col = c[0][:, None]
    pos_row = positions.astype(jnp.int32)
    for l in range(w_ada.shape[0]):
        xl = _layer(l, xl, c_col, pos_row, {k: v[l] for k, v in stacked.items()})
    return xl[None]
```

```python
import functools
import math

import jax
import jax.numpy as jnp
from jax import lax
from jax.experimental import pallas as pl
from jax.experimental.pallas import tpu as pltpu
from jax.experimental.pallas import tpu_sc as plsc

CHUNK = 64
EPS = 1e-6
GLA_HEADS = 4
GLA_DK = 128
GLA_DV = 256
GLA_TAU = 16.0
DIFF_HEADS = 8
DIFF_DH = 64
DIFF_DV = 2 * DIFF_DH
ROPE_THETA = 500000.0
ROT_DIM = DIFF_DH // 4
N_GROUPS = 8
TOPK_GROUPS = 4
TOP_K = 8
ROUTED_SCALE = 2.5

MOE_ROWS = 640
SC_GATHER_WINDOW = 64
VMEM_LIMIT = 56 * 1024 * 1024
TILE_PROJ = 1024
TILE_SEQ = 512
TILE_COMBINE = 256
NEG_BIG = -1e30
LOG2E = 1.4426950408889634
HIGHEST = lax.Precision.HIGHEST
F32 = jnp.float32
BF16 = jnp.bfloat16


def _cparams(sem):
    return pltpu.CompilerParams(dimension_semantics=sem, vmem_limit_bytes=VMEM_LIMIT)


def _nt_dot(a, b, precision=None):
    return lax.dot_general(a, b, (((1,), (1,)), ((), ())), precision=precision,
                           preferred_element_type=F32)


def _pack_halves(x):
    n = x.shape[1] // 2
    lo = pltpu.bitcast(x[:, :n].astype(BF16).astype(F32), jnp.uint32) >> 16
    hi = pltpu.bitcast(x[:, n:].astype(BF16).astype(F32), jnp.uint32) & jnp.uint32(0xFFFF0000)
    return lo | hi


def _unpack_halves(w):
    return (pltpu.bitcast(w << 16, F32), pltpu.bitcast(w & jnp.uint32(0xFFFF0000), F32))


def _rms_mod(x, g, sc, sh):
    xn = x * lax.rsqrt(jnp.mean(x * x, axis=-1, keepdims=True) + EPS)
    return (xn * g) * (1.0 + sc) + sh


def _ada_kernel(c_ref, w_ref, b_ref, o_ref):
    c = c_ref[...]
    ca = c * jax.nn.sigmoid(c)
    o_ref[...] = jnp.sum(ca * w_ref[...], axis=0, keepdims=True) + b_ref[...]


def _ada(c_col, w, b):
    d, n = w.shape
    tn = min(1024, n)
    return pl.pallas_call(
        _ada_kernel,
        grid=(n // tn,),
        in_specs=[pl.BlockSpec((d, 1), lambda j: (0, 0)),
                  pl.BlockSpec((d, tn), lambda j: (0, j)),
                  pl.BlockSpec((1, tn), lambda j: (0, j))],
        out_specs=pl.BlockSpec((1, tn), lambda j: (0, j)),
        out_shape=jax.ShapeDtypeStruct((1, n), F32),
        compiler_params=_cparams(("arbitrary",)),
        name="ada",
    )(c_col, w, b)


def _inproj_kernel(x_ref, g_ref, sc_ref, sh_ref, w_ref, wga_ref, o_ref, ga_ref, h_scr):
    @pl.when(pl.program_id(1) == 0)
    def _():
        h = _rms_mod(x_ref[...], g_ref[...], sc_ref[...], sh_ref[...]).astype(BF16)
        h_scr[...] = h
        ga_ref[...] = jnp.dot(h, wga_ref[...], preferred_element_type=F32)

    o_ref[...] = jnp.dot(h_scr[...], w_ref[...], preferred_element_type=F32).astype(BF16)


def _inproj(x, g, sc, sh, w, wga, tm, tn):
    s, d = x.shape
    n = w.shape[1]
    vec = pl.BlockSpec((1, d), lambda i, j: (0, 0))
    return pl.pallas_call(
        _inproj_kernel,
        grid=(s // tm, n // tn),
        in_specs=[pl.BlockSpec((tm, d), lambda i, j: (i, 0)), vec, vec, vec,
                  pl.BlockSpec((d, tn), lambda i, j: (0, j)),
                  pl.BlockSpec((d, 128), lambda i, j: (0, 0))],
        out_specs=[pl.BlockSpec((tm, tn), lambda i, j: (i, j)),
                   pl.BlockSpec((tm, 128), lambda i, j: (i, 0))],
        out_shape=[jax.ShapeDtypeStruct((s, n), BF16), jax.ShapeDtypeStruct((s, 128), F32)],
        scratch_shapes=[pltpu.VMEM((tm, d), BF16)],
        compiler_params=_cparams(("arbitrary", "arbitrary")),
        name="inproj",
    )(x, g, sc, sh, w, wga)


def _inproj_t_kernel(x_ref, g_ref, sc_ref, sh_ref, wt_ref, o_ref, h_scr):
    @pl.when(pl.program_id(1) == 0)
    def _():
        h_scr[...] = _rms_mod(x_ref[...], g_ref[...], sc_ref[...], sh_ref[...]).astype(BF16)

    o_ref[...] = _nt_dot(wt_ref[...], h_scr[...]).astype(BF16)


def _inproj_t(x, g, sc, sh, wt, tm, tn):
    s, d = x.shape
    n = wt.shape[0]
    vec = pl.BlockSpec((1, d), lambda i, j: (0, 0))
    return pl.pallas_call(
        _inproj_t_kernel,
        grid=(s // tm, n // tn),
        in_specs=[pl.BlockSpec((tm, d), lambda i, j: (i, 0)), vec, vec, vec,
                  pl.BlockSpec((tn, d), lambda i, j: (j, 0))],
        out_specs=pl.BlockSpec((tn, tm), lambda i, j: (j, i)),
        out_shape=jax.ShapeDtypeStruct((n, s), BF16),
        scratch_shapes=[pltpu.VMEM((tm, d), BF16)],
        compiler_params=_cparams(("arbitrary", "arbitrary")),
        name="inproj_t",
    )(x, g, sc, sh, wt)


def _gla_kernel(q_ref, kt_ref, v_ref, gg_ref, ga_ref, wa2t_ref, ba_ref, on_ref, o_ref,
                state_ref, o_scr):
    tt = q_ref.shape[0]
    nchunk = tt // CHUNK

    @pl.when(pl.program_id(0) == 0)
    def _():
        state_ref[...] = jnp.zeros_like(state_ref)

    zt = _nt_dot(wa2t_ref[...], ga_ref[...], precision=HIGHEST) + ba_ref[...]
    lat = (jnp.minimum(zt, 0.0) - jnp.log1p(jnp.exp(-jnp.abs(zt)))) * (1.0 / GLA_TAU)
    row = lax.broadcasted_iota(jnp.int32, (tt, tt), 0)
    col = lax.broadcasted_iota(jnp.int32, (tt, tt), 1)
    same = (row // CHUNK) == (col // CHUNK)
    incl = jnp.where(same & (row <= col), 1.0, 0.0).astype(BF16)
    full = jnp.where(same, 1.0, 0.0).astype(BF16)
    lat_hi = lat.astype(BF16)
    lat_lo = (lat - lat_hi.astype(F32)).astype(BF16)
    cumt = (jnp.dot(lat_hi, incl, preferred_element_type=F32)
            + jnp.dot(lat_lo, incl, preferred_element_type=F32))
    tott = (jnp.dot(lat_hi, full, preferred_element_type=F32)
            + jnp.dot(lat_lo, full, preferred_element_type=F32))
    kdt = kt_ref[...].astype(F32) * jnp.exp(tott - cumt)
    dec = jnp.exp(tott)

    lane = lax.broadcasted_iota(jnp.int32, (GLA_DK, 2 * CHUNK), 1)
    upd = {}
    for c in range(nchunk):
        pair = (c // 2) * 2 * CHUNK
        if nchunk > 1:
            keep = (lane // CHUNK) == (c % 2)
        for h in range(GLA_HEADS):
            rows = slice(h * GLA_DK, (h + 1) * GLA_DK)
            vcols = slice(h * GLA_DV, (h + 1) * GLA_DV)
            if nchunk > 1:
                a = jnp.where(keep, kdt[rows, pair:pair + 2 * CHUNK], 0.0).astype(BF16)
                vp = v_ref[pair:pair + 2 * CHUNK, vcols]
            else:
                a = kdt[rows, :].astype(BF16)
                vp = v_ref[:, vcols]
            upd[c, h] = jnp.dot(a, vp, preferred_element_type=F32)

    for h in range(GLA_HEADS):
        rows = slice(h * GLA_DK, (h + 1) * GLA_DK)
        vcols = slice(h * GLA_DV, (h + 1) * GLA_DV)
        st = state_ref[h]
        states = []
        for c in range(nchunk):
            st = st * dec[rows, c * CHUNK:c * CHUNK + 1] + upd[c, h]
            states.append(st.astype(BF16))
        state_ref[h] = st
        for c in range(nchunk):
            o_scr[c * CHUNK:(c + 1) * CHUNK, vcols] = jnp.dot(
                q_ref[c * CHUNK:(c + 1) * CHUNK, rows], states[c], preferred_element_type=F32)

    for h in range(GLA_HEADS):
        vcols = slice(h * GLA_DV, (h + 1) * GLA_DV)
        o = o_scr[:, vcols] * (GLA_DK ** -0.5)
        o = o * lax.rsqrt(jnp.mean(o * o, axis=-1, keepdims=True) + EPS) * on_ref[...]
        g = gg_ref[:, vcols].astype(F32)
        o_ref[:, vcols] = (o * (g * jax.nn.sigmoid(g))).astype(BF16)


def _gla(proj, projt, ga, wa2t, ba_col, on_g, tt, col_q, col_v, col_g, row_k):
    s = proj.shape[0]
    qk = GLA_HEADS * GLA_DK
    vw = GLA_HEADS * GLA_DV
    return pl.pallas_call(
        _gla_kernel,
        grid=(s // tt,),
        in_specs=[pl.BlockSpec((tt, qk), lambda i: (i, col_q // qk)),
                  pl.BlockSpec((qk, tt), lambda i: (row_k // qk, i)),
                  pl.BlockSpec((tt, vw), lambda i: (i, col_v // vw)),
                  pl.BlockSpec((tt, vw), lambda i: (i, col_g // vw)),
                  pl.BlockSpec((tt, 128), lambda i: (i, 0)),
                  pl.BlockSpec((qk, 128), lambda i: (0, 0)),
                  pl.BlockSpec((qk, 1), lambda i: (0, 0)),
                  pl.BlockSpec((1, GLA_DV), lambda i: (0, 0))],
        out_specs=pl.BlockSpec((tt, vw), lambda i: (i, 0)),
        out_shape=jax.ShapeDtypeStruct((s, vw), BF16),
        scratch_shapes=[pltpu.VMEM((GLA_HEADS, GLA_DK, GLA_DV), F32),
                        pltpu.VMEM((tt, vw), F32)],
        compiler_params=_cparams(("arbitrary",)),
        name="gla",
    )(proj, projt, proj, proj, ga, wa2t, ba_col, on_g)


def _qknorm_rope_t(xt, g_col, cos, sin):
    n, tm = xt.shape
    x3 = xt.reshape(n // DIFF_DH, DIFF_DH, tm)
    r = lax.rsqrt(jnp.mean(x3 * x3, axis=1, keepdims=True) + EPS)
    y = x3 * r * g_col[None]
    half = ROT_DIM // 2
    y1, y2, rest = y[:, :half], y[:, half:ROT_DIM], y[:, ROT_DIM:]
    o1 = y1 * cos[None] - y2 * sin[None]
    o2 = y2 * cos[None] + y1 * sin[None]
    return jnp.concatenate([o1, o2, rest], axis=1)


def _seg_norms(x3):
    return jnp.sqrt(jnp.sum(x3 * x3, axis=1)).reshape(DIFF_HEADS, 2, x3.shape[2])


def _qkprep_kernel(qt_ref, kt_ref, vt_ref, pos_ref, invf_ref, qg_ref, kg_ref, qa_ref, qb_ref,
                   ko_ref, ve_ref, qn_ref, kn_ref):
    tm = qt_ref.shape[1]
    v3 = vt_ref[...].reshape(DIFF_HEADS, DIFF_DV, tm)
    ones = jnp.ones((DIFF_HEADS, ATT_SUM_ROWS, tm), BF16)
    ve_ref[...] = jnp.concatenate([v3, ones], axis=1).reshape(-1, tm)
    ang = pos_ref[...].astype(F32) * invf_ref[...]
    cos, sin = jnp.cos(ang), jnp.sin(ang)
    k3 = _qknorm_rope_t(kt_ref[...].astype(F32), kg_ref[...], cos, sin)
    ko_ref[...] = k3.reshape(-1, tm).T.astype(BF16)
    kn_ref[...] = _seg_norms(k3)
    q3 = _qknorm_rope_t(qt_ref[...].astype(F32), qg_ref[...], cos, sin) * (
        DIFF_DH ** -0.5 * LOG2E)
    qn_ref[...] = _seg_norms(q3)
    seg = lax.broadcasted_iota(jnp.int32, q3.shape, 0)
    qa_ref[...] = jnp.where(seg % 2 == 0, q3, 0.0).reshape(-1, tm).astype(BF16)
    qb_ref[...] = jnp.where(seg % 2 == 1, q3, 0.0).reshape(-1, tm).astype(BF16)


def _qkprep(projt, pos_row, invf_col, qg_col, kg_col, tm, row_q, row_k, row_v):
    s = projt.shape[1]
    n = DIFF_HEADS * 2 * DIFF_DH
    ne = DIFF_HEADS * (DIFF_DV + ATT_SUM_ROWS)
    col = pl.BlockSpec((DIFF_DH, 1), lambda i: (0, 0))
    return pl.pallas_call(
        _qkprep_kernel,
        grid=(s // tm,),
        in_specs=[pl.BlockSpec((n, tm), lambda i: (row_q // n, i)),
                  pl.BlockSpec((n, tm), lambda i: (row_k // n, i)),
                  pl.BlockSpec((n, tm), lambda i: (row_v // n, i)),
                  pl.BlockSpec((1, tm), lambda i: (0, i)),
                  pl.BlockSpec((ROT_DIM // 2, 1), lambda i: (0, 0)), col, col],
        out_specs=[pl.BlockSpec((n, tm), lambda i: (0, i)),
                   pl.BlockSpec((n, tm), lambda i: (0, i)),
                   pl.BlockSpec((tm, n), lambda i: (i, 0)),
                   pl.BlockSpec((ne, tm), lambda i: (0, i)),
                   pl.BlockSpec((DIFF_HEADS, 2, tm), lambda i: (0, 0, i)),
                   pl.BlockSpec((DIFF_HEADS, 2, tm), lambda i: (0, 0, i))],
        out_shape=[jax.ShapeDtypeStruct((n, s), BF16), jax.ShapeDtypeStruct((n, s), BF16),
                   jax.ShapeDtypeStruct((s, n), BF16), jax.ShapeDtypeStruct((ne, s), BF16),
                   jax.ShapeDtypeStruct((DIFF_HEADS, 2, s), F32),
                   jax.ShapeDtypeStruct((DIFF_HEADS, 2, s), F32)],
        compiler_params=_cparams(("arbitrary",)),
        name="qkprep",
    )(projt, projt, projt, pos_row, invf_col, qg_col, kg_col)


ATT_COLS = 256
ATT_LOOKAHEAD = {True: 2, False: 4}
ATT_BODY_TILES = {True: 8, False: 2}
ATT_SUM_ROWS = 16
ATT_BOUND_SLACK = 1.02
ATT_BOUND_LIMIT = 50.0


def _diffattn_kernel(qa_ref, qb_ref, k_ref, vt_ref, qn_ref, kn_ref, lq1_ref, lk1_ref, lq2_ref,
                     lk2_ref, sg_ref, o_ref, *scr, lambda_init, tk, cols):
    tq = qa_ref.shape[1]
    nblk = 2 * tq // cols
    m_scr, acc_scr = (scr[b * nblk:(b + 1) * nblk] for b in range(2))
    kmax_scr = scr[2 * nblk]
    s_scr = scr[2 * nblk + 1:]
    i = pl.program_id(1)

    @pl.when(i == 0)
    def _():
        kmax_scr[...] = jnp.max(kn_ref[...], axis=1, keepdims=True)

    bound = qn_ref[...] * kmax_scr[...] * ATT_BOUND_SLACK
    bound = jnp.concatenate([bound[0:1], bound[1:2]], axis=1)
    bounded = jnp.max(bound) < ATT_BOUND_LIMIT
    for c in range(nblk):
        m_scr[c][...] = jnp.where(bounded, bound[:, c * cols:(c + 1) * cols], NEG_BIG)
        acc_scr[c][...] = jnp.zeros_like(acc_scr[c])

    def scores(j, c):
        start = pl.multiple_of(j * tk, tk)
        q_ref = qa_ref if c * cols < tq else qb_ref
        off = (c * cols) % tq
        return jnp.dot(k_ref[pl.ds(start, tk), :], q_ref[:, off:off + cols],
                       preferred_element_type=F32)

    def steps(tiles, masked, next_tile, fixed):
        look = ATT_LOOKAHEAD[fixed]
        items = [(j, c) for j in tiles for c in range(nblk)]
        pending = []
        for n, (j, c) in enumerate(items):
            s = s_scr[n][...] if n < look else pending.pop(0)
            ahead = n + look
            if ahead < len(items):
                pending.append(scores(*items[ahead]))
            elif next_tile is not None:
                s_scr[ahead - len(items)][...] = scores(next_tile, ahead - len(items))
            start = pl.multiple_of(j * tk, tk)
            keys = tk
            if masked:
                keys = (c * cols) % tq + cols
                s = s[:keys]
                krow = lax.broadcasted_iota(jnp.int32, (keys, cols), 0)
                qcol = lax.broadcasted_iota(jnp.int32, (keys, cols), 1)
                qpos = i * tq + (c * cols) % tq + qcol
                s = jnp.where((start + krow) // CHUNK <= qpos // CHUNK, s, NEG_BIG)
            vj = vt_ref[:, pl.ds(start, keys)]
            if fixed:
                p = jnp.exp2(s - m_scr[c][...]).astype(BF16)
                acc_scr[c][...] += jnp.dot(vj, p, preferred_element_type=F32)
            else:
                m_prev = m_scr[c][...]
                m_new = jnp.maximum(m_prev, jnp.max(s, axis=0, keepdims=True))
                alpha = jnp.exp2(m_prev - m_new)
                p = jnp.exp2((s - m_new).astype(BF16))
                acc_scr[c][...] = alpha * acc_scr[c][...] + jnp.dot(
                    vj, p, preferred_element_type=F32)
                m_scr[c][...] = m_new

    def attend(fixed):
        for c in range(ATT_LOOKAHEAD[fixed]):
            s_scr[c][...] = scores(0, c)
        big = ATT_BODY_TILES[fixed]
        lax.fori_loop(0, i // big, lambda t, c: (steps(
            tuple(big * t + u for u in range(big)), False, big * t + big, fixed), c)[1], 0)
        size = big // 2
        while size >= 1:
            first = (i // (2 * size)) * (2 * size)

            @pl.when(i % (2 * size) >= size)
            def _(first=first, size=size):
                steps(tuple(first + u for u in range(size)), False, first + size, fixed)

            size //= 2
        steps((i,), True, None, fixed)

    @pl.when(bounded)
    def _():
        attend(True)

    @pl.when(jnp.logical_not(bounded))
    def _():
        attend(False)

    o = jnp.concatenate([acc_scr[c][:DIFF_DV] / acc_scr[c][DIFF_DV:DIFF_DV + 1]
                         for c in range(nblk)], axis=1)
    lam = (jnp.exp(jnp.sum(lq1_ref[...] * lk1_ref[...]))
           - jnp.exp(jnp.sum(lq2_ref[...] * lk2_ref[...])) + lambda_init)
    o = o[:, :tq] - lam * o[:, tq:]
    o = o * lax.rsqrt(jnp.mean(o * o, axis=0, keepdims=True) + EPS) * sg_ref[...]
    o_ref[...] = (o * (1.0 - lambda_init)).T.astype(BF16)


def _diffattn(qat, qbt, kr, vte, qn, kn, lq1, lk1, lq2, lk2, sg_col, lambda_init, tq, tk):
    s = kr.shape[0]
    hd = 2 * DIFF_DH
    vec = pl.BlockSpec((1, DIFF_DH), lambda h, i: (0, 0))
    cols = min(ATT_COLS, tq)
    nblk = 2 * tq // cols
    kern = functools.partial(_diffattn_kernel, lambda_init=lambda_init, tk=tk, cols=cols)
    dve = DIFF_DV + ATT_SUM_ROWS
    assert tq == tk and nblk >= max(ATT_LOOKAHEAD.values())
    scratch = ([pltpu.VMEM((1, cols), F32)] * nblk + [pltpu.VMEM((dve, cols), F32)] * nblk
               + [pltpu.VMEM((2, 1), F32)]
               + [pltpu.VMEM((tk, cols), F32)] * max(ATT_LOOKAHEAD.values()))
    return pl.pallas_call(
        kern,
        grid=(DIFF_HEADS, s // tq),
        in_specs=[pl.BlockSpec((hd, tq), lambda h, i: (h, i)),
                  pl.BlockSpec((hd, tq), lambda h, i: (h, i)),
                  pl.BlockSpec((s, hd), lambda h, i: (0, h)),
                  pl.BlockSpec((dve, s), lambda h, i: (h, 0)),
                  pl.BlockSpec((None, 2, tq), lambda h, i: (h, 0, i)),
                  pl.BlockSpec((None, 2, s), lambda h, i: (h, 0, 0)),
                  vec, vec, vec, vec,
                  pl.BlockSpec((DIFF_DV, 1), lambda h, i: (0, 0))],
        out_specs=pl.BlockSpec((tq, DIFF_DV), lambda h, i: (i, h)),
        out_shape=jax.ShapeDtypeStruct((s, DIFF_HEADS * DIFF_DV), BF16),
        scratch_shapes=scratch,
        compiler_params=_cparams(("arbitrary", "arbitrary")),
        name="diffattn",
    )(qat, qbt, kr, vte, qn, kn, lq1, lk1, lq2, lk2, sg_col)


def _mergeout_kernel(og_ref, od_ref, mg_ref, md_ref, x_ref, wbg_ref, wbd_ref, wo_ref, gt_ref,
                     g2_ref, sc_ref, sh_ref, x1_ref, hf_ref, hfp_ref):
    bg = jnp.dot(og_ref[...], wbg_ref[...], preferred_element_type=F32)
    bd = jnp.dot(od_ref[...], wbd_ref[...], preferred_element_type=F32)
    merged = (jax.nn.sigmoid(mg_ref[...].astype(F32)) * bg
              + jax.nn.sigmoid(md_ref[...].astype(F32)) * bd)
    x1 = x_ref[...] + gt_ref[...] * jnp.dot(merged.astype(BF16), wo_ref[...],
                                             preferred_element_type=F32)
    x1_ref[...] = x1
    hf = _rms_mod(x1, g2_ref[...], sc_ref[...], sh_ref[...])
    hf_ref[...] = hf
    hfp_ref[...] = _pack_halves(hf)


def _mergeout(og, od, proj, x, wbg, wbd, wo, gt, g2, sc, sh, tm, col_mg, col_md):
    s, d = x.shape
    vec = pl.BlockSpec((1, d), lambda i: (0, 0))
    wspec = pl.BlockSpec((d, d), lambda i: (0, 0))
    row = pl.BlockSpec((tm, d), lambda i: (i, 0))
    return pl.pallas_call(
        _mergeout_kernel,
        grid=(s // tm,),
        in_specs=[row, row,
                  pl.BlockSpec((tm, d), lambda i: (i, col_mg // d)),
                  pl.BlockSpec((tm, d), lambda i: (i, col_md // d)),
                  row, wspec, wspec, wspec, vec, vec, vec, vec],
        out_specs=[row, row, pl.BlockSpec((tm, d // 2), lambda i: (i, 0))],
        out_shape=[jax.ShapeDtypeStruct((s, d), F32), jax.ShapeDtypeStruct((s, d), F32),
                   jax.ShapeDtypeStruct((s, d // 2), jnp.uint32)],
        compiler_params=_cparams(("arbitrary",)),
        name="mergeout",
    )(og, od, proj, proj, x, wbg, wbd, wo, gt, g2, sc, sh)


def _route_kernel(hf_ref, wrt_ref, bias_ref, idx_ref, wts_ref, rnk_ref, cnt_ref, run_scr):
    tr = hf_ref.shape[0]
    e = wrt_ref.shape[0]
    gsz = e // N_GROUPS

    @pl.when(pl.program_id(0) == 0)
    def _():
        run_scr[...] = jnp.zeros_like(run_scr)

    logits = _nt_dot(wrt_ref[...], hf_ref[...], precision=HIGHEST)
    scores = jax.nn.sigmoid(logits)
    biased = scores + bias_ref[...]
    g3 = biased.reshape(N_GROUPS, gsz, tr)
    m1 = jnp.max(g3, axis=1, keepdims=True)
    n_top = jnp.sum(jnp.where(g3 == m1, 1.0, 0.0), axis=1, keepdims=True)
    m2 = jnp.max(jnp.where(g3 < m1, g3, -jnp.inf), axis=1, keepdims=True)
    gs = (m1 + jnp.where(n_top >= 2.0, m1, m2)).reshape(N_GROUPS, tr)
    gi = lax.broadcasted_iota(jnp.int32, (N_GROUPS, tr), 0)
    beaten = jnp.zeros((N_GROUPS, tr), F32)
    for g in range(N_GROUPS):
        other = gs[g:g + 1, :]
        beaten = beaten + jnp.where((other > gs) | ((other == gs) & (g < gi)), 1.0, 0.0)
    gsel = (beaten < float(TOPK_GROUPS)).reshape(N_GROUPS, 1, tr)
    masked = jnp.where(gsel, g3, -jnp.inf).reshape(e, tr)

    ids = lax.broadcasted_iota(jnp.int32, (e, tr), 0)
    chosen = jnp.zeros((e, tr), F32)
    sel_idx, sel_score = [], []
    for _ in range(TOP_K):
        mx = jnp.max(masked, axis=0, keepdims=True)
        ix = jnp.min(jnp.where(masked == mx, ids, e), axis=0, keepdims=True)
        hit = ids == ix
        sel_idx.append(ix)
        sel_score.append(jnp.sum(jnp.where(hit, scores, 0.0), axis=0, keepdims=True))
        chosen = jnp.where(hit, 1.0, chosen)
        masked = jnp.where(hit, -jnp.inf, masked)
    idx = jnp.concatenate(sel_idx, axis=0)
    sc = jnp.concatenate(sel_score, axis=0)
    idx_ref[...] = idx
    wts_ref[...] = sc / jnp.sum(sc, axis=0, keepdims=True) * ROUTED_SCALE

    row = lax.broadcasted_iota(jnp.int32, (tr, tr), 0)
    col = lax.broadcasted_iota(jnp.int32, (tr, tr), 1)
    before = jnp.where(row < col, 1.0, 0.0).astype(BF16)
    prior = jnp.dot(chosen.astype(BF16), before, preferred_element_type=F32) + run_scr[:, 0:1]
    rnk_ref[...] = jnp.concatenate(
        [jnp.sum(jnp.where(ids == sel_idx[k], prior, 0.0), axis=0, keepdims=True)
         for k in range(TOP_K)], axis=0).astype(jnp.int32)
    run_scr[...] = run_scr[...] + jnp.sum(chosen, axis=1, keepdims=True)
    cnt_ref[...] = run_scr[...].astype(jnp.int32)


def _route(hf, wrt, bias_col, tr):
    s, d = hf.shape
    e = wrt.shape[0]
    tok = pl.BlockSpec((TOP_K, tr), lambda i: (0, i))
    return pl.pallas_call(
        _route_kernel,
        grid=(s // tr,),
        in_specs=[pl.BlockSpec((tr, d), lambda i: (i, 0)),
                  pl.BlockSpec((e, d), lambda i: (0, 0)),
                  pl.BlockSpec((e, 1), lambda i: (0, 0))],
        out_specs=[tok, tok, tok, pl.BlockSpec((e, 128), lambda i: (0, 0))],
        out_shape=[jax.ShapeDtypeStruct((TOP_K, s), jnp.int32),
                   jax.ShapeDtypeStruct((TOP_K, s), F32),
                   jax.ShapeDtypeStruct((TOP_K, s), jnp.int32),
                   jax.ShapeDtypeStruct((e, 128), jnp.int32)],
        scratch_shapes=[pltpu.VMEM((e, 128), F32)],
        compiler_params=_cparams(("arbitrary",)),
        name="route",
    )(hf, wrt, bias_col)


def _positions_kernel(idx_ref, rnk_ref, pstart_ref, pos_ref):
    e = pstart_ref.shape[0]
    ts = idx_ref.shape[1]
    ids = lax.broadcasted_iota(jnp.int32, (e, ts), 0)
    idx = idx_ref[...]
    pos_ref[...] = rnk_ref[...] + jnp.concatenate(
        [jnp.sum(jnp.where(ids == idx[k:k + 1, :], pstart_ref[...], 0), axis=0, keepdims=True)
         for k in range(TOP_K)], axis=0)


def _positions(idx, rnk, pstart_col, ts):
    s = idx.shape[1]
    e = pstart_col.shape[0]
    tok = pl.BlockSpec((TOP_K, ts), lambda i: (0, i))
    return pl.pallas_call(
        _positions_kernel,
        grid=(s // ts,),
        in_specs=[tok, tok, pl.BlockSpec((e, 1), lambda i: (0, 0))],
        out_specs=tok,
        out_shape=jax.ShapeDtypeStruct((TOP_K, s), jnp.int32),
        compiler_params=_cparams(("arbitrary",)),
        name="positions",
    )(idx, rnk, pstart_col)


def _swiglu_packed(xp, wg, wu, wd):
    lo, hi = _unpack_halves(xp)
    lo, hi = lo.astype(BF16), hi.astype(BF16)
    n = lo.shape[1]
    g = (jnp.dot(lo, wg[:n], preferred_element_type=F32)
         + jnp.dot(hi, wg[n:], preferred_element_type=F32))
    u = (jnp.dot(lo, wu[:n], preferred_element_type=F32)
         + jnp.dot(hi, wu[n:], preferred_element_type=F32))
    h = (g * jax.nn.sigmoid(g)) * u
    return jnp.dot(h.astype(BF16), wd[...], preferred_element_type=F32)


def _moe_kernel(ie_ref, ib_ref, first_ref, slot_ref, ne_ref, rows_ref, nv_ref, xs_ref, wg_hbm,
                wu_hbm, wd_hbm, ys_ref, wg_f, wu_f, wd_f, sem):
    del ib_ref
    i = pl.program_id(0)

    def fetch(e, slot):
        return [pltpu.make_async_copy(src.at[e], dst.at[slot], sem.at[slot, n])
                for n, (src, dst) in enumerate(((wg_hbm, wg_f), (wu_hbm, wu_f), (wd_hbm, wd_f)))]

    @pl.when(i == 0)
    def _():
        for cp in fetch(ie_ref[0], 0):
            cp.start()

    @pl.when(i < nv_ref[0])
    def _():
        for slot in range(2):
            @pl.when((first_ref[i] == 1) & (slot_ref[i] == slot))
            def _():
                for cp in fetch(ie_ref[i], slot):
                    cp.wait()

                @pl.when(ne_ref[i] >= 0)
                def _():
                    for cp in fetch(ne_ref[i], 1 - slot):
                        cp.start()

        slot = slot_ref[i]
        row = lax.broadcasted_iota(jnp.int32, xs_ref.shape, 0)
        xp = jnp.where(row < rows_ref[i], xs_ref[...], jnp.uint32(0))
        ys_ref[...] = _pack_halves(_swiglu_packed(
            xp, wg_f[slot].astype(BF16), wu_f[slot].astype(BF16), wd_f[slot].astype(BF16)))


def _moe(item_e, item_b, item_first, item_slot, item_next, item_rows, n_valid, xs, wg, wu, wd):
    m_pad, dh = xs.shape
    _, d, f = wg.shape
    n_items = item_e.shape[0]
    blk = lambda i, ie, ib, fi, sl, ne, nr, nv: (ib[i], 0)
    hbm = pl.BlockSpec(memory_space=pl.ANY)
    return pl.pallas_call(
        _moe_kernel,
        grid_spec=pltpu.PrefetchScalarGridSpec(
            num_scalar_prefetch=7,
            grid=(n_items,),
            in_specs=[pl.BlockSpec((MOE_ROWS, dh), blk), hbm, hbm, hbm],
            out_specs=pl.BlockSpec((MOE_ROWS, dh), blk),
            scratch_shapes=[pltpu.VMEM((2, d, f), F32), pltpu.VMEM((2, d, f), F32),
                            pltpu.VMEM((2, f, d), F32),
                            pltpu.SemaphoreType.DMA((2, 3))],
        ),
        out_shape=jax.ShapeDtypeStruct((m_pad, dh), jnp.uint32),
        compiler_params=_cparams(("arbitrary",)),
        name="moe",
    )(item_e, item_b, item_first, item_slot, item_next, item_rows, n_valid, xs, wg, wu, wd)


def _sc_gather_rows(table, idx_row):
    m = idx_row.shape[1]
    w = table.shape[1]
    idx_row = idx_row.reshape(m // SC_GATHER_WINDOW, SC_GATHER_WINDOW)
    mesh = plsc.VectorSubcoreMesh(core_axis_name="c", subcore_axis_name="s")

    @functools.partial(pl.kernel, mesh=mesh,
                       out_type=jax.ShapeDtypeStruct((m, w), table.dtype))
    def gather(table_hbm, idx_hbm, out_hbm):
        def body(idx_vmem, out_vmem):
            pltpu.sync_copy(table_hbm.at[idx_vmem.at[0]], out_vmem)

        pltpu.emit_pipeline(
            body,
            grid=(m // SC_GATHER_WINDOW,),
            in_specs=[pl.BlockSpec((1, SC_GATHER_WINDOW), lambda i: (i, 0))],
            out_specs=[pl.BlockSpec((SC_GATHER_WINDOW, w), lambda i: (i, 0))],
            core_axis_name=("c", "s"),
            dimension_semantics=(pltpu.PARALLEL,),
        )(idx_hbm, out_hbm)

    return gather(table, idx_row)


def _sc_scatter_rows(rows, idx_blocks, m_out):
    s, w = rows.shape
    mesh = plsc.VectorSubcoreMesh(core_axis_name="c", subcore_axis_name="s")

    @functools.partial(pl.kernel, mesh=mesh,
                       out_type=jax.ShapeDtypeStruct((m_out, w), rows.dtype))
    def scatter(rows_hbm, idx_hbm, out_hbm):
        def body(rows_vmem, idx_vmem):
            for k in range(TOP_K):
                pltpu.sync_copy(rows_vmem, out_hbm.at[idx_vmem.at[k]])

        pltpu.emit_pipeline(
            body,
            grid=(s // SC_GATHER_WINDOW,),
            in_specs=[pl.BlockSpec((SC_GATHER_WINDOW, w), lambda i: (i, 0)),
                      pl.BlockSpec((TOP_K, SC_GATHER_WINDOW), lambda i: (i, 0))],
            out_specs=[],
            core_axis_name=("c", "s"),
            dimension_semantics=(pltpu.PARALLEL,),
        )(rows_hbm, idx_hbm)

    return scatter(rows, idx_blocks)


def _combine_kernel(wt_ref, hf_ref, x1_ref, gt_ref, sg_ref, su_ref, sd_ref, g_ref, o_ref):
    tc = x1_ref.shape[0]
    y = _swiglu_packed(hf_ref[...], sg_ref, su_ref, sd_ref)
    wt = wt_ref[...]
    n = g_ref.shape[2]
    r_lo = jnp.zeros((tc, n), F32)
    r_hi = jnp.zeros((tc, n), F32)
    for k in range(TOP_K):
        lo, hi = _unpack_halves(g_ref[k])
        r_lo = r_lo + lo * wt[:, k:k + 1]
        r_hi = r_hi + hi * wt[:, k:k + 1]
    y = y + jnp.concatenate([r_lo, r_hi], axis=1)
    o_ref[...] = x1_ref[...] + gt_ref[...] * y


def _combine(wts_t, hfp, x1, gt, sg, su, sd, gathered, tc):
    s, d = x1.shape
    f = sg.shape[1]
    row = pl.BlockSpec((tc, d), lambda i: (i, 0))
    return pl.pallas_call(
        _combine_kernel,
        grid=(s // tc,),
        in_specs=[pl.BlockSpec((tc, TOP_K), lambda i: (i, 0)),
                  pl.BlockSpec((tc, d // 2), lambda i: (i, 0)), row,
                  pl.BlockSpec((1, d), lambda i: (0, 0)),
                  pl.BlockSpec((d, f), lambda i: (0, 0)),
                  pl.BlockSpec((d, f), lambda i: (0, 0)),
                  pl.BlockSpec((f, d), lambda i: (0, 0)),
                  pl.BlockSpec((TOP_K, tc, d // 2), lambda i: (0, i, 0))],
        out_specs=row,
        out_shape=jax.ShapeDtypeStruct((s, d), F32),
        compiler_params=_cparams(("arbitrary",)),
        name="combine",
    )(wts_t, hfp, x1, gt, sg, su, sd, gathered)


def _tile(n, want):
    t = min(n, want)
    assert n % t == 0, (n, t)
    return t


def _layer(l, x, c_col, pos_row, p):
    s, d = x.shape
    lambda_init = 0.8 - 0.6 * math.exp(-0.3 * l)
    gqk, gv = GLA_HEADS * GLA_DK, GLA_HEADS * GLA_DV
    dqk, dvw = DIFF_HEADS * 2 * DIFF_DH, DIFF_HEADS * DIFF_DV
    lowrank = p["gla_w_a2"].shape[0]

    mod = _ada(c_col, p["w_ada"], p["b_ada"][None, :])
    sh_a, sc_a, gt_a, sh_f, sc_f, gt_f = [mod[:, j * d:(j + 1) * d] for j in range(6)]

    w_in = p["w_in"]
    o = 0
    cols = {}
    for name, wdt in (("gq", gqk), ("gk", gqk), ("gv", gv), ("ga", lowrank), ("gg", gv),
                      ("dq", dqk), ("dk", dqk), ("dv", dvw), ("mg", d), ("md", d)):
        cols[name] = w_in[:, o:o + wdt]
        o += wdt
    row_names = ("gv", "gg", "mg", "md", "gq")
    w_row = jnp.concatenate([cols[n] for n in row_names], axis=1).astype(BF16)
    col_of, o = {}, 0
    for n in row_names:
        col_of[n] = o
        o += cols[n].shape[1]
    t_names = ("dq", "dk", "dv", "gk")
    w_t = jnp.concatenate([cols[n] for n in t_names], axis=1).T.astype(BF16)
    row_of, o = {}, 0
    for n in t_names:
        row_of[n] = o
        o += cols[n].shape[1]
    w_ga = jnp.pad(cols["ga"], ((0, 0), (0, 128 - lowrank))).astype(BF16)

    g1 = p["norm1_g"][None, :]
    tm = _tile(s, TILE_PROJ)
    ts = _tile(s, TILE_SEQ)
    proj, ga = _inproj(x, g1, sc_a, sh_a, w_row, w_ga, tm, w_row.shape[1] // 3)
    projt = _inproj_t(x, g1, sc_a, sh_a, w_t, tm, w_t.shape[0] // 2)

    wa2t = jnp.pad(p["gla_w_a2"].T, ((0, 0), (0, 128 - lowrank)))
    o_gla = _gla(proj, projt, ga, wa2t, p["gla_b_a"][:, None], p["gla_onorm_g"][None, :],
                 ts, col_of["gq"], col_of["gv"], col_of["gg"], row_of["gk"])

    invf = ROPE_THETA ** (-jnp.arange(0, ROT_DIM, 2, dtype=F32) / ROT_DIM)
    qat, qbt, kr, vte, qn, kn = _qkprep(projt, pos_row, invf[:, None], p["diff_qnorm_g"][:, None],
                                p["diff_knorm_g"][:, None], ts, row_of["dq"],
                                row_of["dk"], row_of["dv"])
    tq = ts
    o_diff = _diffattn(qat, qbt, kr, vte, qn, kn, p["diff_lq1"][None, :], p["diff_lk1"][None, :],
                       p["diff_lq2"][None, :], p["diff_lk2"][None, :],
                       p["diff_subln_g"][:, None], lambda_init, tq, tq)

    x1, hf, hfp = _mergeout(o_gla, o_diff, proj, x, p["w_branch_gla"].astype(BF16),
                       p["w_branch_diff"].astype(BF16), p["w_out"].astype(BF16), gt_a,
                       p["norm2_g"][None, :], sc_f, sh_f, ts,
                       col_of["mg"], col_of["md"])

    e = p["w_router"].shape[1]
    idx, wts, rnk, cnt = _route(hf, p["w_router"].T, p["router_bias"][:, None], ts)

    counts = cnt[:, 0]
    pcounts = ((counts + MOE_ROWS - 1) // MOE_ROWS) * MOE_ROWS
    pend = jnp.cumsum(pcounts)
    pstart = pend - pcounts
    pos = _positions(idx, rnk, pstart[:, None], ts)
    n_items = (s * TOP_K) // MOE_ROWS + e
    n_valid = (pend[-1] // MOE_ROWS).astype(jnp.int32)
    item_b = jnp.minimum(jnp.arange(n_items, dtype=jnp.int32), n_valid - 1)
    item_e = jnp.minimum(jnp.sum(pend[None, :] <= (item_b * MOE_ROWS)[:, None], axis=1),
                         e - 1).astype(jnp.int32)

    wn = SC_GATHER_WINDOW
    pos_w = pos.reshape(TOP_K, s // wn, wn).transpose(1, 0, 2).reshape(s // wn * TOP_K, wn)
    xs = _sc_scatter_rows(hfp, pos_w, n_items * MOE_ROWS)
    item_rows = jnp.clip(pstart[item_e] + counts[item_e] - item_b * MOE_ROWS, 0,
                         MOE_ROWS).astype(jnp.int32)
    prev_e = jnp.concatenate([jnp.full((1,), -1, jnp.int32), item_e[:-1]])
    item_first = ((jnp.arange(n_items) < n_valid) & (item_e != prev_e)).astype(jnp.int32)
    item_slot = ((jnp.cumsum(item_first) - 1) % 2).astype(jnp.int32)
    cand = jnp.where(pcounts > 0, jnp.arange(e, dtype=jnp.int32), e)
    following = jnp.concatenate([lax.cummin(cand[::-1])[::-1][1:], jnp.full((1,), e, jnp.int32)])
    item_next = jnp.where(following[item_e] < e, following[item_e], -1).astype(jnp.int32)
    ys = _moe(item_e, item_b, item_first, item_slot, item_next, item_rows, n_valid[None], xs,
              p["w_exp_gate"], p["w_exp_up"], p["w_exp_down"])
    gathered = _sc_gather_rows(ys, pos.reshape(1, TOP_K * s)).reshape(TOP_K, s, d // 2)
    return _combine(wts.T, hfp, x1, gt_f, p["w_sh_gate"].astype(BF16),
                    p["w_sh_up"].astype(BF16), p["w_sh_down"].astype(BF16), gathered,
                    _tile(s, TILE_COMBINE))


_LAYER_PARAMS = ("w_ada", "b_ada", "norm1_g", "w_in", "gla_w_a2", "gla_b_a", "gla_onorm_g",
                 "diff_qnorm_g", "diff_knorm_g", "diff_lq1", "diff_lk1", "diff_lq2", "diff_lk2",
                 "diff_subln_g", "w_branch_gla", "w_branch_diff", "w_out", "norm2_g", "w_router",
                 "router_bias", "w_exp_gate", "w_exp_up", "w_exp_down", "w_sh_gate", "w_sh_up",
                 "w_sh_down")


def kernel(x, c, positions, w_ada, b_ada, norm1_g, w_in, gla_w_a2, gla_b_a, gla_onorm_g, diff_qnorm_g, diff_knorm_g, diff_lq1, diff_lk1, diff_lq2, diff_lk2, diff_subln_g, w_branch_gla, w_branch_diff, w_out, norm2_g, w_router, router_bias, w_exp_gate, w_exp_up, w_exp_down, w_sh_gate, w_sh_up, w_sh_down):
    stacked = dict(zip(_LAYER_PARAMS, (
        w_ada, b_ada, norm1_g, w_in, gla_w_a2, gla_b_a, gla_onorm_g, diff_qnorm_g, diff_knorm_g,
        diff_lq1, diff_lk1, diff_lq2, diff_lk2, diff_subln_g, w_branch_gla, w_branch_diff, w_out,
        norm2_g, w_router, router_bias, w_exp_gate, w_exp_up, w_exp_down, w_sh_gate, w_sh_up,
        w_sh_down)))
    b, s, d = x.shape
    assert b == 1, "single-sequence kernel"
    xl = x[0]
    c_col = c[0][:, None]
    pos_row = positions.astype(jnp.int32)
    for l in range(w_ada.shape[0]):
        xl = _layer(l, xl, c_col, pos_row, {k: v[l] for k, v in stacked.items()})
    return xl[None]
```

```python
import functools
import math

import jax
import jax.numpy as jnp
from jax import lax
from jax.experimental import pallas as pl
from jax.experimental.pallas import tpu as pltpu
from jax.experimental.pallas import tpu_sc as plsc

CHUNK = 64
EPS = 1e-6
GLA_HEADS = 4
GLA_DK = 128
GLA_DV = 256
GLA_TAU = 16.0
DIFF_HEADS = 8
DIFF_DH = 64
DIFF_DV = 2 * DIFF_DH
ROPE_THETA = 500000.0
ROT_DIM = DIFF_DH // 4
N_GROUPS = 8
TOPK_GROUPS = 4
TOP_K = 8
ROUTED_SCALE = 2.5

MOE_ROWS = 640
MOE_DMA_CHUNKS = 4
SC_GATHER_WINDOW = 64
VMEM_LIMIT = 56 * 1024 * 1024
TILE_PROJ = 1024
TILE_SEQ = 512
TILE_COMBINE = 256
NEG_BIG = -1e30
LOG2E = 1.4426950408889634
HIGHEST = lax.Precision.HIGHEST
F32 = jnp.float32
BF16 = jnp.bfloat16


def _cparams(sem):
    return pltpu.CompilerParams(dimension_semantics=sem, vmem_limit_bytes=VMEM_LIMIT)


def _nt_dot(a, b, precision=None):
    return lax.dot_general(a, b, (((1,), (1,)), ((), ())), precision=precision,
                           preferred_element_type=F32)


def _pack_halves(x):
    n = x.shape[1] // 2
    lo = pltpu.bitcast(x[:, :n].astype(BF16).astype(F32), jnp.uint32) >> 16
    hi = pltpu.bitcast(x[:, n:].astype(BF16).astype(F32), jnp.uint32) & jnp.uint32(0xFFFF0000)
    return lo | hi


def _unpack_halves(w):
    return (pltpu.bitcast(w << 16, F32), pltpu.bitcast(w & jnp.uint32(0xFFFF0000), F32))


def _rms_mod(x, g, sc, sh):
    xn = x * lax.rsqrt(jnp.mean(x * x, axis=-1, keepdims=True) + EPS)
    return (xn * g) * (1.0 + sc) + sh


def _ada_kernel(c_ref, w_ref, b_ref, o_ref):
    c = c_ref[...]
    ca = c * jax.nn.sigmoid(c)
    o_ref[...] = jnp.sum(ca * w_ref[...], axis=0, keepdims=True) + b_ref[...]


def _ada(c_col, w, b):
    d, n = w.shape
    tn = min(1024, n)
    return pl.pallas_call(
        _ada_kernel,
        grid=(n // tn,),
        in_specs=[pl.BlockSpec((d, 1), lambda j: (0, 0)),
                  pl.BlockSpec((d, tn), lambda j: (0, j)),
                  pl.BlockSpec((1, tn), lambda j: (0, j))],
        out_specs=pl.BlockSpec((1, tn), lambda j: (0, j)),
        out_shape=jax.ShapeDtypeStruct((1, n), F32),
        compiler_params=_cparams(("arbitrary",)),
        name="ada",
    )(c_col, w, b)


def _inproj_kernel(x_ref, g_ref, sc_ref, sh_ref, w_ref, wga_ref, o_ref, ga_ref, h_scr):
    @pl.when(pl.program_id(1) == 0)
    def _():
        h = _rms_mod(x_ref[...], g_ref[...], sc_ref[...], sh_ref[...]).astype(BF16)
        h_scr[...] = h
        ga_ref[...] = jnp.dot(h, wga_ref[...], preferred_element_type=F32)

    o_ref[...] = jnp.dot(h_scr[...], w_ref[...], preferred_element_type=F32).astype(BF16)


def _inproj(x, g, sc, sh, w, wga, tm, tn):
    s, d = x.shape
    n = w.shape[1]
    vec = pl.BlockSpec((1, d), lambda i, j: (0, 0))
    return pl.pallas_call(
        _inproj_kernel,
        grid=(s // tm, n // tn),
        in_specs=[pl.BlockSpec((tm, d), lambda i, j: (i, 0)), vec, vec, vec,
                  pl.BlockSpec((d, tn), lambda i, j: (0, j)),
                  pl.BlockSpec((d, 128), lambda i, j: (0, 0))],
        out_specs=[pl.BlockSpec((tm, tn), lambda i, j: (i, j)),
                   pl.BlockSpec((tm, 128), lambda i, j: (i, 0))],
        out_shape=[jax.ShapeDtypeStruct((s, n), BF16), jax.ShapeDtypeStruct((s, 128), F32)],
        scratch_shapes=[pltpu.VMEM((tm, d), BF16)],
        compiler_params=_cparams(("arbitrary", "arbitrary")),
        name="inproj",
    )(x, g, sc, sh, w, wga)


def _inproj_t_kernel(x_ref, g_ref, sc_ref, sh_ref, wt_ref, o_ref, h_scr):
    @pl.when(pl.program_id(1) == 0)
    def _():
        h_scr[...] = _rms_mod(x_ref[...], g_ref[...], sc_ref[...], sh_ref[...]).astype(BF16)

    o_ref[...] = _nt_dot(wt_ref[...], h_scr[...]).astype(BF16)


def _inproj_t(x, g, sc, sh, wt, tm, tn):
    s, d = x.shape
    n = wt.shape[0]
    vec = pl.BlockSpec((1, d), lambda i, j: (0, 0))
    return pl.pallas_call(
        _inproj_t_kernel,
        grid=(s // tm, n // tn),
        in_specs=[pl.BlockSpec((tm, d), lambda i, j: (i, 0)), vec, vec, vec,
                  pl.BlockSpec((tn, d), lambda i, j: (j, 0))],
        out_specs=pl.BlockSpec((tn, tm), lambda i, j: (j, i)),
        out_shape=jax.ShapeDtypeStruct((n, s), BF16),
        scratch_shapes=[pltpu.VMEM((tm, d), BF16)],
        compiler_params=_cparams(("arbitrary", "arbitrary")),
        name="inproj_t",
    )(x, g, sc, sh, wt)


def _gla_kernel(q_ref, kt_ref, v_ref, gg_ref, ga_ref, wa2t_ref, ba_ref, on_ref, o_ref,
                state_ref, o_scr):
    tt = q_ref.shape[0]
    nchunk = tt // CHUNK

    @pl.when(pl.program_id(0) == 0)
    def _():
        state_ref[...] = jnp.zeros_like(state_ref)

    zt = _nt_dot(wa2t_ref[...], ga_ref[...], precision=HIGHEST) + ba_ref[...]
    lat = (jnp.minimum(zt, 0.0) - jnp.log1p(jnp.exp(-jnp.abs(zt)))) * (1.0 / GLA_TAU)
    row = lax.broadcasted_iota(jnp.int32, (tt, tt), 0)
    col = lax.broadcasted_iota(jnp.int32, (tt, tt), 1)
    same = (row // CHUNK) == (col // CHUNK)
    incl = jnp.where(same & (row <= col), 1.0, 0.0).astype(BF16)
    full = jnp.where(same, 1.0, 0.0).astype(BF16)
    lat_hi = lat.astype(BF16)
    lat_lo = (lat - lat_hi.astype(F32)).astype(BF16)
    cumt = (jnp.dot(lat_hi, incl, preferred_element_type=F32)
            + jnp.dot(lat_lo, incl, preferred_element_type=F32))
    tott = (jnp.dot(lat_hi, full, preferred_element_type=F32)
            + jnp.dot(lat_lo, full, preferred_element_type=F32))
    kdt = kt_ref[...].astype(F32) * jnp.exp(tott - cumt)
    dec = jnp.exp(tott)

    lane = lax.broadcasted_iota(jnp.int32, (GLA_DK, 2 * CHUNK), 1)
    upd = {}
    for c in range(nchunk):
        pair = (c // 2) * 2 * CHUNK
        if nchunk > 1:
            keep = (lane // CHUNK) == (c % 2)
        for h in range(GLA_HEADS):
            rows = slice(h * GLA_DK, (h + 1) * GLA_DK)
            vcols = slice(h * GLA_DV, (h + 1) * GLA_DV)
            if nchunk > 1:
                a = jnp.where(keep, kdt[rows, pair:pair + 2 * CHUNK], 0.0).astype(BF16)
                vp = v_ref[pair:pair + 2 * CHUNK, vcols]
            else:
                a = kdt[rows, :].astype(BF16)
                vp = v_ref[:, vcols]
            upd[c, h] = jnp.dot(a, vp, preferred_element_type=F32)

    for h in range(GLA_HEADS):
        rows = slice(h * GLA_DK, (h + 1) * GLA_DK)
        vcols = slice(h * GLA_DV, (h + 1) * GLA_DV)
        st = state_ref[h]
        states = []
        for c in range(nchunk):
            st = st * dec[rows, c * CHUNK:c * CHUNK + 1] + upd[c, h]
            states.append(st.astype(BF16))
        state_ref[h] = st
        for c in range(nchunk):
            o_scr[c * CHUNK:(c + 1) * CHUNK, vcols] = jnp.dot(
                q_ref[c * CHUNK:(c + 1) * CHUNK, rows], states[c], preferred_element_type=F32)

    for h in range(GLA_HEADS):
        vcols = slice(h * GLA_DV, (h + 1) * GLA_DV)
        o = o_scr[:, vcols] * (GLA_DK ** -0.5)
        o = o * lax.rsqrt(jnp.mean(o * o, axis=-1, keepdims=True) + EPS) * on_ref[...]
        g = gg_ref[:, vcols].astype(F32)
        o_ref[:, vcols] = (o * (g * jax.nn.sigmoid(g))).astype(BF16)


def _gla(proj, projt, ga, wa2t, ba_col, on_g, tt, col_q, col_v, col_g, row_k):
    s = proj.shape[0]
    qk = GLA_HEADS * GLA_DK
    vw = GLA_HEADS * GLA_DV
    return pl.pallas_call(
        _gla_kernel,
        grid=(s // tt,),
        in_specs=[pl.BlockSpec((tt, qk), lambda i: (i, col_q // qk)),
                  pl.BlockSpec((qk, tt), lambda i: (row_k // qk, i)),
                  pl.BlockSpec((tt, vw), lambda i: (i, col_v // vw)),
                  pl.BlockSpec((tt, vw), lambda i: (i, col_g // vw)),
                  pl.BlockSpec((tt, 128), lambda i: (i, 0)),
                  pl.BlockSpec((qk, 128), lambda i: (0, 0)),
                  pl.BlockSpec((qk, 1), lambda i: (0, 0)),
                  pl.BlockSpec((1, GLA_DV), lambda i: (0, 0))],
        out_specs=pl.BlockSpec((tt, vw), lambda i: (i, 0)),
        out_shape=jax.ShapeDtypeStruct((s, vw), BF16),
        scratch_shapes=[pltpu.VMEM((GLA_HEADS, GLA_DK, GLA_DV), F32),
                        pltpu.VMEM((tt, vw), F32)],
        compiler_params=_cparams(("arbitrary",)),
        name="gla",
    )(proj, projt, proj, proj, ga, wa2t, ba_col, on_g)


def _qknorm_rope_t(xt, g_col, cos, sin):
    n, tm = xt.shape
    x3 = xt.reshape(n // DIFF_DH, DIFF_DH, tm)
    r = lax.rsqrt(jnp.mean(x3 * x3, axis=1, keepdims=True) + EPS)
    y = x3 * r * g_col[None]
    half = ROT_DIM // 2
    y1, y2, rest = y[:, :half], y[:, half:ROT_DIM], y[:, ROT_DIM:]
    o1 = y1 * cos[None] - y2 * sin[None]
    o2 = y2 * cos[None] + y1 * sin[None]
    return jnp.concatenate([o1, o2, rest], axis=1)


def _seg_norms(x3):
    return jnp.sqrt(jnp.sum(x3 * x3, axis=1)).reshape(DIFF_HEADS, 2, x3.shape[2])


def _qkprep_kernel(qt_ref, kt_ref, vt_ref, pos_ref, invf_ref, qg_ref, kg_ref, qa_ref, qb_ref,
                   ko_ref, ve_ref, qn_ref, kn_ref):
    tm = qt_ref.shape[1]
    v3 = vt_ref[...].reshape(DIFF_HEADS, DIFF_DV, tm)
    ones = jnp.ones((DIFF_HEADS, ATT_SUM_ROWS, tm), BF16)
    ve_ref[...] = jnp.concatenate([v3, ones], axis=1).reshape(-1, tm)
    ang = pos_ref[...].astype(F32) * invf_ref[...]
    cos, sin = jnp.cos(ang), jnp.sin(ang)
    k3 = _qknorm_rope_t(kt_ref[...].astype(F32), kg_ref[...], cos, sin)
    ko_ref[...] = k3.reshape(-1, tm).T.astype(BF16)
    kn_ref[...] = _seg_norms(k3)
    q3 = _qknorm_rope_t(qt_ref[...].astype(F32), qg_ref[...], cos, sin) * (
        DIFF_DH ** -0.5 * LOG2E)
    qn_ref[...] = _seg_norms(q3)
    seg = lax.broadcasted_iota(jnp.int32, q3.shape, 0)
    qa_ref[...] = jnp.where(seg % 2 == 0, q3, 0.0).reshape(-1, tm).astype(BF16)
    qb_ref[...] = jnp.where(seg % 2 == 1, q3, 0.0).reshape(-1, tm).astype(BF16)


def _qkprep(projt, pos_row, invf_col, qg_col, kg_col, tm, row_q, row_k, row_v):
    s = projt.shape[1]
    n = DIFF_HEADS * 2 * DIFF_DH
    ne = DIFF_HEADS * (DIFF_DV + ATT_SUM_ROWS)
    col = pl.BlockSpec((DIFF_DH, 1), lambda i: (0, 0))
    return pl.pallas_call(
        _qkprep_kernel,
        grid=(s // tm,),
        in_specs=[pl.BlockSpec((n, tm), lambda i: (row_q // n, i)),
                  pl.BlockSpec((n, tm), lambda i: (row_k // n, i)),
                  pl.BlockSpec((n, tm), lambda i: (row_v // n, i)),
                  pl.BlockSpec((1, tm), lambda i: (0, i)),
                  pl.BlockSpec((ROT_DIM // 2, 1), lambda i: (0, 0)), col, col],
        out_specs=[pl.BlockSpec((n, tm), lambda i: (0, i)),
                   pl.BlockSpec((n, tm), lambda i: (0, i)),
                   pl.BlockSpec((tm, n), lambda i: (i, 0)),
                   pl.BlockSpec((ne, tm), lambda i: (0, i)),
                   pl.BlockSpec((DIFF_HEADS, 2, tm), lambda i: (0, 0, i)),
                   pl.BlockSpec((DIFF_HEADS, 2, tm), lambda i: (0, 0, i))],
        out_shape=[jax.ShapeDtypeStruct((n, s), BF16), jax.ShapeDtypeStruct((n, s), BF16),
                   jax.ShapeDtypeStruct((s, n), BF16), jax.ShapeDtypeStruct((ne, s), BF16),
                   jax.ShapeDtypeStruct((DIFF_HEADS, 2, s), F32),
                   jax.ShapeDtypeStruct((DIFF_HEADS, 2, s), F32)],
        compiler_params=_cparams(("arbitrary",)),
        name="qkprep",
    )(projt, projt, projt, pos_row, invf_col, qg_col, kg_col)


ATT_COLS = 256
ATT_LOOKAHEAD = {True: 2, False: 4}
ATT_BODY_TILES = {True: 8, False: 2}
ATT_SUM_ROWS = 16
ATT_BOUND_SLACK = 1.02
ATT_BOUND_LIMIT = 50.0


def _diffattn_kernel(qa_ref, qb_ref, k_ref, vt_ref, qn_ref, kn_ref, lq1_ref, lk1_ref, lq2_ref,
                     lk2_ref, sg_ref, o_ref, *scr, lambda_init, tk, cols):
    tq = qa_ref.shape[1]
    nblk = 2 * tq // cols
    m_scr, acc_scr = (scr[b * nblk:(b + 1) * nblk] for b in range(2))
    kmax_scr = scr[2 * nblk]
    s_scr = scr[2 * nblk + 1:]
    i = pl.program_id(1)

    @pl.when(i == 0)
    def _():
        kmax_scr[...] = jnp.max(kn_ref[...], axis=1, keepdims=True)

    bound = qn_ref[...] * kmax_scr[...] * ATT_BOUND_SLACK
    bound = jnp.concatenate([bound[0:1], bound[1:2]], axis=1)
    bounded = jnp.max(bound) < ATT_BOUND_LIMIT
    for c in range(nblk):
        m_scr[c][...] = jnp.where(bounded, bound[:, c * cols:(c + 1) * cols], NEG_BIG)
        acc_scr[c][...] = jnp.zeros_like(acc_scr[c])

    def scores(j, c):
        start = pl.multiple_of(j * tk, tk)
        q_ref = qa_ref if c * cols < tq else qb_ref
        off = (c * cols) % tq
        return jnp.dot(k_ref[pl.ds(start, tk), :], q_ref[:, off:off + cols],
                       preferred_element_type=F32)

    def steps(tiles, masked, next_tile, fixed):
        look = ATT_LOOKAHEAD[fixed]
        items = [(j, c) for j in tiles for c in range(nblk)]
        pending = []
        for n, (j, c) in enumerate(items):
            s = s_scr[n][...] if n < look else pending.pop(0)
            ahead = n + look
            if ahead < len(items):
                pending.append(scores(*items[ahead]))
            elif next_tile is not None:
                s_scr[ahead - len(items)][...] = scores(next_tile, ahead - len(items))
            start = pl.multiple_of(j * tk, tk)
            keys = tk
            if masked:
                keys = (c * cols) % tq + cols
                s = s[:keys]
                krow = lax.broadcasted_iota(jnp.int32, (keys, cols), 0)
                qcol = lax.broadcasted_iota(jnp.int32, (keys, cols), 1)
                qpos = i * tq + (c * cols) % tq + qcol
                s = jnp.where((start + krow) // CHUNK <= qpos // CHUNK, s, NEG_BIG)
            vj = vt_ref[:, pl.ds(start, keys)]
            if fixed:
                p = jnp.exp2(s - m_scr[c][...]).astype(BF16)
                acc_scr[c][...] += jnp.dot(vj, p, preferred_element_type=F32)
            else:
                m_prev = m_scr[c][...]
                m_new = jnp.maximum(m_prev, jnp.max(s, axis=0, keepdims=True))
                alpha = jnp.exp2(m_prev - m_new)
                p = jnp.exp2((s - m_new).astype(BF16))
                acc_scr[c][...] = alpha * acc_scr[c][...] + jnp.dot(
                    vj, p, preferred_element_type=F32)
                m_scr[c][...] = m_new

    def attend(fixed):
        for c in range(ATT_LOOKAHEAD[fixed]):
            s_scr[c][...] = scores(0, c)
        big = ATT_BODY_TILES[fixed]
        lax.fori_loop(0, i // big, lambda t, c: (steps(
            tuple(big * t + u for u in range(big)), False, big * t + big, fixed), c)[1], 0)
        size = big // 2
        while size >= 1:
            first = (i // (2 * size)) * (2 * size)

            @pl.when(i % (2 * size) >= size)
            def _(first=first, size=size):
                steps(tuple(first + u for u in range(size)), False, first + size, fixed)

            size //= 2
        steps((i,), True, None, fixed)

    @pl.when(bounded)
    def _():
        attend(True)

    @pl.when(jnp.logical_not(bounded))
    def _():
        attend(False)

    o = jnp.concatenate([acc_scr[c][:DIFF_DV] / acc_scr[c][DIFF_DV:DIFF_DV + 1]
                         for c in range(nblk)], axis=1)
    lam = (jnp.exp(jnp.sum(lq1_ref[...] * lk1_ref[...]))
           - jnp.exp(jnp.sum(lq2_ref[...] * lk2_ref[...])) + lambda_init)
    o = o[:, :tq] - lam * o[:, tq:]
    o = o * lax.rsqrt(jnp.mean(o * o, axis=0, keepdims=True) + EPS) * sg_ref[...]
    o_ref[...] = (o * (1.0 - lambda_init)).T.astype(BF16)


def _diffattn(qat, qbt, kr, vte, qn, kn, lq1, lk1, lq2, lk2, sg_col, lambda_init, tq, tk):
    s = kr.shape[0]
    hd = 2 * DIFF_DH
    vec = pl.BlockSpec((1, DIFF_DH), lambda h, i: (0, 0))
    cols = min(ATT_COLS, tq)
    nblk = 2 * tq // cols
    kern = functools.partial(_diffattn_kernel, lambda_init=lambda_init, tk=tk, cols=cols)
    dve = DIFF_DV + ATT_SUM_ROWS
    assert tq == tk and nblk >= max(ATT_LOOKAHEAD.values())
    scratch = ([pltpu.VMEM((1, cols), F32)] * nblk + [pltpu.VMEM((dve, cols), F32)] * nblk
               + [pltpu.VMEM((2, 1), F32)]
               + [pltpu.VMEM((tk, cols), F32)] * max(ATT_LOOKAHEAD.values()))
    return pl.pallas_call(
        kern,
        grid=(DIFF_HEADS, s // tq),
        in_specs=[pl.BlockSpec((hd, tq), lambda h, i: (h, i)),
                  pl.BlockSpec((hd, tq), lambda h, i: (h, i)),
                  pl.BlockSpec((s, hd), lambda h, i: (0, h)),
                  pl.BlockSpec((dve, s), lambda h, i: (h, 0)),
                  pl.BlockSpec((None, 2, tq), lambda h, i: (h, 0, i)),
                  pl.BlockSpec((None, 2, s), lambda h, i: (h, 0, 0)),
                  vec, vec, vec, vec,
                  pl.BlockSpec((DIFF_DV, 1), lambda h, i: (0, 0))],
        out_specs=pl.BlockSpec((tq, DIFF_DV), lambda h, i: (i, h)),
        out_shape=jax.ShapeDtypeStruct((s, DIFF_HEADS * DIFF_DV), BF16),
        scratch_shapes=scratch,
        compiler_params=_cparams(("arbitrary", "arbitrary")),
        name="diffattn",
    )(qat, qbt, kr, vte, qn, kn, lq1, lk1, lq2, lk2, sg_col)


def _mergeout_kernel(og_ref, od_ref, mg_ref, md_ref, x_ref, wbg_ref, wbd_ref, wo_ref, gt_ref,
                     g2_ref, sc_ref, sh_ref, x1_ref, hf_ref, hfp_ref):
    bg = jnp.dot(og_ref[...], wbg_ref[...], preferred_element_type=F32)
    bd = jnp.dot(od_ref[...], wbd_ref[...], preferred_element_type=F32)
    merged = (jax.nn.sigmoid(mg_ref[...].astype(F32)) * bg
              + jax.nn.sigmoid(md_ref[...].astype(F32)) * bd)
    x1 = x_ref[...] + gt_ref[...] * jnp.dot(merged.astype(BF16), wo_ref[...],
                                             preferred_element_type=F32)
    x1_ref[...] = x1
    hf = _rms_mod(x1, g2_ref[...], sc_ref[...], sh_ref[...])
    hf_ref[...] = hf
    hfp_ref[...] = _pack_halves(hf)


def _mergeout(og, od, proj, x, wbg, wbd, wo, gt, g2, sc, sh, tm, col_mg, col_md):
    s, d = x.shape
    vec = pl.BlockSpec((1, d), lambda i: (0, 0))
    wspec = pl.BlockSpec((d, d), lambda i: (0, 0))
    row = pl.BlockSpec((tm, d), lambda i: (i, 0))
    return pl.pallas_call(
        _mergeout_kernel,
        grid=(s // tm,),
        in_specs=[row, row,
                  pl.BlockSpec((tm, d), lambda i: (i, col_mg // d)),
                  pl.BlockSpec((tm, d), lambda i: (i, col_md // d)),
                  row, wspec, wspec, wspec, vec, vec, vec, vec],
        out_specs=[row, row, pl.BlockSpec((tm, d // 2), lambda i: (i, 0))],
        out_shape=[jax.ShapeDtypeStruct((s, d), F32), jax.ShapeDtypeStruct((s, d), F32),
                   jax.ShapeDtypeStruct((s, d // 2), jnp.uint32)],
        compiler_params=_cparams(("arbitrary",)),
        name="mergeout",
    )(og, od, proj, proj, x, wbg, wbd, wo, gt, g2, sc, sh)


def _route_kernel(hf_ref, wrt_ref, bias_ref, idx_ref, wts_ref, rnk_ref, cnt_ref, run_scr):
    tr = hf_ref.shape[0]
    e = wrt_ref.shape[0]
    gsz = e // N_GROUPS

    @pl.when(pl.program_id(0) == 0)
    def _():
        run_scr[...] = jnp.zeros_like(run_scr)

    logits = _nt_dot(wrt_ref[...], hf_ref[...], precision=HIGHEST)
    scores = jax.nn.sigmoid(logits)
    biased = scores + bias_ref[...]
    g3 = biased.reshape(N_GROUPS, gsz, tr)
    m1 = jnp.max(g3, axis=1, keepdims=True)
    n_top = jnp.sum(jnp.where(g3 == m1, 1.0, 0.0), axis=1, keepdims=True)
    m2 = jnp.max(jnp.where(g3 < m1, g3, -jnp.inf), axis=1, keepdims=True)
    gs = (m1 + jnp.where(n_top >= 2.0, m1, m2)).reshape(N_GROUPS, tr)
    gi = lax.broadcasted_iota(jnp.int32, (N_GROUPS, tr), 0)
    beaten = jnp.zeros((N_GROUPS, tr), F32)
    for g in range(N_GROUPS):
        other = gs[g:g + 1, :]
        beaten = beaten + jnp.where((other > gs) | ((other == gs) & (g < gi)), 1.0, 0.0)
    gsel = (beaten < float(TOPK_GROUPS)).reshape(N_GROUPS, 1, tr)
    masked = jnp.where(gsel, g3, -jnp.inf).reshape(e, tr)

    ids = lax.broadcasted_iota(jnp.int32, (e, tr), 0)
    chosen = jnp.zeros((e, tr), F32)
    sel_idx, sel_score = [], []
    for _ in range(TOP_K):
        mx = jnp.max(masked, axis=0, keepdims=True)
        ix = jnp.min(jnp.where(masked == mx, ids, e), axis=0, keepdims=True)
        hit = ids == ix
        sel_idx.append(ix)
        sel_score.append(jnp.sum(jnp.where(hit, scores, 0.0), axis=0, keepdims=True))
        chosen = jnp.where(hit, 1.0, chosen)
        masked = jnp.where(hit, -jnp.inf, masked)
    idx = jnp.concatenate(sel_idx, axis=0)
    sc = jnp.concatenate(sel_score, axis=0)
    idx_ref[...] = idx
    wts_ref[...] = sc / jnp.sum(sc, axis=0, keepdims=True) * ROUTED_SCALE

    row = lax.broadcasted_iota(jnp.int32, (tr, tr), 0)
    col = lax.broadcasted_iota(jnp.int32, (tr, tr), 1)
    before = jnp.where(row < col, 1.0, 0.0).astype(BF16)
    prior = jnp.dot(chosen.astype(BF16), before, preferred_element_type=F32) + run_scr[:, 0:1]
    rnk_ref[...] = jnp.concatenate(
        [jnp.sum(jnp.where(ids == sel_idx[k], prior, 0.0), axis=0, keepdims=True)
         for k in range(TOP_K)], axis=0).astype(jnp.int32)
    run_scr[...] = run_scr[...] + jnp.sum(chosen, axis=1, keepdims=True)
    cnt_ref[...] = run_scr[...].astype(jnp.int32)


def _route(hf, wrt, bias_col, tr):
    s, d = hf.shape
    e = wrt.shape[0]
    tok = pl.BlockSpec((TOP_K, tr), lambda i: (0, i))
    return pl.pallas_call(
        _route_kernel,
        grid=(s // tr,),
        in_specs=[pl.BlockSpec((tr, d), lambda i: (i, 0)),
                  pl.BlockSpec((e, d), lambda i: (0, 0)),
                  pl.BlockSpec((e, 1), lambda i: (0, 0))],
        out_specs=[tok, tok, tok, pl.BlockSpec((e, 128), lambda i: (0, 0))],
        out_shape=[jax.ShapeDtypeStruct((TOP_K, s), jnp.int32),
                   jax.ShapeDtypeStruct((TOP_K, s), F32),
                   jax.ShapeDtypeStruct((TOP_K, s), jnp.int32),
                   jax.ShapeDtypeStruct((e, 128), jnp.int32)],
        scratch_shapes=[pltpu.VMEM((e, 128), F32)],
        compiler_params=_cparams(("arbitrary",)),
        name="route",
    )(hf, wrt, bias_col)


def _positions_kernel(idx_ref, rnk_ref, pstart_ref, pos_ref):
    e = pstart_ref.shape[0]
    ts = idx_ref.shape[1]
    ids = lax.broadcasted_iota(jnp.int32, (e, ts), 0)
    idx = idx_ref[...]
    pos_ref[...] = rnk_ref[...] + jnp.concatenate(
        [jnp.sum(jnp.where(ids == idx[k:k + 1, :], pstart_ref[...], 0), axis=0, keepdims=True)
         for k in range(TOP_K)], axis=0)


def _positions(idx, rnk, pstart_col, ts):
    s = idx.shape[1]
    e = pstart_col.shape[0]
    tok = pl.BlockSpec((TOP_K, ts), lambda i: (0, i))
    return pl.pallas_call(
        _positions_kernel,
        grid=(s // ts,),
        in_specs=[tok, tok, pl.BlockSpec((e, 1), lambda i: (0, 0))],
        out_specs=tok,
        out_shape=jax.ShapeDtypeStruct((TOP_K, s), jnp.int32),
        compiler_params=_cparams(("arbitrary",)),
        name="positions",
    )(idx, rnk, pstart_col)


def _swiglu_packed(xp, wg, wu, wd):
    lo, hi = _unpack_halves(xp)
    lo, hi = lo.astype(BF16), hi.astype(BF16)
    n = lo.shape[1]
    g = (jnp.dot(lo, wg[:n], preferred_element_type=F32)
         + jnp.dot(hi, wg[n:], preferred_element_type=F32))
    u = (jnp.dot(lo, wu[:n], preferred_element_type=F32)
         + jnp.dot(hi, wu[n:], preferred_element_type=F32))
    h = (g * jax.nn.sigmoid(g)) * u
    return jnp.dot(h.astype(BF16), wd[...], preferred_element_type=F32)


def _moe_kernel(ie_ref, ib_ref, first_ref, slot_ref, ne_ref, rows_ref, nv_ref, xs_ref, wg_hbm,
                wu_hbm, wd_hbm, ys_ref, wg_f, wu_f, wd_f, sem):
    del ib_ref
    i = pl.program_id(0)

    def fetch(e, slot):
        copies = []
        for n, (src, dst) in enumerate(((wg_hbm, wg_f), (wu_hbm, wu_f), (wd_hbm, wd_f))):
            rows = src.shape[1] // MOE_DMA_CHUNKS
            for ch in range(MOE_DMA_CHUNKS):
                part = pl.ds(ch * rows, rows)
                copies.append(pltpu.make_async_copy(
                    src.at[e, part], dst.at[slot, part], sem.at[slot, n * MOE_DMA_CHUNKS + ch]))
        return copies

    @pl.when(i == 0)
    def _():
        for cp in fetch(ie_ref[0], 0):
            cp.start()

    @pl.when(i < nv_ref[0])
    def _():
        for slot in range(2):
            @pl.when((first_ref[i] == 1) & (slot_ref[i] == slot))
            def _():
                for cp in fetch(ie_ref[i], slot):
                    cp.wait()

                @pl.when(ne_ref[i] >= 0)
                def _():
                    for cp in fetch(ne_ref[i], 1 - slot):
                        cp.start()

        slot = slot_ref[i]
        row = lax.broadcasted_iota(jnp.int32, xs_ref.shape, 0)
        xp = jnp.where(row < rows_ref[i], xs_ref[...], jnp.uint32(0))
        ys_ref[...] = _pack_halves(_swiglu_packed(
            xp, wg_f[slot].astype(BF16), wu_f[slot].astype(BF16), wd_f[slot].astype(BF16)))


def _moe(item_e, item_b, item_first, item_slot, item_next, item_rows, n_valid, xs, wg, wu, wd):
    m_pad, dh = xs.shape
    _, d, f = wg.shape
    n_items = item_e.shape[0]
    blk = lambda i, ie, ib, fi, sl, ne, nr, nv: (ib[i], 0)
    hbm = pl.BlockSpec(memory_space=pl.ANY)
    return pl.pallas_call(
        _moe_kernel,
        grid_spec=pltpu.PrefetchScalarGridSpec(
            num_scalar_prefetch=7,
            grid=(n_items,),
            in_specs=[pl.BlockSpec((MOE_ROWS, dh), blk), hbm, hbm, hbm],
            out_specs=pl.BlockSpec((MOE_ROWS, dh), blk),
            scratch_shapes=[pltpu.VMEM((2, d, f), F32), pltpu.VMEM((2, d, f), F32),
                            pltpu.VMEM((2, f, d), F32),
                            pltpu.SemaphoreType.DMA((2, 3 * MOE_DMA_CHUNKS))],
        ),
        out_shape=jax.ShapeDtypeStruct((m_pad, dh), jnp.uint32),
        compiler_params=_cparams(("arbitrary",)),
        name="moe",
    )(item_e, item_b, item_first, item_slot, item_next, item_rows, n_valid, xs, wg, wu, wd)


def _sc_gather_rows(table, idx_row):
    m = idx_row.shape[1]
    w = table.shape[1]
    idx_row = idx_row.reshape(m // SC_GATHER_WINDOW, SC_GATHER_WINDOW)
    mesh = plsc.VectorSubcoreMesh(core_axis_name="c", subcore_axis_name="s")

    @functools.partial(pl.kernel, mesh=mesh,
                       out_type=jax.ShapeDtypeStruct((m, w), table.dtype))
    def gather(table_hbm, idx_hbm, out_hbm):
        def body(idx_vmem, out_vmem):
            pltpu.sync_copy(table_hbm.at[idx_vmem.at[0]], out_vmem)

        pltpu.emit_pipeline(
            body,
            grid=(m // SC_GATHER_WINDOW,),
            in_specs=[pl.BlockSpec((1, SC_GATHER_WINDOW), lambda i: (i, 0))],
            out_specs=[pl.BlockSpec((SC_GATHER_WINDOW, w), lambda i: (i, 0))],
            core_axis_name=("c", "s"),
            dimension_semantics=(pltpu.PARALLEL,),
        )(idx_hbm, out_hbm)

    return gather(table, idx_row)


def _sc_scatter_rows(rows, idx_blocks, m_out):
    s, w = rows.shape
    mesh = plsc.VectorSubcoreMesh(core_axis_name="c", subcore_axis_name="s")

    @functools.partial(pl.kernel, mesh=mesh,
                       out_type=jax.ShapeDtypeStruct((m_out, w), rows.dtype))
    def scatter(rows_hbm, idx_hbm, out_hbm):
        def body(rows_vmem, idx_vmem):
            for k in range(TOP_K):
                pltpu.sync_copy(rows_vmem, out_hbm.at[idx_vmem.at[k]])

        pltpu.emit_pipeline(
            body,
            grid=(s // SC_GATHER_WINDOW,),
            in_specs=[pl.BlockSpec((SC_GATHER_WINDOW, w), lambda i: (i, 0)),
                      pl.BlockSpec((TOP_K, SC_GATHER_WINDOW), lambda i: (i, 0))],
            out_specs=[],
            core_axis_name=("c", "s"),
            dimension_semantics=(pltpu.PARALLEL,),
        )(rows_hbm, idx_hbm)

    return scatter(rows, idx_blocks)


def _combine_kernel(wt_ref, hf_ref, x1_ref, gt_ref, sg_ref, su_ref, sd_ref, g_ref, o_ref):
    tc = x1_ref.shape[0]
    y = _swiglu_packed(hf_ref[...], sg_ref, su_ref, sd_ref)
    wt = wt_ref[...]
    n = g_ref.shape[2]
    r_lo = jnp.zeros((tc, n), F32)
    r_hi = jnp.zeros((tc, n), F32)
    for k in range(TOP_K):
        lo, hi = _unpack_halves(g_ref[k])
        r_lo = r_lo + lo * wt[:, k:k + 1]
        r_hi = r_hi + hi * wt[:, k:k + 1]
    y = y + jnp.concatenate([r_lo, r_hi], axis=1)
    o_ref[...] = x1_ref[...] + gt_ref[...] * y


def _combine(wts_t, hfp, x1, gt, sg, su, sd, gathered, tc):
    s, d = x1.shape
    f = sg.shape[1]
    row = pl.BlockSpec((tc, d), lambda i: (i, 0))
    return pl.pallas_call(
        _combine_kernel,
        grid=(s // tc,),
        in_specs=[pl.BlockSpec((tc, TOP_K), lambda i: (i, 0)),
                  pl.BlockSpec((tc, d // 2), lambda i: (i, 0)), row,
                  pl.BlockSpec((1, d), lambda i: (0, 0)),
                  pl.BlockSpec((d, f), lambda i: (0, 0)),
                  pl.BlockSpec((d, f), lambda i: (0, 0)),
                  pl.BlockSpec((f, d), lambda i: (0, 0)),
                  pl.BlockSpec((TOP_K, tc, d // 2), lambda i: (0, i, 0))],
        out_specs=row,
        out_shape=jax.ShapeDtypeStruct((s, d), F32),
        compiler_params=_cparams(("arbitrary",)),
        name="combine",
    )(wts_t, hfp, x1, gt, sg, su, sd, gathered)


def _tile(n, want):
    t = min(n, want)
    assert n % t == 0, (n, t)
    return t


def _layer(l, x, c_col, pos_row, p):
    s, d = x.shape
    lambda_init = 0.8 - 0.6 * math.exp(-0.3 * l)
    gqk, gv = GLA_HEADS * GLA_DK, GLA_HEADS * GLA_DV
    dqk, dvw = DIFF_HEADS * 2 * DIFF_DH, DIFF_HEADS * DIFF_DV
    lowrank = p["gla_w_a2"].shape[0]

    mod = _ada(c_col, p["w_ada"], p["b_ada"][None, :])
    sh_a, sc_a, gt_a, sh_f, sc_f, gt_f = [mod[:, j * d:(j + 1) * d] for j in range(6)]

    w_in = p["w_in"]
    o = 0
    cols = {}
    for name, wdt in (("gq", gqk), ("gk", gqk), ("gv", gv), ("ga", lowrank), ("gg", gv),
                      ("dq", dqk), ("dk", dqk), ("dv", dvw), ("mg", d), ("md", d)):
        cols[name] = w_in[:, o:o + wdt]
        o += wdt
    row_names = ("gv", "gg", "mg", "md", "gq")
    w_row = jnp.concatenate([cols[n] for n in row_names], axis=1).astype(BF16)
    col_of, o = {}, 0
    for n in row_names:
        col_of[n] = o
        o += cols[n].shape[1]
    t_names = ("dq", "dk", "dv", "gk")
    w_t = jnp.concatenate([cols[n] for n in t_names], axis=1).T.astype(BF16)
    row_of, o = {}, 0
    for n in t_names:
        row_of[n] = o
        o += cols[n].shape[1]
    w_ga = jnp.pad(cols["ga"], ((0, 0), (0, 128 - lowrank))).astype(BF16)

    g1 = p["norm1_g"][None, :]
    tm = _tile(s, TILE_PROJ)
    ts = _tile(s, TILE_SEQ)
    proj, ga = _inproj(x, g1, sc_a, sh_a, w_row, w_ga, tm, w_row.shape[1] // 3)
    projt = _inproj_t(x, g1, sc_a, sh_a, w_t, tm, w_t.shape[0] // 2)

    wa2t = jnp.pad(p["gla_w_a2"].T, ((0, 0), (0, 128 - lowrank)))
    o_gla = _gla(proj, projt, ga, wa2t, p["gla_b_a"][:, None], p["gla_onorm_g"][None, :],
                 ts, col_of["gq"], col_of["gv"], col_of["gg"], row_of["gk"])

    invf = ROPE_THETA ** (-jnp.arange(0, ROT_DIM, 2, dtype=F32) / ROT_DIM)
    qat, qbt, kr, vte, qn, kn = _qkprep(projt, pos_row, invf[:, None], p["diff_qnorm_g"][:, None],
                                p["diff_knorm_g"][:, None], ts, row_of["dq"],
                                row_of["dk"], row_of["dv"])
    tq = ts
    o_diff = _diffattn(qat, qbt, kr, vte, qn, kn, p["diff_lq1"][None, :], p["diff_lk1"][None, :],
                       p["diff_lq2"][None, :], p["diff_lk2"][None, :],
                       p["diff_subln_g"][:, None], lambda_init, tq, tq)

    x1, hf, hfp = _mergeout(o_gla, o_diff, proj, x, p["w_branch_gla"].astype(BF16),
                       p["w_branch_diff"].astype(BF16), p["w_out"].astype(BF16), gt_a,
                       p["norm2_g"][None, :], sc_f, sh_f, ts,
                       col_of["mg"], col_of["md"])

    e = p["w_router"].shape[1]
    idx, wts, rnk, cnt = _route(hf, p["w_router"].T, p["router_bias"][:, None], ts)

    counts = cnt[:, 0]
    pcounts = ((counts + MOE_ROWS - 1) // MOE_ROWS) * MOE_ROWS
    pend = jnp.cumsum(pcounts)
    pstart = pend - pcounts
    pos = _positions(idx, rnk, pstart[:, None], ts)
    n_items = (s * TOP_K) // MOE_ROWS + e
    n_valid = (pend[-1] // MOE_ROWS).astype(jnp.int32)
    item_b = jnp.minimum(jnp.arange(n_items, dtype=jnp.int32), n_valid - 1)
    item_e = jnp.minimum(jnp.sum(pend[None, :] <= (item_b * MOE_ROWS)[:, None], axis=1),
                         e - 1).astype(jnp.int32)

    wn = SC_GATHER_WINDOW
    pos_w = pos.reshape(TOP_K, s // wn, wn).transpose(1, 0, 2).reshape(s // wn * TOP_K, wn)
    xs = _sc_scatter_rows(hfp, pos_w, n_items * MOE_ROWS)
    item_rows = jnp.clip(pstart[item_e] + counts[item_e] - item_b * MOE_ROWS, 0,
                         MOE_ROWS).astype(jnp.int32)
    prev_e = jnp.concatenate([jnp.full((1,), -1, jnp.int32), item_e[:-1]])
    item_first = ((jnp.arange(n_items) < n_valid) & (item_e != prev_e)).astype(jnp.int32)
    item_slot = ((jnp.cumsum(item_first) - 1) % 2).astype(jnp.int32)
    cand = jnp.where(pcounts > 0, jnp.arange(e, dtype=jnp.int32), e)
    following = jnp.concatenate([lax.cummin(cand[::-1])[::-1][1:], jnp.full((1,), e, jnp.int32)])
    item_next = jnp.where(following[item_e] < e, following[item_e], -1).astype(jnp.int32)
    ys = _moe(item_e, item_b, item_first, item_slot, item_next, item_rows, n_valid[None], xs,
              p["w_exp_gate"], p["w_exp_up"], p["w_exp_down"])
    gathered = _sc_gather_rows(ys, pos.reshape(1, TOP_K * s)).reshape(TOP_K, s, d // 2)
    return _combine(wts.T, hfp, x1, gt_f, p["w_sh_gate"].astype(BF16),
                    p["w_sh_up"].astype(BF16), p["w_sh_down"].astype(BF16), gathered,
                    _tile(s, TILE_COMBINE))


_LAYER_PARAMS = ("w_ada", "b_ada", "norm1_g", "w_in", "gla_w_a2", "gla_b_a", "gla_onorm_g",
                 "diff_qnorm_g", "diff_knorm_g", "diff_lq1", "diff_lk1", "diff_lq2", "diff_lk2",
                 "diff_subln_g", "w_branch_gla", "w_branch_diff", "w_out", "norm2_g", "w_router",
                 "router_bias", "w_exp_gate", "w_exp_up", "w_exp_down", "w_sh_gate", "w_sh_up",
                 "w_sh_down")


def kernel(x, c, positions, w_ada, b_ada, norm1_g, w_in, gla_w_a2, gla_b_a, gla_onorm_g, diff_qnorm_g, diff_knorm_g, diff_lq1, diff_lk1, diff_lq2, diff_lk2, diff_subln_g, w_branch_gla, w_branch_diff, w_out, norm2_g, w_router, router_bias, w_exp_gate, w_exp_up, w_exp_down, w_sh_gate, w_sh_up, w_sh_down):
    stacked = dict(zip(_LAYER_PARAMS, (
        w_ada, b_ada, norm1_g, w_in, gla_w_a2, gla_b_a, gla_onorm_g, diff_qnorm_g, diff_knorm_g,
        diff_lq1, diff_lk1, diff_lq2, diff_lk2, diff_subln_g, w_branch_gla, w_branch_diff, w_out,
        norm2_g, w_router, router_bias, w_exp_gate, w_exp_up, w_exp_down, w_sh_gate, w_sh_up,
        w_sh_down)))
    b, s, d = x.shape
    assert b == 1, "single-sequence kernel"
    xl = x[0]
    c_col = c[0][:, None]
    pos_row = positions.astype(jnp.int32)
    for l in range(w_ada.shape[0]):
        xl = _layer(l, xl, c_col, pos_row, {k: v[l] for k, v in stacked.items()})
    return xl[None]
```

```python
import functools
import math

import jax
import jax.numpy as jnp
from jax import lax
from jax.experimental import pallas as pl
from jax.experimental.pallas import tpu as pltpu
from jax.experimental.pallas import tpu_sc as plsc

CHUNK = 64
EPS = 1e-6
GLA_HEADS = 4
GLA_DK = 128
GLA_DV = 256
GLA_TAU = 16.0
DIFF_HEADS = 8
DIFF_DH = 64
DIFF_DV = 2 * DIFF_DH
ROPE_THETA = 500000.0
ROT_DIM = DIFF_DH // 4
N_GROUPS = 8
TOPK_GROUPS = 4
TOP_K = 8
ROUTED_SCALE = 2.5

MOE_ROWS = 640
MOE_DMA_CHUNKS = 4
MOE_SLOTS = 3
SC_GATHER_WINDOW = 64
VMEM_LIMIT = 56 * 1024 * 1024
TILE_PROJ = 1024
TILE_SEQ = 512
TILE_COMBINE = 256
NEG_BIG = -1e30
LOG2E = 1.4426950408889634
HIGHEST = lax.Precision.HIGHEST
F32 = jnp.float32
BF16 = jnp.bfloat16


def _cparams(sem):
    return pltpu.CompilerParams(dimension_semantics=sem, vmem_limit_bytes=VMEM_LIMIT)


def _nt_dot(a, b, precision=None):
    return lax.dot_general(a, b, (((1,), (1,)), ((), ())), precision=precision,
                           preferred_element_type=F32)


def _pack_halves(x):
    n = x.shape[1] // 2
    lo = pltpu.bitcast(x[:, :n].astype(BF16).astype(F32), jnp.uint32) >> 16
    hi = pltpu.bitcast(x[:, n:].astype(BF16).astype(F32), jnp.uint32) & jnp.uint32(0xFFFF0000)
    return lo | hi


def _unpack_halves(w):
    return (pltpu.bitcast(w << 16, F32), pltpu.bitcast(w & jnp.uint32(0xFFFF0000), F32))


def _rms_mod(x, g, sc, sh):
    xn = x * lax.rsqrt(jnp.mean(x * x, axis=-1, keepdims=True) + EPS)
    return (xn * g) * (1.0 + sc) + sh


def _ada_kernel(c_ref, w_ref, b_ref, o_ref):
    c = c_ref[...]
    ca = c * jax.nn.sigmoid(c)
    o_ref[...] = jnp.sum(ca * w_ref[...], axis=0, keepdims=True) + b_ref[...]


def _ada(c_col, w, b):
    d, n = w.shape
    tn = min(1024, n)
    return pl.pallas_call(
        _ada_kernel,
        grid=(n // tn,),
        in_specs=[pl.BlockSpec((d, 1), lambda j: (0, 0)),
                  pl.BlockSpec((d, tn), lambda j: (0, j)),
                  pl.BlockSpec((1, tn), lambda j: (0, j))],
        out_specs=pl.BlockSpec((1, tn), lambda j: (0, j)),
        out_shape=jax.ShapeDtypeStruct((1, n), F32),
        compiler_params=_cparams(("arbitrary",)),
        name="ada",
    )(c_col, w, b)


def _inproj_kernel(x_ref, g_ref, sc_ref, sh_ref, w_ref, wga_ref, o_ref, ga_ref, h_scr):
    @pl.when(pl.program_id(1) == 0)
    def _():
        h = _rms_mod(x_ref[...], g_ref[...], sc_ref[...], sh_ref[...]).astype(BF16)
        h_scr[...] = h
        ga_ref[...] = jnp.dot(h, wga_ref[...], preferred_element_type=F32)

    o_ref[...] = jnp.dot(h_scr[...], w_ref[...], preferred_element_type=F32).astype(BF16)


def _inproj(x, g, sc, sh, w, wga, tm, tn):
    s, d = x.shape
    n = w.shape[1]
    vec = pl.BlockSpec((1, d), lambda i, j: (0, 0))
    return pl.pallas_call(
        _inproj_kernel,
        grid=(s // tm, n // tn),
        in_specs=[pl.BlockSpec((tm, d), lambda i, j: (i, 0)), vec, vec, vec,
                  pl.BlockSpec((d, tn), lambda i, j: (0, j)),
                  pl.BlockSpec((d, 128), lambda i, j: (0, 0))],
        out_specs=[pl.BlockSpec((tm, tn), lambda i, j: (i, j)),
                   pl.BlockSpec((tm, 128), lambda i, j: (i, 0))],
        out_shape=[jax.ShapeDtypeStruct((s, n), BF16), jax.ShapeDtypeStruct((s, 128), F32)],
        scratch_shapes=[pltpu.VMEM((tm, d), BF16)],
        compiler_params=_cparams(("arbitrary", "arbitrary")),
        name="inproj",
    )(x, g, sc, sh, w, wga)


def _inproj_t_kernel(x_ref, g_ref, sc_ref, sh_ref, wt_ref, o_ref, h_scr):
    @pl.when(pl.program_id(1) == 0)
    def _():
        h_scr[...] = _rms_mod(x_ref[...], g_ref[...], sc_ref[...], sh_ref[...]).astype(BF16)

    o_ref[...] = _nt_dot(wt_ref[...], h_scr[...]).astype(BF16)


def _inproj_t(x, g, sc, sh, wt, tm, tn):
    s, d = x.shape
    n = wt.shape[0]
    vec = pl.BlockSpec((1, d), lambda i, j: (0, 0))
    return pl.pallas_call(
        _inproj_t_kernel,
        grid=(s // tm, n // tn),
        in_specs=[pl.BlockSpec((tm, d), lambda i, j: (i, 0)), vec, vec, vec,
                  pl.BlockSpec((tn, d), lambda i, j: (j, 0))],
        out_specs=pl.BlockSpec((tn, tm), lambda i, j: (j, i)),
        out_shape=jax.ShapeDtypeStruct((n, s), BF16),
        scratch_shapes=[pltpu.VMEM((tm, d), BF16)],
        compiler_params=_cparams(("arbitrary", "arbitrary")),
        name="inproj_t",
    )(x, g, sc, sh, wt)


def _gla_kernel(q_ref, kt_ref, v_ref, gg_ref, ga_ref, wa2t_ref, ba_ref, on_ref, o_ref,
                state_ref, o_scr):
    tt = q_ref.shape[0]
    nchunk = tt // CHUNK

    @pl.when(pl.program_id(0) == 0)
    def _():
        state_ref[...] = jnp.zeros_like(state_ref)

    zt = _nt_dot(wa2t_ref[...], ga_ref[...], precision=HIGHEST) + ba_ref[...]
    lat = (jnp.minimum(zt, 0.0) - jnp.log1p(jnp.exp(-jnp.abs(zt)))) * (1.0 / GLA_TAU)
    row = lax.broadcasted_iota(jnp.int32, (tt, tt), 0)
    col = lax.broadcasted_iota(jnp.int32, (tt, tt), 1)
    same = (row // CHUNK) == (col // CHUNK)
    incl = jnp.where(same & (row <= col), 1.0, 0.0).astype(BF16)
    full = jnp.where(same, 1.0, 0.0).astype(BF16)
    lat_hi = lat.astype(BF16)
    lat_lo = (lat - lat_hi.astype(F32)).astype(BF16)
    cumt = (jnp.dot(lat_hi, incl, preferred_element_type=F32)
            + jnp.dot(lat_lo, incl, preferred_element_type=F32))
    tott = (jnp.dot(lat_hi, full, preferred_element_type=F32)
            + jnp.dot(lat_lo, full, preferred_element_type=F32))
    kdt = kt_ref[...].astype(F32) * jnp.exp(tott - cumt)
    dec = jnp.exp(tott)

    lane = lax.broadcasted_iota(jnp.int32, (GLA_DK, 2 * CHUNK), 1)
    upd = {}
    for c in range(nchunk):
        pair = (c // 2) * 2 * CHUNK
        if nchunk > 1:
            keep = (lane // CHUNK) == (c % 2)
        for h in range(GLA_HEADS):
            rows = slice(h * GLA_DK, (h + 1) * GLA_DK)
            vcols = slice(h * GLA_DV, (h + 1) * GLA_DV)
            if nchunk > 1:
                a = jnp.where(keep, kdt[rows, pair:pair + 2 * CHUNK], 0.0).astype(BF16)
                vp = v_ref[pair:pair + 2 * CHUNK, vcols]
            else:
                a = kdt[rows, :].astype(BF16)
                vp = v_ref[:, vcols]
            upd[c, h] = jnp.dot(a, vp, preferred_element_type=F32)

    for h in range(GLA_HEADS):
        rows = slice(h * GLA_DK, (h + 1) * GLA_DK)
        vcols = slice(h * GLA_DV, (h + 1) * GLA_DV)
        st = state_ref[h]
        states = []
        for c in range(nchunk):
            st = st * dec[rows, c * CHUNK:c * CHUNK + 1] + upd[c, h]
            states.append(st.astype(BF16))
        state_ref[h] = st
        for c in range(nchunk):
            o_scr[c * CHUNK:(c + 1) * CHUNK, vcols] = jnp.dot(
                q_ref[c * CHUNK:(c + 1) * CHUNK, rows], states[c], preferred_element_type=F32)

    for h in range(GLA_HEADS):
        vcols = slice(h * GLA_DV, (h + 1) * GLA_DV)
        o = o_scr[:, vcols] * (GLA_DK ** -0.5)
        o = o * lax.rsqrt(jnp.mean(o * o, axis=-1, keepdims=True) + EPS) * on_ref[...]
        g = gg_ref[:, vcols].astype(F32)
        o_ref[:, vcols] = (o * (g * jax.nn.sigmoid(g))).astype(BF16)


def _gla(proj, projt, ga, wa2t, ba_col, on_g, tt, col_q, col_v, col_g, row_k):
    s = proj.shape[0]
    qk = GLA_HEADS * GLA_DK
    vw = GLA_HEADS * GLA_DV
    return pl.pallas_call(
        _gla_kernel,
        grid=(s // tt,),
        in_specs=[pl.BlockSpec((tt, qk), lambda i: (i, col_q // qk)),
                  pl.BlockSpec((qk, tt), lambda i: (row_k // qk, i)),
                  pl.BlockSpec((tt, vw), lambda i: (i, col_v // vw)),
                  pl.BlockSpec((tt, vw), lambda i: (i, col_g // vw)),
                  pl.BlockSpec((tt, 128), lambda i: (i, 0)),
                  pl.BlockSpec((qk, 128), lambda i: (0, 0)),
                  pl.BlockSpec((qk, 1), lambda i: (0, 0)),
                  pl.BlockSpec((1, GLA_DV), lambda i: (0, 0))],
        out_specs=pl.BlockSpec((tt, vw), lambda i: (i, 0)),
        out_shape=jax.ShapeDtypeStruct((s, vw), BF16),
        scratch_shapes=[pltpu.VMEM((GLA_HEADS, GLA_DK, GLA_DV), F32),
                        pltpu.VMEM((tt, vw), F32)],
        compiler_params=_cparams(("arbitrary",)),
        name="gla",
    )(proj, projt, proj, proj, ga, wa2t, ba_col, on_g)


def _qknorm_rope_t(xt, g_col, cos, sin):
    n, tm = xt.shape
    x3 = xt.reshape(n // DIFF_DH, DIFF_DH, tm)
    r = lax.rsqrt(jnp.mean(x3 * x3, axis=1, keepdims=True) + EPS)
    y = x3 * r * g_col[None]
    half = ROT_DIM // 2
    y1, y2, rest = y[:, :half], y[:, half:ROT_DIM], y[:, ROT_DIM:]
    o1 = y1 * cos[None] - y2 * sin[None]
    o2 = y2 * cos[None] + y1 * sin[None]
    return jnp.concatenate([o1, o2, rest], axis=1)


def _seg_norms(x3):
    return jnp.sqrt(jnp.sum(x3 * x3, axis=1)).reshape(DIFF_HEADS, 2, x3.shape[2])


def _qkprep_kernel(qt_ref, kt_ref, vt_ref, pos_ref, invf_ref, qg_ref, kg_ref, qa_ref, qb_ref,
                   ko_ref, ve_ref, qn_ref, kn_ref):
    tm = qt_ref.shape[1]
    v3 = vt_ref[...].reshape(DIFF_HEADS, DIFF_DV, tm)
    ones = jnp.ones((DIFF_HEADS, ATT_SUM_ROWS, tm), BF16)
    ve_ref[...] = jnp.concatenate([v3, ones], axis=1).reshape(-1, tm)
    ang = pos_ref[...].astype(F32) * invf_ref[...]
    cos, sin = jnp.cos(ang), jnp.sin(ang)
    k3 = _qknorm_rope_t(kt_ref[...].astype(F32), kg_ref[...], cos, sin)
    ko_ref[...] = k3.reshape(-1, tm).T.astype(BF16)
    kn_ref[...] = _seg_norms(k3)
    q3 = _qknorm_rope_t(qt_ref[...].astype(F32), qg_ref[...], cos, sin) * (
        DIFF_DH ** -0.5 * LOG2E)
    qn_ref[...] = _seg_norms(q3)
    seg = lax.broadcasted_iota(jnp.int32, q3.shape, 0)
    qa_ref[...] = jnp.where(seg % 2 == 0, q3, 0.0).reshape(-1, tm).astype(BF16)
    qb_ref[...] = jnp.where(seg % 2 == 1, q3, 0.0).reshape(-1, tm).astype(BF16)


def _qkprep(projt, pos_row, invf_col, qg_col, kg_col, tm, row_q, row_k, row_v):
    s = projt.shape[1]
    n = DIFF_HEADS * 2 * DIFF_DH
    ne = DIFF_HEADS * (DIFF_DV + ATT_SUM_ROWS)
    col = pl.BlockSpec((DIFF_DH, 1), lambda i: (0, 0))
    return pl.pallas_call(
        _qkprep_kernel,
        grid=(s // tm,),
        in_specs=[pl.BlockSpec((n, tm), lambda i: (row_q // n, i)),
                  pl.BlockSpec((n, tm), lambda i: (row_k // n, i)),
                  pl.BlockSpec((n, tm), lambda i: (row_v // n, i)),
                  pl.BlockSpec((1, tm), lambda i: (0, i)),
                  pl.BlockSpec((ROT_DIM // 2, 1), lambda i: (0, 0)), col, col],
        out_specs=[pl.BlockSpec((n, tm), lambda i: (0, i)),
                   pl.BlockSpec((n, tm), lambda i: (0, i)),
                   pl.BlockSpec((tm, n), lambda i: (i, 0)),
                   pl.BlockSpec((ne, tm), lambda i: (0, i)),
                   pl.BlockSpec((DIFF_HEADS, 2, tm), lambda i: (0, 0, i)),
                   pl.BlockSpec((DIFF_HEADS, 2, tm), lambda i: (0, 0, i))],
        out_shape=[jax.ShapeDtypeStruct((n, s), BF16), jax.ShapeDtypeStruct((n, s), BF16),
                   jax.ShapeDtypeStruct((s, n), BF16), jax.ShapeDtypeStruct((ne, s), BF16),
                   jax.ShapeDtypeStruct((DIFF_HEADS, 2, s), F32),
                   jax.ShapeDtypeStruct((DIFF_HEADS, 2, s), F32)],
        compiler_params=_cparams(("arbitrary",)),
        name="qkprep",
    )(projt, projt, projt, pos_row, invf_col, qg_col, kg_col)


ATT_COLS = 256
ATT_LOOKAHEAD = {True: 2, False: 4}
ATT_BODY_TILES = {True: 8, False: 2}
ATT_SUM_ROWS = 16
ATT_BOUND_SLACK = 1.02
ATT_BOUND_LIMIT = 50.0


def _diffattn_kernel(qa_ref, qb_ref, k_ref, vt_ref, qn_ref, kn_ref, lq1_ref, lk1_ref, lq2_ref,
                     lk2_ref, sg_ref, o_ref, *scr, lambda_init, tk, cols):
    tq = qa_ref.shape[1]
    nblk = 2 * tq // cols
    m_scr, acc_scr = (scr[b * nblk:(b + 1) * nblk] for b in range(2))
    kmax_scr = scr[2 * nblk]
    s_scr = scr[2 * nblk + 1:]
    i = pl.program_id(1)

    @pl.when(i == 0)
    def _():
        kmax_scr[...] = jnp.max(kn_ref[...], axis=1, keepdims=True)

    bound = qn_ref[...] * kmax_scr[...] * ATT_BOUND_SLACK
    bound = jnp.concatenate([bound[0:1], bound[1:2]], axis=1)
    bounded = jnp.max(bound) < ATT_BOUND_LIMIT
    for c in range(nblk):
        m_scr[c][...] = jnp.where(bounded, bound[:, c * cols:(c + 1) * cols], NEG_BIG)
        acc_scr[c][...] = jnp.zeros_like(acc_scr[c])

    def scores(j, c):
        start = pl.multiple_of(j * tk, tk)
        q_ref = qa_ref if c * cols < tq else qb_ref
        off = (c * cols) % tq
        return jnp.dot(k_ref[pl.ds(start, tk), :], q_ref[:, off:off + cols],
                       preferred_element_type=F32)

    def steps(tiles, masked, next_tile, fixed):
        look = ATT_LOOKAHEAD[fixed]
        items = [(j, c) for j in tiles for c in range(nblk)]
        pending = []
        for n, (j, c) in enumerate(items):
            s = s_scr[n][...] if n < look else pending.pop(0)
            ahead = n + look
            if ahead < len(items):
                pending.append(scores(*items[ahead]))
            elif next_tile is not None:
                s_scr[ahead - len(items)][...] = scores(next_tile, ahead - len(items))
            start = pl.multiple_of(j * tk, tk)
            keys = tk
            if masked:
                keys = (c * cols) % tq + cols
                s = s[:keys]
                krow = lax.broadcasted_iota(jnp.int32, (keys, cols), 0)
                qcol = lax.broadcasted_iota(jnp.int32, (keys, cols), 1)
                qpos = i * tq + (c * cols) % tq + qcol
                s = jnp.where((start + krow) // CHUNK <= qpos // CHUNK, s, NEG_BIG)
            vj = vt_ref[:, pl.ds(start, keys)]
            if fixed:
                p = jnp.exp2(s - m_scr[c][...]).astype(BF16)
                acc_scr[c][...] += jnp.dot(vj, p, preferred_element_type=F32)
            else:
                m_prev = m_scr[c][...]
                m_new = jnp.maximum(m_prev, jnp.max(s, axis=0, keepdims=True))
                alpha = jnp.exp2(m_prev - m_new)
                p = jnp.exp2((s - m_new).astype(BF16))
                acc_scr[c][...] = alpha * acc_scr[c][...] + jnp.dot(
                    vj, p, preferred_element_type=F32)
                m_scr[c][...] = m_new

    def attend(fixed):
        for c in range(ATT_LOOKAHEAD[fixed]):
            s_scr[c][...] = scores(0, c)
        big = ATT_BODY_TILES[fixed]
        lax.fori_loop(0, i // big, lambda t, c: (steps(
            tuple(big * t + u for u in range(big)), False, big * t + big, fixed), c)[1], 0)
        size = big // 2
        while size >= 1:
            first = (i // (2 * size)) * (2 * size)

            @pl.when(i % (2 * size) >= size)
            def _(first=first, size=size):
                steps(tuple(first + u for u in range(size)), False, first + size, fixed)

            size //= 2
        steps((i,), True, None, fixed)

    @pl.when(bounded)
    def _():
        attend(True)

    @pl.when(jnp.logical_not(bounded))
    def _():
        attend(False)

    o = jnp.concatenate([acc_scr[c][:DIFF_DV] / acc_scr[c][DIFF_DV:DIFF_DV + 1]
                         for c in range(nblk)], axis=1)
    lam = (jnp.exp(jnp.sum(lq1_ref[...] * lk1_ref[...]))
           - jnp.exp(jnp.sum(lq2_ref[...] * lk2_ref[...])) + lambda_init)
    o = o[:, :tq] - lam * o[:, tq:]
    o = o * lax.rsqrt(jnp.mean(o * o, axis=0, keepdims=True) + EPS) * sg_ref[...]
    o_ref[...] = (o * (1.0 - lambda_init)).T.astype(BF16)


def _diffattn(qat, qbt, kr, vte, qn, kn, lq1, lk1, lq2, lk2, sg_col, lambda_init, tq, tk):
    s = kr.shape[0]
    hd = 2 * DIFF_DH
    vec = pl.BlockSpec((1, DIFF_DH), lambda h, i: (0, 0))
    cols = min(ATT_COLS, tq)
    nblk = 2 * tq // cols
    kern = functools.partial(_diffattn_kernel, lambda_init=lambda_init, tk=tk, cols=cols)
    dve = DIFF_DV + ATT_SUM_ROWS
    assert tq == tk and nblk >= max(ATT_LOOKAHEAD.values())
    scratch = ([pltpu.VMEM((1, cols), F32)] * nblk + [pltpu.VMEM((dve, cols), F32)] * nblk
               + [pltpu.VMEM((2, 1), F32)]
               + [pltpu.VMEM((tk, cols), F32)] * max(ATT_LOOKAHEAD.values()))
    return pl.pallas_call(
        kern,
        grid=(DIFF_HEADS, s // tq),
        in_specs=[pl.BlockSpec((hd, tq), lambda h, i: (h, i)),
                  pl.BlockSpec((hd, tq), lambda h, i: (h, i)),
                  pl.BlockSpec((s, hd), lambda h, i: (0, h)),
                  pl.BlockSpec((dve, s), lambda h, i: (h, 0)),
                  pl.BlockSpec((None, 2, tq), lambda h, i: (h, 0, i)),
                  pl.BlockSpec((None, 2, s), lambda h, i: (h, 0, 0)),
                  vec, vec, vec, vec,
                  pl.BlockSpec((DIFF_DV, 1), lambda h, i: (0, 0))],
        out_specs=pl.BlockSpec((tq, DIFF_DV), lambda h, i: (i, h)),
        out_shape=jax.ShapeDtypeStruct((s, DIFF_HEADS * DIFF_DV), BF16),
        scratch_shapes=scratch,
        compiler_params=_cparams(("arbitrary", "arbitrary")),
        name="diffattn",
    )(qat, qbt, kr, vte, qn, kn, lq1, lk1, lq2, lk2, sg_col)


def _mergeout_kernel(og_ref, od_ref, mg_ref, md_ref, x_ref, wbg_ref, wbd_ref, wo_ref, gt_ref,
                     g2_ref, sc_ref, sh_ref, x1_ref, hf_ref, hfp_ref):
    bg = jnp.dot(og_ref[...], wbg_ref[...], preferred_element_type=F32)
    bd = jnp.dot(od_ref[...], wbd_ref[...], preferred_element_type=F32)
    merged = (jax.nn.sigmoid(mg_ref[...].astype(F32)) * bg
              + jax.nn.sigmoid(md_ref[...].astype(F32)) * bd)
    x1 = x_ref[...] + gt_ref[...] * jnp.dot(merged.astype(BF16), wo_ref[...],
                                             preferred_element_type=F32)
    x1_ref[...] = x1
    hf = _rms_mod(x1, g2_ref[...], sc_ref[...], sh_ref[...])
    hf_ref[...] = hf
    hfp_ref[...] = _pack_halves(hf)


def _mergeout(og, od, proj, x, wbg, wbd, wo, gt, g2, sc, sh, tm, col_mg, col_md):
    s, d = x.shape
    vec = pl.BlockSpec((1, d), lambda i: (0, 0))
    wspec = pl.BlockSpec((d, d), lambda i: (0, 0))
    row = pl.BlockSpec((tm, d), lambda i: (i, 0))
    return pl.pallas_call(
        _mergeout_kernel,
        grid=(s // tm,),
        in_specs=[row, row,
                  pl.BlockSpec((tm, d), lambda i: (i, col_mg // d)),
                  pl.BlockSpec((tm, d), lambda i: (i, col_md // d)),
                  row, wspec, wspec, wspec, vec, vec, vec, vec],
        out_specs=[row, row, pl.BlockSpec((tm, d // 2), lambda i: (i, 0))],
        out_shape=[jax.ShapeDtypeStruct((s, d), F32), jax.ShapeDtypeStruct((s, d), F32),
                   jax.ShapeDtypeStruct((s, d // 2), jnp.uint32)],
        compiler_params=_cparams(("arbitrary",)),
        name="mergeout",
    )(og, od, proj, proj, x, wbg, wbd, wo, gt, g2, sc, sh)


def _route_kernel(hf_ref, wrt_ref, bias_ref, idx_ref, wts_ref, rnk_ref, cnt_ref, run_scr):
    tr = hf_ref.shape[0]
    e = wrt_ref.shape[0]
    gsz = e // N_GROUPS

    @pl.when(pl.program_id(0) == 0)
    def _():
        run_scr[...] = jnp.zeros_like(run_scr)

    logits = _nt_dot(wrt_ref[...], hf_ref[...], precision=HIGHEST)
    scores = jax.nn.sigmoid(logits)
    biased = scores + bias_ref[...]
    g3 = biased.reshape(N_GROUPS, gsz, tr)
    m1 = jnp.max(g3, axis=1, keepdims=True)
    n_top = jnp.sum(jnp.where(g3 == m1, 1.0, 0.0), axis=1, keepdims=True)
    m2 = jnp.max(jnp.where(g3 < m1, g3, -jnp.inf), axis=1, keepdims=True)
    gs = (m1 + jnp.where(n_top >= 2.0, m1, m2)).reshape(N_GROUPS, tr)
    gi = lax.broadcasted_iota(jnp.int32, (N_GROUPS, tr), 0)
    beaten = jnp.zeros((N_GROUPS, tr), F32)
    for g in range(N_GROUPS):
        other = gs[g:g + 1, :]
        beaten = beaten + jnp.where((other > gs) | ((other == gs) & (g < gi)), 1.0, 0.0)
    gsel = (beaten < float(TOPK_GROUPS)).reshape(N_GROUPS, 1, tr)
    masked = jnp.where(gsel, g3, -jnp.inf).reshape(e, tr)

    ids = lax.broadcasted_iota(jnp.int32, (e, tr), 0)
    chosen = jnp.zeros((e, tr), F32)
    sel_idx, sel_score = [], []
    for _ in range(TOP_K):
        mx = jnp.max(masked, axis=0, keepdims=True)
        ix = jnp.min(jnp.where(masked == mx, ids, e), axis=0, keepdims=True)
        hit = ids == ix
        sel_idx.append(ix)
        sel_score.append(jnp.sum(jnp.where(hit, scores, 0.0), axis=0, keepdims=True))
        chosen = jnp.where(hit, 1.0, chosen)
        masked = jnp.where(hit, -jnp.inf, masked)
    idx = jnp.concatenate(sel_idx, axis=0)
    sc = jnp.concatenate(sel_score, axis=0)
    idx_ref[...] = idx
    wts_ref[...] = sc / jnp.sum(sc, axis=0, keepdims=True) * ROUTED_SCALE

    row = lax.broadcasted_iota(jnp.int32, (tr, tr), 0)
    col = lax.broadcasted_iota(jnp.int32, (tr, tr), 1)
    before = jnp.where(row < col, 1.0, 0.0).astype(BF16)
    prior = jnp.dot(chosen.astype(BF16), before, preferred_element_type=F32) + run_scr[:, 0:1]
    rnk_ref[...] = jnp.concatenate(
        [jnp.sum(jnp.where(ids == sel_idx[k], prior, 0.0), axis=0, keepdims=True)
         for k in range(TOP_K)], axis=0).astype(jnp.int32)
    run_scr[...] = run_scr[...] + jnp.sum(chosen, axis=1, keepdims=True)
    cnt_ref[...] = run_scr[...].astype(jnp.int32)


def _route(hf, wrt, bias_col, tr):
    s, d = hf.shape
    e = wrt.shape[0]
    tok = pl.BlockSpec((TOP_K, tr), lambda i: (0, i))
    return pl.pallas_call(
        _route_kernel,
        grid=(s // tr,),
        in_specs=[pl.BlockSpec((tr, d), lambda i: (i, 0)),
                  pl.BlockSpec((e, d), lambda i: (0, 0)),
                  pl.BlockSpec((e, 1), lambda i: (0, 0))],
        out_specs=[tok, tok, tok, pl.BlockSpec((e, 128), lambda i: (0, 0))],
        out_shape=[jax.ShapeDtypeStruct((TOP_K, s), jnp.int32),
                   jax.ShapeDtypeStruct((TOP_K, s), F32),
                   jax.ShapeDtypeStruct((TOP_K, s), jnp.int32),
                   jax.ShapeDtypeStruct((e, 128), jnp.int32)],
        scratch_shapes=[pltpu.VMEM((e, 128), F32)],
        compiler_params=_cparams(("arbitrary",)),
        name="route",
    )(hf, wrt, bias_col)


def _positions_kernel(idx_ref, rnk_ref, pstart_ref, pos_ref):
    e = pstart_ref.shape[0]
    ts = idx_ref.shape[1]
    ids = lax.broadcasted_iota(jnp.int32, (e, ts), 0)
    idx = idx_ref[...]
    pos_ref[...] = rnk_ref[...] + jnp.concatenate(
        [jnp.sum(jnp.where(ids == idx[k:k + 1, :], pstart_ref[...], 0), axis=0, keepdims=True)
         for k in range(TOP_K)], axis=0)


def _positions(idx, rnk, pstart_col, ts):
    s = idx.shape[1]
    e = pstart_col.shape[0]
    tok = pl.BlockSpec((TOP_K, ts), lambda i: (0, i))
    return pl.pallas_call(
        _positions_kernel,
        grid=(s // ts,),
        in_specs=[tok, tok, pl.BlockSpec((e, 1), lambda i: (0, 0))],
        out_specs=tok,
        out_shape=jax.ShapeDtypeStruct((TOP_K, s), jnp.int32),
        compiler_params=_cparams(("arbitrary",)),
        name="positions",
    )(idx, rnk, pstart_col)


def _swiglu_packed(xp, wg, wu, wd):
    lo, hi = _unpack_halves(xp)
    lo, hi = lo.astype(BF16), hi.astype(BF16)
    n = lo.shape[1]
    g = (jnp.dot(lo, wg[:n], preferred_element_type=F32)
         + jnp.dot(hi, wg[n:], preferred_element_type=F32))
    u = (jnp.dot(lo, wu[:n], preferred_element_type=F32)
         + jnp.dot(hi, wu[n:], preferred_element_type=F32))
    h = (g * jax.nn.sigmoid(g)) * u
    return jnp.dot(h.astype(BF16), wd[...], preferred_element_type=F32)


def _moe_kernel(ie_ref, ib_ref, first_ref, slot_ref, ne_ref, lead_ref, rows_ref, nv_ref, xs_ref,
                wg_hbm, wu_hbm, wd_hbm, ys_ref, wg_f, wu_f, wd_f, sem):
    del ib_ref
    i = pl.program_id(0)

    def fetch(e, slot):
        copies = []
        for n, (src, dst) in enumerate(((wg_hbm, wg_f), (wu_hbm, wu_f), (wd_hbm, wd_f))):
            rows = src.shape[1] // MOE_DMA_CHUNKS
            for ch in range(MOE_DMA_CHUNKS):
                part = pl.ds(ch * rows, rows)
                copies.append(pltpu.make_async_copy(
                    src.at[e, part], dst.at[slot, part], sem.at[slot, n * MOE_DMA_CHUNKS + ch]))
        return copies

    @pl.when(i == 0)
    def _():
        for s in range(MOE_SLOTS - 1):
            @pl.when(lead_ref[s] >= 0)
            def _(s=s):
                for cp in fetch(lead_ref[s], s):
                    cp.start()

    @pl.when(i < nv_ref[0])
    def _():
        for slot in range(MOE_SLOTS):
            @pl.when((first_ref[i] == 1) & (slot_ref[i] == slot))
            def _(slot=slot):
                for cp in fetch(ie_ref[i], slot):
                    cp.wait()

                @pl.when(ne_ref[i] >= 0)
                def _():
                    for cp in fetch(ne_ref[i], (slot + MOE_SLOTS - 1) % MOE_SLOTS):
                        cp.start()

        slot = slot_ref[i]
        row = lax.broadcasted_iota(jnp.int32, xs_ref.shape, 0)
        xp = jnp.where(row < rows_ref[i], xs_ref[...], jnp.uint32(0))
        ys_ref[...] = _pack_halves(_swiglu_packed(
            xp, wg_f[slot].astype(BF16), wu_f[slot].astype(BF16), wd_f[slot].astype(BF16)))


def _moe(item_e, item_b, item_first, item_slot, item_next, lead, item_rows, n_valid, xs, wg, wu,
         wd):
    m_pad, dh = xs.shape
    _, d, f = wg.shape
    n_items = item_e.shape[0]
    blk = lambda i, ie, ib, fi, sl, ne, ld, nr, nv: (ib[i], 0)
    hbm = pl.BlockSpec(memory_space=pl.ANY)
    return pl.pallas_call(
        _moe_kernel,
        grid_spec=pltpu.PrefetchScalarGridSpec(
            num_scalar_prefetch=8,
            grid=(n_items,),
            in_specs=[pl.BlockSpec((MOE_ROWS, dh), blk), hbm, hbm, hbm],
            out_specs=pl.BlockSpec((MOE_ROWS, dh), blk),
            scratch_shapes=[pltpu.VMEM((MOE_SLOTS, d, f), F32), pltpu.VMEM((MOE_SLOTS, d, f), F32),
                            pltpu.VMEM((MOE_SLOTS, f, d), F32),
                            pltpu.SemaphoreType.DMA((MOE_SLOTS, 3 * MOE_DMA_CHUNKS))],
        ),
        out_shape=jax.ShapeDtypeStruct((m_pad, dh), jnp.uint32),
        compiler_params=_cparams(("arbitrary",)),
        name="moe",
    )(item_e, item_b, item_first, item_slot, item_next, lead, item_rows, n_valid, xs, wg, wu, wd)


def _sc_gather_rows(table, idx_row):
    m = idx_row.shape[1]
    w = table.shape[1]
    idx_row = idx_row.reshape(m // SC_GATHER_WINDOW, SC_GATHER_WINDOW)
    mesh = plsc.VectorSubcoreMesh(core_axis_name="c", subcore_axis_name="s")

    @functools.partial(pl.kernel, mesh=mesh,
                       out_type=jax.ShapeDtypeStruct((m, w), table.dtype))
    def gather(table_hbm, idx_hbm, out_hbm):
        def body(idx_vmem, out_vmem):
            pltpu.sync_copy(table_hbm.at[idx_vmem.at[0]], out_vmem)

        pltpu.emit_pipeline(
            body,
            grid=(m // SC_GATHER_WINDOW,),
            in_specs=[pl.BlockSpec((1, SC_GATHER_WINDOW), lambda i: (i, 0))],
            out_specs=[pl.BlockSpec((SC_GATHER_WINDOW, w), lambda i: (i, 0))],
            core_axis_name=("c", "s"),
            dimension_semantics=(pltpu.PARALLEL,),
        )(idx_hbm, out_hbm)

    return gather(table, idx_row)


def _sc_scatter_rows(rows, idx_blocks, m_out):
    s, w = rows.shape
    mesh = plsc.VectorSubcoreMesh(core_axis_name="c", subcore_axis_name="s")

    @functools.partial(pl.kernel, mesh=mesh,
                       out_type=jax.ShapeDtypeStruct((m_out, w), rows.dtype))
    def scatter(rows_hbm, idx_hbm, out_hbm):
        def body(rows_vmem, idx_vmem):
            for k in range(TOP_K):
                pltpu.sync_copy(rows_vmem, out_hbm.at[idx_vmem.at[k]])

        pltpu.emit_pipeline(
            body,
            grid=(s // SC_GATHER_WINDOW,),
            in_specs=[pl.BlockSpec((SC_GATHER_WINDOW, w), lambda i: (i, 0)),
                      pl.BlockSpec((TOP_K, SC_GATHER_WINDOW), lambda i: (i, 0))],
            out_specs=[],
            core_axis_name=("c", "s"),
            dimension_semantics=(pltpu.PARALLEL,),
        )(rows_hbm, idx_hbm)

    return scatter(rows, idx_blocks)


def _combine_kernel(wt_ref, hf_ref, x1_ref, gt_ref, sg_ref, su_ref, sd_ref, g_ref, o_ref):
    tc = x1_ref.shape[0]
    y = _swiglu_packed(hf_ref[...], sg_ref, su_ref, sd_ref)
    wt = wt_ref[...]
    n = g_ref.shape[2]
    r_lo = jnp.zeros((tc, n), F32)
    r_hi = jnp.zeros((tc, n), F32)
    for k in range(TOP_K):
        lo, hi = _unpack_halves(g_ref[k])
        r_lo = r_lo + lo * wt[:, k:k + 1]
        r_hi = r_hi + hi * wt[:, k:k + 1]
    y = y + jnp.concatenate([r_lo, r_hi], axis=1)
    o_ref[...] = x1_ref[...] + gt_ref[...] * y


def _combine(wts_t, hfp, x1, gt, sg, su, sd, gathered, tc):
    s, d = x1.shape
    f = sg.shape[1]
    row = pl.BlockSpec((tc, d), lambda i: (i, 0))
    return pl.pallas_call(
        _combine_kernel,
        grid=(s // tc,),
        in_specs=[pl.BlockSpec((tc, TOP_K), lambda i: (i, 0)),
                  pl.BlockSpec((tc, d // 2), lambda i: (i, 0)), row,
                  pl.BlockSpec((1, d), lambda i: (0, 0)),
                  pl.BlockSpec((d, f), lambda i: (0, 0)),
                  pl.BlockSpec((d, f), lambda i: (0, 0)),
                  pl.BlockSpec((f, d), lambda i: (0, 0)),
                  pl.BlockSpec((TOP_K, tc, d // 2), lambda i: (0, i, 0))],
        out_specs=row,
        out_shape=jax.ShapeDtypeStruct((s, d), F32),
        compiler_params=_cparams(("arbitrary",)),
        name="combine",
    )(wts_t, hfp, x1, gt, sg, su, sd, gathered)


def _tile(n, want):
    t = min(n, want)
    assert n % t == 0, (n, t)
    return t


def _layer(l, x, c_col, pos_row, p):
    s, d = x.shape
    lambda_init = 0.8 - 0.6 * math.exp(-0.3 * l)
    gqk, gv = GLA_HEADS * GLA_DK, GLA_HEADS * GLA_DV
    dqk, dvw = DIFF_HEADS * 2 * DIFF_DH, DIFF_HEADS * DIFF_DV
    lowrank = p["gla_w_a2"].shape[0]

    mod = _ada(c_col, p["w_ada"], p["b_ada"][None, :])
    sh_a, sc_a, gt_a, sh_f, sc_f, gt_f = [mod[:, j * d:(j + 1) * d] for j in range(6)]

    w_in = p["w_in"]
    o = 0
    cols = {}
    for name, wdt in (("gq", gqk), ("gk", gqk), ("gv", gv), ("ga", lowrank), ("gg", gv),
                      ("dq", dqk), ("dk", dqk), ("dv", dvw), ("mg", d), ("md", d)):
        cols[name] = w_in[:, o:o + wdt]
        o += wdt
    row_names = ("gv", "gg", "mg", "md", "gq")
    w_row = jnp.concatenate([cols[n] for n in row_names], axis=1).astype(BF16)
    col_of, o = {}, 0
    for n in row_names:
        col_of[n] = o
        o += cols[n].shape[1]
    t_names = ("dq", "dk", "dv", "gk")
    w_t = jnp.concatenate([cols[n] for n in t_names], axis=1).T.astype(BF16)
    row_of, o = {}, 0
    for n in t_names:
        row_of[n] = o
        o += cols[n].shape[1]
    w_ga = jnp.pad(cols["ga"], ((0, 0), (0, 128 - lowrank))).astype(BF16)

    g1 = p["norm1_g"][None, :]
    tm = _tile(s, TILE_PROJ)
    ts = _tile(s, TILE_SEQ)
    proj, ga = _inproj(x, g1, sc_a, sh_a, w_row, w_ga, tm, w_row.shape[1] // 3)
    projt = _inproj_t(x, g1, sc_a, sh_a, w_t, tm, w_t.shape[0] // 2)

    wa2t = jnp.pad(p["gla_w_a2"].T, ((0, 0), (0, 128 - lowrank)))
    o_gla = _gla(proj, projt, ga, wa2t, p["gla_b_a"][:, None], p["gla_onorm_g"][None, :],
                 ts, col_of["gq"], col_of["gv"], col_of["gg"], row_of["gk"])

    invf = ROPE_THETA ** (-jnp.arange(0, ROT_DIM, 2, dtype=F32) / ROT_DIM)
    qat, qbt, kr, vte, qn, kn = _qkprep(projt, pos_row, invf[:, None], p["diff_qnorm_g"][:, None],
                                p["diff_knorm_g"][:, None], ts, row_of["dq"],
                                row_of["dk"], row_of["dv"])
    tq = ts
    o_diff = _diffattn(qat, qbt, kr, vte, qn, kn, p["diff_lq1"][None, :], p["diff_lk1"][None, :],
                       p["diff_lq2"][None, :], p["diff_lk2"][None, :],
                       p["diff_subln_g"][:, None], lambda_init, tq, tq)

    x1, hf, hfp = _mergeout(o_gla, o_diff, proj, x, p["w_branch_gla"].astype(BF16),
                       p["w_branch_diff"].astype(BF16), p["w_out"].astype(BF16), gt_a,
                       p["norm2_g"][None, :], sc_f, sh_f, ts,
                       col_of["mg"], col_of["md"])

    e = p["w_router"].shape[1]
    idx, wts, rnk, cnt = _route(hf, p["w_router"].T, p["router_bias"][:, None], ts)

    counts = cnt[:, 0]
    pcounts = ((counts + MOE_ROWS - 1) // MOE_ROWS) * MOE_ROWS
    pend = jnp.cumsum(pcounts)
    pstart = pend - pcounts
    pos = _positions(idx, rnk, pstart[:, None], ts)
    n_items = (s * TOP_K) // MOE_ROWS + e
    n_valid = (pend[-1] // MOE_ROWS).astype(jnp.int32)
    item_b = jnp.minimum(jnp.arange(n_items, dtype=jnp.int32), n_valid - 1)
    item_e = jnp.minimum(jnp.sum(pend[None, :] <= (item_b * MOE_ROWS)[:, None], axis=1),
                         e - 1).astype(jnp.int32)

    wn = SC_GATHER_WINDOW
    pos_w = pos.reshape(TOP_K, s // wn, wn).transpose(1, 0, 2).reshape(s // wn * TOP_K, wn)
    xs = _sc_scatter_rows(hfp, pos_w, n_items * MOE_ROWS)
    item_rows = jnp.clip(pstart[item_e] + counts[item_e] - item_b * MOE_ROWS, 0,
                         MOE_ROWS).astype(jnp.int32)
    prev_e = jnp.concatenate([jnp.full((1,), -1, jnp.int32), item_e[:-1]])
    item_first = ((jnp.arange(n_items) < n_valid) & (item_e != prev_e)).astype(jnp.int32)
    item_slot = ((jnp.cumsum(item_first) - 1) % MOE_SLOTS).astype(jnp.int32)
    cand = jnp.where(pcounts > 0, jnp.arange(e, dtype=jnp.int32), e)
    nonempty_from = lax.cummin(cand[::-1])[::-1]
    following = jnp.concatenate([nonempty_from[1:], jnp.full((2,), e, jnp.int32)])
    ahead = item_e
    lead = [nonempty_from[0]]
    for _ in range(MOE_SLOTS - 1):
        ahead = following[ahead]
        lead.append(following[lead[-1]])
    item_next = jnp.where(ahead < e, ahead, -1).astype(jnp.int32)
    lead = jnp.stack(lead[:MOE_SLOTS - 1])
    lead = jnp.where(lead < e, lead, -1).astype(jnp.int32)
    ys = _moe(item_e, item_b, item_first, item_slot, item_next, lead, item_rows, n_valid[None],
              xs, p["w_exp_gate"], p["w_exp_up"], p["w_exp_down"])
    gathered = _sc_gather_rows(ys, pos.reshape(1, TOP_K * s)).reshape(TOP_K, s, d // 2)
    return _combine(wts.T, hfp, x1, gt_f, p["w_sh_gate"].astype(BF16),
                    p["w_sh_up"].astype(BF16), p["w_sh_down"].astype(BF16), gathered,
                    _tile(s, TILE_COMBINE))


_LAYER_PARAMS = ("w_ada", "b_ada", "norm1_g", "w_in", "gla_w_a2", "gla_b_a", "gla_onorm_g",
                 "diff_qnorm_g", "diff_knorm_g", "diff_lq1", "diff_lk1", "diff_lq2", "diff_lk2",
                 "diff_subln_g", "w_branch_gla", "w_branch_diff", "w_out", "norm2_g", "w_router",
                 "router_bias", "w_exp_gate", "w_exp_up", "w_exp_down", "w_sh_gate", "w_sh_up",
                 "w_sh_down")


def kernel(x, c, positions, w_ada, b_ada, norm1_g, w_in, gla_w_a2, gla_b_a, gla_onorm_g, diff_qnorm_g, diff_knorm_g, diff_lq1, diff_lk1, diff_lq2, diff_lk2, diff_subln_g, w_branch_gla, w_branch_diff, w_out, norm2_g, w_router, router_bias, w_exp_gate, w_exp_up, w_exp_down, w_sh_gate, w_sh_up, w_sh_down):
    stacked = dict(zip(_LAYER_PARAMS, (
        w_ada, b_ada, norm1_g, w_in, gla_w_a2, gla_b_a, gla_onorm_g, diff_qnorm_g, diff_knorm_g,
        diff_lq1, diff_lk1, diff_lq2, diff_lk2, diff_subln_g, w_branch_gla, w_branch_diff, w_out,
        norm2_g, w_router, router_bias, w_exp_gate, w_exp_up, w_exp_down, w_sh_gate, w_sh_up,
        w_sh_down)))
    b, s, d = x.shape
    assert b == 1, "single-sequence kernel"
    xl = x[0]
    c_col = c[0][:, None]
    pos_row = positions.astype(jnp.int32)
    for l in range(w_ada.shape[0]):
        xl = _layer(l, xl, c_col, pos_row, {k: v[l] for k, v in stacked.items()})
    return xl[None]
```

```python
import functools
import math

import jax
import jax.numpy as jnp
from jax import lax
from jax.experimental import pallas as pl
from jax.experimental.pallas import tpu as pltpu
from jax.experimental.pallas import tpu_sc as plsc

CHUNK = 64
EPS = 1e-6
GLA_HEADS = 4
GLA_DK = 128
GLA_DV = 256
GLA_TAU = 16.0
DIFF_HEADS = 8
DIFF_DH = 64
DIFF_DV = 2 * DIFF_DH
ROPE_THETA = 500000.0
ROT_DIM = DIFF_DH // 4
N_GROUPS = 8
TOPK_GROUPS = 4
TOP_K = 8
ROUTED_SCALE = 2.5

MOE_ROWS = 640
MOE_DMA_CHUNKS = 4
MOE_SLOTS = 5
SC_GATHER_WINDOW = 64
VMEM_LIMIT = 56 * 1024 * 1024
TILE_PROJ = 1024
TILE_SEQ = 512
TILE_COMBINE = 256
NEG_BIG = -1e30
LOG2E = 1.4426950408889634
HIGHEST = lax.Precision.HIGHEST
F32 = jnp.float32
BF16 = jnp.bfloat16


def _cparams(sem):
    return pltpu.CompilerParams(dimension_semantics=sem, vmem_limit_bytes=VMEM_LIMIT)


def _nt_dot(a, b, precision=None):
    return lax.dot_general(a, b, (((1,), (1,)), ((), ())), precision=precision,
                           preferred_element_type=F32)


def _pack_halves(x):
    n = x.shape[1] // 2
    lo = pltpu.bitcast(x[:, :n].astype(BF16).astype(F32), jnp.uint32) >> 16
    hi = pltpu.bitcast(x[:, n:].astype(BF16).astype(F32), jnp.uint32) & jnp.uint32(0xFFFF0000)
    return lo | hi


def _unpack_halves(w):
    return (pltpu.bitcast(w << 16, F32), pltpu.bitcast(w & jnp.uint32(0xFFFF0000), F32))


def _rms_mod(x, g, sc, sh):
    xn = x * lax.rsqrt(jnp.mean(x * x, axis=-1, keepdims=True) + EPS)
    return (xn * g) * (1.0 + sc) + sh


def _ada_kernel(c_ref, w_ref, b_ref, o_ref):
    c = c_ref[...]
    ca = c * jax.nn.sigmoid(c)
    o_ref[...] = jnp.sum(ca * w_ref[...], axis=0, keepdims=True) + b_ref[...]


def _ada(c_col, w, b):
    d, n = w.shape
    tn = min(1024, n)
    return pl.pallas_call(
        _ada_kernel,
        grid=(n // tn,),
        in_specs=[pl.BlockSpec((d, 1), lambda j: (0, 0)),
                  pl.BlockSpec((d, tn), lambda j: (0, j)),
                  pl.BlockSpec((1, tn), lambda j: (0, j))],
        out_specs=pl.BlockSpec((1, tn), lambda j: (0, j)),
        out_shape=jax.ShapeDtypeStruct((1, n), F32),
        compiler_params=_cparams(("arbitrary",)),
        name="ada",
    )(c_col, w, b)


def _inproj_kernel(x_ref, g_ref, sc_ref, sh_ref, w_ref, wga_ref, o_ref, ga_ref, h_scr):
    @pl.when(pl.program_id(1) == 0)
    def _():
        h = _rms_mod(x_ref[...], g_ref[...], sc_ref[...], sh_ref[...]).astype(BF16)
        h_scr[...] = h
        ga_ref[...] = jnp.dot(h, wga_ref[...], preferred_element_type=F32)

    o_ref[...] = jnp.dot(h_scr[...], w_ref[...], preferred_element_type=F32).astype(BF16)


def _inproj(x, g, sc, sh, w, wga, tm, tn):
    s, d = x.shape
    n = w.shape[1]
    vec = pl.BlockSpec((1, d), lambda i, j: (0, 0))
    return pl.pallas_call(
        _inproj_kernel,
        grid=(s // tm, n // tn),
        in_specs=[pl.BlockSpec((tm, d), lambda i, j: (i, 0)), vec, vec, vec,
                  pl.BlockSpec((d, tn), lambda i, j: (0, j)),
                  pl.BlockSpec((d, 128), lambda i, j: (0, 0))],
        out_specs=[pl.BlockSpec((tm, tn), lambda i, j: (i, j)),
                   pl.BlockSpec((tm, 128), lambda i, j: (i, 0))],
        out_shape=[jax.ShapeDtypeStruct((s, n), BF16), jax.ShapeDtypeStruct((s, 128), F32)],
        scratch_shapes=[pltpu.VMEM((tm, d), BF16)],
        compiler_params=_cparams(("arbitrary", "arbitrary")),
        name="inproj",
    )(x, g, sc, sh, w, wga)


def _inproj_t_kernel(x_ref, g_ref, sc_ref, sh_ref, wt_ref, o_ref, h_scr):
    @pl.when(pl.program_id(1) == 0)
    def _():
        h_scr[...] = _rms_mod(x_ref[...], g_ref[...], sc_ref[...], sh_ref[...]).astype(BF16)

    o_ref[...] = _nt_dot(wt_ref[...], h_scr[...]).astype(BF16)


def _inproj_t(x, g, sc, sh, wt, tm, tn):
    s, d = x.shape
    n = wt.shape[0]
    vec = pl.BlockSpec((1, d), lambda i, j: (0, 0))
    return pl.pallas_call(
        _inproj_t_kernel,
        grid=(s // tm, n // tn),
        in_specs=[pl.BlockSpec((tm, d), lambda i, j: (i, 0)), vec, vec, vec,
                  pl.BlockSpec((tn, d), lambda i, j: (j, 0))],
        out_specs=pl.BlockSpec((tn, tm), lambda i, j: (j, i)),
        out_shape=jax.ShapeDtypeStruct((n, s), BF16),
        scratch_shapes=[pltpu.VMEM((tm, d), BF16)],
        compiler_params=_cparams(("arbitrary", "arbitrary")),
        name="inproj_t",
    )(x, g, sc, sh, wt)


def _gla_kernel(q_ref, kt_ref, v_ref, gg_ref, ga_ref, wa2t_ref, ba_ref, on_ref, o_ref,
                state_ref, o_scr):
    tt = q_ref.shape[0]
    nchunk = tt // CHUNK

    @pl.when(pl.program_id(0) == 0)
    def _():
        state_ref[...] = jnp.zeros_like(state_ref)

    zt = _nt_dot(wa2t_ref[...], ga_ref[...], precision=HIGHEST) + ba_ref[...]
    lat = (jnp.minimum(zt, 0.0) - jnp.log1p(jnp.exp(-jnp.abs(zt)))) * (1.0 / GLA_TAU)
    row = lax.broadcasted_iota(jnp.int32, (tt, tt), 0)
    col = lax.broadcasted_iota(jnp.int32, (tt, tt), 1)
    same = (row // CHUNK) == (col // CHUNK)
    incl = jnp.where(same & (row <= col), 1.0, 0.0).astype(BF16)
    full = jnp.where(same, 1.0, 0.0).astype(BF16)
    lat_hi = lat.astype(BF16)
    lat_lo = (lat - lat_hi.astype(F32)).astype(BF16)
    cumt = (jnp.dot(lat_hi, incl, preferred_element_type=F32)
            + jnp.dot(lat_lo, incl, preferred_element_type=F32))
    tott = (jnp.dot(lat_hi, full, preferred_element_type=F32)
            + jnp.dot(lat_lo, full, preferred_element_type=F32))
    kdt = kt_ref[...].astype(F32) * jnp.exp(tott - cumt)
    dec = jnp.exp(tott)

    lane = lax.broadcasted_iota(jnp.int32, (GLA_DK, 2 * CHUNK), 1)
    upd = {}
    for c in range(nchunk):
        pair = (c // 2) * 2 * CHUNK
        if nchunk > 1:
            keep = (lane // CHUNK) == (c % 2)
        for h in range(GLA_HEADS):
            rows = slice(h * GLA_DK, (h + 1) * GLA_DK)
            vcols = slice(h * GLA_DV, (h + 1) * GLA_DV)
            if nchunk > 1:
                a = jnp.where(keep, kdt[rows, pair:pair + 2 * CHUNK], 0.0).astype(BF16)
                vp = v_ref[pair:pair + 2 * CHUNK, vcols]
            else:
                a = kdt[rows, :].astype(BF16)
                vp = v_ref[:, vcols]
            upd[c, h] = jnp.dot(a, vp, preferred_element_type=F32)

    for h in range(GLA_HEADS):
        rows = slice(h * GLA_DK, (h + 1) * GLA_DK)
        vcols = slice(h * GLA_DV, (h + 1) * GLA_DV)
        st = state_ref[h]
        states = []
        for c in range(nchunk):
            st = st * dec[rows, c * CHUNK:c * CHUNK + 1] + upd[c, h]
            states.append(st.astype(BF16))
        state_ref[h] = st
        for c in range(nchunk):
            o_scr[c * CHUNK:(c + 1) * CHUNK, vcols] = jnp.dot(
                q_ref[c * CHUNK:(c + 1) * CHUNK, rows], states[c], preferred_element_type=F32)

    for h in range(GLA_HEADS):
        vcols = slice(h * GLA_DV, (h + 1) * GLA_DV)
        o = o_scr[:, vcols] * (GLA_DK ** -0.5)
        o = o * lax.rsqrt(jnp.mean(o * o, axis=-1, keepdims=True) + EPS) * on_ref[...]
        g = gg_ref[:, vcols].astype(F32)
        o_ref[:, vcols] = (o * (g * jax.nn.sigmoid(g))).astype(BF16)


def _gla(proj, projt, ga, wa2t, ba_col, on_g, tt, col_q, col_v, col_g, row_k):
    s = proj.shape[0]
    qk = GLA_HEADS * GLA_DK
    vw = GLA_HEADS * GLA_DV
    return pl.pallas_call(
        _gla_kernel,
        grid=(s // tt,),
        in_specs=[pl.BlockSpec((tt, qk), lambda i: (i, col_q // qk)),
                  pl.BlockSpec((qk, tt), lambda i: (row_k // qk, i)),
                  pl.BlockSpec((tt, vw), lambda i: (i, col_v // vw)),
                  pl.BlockSpec((tt, vw), lambda i: (i, col_g // vw)),
                  pl.BlockSpec((tt, 128), lambda i: (i, 0)),
                  pl.BlockSpec((qk, 128), lambda i: (0, 0)),
                  pl.BlockSpec((qk, 1), lambda i: (0, 0)),
                  pl.BlockSpec((1, GLA_DV), lambda i: (0, 0))],
        out_specs=pl.BlockSpec((tt, vw), lambda i: (i, 0)),
        out_shape=jax.ShapeDtypeStruct((s, vw), BF16),
        scratch_shapes=[pltpu.VMEM((GLA_HEADS, GLA_DK, GLA_DV), F32),
                        pltpu.VMEM((tt, vw), F32)],
        compiler_params=_cparams(("arbitrary",)),
        name="gla",
    )(proj, projt, proj, proj, ga, wa2t, ba_col, on_g)


def _qknorm_rope_t(xt, g_col, cos, sin):
    n, tm = xt.shape
    x3 = xt.reshape(n // DIFF_DH, DIFF_DH, tm)
    r = lax.rsqrt(jnp.mean(x3 * x3, axis=1, keepdims=True) + EPS)
    y = x3 * r * g_col[None]
    half = ROT_DIM // 2
    y1, y2, rest = y[:, :half], y[:, half:ROT_DIM], y[:, ROT_DIM:]
    o1 = y1 * cos[None] - y2 * sin[None]
    o2 = y2 * cos[None] + y1 * sin[None]
    return jnp.concatenate([o1, o2, rest], axis=1)


def _seg_norms(x3):
    return jnp.sqrt(jnp.sum(x3 * x3, axis=1)).reshape(DIFF_HEADS, 2, x3.shape[2])


def _qkprep_kernel(qt_ref, kt_ref, vt_ref, pos_ref, invf_ref, qg_ref, kg_ref, qa_ref, qb_ref,
                   ko_ref, ve_ref, qn_ref, kn_ref):
    tm = qt_ref.shape[1]
    v3 = vt_ref[...].reshape(DIFF_HEADS, DIFF_DV, tm)
    ones = jnp.ones((DIFF_HEADS, ATT_SUM_ROWS, tm), BF16)
    ve_ref[...] = jnp.concatenate([v3, ones], axis=1).reshape(-1, tm)
    ang = pos_ref[...].astype(F32) * invf_ref[...]
    cos, sin = jnp.cos(ang), jnp.sin(ang)
    k3 = _qknorm_rope_t(kt_ref[...].astype(F32), kg_ref[...], cos, sin)
    ko_ref[...] = k3.reshape(-1, tm).T.astype(BF16)
    kn_ref[...] = _seg_norms(k3)
    q3 = _qknorm_rope_t(qt_ref[...].astype(F32), qg_ref[...], cos, sin) * (
        DIFF_DH ** -0.5 * LOG2E)
    qn_ref[...] = _seg_norms(q3)
    seg = lax.broadcasted_iota(jnp.int32, q3.shape, 0)
    qa_ref[...] = jnp.where(seg % 2 == 0, q3, 0.0).reshape(-1, tm).astype(BF16)
    qb_ref[...] = jnp.where(seg % 2 == 1, q3, 0.0).reshape(-1, tm).astype(BF16)


def _qkprep(projt, pos_row, invf_col, qg_col, kg_col, tm, row_q, row_k, row_v):
    s = projt.shape[1]
    n = DIFF_HEADS * 2 * DIFF_DH
    ne = DIFF_HEADS * (DIFF_DV + ATT_SUM_ROWS)
    col = pl.BlockSpec((DIFF_DH, 1), lambda i: (0, 0))
    return pl.pallas_call(
        _qkprep_kernel,
        grid=(s // tm,),
        in_specs=[pl.BlockSpec((n, tm), lambda i: (row_q // n, i)),
                  pl.BlockSpec((n, tm), lambda i: (row_k // n, i)),
                  pl.BlockSpec((n, tm), lambda i: (row_v // n, i)),
                  pl.BlockSpec((1, tm), lambda i: (0, i)),
                  pl.BlockSpec((ROT_DIM // 2, 1), lambda i: (0, 0)), col, col],
        out_specs=[pl.BlockSpec((n, tm), lambda i: (0, i)),
                   pl.BlockSpec((n, tm), lambda i: (0, i)),
                   pl.BlockSpec((tm, n), lambda i: (i, 0)),
                   pl.BlockSpec((ne, tm), lambda i: (0, i)),
                   pl.BlockSpec((DIFF_HEADS, 2, tm), lambda i: (0, 0, i)),
                   pl.BlockSpec((DIFF_HEADS, 2, tm), lambda i: (0, 0, i))],
        out_shape=[jax.ShapeDtypeStruct((n, s), BF16), jax.ShapeDtypeStruct((n, s), BF16),
                   jax.ShapeDtypeStruct((s, n), BF16), jax.ShapeDtypeStruct((ne, s), BF16),
                   jax.ShapeDtypeStruct((DIFF_HEADS, 2, s), F32),
                   jax.ShapeDtypeStruct((DIFF_HEADS, 2, s), F32)],
        compiler_params=_cparams(("arbitrary",)),
        name="qkprep",
    )(projt, projt, projt, pos_row, invf_col, qg_col, kg_col)


ATT_COLS = 256
ATT_LOOKAHEAD = {True: 2, False: 4}
ATT_BODY_TILES = {True: 8, False: 2}
ATT_SUM_ROWS = 16
ATT_BOUND_SLACK = 1.02
ATT_BOUND_LIMIT = 50.0


def _diffattn_kernel(qa_ref, qb_ref, k_ref, vt_ref, qn_ref, kn_ref, lq1_ref, lk1_ref, lq2_ref,
                     lk2_ref, sg_ref, o_ref, *scr, lambda_init, tk, cols):
    tq = qa_ref.shape[1]
    nblk = 2 * tq // cols
    m_scr, acc_scr = (scr[b * nblk:(b + 1) * nblk] for b in range(2))
    kmax_scr = scr[2 * nblk]
    s_scr = scr[2 * nblk + 1:]
    i = pl.program_id(1)

    @pl.when(i == 0)
    def _():
        kmax_scr[...] = jnp.max(kn_ref[...], axis=1, keepdims=True)

    bound = qn_ref[...] * kmax_scr[...] * ATT_BOUND_SLACK
    bound = jnp.concatenate([bound[0:1], bound[1:2]], axis=1)
    bounded = jnp.max(bound) < ATT_BOUND_LIMIT
    for c in range(nblk):
        m_scr[c][...] = jnp.where(bounded, bound[:, c * cols:(c + 1) * cols], NEG_BIG)
        acc_scr[c][...] = jnp.zeros_like(acc_scr[c])

    def scores(j, c):
        start = pl.multiple_of(j * tk, tk)
        q_ref = qa_ref if c * cols < tq else qb_ref
        off = (c * cols) % tq
        return jnp.dot(k_ref[pl.ds(start, tk), :], q_ref[:, off:off + cols],
                       preferred_element_type=F32)

    def steps(tiles, masked, next_tile, fixed):
        look = ATT_LOOKAHEAD[fixed]
        items = [(j, c) for j in tiles for c in range(nblk)]
        pending = []
        for n, (j, c) in enumerate(items):
            s = s_scr[n][...] if n < look else pending.pop(0)
            ahead = n + look
            if ahead < len(items):
                pending.append(scores(*items[ahead]))
            elif next_tile is not None:
                s_scr[ahead - len(items)][...] = scores(next_tile, ahead - len(items))
            start = pl.multiple_of(j * tk, tk)
            keys = tk
            if masked:
                keys = (c * cols) % tq + cols
                s = s[:keys]
                krow = lax.broadcasted_iota(jnp.int32, (keys, cols), 0)
                qcol = lax.broadcasted_iota(jnp.int32, (keys, cols), 1)
                qpos = i * tq + (c * cols) % tq + qcol
                s = jnp.where((start + krow) // CHUNK <= qpos // CHUNK, s, NEG_BIG)
            vj = vt_ref[:, pl.ds(start, keys)]
            if fixed:
                p = jnp.exp2(s - m_scr[c][...]).astype(BF16)
                acc_scr[c][...] += jnp.dot(vj, p, preferred_element_type=F32)
            else:
                m_prev = m_scr[c][...]
                m_new = jnp.maximum(m_prev, jnp.max(s, axis=0, keepdims=True))
                alpha = jnp.exp2(m_prev - m_new)
                p = jnp.exp2((s - m_new).astype(BF16))
                acc_scr[c][...] = alpha * acc_scr[c][...] + jnp.dot(
                    vj, p, preferred_element_type=F32)
                m_scr[c][...] = m_new

    def attend(fixed):
        for c in range(ATT_LOOKAHEAD[fixed]):
            s_scr[c][...] = scores(0, c)
        big = ATT_BODY_TILES[fixed]
        lax.fori_loop(0, i // big, lambda t, c: (steps(
            tuple(big * t + u for u in range(big)), False, big * t + big, fixed), c)[1], 0)
        size = big // 2
        while size >= 1:
            first = (i // (2 * size)) * (2 * size)

            @pl.when(i % (2 * size) >= size)
            def _(first=first, size=size):
                steps(tuple(first + u for u in range(size)), False, first + size, fixed)

            size //= 2
        steps((i,), True, None, fixed)

    @pl.when(bounded)
    def _():
        attend(True)

    @pl.when(jnp.logical_not(bounded))
    def _():
        attend(False)

    o = jnp.concatenate([acc_scr[c][:DIFF_DV] / acc_scr[c][DIFF_DV:DIFF_DV + 1]
                         for c in range(nblk)], axis=1)
    lam = (jnp.exp(jnp.sum(lq1_ref[...] * lk1_ref[...]))
           - jnp.exp(jnp.sum(lq2_ref[...] * lk2_ref[...])) + lambda_init)
    o = o[:, :tq] - lam * o[:, tq:]
    o = o * lax.rsqrt(jnp.mean(o * o, axis=0, keepdims=True) + EPS) * sg_ref[...]
    o_ref[...] = (o * (1.0 - lambda_init)).T.astype(BF16)


def _diffattn(qat, qbt, kr, vte, qn, kn, lq1, lk1, lq2, lk2, sg_col, lambda_init, tq, tk):
    s = kr.shape[0]
    hd = 2 * DIFF_DH
    vec = pl.BlockSpec((1, DIFF_DH), lambda h, i: (0, 0))
    cols = min(ATT_COLS, tq)
    nblk = 2 * tq // cols
    kern = functools.partial(_diffattn_kernel, lambda_init=lambda_init, tk=tk, cols=cols)
    dve = DIFF_DV + ATT_SUM_ROWS
    assert tq == tk and nblk >= max(ATT_LOOKAHEAD.values())
    scratch = ([pltpu.VMEM((1, cols), F32)] * nblk + [pltpu.VMEM((dve, cols), F32)] * nblk
               + [pltpu.VMEM((2, 1), F32)]
               + [pltpu.VMEM((tk, cols), F32)] * max(ATT_LOOKAHEAD.values()))
    return pl.pallas_call(
        kern,
        grid=(DIFF_HEADS, s // tq),
        in_specs=[pl.BlockSpec((hd, tq), lambda h, i: (h, i)),
                  pl.BlockSpec((hd, tq), lambda h, i: (h, i)),
                  pl.BlockSpec((s, hd), lambda h, i: (0, h)),
                  pl.BlockSpec((dve, s), lambda h, i: (h, 0)),
                  pl.BlockSpec((None, 2, tq), lambda h, i: (h, 0, i)),
                  pl.BlockSpec((None, 2, s), lambda h, i: (h, 0, 0)),
                  vec, vec, vec, vec,
                  pl.BlockSpec((DIFF_DV, 1), lambda h, i: (0, 0))],
        out_specs=pl.BlockSpec((tq, DIFF_DV), lambda h, i: (i, h)),
        out_shape=jax.ShapeDtypeStruct((s, DIFF_HEADS * DIFF_DV), BF16),
        scratch_shapes=scratch,
        compiler_params=_cparams(("arbitrary", "arbitrary")),
        name="diffattn",
    )(qat, qbt, kr, vte, qn, kn, lq1, lk1, lq2, lk2, sg_col)


def _mergeout_kernel(og_ref, od_ref, mg_ref, md_ref, x_ref, wbg_ref, wbd_ref, wo_ref, gt_ref,
                     g2_ref, sc_ref, sh_ref, x1_ref, hf_ref, hfp_ref):
    bg = jnp.dot(og_ref[...], wbg_ref[...], preferred_element_type=F32)
    bd = jnp.dot(od_ref[...], wbd_ref[...], preferred_element_type=F32)
    merged = (jax.nn.sigmoid(mg_ref[...].astype(F32)) * bg
              + jax.nn.sigmoid(md_ref[...].astype(F32)) * bd)
    x1 = x_ref[...] + gt_ref[...] * jnp.dot(merged.astype(BF16), wo_ref[...],
                                             preferred_element_type=F32)
    x1_ref[...] = x1
    hf = _rms_mod(x1, g2_ref[...], sc_ref[...], sh_ref[...])
    hf_ref[...] = hf
    hfp_ref[...] = _pack_halves(hf)


def _mergeout(og, od, proj, x, wbg, wbd, wo, gt, g2, sc, sh, tm, col_mg, col_md):
    s, d = x.shape
    vec = pl.BlockSpec((1, d), lambda i: (0, 0))
    wspec = pl.BlockSpec((d, d), lambda i: (0, 0))
    row = pl.BlockSpec((tm, d), lambda i: (i, 0))
    return pl.pallas_call(
        _mergeout_kernel,
        grid=(s // tm,),
        in_specs=[row, row,
                  pl.BlockSpec((tm, d), lambda i: (i, col_mg // d)),
                  pl.BlockSpec((tm, d), lambda i: (i, col_md // d)),
                  row, wspec, wspec, wspec, vec, vec, vec, vec],
        out_specs=[row, row, pl.BlockSpec((tm, d // 2), lambda i: (i, 0))],
        out_shape=[jax.ShapeDtypeStruct((s, d), F32), jax.ShapeDtypeStruct((s, d), F32),
                   jax.ShapeDtypeStruct((s, d // 2), jnp.uint32)],
        compiler_params=_cparams(("arbitrary",)),
        name="mergeout",
    )(og, od, proj, proj, x, wbg, wbd, wo, gt, g2, sc, sh)


def _route_kernel(hf_ref, wrt_ref, bias_ref, idx_ref, wts_ref, rnk_ref, cnt_ref, run_scr):
    tr = hf_ref.shape[0]
    e = wrt_ref.shape[0]
    gsz = e // N_GROUPS

    @pl.when(pl.program_id(0) == 0)
    def _():
        run_scr[...] = jnp.zeros_like(run_scr)

    logits = _nt_dot(wrt_ref[...], hf_ref[...], precision=HIGHEST)
    scores = jax.nn.sigmoid(logits)
    biased = scores + bias_ref[...]
    g3 = biased.reshape(N_GROUPS, gsz, tr)
    m1 = jnp.max(g3, axis=1, keepdims=True)
    n_top = jnp.sum(jnp.where(g3 == m1, 1.0, 0.0), axis=1, keepdims=True)
    m2 = jnp.max(jnp.where(g3 < m1, g3, -jnp.inf), axis=1, keepdims=True)
    gs = (m1 + jnp.where(n_top >= 2.0, m1, m2)).reshape(N_GROUPS, tr)
    gi = lax.broadcasted_iota(jnp.int32, (N_GROUPS, tr), 0)
    beaten = jnp.zeros((N_GROUPS, tr), F32)
    for g in range(N_GROUPS):
        other = gs[g:g + 1, :]
        beaten = beaten + jnp.where((other > gs) | ((other == gs) & (g < gi)), 1.0, 0.0)
    gsel = (beaten < float(TOPK_GROUPS)).reshape(N_GROUPS, 1, tr)
    masked = jnp.where(gsel, g3, -jnp.inf).reshape(e, tr)

    ids = lax.broadcasted_iota(jnp.int32, (e, tr), 0)
    chosen = jnp.zeros((e, tr), F32)
    sel_idx, sel_score = [], []
    for _ in range(TOP_K):
        mx = jnp.max(masked, axis=0, keepdims=True)
        ix = jnp.min(jnp.where(masked == mx, ids, e), axis=0, keepdims=True)
        hit = ids == ix
        sel_idx.append(ix)
        sel_score.append(jnp.sum(jnp.where(hit, scores, 0.0), axis=0, keepdims=True))
        chosen = jnp.where(hit, 1.0, chosen)
        masked = jnp.where(hit, -jnp.inf, masked)
    idx = jnp.concatenate(sel_idx, axis=0)
    sc = jnp.concatenate(sel_score, axis=0)
    idx_ref[...] = idx
    wts_ref[...] = sc / jnp.sum(sc, axis=0, keepdims=True) * ROUTED_SCALE

    row = lax.broadcasted_iota(jnp.int32, (tr, tr), 0)
    col = lax.broadcasted_iota(jnp.int32, (tr, tr), 1)
    before = jnp.where(row < col, 1.0, 0.0).astype(BF16)
    prior = jnp.dot(chosen.astype(BF16), before, preferred_element_type=F32) + run_scr[:, 0:1]
    rnk_ref[...] = jnp.concatenate(
        [jnp.sum(jnp.where(ids == sel_idx[k], prior, 0.0), axis=0, keepdims=True)
         for k in range(TOP_K)], axis=0).astype(jnp.int32)
    run_scr[...] = run_scr[...] + jnp.sum(chosen, axis=1, keepdims=True)
    cnt_ref[...] = run_scr[...].astype(jnp.int32)


def _route(hf, wrt, bias_col, tr):
    s, d = hf.shape
    e = wrt.shape[0]
    tok = pl.BlockSpec((TOP_K, tr), lambda i: (0, i))
    return pl.pallas_call(
        _route_kernel,
        grid=(s // tr,),
        in_specs=[pl.BlockSpec((tr, d), lambda i: (i, 0)),
                  pl.BlockSpec((e, d), lambda i: (0, 0)),
                  pl.BlockSpec((e, 1), lambda i: (0, 0))],
        out_specs=[tok, tok, tok, pl.BlockSpec((e, 128), lambda i: (0, 0))],
        out_shape=[jax.ShapeDtypeStruct((TOP_K, s), jnp.int32),
                   jax.ShapeDtypeStruct((TOP_K, s), F32),
                   jax.ShapeDtypeStruct((TOP_K, s), jnp.int32),
                   jax.ShapeDtypeStruct((e, 128), jnp.int32)],
        scratch_shapes=[pltpu.VMEM((e, 128), F32)],
        compiler_params=_cparams(("arbitrary",)),
        name="route",
    )(hf, wrt, bias_col)


def _positions_kernel(idx_ref, rnk_ref, pstart_ref, pos_ref):
    e = pstart_ref.shape[0]
    ts = idx_ref.shape[1]
    ids = lax.broadcasted_iota(jnp.int32, (e, ts), 0)
    idx = idx_ref[...]
    pos_ref[...] = rnk_ref[...] + jnp.concatenate(
        [jnp.sum(jnp.where(ids == idx[k:k + 1, :], pstart_ref[...], 0), axis=0, keepdims=True)
         for k in range(TOP_K)], axis=0)


def _positions(idx, rnk, pstart_col, ts):
    s = idx.shape[1]
    e = pstart_col.shape[0]
    tok = pl.BlockSpec((TOP_K, ts), lambda i: (0, i))
    return pl.pallas_call(
        _positions_kernel,
        grid=(s // ts,),
        in_specs=[tok, tok, pl.BlockSpec((e, 1), lambda i: (0, 0))],
        out_specs=tok,
        out_shape=jax.ShapeDtypeStruct((TOP_K, s), jnp.int32),
        compiler_params=_cparams(("arbitrary",)),
        name="positions",
    )(idx, rnk, pstart_col)


def _swiglu_packed(xp, wg, wu, wd):
    lo, hi = _unpack_halves(xp)
    lo, hi = lo.astype(BF16), hi.astype(BF16)
    n = lo.shape[1]
    g = (jnp.dot(lo, wg[:n], preferred_element_type=F32)
         + jnp.dot(hi, wg[n:], preferred_element_type=F32))
    u = (jnp.dot(lo, wu[:n], preferred_element_type=F32)
         + jnp.dot(hi, wu[n:], preferred_element_type=F32))
    h = (g * jax.nn.sigmoid(g)) * u
    return jnp.dot(h.astype(BF16), wd[...], preferred_element_type=F32)


def _moe_kernel(ie_ref, ib_ref, first_ref, slot_ref, ne_ref, lead_ref, rows_ref, nv_ref, xs_ref,
                wg_hbm, wu_hbm, wd_hbm, ys_ref, wg_f, wu_f, wd_f, sem):
    del ib_ref
    i = pl.program_id(0)

    def fetch(e, slot):
        copies = []
        for n, (src, dst) in enumerate(((wg_hbm, wg_f), (wu_hbm, wu_f), (wd_hbm, wd_f))):
            rows = src.shape[1] // MOE_DMA_CHUNKS
            for ch in range(MOE_DMA_CHUNKS):
                part = pl.ds(ch * rows, rows)
                copies.append(pltpu.make_async_copy(
                    src.at[e, part], dst.at[slot, part], sem.at[slot, n * MOE_DMA_CHUNKS + ch]))
        return copies

    @pl.when(i == 0)
    def _():
        for s in range(MOE_SLOTS - 1):
            @pl.when(lead_ref[s] >= 0)
            def _(s=s):
                for cp in fetch(lead_ref[s], s):
                    cp.start()

    @pl.when(i < nv_ref[0])
    def _():
        for slot in range(MOE_SLOTS):
            @pl.when((first_ref[i] == 1) & (slot_ref[i] == slot))
            def _(slot=slot):
                for cp in fetch(ie_ref[i], slot):
                    cp.wait()

                @pl.when(ne_ref[i] >= 0)
                def _():
                    for cp in fetch(ne_ref[i], (slot + MOE_SLOTS - 1) % MOE_SLOTS):
                        cp.start()

        slot = slot_ref[i]
        row = lax.broadcasted_iota(jnp.int32, xs_ref.shape, 0)
        xp = jnp.where(row < rows_ref[i], xs_ref[...], jnp.uint32(0))
        ys_ref[...] = _pack_halves(_swiglu_packed(
            xp, wg_f[slot].astype(BF16), wu_f[slot].astype(BF16), wd_f[slot].astype(BF16)))


def _moe(item_e, item_b, item_first, item_slot, item_next, lead, item_rows, n_valid, xs, wg, wu,
         wd):
    m_pad, dh = xs.shape
    _, d, f = wg.shape
    n_items = item_e.shape[0]
    blk = lambda i, ie, ib, fi, sl, ne, ld, nr, nv: (ib[i], 0)
    hbm = pl.BlockSpec(memory_space=pl.ANY)
    return pl.pallas_call(
        _moe_kernel,
        grid_spec=pltpu.PrefetchScalarGridSpec(
            num_scalar_prefetch=8,
            grid=(n_items,),
            in_specs=[pl.BlockSpec((MOE_ROWS, dh), blk), hbm, hbm, hbm],
            out_specs=pl.BlockSpec((MOE_ROWS, dh), blk),
            scratch_shapes=[pltpu.VMEM((MOE_SLOTS, d, f), F32), pltpu.VMEM((MOE_SLOTS, d, f), F32),
                            pltpu.VMEM((MOE_SLOTS, f, d), F32),
                            pltpu.SemaphoreType.DMA((MOE_SLOTS, 3 * MOE_DMA_CHUNKS))],
        ),
        out_shape=jax.ShapeDtypeStruct((m_pad, dh), jnp.uint32),
        compiler_params=_cparams(("arbitrary",)),
        name="moe",
    )(item_e, item_b, item_first, item_slot, item_next, lead, item_rows, n_valid, xs, wg, wu, wd)


def _sc_gather_rows(table, idx_row):
    m = idx_row.shape[1]
    w = table.shape[1]
    idx_row = idx_row.reshape(m // SC_GATHER_WINDOW, SC_GATHER_WINDOW)
    mesh = plsc.VectorSubcoreMesh(core_axis_name="c", subcore_axis_name="s")

    @functools.partial(pl.kernel, mesh=mesh,
                       out_type=jax.ShapeDtypeStruct((m, w), table.dtype))
    def gather(table_hbm, idx_hbm, out_hbm):
        def body(idx_vmem, out_vmem):
            pltpu.sync_copy(table_hbm.at[idx_vmem.at[0]], out_vmem)

        pltpu.emit_pipeline(
            body,
            grid=(m // SC_GATHER_WINDOW,),
            in_specs=[pl.BlockSpec((1, SC_GATHER_WINDOW), lambda i: (i, 0))],
            out_specs=[pl.BlockSpec((SC_GATHER_WINDOW, w), lambda i: (i, 0))],
            core_axis_name=("c", "s"),
            dimension_semantics=(pltpu.PARALLEL,),
        )(idx_hbm, out_hbm)

    return gather(table, idx_row)


def _sc_scatter_rows(rows, idx_blocks, m_out):
    s, w = rows.shape
    mesh = plsc.VectorSubcoreMesh(core_axis_name="c", subcore_axis_name="s")

    @functools.partial(pl.kernel, mesh=mesh,
                       out_type=jax.ShapeDtypeStruct((m_out, w), rows.dtype))
    def scatter(rows_hbm, idx_hbm, out_hbm):
        def body(rows_vmem, idx_vmem):
            for k in range(TOP_K):
                pltpu.sync_copy(rows_vmem, out_hbm.at[idx_vmem.at[k]])

        pltpu.emit_pipeline(
            body,
            grid=(s // SC_GATHER_WINDOW,),
            in_specs=[pl.BlockSpec((SC_GATHER_WINDOW, w), lambda i: (i, 0)),
                      pl.BlockSpec((TOP_K, SC_GATHER_WINDOW), lambda i: (i, 0))],
            out_specs=[],
            core_axis_name=("c", "s"),
            dimension_semantics=(pltpu.PARALLEL,),
        )(rows_hbm, idx_hbm)

    return scatter(rows, idx_blocks)


def _combine_kernel(wt_ref, hf_ref, x1_ref, gt_ref, sg_ref, su_ref, sd_ref, g_ref, o_ref):
    tc = x1_ref.shape[0]
    y = _swiglu_packed(hf_ref[...], sg_ref, su_ref, sd_ref)
    wt = wt_ref[...]
    n = g_ref.shape[2]
    r_lo = jnp.zeros((tc, n), F32)
    r_hi = jnp.zeros((tc, n), F32)
    for k in range(TOP_K):
        lo, hi = _unpack_halves(g_ref[k])
        r_lo = r_lo + lo * wt[:, k:k + 1]
        r_hi = r_hi + hi * wt[:, k:k + 1]
    y = y + jnp.concatenate([r_lo, r_hi], axis=1)
    o_ref[...] = x1_ref[...] + gt_ref[...] * y


def _combine(wts_t, hfp, x1, gt, sg, su, sd, gathered, tc):
    s, d = x1.shape
    f = sg.shape[1]
    row = pl.BlockSpec((tc, d), lambda i: (i, 0))
    return pl.pallas_call(
        _combine_kernel,
        grid=(s // tc,),
        in_specs=[pl.BlockSpec((tc, TOP_K), lambda i: (i, 0)),
                  pl.BlockSpec((tc, d // 2), lambda i: (i, 0)), row,
                  pl.BlockSpec((1, d), lambda i: (0, 0)),
                  pl.BlockSpec((d, f), lambda i: (0, 0)),
                  pl.BlockSpec((d, f), lambda i: (0, 0)),
                  pl.BlockSpec((f, d), lambda i: (0, 0)),
                  pl.BlockSpec((TOP_K, tc, d // 2), lambda i: (0, i, 0))],
        out_specs=row,
        out_shape=jax.ShapeDtypeStruct((s, d), F32),
        compiler_params=_cparams(("arbitrary",)),
        name="combine",
    )(wts_t, hfp, x1, gt, sg, su, sd, gathered)


def _tile(n, want):
    t = min(n, want)
    assert n % t == 0, (n, t)
    return t


def _layer(l, x, c_col, pos_row, p):
    s, d = x.shape
    lambda_init = 0.8 - 0.6 * math.exp(-0.3 * l)
    gqk, gv = GLA_HEADS * GLA_DK, GLA_HEADS * GLA_DV
    dqk, dvw = DIFF_HEADS * 2 * DIFF_DH, DIFF_HEADS * DIFF_DV
    lowrank = p["gla_w_a2"].shape[0]

    mod = _ada(c_col, p["w_ada"], p["b_ada"][None, :])
    sh_a, sc_a, gt_a, sh_f, sc_f, gt_f = [mod[:, j * d:(j + 1) * d] for j in range(6)]

    w_in = p["w_in"]
    o = 0
    cols = {}
    for name, wdt in (("gq", gqk), ("gk", gqk), ("gv", gv), ("ga", lowrank), ("gg", gv),
                      ("dq", dqk), ("dk", dqk), ("dv", dvw), ("mg", d), ("md", d)):
        cols[name] = w_in[:, o:o + wdt]
        o += wdt
    row_names = ("gv", "gg", "mg", "md", "gq")
    w_row = jnp.concatenate([cols[n] for n in row_names], axis=1).astype(BF16)
    col_of, o = {}, 0
    for n in row_names:
        col_of[n] = o
        o += cols[n].shape[1]
    t_names = ("dq", "dk", "dv", "gk")
    w_t = jnp.concatenate([cols[n] for n in t_names], axis=1).T.astype(BF16)
    row_of, o = {}, 0
    for n in t_names:
        row_of[n] = o
        o += cols[n].shape[1]
    w_ga = jnp.pad(cols["ga"], ((0, 0), (0, 128 - lowrank))).astype(BF16)

    g1 = p["norm1_g"][None, :]
    tm = _tile(s, TILE_PROJ)
    ts = _tile(s, TILE_SEQ)
    proj, ga = _inproj(x, g1, sc_a, sh_a, w_row, w_ga, tm, w_row.shape[1] // 3)
    projt = _inproj_t(x, g1, sc_a, sh_a, w_t, tm, w_t.shape[0] // 2)

    wa2t = jnp.pad(p["gla_w_a2"].T, ((0, 0), (0, 128 - lowrank)))
    o_gla = _gla(proj, projt, ga, wa2t, p["gla_b_a"][:, None], p["gla_onorm_g"][None, :],
                 ts, col_of["gq"], col_of["gv"], col_of["gg"], row_of["gk"])

    invf = ROPE_THETA ** (-jnp.arange(0, ROT_DIM, 2, dtype=F32) / ROT_DIM)
    qat, qbt, kr, vte, qn, kn = _qkprep(projt, pos_row, invf[:, None], p["diff_qnorm_g"][:, None],
                                p["diff_knorm_g"][:, None], ts, row_of["dq"],
                                row_of["dk"], row_of["dv"])
    tq = ts
    o_diff = _diffattn(qat, qbt, kr, vte, qn, kn, p["diff_lq1"][None, :], p["diff_lk1"][None, :],
                       p["diff_lq2"][None, :], p["diff_lk2"][None, :],
                       p["diff_subln_g"][:, None], lambda_init, tq, tq)

    x1, hf, hfp = _mergeout(o_gla, o_diff, proj, x, p["w_branch_gla"].astype(BF16),
                       p["w_branch_diff"].astype(BF16), p["w_out"].astype(BF16), gt_a,
                       p["norm2_g"][None, :], sc_f, sh_f, ts,
                       col_of["mg"], col_of["md"])

    e = p["w_router"].shape[1]
    idx, wts, rnk, cnt = _route(hf, p["w_router"].T, p["router_bias"][:, None], ts)

    counts = cnt[:, 0]
    pcounts = ((counts + MOE_ROWS - 1) // MOE_ROWS) * MOE_ROWS
    pend = jnp.cumsum(pcounts)
    pstart = pend - pcounts
    pos = _positions(idx, rnk, pstart[:, None], ts)
    n_items = (s * TOP_K) // MOE_ROWS + e
    n_valid = (pend[-1] // MOE_ROWS).astype(jnp.int32)
    item_b = jnp.minimum(jnp.arange(n_items, dtype=jnp.int32), n_valid - 1)
    item_e = jnp.minimum(jnp.sum(pend[None, :] <= (item_b * MOE_ROWS)[:, None], axis=1),
                         e - 1).astype(jnp.int32)

    wn = SC_GATHER_WINDOW
    pos_w = pos.reshape(TOP_K, s // wn, wn).transpose(1, 0, 2).reshape(s // wn * TOP_K, wn)
    xs = _sc_scatter_rows(hfp, pos_w, n_items * MOE_ROWS)
    item_rows = jnp.clip(pstart[item_e] + counts[item_e] - item_b * MOE_ROWS, 0,
                         MOE_ROWS).astype(jnp.int32)
    prev_e = jnp.concatenate([jnp.full((1,), -1, jnp.int32), item_e[:-1]])
    item_first = ((jnp.arange(n_items) < n_valid) & (item_e != prev_e)).astype(jnp.int32)
    item_slot = ((jnp.cumsum(item_first) - 1) % MOE_SLOTS).astype(jnp.int32)
    cand = jnp.where(pcounts > 0, jnp.arange(e, dtype=jnp.int32), e)
    nonempty_from = lax.cummin(cand[::-1])[::-1]
    following = jnp.concatenate([nonempty_from[1:], jnp.full((2,), e, jnp.int32)])
    ahead = item_e
    lead = [nonempty_from[0]]
    for _ in range(MOE_SLOTS - 1):
        ahead = following[ahead]
        lead.append(following[lead[-1]])
    item_next = jnp.where(ahead < e, ahead, -1).astype(jnp.int32)
    lead = jnp.stack(lead[:MOE_SLOTS - 1])
    lead = jnp.where(lead < e, lead, -1).astype(jnp.int32)
    ys = _moe(item_e, item_b, item_first, item_slot, item_next, lead, item_rows, n_valid[None],
              xs, p["w_exp_gate"], p["w_exp_up"], p["w_exp_down"])
    gathered = _sc_gather_rows(ys, pos.reshape(1, TOP_K * s)).reshape(TOP_K, s, d // 2)
    return _combine(wts.T, hfp, x1, gt_f, p["w_sh_gate"].astype(BF16),
                    p["w_sh_up"].astype(BF16), p["w_sh_down"].astype(BF16), gathered,
                    _tile(s, TILE_COMBINE))


_LAYER_PARAMS = ("w_ada", "b_ada", "norm1_g", "w_in", "gla_w_a2", "gla_b_a", "gla_onorm_g",
                 "diff_qnorm_g", "diff_knorm_g", "diff_lq1", "diff_lk1", "diff_lq2", "diff_lk2",
                 "diff_subln_g", "w_branch_gla", "w_branch_diff", "w_out", "norm2_g", "w_router",
                 "router_bias", "w_exp_gate", "w_exp_up", "w_exp_down", "w_sh_gate", "w_sh_up",
                 "w_sh_down")


def kernel(x, c, positions, w_ada, b_ada, norm1_g, w_in, gla_w_a2, gla_b_a, gla_onorm_g, diff_qnorm_g, diff_knorm_g, diff_lq1, diff_lk1, diff_lq2, diff_lk2, diff_subln_g, w_branch_gla, w_branch_diff, w_out, norm2_g, w_router, router_bias, w_exp_gate, w_exp_up, w_exp_down, w_sh_gate, w_sh_up, w_sh_down):
    stacked = dict(zip(_LAYER_PARAMS, (
        w_ada, b_ada, norm1_g, w_in, gla_w_a2, gla_b_a, gla_onorm_g, diff_qnorm_g, diff_knorm_g,
        diff_lq1, diff_lk1, diff_lq2, diff_lk2, diff_subln_g, w_branch_gla, w_branch_diff, w_out,
        norm2_g, w_router, router_bias, w_exp_gate, w_exp_up, w_exp_down, w_sh_gate, w_sh_up,
        w_sh_down)))
    b, s, d = x.shape
    assert b == 1, "single-sequence kernel"
    xl = x[0]
    c_col = c[0][:, None]
    pos_row = positions.astype(jnp.int32)
    for l in range(w_ada.shape[0]):
        xl = _layer(l, xl, c_col, pos_row, {k: v[l] for k, v in stacked.items()})
    return xl[None]
```

```python
import functools
import math

import jax
import jax.numpy as jnp
from jax import lax
from jax.experimental import pallas as pl
from jax.experimental.pallas import tpu as pltpu
from jax.experimental.pallas import tpu_sc as plsc

CHUNK = 64
EPS = 1e-6
GLA_HEADS = 4
GLA_DK = 128
GLA_DV = 256
GLA_TAU = 16.0
DIFF_HEADS = 8
DIFF_DH = 64
DIFF_DV = 2 * DIFF_DH
ROPE_THETA = 500000.0
ROT_DIM = DIFF_DH // 4
N_GROUPS = 8
TOPK_GROUPS = 4
TOP_K = 8
ROUTED_SCALE = 2.5

MOE_ROWS = 640
MOE_DMA_CHUNKS = 4
MOE_SLOTS = 3
SC_GATHER_WINDOW = 64
VMEM_LIMIT = 56 * 1024 * 1024
TILE_PROJ = 1024
TILE_SEQ = 512
TILE_COMBINE = 256
NEG_BIG = -1e30
LOG2E = 1.4426950408889634
HIGHEST = lax.Precision.HIGHEST
F32 = jnp.float32
BF16 = jnp.bfloat16


def _cparams(sem):
    return pltpu.CompilerParams(dimension_semantics=sem, vmem_limit_bytes=VMEM_LIMIT)


def _nt_dot(a, b, precision=None):
    return lax.dot_general(a, b, (((1,), (1,)), ((), ())), precision=precision,
                           preferred_element_type=F32)


def _pack_halves(x):
    n = x.shape[1] // 2
    lo = pltpu.bitcast(x[:, :n].astype(BF16).astype(F32), jnp.uint32) >> 16
    hi = pltpu.bitcast(x[:, n:].astype(BF16).astype(F32), jnp.uint32) & jnp.uint32(0xFFFF0000)
    return lo | hi


def _unpack_halves(w):
    return (pltpu.bitcast(w << 16, F32), pltpu.bitcast(w & jnp.uint32(0xFFFF0000), F32))


def _rms_mod(x, g, sc, sh):
    xn = x * lax.rsqrt(jnp.mean(x * x, axis=-1, keepdims=True) + EPS)
    return (xn * g) * (1.0 + sc) + sh


def _ada_kernel(c_ref, w_ref, b_ref, o_ref):
    c = c_ref[...]
    ca = c * jax.nn.sigmoid(c)
    o_ref[...] = jnp.sum(ca * w_ref[...], axis=0, keepdims=True) + b_ref[...]


def _ada(c_col, w, b):
    d, n = w.shape
    tn = min(1024, n)
    return pl.pallas_call(
        _ada_kernel,
        grid=(n // tn,),
        in_specs=[pl.BlockSpec((d, 1), lambda j: (0, 0)),
                  pl.BlockSpec((d, tn), lambda j: (0, j)),
                  pl.BlockSpec((1, tn), lambda j: (0, j))],
        out_specs=pl.BlockSpec((1, tn), lambda j: (0, j)),
        out_shape=jax.ShapeDtypeStruct((1, n), F32),
        compiler_params=_cparams(("arbitrary",)),
        name="ada",
    )(c_col, w, b)


def _inproj_kernel(x_ref, g_ref, sc_ref, sh_ref, w_ref, wga_ref, o_ref, ga_ref, h_scr):
    @pl.when(pl.program_id(1) == 0)
    def _():
        h = _rms_mod(x_ref[...], g_ref[...], sc_ref[...], sh_ref[...]).astype(BF16)
        h_scr[...] = h
        ga_ref[...] = jnp.dot(h, wga_ref[...], preferred_element_type=F32)

    o_ref[...] = jnp.dot(h_scr[...], w_ref[...], preferred_element_type=F32).astype(BF16)


def _inproj(x, g, sc, sh, w, wga, tm, tn):
    s, d = x.shape
    n = w.shape[1]
    vec = pl.BlockSpec((1, d), lambda i, j: (0, 0))
    return pl.pallas_call(
        _inproj_kernel,
        grid=(s // tm, n // tn),
        in_specs=[pl.BlockSpec((tm, d), lambda i, j: (i, 0)), vec, vec, vec,
                  pl.BlockSpec((d, tn), lambda i, j: (0, j)),
                  pl.BlockSpec((d, 128), lambda i, j: (0, 0))],
        out_specs=[pl.BlockSpec((tm, tn), lambda i, j: (i, j)),
                   pl.BlockSpec((tm, 128), lambda i, j: (i, 0))],
        out_shape=[jax.ShapeDtypeStruct((s, n), BF16), jax.ShapeDtypeStruct((s, 128), F32)],
        scratch_shapes=[pltpu.VMEM((tm, d), BF16)],
        compiler_params=_cparams(("arbitrary", "arbitrary")),
        name="inproj",
    )(x, g, sc, sh, w, wga)


def _gla_kernel(q_ref, kt_ref, v_ref, gg_ref, ga_ref, wa2t_ref, ba_ref, on_ref, o_ref,
                state_ref, o_scr):
    tt = q_ref.shape[0]
    nchunk = tt // CHUNK

    @pl.when(pl.program_id(0) == 0)
    def _():
        state_ref[...] = jnp.zeros_like(state_ref)

    zt = _nt_dot(wa2t_ref[...], ga_ref[...], precision=HIGHEST) + ba_ref[...]
    lat = (jnp.minimum(zt, 0.0) - jnp.log1p(jnp.exp(-jnp.abs(zt)))) * (1.0 / GLA_TAU)
    row = lax.broadcasted_iota(jnp.int32, (tt, tt), 0)
    col = lax.broadcasted_iota(jnp.int32, (tt, tt), 1)
    same = (row // CHUNK) == (col // CHUNK)
    incl = jnp.where(same & (row <= col), 1.0, 0.0).astype(BF16)
    full = jnp.where(same, 1.0, 0.0).astype(BF16)
    lat_hi = lat.astype(BF16)
    lat_lo = (lat - lat_hi.astype(F32)).astype(BF16)
    cumt = (jnp.dot(lat_hi, incl, preferred_element_type=F32)
            + jnp.dot(lat_lo, incl, preferred_element_type=F32))
    tott = (jnp.dot(lat_hi, full, preferred_element_type=F32)
            + jnp.dot(lat_lo, full, preferred_element_type=F32))
    kdt = kt_ref[...].astype(F32) * jnp.exp(tott - cumt)
    dec = jnp.exp(tott)

    lane = lax.broadcasted_iota(jnp.int32, (GLA_DK, 2 * CHUNK), 1)
    upd = {}
    for c in range(nchunk):
        pair = (c // 2) * 2 * CHUNK
        if nchunk > 1:
            keep = (lane // CHUNK) == (c % 2)
        for h in range(GLA_HEADS):
            rows = slice(h * GLA_DK, (h + 1) * GLA_DK)
            vcols = slice(h * GLA_DV, (h + 1) * GLA_DV)
            if nchunk > 1:
                a = jnp.where(keep, kdt[rows, pair:pair + 2 * CHUNK], 0.0).astype(BF16)
                vp = v_ref[pair:pair + 2 * CHUNK, vcols]
            else:
                a = kdt[rows, :].astype(BF16)
                vp = v_ref[:, vcols]
            upd[c, h] = jnp.dot(a, vp, preferred_element_type=F32)

    for h in range(GLA_HEADS):
        rows = slice(h * GLA_DK, (h + 1) * GLA_DK)
        vcols = slice(h * GLA_DV, (h + 1) * GLA_DV)
        st = state_ref[h]
        states = []
        for c in range(nchunk):
            st = st * dec[rows, c * CHUNK:c * CHUNK + 1] + upd[c, h]
            states.append(st.astype(BF16))
        state_ref[h] = st
        for c in range(nchunk):
            o_scr[c * CHUNK:(c + 1) * CHUNK, vcols] = jnp.dot(
                q_ref[c * CHUNK:(c + 1) * CHUNK, rows], states[c], preferred_element_type=F32)

    for h in range(GLA_HEADS):
        vcols = slice(h * GLA_DV, (h + 1) * GLA_DV)
        o = o_scr[:, vcols] * (GLA_DK ** -0.5)
        o = o * lax.rsqrt(jnp.mean(o * o, axis=-1, keepdims=True) + EPS) * on_ref[...]
        g = gg_ref[:, vcols].astype(F32)
        o_ref[:, vcols] = (o * (g * jax.nn.sigmoid(g))).astype(BF16)


def _gla(proj, projt, ga, wa2t, ba_col, on_g, tt, col_q, col_v, col_g, row_k):
    s = proj.shape[0]
    qk = GLA_HEADS * GLA_DK
    vw = GLA_HEADS * GLA_DV
    return pl.pallas_call(
        _gla_kernel,
        grid=(s // tt,),
        in_specs=[pl.BlockSpec((tt, qk), lambda i: (i, col_q // qk)),
                  pl.BlockSpec((qk, tt), lambda i: (row_k // qk, i)),
                  pl.BlockSpec((tt, vw), lambda i: (i, col_v // vw)),
                  pl.BlockSpec((tt, vw), lambda i: (i, col_g // vw)),
                  pl.BlockSpec((tt, 128), lambda i: (i, 0)),
                  pl.BlockSpec((qk, 128), lambda i: (0, 0)),
                  pl.BlockSpec((qk, 1), lambda i: (0, 0)),
                  pl.BlockSpec((1, GLA_DV), lambda i: (0, 0))],
        out_specs=pl.BlockSpec((tt, vw), lambda i: (i, 0)),
        out_shape=jax.ShapeDtypeStruct((s, vw), BF16),
        scratch_shapes=[pltpu.VMEM((GLA_HEADS, GLA_DK, GLA_DV), F32),
                        pltpu.VMEM((tt, vw), F32)],
        compiler_params=_cparams(("arbitrary",)),
        name="gla",
    )(proj, projt, proj, proj, ga, wa2t, ba_col, on_g)


def _qknorm_rope_t(xt, g_col, cos, sin):
    n, tm = xt.shape
    x3 = xt.reshape(n // DIFF_DH, DIFF_DH, tm)
    r = lax.rsqrt(jnp.mean(x3 * x3, axis=1, keepdims=True) + EPS)
    y = x3 * r * g_col[None]
    half = ROT_DIM // 2
    y1, y2, rest = y[:, :half], y[:, half:ROT_DIM], y[:, ROT_DIM:]
    o1 = y1 * cos[None] - y2 * sin[None]
    o2 = y2 * cos[None] + y1 * sin[None]
    return jnp.concatenate([o1, o2, rest], axis=1)


def _seg_norms(x3):
    return jnp.sqrt(jnp.sum(x3 * x3, axis=1)).reshape(DIFF_HEADS, 2, x3.shape[2])


def _qkvprep_kernel(x_ref, g_ref, sc_ref, sh_ref, wq_ref, wk_ref, wv_ref, wgk_ref, pos_ref,
                    invf_ref, qg_ref, kg_ref, qa_ref, qb_ref, ko_ref, ve_ref, qn_ref, kn_ref,
                    gk_ref):
    tm = x_ref.shape[0]
    h = _rms_mod(x_ref[...], g_ref[...], sc_ref[...], sh_ref[...]).astype(BF16)
    qt = _nt_dot(wq_ref[...], h)
    kt = _nt_dot(wk_ref[...], h)
    vt = _nt_dot(wv_ref[...], h)
    gk_ref[...] = _nt_dot(wgk_ref[...], h).astype(BF16)
    ang = pos_ref[...].astype(F32) * invf_ref[...]
    cos, sin = jnp.cos(ang), jnp.sin(ang)
    q3 = _qknorm_rope_t(qt, qg_ref[...], cos, sin) * (DIFF_DH ** -0.5 * LOG2E)
    qn_ref[...] = _seg_norms(q3)
    seg = lax.broadcasted_iota(jnp.int32, q3.shape, 0)
    qa_ref[...] = jnp.where(seg % 2 == 0, q3, 0.0).reshape(-1, tm).astype(BF16)
    qb_ref[...] = jnp.where(seg % 2 == 1, q3, 0.0).reshape(-1, tm).astype(BF16)
    k3 = _qknorm_rope_t(kt, kg_ref[...], cos, sin)
    ko_ref[...] = k3.reshape(-1, tm).T.astype(BF16)
    kn_ref[...] = _seg_norms(k3)
    v3 = vt.astype(BF16).reshape(DIFF_HEADS, DIFF_DV, tm)
    ones = jnp.ones((DIFF_HEADS, ATT_SUM_ROWS, tm), BF16)
    ve_ref[...] = jnp.concatenate([v3, ones], axis=1).reshape(-1, tm)


def _qkvprep(x, g, sc, sh, wq_t, wk_t, wv_t, wgk_t, pos_row, invf_col, qg_col, kg_col, tm):
    s, d = x.shape
    n = DIFF_HEADS * 2 * DIFF_DH
    ne = DIFF_HEADS * (DIFF_DV + ATT_SUM_ROWS)
    ngk = wgk_t.shape[0]
    col = pl.BlockSpec((DIFF_DH, 1), lambda i: (0, 0))
    vec = pl.BlockSpec((1, d), lambda i: (0, 0))
    wspec = pl.BlockSpec((n, d), lambda i: (0, 0))
    return pl.pallas_call(
        _qkvprep_kernel,
        grid=(s // tm,),
        in_specs=[pl.BlockSpec((tm, d), lambda i: (i, 0)), vec, vec, vec,
                  wspec, wspec, wspec, pl.BlockSpec((ngk, d), lambda i: (0, 0)),
                  pl.BlockSpec((1, tm), lambda i: (0, i)),
                  pl.BlockSpec((ROT_DIM // 2, 1), lambda i: (0, 0)), col, col],
        out_specs=[pl.BlockSpec((n, tm), lambda i: (0, i)),
                   pl.BlockSpec((n, tm), lambda i: (0, i)),
                   pl.BlockSpec((tm, n), lambda i: (i, 0)),
                   pl.BlockSpec((ne, tm), lambda i: (0, i)),
                   pl.BlockSpec((DIFF_HEADS, 2, tm), lambda i: (0, 0, i)),
                   pl.BlockSpec((DIFF_HEADS, 2, tm), lambda i: (0, 0, i)),
                   pl.BlockSpec((ngk, tm), lambda i: (0, i))],
        out_shape=[jax.ShapeDtypeStruct((n, s), BF16), jax.ShapeDtypeStruct((n, s), BF16),
                   jax.ShapeDtypeStruct((s, n), BF16), jax.ShapeDtypeStruct((ne, s), BF16),
                   jax.ShapeDtypeStruct((DIFF_HEADS, 2, s), F32),
                   jax.ShapeDtypeStruct((DIFF_HEADS, 2, s), F32),
                   jax.ShapeDtypeStruct((ngk, s), BF16)],
        compiler_params=_cparams(("arbitrary",)),
        name="qkvprep",
    )(x, g, sc, sh, wq_t, wk_t, wv_t, wgk_t, pos_row, invf_col, qg_col, kg_col)


ATT_COLS = 256
ATT_LOOKAHEAD = {True: 2, False: 4}
ATT_BODY_TILES = {True: 8, False: 2}
ATT_SUM_ROWS = 16
ATT_BOUND_SLACK = 1.02
ATT_BOUND_LIMIT = 50.0


def _diffattn_kernel(qa_ref, qb_ref, k_ref, vt_ref, qn_ref, kn_ref, lq1_ref, lk1_ref, lq2_ref,
                     lk2_ref, sg_ref, o_ref, *scr, lambda_init, tk, cols):
    tq = qa_ref.shape[1]
    nblk = 2 * tq // cols
    m_scr, acc_scr = (scr[b * nblk:(b + 1) * nblk] for b in range(2))
    kmax_scr = scr[2 * nblk]
    s_scr = scr[2 * nblk + 1:]
    i = pl.program_id(1)

    @pl.when(i == 0)
    def _():
        kmax_scr[...] = jnp.max(kn_ref[...], axis=1, keepdims=True)

    bound = qn_ref[...] * kmax_scr[...] * ATT_BOUND_SLACK
    bound = jnp.concatenate([bound[0:1], bound[1:2]], axis=1)
    bounded = jnp.max(bound) < ATT_BOUND_LIMIT
    for c in range(nblk):
        m_scr[c][...] = jnp.where(bounded, bound[:, c * cols:(c + 1) * cols], NEG_BIG)
        acc_scr[c][...] = jnp.zeros_like(acc_scr[c])

    def scores(j, c):
        start = pl.multiple_of(j * tk, tk)
        q_ref = qa_ref if c * cols < tq else qb_ref
        off = (c * cols) % tq
        return jnp.dot(k_ref[pl.ds(start, tk), :], q_ref[:, off:off + cols],
                       preferred_element_type=F32)

    def steps(tiles, masked, next_tile, fixed):
        look = ATT_LOOKAHEAD[fixed]
        items = [(j, c) for j in tiles for c in range(nblk)]
        pending = []
        for n, (j, c) in enumerate(items):
            s = s_scr[n][...] if n < look else pending.pop(0)
            ahead = n + look
            if ahead < len(items):
                pending.append(scores(*items[ahead]))
            elif next_tile is not None:
                s_scr[ahead - len(items)][...] = scores(next_tile, ahead - len(items))
            start = pl.multiple_of(j * tk, tk)
            keys = tk
            if masked:
                keys = (c * cols) % tq + cols
                s = s[:keys]
                krow = lax.broadcasted_iota(jnp.int32, (keys, cols), 0)
                qcol = lax.broadcasted_iota(jnp.int32, (keys, cols), 1)
                qpos = i * tq + (c * cols) % tq + qcol
                s = jnp.where((start + krow) // CHUNK <= qpos // CHUNK, s, NEG_BIG)
            vj = vt_ref[:, pl.ds(start, keys)]
            if fixed:
                p = jnp.exp2(s - m_scr[c][...]).astype(BF16)
                acc_scr[c][...] += jnp.dot(vj, p, preferred_element_type=F32)
            else:
                m_prev = m_scr[c][...]
                m_new = jnp.maximum(m_prev, jnp.max(s, axis=0, keepdims=True))
                alpha = jnp.exp2(m_prev - m_new)
                p = jnp.exp2((s - m_new).astype(BF16))
                acc_scr[c][...] = alpha * acc_scr[c][...] + jnp.dot(
                    vj, p, preferred_element_type=F32)
                m_scr[c][...] = m_new

    def attend(fixed):
        for c in range(ATT_LOOKAHEAD[fixed]):
            s_scr[c][...] = scores(0, c)
        big = ATT_BODY_TILES[fixed]
        lax.fori_loop(0, i // big, lambda t, c: (steps(
            tuple(big * t + u for u in range(big)), False, big * t + big, fixed), c)[1], 0)
        size = big // 2
        while size >= 1:
            first = (i // (2 * size)) * (2 * size)

            @pl.when(i % (2 * size) >= size)
            def _(first=first, size=size):
                steps(tuple(first + u for u in range(size)), False, first + size, fixed)

            size //= 2
        steps((i,), True, None, fixed)

    @pl.when(bounded)
    def _():
        attend(True)

    @pl.when(jnp.logical_not(bounded))
    def _():
        attend(False)

    o = jnp.concatenate([acc_scr[c][:DIFF_DV] * (1.0 / acc_scr[c][DIFF_DV:DIFF_DV + 1])
                         for c in range(nblk)], axis=1)
    lam = (jnp.exp(jnp.sum(lq1_ref[...] * lk1_ref[...]))
           - jnp.exp(jnp.sum(lq2_ref[...] * lk2_ref[...])) + lambda_init)
    o = o[:, :tq] - lam * o[:, tq:]
    o = o * lax.rsqrt(jnp.mean(o * o, axis=0, keepdims=True) + EPS) * sg_ref[...]
    o_ref[...] = (o * (1.0 - lambda_init)).T.astype(BF16)


def _diffattn(qat, qbt, kr, vte, qn, kn, lq1, lk1, lq2, lk2, sg_col, lambda_init, tq, tk):
    s = kr.shape[0]
    hd = 2 * DIFF_DH
    vec = pl.BlockSpec((1, DIFF_DH), lambda h, i: (0, 0))
    cols = min(ATT_COLS, tq)
    nblk = 2 * tq // cols
    kern = functools.partial(_diffattn_kernel, lambda_init=lambda_init, tk=tk, cols=cols)
    dve = DIFF_DV + ATT_SUM_ROWS
    assert tq == tk and nblk >= max(ATT_LOOKAHEAD.values())
    scratch = ([pltpu.VMEM((1, cols), F32)] * nblk + [pltpu.VMEM((dve, cols), F32)] * nblk
               + [pltpu.VMEM((2, 1), F32)]
               + [pltpu.VMEM((tk, cols), F32)] * max(ATT_LOOKAHEAD.values()))
    return pl.pallas_call(
        kern,
        grid=(DIFF_HEADS, s // tq),
        in_specs=[pl.BlockSpec((hd, tq), lambda h, i: (h, i)),
                  pl.BlockSpec((hd, tq), lambda h, i: (h, i)),
                  pl.BlockSpec((s, hd), lambda h, i: (0, h)),
                  pl.BlockSpec((dve, s), lambda h, i: (h, 0)),
                  pl.BlockSpec((None, 2, tq), lambda h, i: (h, 0, i)),
                  pl.BlockSpec((None, 2, s), lambda h, i: (h, 0, 0)),
                  vec, vec, vec, vec,
                  pl.BlockSpec((DIFF_DV, 1), lambda h, i: (0, 0))],
        out_specs=pl.BlockSpec((tq, DIFF_DV), lambda h, i: (i, h)),
        out_shape=jax.ShapeDtypeStruct((s, DIFF_HEADS * DIFF_DV), BF16),
        scratch_shapes=scratch,
        compiler_params=_cparams(("arbitrary", "arbitrary")),
        name="diffattn",
    )(qat, qbt, kr, vte, qn, kn, lq1, lk1, lq2, lk2, sg_col)


def _mergeout_kernel(og_ref, od_ref, mg_ref, md_ref, x_ref, wbg_ref, wbd_ref, wo_ref, gt_ref,
                     g2_ref, sc_ref, sh_ref, x1_ref, hf_ref, hfp_ref):
    bg = jnp.dot(og_ref[...], wbg_ref[...], preferred_element_type=F32)
    bd = jnp.dot(od_ref[...], wbd_ref[...], preferred_element_type=F32)
    merged = (jax.nn.sigmoid(mg_ref[...].astype(F32)) * bg
              + jax.nn.sigmoid(md_ref[...].astype(F32)) * bd)
    x1 = x_ref[...] + gt_ref[...] * jnp.dot(merged.astype(BF16), wo_ref[...],
                                             preferred_element_type=F32)
    x1_ref[...] = x1
    hf = _rms_mod(x1, g2_ref[...], sc_ref[...], sh_ref[...])
    hf_ref[...] = hf
    hfp_ref[...] = _pack_halves(hf)


def _mergeout(og, od, proj, x, wbg, wbd, wo, gt, g2, sc, sh, tm, col_mg, col_md):
    s, d = x.shape
    vec = pl.BlockSpec((1, d), lambda i: (0, 0))
    wspec = pl.BlockSpec((d, d), lambda i: (0, 0))
    row = pl.BlockSpec((tm, d), lambda i: (i, 0))
    return pl.pallas_call(
        _mergeout_kernel,
        grid=(s // tm,),
        in_specs=[row, row,
                  pl.BlockSpec((tm, d), lambda i: (i, col_mg // d)),
                  pl.BlockSpec((tm, d), lambda i: (i, col_md // d)),
                  row, wspec, wspec, wspec, vec, vec, vec, vec],
        out_specs=[row, row, pl.BlockSpec((tm, d // 2), lambda i: (i, 0))],
        out_shape=[jax.ShapeDtypeStruct((s, d), F32), jax.ShapeDtypeStruct((s, d), F32),
                   jax.ShapeDtypeStruct((s, d // 2), jnp.uint32)],
        compiler_params=_cparams(("arbitrary",)),
        name="mergeout",
    )(og, od, proj, proj, x, wbg, wbd, wo, gt, g2, sc, sh)


def _route_kernel(hf_ref, wrt_ref, bias_ref, idx_ref, wts_ref, rnk_ref, cnt_ref, run_scr):
    tr = hf_ref.shape[0]
    e = wrt_ref.shape[0]
    gsz = e // N_GROUPS

    @pl.when(pl.program_id(0) == 0)
    def _():
        run_scr[...] = jnp.zeros_like(run_scr)

    logits = _nt_dot(wrt_ref[...], hf_ref[...], precision=HIGHEST)
    scores = jax.nn.sigmoid(logits)
    biased = scores + bias_ref[...]
    g3 = biased.reshape(N_GROUPS, gsz, tr)
    m1 = jnp.max(g3, axis=1, keepdims=True)
    n_top = jnp.sum(jnp.where(g3 == m1, 1.0, 0.0), axis=1, keepdims=True)
    m2 = jnp.max(jnp.where(g3 < m1, g3, -jnp.inf), axis=1, keepdims=True)
    gs = (m1 + jnp.where(n_top >= 2.0, m1, m2)).reshape(N_GROUPS, tr)
    gi = lax.broadcasted_iota(jnp.int32, (N_GROUPS, tr), 0)
    beaten = jnp.zeros((N_GROUPS, tr), F32)
    for g in range(N_GROUPS):
        other = gs[g:g + 1, :]
        beaten = beaten + jnp.where((other > gs) | ((other == gs) & (g < gi)), 1.0, 0.0)
    gsel = (beaten < float(TOPK_GROUPS)).reshape(N_GROUPS, 1, tr)
    masked = jnp.where(gsel, g3, -jnp.inf).reshape(e, tr)

    ids = lax.broadcasted_iota(jnp.int32, (e, tr), 0)
    chosen = jnp.zeros((e, tr), F32)
    sel_idx, sel_score = [], []
    for _ in range(TOP_K):
        mx = jnp.max(masked, axis=0, keepdims=True)
        ix = jnp.min(jnp.where(masked == mx, ids, e), axis=0, keepdims=True)
        hit = ids == ix
        sel_idx.append(ix)
        sel_score.append(jnp.sum(jnp.where(hit, scores, 0.0), axis=0, keepdims=True))
        chosen = jnp.where(hit, 1.0, chosen)
        masked = jnp.where(hit, -jnp.inf, masked)
    idx = jnp.concatenate(sel_idx, axis=0)
    sc = jnp.concatenate(sel_score, axis=0)
    idx_ref[...] = idx
    wts_ref[...] = sc / jnp.sum(sc, axis=0, keepdims=True) * ROUTED_SCALE

    row = lax.broadcasted_iota(jnp.int32, (tr, tr), 0)
    col = lax.broadcasted_iota(jnp.int32, (tr, tr), 1)
    before = jnp.where(row < col, 1.0, 0.0).astype(BF16)
    prior = jnp.dot(chosen.astype(BF16), before, preferred_element_type=F32) + run_scr[:, 0:1]
    rnk_ref[...] = jnp.concatenate(
        [jnp.sum(jnp.where(ids == sel_idx[k], prior, 0.0), axis=0, keepdims=True)
         for k in range(TOP_K)], axis=0).astype(jnp.int32)
    run_scr[...] = run_scr[...] + jnp.sum(chosen, axis=1, keepdims=True)
    cnt_ref[...] = run_scr[...].astype(jnp.int32)


def _route(hf, wrt, bias_col, tr):
    s, d = hf.shape
    e = wrt.shape[0]
    tok = pl.BlockSpec((TOP_K, tr), lambda i: (0, i))
    return pl.pallas_call(
        _route_kernel,
        grid=(s // tr,),
        in_specs=[pl.BlockSpec((tr, d), lambda i: (i, 0)),
                  pl.BlockSpec((e, d), lambda i: (0, 0)),
                  pl.BlockSpec((e, 1), lambda i: (0, 0))],
        out_specs=[tok, tok, tok, pl.BlockSpec((e, 128), lambda i: (0, 0))],
        out_shape=[jax.ShapeDtypeStruct((TOP_K, s), jnp.int32),
                   jax.ShapeDtypeStruct((TOP_K, s), F32),
                   jax.ShapeDtypeStruct((TOP_K, s), jnp.int32),
                   jax.ShapeDtypeStruct((e, 128), jnp.int32)],
        scratch_shapes=[pltpu.VMEM((e, 128), F32)],
        compiler_params=_cparams(("arbitrary",)),
        name="route",
    )(hf, wrt, bias_col)


def _positions_kernel(idx_ref, rnk_ref, pstart_ref, pos_ref):
    e = pstart_ref.shape[0]
    ts = idx_ref.shape[1]
    ids = lax.broadcasted_iota(jnp.int32, (e, ts), 0)
    idx = idx_ref[...]
    pos_ref[...] = rnk_ref[...] + jnp.concatenate(
        [jnp.sum(jnp.where(ids == idx[k:k + 1, :], pstart_ref[...], 0), axis=0, keepdims=True)
         for k in range(TOP_K)], axis=0)


def _positions(idx, rnk, pstart_col, ts):
    s = idx.shape[1]
    e = pstart_col.shape[0]
    tok = pl.BlockSpec((TOP_K, ts), lambda i: (0, i))
    return pl.pallas_call(
        _positions_kernel,
        grid=(s // ts,),
        in_specs=[tok, tok, pl.BlockSpec((e, 1), lambda i: (0, 0))],
        out_specs=tok,
        out_shape=jax.ShapeDtypeStruct((TOP_K, s), jnp.int32),
        compiler_params=_cparams(("arbitrary",)),
        name="positions",
    )(idx, rnk, pstart_col)


def _swiglu_packed(xp, wg, wu, wd):
    lo, hi = _unpack_halves(xp)
    lo, hi = lo.astype(BF16), hi.astype(BF16)
    n = lo.shape[1]
    g = (jnp.dot(lo, wg[:n], preferred_element_type=F32)
         + jnp.dot(hi, wg[n:], preferred_element_type=F32))
    u = (jnp.dot(lo, wu[:n], preferred_element_type=F32)
         + jnp.dot(hi, wu[n:], preferred_element_type=F32))
    h = (g * jax.nn.sigmoid(g)) * u
    return jnp.dot(h.astype(BF16), wd[...], preferred_element_type=F32)


def _moe_kernel(ie_ref, ib_ref, first_ref, slot_ref, ne_ref, lead_ref, rows_ref, nv_ref, xs_ref,
                wg_hbm, wu_hbm, wd_hbm, ys_ref, wg_f, wu_f, wd_f, sem):
    del ib_ref
    i = pl.program_id(0)

    def fetch(e, slot):
        copies = []
        for n, (src, dst) in enumerate(((wg_hbm, wg_f), (wu_hbm, wu_f), (wd_hbm, wd_f))):
            rows = src.shape[1] // MOE_DMA_CHUNKS
            for ch in range(MOE_DMA_CHUNKS):
                part = pl.ds(ch * rows, rows)
                copies.append(pltpu.make_async_copy(
                    src.at[e, part], dst.at[slot, part], sem.at[slot, n * MOE_DMA_CHUNKS + ch]))
        return copies

    @pl.when(i == 0)
    def _():
        for s in range(MOE_SLOTS - 1):
            @pl.when(lead_ref[s] >= 0)
            def _(s=s):
                for cp in fetch(lead_ref[s], s):
                    cp.start()

    @pl.when(i < nv_ref[0])
    def _():
        for slot in range(MOE_SLOTS):
            @pl.when((first_ref[i] == 1) & (slot_ref[i] == slot))
            def _(slot=slot):
                for cp in fetch(ie_ref[i], slot):
                    cp.wait()

                @pl.when(ne_ref[i] >= 0)
                def _():
                    for cp in fetch(ne_ref[i], (slot + MOE_SLOTS - 1) % MOE_SLOTS):
                        cp.start()

        slot = slot_ref[i]
        row = lax.broadcasted_iota(jnp.int32, xs_ref.shape, 0)
        xp = jnp.where(row < rows_ref[i], xs_ref[...], jnp.uint32(0))
        ys_ref[...] = _pack_halves(_swiglu_packed(
            xp, wg_f[slot].astype(BF16), wu_f[slot].astype(BF16), wd_f[slot].astype(BF16)))


def _moe(item_e, item_b, item_first, item_slot, item_next, lead, item_rows, n_valid, xs, wg, wu,
         wd):
    m_pad, dh = xs.shape
    _, d, f = wg.shape
    n_items = item_e.shape[0]
    blk = lambda i, ie, ib, fi, sl, ne, ld, nr, nv: (ib[i], 0)
    hbm = pl.BlockSpec(memory_space=pl.ANY)
    return pl.pallas_call(
        _moe_kernel,
        grid_spec=pltpu.PrefetchScalarGridSpec(
            num_scalar_prefetch=8,
            grid=(n_items,),
            in_specs=[pl.BlockSpec((MOE_ROWS, dh), blk), hbm, hbm, hbm],
            out_specs=pl.BlockSpec((MOE_ROWS, dh), blk),
            scratch_shapes=[pltpu.VMEM((MOE_SLOTS, d, f), F32), pltpu.VMEM((MOE_SLOTS, d, f), F32),
                            pltpu.VMEM((MOE_SLOTS, f, d), F32),
                            pltpu.SemaphoreType.DMA((MOE_SLOTS, 3 * MOE_DMA_CHUNKS))],
        ),
        out_shape=jax.ShapeDtypeStruct((m_pad, dh), jnp.uint32),
        compiler_params=_cparams(("arbitrary",)),
        name="moe",
    )(item_e, item_b, item_first, item_slot, item_next, lead, item_rows, n_valid, xs, wg, wu, wd)


def _sc_gather_rows(table, idx_row):
    m = idx_row.shape[1]
    w = table.shape[1]
    idx_row = idx_row.reshape(m // SC_GATHER_WINDOW, SC_GATHER_WINDOW)
    mesh = plsc.VectorSubcoreMesh(core_axis_name="c", subcore_axis_name="s")

    @functools.partial(pl.kernel, mesh=mesh,
                       out_type=jax.ShapeDtypeStruct((m, w), table.dtype))
    def gather(table_hbm, idx_hbm, out_hbm):
        def body(idx_vmem, out_vmem):
            pltpu.sync_copy(table_hbm.at[idx_vmem.at[0]], out_vmem)

        pltpu.emit_pipeline(
            body,
            grid=(m // SC_GATHER_WINDOW,),
            in_specs=[pl.BlockSpec((1, SC_GATHER_WINDOW), lambda i: (i, 0))],
            out_specs=[pl.BlockSpec((SC_GATHER_WINDOW, w), lambda i: (i, 0))],
            core_axis_name=("c", "s"),
            dimension_semantics=(pltpu.PARALLEL,),
        )(idx_hbm, out_hbm)

    return gather(table, idx_row)


def _sc_scatter_rows(rows, idx_blocks, m_out):
    s, w = rows.shape
    mesh = plsc.VectorSubcoreMesh(core_axis_name="c", subcore_axis_name="s")

    @functools.partial(pl.kernel, mesh=mesh,
                       out_type=jax.ShapeDtypeStruct((m_out, w), rows.dtype))
    def scatter(rows_hbm, idx_hbm, out_hbm):
        def body(rows_vmem, idx_vmem):
            for k in range(TOP_K):
                pltpu.sync_copy(rows_vmem, out_hbm.at[idx_vmem.at[k]])

        pltpu.emit_pipeline(
            body,
            grid=(s // SC_GATHER_WINDOW,),
            in_specs=[pl.BlockSpec((SC_GATHER_WINDOW, w), lambda i: (i, 0)),
                      pl.BlockSpec((TOP_K, SC_GATHER_WINDOW), lambda i: (i, 0))],
            out_specs=[],
            core_axis_name=("c", "s"),
            dimension_semantics=(pltpu.PARALLEL,),
        )(rows_hbm, idx_hbm)

    return scatter(rows, idx_blocks)


def _combine_kernel(wt_ref, hf_ref, x1_ref, gt_ref, sg_ref, su_ref, sd_ref, g_ref, o_ref):
    tc = x1_ref.shape[0]
    y = _swiglu_packed(hf_ref[...], sg_ref, su_ref, sd_ref)
    wt = wt_ref[...]
    n = g_ref.shape[2]
    r_lo = jnp.zeros((tc, n), F32)
    r_hi = jnp.zeros((tc, n), F32)
    for k in range(TOP_K):
        lo, hi = _unpack_halves(g_ref[k])
        r_lo = r_lo + lo * wt[:, k:k + 1]
        r_hi = r_hi + hi * wt[:, k:k + 1]
    y = y + jnp.concatenate([r_lo, r_hi], axis=1)
    o_ref[...] = x1_ref[...] + gt_ref[...] * y


def _combine(wts_t, hfp, x1, gt, sg, su, sd, gathered, tc):
    s, d = x1.shape
    f = sg.shape[1]
    row = pl.BlockSpec((tc, d), lambda i: (i, 0))
    return pl.pallas_call(
        _combine_kernel,
        grid=(s // tc,),
        in_specs=[pl.BlockSpec((tc, TOP_K), lambda i: (i, 0)),
                  pl.BlockSpec((tc, d // 2), lambda i: (i, 0)), row,
                  pl.BlockSpec((1, d), lambda i: (0, 0)),
                  pl.BlockSpec((d, f), lambda i: (0, 0)),
                  pl.BlockSpec((d, f), lambda i: (0, 0)),
                  pl.BlockSpec((f, d), lambda i: (0, 0)),
                  pl.BlockSpec((TOP_K, tc, d // 2), lambda i: (0, i, 0))],
        out_specs=row,
        out_shape=jax.ShapeDtypeStruct((s, d), F32),
        compiler_params=_cparams(("arbitrary",)),
        name="combine",
    )(wts_t, hfp, x1, gt, sg, su, sd, gathered)


def _tile(n, want):
    t = min(n, want)
    assert n % t == 0, (n, t)
    return t


def _layer(l, x, c_col, pos_row, p):
    s, d = x.shape
    lambda_init = 0.8 - 0.6 * math.exp(-0.3 * l)
    gqk, gv = GLA_HEADS * GLA_DK, GLA_HEADS * GLA_DV
    dqk, dvw = DIFF_HEADS * 2 * DIFF_DH, DIFF_HEADS * DIFF_DV
    lowrank = p["gla_w_a2"].shape[0]

    mod = _ada(c_col, p["w_ada"], p["b_ada"][None, :])
    sh_a, sc_a, gt_a, sh_f, sc_f, gt_f = [mod[:, j * d:(j + 1) * d] for j in range(6)]

    w_in = p["w_in"]
    o = 0
    cols = {}
    for name, wdt in (("gq", gqk), ("gk", gqk), ("gv", gv), ("ga", lowrank), ("gg", gv),
                      ("dq", dqk), ("dk", dqk), ("dv", dvw), ("mg", d), ("md", d)):
        cols[name] = w_in[:, o:o + wdt]
        o += wdt
    row_names = ("gv", "gg", "mg", "md", "gq")
    w_row = jnp.concatenate([cols[n] for n in row_names], axis=1).astype(BF16)
    col_of, o = {}, 0
    for n in row_names:
        col_of[n] = o
        o += cols[n].shape[1]
    w_ga = jnp.pad(cols["ga"], ((0, 0), (0, 128 - lowrank))).astype(BF16)

    g1 = p["norm1_g"][None, :]
    tm = _tile(s, TILE_PROJ)
    ts = _tile(s, TILE_SEQ)
    proj, ga = _inproj(x, g1, sc_a, sh_a, w_row, w_ga, tm, w_row.shape[1] // 3)

    invf = ROPE_THETA ** (-jnp.arange(0, ROT_DIM, 2, dtype=F32) / ROT_DIM)
    qat, qbt, kr, vte, qn, kn, gkt = _qkvprep(
        x, g1, sc_a, sh_a, *(cols[n].T.astype(BF16) for n in ("dq", "dk", "dv", "gk")),
        pos_row, invf[:, None], p["diff_qnorm_g"][:, None], p["diff_knorm_g"][:, None], ts)

    wa2t = jnp.pad(p["gla_w_a2"].T, ((0, 0), (0, 128 - lowrank)))
    o_gla = _gla(proj, gkt, ga, wa2t, p["gla_b_a"][:, None], p["gla_onorm_g"][None, :],
                 ts, col_of["gq"], col_of["gv"], col_of["gg"], 0)
    tq = ts
    o_diff = _diffattn(qat, qbt, kr, vte, qn, kn, p["diff_lq1"][None, :], p["diff_lk1"][None, :],
                       p["diff_lq2"][None, :], p["diff_lk2"][None, :],
                       p["diff_subln_g"][:, None], lambda_init, tq, tq)

    x1, hf, hfp = _mergeout(o_gla, o_diff, proj, x, p["w_branch_gla"].astype(BF16),
                       p["w_branch_diff"].astype(BF16), p["w_out"].astype(BF16), gt_a,
                       p["norm2_g"][None, :], sc_f, sh_f, ts,
                       col_of["mg"], col_of["md"])

    e = p["w_router"].shape[1]
    idx, wts, rnk, cnt = _route(hf, p["w_router"].T, p["router_bias"][:, None], ts)

    counts = cnt[:, 0]
    pcounts = ((counts + MOE_ROWS - 1) // MOE_ROWS) * MOE_ROWS
    pend = jnp.cumsum(pcounts)
    pstart = pend - pcounts
    pos = _positions(idx, rnk, pstart[:, None], ts)
    n_items = (s * TOP_K) // MOE_ROWS + e
    n_valid = (pend[-1] // MOE_ROWS).astype(jnp.int32)
    item_b = jnp.minimum(jnp.arange(n_items, dtype=jnp.int32), n_valid - 1)
    item_e = jnp.minimum(jnp.sum(pend[None, :] <= (item_b * MOE_ROWS)[:, None], axis=1),
                         e - 1).astype(jnp.int32)

    wn = SC_GATHER_WINDOW
    pos_w = pos.reshape(TOP_K, s // wn, wn).transpose(1, 0, 2).reshape(s // wn * TOP_K, wn)
    xs = _sc_scatter_rows(hfp, pos_w, n_items * MOE_ROWS)
    item_rows = jnp.clip(pstart[item_e] + counts[item_e] - item_b * MOE_ROWS, 0,
                         MOE_ROWS).astype(jnp.int32)
    prev_e = jnp.concatenate([jnp.full((1,), -1, jnp.int32), item_e[:-1]])
    item_first = ((jnp.arange(n_items) < n_valid) & (item_e != prev_e)).astype(jnp.int32)
    item_slot = ((jnp.cumsum(item_first) - 1) % MOE_SLOTS).astype(jnp.int32)
    cand = jnp.where(pcounts > 0, jnp.arange(e, dtype=jnp.int32), e)
    nonempty_from = lax.cummin(cand[::-1])[::-1]
    following = jnp.concatenate([nonempty_from[1:], jnp.full((2,), e, jnp.int32)])
    ahead = item_e
    lead = [nonempty_from[0]]
    for _ in range(MOE_SLOTS - 1):
        ahead = following[ahead]
        lead.append(following[lead[-1]])
    item_next = jnp.where(ahead < e, ahead, -1).astype(jnp.int32)
    lead = jnp.stack(lead[:MOE_SLOTS - 1])
    lead = jnp.where(lead < e, lead, -1).astype(jnp.int32)
    ys = _moe(item_e, item_b, item_first, item_slot, item_next, lead, item_rows, n_valid[None],
              xs, p["w_exp_gate"], p["w_exp_up"], p["w_exp_down"])
    gathered = _sc_gather_rows(ys, pos.reshape(1, TOP_K * s)).reshape(TOP_K, s, d // 2)
    return _combine(wts.T, hfp, x1, gt_f, p["w_sh_gate"].astype(BF16),
                    p["w_sh_up"].astype(BF16), p["w_sh_down"].astype(BF16), gathered,
                    _tile(s, TILE_COMBINE))


_LAYER_PARAMS = ("w_ada", "b_ada", "norm1_g", "w_in", "gla_w_a2", "gla_b_a", "gla_onorm_g",
                 "diff_qnorm_g", "diff_knorm_g", "diff_lq1", "diff_lk1", "diff_lq2", "diff_lk2",
                 "diff_subln_g", "w_branch_gla", "w_branch_diff", "w_out", "norm2_g", "w_router",
                 "router_bias", "w_exp_gate", "w_exp_up", "w_exp_down", "w_sh_gate", "w_sh_up",
                 "w_sh_down")


def kernel(x, c, positions, w_ada, b_ada, norm1_g, w_in, gla_w_a2, gla_b_a, gla_onorm_g, diff_qnorm_g, diff_knorm_g, diff_lq1, diff_lk1, diff_lq2, diff_lk2, diff_subln_g, w_branch_gla, w_branch_diff, w_out, norm2_g, w_router, router_bias, w_exp_gate, w_exp_up, w_exp_down, w_sh_gate, w_sh_up, w_sh_down):
    stacked = dict(zip(_LAYER_PARAMS, (
        w_ada, b_ada, norm1_g, w_in, gla_w_a2, gla_b_a, gla_onorm_g, diff_qnorm_g, diff_knorm_g,
        diff_lq1, diff_lk1, diff_lq2, diff_lk2, diff_subln_g, w_branch_gla, w_branch_diff, w_out,
        norm2_g, w_router, router_bias, w_exp_gate, w_exp_up, w_exp_down, w_sh_gate, w_sh_up,
        w_sh_down)))
    b, s, d = x.shape
    assert b == 1, "single-sequence kernel"
    xl = x[0]
    c_col = c[0][:, None]
    pos_row = positions.astype(jnp.int32)
    for l in range(w_ada.shape[0]):
        xl = _layer(l, xl, c_col, pos_row, {k: v[l] for k, v in stacked.items()})
    return xl[None]
```

```python
import functools
import math

import jax
import jax.numpy as jnp
from jax import lax
from jax.experimental import pallas as pl
from jax.experimental.pallas import tpu as pltpu
from jax.experimental.pallas import tpu_sc as plsc

CHUNK = 64
EPS = 1e-6
GLA_HEADS = 4
GLA_DK = 128
GLA_DV = 256
GLA_TAU = 16.0
DIFF_HEADS = 8
DIFF_DH = 64
DIFF_DV = 2 * DIFF_DH
ROPE_THETA = 500000.0
ROT_DIM = DIFF_DH // 4
N_GROUPS = 8
TOPK_GROUPS = 4
TOP_K = 8
ROUTED_SCALE = 2.5

MOE_ROWS = 640
MOE_DMA_CHUNKS = 4
MOE_SLOTS = 3
SC_GATHER_WINDOW = 64
VMEM_LIMIT = 56 * 1024 * 1024
TILE_SEQ = 512
TILE_COMBINE = 256
NEG_BIG = -1e30
LOG2E = 1.4426950408889634
HIGHEST = lax.Precision.HIGHEST
F32 = jnp.float32
BF16 = jnp.bfloat16


def _cparams(sem):
    return pltpu.CompilerParams(dimension_semantics=sem, vmem_limit_bytes=VMEM_LIMIT)


def _nt_dot(a, b, precision=None):
    return lax.dot_general(a, b, (((1,), (1,)), ((), ())), precision=precision,
                           preferred_element_type=F32)


def _pack_halves(x):
    n = x.shape[1] // 2
    lo = pltpu.bitcast(x[:, :n].astype(BF16).astype(F32), jnp.uint32) >> 16
    hi = pltpu.bitcast(x[:, n:].astype(BF16).astype(F32), jnp.uint32) & jnp.uint32(0xFFFF0000)
    return lo | hi


def _unpack_halves(w):
    return (pltpu.bitcast(w << 16, F32), pltpu.bitcast(w & jnp.uint32(0xFFFF0000), F32))


def _rms_mod(x, g, sc, sh):
    xn = x * lax.rsqrt(jnp.mean(x * x, axis=-1, keepdims=True) + EPS)
    return (xn * g) * (1.0 + sc) + sh


def _ada_kernel(c_ref, w_ref, b_ref, o_ref):
    c = c_ref[...]
    ca = c * jax.nn.sigmoid(c)
    o_ref[...] = jnp.sum(ca * w_ref[...], axis=0, keepdims=True) + b_ref[...]


def _ada(c_col, w, b):
    d, n = w.shape
    tn = min(1024, n)
    return pl.pallas_call(
        _ada_kernel,
        grid=(n // tn,),
        in_specs=[pl.BlockSpec((d, 1), lambda j: (0, 0)),
                  pl.BlockSpec((d, tn), lambda j: (0, j)),
                  pl.BlockSpec((1, tn), lambda j: (0, j))],
        out_specs=pl.BlockSpec((1, tn), lambda j: (0, j)),
        out_shape=jax.ShapeDtypeStruct((1, n), F32),
        compiler_params=_cparams(("arbitrary",)),
        name="ada",
    )(c_col, w, b)


def _gla_kernel(x_ref, g1_ref, sc_ref, sh_ref, wq_ref, wv_ref, wg_ref, wga_ref, kt_ref, wa2t_ref,
                ba_ref, on_ref, o_ref, state_ref, o_scr, q_ref, v_ref, gg_ref):
    tt = x_ref.shape[0]
    nchunk = tt // CHUNK

    @pl.when(pl.program_id(0) == 0)
    def _():
        state_ref[...] = jnp.zeros_like(state_ref)

    h_in = _rms_mod(x_ref[...], g1_ref[...], sc_ref[...], sh_ref[...]).astype(BF16)
    q_ref[...] = jnp.dot(h_in, wq_ref[...], preferred_element_type=F32).astype(BF16)
    v_ref[...] = jnp.dot(h_in, wv_ref[...], preferred_element_type=F32).astype(BF16)
    gg_ref[...] = jnp.dot(h_in, wg_ref[...], preferred_element_type=F32).astype(BF16)
    ga = jnp.dot(h_in, wga_ref[...], preferred_element_type=F32)

    zt = _nt_dot(wa2t_ref[...], ga, precision=HIGHEST) + ba_ref[...]
    lat = (jnp.minimum(zt, 0.0) - jnp.log1p(jnp.exp(-jnp.abs(zt)))) * (1.0 / GLA_TAU)
    row = lax.broadcasted_iota(jnp.int32, (tt, tt), 0)
    col = lax.broadcasted_iota(jnp.int32, (tt, tt), 1)
    same = (row // CHUNK) == (col // CHUNK)
    incl = jnp.where(same & (row <= col), 1.0, 0.0).astype(BF16)
    full = jnp.where(same, 1.0, 0.0).astype(BF16)
    lat_hi = lat.astype(BF16)
    lat_lo = (lat - lat_hi.astype(F32)).astype(BF16)
    cumt = (jnp.dot(lat_hi, incl, preferred_element_type=F32)
            + jnp.dot(lat_lo, incl, preferred_element_type=F32))
    tott = (jnp.dot(lat_hi, full, preferred_element_type=F32)
            + jnp.dot(lat_lo, full, preferred_element_type=F32))
    kdt = kt_ref[...].astype(F32) * jnp.exp(tott - cumt)
    dec = jnp.exp(tott)

    lane = lax.broadcasted_iota(jnp.int32, (GLA_DK, 2 * CHUNK), 1)
    upd = {}
    for c in range(nchunk):
        pair = (c // 2) * 2 * CHUNK
        if nchunk > 1:
            keep = (lane // CHUNK) == (c % 2)
        for h in range(GLA_HEADS):
            rows = slice(h * GLA_DK, (h + 1) * GLA_DK)
            vcols = slice(h * GLA_DV, (h + 1) * GLA_DV)
            if nchunk > 1:
                a = jnp.where(keep, kdt[rows, pair:pair + 2 * CHUNK], 0.0).astype(BF16)
                vp = v_ref[pair:pair + 2 * CHUNK, vcols]
            else:
                a = kdt[rows, :].astype(BF16)
                vp = v_ref[:, vcols]
            upd[c, h] = jnp.dot(a, vp, preferred_element_type=F32)

    for h in range(GLA_HEADS):
        rows = slice(h * GLA_DK, (h + 1) * GLA_DK)
        vcols = slice(h * GLA_DV, (h + 1) * GLA_DV)
        st = state_ref[h]
        states = []
        for c in range(nchunk):
            st = st * dec[rows, c * CHUNK:c * CHUNK + 1] + upd[c, h]
            states.append(st.astype(BF16))
        state_ref[h] = st
        for c in range(nchunk):
            o_scr[c * CHUNK:(c + 1) * CHUNK, vcols] = jnp.dot(
                q_ref[c * CHUNK:(c + 1) * CHUNK, rows], states[c], preferred_element_type=F32)

    for h in range(GLA_HEADS):
        vcols = slice(h * GLA_DV, (h + 1) * GLA_DV)
        o = o_scr[:, vcols] * (GLA_DK ** -0.5)
        o = o * lax.rsqrt(jnp.mean(o * o, axis=-1, keepdims=True) + EPS) * on_ref[...]
        g = gg_ref[:, vcols].astype(F32)
        o_ref[:, vcols] = (o * (g * jax.nn.sigmoid(g))).astype(BF16)


def _gla(x, g1, sc, sh, wq, wv, wg, wga, gkt, wa2t, ba_col, on_g, tt):
    s, d = x.shape
    qk = GLA_HEADS * GLA_DK
    vw = GLA_HEADS * GLA_DV
    vec = pl.BlockSpec((1, d), lambda i: (0, 0))
    whole = lambda a: pl.BlockSpec(a.shape, lambda i: (0, 0))
    return pl.pallas_call(
        _gla_kernel,
        grid=(s // tt,),
        in_specs=[pl.BlockSpec((tt, d), lambda i: (i, 0)), vec, vec, vec,
                  whole(wq), whole(wv), whole(wg), whole(wga),
                  pl.BlockSpec((qk, tt), lambda i: (0, i)),
                  whole(wa2t), whole(ba_col), whole(on_g)],
        out_specs=pl.BlockSpec((tt, vw), lambda i: (i, 0)),
        out_shape=jax.ShapeDtypeStruct((s, vw), BF16),
        scratch_shapes=[pltpu.VMEM((GLA_HEADS, GLA_DK, GLA_DV), F32),
                        pltpu.VMEM((tt, vw), F32),
                        pltpu.VMEM((tt, qk), BF16), pltpu.VMEM((tt, vw), BF16),
                        pltpu.VMEM((tt, vw), BF16)],
        compiler_params=_cparams(("arbitrary",)),
        name="gla",
    )(x, g1, sc, sh, wq, wv, wg, wga, gkt, wa2t, ba_col, on_g)


def _qknorm_rope_t(xt, g_col, cos, sin):
    n, tm = xt.shape
    x3 = xt.reshape(n // DIFF_DH, DIFF_DH, tm)
    r = lax.rsqrt(jnp.mean(x3 * x3, axis=1, keepdims=True) + EPS)
    y = x3 * r * g_col[None]
    half = ROT_DIM // 2
    y1, y2, rest = y[:, :half], y[:, half:ROT_DIM], y[:, ROT_DIM:]
    o1 = y1 * cos[None] - y2 * sin[None]
    o2 = y2 * cos[None] + y1 * sin[None]
    return jnp.concatenate([o1, o2, rest], axis=1)


def _seg_norms(x3):
    return jnp.sqrt(jnp.sum(x3 * x3, axis=1)).reshape(DIFF_HEADS, 2, x3.shape[2])


def _qkvprep_kernel(x_ref, g_ref, sc_ref, sh_ref, wq_ref, wk_ref, wv_ref, wgk_ref, pos_ref,
                    invf_ref, qg_ref, kg_ref, qa_ref, qb_ref, ko_ref, ve_ref, qn_ref, kn_ref,
                    gk_ref):
    tm = x_ref.shape[0]
    h = _rms_mod(x_ref[...], g_ref[...], sc_ref[...], sh_ref[...]).astype(BF16)
    qt = _nt_dot(wq_ref[...], h)
    kt = _nt_dot(wk_ref[...], h)
    vt = _nt_dot(wv_ref[...], h)
    gk_ref[...] = _nt_dot(wgk_ref[...], h).astype(BF16)
    ang = pos_ref[...].astype(F32) * invf_ref[...]
    cos, sin = jnp.cos(ang), jnp.sin(ang)
    q3 = _qknorm_rope_t(qt, qg_ref[...], cos, sin) * (DIFF_DH ** -0.5 * LOG2E)
    qn_ref[...] = _seg_norms(q3)
    seg = lax.broadcasted_iota(jnp.int32, q3.shape, 0)
    qa_ref[...] = jnp.where(seg % 2 == 0, q3, 0.0).reshape(-1, tm).astype(BF16)
    qb_ref[...] = jnp.where(seg % 2 == 1, q3, 0.0).reshape(-1, tm).astype(BF16)
    k3 = _qknorm_rope_t(kt, kg_ref[...], cos, sin)
    ko_ref[...] = k3.reshape(-1, tm).T.astype(BF16)
    kn_ref[...] = _seg_norms(k3)
    v3 = vt.astype(BF16).reshape(DIFF_HEADS, DIFF_DV, tm)
    ones = jnp.ones((DIFF_HEADS, ATT_SUM_ROWS, tm), BF16)
    ve_ref[...] = jnp.concatenate([v3, ones], axis=1).reshape(-1, tm)


def _qkvprep(x, g, sc, sh, wq_t, wk_t, wv_t, wgk_t, pos_row, invf_col, qg_col, kg_col, tm):
    s, d = x.shape
    n = DIFF_HEADS * 2 * DIFF_DH
    ne = DIFF_HEADS * (DIFF_DV + ATT_SUM_ROWS)
    ngk = wgk_t.shape[0]
    col = pl.BlockSpec((DIFF_DH, 1), lambda i: (0, 0))
    vec = pl.BlockSpec((1, d), lambda i: (0, 0))
    wspec = pl.BlockSpec((n, d), lambda i: (0, 0))
    return pl.pallas_call(
        _qkvprep_kernel,
        grid=(s // tm,),
        in_specs=[pl.BlockSpec((tm, d), lambda i: (i, 0)), vec, vec, vec,
                  wspec, wspec, wspec, pl.BlockSpec((ngk, d), lambda i: (0, 0)),
                  pl.BlockSpec((1, tm), lambda i: (0, i)),
                  pl.BlockSpec((ROT_DIM // 2, 1), lambda i: (0, 0)), col, col],
        out_specs=[pl.BlockSpec((n, tm), lambda i: (0, i)),
                   pl.BlockSpec((n, tm), lambda i: (0, i)),
                   pl.BlockSpec((tm, n), lambda i: (i, 0)),
                   pl.BlockSpec((ne, tm), lambda i: (0, i)),
                   pl.BlockSpec((DIFF_HEADS, 2, tm), lambda i: (0, 0, i)),
                   pl.BlockSpec((DIFF_HEADS, 2, tm), lambda i: (0, 0, i)),
                   pl.BlockSpec((ngk, tm), lambda i: (0, i))],
        out_shape=[jax.ShapeDtypeStruct((n, s), BF16), jax.ShapeDtypeStruct((n, s), BF16),
                   jax.ShapeDtypeStruct((s, n), BF16), jax.ShapeDtypeStruct((ne, s), BF16),
                   jax.ShapeDtypeStruct((DIFF_HEADS, 2, s), F32),
                   jax.ShapeDtypeStruct((DIFF_HEADS, 2, s), F32),
                   jax.ShapeDtypeStruct((ngk, s), BF16)],
        compiler_params=_cparams(("arbitrary",)),
        name="qkvprep",
    )(x, g, sc, sh, wq_t, wk_t, wv_t, wgk_t, pos_row, invf_col, qg_col, kg_col)


ATT_COLS = 256
ATT_LOOKAHEAD = {True: 2, False: 4}
ATT_BODY_TILES = {True: 8, False: 2}
ATT_SUM_ROWS = 16
ATT_BOUND_SLACK = 1.02
ATT_BOUND_LIMIT = 50.0


def _diffattn_kernel(qa_ref, qb_ref, k_ref, vt_ref, qn_ref, kn_ref, lq1_ref, lk1_ref, lq2_ref,
                     lk2_ref, sg_ref, o_ref, *scr, lambda_init, tk, cols):
    tq = qa_ref.shape[1]
    nblk = 2 * tq // cols
    m_scr, acc_scr = (scr[b * nblk:(b + 1) * nblk] for b in range(2))
    kmax_scr = scr[2 * nblk]
    s_scr = scr[2 * nblk + 1:]
    i = pl.program_id(1)

    @pl.when(i == 0)
    def _():
        kmax_scr[...] = jnp.max(kn_ref[...], axis=1, keepdims=True)

    bound = qn_ref[...] * kmax_scr[...] * ATT_BOUND_SLACK
    bound = jnp.concatenate([bound[0:1], bound[1:2]], axis=1)
    bounded = jnp.max(bound) < ATT_BOUND_LIMIT
    for c in range(nblk):
        m_scr[c][...] = jnp.where(bounded, bound[:, c * cols:(c + 1) * cols], NEG_BIG)
        acc_scr[c][...] = jnp.zeros_like(acc_scr[c])

    def scores(j, c):
        start = pl.multiple_of(j * tk, tk)
        q_ref = qa_ref if c * cols < tq else qb_ref
        off = (c * cols) % tq
        return jnp.dot(k_ref[pl.ds(start, tk), :], q_ref[:, off:off + cols],
                       preferred_element_type=F32)

    def steps(tiles, masked, next_tile, fixed):
        look = ATT_LOOKAHEAD[fixed]
        items = [(j, c) for j in tiles for c in range(nblk)]
        pending = []
        for n, (j, c) in enumerate(items):
            s = s_scr[n][...] if n < look else pending.pop(0)
            ahead = n + look
            if ahead < len(items):
                pending.append(scores(*items[ahead]))
            elif next_tile is not None:
                s_scr[ahead - len(items)][...] = scores(next_tile, ahead - len(items))
            start = pl.multiple_of(j * tk, tk)
            keys = tk
            if masked:
                keys = (c * cols) % tq + cols
                s = s[:keys]
                krow = lax.broadcasted_iota(jnp.int32, (keys, cols), 0)
                qcol = lax.broadcasted_iota(jnp.int32, (keys, cols), 1)
                qpos = i * tq + (c * cols) % tq + qcol
                s = jnp.where((start + krow) // CHUNK <= qpos // CHUNK, s, NEG_BIG)
            vj = vt_ref[:, pl.ds(start, keys)]
            if fixed:
                p = jnp.exp2(s - m_scr[c][...]).astype(BF16)
                acc_scr[c][...] += jnp.dot(vj, p, preferred_element_type=F32)
            else:
                m_prev = m_scr[c][...]
                m_new = jnp.maximum(m_prev, jnp.max(s, axis=0, keepdims=True))
                alpha = jnp.exp2(m_prev - m_new)
                p = jnp.exp2((s - m_new).astype(BF16))
                acc_scr[c][...] = alpha * acc_scr[c][...] + jnp.dot(
                    vj, p, preferred_element_type=F32)
                m_scr[c][...] = m_new

    def attend(fixed):
        for c in range(ATT_LOOKAHEAD[fixed]):
            s_scr[c][...] = scores(0, c)
        big = ATT_BODY_TILES[fixed]
        lax.fori_loop(0, i // big, lambda t, c: (steps(
            tuple(big * t + u for u in range(big)), False, big * t + big, fixed), c)[1], 0)
        size = big // 2
        while size >= 1:
            first = (i // (2 * size)) * (2 * size)

            @pl.when(i % (2 * size) >= size)
            def _(first=first, size=size):
                steps(tuple(first + u for u in range(size)), False, first + size, fixed)

            size //= 2
        steps((i,), True, None, fixed)

    @pl.when(bounded)
    def _():
        attend(True)

    @pl.when(jnp.logical_not(bounded))
    def _():
        attend(False)

    o = jnp.concatenate([acc_scr[c][:DIFF_DV] * (1.0 / acc_scr[c][DIFF_DV:DIFF_DV + 1])
                         for c in range(nblk)], axis=1)
    lam = (jnp.exp(jnp.sum(lq1_ref[...] * lk1_ref[...]))
           - jnp.exp(jnp.sum(lq2_ref[...] * lk2_ref[...])) + lambda_init)
    o = o[:, :tq] - lam * o[:, tq:]
    o = o * lax.rsqrt(jnp.mean(o * o, axis=0, keepdims=True) + EPS) * sg_ref[...]
    o_ref[...] = (o * (1.0 - lambda_init)).T.astype(BF16)


def _diffattn(qat, qbt, kr, vte, qn, kn, lq1, lk1, lq2, lk2, sg_col, lambda_init, tq, tk):
    s = kr.shape[0]
    hd = 2 * DIFF_DH
    vec = pl.BlockSpec((1, DIFF_DH), lambda h, i: (0, 0))
    cols = min(ATT_COLS, tq)
    nblk = 2 * tq // cols
    kern = functools.partial(_diffattn_kernel, lambda_init=lambda_init, tk=tk, cols=cols)
    dve = DIFF_DV + ATT_SUM_ROWS
    assert tq == tk and nblk >= max(ATT_LOOKAHEAD.values())
    scratch = ([pltpu.VMEM((1, cols), F32)] * nblk + [pltpu.VMEM((dve, cols), F32)] * nblk
               + [pltpu.VMEM((2, 1), F32)]
               + [pltpu.VMEM((tk, cols), F32)] * max(ATT_LOOKAHEAD.values()))
    return pl.pallas_call(
        kern,
        grid=(DIFF_HEADS, s // tq),
        in_specs=[pl.BlockSpec((hd, tq), lambda h, i: (h, i)),
                  pl.BlockSpec((hd, tq), lambda h, i: (h, i)),
                  pl.BlockSpec((s, hd), lambda h, i: (0, h)),
                  pl.BlockSpec((dve, s), lambda h, i: (h, 0)),
                  pl.BlockSpec((None, 2, tq), lambda h, i: (h, 0, i)),
                  pl.BlockSpec((None, 2, s), lambda h, i: (h, 0, 0)),
                  vec, vec, vec, vec,
                  pl.BlockSpec((DIFF_DV, 1), lambda h, i: (0, 0))],
        out_specs=pl.BlockSpec((tq, DIFF_DV), lambda h, i: (i, h)),
        out_shape=jax.ShapeDtypeStruct((s, DIFF_HEADS * DIFF_DV), BF16),
        scratch_shapes=scratch,
        compiler_params=_cparams(("arbitrary", "arbitrary")),
        name="diffattn",
    )(qat, qbt, kr, vte, qn, kn, lq1, lk1, lq2, lk2, sg_col)


def _mergeout_kernel(og_ref, od_ref, x_ref, g1_ref, sca_ref, sha_ref, wmg_ref, wmd_ref, wbg_ref,
                     wbd_ref, wo_ref, gt_ref, g2_ref, sc_ref, sh_ref, x1_ref, hf_ref, hfp_ref):
    x = x_ref[...]
    h_in = _rms_mod(x, g1_ref[...], sca_ref[...], sha_ref[...]).astype(BF16)
    mg = jnp.dot(h_in, wmg_ref[...], preferred_element_type=F32)
    md = jnp.dot(h_in, wmd_ref[...], preferred_element_type=F32)
    bg = jnp.dot(og_ref[...], wbg_ref[...], preferred_element_type=F32)
    bd = jnp.dot(od_ref[...], wbd_ref[...], preferred_element_type=F32)
    merged = jax.nn.sigmoid(mg) * bg + jax.nn.sigmoid(md) * bd
    x1 = x + gt_ref[...] * jnp.dot(merged.astype(BF16), wo_ref[...],
                                   preferred_element_type=F32)
    x1_ref[...] = x1
    hf = _rms_mod(x1, g2_ref[...], sc_ref[...], sh_ref[...])
    hf_ref[...] = hf
    hfp_ref[...] = _pack_halves(hf)


def _mergeout(og, od, x, g1, sca, sha, wmg, wmd, wbg, wbd, wo, gt, g2, sc, sh, tm):
    s, d = x.shape
    vec = pl.BlockSpec((1, d), lambda i: (0, 0))
    wspec = pl.BlockSpec((d, d), lambda i: (0, 0))
    row = pl.BlockSpec((tm, d), lambda i: (i, 0))
    return pl.pallas_call(
        _mergeout_kernel,
        grid=(s // tm,),
        in_specs=[row, row, row, vec, vec, vec, wspec, wspec, wspec, wspec, wspec,
                  vec, vec, vec, vec],
        out_specs=[row, row, pl.BlockSpec((tm, d // 2), lambda i: (i, 0))],
        out_shape=[jax.ShapeDtypeStruct((s, d), F32), jax.ShapeDtypeStruct((s, d), F32),
                   jax.ShapeDtypeStruct((s, d // 2), jnp.uint32)],
        compiler_params=_cparams(("arbitrary",)),
        name="mergeout",
    )(og, od, x, g1, sca, sha, wmg, wmd, wbg, wbd, wo, gt, g2, sc, sh)


def _route_kernel(hf_ref, wrt_ref, bias_ref, idx_ref, wts_ref, rnk_ref, cnt_ref, run_scr):
    tr = hf_ref.shape[0]
    e = wrt_ref.shape[0]
    gsz = e // N_GROUPS

    @pl.when(pl.program_id(0) == 0)
    def _():
        run_scr[...] = jnp.zeros_like(run_scr)

    logits = _nt_dot(wrt_ref[...], hf_ref[...], precision=HIGHEST)
    scores = jax.nn.sigmoid(logits)
    biased = scores + bias_ref[...]
    g3 = biased.reshape(N_GROUPS, gsz, tr)
    m1 = jnp.max(g3, axis=1, keepdims=True)
    n_top = jnp.sum(jnp.where(g3 == m1, 1.0, 0.0), axis=1, keepdims=True)
    m2 = jnp.max(jnp.where(g3 < m1, g3, -jnp.inf), axis=1, keepdims=True)
    gs = (m1 + jnp.where(n_top >= 2.0, m1, m2)).reshape(N_GROUPS, tr)
    gi = lax.broadcasted_iota(jnp.int32, (N_GROUPS, tr), 0)
    beaten = jnp.zeros((N_GROUPS, tr), F32)
    for g in range(N_GROUPS):
        other = gs[g:g + 1, :]
        beaten = beaten + jnp.where((other > gs) | ((other == gs) & (g < gi)), 1.0, 0.0)
    gsel = (beaten < float(TOPK_GROUPS)).reshape(N_GROUPS, 1, tr)
    masked = jnp.where(gsel, g3, -jnp.inf).reshape(e, tr)

    ids = lax.broadcasted_iota(jnp.int32, (e, tr), 0)
    chosen = jnp.zeros((e, tr), F32)
    sel_idx, sel_score = [], []
    for _ in range(TOP_K):
        mx = jnp.max(masked, axis=0, keepdims=True)
        ix = jnp.min(jnp.where(masked == mx, ids, e), axis=0, keepdims=True)
        hit = ids == ix
        sel_idx.append(ix)
        sel_score.append(jnp.sum(jnp.where(hit, scores, 0.0), axis=0, keepdims=True))
        chosen = jnp.where(hit, 1.0, chosen)
        masked = jnp.where(hit, -jnp.inf, masked)
    idx = jnp.concatenate(sel_idx, axis=0)
    sc = jnp.concatenate(sel_score, axis=0)
    idx_ref[...] = idx
    wts_ref[...] = sc / jnp.sum(sc, axis=0, keepdims=True) * ROUTED_SCALE

    row = lax.broadcasted_iota(jnp.int32, (tr, tr), 0)
    col = lax.broadcasted_iota(jnp.int32, (tr, tr), 1)
    before = jnp.where(row < col, 1.0, 0.0).astype(BF16)
    prior = jnp.dot(chosen.astype(BF16), before, preferred_element_type=F32) + run_scr[:, 0:1]
    rnk_ref[...] = jnp.concatenate(
        [jnp.sum(jnp.where(ids == sel_idx[k], prior, 0.0), axis=0, keepdims=True)
         for k in range(TOP_K)], axis=0).astype(jnp.int32)
    run_scr[...] = run_scr[...] + jnp.sum(chosen, axis=1, keepdims=True)
    cnt_ref[...] = run_scr[...].astype(jnp.int32)


def _route(hf, wrt, bias_col, tr):
    s, d = hf.shape
    e = wrt.shape[0]
    tok = pl.BlockSpec((TOP_K, tr), lambda i: (0, i))
    return pl.pallas_call(
        _route_kernel,
        grid=(s // tr,),
        in_specs=[pl.BlockSpec((tr, d), lambda i: (i, 0)),
                  pl.BlockSpec((e, d), lambda i: (0, 0)),
                  pl.BlockSpec((e, 1), lambda i: (0, 0))],
        out_specs=[tok, tok, tok, pl.BlockSpec((e, 128), lambda i: (0, 0))],
        out_shape=[jax.ShapeDtypeStruct((TOP_K, s), jnp.int32),
                   jax.ShapeDtypeStruct((TOP_K, s), F32),
                   jax.ShapeDtypeStruct((TOP_K, s), jnp.int32),
                   jax.ShapeDtypeStruct((e, 128), jnp.int32)],
        scratch_shapes=[pltpu.VMEM((e, 128), F32)],
        compiler_params=_cparams(("arbitrary",)),
        name="route",
    )(hf, wrt, bias_col)


def _positions_kernel(idx_ref, rnk_ref, pstart_ref, pos_ref):
    e = pstart_ref.shape[0]
    ts = idx_ref.shape[1]
    ids = lax.broadcasted_iota(jnp.int32, (e, ts), 0)
    idx = idx_ref[...]
    pos_ref[...] = rnk_ref[...] + jnp.concatenate(
        [jnp.sum(jnp.where(ids == idx[k:k + 1, :], pstart_ref[...], 0), axis=0, keepdims=True)
         for k in range(TOP_K)], axis=0)


def _positions(idx, rnk, pstart_col, ts):
    s = idx.shape[1]
    e = pstart_col.shape[0]
    tok = pl.BlockSpec((TOP_K, ts), lambda i: (0, i))
    return pl.pallas_call(
        _positions_kernel,
        grid=(s // ts,),
        in_specs=[tok, tok, pl.BlockSpec((e, 1), lambda i: (0, 0))],
        out_specs=tok,
        out_shape=jax.ShapeDtypeStruct((TOP_K, s), jnp.int32),
        compiler_params=_cparams(("arbitrary",)),
        name="positions",
    )(idx, rnk, pstart_col)


def _swiglu_packed(xp, wg, wu, wd):
    lo, hi = _unpack_halves(xp)
    lo, hi = lo.astype(BF16), hi.astype(BF16)
    n = lo.shape[1]
    g = (jnp.dot(lo, wg[:n], preferred_element_type=F32)
         + jnp.dot(hi, wg[n:], preferred_element_type=F32))
    u = (jnp.dot(lo, wu[:n], preferred_element_type=F32)
         + jnp.dot(hi, wu[n:], preferred_element_type=F32))
    h = (g * jax.nn.sigmoid(g)) * u
    return jnp.dot(h.astype(BF16), wd[...], preferred_element_type=F32)


def _moe_kernel(ie_ref, ib_ref, first_ref, slot_ref, ne_ref, lead_ref, rows_ref, nv_ref, xs_ref,
                wg_hbm, wu_hbm, wd_hbm, ys_ref, wg_f, wu_f, wd_f, sem):
    del ib_ref
    i = pl.program_id(0)

    def fetch(e, slot):
        copies = []
        for n, (src, dst) in enumerate(((wg_hbm, wg_f), (wu_hbm, wu_f), (wd_hbm, wd_f))):
            rows = src.shape[1] // MOE_DMA_CHUNKS
            for ch in range(MOE_DMA_CHUNKS):
                part = pl.ds(ch * rows, rows)
                copies.append(pltpu.make_async_copy(
                    src.at[e, part], dst.at[slot, part], sem.at[slot, n * MOE_DMA_CHUNKS + ch]))
        return copies

    @pl.when(i == 0)
    def _():
        for s in range(MOE_SLOTS - 1):
            @pl.when(lead_ref[s] >= 0)
            def _(s=s):
                for cp in fetch(lead_ref[s], s):
                    cp.start()

    @pl.when(i < nv_ref[0])
    def _():
        for slot in range(MOE_SLOTS):
            @pl.when((first_ref[i] == 1) & (slot_ref[i] == slot))
            def _(slot=slot):
                for cp in fetch(ie_ref[i], slot):
                    cp.wait()

                @pl.when(ne_ref[i] >= 0)
                def _():
                    for cp in fetch(ne_ref[i], (slot + MOE_SLOTS - 1) % MOE_SLOTS):
                        cp.start()

        slot = slot_ref[i]
        row = lax.broadcasted_iota(jnp.int32, xs_ref.shape, 0)
        xp = jnp.where(row < rows_ref[i], xs_ref[...], jnp.uint32(0))
        ys_ref[...] = _pack_halves(_swiglu_packed(
            xp, wg_f[slot].astype(BF16), wu_f[slot].astype(BF16), wd_f[slot].astype(BF16)))


def _moe(item_e, item_b, item_first, item_slot, item_next, lead, item_rows, n_valid, xs, wg, wu,
         wd):
    m_pad, dh = xs.shape
    _, d, f = wg.shape
    n_items = item_e.shape[0]
    blk = lambda i, ie, ib, fi, sl, ne, ld, nr, nv: (ib[i], 0)
    hbm = pl.BlockSpec(memory_space=pl.ANY)
    return pl.pallas_call(
        _moe_kernel,
        grid_spec=pltpu.PrefetchScalarGridSpec(
            num_scalar_prefetch=8,
            grid=(n_items,),
            in_specs=[pl.BlockSpec((MOE_ROWS, dh), blk), hbm, hbm, hbm],
            out_specs=pl.BlockSpec((MOE_ROWS, dh), blk),
            scratch_shapes=[pltpu.VMEM((MOE_SLOTS, d, f), F32), pltpu.VMEM((MOE_SLOTS, d, f), F32),
                            pltpu.VMEM((MOE_SLOTS, f, d), F32),
                            pltpu.SemaphoreType.DMA((MOE_SLOTS, 3 * MOE_DMA_CHUNKS))],
        ),
        out_shape=jax.ShapeDtypeStruct((m_pad, dh), jnp.uint32),
        compiler_params=_cparams(("arbitrary",)),
        name="moe",
    )(item_e, item_b, item_first, item_slot, item_next, lead, item_rows, n_valid, xs, wg, wu, wd)


def _sc_gather_rows(table, idx_row):
    m = idx_row.shape[1]
    w = table.shape[1]
    idx_row = idx_row.reshape(m // SC_GATHER_WINDOW, SC_GATHER_WINDOW)
    mesh = plsc.VectorSubcoreMesh(core_axis_name="c", subcore_axis_name="s")

    @functools.partial(pl.kernel, mesh=mesh,
                       out_type=jax.ShapeDtypeStruct((m, w), table.dtype))
    def gather(table_hbm, idx_hbm, out_hbm):
        def body(idx_vmem, out_vmem):
            pltpu.sync_copy(table_hbm.at[idx_vmem.at[0]], out_vmem)

        pltpu.emit_pipeline(
            body,
            grid=(m // SC_GATHER_WINDOW,),
            in_specs=[pl.BlockSpec((1, SC_GATHER_WINDOW), lambda i: (i, 0))],
            out_specs=[pl.BlockSpec((SC_GATHER_WINDOW, w), lambda i: (i, 0))],
            core_axis_name=("c", "s"),
            dimension_semantics=(pltpu.PARALLEL,),
        )(idx_hbm, out_hbm)

    return gather(table, idx_row)


def _sc_scatter_rows(rows, idx_blocks, m_out):
    s, w = rows.shape
    mesh = plsc.VectorSubcoreMesh(core_axis_name="c", subcore_axis_name="s")

    @functools.partial(pl.kernel, mesh=mesh,
                       out_type=jax.ShapeDtypeStruct((m_out, w), rows.dtype))
    def scatter(rows_hbm, idx_hbm, out_hbm):
        def body(rows_vmem, idx_vmem):
            for k in range(TOP_K):
                pltpu.sync_copy(rows_vmem, out_hbm.at[idx_vmem.at[k]])

        pltpu.emit_pipeline(
            body,
            grid=(s // SC_GATHER_WINDOW,),
            in_specs=[pl.BlockSpec((SC_GATHER_WINDOW, w), lambda i: (i, 0)),
                      pl.BlockSpec((TOP_K, SC_GATHER_WINDOW), lambda i: (i, 0))],
            out_specs=[],
            core_axis_name=("c", "s"),
            dimension_semantics=(pltpu.PARALLEL,),
        )(rows_hbm, idx_hbm)

    return scatter(rows, idx_blocks)


def _combine_kernel(wt_ref, hf_ref, x1_ref, gt_ref, sg_ref, su_ref, sd_ref, g_ref, o_ref):
    tc = x1_ref.shape[0]
    y = _swiglu_packed(hf_ref[...], sg_ref, su_ref, sd_ref)
    wt = wt_ref[...]
    n = g_ref.shape[2]
    r_lo = jnp.zeros((tc, n), F32)
    r_hi = jnp.zeros((tc, n), F32)
    for k in range(TOP_K):
        lo, hi = _unpack_halves(g_ref[k])
        r_lo = r_lo + lo * wt[:, k:k + 1]
        r_hi = r_hi + hi * wt[:, k:k + 1]
    y = y + jnp.concatenate([r_lo, r_hi], axis=1)
    o_ref[...] = x1_ref[...] + gt_ref[...] * y


def _combine(wts_t, hfp, x1, gt, sg, su, sd, gathered, tc):
    s, d = x1.shape
    f = sg.shape[1]
    row = pl.BlockSpec((tc, d), lambda i: (i, 0))
    return pl.pallas_call(
        _combine_kernel,
        grid=(s // tc,),
        in_specs=[pl.BlockSpec((tc, TOP_K), lambda i: (i, 0)),
                  pl.BlockSpec((tc, d // 2), lambda i: (i, 0)), row,
                  pl.BlockSpec((1, d), lambda i: (0, 0)),
                  pl.BlockSpec((d, f), lambda i: (0, 0)),
                  pl.BlockSpec((d, f), lambda i: (0, 0)),
                  pl.BlockSpec((f, d), lambda i: (0, 0)),
                  pl.BlockSpec((TOP_K, tc, d // 2), lambda i: (0, i, 0))],
        out_specs=row,
        out_shape=jax.ShapeDtypeStruct((s, d), F32),
        compiler_params=_cparams(("arbitrary",)),
        name="combine",
    )(wts_t, hfp, x1, gt, sg, su, sd, gathered)


def _tile(n, want):
    t = min(n, want)
    assert n % t == 0, (n, t)
    return t


def _layer(l, x, c_col, pos_row, p):
    s, d = x.shape
    lambda_init = 0.8 - 0.6 * math.exp(-0.3 * l)
    gqk, gv = GLA_HEADS * GLA_DK, GLA_HEADS * GLA_DV
    dqk, dvw = DIFF_HEADS * 2 * DIFF_DH, DIFF_HEADS * DIFF_DV
    lowrank = p["gla_w_a2"].shape[0]

    mod = _ada(c_col, p["w_ada"], p["b_ada"][None, :])
    sh_a, sc_a, gt_a, sh_f, sc_f, gt_f = [mod[:, j * d:(j + 1) * d] for j in range(6)]

    w_in = p["w_in"]
    o = 0
    cols = {}
    for name, wdt in (("gq", gqk), ("gk", gqk), ("gv", gv), ("ga", lowrank), ("gg", gv),
                      ("dq", dqk), ("dk", dqk), ("dv", dvw), ("mg", d), ("md", d)):
        cols[name] = w_in[:, o:o + wdt]
        o += wdt
    w_ga = jnp.pad(cols["ga"], ((0, 0), (0, 128 - lowrank))).astype(BF16)

    g1 = p["norm1_g"][None, :]
    ts = _tile(s, TILE_SEQ)

    invf = ROPE_THETA ** (-jnp.arange(0, ROT_DIM, 2, dtype=F32) / ROT_DIM)
    qat, qbt, kr, vte, qn, kn, gkt = _qkvprep(
        x, g1, sc_a, sh_a, *(cols[n].T.astype(BF16) for n in ("dq", "dk", "dv", "gk")),
        pos_row, invf[:, None], p["diff_qnorm_g"][:, None], p["diff_knorm_g"][:, None], ts)

    wa2t = jnp.pad(p["gla_w_a2"].T, ((0, 0), (0, 128 - lowrank)))
    o_gla = _gla(x, g1, sc_a, sh_a, cols["gq"].astype(BF16), cols["gv"].astype(BF16),
                 cols["gg"].astype(BF16), w_ga, gkt, wa2t, p["gla_b_a"][:, None],
                 p["gla_onorm_g"][None, :], ts)
    tq = ts
    o_diff = _diffattn(qat, qbt, kr, vte, qn, kn, p["diff_lq1"][None, :], p["diff_lk1"][None, :],
                       p["diff_lq2"][None, :], p["diff_lk2"][None, :],
                       p["diff_subln_g"][:, None], lambda_init, tq, tq)

    x1, hf, hfp = _mergeout(o_gla, o_diff, x, g1, sc_a, sh_a, cols["mg"].astype(BF16),
                            cols["md"].astype(BF16), p["w_branch_gla"].astype(BF16),
                            p["w_branch_diff"].astype(BF16), p["w_out"].astype(BF16), gt_a,
                            p["norm2_g"][None, :], sc_f, sh_f, ts)

    e = p["w_router"].shape[1]
    idx, wts, rnk, cnt = _route(hf, p["w_router"].T, p["router_bias"][:, None], ts)

    counts = cnt[:, 0]
    pcounts = ((counts + MOE_ROWS - 1) // MOE_ROWS) * MOE_ROWS
    pend = jnp.cumsum(pcounts)
    pstart = pend - pcounts
    pos = _positions(idx, rnk, pstart[:, None], ts)
    n_items = (s * TOP_K) // MOE_ROWS + e
    n_valid = (pend[-1] // MOE_ROWS).astype(jnp.int32)
    item_b = jnp.minimum(jnp.arange(n_items, dtype=jnp.int32), n_valid - 1)
    item_e = jnp.minimum(jnp.sum(pend[None, :] <= (item_b * MOE_ROWS)[:, None], axis=1),
                         e - 1).astype(jnp.int32)

    wn = SC_GATHER_WINDOW
    pos_w = pos.reshape(TOP_K, s // wn, wn).transpose(1, 0, 2).reshape(s // wn * TOP_K, wn)
    xs = _sc_scatter_rows(hfp, pos_w, n_items * MOE_ROWS)
    item_rows = jnp.clip(pstart[item_e] + counts[item_e] - item_b * MOE_ROWS, 0,
                         MOE_ROWS).astype(jnp.int32)
    prev_e = jnp.concatenate([jnp.full((1,), -1, jnp.int32), item_e[:-1]])
    item_first = ((jnp.arange(n_items) < n_valid) & (item_e != prev_e)).astype(jnp.int32)
    item_slot = ((jnp.cumsum(item_first) - 1) % MOE_SLOTS).astype(jnp.int32)
    cand = jnp.where(pcounts > 0, jnp.arange(e, dtype=jnp.int32), e)
    nonempty_from = lax.cummin(cand[::-1])[::-1]
    following = jnp.concatenate([nonempty_from[1:], jnp.full((2,), e, jnp.int32)])
    ahead = item_e
    lead = [nonempty_from[0]]
    for _ in range(MOE_SLOTS - 1):
        ahead = following[ahead]
        lead.append(following[lead[-1]])
    item_next = jnp.where(ahead < e, ahead, -1).astype(jnp.int32)
    lead = jnp.stack(lead[:MOE_SLOTS - 1])
    lead = jnp.where(lead < e, lead, -1).astype(jnp.int32)
    ys = _moe(item_e, item_b, item_first, item_slot, item_next, lead, item_rows, n_valid[None],
              xs, p["w_exp_gate"], p["w_exp_up"], p["w_exp_down"])
    gathered = _sc_gather_rows(ys, pos.reshape(1, TOP_K * s)).reshape(TOP_K, s, d // 2)
    return _combine(wts.T, hfp, x1, gt_f, p["w_sh_gate"].astype(BF16),
                    p["w_sh_up"].astype(BF16), p["w_sh_down"].astype(BF16), gathered,
                    _tile(s, TILE_COMBINE))


_LAYER_PARAMS = ("w_ada", "b_ada", "norm1_g", "w_in", "gla_w_a2", "gla_b_a", "gla_onorm_g",
                 "diff_qnorm_g", "diff_knorm_g", "diff_lq1", "diff_lk1", "diff_lq2", "diff_lk2",
                 "diff_subln_g", "w_branch_gla", "w_branch_diff", "w_out", "norm2_g", "w_router",
                 "router_bias", "w_exp_gate", "w_exp_up", "w_exp_down", "w_sh_gate", "w_sh_up",
                 "w_sh_down")


def kernel(x, c, positions, w_ada, b_ada, norm1_g, w_in, gla_w_a2, gla_b_a, gla_onorm_g, diff_qnorm_g, diff_knorm_g, diff_lq1, diff_lk1, diff_lq2, diff_lk2, diff_subln_g, w_branch_gla, w_branch_diff, w_out, norm2_g, w_router, router_bias, w_exp_gate, w_exp_up, w_exp_down, w_sh_gate, w_sh_up, w_sh_down):
    stacked = dict(zip(_LAYER_PARAMS, (
        w_ada, b_ada, norm1_g, w_in, gla_w_a2, gla_b_a, gla_onorm_g, diff_qnorm_g, diff_knorm_g,
        diff_lq1, diff_lk1, diff_lq2, diff_lk2, diff_subln_g, w_branch_gla, w_branch_diff, w_out,
        norm2_g, w_router, router_bias, w_exp_gate, w_exp_up, w_exp_down, w_sh_gate, w_sh_up,
        w_sh_down)))
    b, s, d = x.shape
    assert b == 1, "single-sequence kernel"
    xl = x[0]
    c_col = c[0][:, None]
    pos_row = positions.astype(jnp.int32)
    for l in range(w_ada.shape[0]):
        xl = _layer(l, xl, c_col, pos_row, {k: v[l] for k, v in stacked.items()})
    return xl[None]
```

```python
import functools
import math

import jax
import jax.numpy as jnp
from jax import lax
from jax.experimental import pallas as pl
from jax.experimental.pallas import tpu as pltpu
from jax.experimental.pallas import tpu_sc as plsc

CHUNK = 64
EPS = 1e-6
GLA_HEADS = 4
GLA_DK = 128
GLA_DV = 256
GLA_TAU = 16.0
DIFF_HEADS = 8
DIFF_DH = 64
DIFF_DV = 2 * DIFF_DH
ROPE_THETA = 500000.0
ROT_DIM = DIFF_DH // 4
N_GROUPS = 8
TOPK_GROUPS = 4
TOP_K = 8
ROUTED_SCALE = 2.5

MOE_ROWS = 640
MOE_DMA_CHUNKS = 4
MOE_SLOTS = 3
SC_GATHER_WINDOW = 64
VMEM_LIMIT = 56 * 1024 * 1024
TILE_SEQ = 512
TILE_COMBINE = 256
NEG_BIG = -1e30
LOG2E = 1.4426950408889634
HIGHEST = lax.Precision.HIGHEST
F32 = jnp.float32
BF16 = jnp.bfloat16


def _cparams(sem):
    return pltpu.CompilerParams(dimension_semantics=sem, vmem_limit_bytes=VMEM_LIMIT)


def _nt_dot(a, b, precision=None):
    return lax.dot_general(a, b, (((1,), (1,)), ((), ())), precision=precision,
                           preferred_element_type=F32)


def _pack_halves(x):
    n = x.shape[1] // 2
    lo = pltpu.bitcast(x[:, :n].astype(BF16).astype(F32), jnp.uint32) >> 16
    hi = pltpu.bitcast(x[:, n:].astype(BF16).astype(F32), jnp.uint32) & jnp.uint32(0xFFFF0000)
    return lo | hi


def _unpack_halves(w):
    return (pltpu.bitcast(w << 16, F32), pltpu.bitcast(w & jnp.uint32(0xFFFF0000), F32))


def _rms_mod(x, g, sc, sh):
    xn = x * lax.rsqrt(jnp.mean(x * x, axis=-1, keepdims=True) + EPS)
    return (xn * g) * (1.0 + sc) + sh


def _ada_kernel(c_ref, w_ref, b_ref, o_ref):
    c = c_ref[...]
    ca = c * jax.nn.sigmoid(c)
    o_ref[...] = jnp.sum(ca * w_ref[...], axis=0, keepdims=True) + b_ref[...]


def _ada(c_col, w, b):
    d, n = w.shape
    tn = min(1024, n)
    return pl.pallas_call(
        _ada_kernel,
        grid=(n // tn,),
        in_specs=[pl.BlockSpec((d, 1), lambda j: (0, 0)),
                  pl.BlockSpec((d, tn), lambda j: (0, j)),
                  pl.BlockSpec((1, tn), lambda j: (0, j))],
        out_specs=pl.BlockSpec((1, tn), lambda j: (0, j)),
        out_shape=jax.ShapeDtypeStruct((1, n), F32),
        compiler_params=_cparams(("arbitrary",)),
        name="ada",
    )(c_col, w, b)


def _gla_kernel(x_ref, g1_ref, sc_ref, sh_ref, wq_ref, wv_ref, wg_ref, wga_ref, kt_ref, wa2t_ref,
                ba_ref, on_ref, o_ref, state_ref, o_scr, q_ref, v_ref, gg_ref):
    tt = x_ref.shape[0]
    nchunk = tt // CHUNK

    @pl.when(pl.program_id(0) == 0)
    def _():
        state_ref[...] = jnp.zeros_like(state_ref)

    h_in = _rms_mod(x_ref[...], g1_ref[...], sc_ref[...], sh_ref[...]).astype(BF16)
    q_ref[...] = jnp.dot(h_in, wq_ref[...], preferred_element_type=F32).astype(BF16)
    v_ref[...] = jnp.dot(h_in, wv_ref[...], preferred_element_type=F32).astype(BF16)
    gg_ref[...] = jnp.dot(h_in, wg_ref[...], preferred_element_type=F32).astype(BF16)
    ga = jnp.dot(h_in, wga_ref[...], preferred_element_type=F32)

    zt = _nt_dot(wa2t_ref[...], ga, precision=HIGHEST) + ba_ref[...]
    lat = (jnp.minimum(zt, 0.0) - jnp.log1p(jnp.exp(-jnp.abs(zt)))) * (1.0 / GLA_TAU)
    row = lax.broadcasted_iota(jnp.int32, (tt, tt), 0)
    col = lax.broadcasted_iota(jnp.int32, (tt, tt), 1)
    same = (row // CHUNK) == (col // CHUNK)
    incl = jnp.where(same & (row <= col), 1.0, 0.0).astype(BF16)
    full = jnp.where(same, 1.0, 0.0).astype(BF16)
    lat_hi = lat.astype(BF16)
    lat_lo = (lat - lat_hi.astype(F32)).astype(BF16)
    cumt = (jnp.dot(lat_hi, incl, preferred_element_type=F32)
            + jnp.dot(lat_lo, incl, preferred_element_type=F32))
    tott = (jnp.dot(lat_hi, full, preferred_element_type=F32)
            + jnp.dot(lat_lo, full, preferred_element_type=F32))
    kdt = kt_ref[...].astype(F32) * jnp.exp(tott - cumt)
    dec = jnp.exp(tott)

    lane = lax.broadcasted_iota(jnp.int32, (GLA_DK, 2 * CHUNK), 1)
    upd = {}
    for c in range(nchunk):
        pair = (c // 2) * 2 * CHUNK
        if nchunk > 1:
            keep = (lane // CHUNK) == (c % 2)
        for h in range(GLA_HEADS):
            rows = slice(h * GLA_DK, (h + 1) * GLA_DK)
            vcols = slice(h * GLA_DV, (h + 1) * GLA_DV)
            if nchunk > 1:
                a = jnp.where(keep, kdt[rows, pair:pair + 2 * CHUNK], 0.0).astype(BF16)
                vp = v_ref[pair:pair + 2 * CHUNK, vcols]
            else:
                a = kdt[rows, :].astype(BF16)
                vp = v_ref[:, vcols]
            upd[c, h] = jnp.dot(a, vp, preferred_element_type=F32)

    for h in range(GLA_HEADS):
        rows = slice(h * GLA_DK, (h + 1) * GLA_DK)
        vcols = slice(h * GLA_DV, (h + 1) * GLA_DV)
        st = state_ref[h]
        states = []
        for c in range(nchunk):
            st = st * dec[rows, c * CHUNK:c * CHUNK + 1] + upd[c, h]
            states.append(st.astype(BF16))
        state_ref[h] = st
        for c in range(nchunk):
            o_scr[c * CHUNK:(c + 1) * CHUNK, vcols] = jnp.dot(
                q_ref[c * CHUNK:(c + 1) * CHUNK, rows], states[c], preferred_element_type=F32)

    for h in range(GLA_HEADS):
        vcols = slice(h * GLA_DV, (h + 1) * GLA_DV)
        o = o_scr[:, vcols] * (GLA_DK ** -0.5)
        o = o * lax.rsqrt(jnp.mean(o * o, axis=-1, keepdims=True) + EPS) * on_ref[...]
        g = gg_ref[:, vcols].astype(F32)
        o_ref[:, vcols] = (o * (g * jax.nn.sigmoid(g))).astype(BF16)


def _gla(x, g1, sc, sh, wq, wv, wg, wga, gkt, wa2t, ba_col, on_g, tt):
    s, d = x.shape
    qk = GLA_HEADS * GLA_DK
    vw = GLA_HEADS * GLA_DV
    vec = pl.BlockSpec((1, d), lambda i: (0, 0))
    whole = lambda a: pl.BlockSpec(a.shape, lambda i: (0, 0))
    return pl.pallas_call(
        _gla_kernel,
        grid=(s // tt,),
        in_specs=[pl.BlockSpec((tt, d), lambda i: (i, 0)), vec, vec, vec,
                  whole(wq), whole(wv), whole(wg), whole(wga),
                  pl.BlockSpec((qk, tt), lambda i: (0, i)),
                  whole(wa2t), whole(ba_col), whole(on_g)],
        out_specs=pl.BlockSpec((tt, vw), lambda i: (i, 0)),
        out_shape=jax.ShapeDtypeStruct((s, vw), BF16),
        scratch_shapes=[pltpu.VMEM((GLA_HEADS, GLA_DK, GLA_DV), F32),
                        pltpu.VMEM((tt, vw), F32),
                        pltpu.VMEM((tt, qk), BF16), pltpu.VMEM((tt, vw), BF16),
                        pltpu.VMEM((tt, vw), BF16)],
        compiler_params=_cparams(("arbitrary",)),
        name="gla",
    )(x, g1, sc, sh, wq, wv, wg, wga, gkt, wa2t, ba_col, on_g)


def _qknorm_rope_t(xt, g_col, cos, sin):
    n, tm = xt.shape
    x3 = xt.reshape(n // DIFF_DH, DIFF_DH, tm)
    r = lax.rsqrt(jnp.mean(x3 * x3, axis=1, keepdims=True) + EPS)
    y = x3 * r * g_col[None]
    half = ROT_DIM // 2
    y1, y2, rest = y[:, :half], y[:, half:ROT_DIM], y[:, ROT_DIM:]
    o1 = y1 * cos[None] - y2 * sin[None]
    o2 = y2 * cos[None] + y1 * sin[None]
    return jnp.concatenate([o1, o2, rest], axis=1)


def _seg_norms(x3):
    return jnp.sqrt(jnp.sum(x3 * x3, axis=1)).reshape(DIFF_HEADS, 2, x3.shape[2])


def _qkvprep_kernel(x_ref, g_ref, sc_ref, sh_ref, wq_ref, wk_ref, wv_ref, wgk_ref, pos_ref,
                    invf_ref, qg_ref, kg_ref, qa_ref, qb_ref, ko_ref, ve_ref, qn_ref, kn_ref,
                    gk_ref):
    tm = x_ref.shape[0]
    h = _rms_mod(x_ref[...], g_ref[...], sc_ref[...], sh_ref[...]).astype(BF16)
    qt = _nt_dot(wq_ref[...], h)
    kt = _nt_dot(wk_ref[...], h)
    vt = _nt_dot(wv_ref[...], h)
    gk_ref[...] = _nt_dot(wgk_ref[...], h).astype(BF16)
    ang = pos_ref[...].astype(F32) * invf_ref[...]
    cos, sin = jnp.cos(ang), jnp.sin(ang)
    q3 = _qknorm_rope_t(qt, qg_ref[...], cos, sin) * (DIFF_DH ** -0.5 * LOG2E)
    qn_ref[...] = _seg_norms(q3)
    seg = lax.broadcasted_iota(jnp.int32, q3.shape, 0)
    qa_ref[...] = jnp.where(seg % 2 == 0, q3, 0.0).reshape(-1, tm).astype(BF16)
    qb_ref[...] = jnp.where(seg % 2 == 1, q3, 0.0).reshape(-1, tm).astype(BF16)
    k3 = _qknorm_rope_t(kt, kg_ref[...], cos, sin)
    ko_ref[...] = k3.reshape(-1, tm).T.astype(BF16)
    kn_ref[...] = _seg_norms(k3)
    v3 = vt.astype(BF16).reshape(DIFF_HEADS, DIFF_DV, tm)
    ones = jnp.ones((DIFF_HEADS, ATT_SUM_ROWS, tm), BF16)
    ve_ref[...] = jnp.concatenate([v3, ones], axis=1).reshape(-1, tm)


def _qkvprep(x, g, sc, sh, wq_t, wk_t, wv_t, wgk_t, pos_row, invf_col, qg_col, kg_col, tm):
    s, d = x.shape
    n = DIFF_HEADS * 2 * DIFF_DH
    ne = DIFF_HEADS * (DIFF_DV + ATT_SUM_ROWS)
    ngk = wgk_t.shape[0]
    col = pl.BlockSpec((DIFF_DH, 1), lambda i: (0, 0))
    vec = pl.BlockSpec((1, d), lambda i: (0, 0))
    wspec = pl.BlockSpec((n, d), lambda i: (0, 0))
    return pl.pallas_call(
        _qkvprep_kernel,
        grid=(s // tm,),
        in_specs=[pl.BlockSpec((tm, d), lambda i: (i, 0)), vec, vec, vec,
                  wspec, wspec, wspec, pl.BlockSpec((ngk, d), lambda i: (0, 0)),
                  pl.BlockSpec((1, tm), lambda i: (0, i)),
                  pl.BlockSpec((ROT_DIM // 2, 1), lambda i: (0, 0)), col, col],
        out_specs=[pl.BlockSpec((n, tm), lambda i: (0, i)),
                   pl.BlockSpec((n, tm), lambda i: (0, i)),
                   pl.BlockSpec((tm, n), lambda i: (i, 0)),
                   pl.BlockSpec((ne, tm), lambda i: (0, i)),
                   pl.BlockSpec((DIFF_HEADS, 2, tm), lambda i: (0, 0, i)),
                   pl.BlockSpec((DIFF_HEADS, 2, tm), lambda i: (0, 0, i)),
                   pl.BlockSpec((ngk, tm), lambda i: (0, i))],
        out_shape=[jax.ShapeDtypeStruct((n, s), BF16), jax.ShapeDtypeStruct((n, s), BF16),
                   jax.ShapeDtypeStruct((s, n), BF16), jax.ShapeDtypeStruct((ne, s), BF16),
                   jax.ShapeDtypeStruct((DIFF_HEADS, 2, s), F32),
                   jax.ShapeDtypeStruct((DIFF_HEADS, 2, s), F32),
                   jax.ShapeDtypeStruct((ngk, s), BF16)],
        compiler_params=_cparams(("arbitrary",)),
        name="qkvprep",
    )(x, g, sc, sh, wq_t, wk_t, wv_t, wgk_t, pos_row, invf_col, qg_col, kg_col)


ATT_COLS = 256
ATT_LOOKAHEAD = {True: 2, False: 4}
ATT_BODY_TILES = {True: 8, False: 2}
ATT_SUM_ROWS = 16
ATT_BOUND_SLACK = 1.02
ATT_BOUND_LIMIT = 50.0


def _diffattn_kernel(qa_ref, qb_ref, k_ref, vt_ref, qn_ref, kn_ref, lq1_ref, lk1_ref, lq2_ref,
                     lk2_ref, sg_ref, o_ref, *scr, lambda_init, tk, cols):
    tq = qa_ref.shape[1]
    nblk = 2 * tq // cols
    m_scr, acc_scr = (scr[b * nblk:(b + 1) * nblk] for b in range(2))
    kmax_scr = scr[2 * nblk]
    s_scr = scr[2 * nblk + 1:]
    i = pl.program_id(1)

    @pl.when(i == 0)
    def _():
        kmax_scr[...] = jnp.max(kn_ref[...], axis=1, keepdims=True)

    bound = qn_ref[...] * kmax_scr[...] * ATT_BOUND_SLACK
    bound = jnp.concatenate([bound[0:1], bound[1:2]], axis=1)
    bounded = jnp.max(bound) < ATT_BOUND_LIMIT
    for c in range(nblk):
        m_scr[c][...] = jnp.where(bounded, bound[:, c * cols:(c + 1) * cols], NEG_BIG)
        acc_scr[c][...] = jnp.zeros_like(acc_scr[c])

    def scores(j, c):
        start = pl.multiple_of(j * tk, tk)
        q_ref = qa_ref if c * cols < tq else qb_ref
        off = (c * cols) % tq
        return jnp.dot(k_ref[pl.ds(start, tk), :], q_ref[:, off:off + cols],
                       preferred_element_type=F32)

    def steps(tiles, masked, next_tile, fixed):
        look = ATT_LOOKAHEAD[fixed]
        items = [(j, c) for j in tiles for c in range(nblk)]
        pending = []
        for n, (j, c) in enumerate(items):
            s = s_scr[n][...] if n < look else pending.pop(0)
            ahead = n + look
            if ahead < len(items):
                pending.append(scores(*items[ahead]))
            elif next_tile is not None:
                s_scr[ahead - len(items)][...] = scores(next_tile, ahead - len(items))
            start = pl.multiple_of(j * tk, tk)
            keys = tk
            if masked:
                keys = (c * cols) % tq + cols
                s = s[:keys]
                krow = lax.broadcasted_iota(jnp.int32, (keys, cols), 0)
                qcol = lax.broadcasted_iota(jnp.int32, (keys, cols), 1)
                qpos = i * tq + (c * cols) % tq + qcol
                s = jnp.where((start + krow) // CHUNK <= qpos // CHUNK, s, NEG_BIG)
            vj = vt_ref[:, pl.ds(start, keys)]
            if fixed:
                p = jnp.exp2(s - m_scr[c][...])
                acc_scr[c][:DIFF_DV] += jnp.dot(vj[:DIFF_DV], p.astype(BF16),
                                                preferred_element_type=F32)
                acc_scr[c][DIFF_DV:DIFF_DV + 1] += jnp.sum(p, axis=0, keepdims=True)
            else:
                m_prev = m_scr[c][...]
                m_new = jnp.maximum(m_prev, jnp.max(s, axis=0, keepdims=True))
                alpha = jnp.exp2(m_prev - m_new)
                p = jnp.exp2((s - m_new).astype(BF16))
                acc_scr[c][...] = alpha * acc_scr[c][...] + jnp.dot(
                    vj, p, preferred_element_type=F32)
                m_scr[c][...] = m_new

    def attend(fixed):
        for c in range(ATT_LOOKAHEAD[fixed]):
            s_scr[c][...] = scores(0, c)
        big = ATT_BODY_TILES[fixed]
        lax.fori_loop(0, i // big, lambda t, c: (steps(
            tuple(big * t + u for u in range(big)), False, big * t + big, fixed), c)[1], 0)
        size = big // 2
        while size >= 1:
            first = (i // (2 * size)) * (2 * size)

            @pl.when(i % (2 * size) >= size)
            def _(first=first, size=size):
                steps(tuple(first + u for u in range(size)), False, first + size, fixed)

            size //= 2
        steps((i,), True, None, fixed)

    @pl.when(bounded)
    def _():
        attend(True)

    @pl.when(jnp.logical_not(bounded))
    def _():
        attend(False)

    o = jnp.concatenate([acc_scr[c][:DIFF_DV] * (1.0 / acc_scr[c][DIFF_DV:DIFF_DV + 1])
                         for c in range(nblk)], axis=1)
    lam = (jnp.exp(jnp.sum(lq1_ref[...] * lk1_ref[...]))
           - jnp.exp(jnp.sum(lq2_ref[...] * lk2_ref[...])) + lambda_init)
    o = o[:, :tq] - lam * o[:, tq:]
    o = o * lax.rsqrt(jnp.mean(o * o, axis=0, keepdims=True) + EPS) * sg_ref[...]
    o_ref[...] = (o * (1.0 - lambda_init)).T.astype(BF16)


def _diffattn(qat, qbt, kr, vte, qn, kn, lq1, lk1, lq2, lk2, sg_col, lambda_init, tq, tk):
    s = kr.shape[0]
    hd = 2 * DIFF_DH
    vec = pl.BlockSpec((1, DIFF_DH), lambda h, i: (0, 0))
    cols = min(ATT_COLS, tq)
    nblk = 2 * tq // cols
    kern = functools.partial(_diffattn_kernel, lambda_init=lambda_init, tk=tk, cols=cols)
    dve = DIFF_DV + ATT_SUM_ROWS
    assert tq == tk and nblk >= max(ATT_LOOKAHEAD.values())
    scratch = ([pltpu.VMEM((1, cols), F32)] * nblk + [pltpu.VMEM((dve, cols), F32)] * nblk
               + [pltpu.VMEM((2, 1), F32)]
               + [pltpu.VMEM((tk, cols), F32)] * max(ATT_LOOKAHEAD.values()))
    return pl.pallas_call(
        kern,
        grid=(DIFF_HEADS, s // tq),
        in_specs=[pl.BlockSpec((hd, tq), lambda h, i: (h, i)),
                  pl.BlockSpec((hd, tq), lambda h, i: (h, i)),
                  pl.BlockSpec((s, hd), lambda h, i: (0, h)),
                  pl.BlockSpec((dve, s), lambda h, i: (h, 0)),
                  pl.BlockSpec((None, 2, tq), lambda h, i: (h, 0, i)),
                  pl.BlockSpec((None, 2, s), lambda h, i: (h, 0, 0)),
                  vec, vec, vec, vec,
                  pl.BlockSpec((DIFF_DV, 1), lambda h, i: (0, 0))],
        out_specs=pl.BlockSpec((tq, DIFF_DV), lambda h, i: (i, h)),
        out_shape=jax.ShapeDtypeStruct((s, DIFF_HEADS * DIFF_DV), BF16),
        scratch_shapes=scratch,
        compiler_params=_cparams(("arbitrary", "arbitrary")),
        name="diffattn",
    )(qat, qbt, kr, vte, qn, kn, lq1, lk1, lq2, lk2, sg_col)


def _mergeout_kernel(og_ref, od_ref, x_ref, g1_ref, sca_ref, sha_ref, wmg_ref, wmd_ref, wbg_ref,
                     wbd_ref, wo_ref, gt_ref, g2_ref, sc_ref, sh_ref, x1_ref, hf_ref, hfp_ref):
    x = x_ref[...]
    h_in = _rms_mod(x, g1_ref[...], sca_ref[...], sha_ref[...]).astype(BF16)
    mg = jnp.dot(h_in, wmg_ref[...], preferred_element_type=F32)
    md = jnp.dot(h_in, wmd_ref[...], preferred_element_type=F32)
    bg = jnp.dot(og_ref[...], wbg_ref[...], preferred_element_type=F32)
    bd = jnp.dot(od_ref[...], wbd_ref[...], preferred_element_type=F32)
    merged = jax.nn.sigmoid(mg) * bg + jax.nn.sigmoid(md) * bd
    x1 = x + gt_ref[...] * jnp.dot(merged.astype(BF16), wo_ref[...],
                                   preferred_element_type=F32)
    x1_ref[...] = x1
    hf = _rms_mod(x1, g2_ref[...], sc_ref[...], sh_ref[...])
    hf_ref[...] = hf
    hfp_ref[...] = _pack_halves(hf)


def _mergeout(og, od, x, g1, sca, sha, wmg, wmd, wbg, wbd, wo, gt, g2, sc, sh, tm):
    s, d = x.shape
    vec = pl.BlockSpec((1, d), lambda i: (0, 0))
    wspec = pl.BlockSpec((d, d), lambda i: (0, 0))
    row = pl.BlockSpec((tm, d), lambda i: (i, 0))
    return pl.pallas_call(
        _mergeout_kernel,
        grid=(s // tm,),
        in_specs=[row, row, row, vec, vec, vec, wspec, wspec, wspec, wspec, wspec,
                  vec, vec, vec, vec],
        out_specs=[row, row, pl.BlockSpec((tm, d // 2), lambda i: (i, 0))],
        out_shape=[jax.ShapeDtypeStruct((s, d), F32), jax.ShapeDtypeStruct((s, d), F32),
                   jax.ShapeDtypeStruct((s, d // 2), jnp.uint32)],
        compiler_params=_cparams(("arbitrary",)),
        name="mergeout",
    )(og, od, x, g1, sca, sha, wmg, wmd, wbg, wbd, wo, gt, g2, sc, sh)


def _route_kernel(hf_ref, wrt_ref, bias_ref, idx_ref, wts_ref, rnk_ref, cnt_ref, run_scr):
    tr = hf_ref.shape[0]
    e = wrt_ref.shape[0]
    gsz = e // N_GROUPS

    @pl.when(pl.program_id(0) == 0)
    def _():
        run_scr[...] = jnp.zeros_like(run_scr)

    logits = _nt_dot(wrt_ref[...], hf_ref[...], precision=HIGHEST)
    scores = jax.nn.sigmoid(logits)
    biased = scores + bias_ref[...]
    g3 = biased.reshape(N_GROUPS, gsz, tr)
    m1 = jnp.max(g3, axis=1, keepdims=True)
    n_top = jnp.sum(jnp.where(g3 == m1, 1.0, 0.0), axis=1, keepdims=True)
    m2 = jnp.max(jnp.where(g3 < m1, g3, -jnp.inf), axis=1, keepdims=True)
    gs = (m1 + jnp.where(n_top >= 2.0, m1, m2)).reshape(N_GROUPS, tr)
    gi = lax.broadcasted_iota(jnp.int32, (N_GROUPS, tr), 0)
    beaten = jnp.zeros((N_GROUPS, tr), F32)
    for g in range(N_GROUPS):
        other = gs[g:g + 1, :]
        beaten = beaten + jnp.where((other > gs) | ((other == gs) & (g < gi)), 1.0, 0.0)
    gsel = (beaten < float(TOPK_GROUPS)).reshape(N_GROUPS, 1, tr)
    masked = jnp.where(gsel, g3, -jnp.inf).reshape(e, tr)

    ids = lax.broadcasted_iota(jnp.int32, (e, tr), 0)
    chosen = jnp.zeros((e, tr), F32)
    sel_idx, sel_score = [], []
    for _ in range(TOP_K):
        mx = jnp.max(masked, axis=0, keepdims=True)
        ix = jnp.min(jnp.where(masked == mx, ids, e), axis=0, keepdims=True)
        hit = ids == ix
        sel_idx.append(ix)
        sel_score.append(jnp.sum(jnp.where(hit, scores, 0.0), axis=0, keepdims=True))
        chosen = jnp.where(hit, 1.0, chosen)
        masked = jnp.where(hit, -jnp.inf, masked)
    idx = jnp.concatenate(sel_idx, axis=0)
    sc = jnp.concatenate(sel_score, axis=0)
    idx_ref[...] = idx
    wts_ref[...] = sc / jnp.sum(sc, axis=0, keepdims=True) * ROUTED_SCALE

    row = lax.broadcasted_iota(jnp.int32, (tr, tr), 0)
    col = lax.broadcasted_iota(jnp.int32, (tr, tr), 1)
    before = jnp.where(row < col, 1.0, 0.0).astype(BF16)
    prior = jnp.dot(chosen.astype(BF16), before, preferred_element_type=F32) + run_scr[:, 0:1]
    rnk_ref[...] = jnp.concatenate(
        [jnp.sum(jnp.where(ids == sel_idx[k], prior, 0.0), axis=0, keepdims=True)
         for k in range(TOP_K)], axis=0).astype(jnp.int32)
    run_scr[...] = run_scr[...] + jnp.sum(chosen, axis=1, keepdims=True)
    cnt_ref[...] = run_scr[...].astype(jnp.int32)


def _route(hf, wrt, bias_col, tr):
    s, d = hf.shape
    e = wrt.shape[0]
    tok = pl.BlockSpec((TOP_K, tr), lambda i: (0, i))
    return pl.pallas_call(
        _route_kernel,
        grid=(s // tr,),
        in_specs=[pl.BlockSpec((tr, d), lambda i: (i, 0)),
                  pl.BlockSpec((e, d), lambda i: (0, 0)),
                  pl.BlockSpec((e, 1), lambda i: (0, 0))],
        out_specs=[tok, tok, tok, pl.BlockSpec((e, 128), lambda i: (0, 0))],
        out_shape=[jax.ShapeDtypeStruct((TOP_K, s), jnp.int32),
                   jax.ShapeDtypeStruct((TOP_K, s), F32),
                   jax.ShapeDtypeStruct((TOP_K, s), jnp.int32),
                   jax.ShapeDtypeStruct((e, 128), jnp.int32)],
        scratch_shapes=[pltpu.VMEM((e, 128), F32)],
        compiler_params=_cparams(("arbitrary",)),
        name="route",
    )(hf, wrt, bias_col)


def _positions_kernel(idx_ref, rnk_ref, pstart_ref, pos_ref):
    e = pstart_ref.shape[0]
    ts = idx_ref.shape[1]
    ids = lax.broadcasted_iota(jnp.int32, (e, ts), 0)
    idx = idx_ref[...]
    pos_ref[...] = rnk_ref[...] + jnp.concatenate(
        [jnp.sum(jnp.where(ids == idx[k:k + 1, :], pstart_ref[...], 0), axis=0, keepdims=True)
         for k in range(TOP_K)], axis=0)


def _positions(idx, rnk, pstart_col, ts):
    s = idx.shape[1]
    e = pstart_col.shape[0]
    tok = pl.BlockSpec((TOP_K, ts), lambda i: (0, i))
    return pl.pallas_call(
        _positions_kernel,
        grid=(s // ts,),
        in_specs=[tok, tok, pl.BlockSpec((e, 1), lambda i: (0, 0))],
        out_specs=tok,
        out_shape=jax.ShapeDtypeStruct((TOP_K, s), jnp.int32),
        compiler_params=_cparams(("arbitrary",)),
        name="positions",
    )(idx, rnk, pstart_col)


def _swiglu_packed(xp, wg, wu, wd):
    lo, hi = _unpack_halves(xp)
    lo, hi = lo.astype(BF16), hi.astype(BF16)
    n = lo.shape[1]
    g = (jnp.dot(lo, wg[:n], preferred_element_type=F32)
         + jnp.dot(hi, wg[n:], preferred_element_type=F32))
    u = (jnp.dot(lo, wu[:n], preferred_element_type=F32)
         + jnp.dot(hi, wu[n:], preferred_element_type=F32))
    h = (g * jax.nn.sigmoid(g)) * u
    return jnp.dot(h.astype(BF16), wd[...], preferred_element_type=F32)


def _moe_kernel(ie_ref, ib_ref, first_ref, slot_ref, ne_ref, lead_ref, rows_ref, nv_ref, xs_ref,
                wg_hbm, wu_hbm, wd_hbm, ys_ref, wg_f, wu_f, wd_f, sem):
    del ib_ref
    i = pl.program_id(0)

    def fetch(e, slot):
        copies = []
        for n, (src, dst) in enumerate(((wg_hbm, wg_f), (wu_hbm, wu_f), (wd_hbm, wd_f))):
            rows = src.shape[1] // MOE_DMA_CHUNKS
            for ch in range(MOE_DMA_CHUNKS):
                part = pl.ds(ch * rows, rows)
                copies.append(pltpu.make_async_copy(
                    src.at[e, part], dst.at[slot, part], sem.at[slot, n * MOE_DMA_CHUNKS + ch]))
        return copies

    @pl.when(i == 0)
    def _():
        for s in range(MOE_SLOTS - 1):
            @pl.when(lead_ref[s] >= 0)
            def _(s=s):
                for cp in fetch(lead_ref[s], s):
                    cp.start()

    @pl.when(i < nv_ref[0])
    def _():
        for slot in range(MOE_SLOTS):
            @pl.when((first_ref[i] == 1) & (slot_ref[i] == slot))
            def _(slot=slot):
                for cp in fetch(ie_ref[i], slot):
                    cp.wait()

                @pl.when(ne_ref[i] >= 0)
                def _():
                    for cp in fetch(ne_ref[i], (slot + MOE_SLOTS - 1) % MOE_SLOTS):
                        cp.start()

        slot = slot_ref[i]
        row = lax.broadcasted_iota(jnp.int32, xs_ref.shape, 0)
        xp = jnp.where(row < rows_ref[i], xs_ref[...], jnp.uint32(0))
        ys_ref[...] = _pack_halves(_swiglu_packed(
            xp, wg_f[slot].astype(BF16), wu_f[slot].astype(BF16), wd_f[slot].astype(BF16)))


def _moe(item_e, item_b, item_first, item_slot, item_next, lead, item_rows, n_valid, xs, wg, wu,
         wd):
    m_pad, dh = xs.shape
    _, d, f = wg.shape
    n_items = item_e.shape[0]
    blk = lambda i, ie, ib, fi, sl, ne, ld, nr, nv: (ib[i], 0)
    hbm = pl.BlockSpec(memory_space=pl.ANY)
    return pl.pallas_call(
        _moe_kernel,
        grid_spec=pltpu.PrefetchScalarGridSpec(
            num_scalar_prefetch=8,
            grid=(n_items,),
            in_specs=[pl.BlockSpec((MOE_ROWS, dh), blk), hbm, hbm, hbm],
            out_specs=pl.BlockSpec((MOE_ROWS, dh), blk),
            scratch_shapes=[pltpu.VMEM((MOE_SLOTS, d, f), F32), pltpu.VMEM((MOE_SLOTS, d, f), F32),
                            pltpu.VMEM((MOE_SLOTS, f, d), F32),
                            pltpu.SemaphoreType.DMA((MOE_SLOTS, 3 * MOE_DMA_CHUNKS))],
        ),
        out_shape=jax.ShapeDtypeStruct((m_pad, dh), jnp.uint32),
        compiler_params=_cparams(("arbitrary",)),
        name="moe",
    )(item_e, item_b, item_first, item_slot, item_next, lead, item_rows, n_valid, xs, wg, wu, wd)


def _sc_gather_rows(table, idx_row):
    m = idx_row.shape[1]
    w = table.shape[1]
    idx_row = idx_row.reshape(m // SC_GATHER_WINDOW, SC_GATHER_WINDOW)
    mesh = plsc.VectorSubcoreMesh(core_axis_name="c", subcore_axis_name="s")

    @functools.partial(pl.kernel, mesh=mesh,
                       out_type=jax.ShapeDtypeStruct((m, w), table.dtype))
    def gather(table_hbm, idx_hbm, out_hbm):
        def body(idx_vmem, out_vmem):
            pltpu.sync_copy(table_hbm.at[idx_vmem.at[0]], out_vmem)

        pltpu.emit_pipeline(
            body,
            grid=(m // SC_GATHER_WINDOW,),
            in_specs=[pl.BlockSpec((1, SC_GATHER_WINDOW), lambda i: (i, 0))],
            out_specs=[pl.BlockSpec((SC_GATHER_WINDOW, w), lambda i: (i, 0))],
            core_axis_name=("c", "s"),
            dimension_semantics=(pltpu.PARALLEL,),
        )(idx_hbm, out_hbm)

    return gather(table, idx_row)


def _sc_scatter_rows(rows, idx_blocks, m_out):
    s, w = rows.shape
    mesh = plsc.VectorSubcoreMesh(core_axis_name="c", subcore_axis_name="s")

    @functools.partial(pl.kernel, mesh=mesh,
                       out_type=jax.ShapeDtypeStruct((m_out, w), rows.dtype))
    def scatter(rows_hbm, idx_hbm, out_hbm):
        def body(rows_vmem, idx_vmem):
            for k in range(TOP_K):
                pltpu.sync_copy(rows_vmem, out_hbm.at[idx_vmem.at[k]])

        pltpu.emit_pipeline(
            body,
            grid=(s // SC_GATHER_WINDOW,),
            in_specs=[pl.BlockSpec((SC_GATHER_WINDOW, w), lambda i: (i, 0)),
                      pl.BlockSpec((TOP_K, SC_GATHER_WINDOW), lambda i: (i, 0))],
            out_specs=[],
            core_axis_name=("c", "s"),
            dimension_semantics=(pltpu.PARALLEL,),
        )(rows_hbm, idx_hbm)

    return scatter(rows, idx_blocks)


def _combine_kernel(wt_ref, hf_ref, x1_ref, gt_ref, sg_ref, su_ref, sd_ref, g_ref, o_ref):
    tc = x1_ref.shape[0]
    y = _swiglu_packed(hf_ref[...], sg_ref, su_ref, sd_ref)
    wt = wt_ref[...]
    n = g_ref.shape[2]
    r_lo = jnp.zeros((tc, n), F32)
    r_hi = jnp.zeros((tc, n), F32)
    for k in range(TOP_K):
        lo, hi = _unpack_halves(g_ref[k])
        r_lo = r_lo + lo * wt[:, k:k + 1]
        r_hi = r_hi + hi * wt[:, k:k + 1]
    y = y + jnp.concatenate([r_lo, r_hi], axis=1)
    o_ref[...] = x1_ref[...] + gt_ref[...] * y


def _combine(wts_t, hfp, x1, gt, sg, su, sd, gathered, tc):
    s, d = x1.shape
    f = sg.shape[1]
    row = pl.BlockSpec((tc, d), lambda i: (i, 0))
    return pl.pallas_call(
        _combine_kernel,
        grid=(s // tc,),
        in_specs=[pl.BlockSpec((tc, TOP_K), lambda i: (i, 0)),
                  pl.BlockSpec((tc, d // 2), lambda i: (i, 0)), row,
                  pl.BlockSpec((1, d), lambda i: (0, 0)),
                  pl.BlockSpec((d, f), lambda i: (0, 0)),
                  pl.BlockSpec((d, f), lambda i: (0, 0)),
                  pl.BlockSpec((f, d), lambda i: (0, 0)),
                  pl.BlockSpec((TOP_K, tc, d // 2), lambda i: (0, i, 0))],
        out_specs=row,
        out_shape=jax.ShapeDtypeStruct((s, d), F32),
        compiler_params=_cparams(("arbitrary",)),
        name="combine",
    )(wts_t, hfp, x1, gt, sg, su, sd, gathered)


def _tile(n, want):
    t = min(n, want)
    assert n % t == 0, (n, t)
    return t


def _layer(l, x, c_col, pos_row, p):
    s, d = x.shape
    lambda_init = 0.8 - 0.6 * math.exp(-0.3 * l)
    gqk, gv = GLA_HEADS * GLA_DK, GLA_HEADS * GLA_DV
    dqk, dvw = DIFF_HEADS * 2 * DIFF_DH, DIFF_HEADS * DIFF_DV
    lowrank = p["gla_w_a2"].shape[0]

    mod = _ada(c_col, p["w_ada"], p["b_ada"][None, :])
    sh_a, sc_a, gt_a, sh_f, sc_f, gt_f = [mod[:, j * d:(j + 1) * d] for j in range(6)]

    w_in = p["w_in"]
    o = 0
    cols = {}
    for name, wdt in (("gq", gqk), ("gk", gqk), ("gv", gv), ("ga", lowrank), ("gg", gv),
                      ("dq", dqk), ("dk", dqk), ("dv", dvw), ("mg", d), ("md", d)):
        cols[name] = w_in[:, o:o + wdt]
        o += wdt
    w_ga = jnp.pad(cols["ga"], ((0, 0), (0, 128 - lowrank))).astype(BF16)

    g1 = p["norm1_g"][None, :]
    ts = _tile(s, TILE_SEQ)

    invf = ROPE_THETA ** (-jnp.arange(0, ROT_DIM, 2, dtype=F32) / ROT_DIM)
    qat, qbt, kr, vte, qn, kn, gkt = _qkvprep(
        x, g1, sc_a, sh_a, *(cols[n].T.astype(BF16) for n in ("dq", "dk", "dv", "gk")),
        pos_row, invf[:, None], p["diff_qnorm_g"][:, None], p["diff_knorm_g"][:, None], ts)

    wa2t = jnp.pad(p["gla_w_a2"].T, ((0, 0), (0, 128 - lowrank)))
    o_gla = _gla(x, g1, sc_a, sh_a, cols["gq"].astype(BF16), cols["gv"].astype(BF16),
                 cols["gg"].astype(BF16), w_ga, gkt, wa2t, p["gla_b_a"][:, None],
                 p["gla_onorm_g"][None, :], ts)
    tq = ts
    o_diff = _diffattn(qat, qbt, kr, vte, qn, kn, p["diff_lq1"][None, :], p["diff_lk1"][None, :],
                       p["diff_lq2"][None, :], p["diff_lk2"][None, :],
                       p["diff_subln_g"][:, None], lambda_init, tq, tq)

    x1, hf, hfp = _mergeout(o_gla, o_diff, x, g1, sc_a, sh_a, cols["mg"].astype(BF16),
                            cols["md"].astype(BF16), p["w_branch_gla"].astype(BF16),
                            p["w_branch_diff"].astype(BF16), p["w_out"].astype(BF16), gt_a,
                            p["norm2_g"][None, :], sc_f, sh_f, ts)

    e = p["w_router"].shape[1]
    idx, wts, rnk, cnt = _route(hf, p["w_router"].T, p["router_bias"][:, None], ts)

    counts = cnt[:, 0]
    pcounts = ((counts + MOE_ROWS - 1) // MOE_ROWS) * MOE_ROWS
    pend = jnp.cumsum(pcounts)
    pstart = pend - pcounts
    pos = _positions(idx, rnk, pstart[:, None], ts)
    n_items = (s * TOP_K) // MOE_ROWS + e
    n_valid = (pend[-1] // MOE_ROWS).astype(jnp.int32)
    item_b = jnp.minimum(jnp.arange(n_items, dtype=jnp.int32), n_valid - 1)
    item_e = jnp.minimum(jnp.sum(pend[None, :] <= (item_b * MOE_ROWS)[:, None], axis=1),
                         e - 1).astype(jnp.int32)

    wn = SC_GATHER_WINDOW
    pos_w = pos.reshape(TOP_K, s // wn, wn).transpose(1, 0, 2).reshape(s // wn * TOP_K, wn)
    xs = _sc_scatter_rows(hfp, pos_w, n_items * MOE_ROWS)
    item_rows = jnp.clip(pstart[item_e] + counts[item_e] - item_b * MOE_ROWS, 0,
                         MOE_ROWS).astype(jnp.int32)
    prev_e = jnp.concatenate([jnp.full((1,), -1, jnp.int32), item_e[:-1]])
    item_first = ((jnp.arange(n_items) < n_valid) & (item_e != prev_e)).astype(jnp.int32)
    item_slot = ((jnp.cumsum(item_first) - 1) % MOE_SLOTS).astype(jnp.int32)
    cand = jnp.where(pcounts > 0, jnp.arange(e, dtype=jnp.int32), e)
    nonempty_from = lax.cummin(cand[::-1])[::-1]
    following = jnp.concatenate([nonempty_from[1:], jnp.full((2,), e, jnp.int32)])
    ahead = item_e
    lead = [nonempty_from[0]]
    for _ in range(MOE_SLOTS - 1):
        ahead = following[ahead]
        lead.append(following[lead[-1]])
    item_next = jnp.where(ahead < e, ahead, -1).astype(jnp.int32)
    lead = jnp.stack(lead[:MOE_SLOTS - 1])
    lead = jnp.where(lead < e, lead, -1).astype(jnp.int32)
    ys = _moe(item_e, item_b, item_first, item_slot, item_next, lead, item_rows, n_valid[None],
              xs, p["w_exp_gate"], p["w_exp_up"], p["w_exp_down"])
    gathered = _sc_gather_rows(ys, pos.reshape(1, TOP_K * s)).reshape(TOP_K, s, d // 2)
    return _combine(wts.T, hfp, x1, gt_f, p["w_sh_gate"].astype(BF16),
                    p["w_sh_up"].astype(BF16), p["w_sh_down"].astype(BF16), gathered,
                    _tile(s, TILE_COMBINE))


_LAYER_PARAMS = ("w_ada", "b_ada", "norm1_g", "w_in", "gla_w_a2", "gla_b_a", "gla_onorm_g",
                 "diff_qnorm_g", "diff_knorm_g", "diff_lq1", "diff_lk1", "diff_lq2", "diff_lk2",
                 "diff_subln_g", "w_branch_gla", "w_branch_diff", "w_out", "norm2_g", "w_router",
                 "router_bias", "w_exp_gate", "w_exp_up", "w_exp_down", "w_sh_gate", "w_sh_up",
                 "w_sh_down")


def kernel(x, c, positions, w_ada, b_ada, norm1_g, w_in, gla_w_a2, gla_b_a, gla_onorm_g, diff_qnorm_g, diff_knorm_g, diff_lq1, diff_lk1, diff_lq2, diff_lk2, diff_subln_g, w_branch_gla, w_branch_diff, w_out, norm2_g, w_router, router_bias, w_exp_gate, w_exp_up, w_exp_down, w_sh_gate, w_sh_up, w_sh_down):
    stacked = dict(zip(_LAYER_PARAMS, (
        w_ada, b_ada, norm1_g, w_in, gla_w_a2, gla_b_a, gla_onorm_g, diff_qnorm_g, diff_knorm_g,
        diff_lq1, diff_lk1, diff_lq2, diff_lk2, diff_subln_g, w_branch_gla, w_branch_diff, w_out,
        norm2_g, w_router, router_bias, w_exp_gate, w_exp_up, w_exp_down, w_sh_gate, w_sh_up,
        w_sh_down)))
    b, s, d = x.shape
    assert b == 1, "single-sequence kernel"
    xl = x[0]
    c_col = c[0][:, None]
    pos_row = positions.astype(jnp.int32)
    for l in range(w_ada.shape[0]):
        xl = _layer(l, xl, c_col, pos_row, {k: v[l] for k, v in stacked.items()})
    return xl[None]
```

```python
import functools
import math

import jax
import jax.numpy as jnp
from jax import lax
from jax.experimental import pallas as pl
from jax.experimental.pallas import tpu as pltpu
from jax.experimental.pallas import tpu_sc as plsc

CHUNK = 64
EPS = 1e-6
GLA_HEADS = 4
GLA_DK = 128
GLA_DV = 256
GLA_TAU = 16.0
DIFF_HEADS = 8
DIFF_DH = 64
DIFF_DV = 2 * DIFF_DH
ROPE_THETA = 500000.0
ROT_DIM = DIFF_DH // 4
N_GROUPS = 8
TOPK_GROUPS = 4
TOP_K = 8
ROUTED_SCALE = 2.5

MOE_ROWS = 640
MOE_DMA_CHUNKS = 4
MOE_SLOTS = 3
SC_GATHER_WINDOW = 64
VMEM_LIMIT = 56 * 1024 * 1024
TILE_SEQ = 512
TILE_COMBINE = 256
NEG_BIG = -1e30
LOG2E = 1.4426950408889634
HIGHEST = lax.Precision.HIGHEST
F32 = jnp.float32
BF16 = jnp.bfloat16


def _cparams(sem):
    return pltpu.CompilerParams(dimension_semantics=sem, vmem_limit_bytes=VMEM_LIMIT)


def _nt_dot(a, b, precision=None):
    return lax.dot_general(a, b, (((1,), (1,)), ((), ())), precision=precision,
                           preferred_element_type=F32)


def _pack_halves(x):
    n = x.shape[1] // 2
    lo = pltpu.bitcast(x[:, :n].astype(BF16).astype(F32), jnp.uint32) >> 16
    hi = pltpu.bitcast(x[:, n:].astype(BF16).astype(F32), jnp.uint32) & jnp.uint32(0xFFFF0000)
    return lo | hi


def _unpack_halves(w):
    return (pltpu.bitcast(w << 16, F32), pltpu.bitcast(w & jnp.uint32(0xFFFF0000), F32))


def _rms_mod(x, g, sc, sh):
    xn = x * lax.rsqrt(jnp.mean(x * x, axis=-1, keepdims=True) + EPS)
    return (xn * g) * (1.0 + sc) + sh


def _ada_kernel(c_ref, w_ref, b_ref, o_ref):
    c = c_ref[...]
    ca = c * jax.nn.sigmoid(c)
    o_ref[...] = jnp.sum(ca * w_ref[...], axis=0, keepdims=True) + b_ref[...]


def _ada(c_col, w, b):
    d, n = w.shape
    tn = min(1024, n)
    return pl.pallas_call(
        _ada_kernel,
        grid=(n // tn,),
        in_specs=[pl.BlockSpec((d, 1), lambda j: (0, 0)),
                  pl.BlockSpec((d, tn), lambda j: (0, j)),
                  pl.BlockSpec((1, tn), lambda j: (0, j))],
        out_specs=pl.BlockSpec((1, tn), lambda j: (0, j)),
        out_shape=jax.ShapeDtypeStruct((1, n), F32),
        compiler_params=_cparams(("arbitrary",)),
        name="ada",
    )(c_col, w, b)


def _gla_kernel(x_ref, g1_ref, sc_ref, sh_ref, wq_ref, wv_ref, wg_ref, wga_ref, kt_ref, wa2t_ref,
                ba_ref, on_ref, o_ref, state_ref, o_scr, q_ref, v_ref, gg_ref):
    tt = x_ref.shape[0]
    nchunk = tt // CHUNK

    @pl.when(pl.program_id(0) == 0)
    def _():
        state_ref[...] = jnp.zeros_like(state_ref)

    h_in = _rms_mod(x_ref[...], g1_ref[...], sc_ref[...], sh_ref[...]).astype(BF16)
    q_ref[...] = jnp.dot(h_in, wq_ref[...], preferred_element_type=F32).astype(BF16)
    v_ref[...] = jnp.dot(h_in, wv_ref[...], preferred_element_type=F32).astype(BF16)
    gg_ref[...] = jnp.dot(h_in, wg_ref[...], preferred_element_type=F32).astype(BF16)
    ga = jnp.dot(h_in, wga_ref[...], preferred_element_type=F32)

    zt = _nt_dot(wa2t_ref[...], ga, precision=HIGHEST) + ba_ref[...]
    lat = (jnp.minimum(zt, 0.0) - jnp.log1p(jnp.exp(-jnp.abs(zt)))) * (1.0 / GLA_TAU)
    row = lax.broadcasted_iota(jnp.int32, (tt, tt), 0)
    col = lax.broadcasted_iota(jnp.int32, (tt, tt), 1)
    same = (row // CHUNK) == (col // CHUNK)
    incl = jnp.where(same & (row <= col), 1.0, 0.0).astype(BF16)
    full = jnp.where(same, 1.0, 0.0).astype(BF16)
    lat_hi = lat.astype(BF16)
    lat_lo = (lat - lat_hi.astype(F32)).astype(BF16)
    cumt = (jnp.dot(lat_hi, incl, preferred_element_type=F32)
            + jnp.dot(lat_lo, incl, preferred_element_type=F32))
    tott = (jnp.dot(lat_hi, full, preferred_element_type=F32)
            + jnp.dot(lat_lo, full, preferred_element_type=F32))
    kdt = kt_ref[...].astype(F32) * jnp.exp(tott - cumt)
    dec = jnp.exp(tott)

    lane = lax.broadcasted_iota(jnp.int32, (GLA_DK, 2 * CHUNK), 1)
    upd = {}
    for c in range(nchunk):
        pair = (c // 2) * 2 * CHUNK
        if nchunk > 1:
            keep = (lane // CHUNK) == (c % 2)
        for h in range(GLA_HEADS):
            rows = slice(h * GLA_DK, (h + 1) * GLA_DK)
            vcols = slice(h * GLA_DV, (h + 1) * GLA_DV)
            if nchunk > 1:
                a = jnp.where(keep, kdt[rows, pair:pair + 2 * CHUNK], 0.0).astype(BF16)
                vp = v_ref[pair:pair + 2 * CHUNK, vcols]
            else:
                a = kdt[rows, :].astype(BF16)
                vp = v_ref[:, vcols]
            upd[c, h] = jnp.dot(a, vp, preferred_element_type=F32)

    for h in range(GLA_HEADS):
        rows = slice(h * GLA_DK, (h + 1) * GLA_DK)
        vcols = slice(h * GLA_DV, (h + 1) * GLA_DV)
        st = state_ref[h]
        states = []
        for c in range(nchunk):
            st = st * dec[rows, c * CHUNK:c * CHUNK + 1] + upd[c, h]
            states.append(st.astype(BF16))
        state_ref[h] = st
        for c in range(nchunk):
            o_scr[c * CHUNK:(c + 1) * CHUNK, vcols] = jnp.dot(
                q_ref[c * CHUNK:(c + 1) * CHUNK, rows], states[c], preferred_element_type=F32)

    for h in range(GLA_HEADS):
        vcols = slice(h * GLA_DV, (h + 1) * GLA_DV)
        o = o_scr[:, vcols] * (GLA_DK ** -0.5)
        o = o * lax.rsqrt(jnp.mean(o * o, axis=-1, keepdims=True) + EPS) * on_ref[...]
        g = gg_ref[:, vcols].astype(F32)
        o_ref[:, vcols] = (o * (g * jax.nn.sigmoid(g))).astype(BF16)


def _gla(x, g1, sc, sh, wq, wv, wg, wga, gkt, wa2t, ba_col, on_g, tt):
    s, d = x.shape
    qk = GLA_HEADS * GLA_DK
    vw = GLA_HEADS * GLA_DV
    vec = pl.BlockSpec((1, d), lambda i: (0, 0))
    whole = lambda a: pl.BlockSpec(a.shape, lambda i: (0, 0))
    return pl.pallas_call(
        _gla_kernel,
        grid=(s // tt,),
        in_specs=[pl.BlockSpec((tt, d), lambda i: (i, 0)), vec, vec, vec,
                  whole(wq), whole(wv), whole(wg), whole(wga),
                  pl.BlockSpec((qk, tt), lambda i: (0, i)),
                  whole(wa2t), whole(ba_col), whole(on_g)],
        out_specs=pl.BlockSpec((tt, vw), lambda i: (i, 0)),
        out_shape=jax.ShapeDtypeStruct((s, vw), BF16),
        scratch_shapes=[pltpu.VMEM((GLA_HEADS, GLA_DK, GLA_DV), F32),
                        pltpu.VMEM((tt, vw), F32),
                        pltpu.VMEM((tt, qk), BF16), pltpu.VMEM((tt, vw), BF16),
                        pltpu.VMEM((tt, vw), BF16)],
        compiler_params=_cparams(("arbitrary",)),
        name="gla",
    )(x, g1, sc, sh, wq, wv, wg, wga, gkt, wa2t, ba_col, on_g)


def _qknorm_rope_t(xt, g_col, cos, sin):
    n, tm = xt.shape
    x3 = xt.reshape(n // DIFF_DH, DIFF_DH, tm)
    r = lax.rsqrt(jnp.mean(x3 * x3, axis=1, keepdims=True) + EPS)
    y = x3 * r * g_col[None]
    half = ROT_DIM // 2
    y1, y2, rest = y[:, :half], y[:, half:ROT_DIM], y[:, ROT_DIM:]
    o1 = y1 * cos[None] - y2 * sin[None]
    o2 = y2 * cos[None] + y1 * sin[None]
    return jnp.concatenate([o1, o2, rest], axis=1)


def _seg_norms(x3):
    return jnp.sqrt(jnp.sum(x3 * x3, axis=1)).reshape(DIFF_HEADS, 2, x3.shape[2])


def _qkvprep_kernel(x_ref, g_ref, sc_ref, sh_ref, wq_ref, wk_ref, wv_ref, wgk_ref, pos_ref,
                    invf_ref, qg_ref, kg_ref, qa_ref, qb_ref, ko_ref, ve_ref, qn_ref, kn_ref,
                    gk_ref):
    tm = x_ref.shape[0]
    h = _rms_mod(x_ref[...], g_ref[...], sc_ref[...], sh_ref[...]).astype(BF16)
    qt = _nt_dot(wq_ref[...], h)
    kt = _nt_dot(wk_ref[...], h)
    vt = _nt_dot(wv_ref[...], h)
    gk_ref[...] = _nt_dot(wgk_ref[...], h).astype(BF16)
    ang = pos_ref[...].astype(F32) * invf_ref[...]
    cos, sin = jnp.cos(ang), jnp.sin(ang)
    q3 = _qknorm_rope_t(qt, qg_ref[...], cos, sin) * (DIFF_DH ** -0.5 * LOG2E)
    qn_ref[...] = _seg_norms(q3)
    seg = lax.broadcasted_iota(jnp.int32, q3.shape, 0)
    qa_ref[...] = jnp.where(seg % 2 == 0, q3, 0.0).reshape(-1, tm).astype(BF16)
    qb_ref[...] = jnp.where(seg % 2 == 1, q3, 0.0).reshape(-1, tm).astype(BF16)
    k3 = _qknorm_rope_t(kt, kg_ref[...], cos, sin)
    ko_ref[...] = k3.reshape(-1, tm).T.astype(BF16)
    kn_ref[...] = _seg_norms(k3)
    v3 = vt.astype(BF16).reshape(DIFF_HEADS, DIFF_DV, tm)
    ones = jnp.ones((DIFF_HEADS, ATT_SUM_ROWS, tm), BF16)
    ve_ref[...] = jnp.concatenate([v3, ones], axis=1).reshape(-1, tm)


def _qkvprep(x, g, sc, sh, wq_t, wk_t, wv_t, wgk_t, pos_row, invf_col, qg_col, kg_col, tm):
    s, d = x.shape
    n = DIFF_HEADS * 2 * DIFF_DH
    ne = DIFF_HEADS * (DIFF_DV + ATT_SUM_ROWS)
    ngk = wgk_t.shape[0]
    col = pl.BlockSpec((DIFF_DH, 1), lambda i: (0, 0))
    vec = pl.BlockSpec((1, d), lambda i: (0, 0))
    wspec = pl.BlockSpec((n, d), lambda i: (0, 0))
    return pl.pallas_call(
        _qkvprep_kernel,
        grid=(s // tm,),
        in_specs=[pl.BlockSpec((tm, d), lambda i: (i, 0)), vec, vec, vec,
                  wspec, wspec, wspec, pl.BlockSpec((ngk, d), lambda i: (0, 0)),
                  pl.BlockSpec((1, tm), lambda i: (0, i)),
                  pl.BlockSpec((ROT_DIM // 2, 1), lambda i: (0, 0)), col, col],
        out_specs=[pl.BlockSpec((n, tm), lambda i: (0, i)),
                   pl.BlockSpec((n, tm), lambda i: (0, i)),
                   pl.BlockSpec((tm, n), lambda i: (i, 0)),
                   pl.BlockSpec((ne, tm), lambda i: (0, i)),
                   pl.BlockSpec((DIFF_HEADS, 2, tm), lambda i: (0, 0, i)),
                   pl.BlockSpec((DIFF_HEADS, 2, tm), lambda i: (0, 0, i)),
                   pl.BlockSpec((ngk, tm), lambda i: (0, i))],
        out_shape=[jax.ShapeDtypeStruct((n, s), BF16), jax.ShapeDtypeStruct((n, s), BF16),
                   jax.ShapeDtypeStruct((s, n), BF16), jax.ShapeDtypeStruct((ne, s), BF16),
                   jax.ShapeDtypeStruct((DIFF_HEADS, 2, s), F32),
                   jax.ShapeDtypeStruct((DIFF_HEADS, 2, s), F32),
                   jax.ShapeDtypeStruct((ngk, s), BF16)],
        compiler_params=_cparams(("arbitrary",)),
        name="qkvprep",
    )(x, g, sc, sh, wq_t, wk_t, wv_t, wgk_t, pos_row, invf_col, qg_col, kg_col)


ATT_COLS = 256
ATT_LOOKAHEAD = {True: 2, False: 4}
ATT_BODY_TILES = {True: 8, False: 2}
ATT_SUM_ROWS = 16
ATT_BOUND_SLACK = 1.02
ATT_BOUND_LIMIT = 50.0


def _diffattn_kernel(qa_ref, qb_ref, k_ref, vt_ref, qn_ref, kn_ref, lq1_ref, lk1_ref, lq2_ref,
                     lk2_ref, sg_ref, o_ref, *scr, lambda_init, tk, cols):
    tq = qa_ref.shape[1]
    nblk = 2 * tq // cols
    m_scr, acc_scr = (scr[b * nblk:(b + 1) * nblk] for b in range(2))
    kmax_scr = scr[2 * nblk]
    s_scr = scr[2 * nblk + 1:]
    i = pl.program_id(1)

    @pl.when(i == 0)
    def _():
        kmax_scr[...] = jnp.max(kn_ref[...], axis=1, keepdims=True)

    bound = qn_ref[...] * kmax_scr[...] * ATT_BOUND_SLACK
    bound = jnp.concatenate([bound[0:1], bound[1:2]], axis=1)
    bounded = jnp.max(bound) < ATT_BOUND_LIMIT
    for c in range(nblk):
        m_scr[c][...] = jnp.where(bounded, bound[:, c * cols:(c + 1) * cols], NEG_BIG)
        acc_scr[c][...] = jnp.zeros_like(acc_scr[c])

    def scores(j, c):
        start = pl.multiple_of(j * tk, tk)
        q_ref = qa_ref if c * cols < tq else qb_ref
        off = (c * cols) % tq
        return jnp.dot(k_ref[pl.ds(start, tk), :], q_ref[:, off:off + cols],
                       preferred_element_type=F32)

    def steps(tiles, masked, next_tile, fixed):
        look = ATT_LOOKAHEAD[fixed]
        items = [(j, c) for j in tiles for c in range(nblk)]
        pending = []
        for n, (j, c) in enumerate(items):
            s = s_scr[n][...] if n < look else pending.pop(0)
            ahead = n + look
            if ahead < len(items):
                pending.append(scores(*items[ahead]))
            elif next_tile is not None:
                s_scr[ahead - len(items)][...] = scores(next_tile, ahead - len(items))
            start = pl.multiple_of(j * tk, tk)
            keys = tk
            if masked:
                keys = (c * cols) % tq + cols
                s = s[:keys]
                krow = lax.broadcasted_iota(jnp.int32, (keys, cols), 0)
                qcol = lax.broadcasted_iota(jnp.int32, (keys, cols), 1)
                qpos = i * tq + (c * cols) % tq + qcol
                s = jnp.where((start + krow) // CHUNK <= qpos // CHUNK, s, NEG_BIG)
            vj = vt_ref[:, pl.ds(start, keys)]
            if fixed:
                p = jnp.exp2(s - m_scr[c][...])
                acc_scr[c][:DIFF_DV] += jnp.dot(vj[:DIFF_DV], p.astype(BF16),
                                                preferred_element_type=F32)
                acc_scr[c][DIFF_DV:DIFF_DV + 1] += jnp.sum(p, axis=0, keepdims=True)
            else:
                m_prev = m_scr[c][...]
                m_new = jnp.maximum(m_prev, jnp.max(s, axis=0, keepdims=True))
                alpha = jnp.exp2(m_prev - m_new)
                p = jnp.exp2((s - m_new).astype(BF16))
                acc_scr[c][...] = alpha * acc_scr[c][...] + jnp.dot(
                    vj, p, preferred_element_type=F32)
                m_scr[c][...] = m_new

    def attend(fixed):
        for c in range(ATT_LOOKAHEAD[fixed]):
            s_scr[c][...] = scores(0, c)
        big = ATT_BODY_TILES[fixed]
        lax.fori_loop(0, i // big, lambda t, c: (steps(
            tuple(big * t + u for u in range(big)), False, big * t + big, fixed), c)[1], 0)
        size = big // 2
        while size >= 1:
            first = (i // (2 * size)) * (2 * size)

            @pl.when(i % (2 * size) >= size)
            def _(first=first, size=size):
                steps(tuple(first + u for u in range(size)), False, first + size, fixed)

            size //= 2
        steps((i,), True, None, fixed)

    @pl.when(bounded)
    def _():
        attend(True)

    @pl.when(jnp.logical_not(bounded))
    def _():
        attend(False)

    o = jnp.concatenate([acc_scr[c][:DIFF_DV] * (1.0 / acc_scr[c][DIFF_DV:DIFF_DV + 1])
                         for c in range(nblk)], axis=1)
    lam = (jnp.exp(jnp.sum(lq1_ref[...] * lk1_ref[...]))
           - jnp.exp(jnp.sum(lq2_ref[...] * lk2_ref[...])) + lambda_init)
    o = o[:, :tq] - lam * o[:, tq:]
    o = o * lax.rsqrt(jnp.mean(o * o, axis=0, keepdims=True) + EPS) * sg_ref[...]
    o_ref[...] = (o * (1.0 - lambda_init)).T.astype(BF16)


def _diffattn(qat, qbt, kr, vte, qn, kn, lq1, lk1, lq2, lk2, sg_col, lambda_init, tq, tk):
    s = kr.shape[0]
    hd = 2 * DIFF_DH
    vec = pl.BlockSpec((1, DIFF_DH), lambda h, i: (0, 0))
    cols = min(ATT_COLS, tq)
    nblk = 2 * tq // cols
    kern = functools.partial(_diffattn_kernel, lambda_init=lambda_init, tk=tk, cols=cols)
    dve = DIFF_DV + ATT_SUM_ROWS
    assert tq == tk and nblk >= max(ATT_LOOKAHEAD.values())
    scratch = ([pltpu.VMEM((1, cols), F32)] * nblk + [pltpu.VMEM((dve, cols), F32)] * nblk
               + [pltpu.VMEM((2, 1), F32)]
               + [pltpu.VMEM((tk, cols), F32)] * max(ATT_LOOKAHEAD.values()))
    return pl.pallas_call(
        kern,
        grid=(DIFF_HEADS, s // tq),
        in_specs=[pl.BlockSpec((hd, tq), lambda h, i: (h, i)),
                  pl.BlockSpec((hd, tq), lambda h, i: (h, i)),
                  pl.BlockSpec((s, hd), lambda h, i: (0, h)),
                  pl.BlockSpec((dve, s), lambda h, i: (h, 0)),
                  pl.BlockSpec((None, 2, tq), lambda h, i: (h, 0, i)),
                  pl.BlockSpec((None, 2, s), lambda h, i: (h, 0, 0)),
                  vec, vec, vec, vec,
                  pl.BlockSpec((DIFF_DV, 1), lambda h, i: (0, 0))],
        out_specs=pl.BlockSpec((tq, DIFF_DV), lambda h, i: (i, h)),
        out_shape=jax.ShapeDtypeStruct((s, DIFF_HEADS * DIFF_DV), BF16),
        scratch_shapes=scratch,
        compiler_params=_cparams(("arbitrary", "arbitrary")),
        name="diffattn",
    )(qat, qbt, kr, vte, qn, kn, lq1, lk1, lq2, lk2, sg_col)


def _mergeout_kernel(og_ref, od_ref, x_ref, g1_ref, sca_ref, sha_ref, wmg_ref, wmd_ref, wbg_ref,
                     wbd_ref, wo_ref, gt_ref, g2_ref, sc_ref, sh_ref, wrt_ref, bias_ref,
                     x1_ref, hfp_ref, idx_ref, wts_ref, rnk_ref, cnt_ref, run_scr):
    x = x_ref[...]
    h_in = _rms_mod(x, g1_ref[...], sca_ref[...], sha_ref[...]).astype(BF16)
    mg = jnp.dot(h_in, wmg_ref[...], preferred_element_type=F32)
    md = jnp.dot(h_in, wmd_ref[...], preferred_element_type=F32)
    bg = jnp.dot(og_ref[...], wbg_ref[...], preferred_element_type=F32)
    bd = jnp.dot(od_ref[...], wbd_ref[...], preferred_element_type=F32)
    merged = jax.nn.sigmoid(mg) * bg + jax.nn.sigmoid(md) * bd
    x1 = x + gt_ref[...] * jnp.dot(merged.astype(BF16), wo_ref[...],
                                   preferred_element_type=F32)
    x1_ref[...] = x1
    hf = _rms_mod(x1, g2_ref[...], sc_ref[...], sh_ref[...])
    hfp_ref[...] = _pack_halves(hf)
    _route_tile(hf, wrt_ref, bias_ref, idx_ref, wts_ref, rnk_ref, cnt_ref, run_scr)


def _mergeout(og, od, x, g1, sca, sha, wmg, wmd, wbg, wbd, wo, gt, g2, sc, sh, wrt, bias_col,
              tm):
    s, d = x.shape
    e = wrt.shape[0]
    vec = pl.BlockSpec((1, d), lambda i: (0, 0))
    wspec = pl.BlockSpec((d, d), lambda i: (0, 0))
    row = pl.BlockSpec((tm, d), lambda i: (i, 0))
    tok = pl.BlockSpec((TOP_K, tm), lambda i: (0, i))
    return pl.pallas_call(
        _mergeout_kernel,
        grid=(s // tm,),
        in_specs=[row, row, row, vec, vec, vec, wspec, wspec, wspec, wspec, wspec,
                  vec, vec, vec, vec,
                  pl.BlockSpec((e, d), lambda i: (0, 0)), pl.BlockSpec((e, 1), lambda i: (0, 0))],
        out_specs=[row, pl.BlockSpec((tm, d // 2), lambda i: (i, 0)), tok, tok, tok,
                   pl.BlockSpec((e, 128), lambda i: (0, 0))],
        out_shape=[jax.ShapeDtypeStruct((s, d), F32),
                   jax.ShapeDtypeStruct((s, d // 2), jnp.uint32),
                   jax.ShapeDtypeStruct((TOP_K, s), jnp.int32),
                   jax.ShapeDtypeStruct((TOP_K, s), F32),
                   jax.ShapeDtypeStruct((TOP_K, s), jnp.int32),
                   jax.ShapeDtypeStruct((e, 128), jnp.int32)],
        scratch_shapes=[pltpu.VMEM((e, 128), F32)],
        compiler_params=_cparams(("arbitrary",)),
        name="mergeout",
    )(og, od, x, g1, sca, sha, wmg, wmd, wbg, wbd, wo, gt, g2, sc, sh, wrt, bias_col)


def _route_tile(hf, wrt_ref, bias_ref, idx_ref, wts_ref, rnk_ref, cnt_ref, run_scr):
    tr = hf.shape[0]
    e = wrt_ref.shape[0]
    gsz = e // N_GROUPS

    @pl.when(pl.program_id(0) == 0)
    def _():
        run_scr[...] = jnp.zeros_like(run_scr)

    logits = _nt_dot(wrt_ref[...], hf, precision=HIGHEST)
    scores = jax.nn.sigmoid(logits)
    biased = scores + bias_ref[...]
    g3 = biased.reshape(N_GROUPS, gsz, tr)
    m1 = jnp.max(g3, axis=1, keepdims=True)
    n_top = jnp.sum(jnp.where(g3 == m1, 1.0, 0.0), axis=1, keepdims=True)
    m2 = jnp.max(jnp.where(g3 < m1, g3, -jnp.inf), axis=1, keepdims=True)
    gs = (m1 + jnp.where(n_top >= 2.0, m1, m2)).reshape(N_GROUPS, tr)
    gi = lax.broadcasted_iota(jnp.int32, (N_GROUPS, tr), 0)
    beaten = jnp.zeros((N_GROUPS, tr), F32)
    for g in range(N_GROUPS):
        other = gs[g:g + 1, :]
        beaten = beaten + jnp.where((other > gs) | ((other == gs) & (g < gi)), 1.0, 0.0)
    gsel = (beaten < float(TOPK_GROUPS)).reshape(N_GROUPS, 1, tr)
    masked = jnp.where(gsel, g3, -jnp.inf).reshape(e, tr)

    ids = lax.broadcasted_iota(jnp.int32, (e, tr), 0)
    chosen = jnp.zeros((e, tr), F32)
    sel_idx, sel_score = [], []
    for _ in range(TOP_K):
        mx = jnp.max(masked, axis=0, keepdims=True)
        ix = jnp.min(jnp.where(masked == mx, ids, e), axis=0, keepdims=True)
        hit = ids == ix
        sel_idx.append(ix)
        sel_score.append(jnp.sum(jnp.where(hit, scores, 0.0), axis=0, keepdims=True))
        chosen = jnp.where(hit, 1.0, chosen)
        masked = jnp.where(hit, -jnp.inf, masked)
    idx = jnp.concatenate(sel_idx, axis=0)
    sc = jnp.concatenate(sel_score, axis=0)
    idx_ref[...] = idx
    wts_ref[...] = sc / jnp.sum(sc, axis=0, keepdims=True) * ROUTED_SCALE

    row = lax.broadcasted_iota(jnp.int32, (tr, tr), 0)
    col = lax.broadcasted_iota(jnp.int32, (tr, tr), 1)
    before = jnp.where(row < col, 1.0, 0.0).astype(BF16)
    prior = jnp.dot(chosen.astype(BF16), before, preferred_element_type=F32) + run_scr[:, 0:1]
    rnk_ref[...] = jnp.concatenate(
        [jnp.sum(jnp.where(ids == sel_idx[k], prior, 0.0), axis=0, keepdims=True)
         for k in range(TOP_K)], axis=0).astype(jnp.int32)
    run_scr[...] = run_scr[...] + jnp.sum(chosen, axis=1, keepdims=True)
    cnt_ref[...] = run_scr[...].astype(jnp.int32)


def _positions_kernel(idx_ref, rnk_ref, pstart_ref, pos_ref):
    e = pstart_ref.shape[0]
    ts = idx_ref.shape[1]
    ids = lax.broadcasted_iota(jnp.int32, (e, ts), 0)
    idx = idx_ref[...]
    pos_ref[...] = rnk_ref[...] + jnp.concatenate(
        [jnp.sum(jnp.where(ids == idx[k:k + 1, :], pstart_ref[...], 0), axis=0, keepdims=True)
         for k in range(TOP_K)], axis=0)


def _positions(idx, rnk, pstart_col, ts):
    s = idx.shape[1]
    e = pstart_col.shape[0]
    tok = pl.BlockSpec((TOP_K, ts), lambda i: (0, i))
    return pl.pallas_call(
        _positions_kernel,
        grid=(s // ts,),
        in_specs=[tok, tok, pl.BlockSpec((e, 1), lambda i: (0, 0))],
        out_specs=tok,
        out_shape=jax.ShapeDtypeStruct((TOP_K, s), jnp.int32),
        compiler_params=_cparams(("arbitrary",)),
        name="positions",
    )(idx, rnk, pstart_col)


def _swiglu_packed(xp, wg, wu, wd):
    lo, hi = _unpack_halves(xp)
    lo, hi = lo.astype(BF16), hi.astype(BF16)
    n = lo.shape[1]
    g = (jnp.dot(lo, wg[:n], preferred_element_type=F32)
         + jnp.dot(hi, wg[n:], preferred_element_type=F32))
    u = (jnp.dot(lo, wu[:n], preferred_element_type=F32)
         + jnp.dot(hi, wu[n:], preferred_element_type=F32))
    h = (g * jax.nn.sigmoid(g)) * u
    return jnp.dot(h.astype(BF16), wd[...], preferred_element_type=F32)


def _moe_kernel(ie_ref, ib_ref, first_ref, slot_ref, ne_ref, lead_ref, rows_ref, nv_ref, xs_ref,
                wg_hbm, wu_hbm, wd_hbm, ys_ref, wg_f, wu_f, wd_f, sem):
    del ib_ref
    i = pl.program_id(0)

    def fetch(e, slot):
        copies = []
        for n, (src, dst) in enumerate(((wg_hbm, wg_f), (wu_hbm, wu_f), (wd_hbm, wd_f))):
            rows = src.shape[1] // MOE_DMA_CHUNKS
            for ch in range(MOE_DMA_CHUNKS):
                part = pl.ds(ch * rows, rows)
                copies.append(pltpu.make_async_copy(
                    src.at[e, part], dst.at[slot, part], sem.at[slot, n * MOE_DMA_CHUNKS + ch]))
        return copies

    @pl.when(i == 0)
    def _():
        for s in range(MOE_SLOTS - 1):
            @pl.when(lead_ref[s] >= 0)
            def _(s=s):
                for cp in fetch(lead_ref[s], s):
                    cp.start()

    @pl.when(i < nv_ref[0])
    def _():
        for slot in range(MOE_SLOTS):
            @pl.when((first_ref[i] == 1) & (slot_ref[i] == slot))
            def _(slot=slot):
                for cp in fetch(ie_ref[i], slot):
                    cp.wait()

                @pl.when(ne_ref[i] >= 0)
                def _():
                    for cp in fetch(ne_ref[i], (slot + MOE_SLOTS - 1) % MOE_SLOTS):
                        cp.start()

        slot = slot_ref[i]
        row = lax.broadcasted_iota(jnp.int32, xs_ref.shape, 0)
        xp = jnp.where(row < rows_ref[i], xs_ref[...], jnp.uint32(0))
        ys_ref[...] = _pack_halves(_swiglu_packed(
            xp, wg_f[slot].astype(BF16), wu_f[slot].astype(BF16), wd_f[slot].astype(BF16)))


def _moe(item_e, item_b, item_first, item_slot, item_next, lead, item_rows, n_valid, xs, wg, wu,
         wd):
    m_pad, dh = xs.shape
    _, d, f = wg.shape
    n_items = item_e.shape[0]
    blk = lambda i, ie, ib, fi, sl, ne, ld, nr, nv: (ib[i], 0)
    hbm = pl.BlockSpec(memory_space=pl.ANY)
    return pl.pallas_call(
        _moe_kernel,
        grid_spec=pltpu.PrefetchScalarGridSpec(
            num_scalar_prefetch=8,
            grid=(n_items,),
            in_specs=[pl.BlockSpec((MOE_ROWS, dh), blk), hbm, hbm, hbm],
            out_specs=pl.BlockSpec((MOE_ROWS, dh), blk),
            scratch_shapes=[pltpu.VMEM((MOE_SLOTS, d, f), F32), pltpu.VMEM((MOE_SLOTS, d, f), F32),
                            pltpu.VMEM((MOE_SLOTS, f, d), F32),
                            pltpu.SemaphoreType.DMA((MOE_SLOTS, 3 * MOE_DMA_CHUNKS))],
        ),
        out_shape=jax.ShapeDtypeStruct((m_pad, dh), jnp.uint32),
        compiler_params=_cparams(("arbitrary",)),
        name="moe",
    )(item_e, item_b, item_first, item_slot, item_next, lead, item_rows, n_valid, xs, wg, wu, wd)


def _sc_gather_rows(table, idx_row):
    m = idx_row.shape[1]
    w = table.shape[1]
    idx_row = idx_row.reshape(m // SC_GATHER_WINDOW, SC_GATHER_WINDOW)
    mesh = plsc.VectorSubcoreMesh(core_axis_name="c", subcore_axis_name="s")

    @functools.partial(pl.kernel, mesh=mesh,
                       out_type=jax.ShapeDtypeStruct((m, w), table.dtype))
    def gather(table_hbm, idx_hbm, out_hbm):
        def body(idx_vmem, out_vmem):
            pltpu.sync_copy(table_hbm.at[idx_vmem.at[0]], out_vmem)

        pltpu.emit_pipeline(
            body,
            grid=(m // SC_GATHER_WINDOW,),
            in_specs=[pl.BlockSpec((1, SC_GATHER_WINDOW), lambda i: (i, 0))],
            out_specs=[pl.BlockSpec((SC_GATHER_WINDOW, w), lambda i: (i, 0))],
            core_axis_name=("c", "s"),
            dimension_semantics=(pltpu.PARALLEL,),
        )(idx_hbm, out_hbm)

    return gather(table, idx_row)


def _sc_scatter_rows(rows, idx_blocks, m_out):
    s, w = rows.shape
    mesh = plsc.VectorSubcoreMesh(core_axis_name="c", subcore_axis_name="s")

    @functools.partial(pl.kernel, mesh=mesh,
                       out_type=jax.ShapeDtypeStruct((m_out, w), rows.dtype))
    def scatter(rows_hbm, idx_hbm, out_hbm):
        def body(rows_vmem, idx_vmem):
            for k in range(TOP_K):
                pltpu.sync_copy(rows_vmem, out_hbm.at[idx_vmem.at[k]])

        pltpu.emit_pipeline(
            body,
            grid=(s // SC_GATHER_WINDOW,),
            in_specs=[pl.BlockSpec((SC_GATHER_WINDOW, w), lambda i: (i, 0)),
                      pl.BlockSpec((TOP_K, SC_GATHER_WINDOW), lambda i: (i, 0))],
            out_specs=[],
            core_axis_name=("c", "s"),
            dimension_semantics=(pltpu.PARALLEL,),
        )(rows_hbm, idx_hbm)

    return scatter(rows, idx_blocks)


def _combine_kernel(wt_ref, hf_ref, x1_ref, gt_ref, sg_ref, su_ref, sd_ref, g_ref, o_ref):
    tc = x1_ref.shape[0]
    y = _swiglu_packed(hf_ref[...], sg_ref, su_ref, sd_ref)
    wt = wt_ref[...]
    n = g_ref.shape[2]
    r_lo = jnp.zeros((tc, n), F32)
    r_hi = jnp.zeros((tc, n), F32)
    for k in range(TOP_K):
        lo, hi = _unpack_halves(g_ref[k])
        r_lo = r_lo + lo * wt[:, k:k + 1]
        r_hi = r_hi + hi * wt[:, k:k + 1]
    y = y + jnp.concatenate([r_lo, r_hi], axis=1)
    o_ref[...] = x1_ref[...] + gt_ref[...] * y


def _combine(wts_t, hfp, x1, gt, sg, su, sd, gathered, tc):
    s, d = x1.shape
    f = sg.shape[1]
    row = pl.BlockSpec((tc, d), lambda i: (i, 0))
    return pl.pallas_call(
        _combine_kernel,
        grid=(s // tc,),
        in_specs=[pl.BlockSpec((tc, TOP_K), lambda i: (i, 0)),
                  pl.BlockSpec((tc, d // 2), lambda i: (i, 0)), row,
                  pl.BlockSpec((1, d), lambda i: (0, 0)),
                  pl.BlockSpec((d, f), lambda i: (0, 0)),
                  pl.BlockSpec((d, f), lambda i: (0, 0)),
                  pl.BlockSpec((f, d), lambda i: (0, 0)),
                  pl.BlockSpec((TOP_K, tc, d // 2), lambda i: (0, i, 0))],
        out_specs=row,
        out_shape=jax.ShapeDtypeStruct((s, d), F32),
        compiler_params=_cparams(("arbitrary",)),
        name="combine",
    )(wts_t, hfp, x1, gt, sg, su, sd, gathered)


def _tile(n, want):
    t = min(n, want)
    assert n % t == 0, (n, t)
    return t


def _layer(l, x, c_col, pos_row, p):
    s, d = x.shape
    lambda_init = 0.8 - 0.6 * math.exp(-0.3 * l)
    gqk, gv = GLA_HEADS * GLA_DK, GLA_HEADS * GLA_DV
    dqk, dvw = DIFF_HEADS * 2 * DIFF_DH, DIFF_HEADS * DIFF_DV
    lowrank = p["gla_w_a2"].shape[0]

    mod = _ada(c_col, p["w_ada"], p["b_ada"][None, :])
    sh_a, sc_a, gt_a, sh_f, sc_f, gt_f = [mod[:, j * d:(j + 1) * d] for j in range(6)]

    w_in = p["w_in"]
    o = 0
    cols = {}
    for name, wdt in (("gq", gqk), ("gk", gqk), ("gv", gv), ("ga", lowrank), ("gg", gv),
                      ("dq", dqk), ("dk", dqk), ("dv", dvw), ("mg", d), ("md", d)):
        cols[name] = w_in[:, o:o + wdt]
        o += wdt
    w_ga = jnp.pad(cols["ga"], ((0, 0), (0, 128 - lowrank))).astype(BF16)

    g1 = p["norm1_g"][None, :]
    ts = _tile(s, TILE_SEQ)

    invf = ROPE_THETA ** (-jnp.arange(0, ROT_DIM, 2, dtype=F32) / ROT_DIM)
    qat, qbt, kr, vte, qn, kn, gkt = _qkvprep(
        x, g1, sc_a, sh_a, *(cols[n].T.astype(BF16) for n in ("dq", "dk", "dv", "gk")),
        pos_row, invf[:, None], p["diff_qnorm_g"][:, None], p["diff_knorm_g"][:, None], ts)

    wa2t = jnp.pad(p["gla_w_a2"].T, ((0, 0), (0, 128 - lowrank)))
    o_gla = _gla(x, g1, sc_a, sh_a, cols["gq"].astype(BF16), cols["gv"].astype(BF16),
                 cols["gg"].astype(BF16), w_ga, gkt, wa2t, p["gla_b_a"][:, None],
                 p["gla_onorm_g"][None, :], ts)
    tq = ts
    o_diff = _diffattn(qat, qbt, kr, vte, qn, kn, p["diff_lq1"][None, :], p["diff_lk1"][None, :],
                       p["diff_lq2"][None, :], p["diff_lk2"][None, :],
                       p["diff_subln_g"][:, None], lambda_init, tq, tq)

    e = p["w_router"].shape[1]
    x1, hfp, idx, wts, rnk, cnt = _mergeout(
        o_gla, o_diff, x, g1, sc_a, sh_a, cols["mg"].astype(BF16), cols["md"].astype(BF16),
        p["w_branch_gla"].astype(BF16), p["w_branch_diff"].astype(BF16),
        p["w_out"].astype(BF16), gt_a, p["norm2_g"][None, :], sc_f, sh_f,
        p["w_router"].T, p["router_bias"][:, None], ts)

    counts = cnt[:, 0]
    pcounts = ((counts + MOE_ROWS - 1) // MOE_ROWS) * MOE_ROWS
    pend = jnp.cumsum(pcounts)
    pstart = pend - pcounts
    pos = _positions(idx, rnk, pstart[:, None], ts)
    n_items = (s * TOP_K) // MOE_ROWS + e
    n_valid = (pend[-1] // MOE_ROWS).astype(jnp.int32)
    item_b = jnp.minimum(jnp.arange(n_items, dtype=jnp.int32), n_valid - 1)
    item_e = jnp.minimum(jnp.sum(pend[None, :] <= (item_b * MOE_ROWS)[:, None], axis=1),
                         e - 1).astype(jnp.int32)

    wn = SC_GATHER_WINDOW
    pos_w = pos.reshape(TOP_K, s // wn, wn).transpose(1, 0, 2).reshape(s // wn * TOP_K, wn)
    xs = _sc_scatter_rows(hfp, pos_w, n_items * MOE_ROWS)
    item_rows = jnp.clip(pstart[item_e] + counts[item_e] - item_b * MOE_ROWS, 0,
                         MOE_ROWS).astype(jnp.int32)
    prev_e = jnp.concatenate([jnp.full((1,), -1, jnp.int32), item_e[:-1]])
    item_first = ((jnp.arange(n_items) < n_valid) & (item_e != prev_e)).astype(jnp.int32)
    item_slot = ((jnp.cumsum(item_first) - 1) % MOE_SLOTS).astype(jnp.int32)
    cand = jnp.where(pcounts > 0, jnp.arange(e, dtype=jnp.int32), e)
    nonempty_from = lax.cummin(cand[::-1])[::-1]
    following = jnp.concatenate([nonempty_from[1:], jnp.full((2,), e, jnp.int32)])
    ahead = item_e
    lead = [nonempty_from[0]]
    for _ in range(MOE_SLOTS - 1):
        ahead = following[ahead]
        lead.append(following[lead[-1]])
    item_next = jnp.where(ahead < e, ahead, -1).astype(jnp.int32)
    lead = jnp.stack(lead[:MOE_SLOTS - 1])
    lead = jnp.where(lead < e, lead, -1).astype(jnp.int32)
    ys = _moe(item_e, item_b, item_first, item_slot, item_next, lead, item_rows, n_valid[None],
              xs, p["w_exp_gate"], p["w_exp_up"], p["w_exp_down"])
    gathered = _sc_gather_rows(ys, pos.reshape(1, TOP_K * s)).reshape(TOP_K, s, d // 2)
    return _combine(wts.T, hfp, x1, gt_f, p["w_sh_gate"].astype(BF16),
                    p["w_sh_up"].astype(BF16), p["w_sh_down"].astype(BF16), gathered,
                    _tile(s, TILE_COMBINE))


_LAYER_PARAMS = ("w_ada", "b_ada", "norm1_g", "w_in", "gla_w_a2", "gla_b_a", "gla_onorm_g",
                 "diff_qnorm_g", "diff_knorm_g", "diff_lq1", "diff_lk1", "diff_lq2", "diff_lk2",
                 "diff_subln_g", "w_branch_gla", "w_branch_diff", "w_out", "norm2_g", "w_router",
                 "router_bias", "w_exp_gate", "w_exp_up", "w_exp_down", "w_sh_gate", "w_sh_up",
                 "w_sh_down")


def kernel(x, c, positions, w_ada, b_ada, norm1_g, w_in, gla_w_a2, gla_b_a, gla_onorm_g, diff_qnorm_g, diff_knorm_g, diff_lq1, diff_lk1, diff_lq2, diff_lk2, diff_subln_g, w_branch_gla, w_branch_diff, w_out, norm2_g, w_router, router_bias, w_exp_gate, w_exp_up, w_exp_down, w_sh_gate, w_sh_up, w_sh_down):
    stacked = dict(zip(_LAYER_PARAMS, (
        w_ada, b_ada, norm1_g, w_in, gla_w_a2, gla_b_a, gla_onorm_g, diff_qnorm_g, diff_knorm_g,
        diff_lq1, diff_lk1, diff_lq2, diff_lk2, diff_subln_g, w_branch_gla, w_branch_diff, w_out,
        norm2_g, w_router, router_bias, w_exp_gate, w_exp_up, w_exp_down, w_sh_gate, w_sh_up,
        w_sh_down)))
    b, s, d = x.shape
    assert b == 1, "single-sequence kernel"
    xl = x[0]
    c_col = c[0][:, None]
    pos_row = positions.astype(jnp.int32)
    for l in range(w_ada.shape[0]):
        xl = _layer(l, xl, c_col, pos_row, {k: v[l] for k, v in stacked.items()})
    return xl[None]
```

```python
import functools
import math

import jax
import jax.numpy as jnp
from jax import lax
from jax.experimental import pallas as pl
from jax.experimental.pallas import tpu as pltpu
from jax.experimental.pallas import tpu_sc as plsc

CHUNK = 64
EPS = 1e-6
GLA_HEADS = 4
GLA_DK = 128
GLA_DV = 256
GLA_TAU = 16.0
DIFF_HEADS = 8
DIFF_DH = 64
DIFF_DV = 2 * DIFF_DH
ROPE_THETA = 500000.0
ROT_DIM = DIFF_DH // 4
N_GROUPS = 8
TOPK_GROUPS = 4
TOP_K = 8
ROUTED_SCALE = 2.5

MOE_ROWS = 640
MOE_DMA_CHUNKS = 4
MOE_SLOTS = 3
SC_GATHER_WINDOW = 64
VMEM_LIMIT = 56 * 1024 * 1024
TILE_SEQ = 512
TILE_COMBINE = 256
NEG_BIG = -1e30
LOG2E = 1.4426950408889634
HIGHEST = lax.Precision.HIGHEST
F32 = jnp.float32
BF16 = jnp.bfloat16


def _cparams(sem):
    return pltpu.CompilerParams(dimension_semantics=sem, vmem_limit_bytes=VMEM_LIMIT)


def _nt_dot(a, b, precision=None):
    return lax.dot_general(a, b, (((1,), (1,)), ((), ())), precision=precision,
                           preferred_element_type=F32)


def _pack_halves(x):
    n = x.shape[1] // 2
    lo = pltpu.bitcast(x[:, :n].astype(BF16).astype(F32), jnp.uint32) >> 16
    hi = pltpu.bitcast(x[:, n:].astype(BF16).astype(F32), jnp.uint32) & jnp.uint32(0xFFFF0000)
    return lo | hi


def _unpack_halves(w):
    return (pltpu.bitcast(w << 16, F32), pltpu.bitcast(w & jnp.uint32(0xFFFF0000), F32))


def _rms_mod(x, g, sc, sh):
    xn = x * lax.rsqrt(jnp.mean(x * x, axis=-1, keepdims=True) + EPS)
    return (xn * g) * (1.0 + sc) + sh


def _ada_kernel(c_ref, w_ref, b_ref, o_ref):
    c = c_ref[...]
    ca = c * jax.nn.sigmoid(c)
    o_ref[...] = jnp.sum(ca * w_ref[...], axis=0, keepdims=True) + b_ref[...]


def _ada(c_col, w, b):
    d, n = w.shape
    tn = min(1024, n)
    return pl.pallas_call(
        _ada_kernel,
        grid=(n // tn,),
        in_specs=[pl.BlockSpec((d, 1), lambda j: (0, 0)),
                  pl.BlockSpec((d, tn), lambda j: (0, j)),
                  pl.BlockSpec((1, tn), lambda j: (0, j))],
        out_specs=pl.BlockSpec((1, tn), lambda j: (0, j)),
        out_shape=jax.ShapeDtypeStruct((1, n), F32),
        compiler_params=_cparams(("arbitrary",)),
        name="ada",
    )(c_col, w, b)


def _gla_kernel(x_ref, g1_ref, sc_ref, sh_ref, wq_ref, wv_ref, wg_ref, wga_ref, kt_ref, wa2t_ref,
                ba_ref, on_ref, o_ref, state_ref, o_scr, q_ref, v_ref, gg_ref):
    tt = x_ref.shape[0]
    nchunk = tt // CHUNK

    @pl.when(pl.program_id(0) == 0)
    def _():
        state_ref[...] = jnp.zeros_like(state_ref)

    h_in = _rms_mod(x_ref[...], g1_ref[...], sc_ref[...], sh_ref[...]).astype(BF16)
    q_ref[...] = jnp.dot(h_in, wq_ref[...], preferred_element_type=F32).astype(BF16)
    v_ref[...] = jnp.dot(h_in, wv_ref[...], preferred_element_type=F32).astype(BF16)
    gg_ref[...] = jnp.dot(h_in, wg_ref[...], preferred_element_type=F32).astype(BF16)
    ga = jnp.dot(h_in, wga_ref[...], preferred_element_type=F32)

    zt = _nt_dot(wa2t_ref[...], ga, precision=HIGHEST) + ba_ref[...]
    lat = (jnp.minimum(zt, 0.0) - jnp.log1p(jnp.exp(-jnp.abs(zt)))) * (1.0 / GLA_TAU)
    row = lax.broadcasted_iota(jnp.int32, (tt, tt), 0)
    col = lax.broadcasted_iota(jnp.int32, (tt, tt), 1)
    same = (row // CHUNK) == (col // CHUNK)
    incl = jnp.where(same & (row <= col), 1.0, 0.0).astype(BF16)
    full = jnp.where(same, 1.0, 0.0).astype(BF16)
    lat_hi = lat.astype(BF16)
    lat_lo = (lat - lat_hi.astype(F32)).astype(BF16)
    cumt = (jnp.dot(lat_hi, incl, preferred_element_type=F32)
            + jnp.dot(lat_lo, incl, preferred_element_type=F32))
    tott = (jnp.dot(lat_hi, full, preferred_element_type=F32)
            + jnp.dot(lat_lo, full, preferred_element_type=F32))
    kdt = kt_ref[...].astype(F32) * jnp.exp(tott - cumt)
    dec = jnp.exp(tott)

    lane = lax.broadcasted_iota(jnp.int32, (GLA_DK, 2 * CHUNK), 1)
    upd = {}
    for c in range(nchunk):
        pair = (c // 2) * 2 * CHUNK
        if nchunk > 1:
            keep = (lane // CHUNK) == (c % 2)
        for h in range(GLA_HEADS):
            rows = slice(h * GLA_DK, (h + 1) * GLA_DK)
            vcols = slice(h * GLA_DV, (h + 1) * GLA_DV)
            if nchunk > 1:
                a = jnp.where(keep, kdt[rows, pair:pair + 2 * CHUNK], 0.0).astype(BF16)
                vp = v_ref[pair:pair + 2 * CHUNK, vcols]
            else:
                a = kdt[rows, :].astype(BF16)
                vp = v_ref[:, vcols]
            upd[c, h] = jnp.dot(a, vp, preferred_element_type=F32)

    for h in range(GLA_HEADS):
        rows = slice(h * GLA_DK, (h + 1) * GLA_DK)
        vcols = slice(h * GLA_DV, (h + 1) * GLA_DV)
        st = state_ref[h]
        states = []
        for c in range(nchunk):
            st = st * dec[rows, c * CHUNK:c * CHUNK + 1] + upd[c, h]
            states.append(st.astype(BF16))
        state_ref[h] = st
        for c in range(nchunk):
            o_scr[c * CHUNK:(c + 1) * CHUNK, vcols] = jnp.dot(
                q_ref[c * CHUNK:(c + 1) * CHUNK, rows], states[c], preferred_element_type=F32)

    for h in range(GLA_HEADS):
        vcols = slice(h * GLA_DV, (h + 1) * GLA_DV)
        o = o_scr[:, vcols] * (GLA_DK ** -0.5)
        o = o * lax.rsqrt(jnp.mean(o * o, axis=-1, keepdims=True) + EPS) * on_ref[...]
        g = gg_ref[:, vcols].astype(F32)
        o_ref[:, vcols] = (o * (g * jax.nn.sigmoid(g))).astype(BF16)


def _gla(x, g1, sc, sh, wq, wv, wg, wga, gkt, wa2t, ba_col, on_g, tt):
    s, d = x.shape
    qk = GLA_HEADS * GLA_DK
    vw = GLA_HEADS * GLA_DV
    vec = pl.BlockSpec((1, d), lambda i: (0, 0))
    whole = lambda a: pl.BlockSpec(a.shape, lambda i: (0, 0))
    return pl.pallas_call(
        _gla_kernel,
        grid=(s // tt,),
        in_specs=[pl.BlockSpec((tt, d), lambda i: (i, 0)), vec, vec, vec,
                  whole(wq), whole(wv), whole(wg), whole(wga),
                  pl.BlockSpec((qk, tt), lambda i: (0, i)),
                  whole(wa2t), whole(ba_col), whole(on_g)],
        out_specs=pl.BlockSpec((tt, vw), lambda i: (i, 0)),
        out_shape=jax.ShapeDtypeStruct((s, vw), BF16),
        scratch_shapes=[pltpu.VMEM((GLA_HEADS, GLA_DK, GLA_DV), F32),
                        pltpu.VMEM((tt, vw), F32),
                        pltpu.VMEM((tt, qk), BF16), pltpu.VMEM((tt, vw), BF16),
                        pltpu.VMEM((tt, vw), BF16)],
        compiler_params=_cparams(("arbitrary",)),
        name="gla",
    )(x, g1, sc, sh, wq, wv, wg, wga, gkt, wa2t, ba_col, on_g)


def _qknorm_rope_t(xt, g_col, cos, sin):
    n, tm = xt.shape
    x3 = xt.reshape(n // DIFF_DH, DIFF_DH, tm)
    r = lax.rsqrt(jnp.mean(x3 * x3, axis=1, keepdims=True) + EPS)
    y = x3 * r * g_col[None]
    half = ROT_DIM // 2
    y1, y2, rest = y[:, :half], y[:, half:ROT_DIM], y[:, ROT_DIM:]
    o1 = y1 * cos[None] - y2 * sin[None]
    o2 = y2 * cos[None] + y1 * sin[None]
    return jnp.concatenate([o1, o2, rest], axis=1)


def _seg_norms(x3):
    return jnp.sqrt(jnp.sum(x3 * x3, axis=1)).reshape(DIFF_HEADS, 2, x3.shape[2])


def _qkvprep_kernel(x_ref, g_ref, sc_ref, sh_ref, wq_ref, wk_ref, wv_ref, wgk_ref, pos_ref,
                    invf_ref, qg_ref, kg_ref, qa_ref, qb_ref, ko_ref, ve_ref, qn_ref, kn_ref,
                    gk_ref):
    tm = x_ref.shape[0]
    h = _rms_mod(x_ref[...], g_ref[...], sc_ref[...], sh_ref[...]).astype(BF16)
    qt = _nt_dot(wq_ref[...], h)
    kt = _nt_dot(wk_ref[...], h)
    vt = _nt_dot(wv_ref[...], h)
    gk_ref[...] = _nt_dot(wgk_ref[...], h).astype(BF16)
    ang = pos_ref[...].astype(F32) * invf_ref[...]
    cos, sin = jnp.cos(ang), jnp.sin(ang)
    q3 = _qknorm_rope_t(qt, qg_ref[...], cos, sin) * (DIFF_DH ** -0.5 * LOG2E)
    qn_ref[...] = _seg_norms(q3)
    seg = lax.broadcasted_iota(jnp.int32, q3.shape, 0)
    qa_ref[...] = jnp.where(seg % 2 == 0, q3, 0.0).reshape(-1, tm).astype(BF16)
    qb_ref[...] = jnp.where(seg % 2 == 1, q3, 0.0).reshape(-1, tm).astype(BF16)
    k3 = _qknorm_rope_t(kt, kg_ref[...], cos, sin)
    ko_ref[...] = k3.reshape(-1, tm).T.astype(BF16)
    kn_ref[...] = _seg_norms(k3)
    v3 = vt.astype(BF16).reshape(DIFF_HEADS, DIFF_DV, tm)
    ones = jnp.ones((DIFF_HEADS, ATT_SUM_ROWS, tm), BF16)
    ve_ref[...] = jnp.concatenate([v3, ones], axis=1).reshape(-1, tm)


def _qkvprep(x, g, sc, sh, wq_t, wk_t, wv_t, wgk_t, pos_row, invf_col, qg_col, kg_col, tm):
    s, d = x.shape
    n = DIFF_HEADS * 2 * DIFF_DH
    ne = DIFF_HEADS * (DIFF_DV + ATT_SUM_ROWS)
    ngk = wgk_t.shape[0]
    col = pl.BlockSpec((DIFF_DH, 1), lambda i: (0, 0))
    vec = pl.BlockSpec((1, d), lambda i: (0, 0))
    wspec = pl.BlockSpec((n, d), lambda i: (0, 0))
    return pl.pallas_call(
        _qkvprep_kernel,
        grid=(s // tm,),
        in_specs=[pl.BlockSpec((tm, d), lambda i: (i, 0)), vec, vec, vec,
                  wspec, wspec, wspec, pl.BlockSpec((ngk, d), lambda i: (0, 0)),
                  pl.BlockSpec((1, tm), lambda i: (0, i)),
                  pl.BlockSpec((ROT_DIM // 2, 1), lambda i: (0, 0)), col, col],
        out_specs=[pl.BlockSpec((n, tm), lambda i: (0, i)),
                   pl.BlockSpec((n, tm), lambda i: (0, i)),
                   pl.BlockSpec((tm, n), lambda i: (i, 0)),
                   pl.BlockSpec((ne, tm), lambda i: (0, i)),
                   pl.BlockSpec((DIFF_HEADS, 2, tm), lambda i: (0, 0, i)),
                   pl.BlockSpec((DIFF_HEADS, 2, tm), lambda i: (0, 0, i)),
                   pl.BlockSpec((ngk, tm), lambda i: (0, i))],
        out_shape=[jax.ShapeDtypeStruct((n, s), BF16), jax.ShapeDtypeStruct((n, s), BF16),
                   jax.ShapeDtypeStruct((s, n), BF16), jax.ShapeDtypeStruct((ne, s), BF16),
                   jax.ShapeDtypeStruct((DIFF_HEADS, 2, s), F32),
                   jax.ShapeDtypeStruct((DIFF_HEADS, 2, s), F32),
                   jax.ShapeDtypeStruct((ngk, s), BF16)],
        compiler_params=_cparams(("arbitrary",)),
        name="qkvprep",
    )(x, g, sc, sh, wq_t, wk_t, wv_t, wgk_t, pos_row, invf_col, qg_col, kg_col)


ATT_COLS = 256
ATT_LOOKAHEAD = {True: 2, False: 4}
ATT_BODY_TILES = {True: 8, False: 2}
ATT_SUM_ROWS = 16
ATT_BOUND_SLACK = 1.02
ATT_BOUND_LIMIT = 50.0


def _diffattn_kernel(qa_ref, qb_ref, k_ref, vt_ref, qn_ref, kn_ref, lq1_ref, lk1_ref, lq2_ref,
                     lk2_ref, sg_ref, o_ref, *scr, lambda_init, tk, cols):
    tq = qa_ref.shape[1]
    nblk = 2 * tq // cols
    m_scr, acc_scr = (scr[b * nblk:(b + 1) * nblk] for b in range(2))
    kmax_scr = scr[2 * nblk]
    s_scr = scr[2 * nblk + 1:]
    i = pl.program_id(1)

    @pl.when(i == 0)
    def _():
        kmax_scr[...] = jnp.max(kn_ref[...], axis=1, keepdims=True)

    bound = qn_ref[...] * kmax_scr[...] * ATT_BOUND_SLACK
    bound = jnp.concatenate([bound[0:1], bound[1:2]], axis=1)
    bounded = jnp.max(bound) < ATT_BOUND_LIMIT
    for c in range(nblk):
        m_scr[c][...] = jnp.where(bounded, bound[:, c * cols:(c + 1) * cols], NEG_BIG)
        acc_scr[c][...] = jnp.zeros_like(acc_scr[c])

    def scores(j, c):
        start = pl.multiple_of(j * tk, tk)
        q_ref = qa_ref if c * cols < tq else qb_ref
        off = (c * cols) % tq
        return jnp.dot(k_ref[pl.ds(start, tk), :], q_ref[:, off:off + cols],
                       preferred_element_type=F32)

    def steps(tiles, masked, next_tile, fixed):
        look = ATT_LOOKAHEAD[fixed]
        items = [(j, c) for j in tiles for c in range(nblk)]
        pending = []
        for n, (j, c) in enumerate(items):
            s = s_scr[n][...] if n < look else pending.pop(0)
            ahead = n + look
            if ahead < len(items):
                pending.append(scores(*items[ahead]))
            elif next_tile is not None:
                s_scr[ahead - len(items)][...] = scores(next_tile, ahead - len(items))
            start = pl.multiple_of(j * tk, tk)
            keys = tk
            if masked:
                keys = (c * cols) % tq + cols
                s = s[:keys]
                krow = lax.broadcasted_iota(jnp.int32, (keys, cols), 0)
                qcol = lax.broadcasted_iota(jnp.int32, (keys, cols), 1)
                qpos = i * tq + (c * cols) % tq + qcol
                s = jnp.where((start + krow) // CHUNK <= qpos // CHUNK, s, NEG_BIG)
            vj = vt_ref[:, pl.ds(start, keys)]
            if fixed:
                p = jnp.exp2(s - m_scr[c][...])
                acc_scr[c][:DIFF_DV] += jnp.dot(vj[:DIFF_DV], p.astype(BF16),
                                                preferred_element_type=F32)
                acc_scr[c][DIFF_DV:DIFF_DV + 1] += jnp.sum(p, axis=0, keepdims=True)
            else:
                m_prev = m_scr[c][...]
                m_new = jnp.maximum(m_prev, jnp.max(s, axis=0, keepdims=True))
                alpha = jnp.exp2(m_prev - m_new)
                p = jnp.exp2((s - m_new).astype(BF16))
                acc_scr[c][...] = alpha * acc_scr[c][...] + jnp.dot(
                    vj, p, preferred_element_type=F32)
                m_scr[c][...] = m_new

    def attend(fixed):
        for c in range(ATT_LOOKAHEAD[fixed]):
            s_scr[c][...] = scores(0, c)
        big = ATT_BODY_TILES[fixed]
        lax.fori_loop(0, i // big, lambda t, c: (steps(
            tuple(big * t + u for u in range(big)), False, big * t + big, fixed), c)[1], 0)
        size = big // 2
        while size >= 1:
            first = (i // (2 * size)) * (2 * size)

            @pl.when(i % (2 * size) >= size)
            def _(first=first, size=size):
                steps(tuple(first + u for u in range(size)), False, first + size, fixed)

            size //= 2
        steps((i,), True, None, fixed)

    @pl.when(bounded)
    def _():
        attend(True)

    @pl.when(jnp.logical_not(bounded))
    def _():
        attend(False)

    o = jnp.concatenate([acc_scr[c][:DIFF_DV] * (1.0 / acc_scr[c][DIFF_DV:DIFF_DV + 1])
                         for c in range(nblk)], axis=1)
    lam = (jnp.exp(jnp.sum(lq1_ref[...] * lk1_ref[...]))
           - jnp.exp(jnp.sum(lq2_ref[...] * lk2_ref[...])) + lambda_init)
    o = o[:, :tq] - lam * o[:, tq:]
    o = o * lax.rsqrt(jnp.mean(o * o, axis=0, keepdims=True) + EPS) * sg_ref[...]
    o_ref[...] = (o * (1.0 - lambda_init)).T.astype(BF16)


def _diffattn(qat, qbt, kr, vte, qn, kn, lq1, lk1, lq2, lk2, sg_col, lambda_init, tq, tk):
    s = kr.shape[0]
    hd = 2 * DIFF_DH
    vec = pl.BlockSpec((1, DIFF_DH), lambda h, i: (0, 0))
    cols = min(ATT_COLS, tq)
    nblk = 2 * tq // cols
    kern = functools.partial(_diffattn_kernel, lambda_init=lambda_init, tk=tk, cols=cols)
    dve = DIFF_DV + ATT_SUM_ROWS
    assert tq == tk and nblk >= max(ATT_LOOKAHEAD.values())
    scratch = ([pltpu.VMEM((1, cols), F32)] * nblk + [pltpu.VMEM((dve, cols), F32)] * nblk
               + [pltpu.VMEM((2, 1), F32)]
               + [pltpu.VMEM((tk, cols), F32)] * max(ATT_LOOKAHEAD.values()))
    return pl.pallas_call(
        kern,
        grid=(DIFF_HEADS, s // tq),
        in_specs=[pl.BlockSpec((hd, tq), lambda h, i: (h, i)),
                  pl.BlockSpec((hd, tq), lambda h, i: (h, i)),
                  pl.BlockSpec((s, hd), lambda h, i: (0, h)),
                  pl.BlockSpec((dve, s), lambda h, i: (h, 0)),
                  pl.BlockSpec((None, 2, tq), lambda h, i: (h, 0, i)),
                  pl.BlockSpec((None, 2, s), lambda h, i: (h, 0, 0)),
                  vec, vec, vec, vec,
                  pl.BlockSpec((DIFF_DV, 1), lambda h, i: (0, 0))],
        out_specs=pl.BlockSpec((tq, DIFF_DV), lambda h, i: (i, h)),
        out_shape=jax.ShapeDtypeStruct((s, DIFF_HEADS * DIFF_DV), BF16),
        scratch_shapes=scratch,
        compiler_params=_cparams(("arbitrary", "arbitrary")),
        name="diffattn",
    )(qat, qbt, kr, vte, qn, kn, lq1, lk1, lq2, lk2, sg_col)


def _mergeout_kernel(og_ref, od_ref, x_ref, g1_ref, sca_ref, sha_ref, wmg_ref, wmd_ref, wbg_ref,
                     wbd_ref, wo_ref, gt_ref, g2_ref, sc_ref, sh_ref, wrt_ref, bias_ref,
                     x1_ref, hfp_ref, idx_ref, wts_ref, rnk_ref, cnt_ref, run_scr, hf_scr):
    i = pl.program_id(0)

    @pl.when(i == 0)
    def _():
        hf_scr[...] = jnp.zeros_like(hf_scr)
        run_scr[...] = jnp.zeros_like(run_scr)

    logits = _nt_dot(wrt_ref[...], hf_scr[...], precision=HIGHEST)
    x = x_ref[...]
    h_in = _rms_mod(x, g1_ref[...], sca_ref[...], sha_ref[...]).astype(BF16)
    mg = jnp.dot(h_in, wmg_ref[...], preferred_element_type=F32)
    md = jnp.dot(h_in, wmd_ref[...], preferred_element_type=F32)
    bg = jnp.dot(og_ref[...], wbg_ref[...], preferred_element_type=F32)
    bd = jnp.dot(od_ref[...], wbd_ref[...], preferred_element_type=F32)
    merged = jax.nn.sigmoid(mg) * bg + jax.nn.sigmoid(md) * bd
    x1 = x + gt_ref[...] * jnp.dot(merged.astype(BF16), wo_ref[...],
                                   preferred_element_type=F32)
    x1_ref[...] = x1
    hf = _rms_mod(x1, g2_ref[...], sc_ref[...], sh_ref[...])
    hfp_ref[...] = _pack_halves(hf)
    hf_scr[...] = hf
    _route_select(logits, i > 0, bias_ref, idx_ref, wts_ref, rnk_ref, cnt_ref, run_scr)


def _mergeout(og, od, x, g1, sca, sha, wmg, wmd, wbg, wbd, wo, gt, g2, sc, sh, wrt, bias_col,
              tm):
    s, d = x.shape
    e = wrt.shape[0]
    vec = pl.BlockSpec((1, d), lambda i: (0, 0))
    wspec = pl.BlockSpec((d, d), lambda i: (0, 0))
    n = s // tm
    row = pl.BlockSpec((tm, d), lambda i: (jnp.minimum(i, n - 1), 0))
    tok = pl.BlockSpec((TOP_K, tm), lambda i: (0, jnp.maximum(i - 1, 0)))
    return pl.pallas_call(
        _mergeout_kernel,
        grid=(n + 1,),
        in_specs=[row, row, row, vec, vec, vec, wspec, wspec, wspec, wspec, wspec,
                  vec, vec, vec, vec,
                  pl.BlockSpec((e, d), lambda i: (0, 0)), pl.BlockSpec((e, 1), lambda i: (0, 0))],
        out_specs=[row, pl.BlockSpec((tm, d // 2), lambda i: (jnp.minimum(i, n - 1), 0)),
                   tok, tok, tok, pl.BlockSpec((e, 128), lambda i: (0, 0))],
        out_shape=[jax.ShapeDtypeStruct((s, d), F32),
                   jax.ShapeDtypeStruct((s, d // 2), jnp.uint32),
                   jax.ShapeDtypeStruct((TOP_K, s), jnp.int32),
                   jax.ShapeDtypeStruct((TOP_K, s), F32),
                   jax.ShapeDtypeStruct((TOP_K, s), jnp.int32),
                   jax.ShapeDtypeStruct((e, 128), jnp.int32)],
        scratch_shapes=[pltpu.VMEM((e, 128), F32), pltpu.VMEM((tm, d), F32)],
        compiler_params=_cparams(("arbitrary",)),
        name="mergeout",
    )(og, od, x, g1, sca, sha, wmg, wmd, wbg, wbd, wo, gt, g2, sc, sh, wrt, bias_col)


def _route_select(logits, live, bias_ref, idx_ref, wts_ref, rnk_ref, cnt_ref, run_scr):
    e, tr = logits.shape
    gsz = e // N_GROUPS
    scores = jax.nn.sigmoid(logits)
    biased = scores + bias_ref[...]
    g3 = biased.reshape(N_GROUPS, gsz, tr)
    m1 = jnp.max(g3, axis=1, keepdims=True)
    n_top = jnp.sum(jnp.where(g3 == m1, 1.0, 0.0), axis=1, keepdims=True)
    m2 = jnp.max(jnp.where(g3 < m1, g3, -jnp.inf), axis=1, keepdims=True)
    gs = (m1 + jnp.where(n_top >= 2.0, m1, m2)).reshape(N_GROUPS, tr)
    gi = lax.broadcasted_iota(jnp.int32, (N_GROUPS, tr), 0)
    beaten = jnp.zeros((N_GROUPS, tr), F32)
    for g in range(N_GROUPS):
        other = gs[g:g + 1, :]
        beaten = beaten + jnp.where((other > gs) | ((other == gs) & (g < gi)), 1.0, 0.0)
    gsel = (beaten < float(TOPK_GROUPS)).reshape(N_GROUPS, 1, tr)
    masked = jnp.where(gsel, g3, -jnp.inf).reshape(e, tr)

    ids = lax.broadcasted_iota(jnp.int32, (e, tr), 0)
    chosen = jnp.zeros((e, tr), F32)
    sel_idx, sel_score = [], []
    for _ in range(TOP_K):
        mx = jnp.max(masked, axis=0, keepdims=True)
        ix = jnp.min(jnp.where(masked == mx, ids, e), axis=0, keepdims=True)
        hit = ids == ix
        sel_idx.append(ix)
        sel_score.append(jnp.sum(jnp.where(hit, scores, 0.0), axis=0, keepdims=True))
        chosen = jnp.where(hit, 1.0, chosen)
        masked = jnp.where(hit, -jnp.inf, masked)
    idx = jnp.concatenate(sel_idx, axis=0)
    sc = jnp.concatenate(sel_score, axis=0)
    idx_ref[...] = idx
    wts_ref[...] = sc / jnp.sum(sc, axis=0, keepdims=True) * ROUTED_SCALE

    row = lax.broadcasted_iota(jnp.int32, (tr, tr), 0)
    col = lax.broadcasted_iota(jnp.int32, (tr, tr), 1)
    before = jnp.where(row < col, 1.0, 0.0).astype(BF16)
    prior = jnp.dot(chosen.astype(BF16), before, preferred_element_type=F32) + run_scr[:, 0:1]
    rnk_ref[...] = jnp.concatenate(
        [jnp.sum(jnp.where(ids == sel_idx[k], prior, 0.0), axis=0, keepdims=True)
         for k in range(TOP_K)], axis=0).astype(jnp.int32)
    run_scr[...] = run_scr[...] + jnp.where(live, jnp.sum(chosen, axis=1, keepdims=True), 0.0)
    cnt_ref[...] = run_scr[...].astype(jnp.int32)


def _positions_kernel(idx_ref, rnk_ref, pstart_ref, pos_ref):
    e = pstart_ref.shape[0]
    ts = idx_ref.shape[1]
    ids = lax.broadcasted_iota(jnp.int32, (e, ts), 0)
    idx = idx_ref[...]
    pos_ref[...] = rnk_ref[...] + jnp.concatenate(
        [jnp.sum(jnp.where(ids == idx[k:k + 1, :], pstart_ref[...], 0), axis=0, keepdims=True)
         for k in range(TOP_K)], axis=0)


def _positions(idx, rnk, pstart_col, ts):
    s = idx.shape[1]
    e = pstart_col.shape[0]
    tok = pl.BlockSpec((TOP_K, ts), lambda i: (0, i))
    return pl.pallas_call(
        _positions_kernel,
        grid=(s // ts,),
        in_specs=[tok, tok, pl.BlockSpec((e, 1), lambda i: (0, 0))],
        out_specs=tok,
        out_shape=jax.ShapeDtypeStruct((TOP_K, s), jnp.int32),
        compiler_params=_cparams(("arbitrary",)),
        name="positions",
    )(idx, rnk, pstart_col)


def _swiglu_packed(xp, wg, wu, wd):
    lo, hi = _unpack_halves(xp)
    lo, hi = lo.astype(BF16), hi.astype(BF16)
    n = lo.shape[1]
    g = (jnp.dot(lo, wg[:n], preferred_element_type=F32)
         + jnp.dot(hi, wg[n:], preferred_element_type=F32))
    u = (jnp.dot(lo, wu[:n], preferred_element_type=F32)
         + jnp.dot(hi, wu[n:], preferred_element_type=F32))
    h = (g * jax.nn.sigmoid(g)) * u
    return jnp.dot(h.astype(BF16), wd[...], preferred_element_type=F32)


def _moe_kernel(ie_ref, ib_ref, first_ref, slot_ref, ne_ref, lead_ref, rows_ref, nv_ref, xs_ref,
                wg_hbm, wu_hbm, wd_hbm, ys_ref, wg_f, wu_f, wd_f, sem):
    del ib_ref
    i = pl.program_id(0)

    def fetch(e, slot):
        copies = []
        for n, (src, dst) in enumerate(((wg_hbm, wg_f), (wu_hbm, wu_f), (wd_hbm, wd_f))):
            rows = src.shape[1] // MOE_DMA_CHUNKS
            for ch in range(MOE_DMA_CHUNKS):
                part = pl.ds(ch * rows, rows)
                copies.append(pltpu.make_async_copy(
                    src.at[e, part], dst.at[slot, part], sem.at[slot, n * MOE_DMA_CHUNKS + ch]))
        return copies

    @pl.when(i == 0)
    def _():
        for s in range(MOE_SLOTS - 1):
            @pl.when(lead_ref[s] >= 0)
            def _(s=s):
                for cp in fetch(lead_ref[s], s):
                    cp.start()

    @pl.when(i < nv_ref[0])
    def _():
        for slot in range(MOE_SLOTS):
            @pl.when((first_ref[i] == 1) & (slot_ref[i] == slot))
            def _(slot=slot):
                for cp in fetch(ie_ref[i], slot):
                    cp.wait()

                @pl.when(ne_ref[i] >= 0)
                def _():
                    for cp in fetch(ne_ref[i], (slot + MOE_SLOTS - 1) % MOE_SLOTS):
                        cp.start()

        slot = slot_ref[i]
        row = lax.broadcasted_iota(jnp.int32, xs_ref.shape, 0)
        xp = jnp.where(row < rows_ref[i], xs_ref[...], jnp.uint32(0))
        ys_ref[...] = _pack_halves(_swiglu_packed(
            xp, wg_f[slot].astype(BF16), wu_f[slot].astype(BF16), wd_f[slot].astype(BF16)))


def _moe(item_e, item_b, item_first, item_slot, item_next, lead, item_rows, n_valid, xs, wg, wu,
         wd):
    m_pad, dh = xs.shape
    _, d, f = wg.shape
    n_items = item_e.shape[0]
    blk = lambda i, ie, ib, fi, sl, ne, ld, nr, nv: (ib[i], 0)
    hbm = pl.BlockSpec(memory_space=pl.ANY)
    return pl.pallas_call(
        _moe_kernel,
        grid_spec=pltpu.PrefetchScalarGridSpec(
            num_scalar_prefetch=8,
            grid=(n_items,),
            in_specs=[pl.BlockSpec((MOE_ROWS, dh), blk), hbm, hbm, hbm],
            out_specs=pl.BlockSpec((MOE_ROWS, dh), blk),
            scratch_shapes=[pltpu.VMEM((MOE_SLOTS, d, f), F32), pltpu.VMEM((MOE_SLOTS, d, f), F32),
                            pltpu.VMEM((MOE_SLOTS, f, d), F32),
                            pltpu.SemaphoreType.DMA((MOE_SLOTS, 3 * MOE_DMA_CHUNKS))],
        ),
        out_shape=jax.ShapeDtypeStruct((m_pad, dh), jnp.uint32),
        compiler_params=_cparams(("arbitrary",)),
        name="moe",
    )(item_e, item_b, item_first, item_slot, item_next, lead, item_rows, n_valid, xs, wg, wu, wd)


def _sc_gather_rows(table, idx_row):
    m = idx_row.shape[1]
    w = table.shape[1]
    idx_row = idx_row.reshape(m // SC_GATHER_WINDOW, SC_GATHER_WINDOW)
    mesh = plsc.VectorSubcoreMesh(core_axis_name="c", subcore_axis_name="s")

    @functools.partial(pl.kernel, mesh=mesh,
                       out_type=jax.ShapeDtypeStruct((m, w), table.dtype))
    def gather(table_hbm, idx_hbm, out_hbm):
        def body(idx_vmem, out_vmem):
            pltpu.sync_copy(table_hbm.at[idx_vmem.at[0]], out_vmem)

        pltpu.emit_pipeline(
            body,
            grid=(m // SC_GATHER_WINDOW,),
            in_specs=[pl.BlockSpec((1, SC_GATHER_WINDOW), lambda i: (i, 0))],
            out_specs=[pl.BlockSpec((SC_GATHER_WINDOW, w), lambda i: (i, 0))],
            core_axis_name=("c", "s"),
            dimension_semantics=(pltpu.PARALLEL,),
        )(idx_hbm, out_hbm)

    return gather(table, idx_row)


def _sc_scatter_rows(rows, idx_blocks, m_out):
    s, w = rows.shape
    mesh = plsc.VectorSubcoreMesh(core_axis_name="c", subcore_axis_name="s")

    @functools.partial(pl.kernel, mesh=mesh,
                       out_type=jax.ShapeDtypeStruct((m_out, w), rows.dtype))
    def scatter(rows_hbm, idx_hbm, out_hbm):
        def body(rows_vmem, idx_vmem):
            for k in range(TOP_K):
                pltpu.sync_copy(rows_vmem, out_hbm.at[idx_vmem.at[k]])

        pltpu.emit_pipeline(
            body,
            grid=(s // SC_GATHER_WINDOW,),
            in_specs=[pl.BlockSpec((SC_GATHER_WINDOW, w), lambda i: (i, 0)),
                      pl.BlockSpec((TOP_K, SC_GATHER_WINDOW), lambda i: (i, 0))],
            out_specs=[],
            core_axis_name=("c", "s"),
            dimension_semantics=(pltpu.PARALLEL,),
        )(rows_hbm, idx_hbm)

    return scatter(rows, idx_blocks)


def _combine_kernel(wt_ref, hf_ref, x1_ref, gt_ref, sg_ref, su_ref, sd_ref, g_ref, o_ref):
    tc = x1_ref.shape[0]
    y = _swiglu_packed(hf_ref[...], sg_ref, su_ref, sd_ref)
    wt = wt_ref[...]
    n = g_ref.shape[2]
    r_lo = jnp.zeros((tc, n), F32)
    r_hi = jnp.zeros((tc, n), F32)
    for k in range(TOP_K):
        lo, hi = _unpack_halves(g_ref[k])
        r_lo = r_lo + lo * wt[:, k:k + 1]
        r_hi = r_hi + hi * wt[:, k:k + 1]
    y = y + jnp.concatenate([r_lo, r_hi], axis=1)
    o_ref[...] = x1_ref[...] + gt_ref[...] * y


def _combine(wts_t, hfp, x1, gt, sg, su, sd, gathered, tc):
    s, d = x1.shape
    f = sg.shape[1]
    row = pl.BlockSpec((tc, d), lambda i: (i, 0))
    return pl.pallas_call(
        _combine_kernel,
        grid=(s // tc,),
        in_specs=[pl.BlockSpec((tc, TOP_K), lambda i: (i, 0)),
                  pl.BlockSpec((tc, d // 2), lambda i: (i, 0)), row,
                  pl.BlockSpec((1, d), lambda i: (0, 0)),
                  pl.BlockSpec((d, f), lambda i: (0, 0)),
                  pl.BlockSpec((d, f), lambda i: (0, 0)),
                  pl.BlockSpec((f, d), lambda i: (0, 0)),
                  pl.BlockSpec((TOP_K, tc, d // 2), lambda i: (0, i, 0))],
        out_specs=row,
        out_shape=jax.ShapeDtypeStruct((s, d), F32),
        compiler_params=_cparams(("arbitrary",)),
        name="combine",
    )(wts_t, hfp, x1, gt, sg, su, sd, gathered)


def _tile(n, want):
    t = min(n, want)
    assert n % t == 0, (n, t)
    return t


def _layer(l, x, c_col, pos_row, p):
    s, d = x.shape
    lambda_init = 0.8 - 0.6 * math.exp(-0.3 * l)
    gqk, gv = GLA_HEADS * GLA_DK, GLA_HEADS * GLA_DV
    dqk, dvw = DIFF_HEADS * 2 * DIFF_DH, DIFF_HEADS * DIFF_DV
    lowrank = p["gla_w_a2"].shape[0]

    mod = _ada(c_col, p["w_ada"], p["b_ada"][None, :])
    sh_a, sc_a, gt_a, sh_f, sc_f, gt_f = [mod[:, j * d:(j + 1) * d] for j in range(6)]

    w_in = p["w_in"]
    o = 0
    cols = {}
    for name, wdt in (("gq", gqk), ("gk", gqk), ("gv", gv), ("ga", lowrank), ("gg", gv),
                      ("dq", dqk), ("dk", dqk), ("dv", dvw), ("mg", d), ("md", d)):
        cols[name] = w_in[:, o:o + wdt]
        o += wdt
    w_ga = jnp.pad(cols["ga"], ((0, 0), (0, 128 - lowrank))).astype(BF16)

    g1 = p["norm1_g"][None, :]
    ts = _tile(s, TILE_SEQ)

    invf = ROPE_THETA ** (-jnp.arange(0, ROT_DIM, 2, dtype=F32) / ROT_DIM)
    qat, qbt, kr, vte, qn, kn, gkt = _qkvprep(
        x, g1, sc_a, sh_a, *(cols[n].T.astype(BF16) for n in ("dq", "dk", "dv", "gk")),
        pos_row, invf[:, None], p["diff_qnorm_g"][:, None], p["diff_knorm_g"][:, None], ts)

    wa2t = jnp.pad(p["gla_w_a2"].T, ((0, 0), (0, 128 - lowrank)))
    o_gla = _gla(x, g1, sc_a, sh_a, cols["gq"].astype(BF16), cols["gv"].astype(BF16),
                 cols["gg"].astype(BF16), w_ga, gkt, wa2t, p["gla_b_a"][:, None],
                 p["gla_onorm_g"][None, :], ts)
    tq = ts
    o_diff = _diffattn(qat, qbt, kr, vte, qn, kn, p["diff_lq1"][None, :], p["diff_lk1"][None, :],
                       p["diff_lq2"][None, :], p["diff_lk2"][None, :],
                       p["diff_subln_g"][:, None], lambda_init, tq, tq)

    e = p["w_router"].shape[1]
    x1, hfp, idx, wts, rnk, cnt = _mergeout(
        o_gla, o_diff, x, g1, sc_a, sh_a, cols["mg"].astype(BF16), cols["md"].astype(BF16),
        p["w_branch_gla"].astype(BF16), p["w_branch_diff"].astype(BF16),
        p["w_out"].astype(BF16), gt_a, p["norm2_g"][None, :], sc_f, sh_f,
        p["w_router"].T, p["router_bias"][:, None], ts)

    counts = cnt[:, 0]
    pcounts = ((counts + MOE_ROWS - 1) // MOE_ROWS) * MOE_ROWS
    pend = jnp.cumsum(pcounts)
    pstart = pend - pcounts
    pos = _positions(idx, rnk, pstart[:, None], ts)
    n_items = (s * TOP_K) // MOE_ROWS + e
    n_valid = (pend[-1] // MOE_ROWS).astype(jnp.int32)
    item_b = jnp.minimum(jnp.arange(n_items, dtype=jnp.int32), n_valid - 1)
    item_e = jnp.minimum(jnp.sum(pend[None, :] <= (item_b * MOE_ROWS)[:, None], axis=1),
                         e - 1).astype(jnp.int32)

    wn = SC_GATHER_WINDOW
    pos_w = pos.reshape(TOP_K, s // wn, wn).transpose(1, 0, 2).reshape(s // wn * TOP_K, wn)
    xs = _sc_scatter_rows(hfp, pos_w, n_items * MOE_ROWS)
    item_rows = jnp.clip(pstart[item_e] + counts[item_e] - item_b * MOE_ROWS, 0,
                         MOE_ROWS).astype(jnp.int32)
    prev_e = jnp.concatenate([jnp.full((1,), -1, jnp.int32), item_e[:-1]])
    item_first = ((jnp.arange(n_items) < n_valid) & (item_e != prev_e)).astype(jnp.int32)
    item_slot = ((jnp.cumsum(item_first) - 1) % MOE_SLOTS).astype(jnp.int32)
    cand = jnp.where(pcounts > 0, jnp.arange(e, dtype=jnp.int32), e)
    nonempty_from = lax.cummin(cand[::-1])[::-1]
    following = jnp.concatenate([nonempty_from[1:], jnp.full((2,), e, jnp.int32)])
    ahead = item_e
    lead = [nonempty_from[0]]
    for _ in range(MOE_SLOTS - 1):
        ahead = following[ahead]
        lead.append(following[lead[-1]])
    item_next = jnp.where(ahead < e, ahead, -1).astype(jnp.int32)
    lead = jnp.stack(lead[:MOE_SLOTS - 1])
    lead = jnp.where(lead < e, lead, -1).astype(jnp.int32)
    ys = _moe(item_e, item_b, item_first, item_slot, item_next, lead, item_rows, n_valid[None],
              xs, p["w_exp_gate"], p["w_exp_up"], p["w_exp_down"])
    gathered = _sc_gather_rows(ys, pos.reshape(1, TOP_K * s)).reshape(TOP_K, s, d // 2)
    return _combine(wts.T, hfp, x1, gt_f, p["w_sh_gate"].astype(BF16),
                    p["w_sh_up"].astype(BF16), p["w_sh_down"].astype(BF16), gathered,
                    _tile(s, TILE_COMBINE))


_LAYER_PARAMS = ("w_ada", "b_ada", "norm1_g", "w_in", "gla_w_a2", "gla_b_a", "gla_onorm_g",
                 "diff_qnorm_g", "diff_knorm_g", "diff_lq1", "diff_lk1", "diff_lq2", "diff_lk2",
                 "diff_subln_g", "w_branch_gla", "w_branch_diff", "w_out", "norm2_g", "w_router",
                 "router_bias", "w_exp_gate", "w_exp_up", "w_exp_down", "w_sh_gate", "w_sh_up",
                 "w_sh_down")


def kernel(x, c, positions, w_ada, b_ada, norm1_g, w_in, gla_w_a2, gla_b_a, gla_onorm_g, diff_qnorm_g, diff_knorm_g, diff_lq1, diff_lk1, diff_lq2, diff_lk2, diff_subln_g, w_branch_gla, w_branch_diff, w_out, norm2_g, w_router, router_bias, w_exp_gate, w_exp_up, w_exp_down, w_sh_gate, w_sh_up, w_sh_down):
    stacked = dict(zip(_LAYER_PARAMS, (
        w_ada, b_ada, norm1_g, w_in, gla_w_a2, gla_b_a, gla_onorm_g, diff_qnorm_g, diff_knorm_g,
        diff_lq1, diff_lk1, diff_lq2, diff_lk2, diff_subln_g, w_branch_gla, w_branch_diff, w_out,
        norm2_g, w_router, router_bias, w_exp_gate, w_exp_up, w_exp_down, w_sh_gate, w_sh_up,
        w_sh_down)))
    b, s, d = x.shape
    assert b == 1, "single-sequence kernel"
    xl = x[0]
    c_col = c[0][:, None]
    pos_row = positions.astype(jnp.int32)
    for l in range(w_ada.shape[0]):
        xl = _layer(l, xl, c_col, pos_row, {k: v[l] for k, v in stacked.items()})
    return xl[None]
```

```python
import functools
import math

import jax
import jax.numpy as jnp
from jax import lax
from jax.experimental import pallas as pl
from jax.experimental.pallas import tpu as pltpu
from jax.experimental.pallas import tpu_sc as plsc

CHUNK = 64
EPS = 1e-6
GLA_HEADS = 4
GLA_DK = 128
GLA_DV = 256
GLA_TAU = 16.0
DIFF_HEADS = 8
DIFF_DH = 64
DIFF_DV = 2 * DIFF_DH
ROPE_THETA = 500000.0
ROT_DIM = DIFF_DH // 4
N_GROUPS = 8
TOPK_GROUPS = 4
TOP_K = 8
ROUTED_SCALE = 2.5

MOE_ROWS = 640
MOE_DMA_CHUNKS = 4
MOE_SLOTS = 3
SC_GATHER_WINDOW = 64
VMEM_LIMIT = 56 * 1024 * 1024
TILE_SEQ = 512
TILE_COMBINE = 512
NEG_BIG = -1e30
LOG2E = 1.4426950408889634
F32 = jnp.float32
BF16 = jnp.bfloat16


def _cparams(sem):
    return pltpu.CompilerParams(dimension_semantics=sem, vmem_limit_bytes=VMEM_LIMIT)


def _nt_dot(a, b):
    return lax.dot_general(a, b, (((1,), (1,)), ((), ())), preferred_element_type=F32)


def _pack_halves(x):
    n = x.shape[1] // 2
    lo = pltpu.bitcast(x[:, :n].astype(BF16).astype(F32), jnp.uint32) >> 16
    hi = pltpu.bitcast(x[:, n:].astype(BF16).astype(F32), jnp.uint32) & jnp.uint32(0xFFFF0000)
    return lo | hi


def _unpack_halves(w):
    return (pltpu.bitcast(w << 16, F32), pltpu.bitcast(w & jnp.uint32(0xFFFF0000), F32))


def _split_bf16(a):
    hi = a.astype(BF16)
    return hi, (a - hi.astype(F32)).astype(BF16)


def _rms_mod(x, g, sc, sh):
    xn = x * lax.rsqrt(jnp.mean(x * x, axis=-1, keepdims=True) + EPS)
    return (xn * g) * (1.0 + sc) + sh


def _ada_kernel(c_ref, w_ref, b_ref, o_ref):
    c = c_ref[...]
    ca = c * jax.nn.sigmoid(c)
    o_ref[...] = jnp.sum(ca * w_ref[...], axis=0, keepdims=True) + b_ref[...]


def _ada(c_col, w, b):
    d, n = w.shape
    tn = min(1024, n)
    return pl.pallas_call(
        _ada_kernel,
        grid=(n // tn,),
        in_specs=[pl.BlockSpec((d, 1), lambda j: (0, 0)),
                  pl.BlockSpec((d, tn), lambda j: (0, j)),
                  pl.BlockSpec((1, tn), lambda j: (0, j))],
        out_specs=pl.BlockSpec((1, tn), lambda j: (0, j)),
        out_shape=jax.ShapeDtypeStruct((1, n), F32),
        compiler_params=_cparams(("arbitrary",)),
        name="ada",
    )(c_col, w, b)


def _gla_kernel(x_ref, g1_ref, sc_ref, sh_ref, wq_ref, wv_ref, wg_ref, wga_ref, kt_ref, wa2t_ref,
                ba_ref, on_ref, o_ref, state_ref, o_scr, q_ref, v_ref, gg_ref):
    tt = x_ref.shape[0]
    nchunk = tt // CHUNK

    @pl.when(pl.program_id(0) == 0)
    def _():
        state_ref[...] = jnp.zeros_like(state_ref)

    h_in = _rms_mod(x_ref[...], g1_ref[...], sc_ref[...], sh_ref[...]).astype(BF16)
    q_ref[...] = jnp.dot(h_in, wq_ref[...], preferred_element_type=F32).astype(BF16)
    v_ref[...] = jnp.dot(h_in, wv_ref[...], preferred_element_type=F32).astype(BF16)
    gg_ref[...] = jnp.dot(h_in, wg_ref[...], preferred_element_type=F32).astype(BF16)
    ga = jnp.dot(h_in, wga_ref[...], preferred_element_type=F32)

    a_hi, a_lo = _split_bf16(wa2t_ref[...])
    g_hi, g_lo = _split_bf16(ga)
    zt = _nt_dot(a_hi, g_hi) + _nt_dot(a_hi, g_lo) + _nt_dot(a_lo, g_hi) + ba_ref[...]
    lat = (jnp.minimum(zt, 0.0) - jnp.log1p(jnp.exp(-jnp.abs(zt)))) * (1.0 / GLA_TAU)
    row = lax.broadcasted_iota(jnp.int32, (tt, tt), 0)
    col = lax.broadcasted_iota(jnp.int32, (tt, tt), 1)
    same = (row // CHUNK) == (col // CHUNK)
    incl = jnp.where(same & (row <= col), 1.0, 0.0).astype(BF16)
    full = jnp.where(same, 1.0, 0.0).astype(BF16)
    lat_hi, lat_lo = _split_bf16(lat)
    cumt = (jnp.dot(lat_hi, incl, preferred_element_type=F32)
            + jnp.dot(lat_lo, incl, preferred_element_type=F32))
    tott = (jnp.dot(lat_hi, full, preferred_element_type=F32)
            + jnp.dot(lat_lo, full, preferred_element_type=F32))
    kdt = kt_ref[...].astype(F32) * jnp.exp(tott - cumt)
    dec = jnp.exp(tott)

    lane = lax.broadcasted_iota(jnp.int32, (GLA_DK, 2 * CHUNK), 1)
    upd = {}
    for c in range(nchunk):
        pair = (c // 2) * 2 * CHUNK
        if nchunk > 1:
            keep = (lane // CHUNK) == (c % 2)
        for h in range(GLA_HEADS):
            rows = slice(h * GLA_DK, (h + 1) * GLA_DK)
            vcols = slice(h * GLA_DV, (h + 1) * GLA_DV)
            if nchunk > 1:
                a = jnp.where(keep, kdt[rows, pair:pair + 2 * CHUNK], 0.0).astype(BF16)
                vp = v_ref[pair:pair + 2 * CHUNK, vcols]
            else:
                a = kdt[rows, :].astype(BF16)
                vp = v_ref[:, vcols]
            upd[c, h] = jnp.dot(a, vp, preferred_element_type=F32)

    for h in range(GLA_HEADS):
        rows = slice(h * GLA_DK, (h + 1) * GLA_DK)
        vcols = slice(h * GLA_DV, (h + 1) * GLA_DV)
        st = state_ref[h]
        states = []
        for c in range(nchunk):
            st = st * dec[rows, c * CHUNK:c * CHUNK + 1] + upd[c, h]
            states.append(st.astype(BF16))
        state_ref[h] = st
        for c in range(nchunk):
            o_scr[c * CHUNK:(c + 1) * CHUNK, vcols] = jnp.dot(
                q_ref[c * CHUNK:(c + 1) * CHUNK, rows], states[c], preferred_element_type=F32)

    for h in range(GLA_HEADS):
        vcols = slice(h * GLA_DV, (h + 1) * GLA_DV)
        o = o_scr[:, vcols] * (GLA_DK ** -0.5)
        o = o * lax.rsqrt(jnp.mean(o * o, axis=-1, keepdims=True) + EPS) * on_ref[...]
        g = gg_ref[:, vcols].astype(F32)
        o_ref[:, vcols] = (o * (g * jax.nn.sigmoid(g))).astype(BF16)


def _gla(x, g1, sc, sh, wq, wv, wg, wga, gkt, wa2t, ba_col, on_g, tt):
    s, d = x.shape
    qk = GLA_HEADS * GLA_DK
    vw = GLA_HEADS * GLA_DV
    vec = pl.BlockSpec((1, d), lambda i: (0, 0))
    whole = lambda a: pl.BlockSpec(a.shape, lambda i: (0, 0))
    return pl.pallas_call(
        _gla_kernel,
        grid=(s // tt,),
        in_specs=[pl.BlockSpec((tt, d), lambda i: (i, 0)), vec, vec, vec,
                  whole(wq), whole(wv), whole(wg), whole(wga),
                  pl.BlockSpec((qk, tt), lambda i: (0, i)),
                  whole(wa2t), whole(ba_col), whole(on_g)],
        out_specs=pl.BlockSpec((tt, vw), lambda i: (i, 0)),
        out_shape=jax.ShapeDtypeStruct((s, vw), BF16),
        scratch_shapes=[pltpu.VMEM((GLA_HEADS, GLA_DK, GLA_DV), F32),
                        pltpu.VMEM((tt, vw), F32),
                        pltpu.VMEM((tt, qk), BF16), pltpu.VMEM((tt, vw), BF16),
                        pltpu.VMEM((tt, vw), BF16)],
        compiler_params=_cparams(("arbitrary",)),
        name="gla",
    )(x, g1, sc, sh, wq, wv, wg, wga, gkt, wa2t, ba_col, on_g)


def _qknorm_rope_t(xt, g_col, cos, sin):
    n, tm = xt.shape
    x3 = xt.reshape(n // DIFF_DH, DIFF_DH, tm)
    r = lax.rsqrt(jnp.mean(x3 * x3, axis=1, keepdims=True) + EPS)
    y = x3 * r * g_col[None]
    half = ROT_DIM // 2
    y1, y2, rest = y[:, :half], y[:, half:ROT_DIM], y[:, ROT_DIM:]
    o1 = y1 * cos[None] - y2 * sin[None]
    o2 = y2 * cos[None] + y1 * sin[None]
    return jnp.concatenate([o1, o2, rest], axis=1)


def _seg_norms(x3):
    return jnp.sqrt(jnp.sum(x3 * x3, axis=1)).reshape(DIFF_HEADS, 2, x3.shape[2])


def _qkvprep_kernel(x_ref, g_ref, sc_ref, sh_ref, wq_ref, wk_ref, wv_ref, wgk_ref, pos_ref,
                    invf_ref, qg_ref, kg_ref, qa_ref, qb_ref, ko_ref, ve_ref, qn_ref, kn_ref,
                    gk_ref):
    tm = x_ref.shape[0]
    h = _rms_mod(x_ref[...], g_ref[...], sc_ref[...], sh_ref[...]).astype(BF16)
    qt = _nt_dot(wq_ref[...], h)
    kt = _nt_dot(wk_ref[...], h)
    vt = _nt_dot(wv_ref[...], h)
    gk_ref[...] = _nt_dot(wgk_ref[...], h).astype(BF16)
    ang = pos_ref[...].astype(F32) * invf_ref[...]
    cos, sin = jnp.cos(ang), jnp.sin(ang)
    q3 = _qknorm_rope_t(qt, qg_ref[...], cos, sin) * (DIFF_DH ** -0.5 * LOG2E)
    qn_ref[...] = _seg_norms(q3)
    seg = lax.broadcasted_iota(jnp.int32, q3.shape, 0)
    qa_ref[...] = jnp.where(seg % 2 == 0, q3, 0.0).reshape(-1, tm).astype(BF16)
    qb_ref[...] = jnp.where(seg % 2 == 1, q3, 0.0).reshape(-1, tm).astype(BF16)
    k3 = _qknorm_rope_t(kt, kg_ref[...], cos, sin)
    ko_ref[...] = k3.reshape(-1, tm).T.astype(BF16)
    kn_ref[...] = _seg_norms(k3)
    v3 = vt.astype(BF16).reshape(DIFF_HEADS, DIFF_DV, tm)
    ones = jnp.ones((DIFF_HEADS, ATT_SUM_ROWS, tm), BF16)
    ve_ref[...] = jnp.concatenate([v3, ones], axis=1).reshape(-1, tm)


def _qkvprep(x, g, sc, sh, wq_t, wk_t, wv_t, wgk_t, pos_row, invf_col, qg_col, kg_col, tm):
    s, d = x.shape
    n = DIFF_HEADS * 2 * DIFF_DH
    ne = DIFF_HEADS * (DIFF_DV + ATT_SUM_ROWS)
    ngk = wgk_t.shape[0]
    col = pl.BlockSpec((DIFF_DH, 1), lambda i: (0, 0))
    vec = pl.BlockSpec((1, d), lambda i: (0, 0))
    wspec = pl.BlockSpec((n, d), lambda i: (0, 0))
    return pl.pallas_call(
        _qkvprep_kernel,
        grid=(s // tm,),
        in_specs=[pl.BlockSpec((tm, d), lambda i: (i, 0)), vec, vec, vec,
                  wspec, wspec, wspec, pl.BlockSpec((ngk, d), lambda i: (0, 0)),
                  pl.BlockSpec((1, tm), lambda i: (0, i)),
                  pl.BlockSpec((ROT_DIM // 2, 1), lambda i: (0, 0)), col, col],
        out_specs=[pl.BlockSpec((n, tm), lambda i: (0, i)),
                   pl.BlockSpec((n, tm), lambda i: (0, i)),
                   pl.BlockSpec((tm, n), lambda i: (i, 0)),
                   pl.BlockSpec((ne, tm), lambda i: (0, i)),
                   pl.BlockSpec((DIFF_HEADS, 2, tm), lambda i: (0, 0, i)),
                   pl.BlockSpec((DIFF_HEADS, 2, tm), lambda i: (0, 0, i)),
                   pl.BlockSpec((ngk, tm), lambda i: (0, i))],
        out_shape=[jax.ShapeDtypeStruct((n, s), BF16), jax.ShapeDtypeStruct((n, s), BF16),
                   jax.ShapeDtypeStruct((s, n), BF16), jax.ShapeDtypeStruct((ne, s), BF16),
                   jax.ShapeDtypeStruct((DIFF_HEADS, 2, s), F32),
                   jax.ShapeDtypeStruct((DIFF_HEADS, 2, s), F32),
                   jax.ShapeDtypeStruct((ngk, s), BF16)],
        compiler_params=_cparams(("arbitrary",)),
        name="qkvprep",
    )(x, g, sc, sh, wq_t, wk_t, wv_t, wgk_t, pos_row, invf_col, qg_col, kg_col)


ATT_COLS = 256
ATT_LOOKAHEAD = {True: 2, False: 4}
ATT_BODY_TILES = {True: 8, False: 2}
ATT_SUM_ROWS = 16
ATT_BOUND_SLACK = 1.02
ATT_BOUND_LIMIT = 50.0


def _diffattn_kernel(qa_ref, qb_ref, k_ref, vt_ref, qn_ref, kn_ref, lq1_ref, lk1_ref, lq2_ref,
                     lk2_ref, sg_ref, o_ref, *scr, lambda_init, tk, cols):
    tq = qa_ref.shape[1]
    nblk = 2 * tq // cols
    m_scr, acc_scr = (scr[b * nblk:(b + 1) * nblk] for b in range(2))
    kmax_scr = scr[2 * nblk]
    s_scr = scr[2 * nblk + 1:]
    i = pl.program_id(1)

    @pl.when(i == 0)
    def _():
        kmax_scr[...] = jnp.max(kn_ref[...], axis=1, keepdims=True)

    bound = qn_ref[...] * kmax_scr[...] * ATT_BOUND_SLACK
    bound = jnp.concatenate([bound[0:1], bound[1:2]], axis=1)
    bounded = jnp.max(bound) < ATT_BOUND_LIMIT
    for c in range(nblk):
        m_scr[c][...] = jnp.where(bounded, bound[:, c * cols:(c + 1) * cols], NEG_BIG)
        acc_scr[c][...] = jnp.zeros_like(acc_scr[c])

    def scores(j, c):
        start = pl.multiple_of(j * tk, tk)
        q_ref = qa_ref if c * cols < tq else qb_ref
        off = (c * cols) % tq
        return jnp.dot(k_ref[pl.ds(start, tk), :], q_ref[:, off:off + cols],
                       preferred_element_type=F32)

    def steps(tiles, masked, next_tile, fixed):
        look = ATT_LOOKAHEAD[fixed]
        items = [(j, c) for j in tiles for c in range(nblk)]
        pending = []
        for n, (j, c) in enumerate(items):
            s = s_scr[n][...] if n < look else pending.pop(0)
            ahead = n + look
            if ahead < len(items):
                pending.append(scores(*items[ahead]))
            elif next_tile is not None:
                s_scr[ahead - len(items)][...] = scores(next_tile, ahead - len(items))
            start = pl.multiple_of(j * tk, tk)
            keys = tk
            if masked:
                keys = (c * cols) % tq + cols
                s = s[:keys]
                krow = lax.broadcasted_iota(jnp.int32, (keys, cols), 0)
                qcol = lax.broadcasted_iota(jnp.int32, (keys, cols), 1)
                qpos = i * tq + (c * cols) % tq + qcol
                s = jnp.where((start + krow) // CHUNK <= qpos // CHUNK, s, NEG_BIG)
            vj = vt_ref[:, pl.ds(start, keys)]
            if fixed:
                p = jnp.exp2(s - m_scr[c][...])
                acc_scr[c][:DIFF_DV] += jnp.dot(vj[:DIFF_DV], p.astype(BF16),
                                                preferred_element_type=F32)
                acc_scr[c][DIFF_DV:DIFF_DV + 1] += jnp.sum(p, axis=0, keepdims=True)
            else:
                m_prev = m_scr[c][...]
                m_new = jnp.maximum(m_prev, jnp.max(s, axis=0, keepdims=True))
                alpha = jnp.exp2(m_prev - m_new)
                p = jnp.exp2((s - m_new).astype(BF16))
                acc_scr[c][...] = alpha * acc_scr[c][...] + jnp.dot(
                    vj, p, preferred_element_type=F32)
                m_scr[c][...] = m_new

    def attend(fixed):
        for c in range(ATT_LOOKAHEAD[fixed]):
            s_scr[c][...] = scores(0, c)
        big = ATT_BODY_TILES[fixed]
        lax.fori_loop(0, i // big, lambda t, c: (steps(
            tuple(big * t + u for u in range(big)), False, big * t + big, fixed), c)[1], 0)
        size = big // 2
        while size >= 1:
            first = (i // (2 * size)) * (2 * size)

            @pl.when(i % (2 * size) >= size)
            def _(first=first, size=size):
                steps(tuple(first + u for u in range(size)), False, first + size, fixed)

            size //= 2
        steps((i,), True, None, fixed)

    @pl.when(bounded)
    def _():
        attend(True)

    @pl.when(jnp.logical_not(bounded))
    def _():
        attend(False)

    o = jnp.concatenate([acc_scr[c][:DIFF_DV] * (1.0 / acc_scr[c][DIFF_DV:DIFF_DV + 1])
                         for c in range(nblk)], axis=1)
    lam = (jnp.exp(jnp.sum(lq1_ref[...] * lk1_ref[...]))
           - jnp.exp(jnp.sum(lq2_ref[...] * lk2_ref[...])) + lambda_init)
    o = o[:, :tq] - lam * o[:, tq:]
    o = o * lax.rsqrt(jnp.mean(o * o, axis=0, keepdims=True) + EPS) * sg_ref[...]
    o_ref[...] = (o * (1.0 - lambda_init)).T.astype(BF16)


def _diffattn(qat, qbt, kr, vte, qn, kn, lq1, lk1, lq2, lk2, sg_col, lambda_init, tq, tk):
    s = kr.shape[0]
    hd = 2 * DIFF_DH
    vec = pl.BlockSpec((1, DIFF_DH), lambda h, i: (0, 0))
    cols = min(ATT_COLS, tq)
    nblk = 2 * tq // cols
    kern = functools.partial(_diffattn_kernel, lambda_init=lambda_init, tk=tk, cols=cols)
    dve = DIFF_DV + ATT_SUM_ROWS
    assert tq == tk and nblk >= max(ATT_LOOKAHEAD.values())
    scratch = ([pltpu.VMEM((1, cols), F32)] * nblk + [pltpu.VMEM((dve, cols), F32)] * nblk
               + [pltpu.VMEM((2, 1), F32)]
               + [pltpu.VMEM((tk, cols), F32)] * max(ATT_LOOKAHEAD.values()))
    return pl.pallas_call(
        kern,
        grid=(DIFF_HEADS, s // tq),
        in_specs=[pl.BlockSpec((hd, tq), lambda h, i: (h, i)),
                  pl.BlockSpec((hd, tq), lambda h, i: (h, i)),
                  pl.BlockSpec((s, hd), lambda h, i: (0, h)),
                  pl.BlockSpec((dve, s), lambda h, i: (h, 0)),
                  pl.BlockSpec((None, 2, tq), lambda h, i: (h, 0, i)),
                  pl.BlockSpec((None, 2, s), lambda h, i: (h, 0, 0)),
                  vec, vec, vec, vec,
                  pl.BlockSpec((DIFF_DV, 1), lambda h, i: (0, 0))],
        out_specs=pl.BlockSpec((tq, DIFF_DV), lambda h, i: (i, h)),
        out_shape=jax.ShapeDtypeStruct((s, DIFF_HEADS * DIFF_DV), BF16),
        scratch_shapes=scratch,
        compiler_params=_cparams(("arbitrary", "arbitrary")),
        name="diffattn",
    )(qat, qbt, kr, vte, qn, kn, lq1, lk1, lq2, lk2, sg_col)


def _mergeout_kernel(og_ref, od_ref, x_ref, g1_ref, sca_ref, sha_ref, wmg_ref, wmd_ref, wbg_ref,
                     wbd_ref, wo_ref, gt_ref, g2_ref, sc_ref, sh_ref, wrt_ref, bias_ref,
                     x1_ref, hfp_ref, idx_ref, wts_ref, rnk_ref, cnt_ref, run_scr, hf_scr):
    i = pl.program_id(0)

    @pl.when(i == 0)
    def _():
        hf_scr[...] = jnp.zeros_like(hf_scr)
        run_scr[...] = jnp.zeros_like(run_scr)

    w_hi, w_lo = _split_bf16(wrt_ref[...])
    h_hi, h_lo = _split_bf16(hf_scr[...])
    logits = _nt_dot(w_hi, h_hi) + _nt_dot(w_hi, h_lo) + _nt_dot(w_lo, h_hi)
    x = x_ref[...]
    h_in = _rms_mod(x, g1_ref[...], sca_ref[...], sha_ref[...]).astype(BF16)
    mg = jnp.dot(h_in, wmg_ref[...], preferred_element_type=F32)
    md = jnp.dot(h_in, wmd_ref[...], preferred_element_type=F32)
    bg = jnp.dot(og_ref[...], wbg_ref[...], preferred_element_type=F32)
    bd = jnp.dot(od_ref[...], wbd_ref[...], preferred_element_type=F32)
    merged = jax.nn.sigmoid(mg) * bg + jax.nn.sigmoid(md) * bd
    x1 = x + gt_ref[...] * jnp.dot(merged.astype(BF16), wo_ref[...],
                                   preferred_element_type=F32)
    x1_ref[...] = x1
    hf = _rms_mod(x1, g2_ref[...], sc_ref[...], sh_ref[...])
    hfp_ref[...] = _pack_halves(hf)
    hf_scr[...] = hf
    _route_select(logits, i > 0, bias_ref, idx_ref, wts_ref, rnk_ref, cnt_ref, run_scr)


def _mergeout(og, od, x, g1, sca, sha, wmg, wmd, wbg, wbd, wo, gt, g2, sc, sh, wrt, bias_col,
              tm):
    s, d = x.shape
    e = wrt.shape[0]
    vec = pl.BlockSpec((1, d), lambda i: (0, 0))
    wspec = pl.BlockSpec((d, d), lambda i: (0, 0))
    n = s // tm
    row = pl.BlockSpec((tm, d), lambda i: (jnp.minimum(i, n - 1), 0))
    tok = pl.BlockSpec((TOP_K, tm), lambda i: (0, jnp.maximum(i - 1, 0)))
    return pl.pallas_call(
        _mergeout_kernel,
        grid=(n + 1,),
        in_specs=[row, row, row, vec, vec, vec, wspec, wspec, wspec, wspec, wspec,
                  vec, vec, vec, vec,
                  pl.BlockSpec((e, d), lambda i: (0, 0)), pl.BlockSpec((e, 1), lambda i: (0, 0))],
        out_specs=[row, pl.BlockSpec((tm, d // 2), lambda i: (jnp.minimum(i, n - 1), 0)),
                   tok, tok, tok, pl.BlockSpec((e, 128), lambda i: (0, 0))],
        out_shape=[jax.ShapeDtypeStruct((s, d), F32),
                   jax.ShapeDtypeStruct((s, d // 2), jnp.uint32),
                   jax.ShapeDtypeStruct((TOP_K, s), jnp.int32),
                   jax.ShapeDtypeStruct((TOP_K, s), F32),
                   jax.ShapeDtypeStruct((TOP_K, s), jnp.int32),
                   jax.ShapeDtypeStruct((e, 128), jnp.int32)],
        scratch_shapes=[pltpu.VMEM((e, 128), F32), pltpu.VMEM((tm, d), F32)],
        compiler_params=_cparams(("arbitrary",)),
        name="mergeout",
    )(og, od, x, g1, sca, sha, wmg, wmd, wbg, wbd, wo, gt, g2, sc, sh, wrt, bias_col)


def _route_select(logits, live, bias_ref, idx_ref, wts_ref, rnk_ref, cnt_ref, run_scr):
    e, tr = logits.shape
    gsz = e // N_GROUPS
    scores = jax.nn.sigmoid(logits)
    biased = scores + bias_ref[...]
    g3 = biased.reshape(N_GROUPS, gsz, tr)
    m1 = jnp.max(g3, axis=1, keepdims=True)
    n_top = jnp.sum(jnp.where(g3 == m1, 1.0, 0.0), axis=1, keepdims=True)
    m2 = jnp.max(jnp.where(g3 < m1, g3, -jnp.inf), axis=1, keepdims=True)
    gs = (m1 + jnp.where(n_top >= 2.0, m1, m2)).reshape(N_GROUPS, tr)
    gi = lax.broadcasted_iota(jnp.int32, (N_GROUPS, tr), 0)
    beaten = jnp.zeros((N_GROUPS, tr), F32)
    for g in range(N_GROUPS):
        other = gs[g:g + 1, :]
        beaten = beaten + jnp.where((other > gs) | ((other == gs) & (g < gi)), 1.0, 0.0)
    gsel = (beaten < float(TOPK_GROUPS)).reshape(N_GROUPS, 1, tr)
    masked = jnp.where(gsel, g3, -jnp.inf).reshape(e, tr)

    ids = lax.broadcasted_iota(jnp.int32, (e, tr), 0)
    chosen = jnp.zeros((e, tr), F32)
    sel_idx, sel_score = [], []
    for _ in range(TOP_K):
        mx = jnp.max(masked, axis=0, keepdims=True)
        ix = jnp.min(jnp.where(masked == mx, ids, e), axis=0, keepdims=True)
        hit = ids == ix
        sel_idx.append(ix)
        sel_score.append(jnp.sum(jnp.where(hit, scores, 0.0), axis=0, keepdims=True))
        chosen = jnp.where(hit, 1.0, chosen)
        masked = jnp.where(hit, -jnp.inf, masked)
    idx = jnp.concatenate(sel_idx, axis=0)
    sc = jnp.concatenate(sel_score, axis=0)
    idx_ref[...] = idx
    wts_ref[...] = sc / jnp.sum(sc, axis=0, keepdims=True) * ROUTED_SCALE

    row = lax.broadcasted_iota(jnp.int32, (tr, tr), 0)
    col = lax.broadcasted_iota(jnp.int32, (tr, tr), 1)
    before = jnp.where(row < col, 1.0, 0.0).astype(BF16)
    prior = jnp.dot(chosen.astype(BF16), before, preferred_element_type=F32) + run_scr[:, 0:1]
    rnk_ref[...] = jnp.concatenate(
        [jnp.sum(jnp.where(ids == sel_idx[k], prior, 0.0), axis=0, keepdims=True)
         for k in range(TOP_K)], axis=0).astype(jnp.int32)
    run_scr[...] = run_scr[...] + jnp.where(live, jnp.sum(chosen, axis=1, keepdims=True), 0.0)
    cnt_ref[...] = run_scr[...].astype(jnp.int32)


def _positions_kernel(idx_ref, rnk_ref, pstart_ref, pos_ref):
    e = pstart_ref.shape[0]
    ts = idx_ref.shape[1]
    ids = lax.broadcasted_iota(jnp.int32, (e, ts), 0)
    idx = idx_ref[...]
    pos_ref[...] = rnk_ref[...] + jnp.concatenate(
        [jnp.sum(jnp.where(ids == idx[k:k + 1, :], pstart_ref[...], 0), axis=0, keepdims=True)
         for k in range(TOP_K)], axis=0)


def _positions(idx, rnk, pstart_col, ts):
    s = idx.shape[1]
    e = pstart_col.shape[0]
    tok = pl.BlockSpec((TOP_K, ts), lambda i: (0, i))
    return pl.pallas_call(
        _positions_kernel,
        grid=(s // ts,),
        in_specs=[tok, tok, pl.BlockSpec((e, 1), lambda i: (0, 0))],
        out_specs=tok,
        out_shape=jax.ShapeDtypeStruct((TOP_K, s), jnp.int32),
        compiler_params=_cparams(("arbitrary",)),
        name="positions",
    )(idx, rnk, pstart_col)


def _swiglu_packed(xp, wg, wu, wd):
    lo, hi = _unpack_halves(xp)
    lo, hi = lo.astype(BF16), hi.astype(BF16)
    n = lo.shape[1]
    g = (jnp.dot(lo, wg[:n], preferred_element_type=F32)
         + jnp.dot(hi, wg[n:], preferred_element_type=F32))
    u = (jnp.dot(lo, wu[:n], preferred_element_type=F32)
         + jnp.dot(hi, wu[n:], preferred_element_type=F32))
    h = (g * jax.nn.sigmoid(g)) * u
    return jnp.dot(h.astype(BF16), wd[...], preferred_element_type=F32)


def _moe_kernel(ie_ref, ib_ref, first_ref, slot_ref, ne_ref, lead_ref, rows_ref, nv_ref, xs_ref,
                wg_hbm, wu_hbm, wd_hbm, ys_ref, wg_f, wu_f, wd_f, sem):
    del ib_ref
    i = pl.program_id(0)

    def fetch(e, slot):
        copies = []
        for n, (src, dst) in enumerate(((wg_hbm, wg_f), (wu_hbm, wu_f), (wd_hbm, wd_f))):
            rows = src.shape[1] // MOE_DMA_CHUNKS
            for ch in range(MOE_DMA_CHUNKS):
                part = pl.ds(ch * rows, rows)
                copies.append(pltpu.make_async_copy(
                    src.at[e, part], dst.at[slot, part], sem.at[slot, n * MOE_DMA_CHUNKS + ch]))
        return copies

    @pl.when(i == 0)
    def _():
        for s in range(MOE_SLOTS - 1):
            @pl.when(lead_ref[s] >= 0)
            def _(s=s):
                for cp in fetch(lead_ref[s], s):
                    cp.start()

    @pl.when(i < nv_ref[0])
    def _():
        for slot in range(MOE_SLOTS):
            @pl.when((first_ref[i] == 1) & (slot_ref[i] == slot))
            def _(slot=slot):
                for cp in fetch(ie_ref[i], slot):
                    cp.wait()

                @pl.when(ne_ref[i] >= 0)
                def _():
                    for cp in fetch(ne_ref[i], (slot + MOE_SLOTS - 1) % MOE_SLOTS):
                        cp.start()

        slot = slot_ref[i]
        row = lax.broadcasted_iota(jnp.int32, xs_ref.shape, 0)
        xp = jnp.where(row < rows_ref[i], xs_ref[...], jnp.uint32(0))
        ys_ref[...] = _pack_halves(_swiglu_packed(
            xp, wg_f[slot].astype(BF16), wu_f[slot].astype(BF16), wd_f[slot].astype(BF16)))


def _moe(item_e, item_b, item_first, item_slot, item_next, lead, item_rows, n_valid, xs, wg, wu,
         wd):
    m_pad, dh = xs.shape
    _, d, f = wg.shape
    n_items = item_e.shape[0]
    blk = lambda i, ie, ib, fi, sl, ne, ld, nr, nv: (ib[i], 0)
    hbm = pl.BlockSpec(memory_space=pl.ANY)
    return pl.pallas_call(
        _moe_kernel,
        grid_spec=pltpu.PrefetchScalarGridSpec(
            num_scalar_prefetch=8,
            grid=(n_items,),
            in_specs=[pl.BlockSpec((MOE_ROWS, dh), blk), hbm, hbm, hbm],
            out_specs=pl.BlockSpec((MOE_ROWS, dh), blk),
            scratch_shapes=[pltpu.VMEM((MOE_SLOTS, d, f), F32), pltpu.VMEM((MOE_SLOTS, d, f), F32),
                            pltpu.VMEM((MOE_SLOTS, f, d), F32),
                            pltpu.SemaphoreType.DMA((MOE_SLOTS, 3 * MOE_DMA_CHUNKS))],
        ),
        out_shape=jax.ShapeDtypeStruct((m_pad, dh), jnp.uint32),
        compiler_params=_cparams(("arbitrary",)),
        name="moe",
    )(item_e, item_b, item_first, item_slot, item_next, lead, item_rows, n_valid, xs, wg, wu, wd)


def _sc_gather_rows(table, idx_row):
    m = idx_row.shape[1]
    w = table.shape[1]
    idx_row = idx_row.reshape(m // SC_GATHER_WINDOW, SC_GATHER_WINDOW)
    mesh = plsc.VectorSubcoreMesh(core_axis_name="c", subcore_axis_name="s")

    @functools.partial(pl.kernel, mesh=mesh,
                       out_type=jax.ShapeDtypeStruct((m, w), table.dtype))
    def gather(table_hbm, idx_hbm, out_hbm):
        def body(idx_vmem, out_vmem):
            pltpu.sync_copy(table_hbm.at[idx_vmem.at[0]], out_vmem)

        pltpu.emit_pipeline(
            body,
            grid=(m // SC_GATHER_WINDOW,),
            in_specs=[pl.BlockSpec((1, SC_GATHER_WINDOW), lambda i: (i, 0))],
            out_specs=[pl.BlockSpec((SC_GATHER_WINDOW, w), lambda i: (i, 0))],
            core_axis_name=("c", "s"),
            dimension_semantics=(pltpu.PARALLEL,),
        )(idx_hbm, out_hbm)

    return gather(table, idx_row)


def _sc_scatter_rows(rows, idx_blocks, m_out):
    s, w = rows.shape
    mesh = plsc.VectorSubcoreMesh(core_axis_name="c", subcore_axis_name="s")

    @functools.partial(pl.kernel, mesh=mesh,
                       out_type=jax.ShapeDtypeStruct((m_out, w), rows.dtype))
    def scatter(rows_hbm, idx_hbm, out_hbm):
        def body(rows_vmem, idx_vmem):
            for k in range(TOP_K):
                pltpu.sync_copy(rows_vmem, out_hbm.at[idx_vmem.at[k]])

        pltpu.emit_pipeline(
            body,
            grid=(s // SC_GATHER_WINDOW,),
            in_specs=[pl.BlockSpec((SC_GATHER_WINDOW, w), lambda i: (i, 0)),
                      pl.BlockSpec((TOP_K, SC_GATHER_WINDOW), lambda i: (i, 0))],
            out_specs=[],
            core_axis_name=("c", "s"),
            dimension_semantics=(pltpu.PARALLEL,),
        )(rows_hbm, idx_hbm)

    return scatter(rows, idx_blocks)


def _combine_kernel(wt_ref, hf_ref, x1_ref, gt_ref, sg_ref, su_ref, sd_ref, g_ref, o_ref):
    tc = x1_ref.shape[0]
    y = _swiglu_packed(hf_ref[...], sg_ref, su_ref, sd_ref)
    wt = wt_ref[...]
    n = g_ref.shape[2]
    r_lo = jnp.zeros((tc, n), F32)
    r_hi = jnp.zeros((tc, n), F32)
    for k in range(TOP_K):
        lo, hi = _unpack_halves(g_ref[k])
        r_lo = r_lo + lo * wt[:, k:k + 1]
        r_hi = r_hi + hi * wt[:, k:k + 1]
    y = y + jnp.concatenate([r_lo, r_hi], axis=1)
    o_ref[...] = x1_ref[...] + gt_ref[...] * y


def _combine(wts_t, hfp, x1, gt, sg, su, sd, gathered, tc):
    s, d = x1.shape
    f = sg.shape[1]
    row = pl.BlockSpec((tc, d), lambda i: (i, 0))
    return pl.pallas_call(
        _combine_kernel,
        grid=(s // tc,),
        in_specs=[pl.BlockSpec((tc, TOP_K), lambda i: (i, 0)),
                  pl.BlockSpec((tc, d // 2), lambda i: (i, 0)), row,
                  pl.BlockSpec((1, d), lambda i: (0, 0)),
                  pl.BlockSpec((d, f), lambda i: (0, 0)),
                  pl.BlockSpec((d, f), lambda i: (0, 0)),
                  pl.BlockSpec((f, d), lambda i: (0, 0)),
                  pl.BlockSpec((TOP_K, tc, d // 2), lambda i: (0, i, 0))],
        out_specs=row,
        out_shape=jax.ShapeDtypeStruct((s, d), F32),
        compiler_params=_cparams(("arbitrary",)),
        name="combine",
    )(wts_t, hfp, x1, gt, sg, su, sd, gathered)


def _tile(n, want):
    t = min(n, want)
    assert n % t == 0, (n, t)
    return t


def _layer(l, x, c_col, pos_row, p):
    s, d = x.shape
    lambda_init = 0.8 - 0.6 * math.exp(-0.3 * l)
    gqk, gv = GLA_HEADS * GLA_DK, GLA_HEADS * GLA_DV
    dqk, dvw = DIFF_HEADS * 2 * DIFF_DH, DIFF_HEADS * DIFF_DV
    lowrank = p["gla_w_a2"].shape[0]

    mod = _ada(c_col, p["w_ada"], p["b_ada"][None, :])
    sh_a, sc_a, gt_a, sh_f, sc_f, gt_f = [mod[:, j * d:(j + 1) * d] for j in range(6)]

    w_in = p["w_in"]
    o = 0
    cols = {}
    for name, wdt in (("gq", gqk), ("gk", gqk), ("gv", gv), ("ga", lowrank), ("gg", gv),
                      ("dq", dqk), ("dk", dqk), ("dv", dvw), ("mg", d), ("md", d)):
        cols[name] = w_in[:, o:o + wdt]
        o += wdt
    w_ga = jnp.pad(cols["ga"], ((0, 0), (0, 128 - lowrank))).astype(BF16)

    g1 = p["norm1_g"][None, :]
    ts = _tile(s, TILE_SEQ)

    invf = ROPE_THETA ** (-jnp.arange(0, ROT_DIM, 2, dtype=F32) / ROT_DIM)
    qat, qbt, kr, vte, qn, kn, gkt = _qkvprep(
        x, g1, sc_a, sh_a, *(cols[n].T.astype(BF16) for n in ("dq", "dk", "dv", "gk")),
        pos_row, invf[:, None], p["diff_qnorm_g"][:, None], p["diff_knorm_g"][:, None], ts)

    wa2t = jnp.pad(p["gla_w_a2"].T, ((0, 0), (0, 128 - lowrank)))
    o_gla = _gla(x, g1, sc_a, sh_a, cols["gq"].astype(BF16), cols["gv"].astype(BF16),
                 cols["gg"].astype(BF16), w_ga, gkt, wa2t, p["gla_b_a"][:, None],
                 p["gla_onorm_g"][None, :], ts)
    tq = ts
    o_diff = _diffattn(qat, qbt, kr, vte, qn, kn, p["diff_lq1"][None, :], p["diff_lk1"][None, :],
                       p["diff_lq2"][None, :], p["diff_lk2"][None, :],
                       p["diff_subln_g"][:, None], lambda_init, tq, tq)

    e = p["w_router"].shape[1]
    x1, hfp, idx, wts, rnk, cnt = _mergeout(
        o_gla, o_diff, x, g1, sc_a, sh_a, cols["mg"].astype(BF16), cols["md"].astype(BF16),
        p["w_branch_gla"].astype(BF16), p["w_branch_diff"].astype(BF16),
        p["w_out"].astype(BF16), gt_a, p["norm2_g"][None, :], sc_f, sh_f,
        p["w_router"].T, p["router_bias"][:, None], ts)

    counts = cnt[:, 0]
    pcounts = ((counts + MOE_ROWS - 1) // MOE_ROWS) * MOE_ROWS
    pend = jnp.cumsum(pcounts)
    pstart = pend - pcounts
    pos = _positions(idx, rnk, pstart[:, None], ts)
    n_items = (s * TOP_K) // MOE_ROWS + e
    n_valid = (pend[-1] // MOE_ROWS).astype(jnp.int32)
    item_b = jnp.minimum(jnp.arange(n_items, dtype=jnp.int32), n_valid - 1)
    item_e = jnp.minimum(jnp.sum(pend[None, :] <= (item_b * MOE_ROWS)[:, None], axis=1),
                         e - 1).astype(jnp.int32)

    wn = SC_GATHER_WINDOW
    pos_w = pos.reshape(TOP_K, s // wn, wn).transpose(1, 0, 2).reshape(s // wn * TOP_K, wn)
    xs = _sc_scatter_rows(hfp, pos_w, n_items * MOE_ROWS)
    item_rows = jnp.clip(pstart[item_e] + counts[item_e] - item_b * MOE_ROWS, 0,
                         MOE_ROWS).astype(jnp.int32)
    prev_e = jnp.concatenate([jnp.full((1,), -1, jnp.int32), item_e[:-1]])
    item_first = ((jnp.arange(n_items) < n_valid) & (item_e != prev_e)).astype(jnp.int32)
    item_slot = ((jnp.cumsum(item_first) - 1) % MOE_SLOTS).astype(jnp.int32)
    cand = jnp.where(pcounts > 0, jnp.arange(e, dtype=jnp.int32), e)
    nonempty_from = lax.cummin(cand[::-1])[::-1]
    following = jnp.concatenate([nonempty_from[1:], jnp.full((2,), e, jnp.int32)])
    ahead = item_e
    lead = [nonempty_from[0]]
    for _ in range(MOE_SLOTS - 1):
        ahead = following[ahead]
        lead.append(following[lead[-1]])
    item_next = jnp.where(ahead < e, ahead, -1).astype(jnp.int32)
    lead = jnp.stack(lead[:MOE_SLOTS - 1])
    lead = jnp.where(lead < e, lead, -1).astype(jnp.int32)
    ys = _moe(item_e, item_b, item_first, item_slot, item_next, lead, item_rows, n_valid[None],
              xs, p["w_exp_gate"], p["w_exp_up"], p["w_exp_down"])
    gathered = _sc_gather_rows(ys, pos.reshape(1, TOP_K * s)).reshape(TOP_K, s, d // 2)
    return _combine(wts.T, hfp, x1, gt_f, p["w_sh_gate"].astype(BF16),
                    p["w_sh_up"].astype(BF16), p["w_sh_down"].astype(BF16), gathered,
                    _tile(s, TILE_COMBINE))


_LAYER_PARAMS = ("w_ada", "b_ada", "norm1_g", "w_in", "gla_w_a2", "gla_b_a", "gla_onorm_g",
                 "diff_qnorm_g", "diff_knorm_g", "diff_lq1", "diff_lk1", "diff_lq2", "diff_lk2",
                 "diff_subln_g", "w_branch_gla", "w_branch_diff", "w_out", "norm2_g", "w_router",
                 "router_bias", "w_exp_gate", "w_exp_up", "w_exp_down", "w_sh_gate", "w_sh_up",
                 "w_sh_down")


def kernel(x, c, positions, w_ada, b_ada, norm1_g, w_in, gla_w_a2, gla_b_a, gla_onorm_g, diff_qnorm_g, diff_knorm_g, diff_lq1, diff_lk1, diff_lq2, diff_lk2, diff_subln_g, w_branch_gla, w_branch_diff, w_out, norm2_g, w_router, router_bias, w_exp_gate, w_exp_up, w_exp_down, w_sh_gate, w_sh_up, w_sh_down):
    stacked = dict(zip(_LAYER_PARAMS, (
        w_ada, b_ada, norm1_g, w_in, gla_w_a2, gla_b_a, gla_onorm_g, diff_qnorm_g, diff_knorm_g,
        diff_lq1, diff_lk1, diff_lq2, diff_lk2, diff_subln_g, w_branch_gla, w_branch_diff, w_out,
        norm2_g, w_router, router_bias, w_exp_gate, w_exp_up, w_exp_down, w_sh_gate, w_sh_up,
        w_sh_down)))
    b, s, d = x.shape
    assert b == 1, "single-sequence kernel"
    xl = x[0]
    c_col = c[0][:, None]
    pos_row = positions.astype(jnp.int32)
    for l in range(w_ada.shape[0]):
        xl = _layer(l, xl, c_col, pos_row, {k: v[l] for k, v in stacked.items()})
    return xl[None]
```

```python
import functools
import math

import jax
import jax.numpy as jnp
from jax import lax
from jax.experimental import pallas as pl
from jax.experimental.pallas import tpu as pltpu
from jax.experimental.pallas import tpu_sc as plsc

CHUNK = 64
EPS = 1e-6
GLA_HEADS = 4
GLA_DK = 128
GLA_DV = 256
GLA_TAU = 16.0
DIFF_HEADS = 8
DIFF_DH = 64
DIFF_DV = 2 * DIFF_DH
ROPE_THETA = 500000.0
ROT_DIM = DIFF_DH // 4
N_GROUPS = 8
TOPK_GROUPS = 4
TOP_K = 8
ROUTED_SCALE = 2.5

MOE_ROWS = 640
MOE_DMA_CHUNKS = 4
MOE_SLOTS = 3
SC_GATHER_WINDOW = 64
VMEM_LIMIT = 56 * 1024 * 1024
TILE_SEQ = 512
TILE_ATT_Q = 1024
TILE_COMBINE = 512
NEG_BIG = -1e30
LOG2E = 1.4426950408889634
F32 = jnp.float32
BF16 = jnp.bfloat16


def _cparams(sem):
    return pltpu.CompilerParams(dimension_semantics=sem, vmem_limit_bytes=VMEM_LIMIT)


def _nt_dot(a, b):
    return lax.dot_general(a, b, (((1,), (1,)), ((), ())), preferred_element_type=F32)


def _pack_halves(x):
    n = x.shape[1] // 2
    lo = pltpu.bitcast(x[:, :n].astype(BF16).astype(F32), jnp.uint32) >> 16
    hi = pltpu.bitcast(x[:, n:].astype(BF16).astype(F32), jnp.uint32) & jnp.uint32(0xFFFF0000)
    return lo | hi


def _unpack_halves(w):
    return (pltpu.bitcast(w << 16, F32), pltpu.bitcast(w & jnp.uint32(0xFFFF0000), F32))


def _split_bf16(a):
    hi = a.astype(BF16)
    return hi, (a - hi.astype(F32)).astype(BF16)


def _rms_mod(x, g, sc, sh):
    xn = x * lax.rsqrt(jnp.mean(x * x, axis=-1, keepdims=True) + EPS)
    return (xn * g) * (1.0 + sc) + sh


def _ada_kernel(c_ref, w_ref, b_ref, o_ref):
    c = c_ref[...]
    ca = c * jax.nn.sigmoid(c)
    o_ref[...] = jnp.sum(ca * w_ref[...], axis=0, keepdims=True) + b_ref[...]


def _ada(c_col, w, b):
    d, n = w.shape
    tn = min(1024, n)
    return pl.pallas_call(
        _ada_kernel,
        grid=(n // tn,),
        in_specs=[pl.BlockSpec((d, 1), lambda j: (0, 0)),
                  pl.BlockSpec((d, tn), lambda j: (0, j)),
                  pl.BlockSpec((1, tn), lambda j: (0, j))],
        out_specs=pl.BlockSpec((1, tn), lambda j: (0, j)),
        out_shape=jax.ShapeDtypeStruct((1, n), F32),
        compiler_params=_cparams(("arbitrary",)),
        name="ada",
    )(c_col, w, b)


def _gla_kernel(x_ref, g1_ref, sc_ref, sh_ref, wq_ref, wv_ref, wg_ref, wga_ref, kt_ref, wa2t_ref,
                ba_ref, on_ref, o_ref, state_ref, o_scr, q_ref, v_ref, gg_ref):
    tt = x_ref.shape[0]
    nchunk = tt // CHUNK

    @pl.when(pl.program_id(0) == 0)
    def _():
        state_ref[...] = jnp.zeros_like(state_ref)

    h_in = _rms_mod(x_ref[...], g1_ref[...], sc_ref[...], sh_ref[...]).astype(BF16)
    q_ref[...] = jnp.dot(h_in, wq_ref[...], preferred_element_type=F32).astype(BF16)
    v_ref[...] = jnp.dot(h_in, wv_ref[...], preferred_element_type=F32).astype(BF16)
    gg_ref[...] = jnp.dot(h_in, wg_ref[...], preferred_element_type=F32).astype(BF16)
    ga = jnp.dot(h_in, wga_ref[...], preferred_element_type=F32)

    a_hi, a_lo = _split_bf16(wa2t_ref[...])
    g_hi, g_lo = _split_bf16(ga)
    zt = _nt_dot(a_hi, g_hi) + _nt_dot(a_hi, g_lo) + _nt_dot(a_lo, g_hi) + ba_ref[...]
    lat = (jnp.minimum(zt, 0.0) - jnp.log1p(jnp.exp(-jnp.abs(zt)))) * (1.0 / GLA_TAU)
    row = lax.broadcasted_iota(jnp.int32, (tt, tt), 0)
    col = lax.broadcasted_iota(jnp.int32, (tt, tt), 1)
    same = (row // CHUNK) == (col // CHUNK)
    incl = jnp.where(same & (row <= col), 1.0, 0.0).astype(BF16)
    full = jnp.where(same, 1.0, 0.0).astype(BF16)
    lat_hi, lat_lo = _split_bf16(lat)
    cumt = (jnp.dot(lat_hi, incl, preferred_element_type=F32)
            + jnp.dot(lat_lo, incl, preferred_element_type=F32))
    tott = (jnp.dot(lat_hi, full, preferred_element_type=F32)
            + jnp.dot(lat_lo, full, preferred_element_type=F32))
    kdt = kt_ref[...].astype(F32) * jnp.exp(tott - cumt)
    dec = jnp.exp(tott)

    lane = lax.broadcasted_iota(jnp.int32, (GLA_DK, 2 * CHUNK), 1)
    upd = {}
    for c in range(nchunk):
        pair = (c // 2) * 2 * CHUNK
        if nchunk > 1:
            keep = (lane // CHUNK) == (c % 2)
        for h in range(GLA_HEADS):
            rows = slice(h * GLA_DK, (h + 1) * GLA_DK)
            vcols = slice(h * GLA_DV, (h + 1) * GLA_DV)
            if nchunk > 1:
                a = jnp.where(keep, kdt[rows, pair:pair + 2 * CHUNK], 0.0).astype(BF16)
                vp = v_ref[pair:pair + 2 * CHUNK, vcols]
            else:
                a = kdt[rows, :].astype(BF16)
                vp = v_ref[:, vcols]
            upd[c, h] = jnp.dot(a, vp, preferred_element_type=F32)

    for h in range(GLA_HEADS):
        rows = slice(h * GLA_DK, (h + 1) * GLA_DK)
        vcols = slice(h * GLA_DV, (h + 1) * GLA_DV)
        st = state_ref[h]
        states = []
        for c in range(nchunk):
            st = st * dec[rows, c * CHUNK:c * CHUNK + 1] + upd[c, h]
            states.append(st.astype(BF16))
        state_ref[h] = st
        for c in range(nchunk):
            o_scr[c * CHUNK:(c + 1) * CHUNK, vcols] = jnp.dot(
                q_ref[c * CHUNK:(c + 1) * CHUNK, rows], states[c], preferred_element_type=F32)

    for h in range(GLA_HEADS):
        vcols = slice(h * GLA_DV, (h + 1) * GLA_DV)
        o = o_scr[:, vcols] * (GLA_DK ** -0.5)
        o = o * lax.rsqrt(jnp.mean(o * o, axis=-1, keepdims=True) + EPS) * on_ref[...]
        g = gg_ref[:, vcols].astype(F32)
        o_ref[:, vcols] = (o * (g * jax.nn.sigmoid(g))).astype(BF16)


def _gla(x, g1, sc, sh, wq, wv, wg, wga, gkt, wa2t, ba_col, on_g, tt):
    s, d = x.shape
    qk = GLA_HEADS * GLA_DK
    vw = GLA_HEADS * GLA_DV
    vec = pl.BlockSpec((1, d), lambda i: (0, 0))
    whole = lambda a: pl.BlockSpec(a.shape, lambda i: (0, 0))
    return pl.pallas_call(
        _gla_kernel,
        grid=(s // tt,),
        in_specs=[pl.BlockSpec((tt, d), lambda i: (i, 0)), vec, vec, vec,
                  whole(wq), whole(wv), whole(wg), whole(wga),
                  pl.BlockSpec((qk, tt), lambda i: (0, i)),
                  whole(wa2t), whole(ba_col), whole(on_g)],
        out_specs=pl.BlockSpec((tt, vw), lambda i: (i, 0)),
        out_shape=jax.ShapeDtypeStruct((s, vw), BF16),
        scratch_shapes=[pltpu.VMEM((GLA_HEADS, GLA_DK, GLA_DV), F32),
                        pltpu.VMEM((tt, vw), F32),
                        pltpu.VMEM((tt, qk), BF16), pltpu.VMEM((tt, vw), BF16),
                        pltpu.VMEM((tt, vw), BF16)],
        compiler_params=_cparams(("arbitrary",)),
        name="gla",
    )(x, g1, sc, sh, wq, wv, wg, wga, gkt, wa2t, ba_col, on_g)


def _qknorm_rope_t(xt, g_col, cos, sin):
    n, tm = xt.shape
    x3 = xt.reshape(n // DIFF_DH, DIFF_DH, tm)
    r = lax.rsqrt(jnp.mean(x3 * x3, axis=1, keepdims=True) + EPS)
    y = x3 * r * g_col[None]
    half = ROT_DIM // 2
    y1, y2, rest = y[:, :half], y[:, half:ROT_DIM], y[:, ROT_DIM:]
    o1 = y1 * cos[None] - y2 * sin[None]
    o2 = y2 * cos[None] + y1 * sin[None]
    return jnp.concatenate([o1, o2, rest], axis=1)


def _seg_norms(x3):
    return jnp.sqrt(jnp.sum(x3 * x3, axis=1)).reshape(DIFF_HEADS, 2, x3.shape[2])


def _qkvprep_kernel(x_ref, g_ref, sc_ref, sh_ref, wq_ref, wk_ref, wv_ref, wgk_ref, pos_ref,
                    invf_ref, qg_ref, kg_ref, qa_ref, qb_ref, ko_ref, ve_ref, qn_ref, kn_ref,
                    gk_ref):
    tm = x_ref.shape[0]
    h = _rms_mod(x_ref[...], g_ref[...], sc_ref[...], sh_ref[...]).astype(BF16)
    qt = _nt_dot(wq_ref[...], h)
    kt = _nt_dot(wk_ref[...], h)
    vt = _nt_dot(wv_ref[...], h)
    gk_ref[...] = _nt_dot(wgk_ref[...], h).astype(BF16)
    ang = pos_ref[...].astype(F32) * invf_ref[...]
    cos, sin = jnp.cos(ang), jnp.sin(ang)
    q3 = _qknorm_rope_t(qt, qg_ref[...], cos, sin) * (DIFF_DH ** -0.5 * LOG2E)
    qn_ref[...] = _seg_norms(q3)
    seg = lax.broadcasted_iota(jnp.int32, q3.shape, 0)
    qa_ref[...] = jnp.where(seg % 2 == 0, q3, 0.0).reshape(-1, tm).astype(BF16)
    qb_ref[...] = jnp.where(seg % 2 == 1, q3, 0.0).reshape(-1, tm).astype(BF16)
    k3 = _qknorm_rope_t(kt, kg_ref[...], cos, sin)
    ko_ref[...] = k3.reshape(-1, tm).T.astype(BF16)
    kn_ref[...] = _seg_norms(k3)
    v3 = vt.astype(BF16).reshape(DIFF_HEADS, DIFF_DV, tm)
    ones = jnp.ones((DIFF_HEADS, ATT_SUM_ROWS, tm), BF16)
    ve_ref[...] = jnp.concatenate([v3, ones], axis=1).reshape(-1, tm)


def _qkvprep(x, g, sc, sh, wq_t, wk_t, wv_t, wgk_t, pos_row, invf_col, qg_col, kg_col, tm):
    s, d = x.shape
    n = DIFF_HEADS * 2 * DIFF_DH
    ne = DIFF_HEADS * (DIFF_DV + ATT_SUM_ROWS)
    ngk = wgk_t.shape[0]
    col = pl.BlockSpec((DIFF_DH, 1), lambda i: (0, 0))
    vec = pl.BlockSpec((1, d), lambda i: (0, 0))
    wspec = pl.BlockSpec((n, d), lambda i: (0, 0))
    return pl.pallas_call(
        _qkvprep_kernel,
        grid=(s // tm,),
        in_specs=[pl.BlockSpec((tm, d), lambda i: (i, 0)), vec, vec, vec,
                  wspec, wspec, wspec, pl.BlockSpec((ngk, d), lambda i: (0, 0)),
                  pl.BlockSpec((1, tm), lambda i: (0, i)),
                  pl.BlockSpec((ROT_DIM // 2, 1), lambda i: (0, 0)), col, col],
        out_specs=[pl.BlockSpec((n, tm), lambda i: (0, i)),
                   pl.BlockSpec((n, tm), lambda i: (0, i)),
                   pl.BlockSpec((tm, n), lambda i: (i, 0)),
                   pl.BlockSpec((ne, tm), lambda i: (0, i)),
                   pl.BlockSpec((DIFF_HEADS, 2, tm), lambda i: (0, 0, i)),
                   pl.BlockSpec((DIFF_HEADS, 2, tm), lambda i: (0, 0, i)),
                   pl.BlockSpec((ngk, tm), lambda i: (0, i))],
        out_shape=[jax.ShapeDtypeStruct((n, s), BF16), jax.ShapeDtypeStruct((n, s), BF16),
                   jax.ShapeDtypeStruct((s, n), BF16), jax.ShapeDtypeStruct((ne, s), BF16),
                   jax.ShapeDtypeStruct((DIFF_HEADS, 2, s), F32),
                   jax.ShapeDtypeStruct((DIFF_HEADS, 2, s), F32),
                   jax.ShapeDtypeStruct((ngk, s), BF16)],
        compiler_params=_cparams(("arbitrary",)),
        name="qkvprep",
    )(x, g, sc, sh, wq_t, wk_t, wv_t, wgk_t, pos_row, invf_col, qg_col, kg_col)


ATT_COLS = 256
ATT_LOOKAHEAD = {True: 2, False: 4}
ATT_BODY_ITEMS = {True: 32, False: 8}
ATT_SUM_ROWS = 16
ATT_BOUND_SLACK = 1.02
ATT_BOUND_LIMIT = 50.0


def _diffattn_kernel(qa_ref, qb_ref, k_ref, vt_ref, qn_ref, kn_ref, lq1_ref, lk1_ref, lq2_ref,
                     lk2_ref, sg_ref, o_ref, *scr, lambda_init, tk, cols):
    tq = qa_ref.shape[1]
    nblk = 2 * tq // cols
    m_scr, acc_scr = (scr[b * nblk:(b + 1) * nblk] for b in range(2))
    kmax_scr = scr[2 * nblk]
    s_scr = scr[2 * nblk + 1:]
    i = pl.program_id(1)

    @pl.when(i == 0)
    def _():
        kmax_scr[...] = jnp.max(kn_ref[...], axis=1, keepdims=True)

    bound = qn_ref[...] * kmax_scr[...] * ATT_BOUND_SLACK
    bound = jnp.concatenate([bound[0:1], bound[1:2]], axis=1)
    bounded = jnp.max(bound) < ATT_BOUND_LIMIT
    for c in range(nblk):
        m_scr[c][...] = jnp.where(bounded, bound[:, c * cols:(c + 1) * cols], NEG_BIG)
        acc_scr[c][...] = jnp.zeros_like(acc_scr[c])

    def scores(j, c):
        start = pl.multiple_of(j * tk, tk)
        q_ref = qa_ref if c * cols < tq else qb_ref
        off = (c * cols) % tq
        return jnp.dot(k_ref[pl.ds(start, tk), :], q_ref[:, off:off + cols],
                       preferred_element_type=F32)

    def steps(items, next_tile, fixed):
        look = ATT_LOOKAHEAD[fixed]
        pending = []
        for n, (j, c, keys) in enumerate(items):
            s = s_scr[n][...] if n < look else pending.pop(0)
            ahead = n + look
            if ahead < len(items):
                pending.append(scores(*items[ahead][:2]))
            elif next_tile is not None:
                s_scr[ahead - len(items)][...] = scores(next_tile, ahead - len(items))
            start = pl.multiple_of(j * tk, tk)
            if keys is None:
                keys = tk
            else:
                s = s[:keys]
                krow = lax.broadcasted_iota(jnp.int32, (keys, cols), 0)
                qcol = lax.broadcasted_iota(jnp.int32, (keys, cols), 1)
                qpos = i * tq + (c * cols) % tq + qcol
                s = jnp.where((start + krow) // CHUNK <= qpos // CHUNK, s, NEG_BIG)
            vj = vt_ref[:, pl.ds(start, keys)]
            if fixed:
                p = jnp.exp2(s - m_scr[c][...])
                acc_scr[c][:DIFF_DV] += jnp.dot(vj[:DIFF_DV], p.astype(BF16),
                                                preferred_element_type=F32)
                acc_scr[c][DIFF_DV:DIFF_DV + 1] += jnp.sum(p, axis=0, keepdims=True)
            else:
                m_prev = m_scr[c][...]
                m_new = jnp.maximum(m_prev, jnp.max(s, axis=0, keepdims=True))
                alpha = jnp.exp2(m_prev - m_new)
                p = jnp.exp2((s - m_new).astype(BF16))
                acc_scr[c][...] = alpha * acc_scr[c][...] + jnp.dot(
                    vj, p, preferred_element_type=F32)
                m_scr[c][...] = m_new

    def past(first, count):
        return [(first + u, c, None) for u in range(count) for c in range(nblk)]

    def attend(fixed):
        n_past = i * (tq // tk)
        for c in range(ATT_LOOKAHEAD[fixed]):
            s_scr[c][...] = scores(0, c)
        big = ATT_BODY_ITEMS[fixed] // nblk
        lax.fori_loop(0, n_past // big, lambda t, c: (steps(
            past(big * t, big), big * t + big, fixed), c)[1], 0)
        size = big // 2
        while size >= 1:
            first = (n_past // (2 * size)) * (2 * size)

            @pl.when(n_past % (2 * size) >= size)
            def _(first=first, size=size):
                steps(past(first, size), first + size, fixed)

            size //= 2
        diagonal = []
        for d in range(tq // tk):
            for c in range(nblk):
                off = (c * cols) % tq
                visible = off + cols - d * tk
                if visible > 0:
                    diagonal.append((n_past + d, c, None if off >= (d + 1) * tk
                                     else min(tk, visible)))
        steps(diagonal, None, fixed)

    @pl.when(bounded)
    def _():
        attend(True)

    @pl.when(jnp.logical_not(bounded))
    def _():
        attend(False)

    o = jnp.concatenate([acc_scr[c][:DIFF_DV] * (1.0 / acc_scr[c][DIFF_DV:DIFF_DV + 1])
                         for c in range(nblk)], axis=1)
    lam = (jnp.exp(jnp.sum(lq1_ref[...] * lk1_ref[...]))
           - jnp.exp(jnp.sum(lq2_ref[...] * lk2_ref[...])) + lambda_init)
    o = o[:, :tq] - lam * o[:, tq:]
    o = o * lax.rsqrt(jnp.mean(o * o, axis=0, keepdims=True) + EPS) * sg_ref[...]
    o_ref[...] = (o * (1.0 - lambda_init)).T.astype(BF16)


def _diffattn(qat, qbt, kr, vte, qn, kn, lq1, lk1, lq2, lk2, sg_col, lambda_init, tq, tk):
    s = kr.shape[0]
    hd = 2 * DIFF_DH
    vec = pl.BlockSpec((1, DIFF_DH), lambda h, i: (0, 0))
    cols = min(ATT_COLS, tq)
    nblk = 2 * tq // cols
    kern = functools.partial(_diffattn_kernel, lambda_init=lambda_init, tk=tk, cols=cols)
    dve = DIFF_DV + ATT_SUM_ROWS
    assert tq % tk == 0 and nblk >= max(ATT_LOOKAHEAD.values())
    scratch = ([pltpu.VMEM((1, cols), F32)] * nblk + [pltpu.VMEM((dve, cols), F32)] * nblk
               + [pltpu.VMEM((2, 1), F32)]
               + [pltpu.VMEM((tk, cols), F32)] * max(ATT_LOOKAHEAD.values()))
    return pl.pallas_call(
        kern,
        grid=(DIFF_HEADS, s // tq),
        in_specs=[pl.BlockSpec((hd, tq), lambda h, i: (h, i)),
                  pl.BlockSpec((hd, tq), lambda h, i: (h, i)),
                  pl.BlockSpec((s, hd), lambda h, i: (0, h)),
                  pl.BlockSpec((dve, s), lambda h, i: (h, 0)),
                  pl.BlockSpec((None, 2, tq), lambda h, i: (h, 0, i)),
                  pl.BlockSpec((None, 2, s), lambda h, i: (h, 0, 0)),
                  vec, vec, vec, vec,
                  pl.BlockSpec((DIFF_DV, 1), lambda h, i: (0, 0))],
        out_specs=pl.BlockSpec((tq, DIFF_DV), lambda h, i: (i, h)),
        out_shape=jax.ShapeDtypeStruct((s, DIFF_HEADS * DIFF_DV), BF16),
        scratch_shapes=scratch,
        compiler_params=_cparams(("arbitrary", "arbitrary")),
        name="diffattn",
    )(qat, qbt, kr, vte, qn, kn, lq1, lk1, lq2, lk2, sg_col)


def _mergeout_kernel(og_ref, od_ref, x_ref, g1_ref, sca_ref, sha_ref, wmg_ref, wmd_ref, wbg_ref,
                     wbd_ref, wo_ref, gt_ref, g2_ref, sc_ref, sh_ref, wrt_ref, bias_ref,
                     x1_ref, hfp_ref, idx_ref, wts_ref, rnk_ref, cnt_ref, run_scr, hf_scr):
    i = pl.program_id(0)

    @pl.when(i == 0)
    def _():
        hf_scr[...] = jnp.zeros_like(hf_scr)
        run_scr[...] = jnp.zeros_like(run_scr)

    w_hi, w_lo = _split_bf16(wrt_ref[...])
    h_hi, h_lo = _split_bf16(hf_scr[...])
    logits = _nt_dot(w_hi, h_hi) + _nt_dot(w_hi, h_lo) + _nt_dot(w_lo, h_hi)
    x = x_ref[...]
    h_in = _rms_mod(x, g1_ref[...], sca_ref[...], sha_ref[...]).astype(BF16)
    mg = jnp.dot(h_in, wmg_ref[...], preferred_element_type=F32)
    md = jnp.dot(h_in, wmd_ref[...], preferred_element_type=F32)
    bg = jnp.dot(og_ref[...], wbg_ref[...], preferred_element_type=F32)
    bd = jnp.dot(od_ref[...], wbd_ref[...], preferred_element_type=F32)
    merged = jax.nn.sigmoid(mg) * bg + jax.nn.sigmoid(md) * bd
    x1 = x + gt_ref[...] * jnp.dot(merged.astype(BF16), wo_ref[...],
                                   preferred_element_type=F32)
    x1_ref[...] = x1
    hf = _rms_mod(x1, g2_ref[...], sc_ref[...], sh_ref[...])
    hfp_ref[...] = _pack_halves(hf)
    hf_scr[...] = hf
    _route_select(logits, i > 0, bias_ref, idx_ref, wts_ref, rnk_ref, cnt_ref, run_scr)


def _mergeout(og, od, x, g1, sca, sha, wmg, wmd, wbg, wbd, wo, gt, g2, sc, sh, wrt, bias_col,
              tm):
    s, d = x.shape
    e = wrt.shape[0]
    vec = pl.BlockSpec((1, d), lambda i: (0, 0))
    wspec = pl.BlockSpec((d, d), lambda i: (0, 0))
    n = s // tm
    row = pl.BlockSpec((tm, d), lambda i: (jnp.minimum(i, n - 1), 0))
    tok = pl.BlockSpec((TOP_K, tm), lambda i: (0, jnp.maximum(i - 1, 0)))
    return pl.pallas_call(
        _mergeout_kernel,
        grid=(n + 1,),
        in_specs=[row, row, row, vec, vec, vec, wspec, wspec, wspec, wspec, wspec,
                  vec, vec, vec, vec,
                  pl.BlockSpec((e, d), lambda i: (0, 0)), pl.BlockSpec((e, 1), lambda i: (0, 0))],
        out_specs=[row, pl.BlockSpec((tm, d // 2), lambda i: (jnp.minimum(i, n - 1), 0)),
                   tok, tok, tok, pl.BlockSpec((e, 128), lambda i: (0, 0))],
        out_shape=[jax.ShapeDtypeStruct((s, d), F32),
                   jax.ShapeDtypeStruct((s, d // 2), jnp.uint32),
                   jax.ShapeDtypeStruct((TOP_K, s), jnp.int32),
                   jax.ShapeDtypeStruct((TOP_K, s), F32),
                   jax.ShapeDtypeStruct((TOP_K, s), jnp.int32),
                   jax.ShapeDtypeStruct((e, 128), jnp.int32)],
        scratch_shapes=[pltpu.VMEM((e, 128), F32), pltpu.VMEM((tm, d), F32)],
        compiler_params=_cparams(("arbitrary",)),
        name="mergeout",
    )(og, od, x, g1, sca, sha, wmg, wmd, wbg, wbd, wo, gt, g2, sc, sh, wrt, bias_col)


def _route_select(logits, live, bias_ref, idx_ref, wts_ref, rnk_ref, cnt_ref, run_scr):
    e, tr = logits.shape
    gsz = e // N_GROUPS
    scores = jax.nn.sigmoid(logits)
    biased = scores + bias_ref[...]
    g3 = biased.reshape(N_GROUPS, gsz, tr)
    m1 = jnp.max(g3, axis=1, keepdims=True)
    n_top = jnp.sum(jnp.where(g3 == m1, 1.0, 0.0), axis=1, keepdims=True)
    m2 = jnp.max(jnp.where(g3 < m1, g3, -jnp.inf), axis=1, keepdims=True)
    gs = (m1 + jnp.where(n_top >= 2.0, m1, m2)).reshape(N_GROUPS, tr)
    gi = lax.broadcasted_iota(jnp.int32, (N_GROUPS, tr), 0)
    beaten = jnp.zeros((N_GROUPS, tr), F32)
    for g in range(N_GROUPS):
        other = gs[g:g + 1, :]
        beaten = beaten + jnp.where((other > gs) | ((other == gs) & (g < gi)), 1.0, 0.0)
    gsel = (beaten < float(TOPK_GROUPS)).reshape(N_GROUPS, 1, tr)
    masked = jnp.where(gsel, g3, -jnp.inf).reshape(e, tr)

    ids = lax.broadcasted_iota(jnp.int32, (e, tr), 0)
    chosen = jnp.zeros((e, tr), F32)
    sel_idx, sel_score = [], []
    for _ in range(TOP_K):
        mx = jnp.max(masked, axis=0, keepdims=True)
        ix = jnp.min(jnp.where(masked == mx, ids, e), axis=0, keepdims=True)
        hit = ids == ix
        sel_idx.append(ix)
        sel_score.append(jnp.sum(jnp.where(hit, scores, 0.0), axis=0, keepdims=True))
        chosen = jnp.where(hit, 1.0, chosen)
        masked = jnp.where(hit, -jnp.inf, masked)
    idx = jnp.concatenate(sel_idx, axis=0)
    sc = jnp.concatenate(sel_score, axis=0)
    idx_ref[...] = idx
    wts_ref[...] = sc / jnp.sum(sc, axis=0, keepdims=True) * ROUTED_SCALE

    row = lax.broadcasted_iota(jnp.int32, (tr, tr), 0)
    col = lax.broadcasted_iota(jnp.int32, (tr, tr), 1)
    before = jnp.where(row < col, 1.0, 0.0).astype(BF16)
    prior = jnp.dot(chosen.astype(BF16), before, preferred_element_type=F32) + run_scr[:, 0:1]
    rnk_ref[...] = jnp.concatenate(
        [jnp.sum(jnp.where(ids == sel_idx[k], prior, 0.0), axis=0, keepdims=True)
         for k in range(TOP_K)], axis=0).astype(jnp.int32)
    run_scr[...] = run_scr[...] + jnp.where(live, jnp.sum(chosen, axis=1, keepdims=True), 0.0)
    cnt_ref[...] = run_scr[...].astype(jnp.int32)


def _positions_kernel(idx_ref, rnk_ref, pstart_ref, pos_ref):
    e = pstart_ref.shape[0]
    ts = idx_ref.shape[1]
    ids = lax.broadcasted_iota(jnp.int32, (e, ts), 0)
    idx = idx_ref[...]
    pos_ref[...] = rnk_ref[...] + jnp.concatenate(
        [jnp.sum(jnp.where(ids == idx[k:k + 1, :], pstart_ref[...], 0), axis=0, keepdims=True)
         for k in range(TOP_K)], axis=0)


def _positions(idx, rnk, pstart_col, ts):
    s = idx.shape[1]
    e = pstart_col.shape[0]
    tok = pl.BlockSpec((TOP_K, ts), lambda i: (0, i))
    return pl.pallas_call(
        _positions_kernel,
        grid=(s // ts,),
        in_specs=[tok, tok, pl.BlockSpec((e, 1), lambda i: (0, 0))],
        out_specs=tok,
        out_shape=jax.ShapeDtypeStruct((TOP_K, s), jnp.int32),
        compiler_params=_cparams(("arbitrary",)),
        name="positions",
    )(idx, rnk, pstart_col)


def _swiglu_packed(xp, wg, wu, wd):
    lo, hi = _unpack_halves(xp)
    lo, hi = lo.astype(BF16), hi.astype(BF16)
    n = lo.shape[1]
    g = (jnp.dot(lo, wg[:n], preferred_element_type=F32)
         + jnp.dot(hi, wg[n:], preferred_element_type=F32))
    u = (jnp.dot(lo, wu[:n], preferred_element_type=F32)
         + jnp.dot(hi, wu[n:], preferred_element_type=F32))
    h = (g * jax.nn.sigmoid(g)) * u
    return jnp.dot(h.astype(BF16), wd[...], preferred_element_type=F32)


def _moe_kernel(ie_ref, ib_ref, first_ref, slot_ref, ne_ref, lead_ref, rows_ref, nv_ref, xs_ref,
                wg_hbm, wu_hbm, wd_hbm, ys_ref, wg_f, wu_f, wd_f, sem):
    del ib_ref
    i = pl.program_id(0)

    def fetch(e, slot):
        copies = []
        for n, (src, dst) in enumerate(((wg_hbm, wg_f), (wu_hbm, wu_f), (wd_hbm, wd_f))):
            rows = src.shape[1] // MOE_DMA_CHUNKS
            for ch in range(MOE_DMA_CHUNKS):
                part = pl.ds(ch * rows, rows)
                copies.append(pltpu.make_async_copy(
                    src.at[e, part], dst.at[slot, part], sem.at[slot, n * MOE_DMA_CHUNKS + ch]))
        return copies

    @pl.when(i == 0)
    def _():
        for s in range(MOE_SLOTS - 1):
            @pl.when(lead_ref[s] >= 0)
            def _(s=s):
                for cp in fetch(lead_ref[s], s):
                    cp.start()

    @pl.when(i < nv_ref[0])
    def _():
        for slot in range(MOE_SLOTS):
            @pl.when((first_ref[i] == 1) & (slot_ref[i] == slot))
            def _(slot=slot):
                for cp in fetch(ie_ref[i], slot):
                    cp.wait()

                @pl.when(ne_ref[i] >= 0)
                def _():
                    for cp in fetch(ne_ref[i], (slot + MOE_SLOTS - 1) % MOE_SLOTS):
                        cp.start()

        slot = slot_ref[i]
        row = lax.broadcasted_iota(jnp.int32, xs_ref.shape, 0)
        xp = jnp.where(row < rows_ref[i], xs_ref[...], jnp.uint32(0))
        ys_ref[...] = _pack_halves(_swiglu_packed(
            xp, wg_f[slot].astype(BF16), wu_f[slot].astype(BF16), wd_f[slot].astype(BF16)))


def _moe(item_e, item_b, item_first, item_slot, item_next, lead, item_rows, n_valid, xs, wg, wu,
         wd):
    m_pad, dh = xs.shape
    _, d, f = wg.shape
    n_items = item_e.shape[0]
    blk = lambda i, ie, ib, fi, sl, ne, ld, nr, nv: (ib[i], 0)
    hbm = pl.BlockSpec(memory_space=pl.ANY)
    return pl.pallas_call(
        _moe_kernel,
        grid_spec=pltpu.PrefetchScalarGridSpec(
            num_scalar_prefetch=8,
            grid=(n_items,),
            in_specs=[pl.BlockSpec((MOE_ROWS, dh), blk), hbm, hbm, hbm],
            out_specs=pl.BlockSpec((MOE_ROWS, dh), blk),
            scratch_shapes=[pltpu.VMEM((MOE_SLOTS, d, f), F32), pltpu.VMEM((MOE_SLOTS, d, f), F32),
                            pltpu.VMEM((MOE_SLOTS, f, d), F32),
                            pltpu.SemaphoreType.DMA((MOE_SLOTS, 3 * MOE_DMA_CHUNKS))],
        ),
        out_shape=jax.ShapeDtypeStruct((m_pad, dh), jnp.uint32),
        compiler_params=_cparams(("arbitrary",)),
        name="moe",
    )(item_e, item_b, item_first, item_slot, item_next, lead, item_rows, n_valid, xs, wg, wu, wd)


def _sc_gather_rows(table, idx_row):
    m = idx_row.shape[1]
    w = table.shape[1]
    idx_row = idx_row.reshape(m // SC_GATHER_WINDOW, SC_GATHER_WINDOW)
    mesh = plsc.VectorSubcoreMesh(core_axis_name="c", subcore_axis_name="s")

    @functools.partial(pl.kernel, mesh=mesh,
                       out_type=jax.ShapeDtypeStruct((m, w), table.dtype))
    def gather(table_hbm, idx_hbm, out_hbm):
        def body(idx_vmem, out_vmem):
            pltpu.sync_copy(table_hbm.at[idx_vmem.at[0]], out_vmem)

        pltpu.emit_pipeline(
            body,
            grid=(m // SC_GATHER_WINDOW,),
            in_specs=[pl.BlockSpec((1, SC_GATHER_WINDOW), lambda i: (i, 0))],
            out_specs=[pl.BlockSpec((SC_GATHER_WINDOW, w), lambda i: (i, 0))],
            core_axis_name=("c", "s"),
            dimension_semantics=(pltpu.PARALLEL,),
        )(idx_hbm, out_hbm)

    return gather(table, idx_row)


def _sc_scatter_rows(rows, idx_blocks, m_out):
    s, w = rows.shape
    mesh = plsc.VectorSubcoreMesh(core_axis_name="c", subcore_axis_name="s")

    @functools.partial(pl.kernel, mesh=mesh,
                       out_type=jax.ShapeDtypeStruct((m_out, w), rows.dtype))
    def scatter(rows_hbm, idx_hbm, out_hbm):
        def body(rows_vmem, idx_vmem):
            for k in range(TOP_K):
                pltpu.sync_copy(rows_vmem, out_hbm.at[idx_vmem.at[k]])

        pltpu.emit_pipeline(
            body,
            grid=(s // SC_GATHER_WINDOW,),
            in_specs=[pl.BlockSpec((SC_GATHER_WINDOW, w), lambda i: (i, 0)),
                      pl.BlockSpec((TOP_K, SC_GATHER_WINDOW), lambda i: (i, 0))],
            out_specs=[],
            core_axis_name=("c", "s"),
            dimension_semantics=(pltpu.PARALLEL,),
        )(rows_hbm, idx_hbm)

    return scatter(rows, idx_blocks)


def _combine_kernel(wt_ref, hf_ref, x1_ref, gt_ref, sg_ref, su_ref, sd_ref, g_ref, o_ref):
    tc = x1_ref.shape[0]
    y = _swiglu_packed(hf_ref[...], sg_ref, su_ref, sd_ref)
    wt = wt_ref[...]
    n = g_ref.shape[2]
    r_lo = jnp.zeros((tc, n), F32)
    r_hi = jnp.zeros((tc, n), F32)
    for k in range(TOP_K):
        lo, hi = _unpack_halves(g_ref[k])
        r_lo = r_lo + lo * wt[:, k:k + 1]
        r_hi = r_hi + hi * wt[:, k:k + 1]
    y = y + jnp.concatenate([r_lo, r_hi], axis=1)
    o_ref[...] = x1_ref[...] + gt_ref[...] * y


def _combine(wts_t, hfp, x1, gt, sg, su, sd, gathered, tc):
    s, d = x1.shape
    f = sg.shape[1]
    row = pl.BlockSpec((tc, d), lambda i: (i, 0))
    return pl.pallas_call(
        _combine_kernel,
        grid=(s // tc,),
        in_specs=[pl.BlockSpec((tc, TOP_K), lambda i: (i, 0)),
                  pl.BlockSpec((tc, d // 2), lambda i: (i, 0)), row,
                  pl.BlockSpec((1, d), lambda i: (0, 0)),
                  pl.BlockSpec((d, f), lambda i: (0, 0)),
                  pl.BlockSpec((d, f), lambda i: (0, 0)),
                  pl.BlockSpec((f, d), lambda i: (0, 0)),
                  pl.BlockSpec((TOP_K, tc, d // 2), lambda i: (0, i, 0))],
        out_specs=row,
        out_shape=jax.ShapeDtypeStruct((s, d), F32),
        compiler_params=_cparams(("arbitrary",)),
        name="combine",
    )(wts_t, hfp, x1, gt, sg, su, sd, gathered)


def _tile(n, want):
    t = min(n, want)
    assert n % t == 0, (n, t)
    return t


def _layer(l, x, c_col, pos_row, p):
    s, d = x.shape
    lambda_init = 0.8 - 0.6 * math.exp(-0.3 * l)
    gqk, gv = GLA_HEADS * GLA_DK, GLA_HEADS * GLA_DV
    dqk, dvw = DIFF_HEADS * 2 * DIFF_DH, DIFF_HEADS * DIFF_DV
    lowrank = p["gla_w_a2"].shape[0]

    mod = _ada(c_col, p["w_ada"], p["b_ada"][None, :])
    sh_a, sc_a, gt_a, sh_f, sc_f, gt_f = [mod[:, j * d:(j + 1) * d] for j in range(6)]

    w_in = p["w_in"]
    o = 0
    cols = {}
    for name, wdt in (("gq", gqk), ("gk", gqk), ("gv", gv), ("ga", lowrank), ("gg", gv),
                      ("dq", dqk), ("dk", dqk), ("dv", dvw), ("mg", d), ("md", d)):
        cols[name] = w_in[:, o:o + wdt]
        o += wdt
    w_ga = jnp.pad(cols["ga"], ((0, 0), (0, 128 - lowrank))).astype(BF16)

    g1 = p["norm1_g"][None, :]
    ts = _tile(s, TILE_SEQ)

    invf = ROPE_THETA ** (-jnp.arange(0, ROT_DIM, 2, dtype=F32) / ROT_DIM)
    qat, qbt, kr, vte, qn, kn, gkt = _qkvprep(
        x, g1, sc_a, sh_a, *(cols[n].T.astype(BF16) for n in ("dq", "dk", "dv", "gk")),
        pos_row, invf[:, None], p["diff_qnorm_g"][:, None], p["diff_knorm_g"][:, None], ts)

    wa2t = jnp.pad(p["gla_w_a2"].T, ((0, 0), (0, 128 - lowrank)))
    o_gla = _gla(x, g1, sc_a, sh_a, cols["gq"].astype(BF16), cols["gv"].astype(BF16),
                 cols["gg"].astype(BF16), w_ga, gkt, wa2t, p["gla_b_a"][:, None],
                 p["gla_onorm_g"][None, :], ts)
    o_diff = _diffattn(qat, qbt, kr, vte, qn, kn, p["diff_lq1"][None, :], p["diff_lk1"][None, :],
                       p["diff_lq2"][None, :], p["diff_lk2"][None, :],
                       p["diff_subln_g"][:, None], lambda_init, _tile(s, TILE_ATT_Q), ts)

    e = p["w_router"].shape[1]
    x1, hfp, idx, wts, rnk, cnt = _mergeout(
        o_gla, o_diff, x, g1, sc_a, sh_a, cols["mg"].astype(BF16), cols["md"].astype(BF16),
        p["w_branch_gla"].astype(BF16), p["w_branch_diff"].astype(BF16),
        p["w_out"].astype(BF16), gt_a, p["norm2_g"][None, :], sc_f, sh_f,
        p["w_router"].T, p["router_bias"][:, None], ts)

    counts = cnt[:, 0]
    pcounts = ((counts + MOE_ROWS - 1) // MOE_ROWS) * MOE_ROWS
    pend = jnp.cumsum(pcounts)
    pstart = pend - pcounts
    pos = _positions(idx, rnk, pstart[:, None], ts)
    n_items = (s * TOP_K) // MOE_ROWS + e
    n_valid = (pend[-1] // MOE_ROWS).astype(jnp.int32)
    item_b = jnp.minimum(jnp.arange(n_items, dtype=jnp.int32), n_valid - 1)
    item_e = jnp.minimum(jnp.sum(pend[None, :] <= (item_b * MOE_ROWS)[:, None], axis=1),
                         e - 1).astype(jnp.int32)

    wn = SC_GATHER_WINDOW
    pos_w = pos.reshape(TOP_K, s // wn, wn).transpose(1, 0, 2).reshape(s // wn * TOP_K, wn)
    xs = _sc_scatter_rows(hfp, pos_w, n_items * MOE_ROWS)
    item_rows = jnp.clip(pstart[item_e] + counts[item_e] - item_b * MOE_ROWS, 0,
                         MOE_ROWS).astype(jnp.int32)
    prev_e = jnp.concatenate([jnp.full((1,), -1, jnp.int32), item_e[:-1]])
    item_first = ((jnp.arange(n_items) < n_valid) & (item_e != prev_e)).astype(jnp.int32)
    item_slot = ((jnp.cumsum(item_first) - 1) % MOE_SLOTS).astype(jnp.int32)
    cand = jnp.where(pcounts > 0, jnp.arange(e, dtype=jnp.int32), e)
    nonempty_from = lax.cummin(cand[::-1])[::-1]
    following = jnp.concatenate([nonempty_from[1:], jnp.full((2,), e, jnp.int32)])
    ahead = item_e
    lead = [nonempty_from[0]]
    for _ in range(MOE_SLOTS - 1):
        ahead = following[ahead]
        lead.append(following[lead[-1]])
    item_next = jnp.where(ahead < e, ahead, -1).astype(jnp.int32)
    lead = jnp.stack(lead[:MOE_SLOTS - 1])
    lead = jnp.where(lead < e, lead, -1).astype(jnp.int32)
    ys = _moe(item_e, item_b, item_first, item_slot, item_next, lead, item_rows, n_valid[None],
              xs, p["w_exp_gate"], p["w_exp_up"], p["w_exp_down"])
    gathered = _sc_gather_rows(ys, pos.reshape(1, TOP_K * s)).reshape(TOP_K, s, d // 2)
    return _combine(wts.T, hfp, x1, gt_f, p["w_sh_gate"].astype(BF16),
                    p["w_sh_up"].astype(BF16), p["w_sh_down"].astype(BF16), gathered,
                    _tile(s, TILE_COMBINE))


_LAYER_PARAMS = ("w_ada", "b_ada", "norm1_g", "w_in", "gla_w_a2", "gla_b_a", "gla_onorm_g",
                 "diff_qnorm_g", "diff_knorm_g", "diff_lq1", "diff_lk1", "diff_lq2", "diff_lk2",
                 "diff_subln_g", "w_branch_gla", "w_branch_diff", "w_out", "norm2_g", "w_router",
                 "router_bias", "w_exp_gate", "w_exp_up", "w_exp_down", "w_sh_gate", "w_sh_up",
                 "w_sh_down")


def kernel(x, c, positions, w_ada, b_ada, norm1_g, w_in, gla_w_a2, gla_b_a, gla_onorm_g, diff_qnorm_g, diff_knorm_g, diff_lq1, diff_lk1, diff_lq2, diff_lk2, diff_subln_g, w_branch_gla, w_branch_diff, w_out, norm2_g, w_router, router_bias, w_exp_gate, w_exp_up, w_exp_down, w_sh_gate, w_sh_up, w_sh_down):
    stacked = dict(zip(_LAYER_PARAMS, (
        w_ada, b_ada, norm1_g, w_in, gla_w_a2, gla_b_a, gla_onorm_g, diff_qnorm_g, diff_knorm_g,
        diff_lq1, diff_lk1, diff_lq2, diff_lk2, diff_subln_g, w_branch_gla, w_branch_diff, w_out,
        norm2_g, w_router, router_bias, w_exp_gate, w_exp_up, w_exp_down, w_sh_gate, w_sh_up,
        w_sh_down)))
    b, s, d = x.shape
    assert b == 1, "single-sequence kernel"
    xl = x[0]
    c_col = c[0][:, None]
    pos_row = positions.astype(jnp.int32)
    for l in range(w_ada.shape[0]):
        xl = _layer(l, xl, c_col, pos_row, {k: v[l] for k, v in stacked.items()})
    return xl[None]
```

```python
import functools
import math

import jax
import jax.numpy as jnp
from jax import lax
from jax.experimental import pallas as pl
from jax.experimental.pallas import tpu as pltpu
from jax.experimental.pallas import tpu_sc as plsc

CHUNK = 64
EPS = 1e-6
GLA_HEADS = 4
GLA_DK = 128
GLA_DV = 256
GLA_TAU = 16.0
DIFF_HEADS = 8
DIFF_DH = 64
DIFF_DV = 2 * DIFF_DH
ROPE_THETA = 500000.0
ROT_DIM = DIFF_DH // 4
N_GROUPS = 8
TOPK_GROUPS = 4
TOP_K = 8
ROUTED_SCALE = 2.5

MOE_ROWS = 640
MOE_DMA_CHUNKS = 4
MOE_SLOTS = 3
SC_GATHER_WINDOW = 64
VMEM_LIMIT = 56 * 1024 * 1024
TILE_SEQ = 512
TILE_ATT_Q = 2048
TILE_COMBINE = 512
NEG_BIG = -1e30
LOG2E = 1.4426950408889634
F32 = jnp.float32
BF16 = jnp.bfloat16


def _cparams(sem):
    return pltpu.CompilerParams(dimension_semantics=sem, vmem_limit_bytes=VMEM_LIMIT)


def _nt_dot(a, b):
    return lax.dot_general(a, b, (((1,), (1,)), ((), ())), preferred_element_type=F32)


def _pack_halves(x):
    n = x.shape[1] // 2
    lo = pltpu.bitcast(x[:, :n].astype(BF16).astype(F32), jnp.uint32) >> 16
    hi = pltpu.bitcast(x[:, n:].astype(BF16).astype(F32), jnp.uint32) & jnp.uint32(0xFFFF0000)
    return lo | hi


def _unpack_halves(w):
    return (pltpu.bitcast(w << 16, F32), pltpu.bitcast(w & jnp.uint32(0xFFFF0000), F32))


def _split_bf16(a):
    hi = a.astype(BF16)
    return hi, (a - hi.astype(F32)).astype(BF16)


def _rms_mod(x, g, sc, sh):
    xn = x * lax.rsqrt(jnp.mean(x * x, axis=-1, keepdims=True) + EPS)
    return (xn * g) * (1.0 + sc) + sh


def _ada_kernel(c_ref, w_ref, b_ref, o_ref):
    c = c_ref[...]
    ca = c * jax.nn.sigmoid(c)
    o_ref[...] = jnp.sum(ca * w_ref[...], axis=0, keepdims=True) + b_ref[...]


def _ada(c_col, w, b):
    d, n = w.shape
    tn = min(1024, n)
    return pl.pallas_call(
        _ada_kernel,
        grid=(n // tn,),
        in_specs=[pl.BlockSpec((d, 1), lambda j: (0, 0)),
                  pl.BlockSpec((d, tn), lambda j: (0, j)),
                  pl.BlockSpec((1, tn), lambda j: (0, j))],
        out_specs=pl.BlockSpec((1, tn), lambda j: (0, j)),
        out_shape=jax.ShapeDtypeStruct((1, n), F32),
        compiler_params=_cparams(("arbitrary",)),
        name="ada",
    )(c_col, w, b)


def _gla_kernel(x_ref, g1_ref, sc_ref, sh_ref, wq_ref, wv_ref, wg_ref, wga_ref, kt_ref, wa2t_ref,
                ba_ref, on_ref, o_ref, state_ref, o_scr, q_ref, v_ref, gg_ref):
    tt = x_ref.shape[0]
    nchunk = tt // CHUNK

    @pl.when(pl.program_id(0) == 0)
    def _():
        state_ref[...] = jnp.zeros_like(state_ref)

    h_in = _rms_mod(x_ref[...], g1_ref[...], sc_ref[...], sh_ref[...]).astype(BF16)
    q_ref[...] = jnp.dot(h_in, wq_ref[...], preferred_element_type=F32).astype(BF16)
    v_ref[...] = jnp.dot(h_in, wv_ref[...], preferred_element_type=F32).astype(BF16)
    gg_ref[...] = jnp.dot(h_in, wg_ref[...], preferred_element_type=F32).astype(BF16)
    ga = jnp.dot(h_in, wga_ref[...], preferred_element_type=F32)

    a_hi, a_lo = _split_bf16(wa2t_ref[...])
    g_hi, g_lo = _split_bf16(ga)
    zt = _nt_dot(a_hi, g_hi) + _nt_dot(a_hi, g_lo) + _nt_dot(a_lo, g_hi) + ba_ref[...]
    lat = (jnp.minimum(zt, 0.0) - jnp.log1p(jnp.exp(-jnp.abs(zt)))) * (1.0 / GLA_TAU)
    row = lax.broadcasted_iota(jnp.int32, (tt, tt), 0)
    col = lax.broadcasted_iota(jnp.int32, (tt, tt), 1)
    same = (row // CHUNK) == (col // CHUNK)
    incl = jnp.where(same & (row <= col), 1.0, 0.0).astype(BF16)
    full = jnp.where(same, 1.0, 0.0).astype(BF16)
    lat_hi, lat_lo = _split_bf16(lat)
    cumt = (jnp.dot(lat_hi, incl, preferred_element_type=F32)
            + jnp.dot(lat_lo, incl, preferred_element_type=F32))
    tott = (jnp.dot(lat_hi, full, preferred_element_type=F32)
            + jnp.dot(lat_lo, full, preferred_element_type=F32))
    kdt = kt_ref[...].astype(F32) * jnp.exp(tott - cumt)
    dec = jnp.exp(tott)

    lane = lax.broadcasted_iota(jnp.int32, (GLA_DK, 2 * CHUNK), 1)
    upd = {}
    for c in range(nchunk):
        pair = (c // 2) * 2 * CHUNK
        if nchunk > 1:
            keep = (lane // CHUNK) == (c % 2)
        for h in range(GLA_HEADS):
            rows = slice(h * GLA_DK, (h + 1) * GLA_DK)
            vcols = slice(h * GLA_DV, (h + 1) * GLA_DV)
            if nchunk > 1:
                a = jnp.where(keep, kdt[rows, pair:pair + 2 * CHUNK], 0.0).astype(BF16)
                vp = v_ref[pair:pair + 2 * CHUNK, vcols]
            else:
                a = kdt[rows, :].astype(BF16)
                vp = v_ref[:, vcols]
            upd[c, h] = jnp.dot(a, vp, preferred_element_type=F32)

    for h in range(GLA_HEADS):
        rows = slice(h * GLA_DK, (h + 1) * GLA_DK)
        vcols = slice(h * GLA_DV, (h + 1) * GLA_DV)
        st = state_ref[h]
        states = []
        for c in range(nchunk):
            st = st * dec[rows, c * CHUNK:c * CHUNK + 1] + upd[c, h]
            states.append(st.astype(BF16))
        state_ref[h] = st
        for c in range(nchunk):
            o_scr[c * CHUNK:(c + 1) * CHUNK, vcols] = jnp.dot(
                q_ref[c * CHUNK:(c + 1) * CHUNK, rows], states[c], preferred_element_type=F32)

    for h in range(GLA_HEADS):
        vcols = slice(h * GLA_DV, (h + 1) * GLA_DV)
        o = o_scr[:, vcols] * (GLA_DK ** -0.5)
        o = o * lax.rsqrt(jnp.mean(o * o, axis=-1, keepdims=True) + EPS) * on_ref[...]
        g = gg_ref[:, vcols].astype(F32)
        o_ref[:, vcols] = (o * (g * jax.nn.sigmoid(g))).astype(BF16)


def _gla(x, g1, sc, sh, wq, wv, wg, wga, gkt, wa2t, ba_col, on_g, tt):
    s, d = x.shape
    qk = GLA_HEADS * GLA_DK
    vw = GLA_HEADS * GLA_DV
    vec = pl.BlockSpec((1, d), lambda i: (0, 0))
    whole = lambda a: pl.BlockSpec(a.shape, lambda i: (0, 0))
    return pl.pallas_call(
        _gla_kernel,
        grid=(s // tt,),
        in_specs=[pl.BlockSpec((tt, d), lambda i: (i, 0)), vec, vec, vec,
                  whole(wq), whole(wv), whole(wg), whole(wga),
                  pl.BlockSpec((qk, tt), lambda i: (0, i)),
                  whole(wa2t), whole(ba_col), whole(on_g)],
        out_specs=pl.BlockSpec((tt, vw), lambda i: (i, 0)),
        out_shape=jax.ShapeDtypeStruct((s, vw), BF16),
        scratch_shapes=[pltpu.VMEM((GLA_HEADS, GLA_DK, GLA_DV), F32),
                        pltpu.VMEM((tt, vw), F32),
                        pltpu.VMEM((tt, qk), BF16), pltpu.VMEM((tt, vw), BF16),
                        pltpu.VMEM((tt, vw), BF16)],
        compiler_params=_cparams(("arbitrary",)),
        name="gla",
    )(x, g1, sc, sh, wq, wv, wg, wga, gkt, wa2t, ba_col, on_g)


def _qknorm_rope_t(xt, g_col, cos, sin):
    n, tm = xt.shape
    x3 = xt.reshape(n // DIFF_DH, DIFF_DH, tm)
    r = lax.rsqrt(jnp.mean(x3 * x3, axis=1, keepdims=True) + EPS)
    y = x3 * r * g_col[None]
    half = ROT_DIM // 2
    y1, y2, rest = y[:, :half], y[:, half:ROT_DIM], y[:, ROT_DIM:]
    o1 = y1 * cos[None] - y2 * sin[None]
    o2 = y2 * cos[None] + y1 * sin[None]
    return jnp.concatenate([o1, o2, rest], axis=1)


def _seg_norms(x3):
    return jnp.sqrt(jnp.sum(x3 * x3, axis=1)).reshape(DIFF_HEADS, 2, x3.shape[2])


def _qkvprep_kernel(x_ref, g_ref, sc_ref, sh_ref, wq_ref, wk_ref, wv_ref, wgk_ref, pos_ref,
                    invf_ref, qg_ref, kg_ref, qa_ref, qb_ref, ko_ref, ve_ref, qn_ref, kn_ref,
                    gk_ref):
    tm = x_ref.shape[0]
    h = _rms_mod(x_ref[...], g_ref[...], sc_ref[...], sh_ref[...]).astype(BF16)
    qt = _nt_dot(wq_ref[...], h)
    kt = _nt_dot(wk_ref[...], h)
    vt = _nt_dot(wv_ref[...], h)
    gk_ref[...] = _nt_dot(wgk_ref[...], h).astype(BF16)
    ang = pos_ref[...].astype(F32) * invf_ref[...]
    cos, sin = jnp.cos(ang), jnp.sin(ang)
    q3 = _qknorm_rope_t(qt, qg_ref[...], cos, sin) * (DIFF_DH ** -0.5 * LOG2E)
    qn_ref[...] = _seg_norms(q3)
    seg = lax.broadcasted_iota(jnp.int32, q3.shape, 0)
    qa_ref[...] = jnp.where(seg % 2 == 0, q3, 0.0).reshape(-1, tm).astype(BF16)
    qb_ref[...] = jnp.where(seg % 2 == 1, q3, 0.0).reshape(-1, tm).astype(BF16)
    k3 = _qknorm_rope_t(kt, kg_ref[...], cos, sin)
    ko_ref[...] = k3.reshape(-1, tm).T.astype(BF16)
    kn_ref[...] = _seg_norms(k3)
    v3 = vt.astype(BF16).reshape(DIFF_HEADS, DIFF_DV, tm)
    ones = jnp.ones((DIFF_HEADS, ATT_SUM_ROWS, tm), BF16)
    ve_ref[...] = jnp.concatenate([v3, ones], axis=1).reshape(-1, tm)


def _qkvprep(x, g, sc, sh, wq_t, wk_t, wv_t, wgk_t, pos_row, invf_col, qg_col, kg_col, tm):
    s, d = x.shape
    n = DIFF_HEADS * 2 * DIFF_DH
    ne = DIFF_HEADS * (DIFF_DV + ATT_SUM_ROWS)
    ngk = wgk_t.shape[0]
    col = pl.BlockSpec((DIFF_DH, 1), lambda i: (0, 0))
    vec = pl.BlockSpec((1, d), lambda i: (0, 0))
    wspec = pl.BlockSpec((n, d), lambda i: (0, 0))
    return pl.pallas_call(
        _qkvprep_kernel,
        grid=(s // tm,),
        in_specs=[pl.BlockSpec((tm, d), lambda i: (i, 0)), vec, vec, vec,
                  wspec, wspec, wspec, pl.BlockSpec((ngk, d), lambda i: (0, 0)),
                  pl.BlockSpec((1, tm), lambda i: (0, i)),
                  pl.BlockSpec((ROT_DIM // 2, 1), lambda i: (0, 0)), col, col],
        out_specs=[pl.BlockSpec((n, tm), lambda i: (0, i)),
                   pl.BlockSpec((n, tm), lambda i: (0, i)),
                   pl.BlockSpec((tm, n), lambda i: (i, 0)),
                   pl.BlockSpec((ne, tm), lambda i: (0, i)),
                   pl.BlockSpec((DIFF_HEADS, 2, tm), lambda i: (0, 0, i)),
                   pl.BlockSpec((DIFF_HEADS, 2, tm), lambda i: (0, 0, i)),
                   pl.BlockSpec((ngk, tm), lambda i: (0, i))],
        out_shape=[jax.ShapeDtypeStruct((n, s), BF16), jax.ShapeDtypeStruct((n, s), BF16),
                   jax.ShapeDtypeStruct((s, n), BF16), jax.ShapeDtypeStruct((ne, s), BF16),
                   jax.ShapeDtypeStruct((DIFF_HEADS, 2, s), F32),
                   jax.ShapeDtypeStruct((DIFF_HEADS, 2, s), F32),
                   jax.ShapeDtypeStruct((ngk, s), BF16)],
        compiler_params=_cparams(("arbitrary",)),
        name="qkvprep",
    )(x, g, sc, sh, wq_t, wk_t, wv_t, wgk_t, pos_row, invf_col, qg_col, kg_col)


ATT_COLS = 256
ATT_LOOKAHEAD = {True: 2, False: 4}
ATT_BODY_ITEMS = {True: 32, False: 8}
ATT_SUM_ROWS = 16
ATT_BOUND_SLACK = 1.02
ATT_BOUND_LIMIT = 50.0


def _diffattn_kernel(qa_ref, qb_ref, k_ref, vt_ref, qn_ref, kn_ref, lq1_ref, lk1_ref, lq2_ref,
                     lk2_ref, sg_ref, o_ref, *scr, lambda_init, tk, cols):
    tq = qa_ref.shape[1]
    nblk = 2 * tq // cols
    m_scr, acc_scr = (scr[b * nblk:(b + 1) * nblk] for b in range(2))
    kmax_scr = scr[2 * nblk]
    s_scr = scr[2 * nblk + 1:]
    i = pl.program_id(1)

    @pl.when(i == 0)
    def _():
        kmax_scr[...] = jnp.max(kn_ref[...], axis=1, keepdims=True)

    bound = qn_ref[...] * kmax_scr[...] * ATT_BOUND_SLACK
    bound = jnp.concatenate([bound[0:1], bound[1:2]], axis=1)
    bounded = jnp.max(bound) < ATT_BOUND_LIMIT
    for c in range(nblk):
        m_scr[c][...] = jnp.where(bounded, bound[:, c * cols:(c + 1) * cols], NEG_BIG)
        acc_scr[c][...] = jnp.zeros_like(acc_scr[c])

    def scores(j, c):
        start = pl.multiple_of(j * tk, tk)
        q_ref = qa_ref if c * cols < tq else qb_ref
        off = (c * cols) % tq
        return jnp.dot(k_ref[pl.ds(start, tk), :], q_ref[:, off:off + cols],
                       preferred_element_type=F32)

    def steps(items, next_tile, fixed):
        look = ATT_LOOKAHEAD[fixed]
        pending = []
        for n, (j, c, keys) in enumerate(items):
            s = s_scr[n][...] if n < look else pending.pop(0)
            ahead = n + look
            if ahead < len(items):
                pending.append(scores(*items[ahead][:2]))
            elif next_tile is not None:
                s_scr[ahead - len(items)][...] = scores(next_tile, ahead - len(items))
            start = pl.multiple_of(j * tk, tk)
            if keys is None:
                keys = tk
            else:
                s = s[:keys]
                krow = lax.broadcasted_iota(jnp.int32, (keys, cols), 0)
                qcol = lax.broadcasted_iota(jnp.int32, (keys, cols), 1)
                qpos = i * tq + (c * cols) % tq + qcol
                s = jnp.where((start + krow) // CHUNK <= qpos // CHUNK, s, NEG_BIG)
            vj = vt_ref[:, pl.ds(start, keys)]
            if fixed:
                p = jnp.exp2(s - m_scr[c][...])
                acc_scr[c][:DIFF_DV] += jnp.dot(vj[:DIFF_DV], p.astype(BF16),
                                                preferred_element_type=F32)
                acc_scr[c][DIFF_DV:DIFF_DV + 1] += jnp.sum(p, axis=0, keepdims=True)
            else:
                m_prev = m_scr[c][...]
                m_new = jnp.maximum(m_prev, jnp.max(s, axis=0, keepdims=True))
                alpha = jnp.exp2(m_prev - m_new)
                p = jnp.exp2((s - m_new).astype(BF16))
                acc_scr[c][...] = alpha * acc_scr[c][...] + jnp.dot(
                    vj, p, preferred_element_type=F32)
                m_scr[c][...] = m_new

    def past(first, count):
        return [(first + u, c, None) for u in range(count) for c in range(nblk)]

    def attend(fixed):
        n_past = i * (tq // tk)
        for c in range(ATT_LOOKAHEAD[fixed]):
            s_scr[c][...] = scores(0, c)
        big = max(1, ATT_BODY_ITEMS[fixed] // nblk)
        lax.fori_loop(0, n_past // big, lambda t, c: (steps(
            past(big * t, big), big * t + big, fixed), c)[1], 0)
        size = big // 2
        while size >= 1:
            first = (n_past // (2 * size)) * (2 * size)

            @pl.when(n_past % (2 * size) >= size)
            def _(first=first, size=size):
                steps(past(first, size), first + size, fixed)

            size //= 2
        diagonal = []
        for d in range(tq // tk):
            for c in range(nblk):
                off = (c * cols) % tq
                visible = off + cols - d * tk
                if visible > 0:
                    diagonal.append((n_past + d, c, None if off >= (d + 1) * tk
                                     else min(tk, visible)))
        steps(diagonal, None, fixed)

    @pl.when(bounded)
    def _():
        attend(True)

    @pl.when(jnp.logical_not(bounded))
    def _():
        attend(False)

    o = jnp.concatenate([acc_scr[c][:DIFF_DV] * (1.0 / acc_scr[c][DIFF_DV:DIFF_DV + 1])
                         for c in range(nblk)], axis=1)
    lam = (jnp.exp(jnp.sum(lq1_ref[...] * lk1_ref[...]))
           - jnp.exp(jnp.sum(lq2_ref[...] * lk2_ref[...])) + lambda_init)
    o = o[:, :tq] - lam * o[:, tq:]
    o = o * lax.rsqrt(jnp.mean(o * o, axis=0, keepdims=True) + EPS) * sg_ref[...]
    o_ref[...] = (o * (1.0 - lambda_init)).T.astype(BF16)


def _diffattn(qat, qbt, kr, vte, qn, kn, lq1, lk1, lq2, lk2, sg_col, lambda_init, tq, tk):
    s = kr.shape[0]
    hd = 2 * DIFF_DH
    vec = pl.BlockSpec((1, DIFF_DH), lambda h, i: (0, 0))
    cols = min(ATT_COLS, tq)
    nblk = 2 * tq // cols
    kern = functools.partial(_diffattn_kernel, lambda_init=lambda_init, tk=tk, cols=cols)
    dve = DIFF_DV + ATT_SUM_ROWS
    assert tq % tk == 0 and nblk >= max(ATT_LOOKAHEAD.values())
    scratch = ([pltpu.VMEM((1, cols), F32)] * nblk + [pltpu.VMEM((dve, cols), F32)] * nblk
               + [pltpu.VMEM((2, 1), F32)]
               + [pltpu.VMEM((tk, cols), F32)] * max(ATT_LOOKAHEAD.values()))
    return pl.pallas_call(
        kern,
        grid=(DIFF_HEADS, s // tq),
        in_specs=[pl.BlockSpec((hd, tq), lambda h, i: (h, i)),
                  pl.BlockSpec((hd, tq), lambda h, i: (h, i)),
                  pl.BlockSpec((s, hd), lambda h, i: (0, h)),
                  pl.BlockSpec((dve, s), lambda h, i: (h, 0)),
                  pl.BlockSpec((None, 2, tq), lambda h, i: (h, 0, i)),
                  pl.BlockSpec((None, 2, s), lambda h, i: (h, 0, 0)),
                  vec, vec, vec, vec,
                  pl.BlockSpec((DIFF_DV, 1), lambda h, i: (0, 0))],
        out_specs=pl.BlockSpec((tq, DIFF_DV), lambda h, i: (i, h)),
        out_shape=jax.ShapeDtypeStruct((s, DIFF_HEADS * DIFF_DV), BF16),
        scratch_shapes=scratch,
        compiler_params=_cparams(("arbitrary", "arbitrary")),
        name="diffattn",
    )(qat, qbt, kr, vte, qn, kn, lq1, lk1, lq2, lk2, sg_col)


def _mergeout_kernel(og_ref, od_ref, x_ref, g1_ref, sca_ref, sha_ref, wmg_ref, wmd_ref, wbg_ref,
                     wbd_ref, wo_ref, gt_ref, g2_ref, sc_ref, sh_ref, wrt_ref, bias_ref,
                     x1_ref, hfp_ref, idx_ref, wts_ref, rnk_ref, cnt_ref, run_scr, hf_scr):
    i = pl.program_id(0)

    @pl.when(i == 0)
    def _():
        hf_scr[...] = jnp.zeros_like(hf_scr)
        run_scr[...] = jnp.zeros_like(run_scr)

    w_hi, w_lo = _split_bf16(wrt_ref[...])
    h_hi, h_lo = _split_bf16(hf_scr[...])
    logits = _nt_dot(w_hi, h_hi) + _nt_dot(w_hi, h_lo) + _nt_dot(w_lo, h_hi)
    x = x_ref[...]
    h_in = _rms_mod(x, g1_ref[...], sca_ref[...], sha_ref[...]).astype(BF16)
    mg = jnp.dot(h_in, wmg_ref[...], preferred_element_type=F32)
    md = jnp.dot(h_in, wmd_ref[...], preferred_element_type=F32)
    bg = jnp.dot(og_ref[...], wbg_ref[...], preferred_element_type=F32)
    bd = jnp.dot(od_ref[...], wbd_ref[...], preferred_element_type=F32)
    merged = jax.nn.sigmoid(mg) * bg + jax.nn.sigmoid(md) * bd
    x1 = x + gt_ref[...] * jnp.dot(merged.astype(BF16), wo_ref[...],
                                   preferred_element_type=F32)
    x1_ref[...] = x1
    hf = _rms_mod(x1, g2_ref[...], sc_ref[...], sh_ref[...])
    hfp_ref[...] = _pack_halves(hf)
    hf_scr[...] = hf
    _route_select(logits, i > 0, bias_ref, idx_ref, wts_ref, rnk_ref, cnt_ref, run_scr)


def _mergeout(og, od, x, g1, sca, sha, wmg, wmd, wbg, wbd, wo, gt, g2, sc, sh, wrt, bias_col,
              tm):
    s, d = x.shape
    e = wrt.shape[0]
    vec = pl.BlockSpec((1, d), lambda i: (0, 0))
    wspec = pl.BlockSpec((d, d), lambda i: (0, 0))
    n = s // tm
    row = pl.BlockSpec((tm, d), lambda i: (jnp.minimum(i, n - 1), 0))
    tok = pl.BlockSpec((TOP_K, tm), lambda i: (0, jnp.maximum(i - 1, 0)))
    return pl.pallas_call(
        _mergeout_kernel,
        grid=(n + 1,),
        in_specs=[row, row, row, vec, vec, vec, wspec, wspec, wspec, wspec, wspec,
                  vec, vec, vec, vec,
                  pl.BlockSpec((e, d), lambda i: (0, 0)), pl.BlockSpec((e, 1), lambda i: (0, 0))],
        out_specs=[row, pl.BlockSpec((tm, d // 2), lambda i: (jnp.minimum(i, n - 1), 0)),
                   tok, tok, tok, pl.BlockSpec((e, 128), lambda i: (0, 0))],
        out_shape=[jax.ShapeDtypeStruct((s, d), F32),
                   jax.ShapeDtypeStruct((s, d // 2), jnp.uint32),
                   jax.ShapeDtypeStruct((TOP_K, s), jnp.int32),
                   jax.ShapeDtypeStruct((TOP_K, s), F32),
                   jax.ShapeDtypeStruct((TOP_K, s), jnp.int32),
                   jax.ShapeDtypeStruct((e, 128), jnp.int32)],
        scratch_shapes=[pltpu.VMEM((e, 128), F32), pltpu.VMEM((tm, d), F32)],
        compiler_params=_cparams(("arbitrary",)),
        name="mergeout",
    )(og, od, x, g1, sca, sha, wmg, wmd, wbg, wbd, wo, gt, g2, sc, sh, wrt, bias_col)


def _route_select(logits, live, bias_ref, idx_ref, wts_ref, rnk_ref, cnt_ref, run_scr):
    e, tr = logits.shape
    gsz = e // N_GROUPS
    scores = jax.nn.sigmoid(logits)
    biased = scores + bias_ref[...]
    g3 = biased.reshape(N_GROUPS, gsz, tr)
    m1 = jnp.max(g3, axis=1, keepdims=True)
    n_top = jnp.sum(jnp.where(g3 == m1, 1.0, 0.0), axis=1, keepdims=True)
    m2 = jnp.max(jnp.where(g3 < m1, g3, -jnp.inf), axis=1, keepdims=True)
    gs = (m1 + jnp.where(n_top >= 2.0, m1, m2)).reshape(N_GROUPS, tr)
    gi = lax.broadcasted_iota(jnp.int32, (N_GROUPS, tr), 0)
    beaten = jnp.zeros((N_GROUPS, tr), F32)
    for g in range(N_GROUPS):
        other = gs[g:g + 1, :]
        beaten = beaten + jnp.where((other > gs) | ((other == gs) & (g < gi)), 1.0, 0.0)
    gsel = (beaten < float(TOPK_GROUPS)).reshape(N_GROUPS, 1, tr)
    masked = jnp.where(gsel, g3, -jnp.inf).reshape(e, tr)

    ids = lax.broadcasted_iota(jnp.int32, (e, tr), 0)
    chosen = jnp.zeros((e, tr), F32)
    sel_idx, sel_score = [], []
    for _ in range(TOP_K):
        mx = jnp.max(masked, axis=0, keepdims=True)
        ix = jnp.min(jnp.where(masked == mx, ids, e), axis=0, keepdims=True)
        hit = ids == ix
        sel_idx.append(ix)
        sel_score.append(jnp.sum(jnp.where(hit, scores, 0.0), axis=0, keepdims=True))
        chosen = jnp.where(hit, 1.0, chosen)
        masked = jnp.where(hit, -jnp.inf, masked)
    idx = jnp.concatenate(sel_idx, axis=0)
    sc = jnp.concatenate(sel_score, axis=0)
    idx_ref[...] = idx
    wts_ref[...] = sc / jnp.sum(sc, axis=0, keepdims=True) * ROUTED_SCALE

    row = lax.broadcasted_iota(jnp.int32, (tr, tr), 0)
    col = lax.broadcasted_iota(jnp.int32, (tr, tr), 1)
    before = jnp.where(row < col, 1.0, 0.0).astype(BF16)
    prior = jnp.dot(chosen.astype(BF16), before, preferred_element_type=F32) + run_scr[:, 0:1]
    rnk_ref[...] = jnp.concatenate(
        [jnp.sum(jnp.where(ids == sel_idx[k], prior, 0.0), axis=0, keepdims=True)
         for k in range(TOP_K)], axis=0).astype(jnp.int32)
    run_scr[...] = run_scr[...] + jnp.where(live, jnp.sum(chosen, axis=1, keepdims=True), 0.0)
    cnt_ref[...] = run_scr[...].astype(jnp.int32)


def _positions_kernel(idx_ref, rnk_ref, pstart_ref, pos_ref):
    e = pstart_ref.shape[0]
    ts = idx_ref.shape[1]
    ids = lax.broadcasted_iota(jnp.int32, (e, ts), 0)
    idx = idx_ref[...]
    pos_ref[...] = rnk_ref[...] + jnp.concatenate(
        [jnp.sum(jnp.where(ids == idx[k:k + 1, :], pstart_ref[...], 0), axis=0, keepdims=True)
         for k in range(TOP_K)], axis=0)


def _positions(idx, rnk, pstart_col, ts):
    s = idx.shape[1]
    e = pstart_col.shape[0]
    tok = pl.BlockSpec((TOP_K, ts), lambda i: (0, i))
    return pl.pallas_call(
        _positions_kernel,
        grid=(s // ts,),
        in_specs=[tok, tok, pl.BlockSpec((e, 1), lambda i: (0, 0))],
        out_specs=tok,
        out_shape=jax.ShapeDtypeStruct((TOP_K, s), jnp.int32),
        compiler_params=_cparams(("arbitrary",)),
        name="positions",
    )(idx, rnk, pstart_col)


def _swiglu_packed(xp, wg, wu, wd):
    lo, hi = _unpack_halves(xp)
    lo, hi = lo.astype(BF16), hi.astype(BF16)
    n = lo.shape[1]
    g = (jnp.dot(lo, wg[:n], preferred_element_type=F32)
         + jnp.dot(hi, wg[n:], preferred_element_type=F32))
    u = (jnp.dot(lo, wu[:n], preferred_element_type=F32)
         + jnp.dot(hi, wu[n:], preferred_element_type=F32))
    h = (g * jax.nn.sigmoid(g)) * u
    return jnp.dot(h.astype(BF16), wd[...], preferred_element_type=F32)


def _moe_kernel(ie_ref, ib_ref, first_ref, slot_ref, ne_ref, lead_ref, rows_ref, nv_ref, xs_ref,
                wg_hbm, wu_hbm, wd_hbm, ys_ref, wg_f, wu_f, wd_f, sem):
    del ib_ref
    i = pl.program_id(0)

    def fetch(e, slot):
        copies = []
        for n, (src, dst) in enumerate(((wg_hbm, wg_f), (wu_hbm, wu_f), (wd_hbm, wd_f))):
            rows = src.shape[1] // MOE_DMA_CHUNKS
            for ch in range(MOE_DMA_CHUNKS):
                part = pl.ds(ch * rows, rows)
                copies.append(pltpu.make_async_copy(
                    src.at[e, part], dst.at[slot, part], sem.at[slot, n * MOE_DMA_CHUNKS + ch]))
        return copies

    @pl.when(i == 0)
    def _():
        for s in range(MOE_SLOTS - 1):
            @pl.when(lead_ref[s] >= 0)
            def _(s=s):
                for cp in fetch(lead_ref[s], s):
                    cp.start()

    @pl.when(i < nv_ref[0])
    def _():
        for slot in range(MOE_SLOTS):
            @pl.when((first_ref[i] == 1) & (slot_ref[i] == slot))
            def _(slot=slot):
                for cp in fetch(ie_ref[i], slot):
                    cp.wait()

                @pl.when(ne_ref[i] >= 0)
                def _():
                    for cp in fetch(ne_ref[i], (slot + MOE_SLOTS - 1) % MOE_SLOTS):
                        cp.start()

        slot = slot_ref[i]
        row = lax.broadcasted_iota(jnp.int32, xs_ref.shape, 0)
        xp = jnp.where(row < rows_ref[i], xs_ref[...], jnp.uint32(0))
        ys_ref[...] = _pack_halves(_swiglu_packed(
            xp, wg_f[slot].astype(BF16), wu_f[slot].astype(BF16), wd_f[slot].astype(BF16)))


def _moe(item_e, item_b, item_first, item_slot, item_next, lead, item_rows, n_valid, xs, wg, wu,
         wd):
    m_pad, dh = xs.shape
    _, d, f = wg.shape
    n_items = item_e.shape[0]
    blk = lambda i, ie, ib, fi, sl, ne, ld, nr, nv: (ib[i], 0)
    hbm = pl.BlockSpec(memory_space=pl.ANY)
    return pl.pallas_call(
        _moe_kernel,
        grid_spec=pltpu.PrefetchScalarGridSpec(
            num_scalar_prefetch=8,
            grid=(n_items,),
            in_specs=[pl.BlockSpec((MOE_ROWS, dh), blk), hbm, hbm, hbm],
            out_specs=pl.BlockSpec((MOE_ROWS, dh), blk),
            scratch_shapes=[pltpu.VMEM((MOE_SLOTS, d, f), F32), pltpu.VMEM((MOE_SLOTS, d, f), F32),
                            pltpu.VMEM((MOE_SLOTS, f, d), F32),
                            pltpu.SemaphoreType.DMA((MOE_SLOTS, 3 * MOE_DMA_CHUNKS))],
        ),
        out_shape=jax.ShapeDtypeStruct((m_pad, dh), jnp.uint32),
        compiler_params=_cparams(("arbitrary",)),
        name="moe",
    )(item_e, item_b, item_first, item_slot, item_next, lead, item_rows, n_valid, xs, wg, wu, wd)


def _sc_gather_rows(table, idx_row):
    m = idx_row.shape[1]
    w = table.shape[1]
    idx_row = idx_row.reshape(m // SC_GATHER_WINDOW, SC_GATHER_WINDOW)
    mesh = plsc.VectorSubcoreMesh(core_axis_name="c", subcore_axis_name="s")

    @functools.partial(pl.kernel, mesh=mesh,
                       out_type=jax.ShapeDtypeStruct((m, w), table.dtype))
    def gather(table_hbm, idx_hbm, out_hbm):
        def body(idx_vmem, out_vmem):
            pltpu.sync_copy(table_hbm.at[idx_vmem.at[0]], out_vmem)

        pltpu.emit_pipeline(
            body,
            grid=(m // SC_GATHER_WINDOW,),
            in_specs=[pl.BlockSpec((1, SC_GATHER_WINDOW), lambda i: (i, 0))],
            out_specs=[pl.BlockSpec((SC_GATHER_WINDOW, w), lambda i: (i, 0))],
            core_axis_name=("c", "s"),
            dimension_semantics=(pltpu.PARALLEL,),
        )(idx_hbm, out_hbm)

    return gather(table, idx_row)


def _sc_scatter_rows(rows, idx_blocks, m_out):
    s, w = rows.shape
    mesh = plsc.VectorSubcoreMesh(core_axis_name="c", subcore_axis_name="s")

    @functools.partial(pl.kernel, mesh=mesh,
                       out_type=jax.ShapeDtypeStruct((m_out, w), rows.dtype))
    def scatter(rows_hbm, idx_hbm, out_hbm):
        def body(rows_vmem, idx_vmem):
            for k in range(TOP_K):
                pltpu.sync_copy(rows_vmem, out_hbm.at[idx_vmem.at[k]])

        pltpu.emit_pipeline(
            body,
            grid=(s // SC_GATHER_WINDOW,),
            in_specs=[pl.BlockSpec((SC_GATHER_WINDOW, w), lambda i: (i, 0)),
                      pl.BlockSpec((TOP_K, SC_GATHER_WINDOW), lambda i: (i, 0))],
            out_specs=[],
            core_axis_name=("c", "s"),
            dimension_semantics=(pltpu.PARALLEL,),
        )(rows_hbm, idx_hbm)

    return scatter(rows, idx_blocks)


def _combine_kernel(wt_ref, hf_ref, x1_ref, gt_ref, sg_ref, su_ref, sd_ref, g_ref, o_ref):
    tc = x1_ref.shape[0]
    y = _swiglu_packed(hf_ref[...], sg_ref, su_ref, sd_ref)
    wt = wt_ref[...]
    n = g_ref.shape[2]
    r_lo = jnp.zeros((tc, n), F32)
    r_hi = jnp.zeros((tc, n), F32)
    for k in range(TOP_K):
        lo, hi = _unpack_halves(g_ref[k])
        r_lo = r_lo + lo * wt[:, k:k + 1]
        r_hi = r_hi + hi * wt[:, k:k + 1]
    y = y + jnp.concatenate([r_lo, r_hi], axis=1)
    o_ref[...] = x1_ref[...] + gt_ref[...] * y


def _combine(wts_t, hfp, x1, gt, sg, su, sd, gathered, tc):
    s, d = x1.shape
    f = sg.shape[1]
    row = pl.BlockSpec((tc, d), lambda i: (i, 0))
    return pl.pallas_call(
        _combine_kernel,
        grid=(s // tc,),
        in_specs=[pl.BlockSpec((tc, TOP_K), lambda i: (i, 0)),
                  pl.BlockSpec((tc, d // 2), lambda i: (i, 0)), row,
                  pl.BlockSpec((1, d), lambda i: (0, 0)),
                  pl.BlockSpec((d, f), lambda i: (0, 0)),
                  pl.BlockSpec((d, f), lambda i: (0, 0)),
                  pl.BlockSpec((f, d), lambda i: (0, 0)),
                  pl.BlockSpec((TOP_K, tc, d // 2), lambda i: (0, i, 0))],
        out_specs=row,
        out_shape=jax.ShapeDtypeStruct((s, d), F32),
        compiler_params=_cparams(("arbitrary",)),
        name="combine",
    )(wts_t, hfp, x1, gt, sg, su, sd, gathered)


def _tile(n, want):
    t = min(n, want)
    assert n % t == 0, (n, t)
    return t


def _layer(l, x, c_col, pos_row, p):
    s, d = x.shape
    lambda_init = 0.8 - 0.6 * math.exp(-0.3 * l)
    gqk, gv = GLA_HEADS * GLA_DK, GLA_HEADS * GLA_DV
    dqk, dvw = DIFF_HEADS * 2 * DIFF_DH, DIFF_HEADS * DIFF_DV
    lowrank = p["gla_w_a2"].shape[0]

    mod = _ada(c_col, p["w_ada"], p["b_ada"][None, :])
    sh_a, sc_a, gt_a, sh_f, sc_f, gt_f = [mod[:, j * d:(j + 1) * d] for j in range(6)]

    w_in = p["w_in"]
    o = 0
    cols = {}
    for name, wdt in (("gq", gqk), ("gk", gqk), ("gv", gv), ("ga", lowrank), ("gg", gv),
                      ("dq", dqk), ("dk", dqk), ("dv", dvw), ("mg", d), ("md", d)):
        cols[name] = w_in[:, o:o + wdt]
        o += wdt
    w_ga = jnp.pad(cols["ga"], ((0, 0), (0, 128 - lowrank))).astype(BF16)

    g1 = p["norm1_g"][None, :]
    ts = _tile(s, TILE_SEQ)

    invf = ROPE_THETA ** (-jnp.arange(0, ROT_DIM, 2, dtype=F32) / ROT_DIM)
    qat, qbt, kr, vte, qn, kn, gkt = _qkvprep(
        x, g1, sc_a, sh_a, *(cols[n].T.astype(BF16) for n in ("dq", "dk", "dv", "gk")),
        pos_row, invf[:, None], p["diff_qnorm_g"][:, None], p["diff_knorm_g"][:, None], ts)

    wa2t = jnp.pad(p["gla_w_a2"].T, ((0, 0), (0, 128 - lowrank)))
    o_gla = _gla(x, g1, sc_a, sh_a, cols["gq"].astype(BF16), cols["gv"].astype(BF16),
                 cols["gg"].astype(BF16), w_ga, gkt, wa2t, p["gla_b_a"][:, None],
                 p["gla_onorm_g"][None, :], ts)
    o_diff = _diffattn(qat, qbt, kr, vte, qn, kn, p["diff_lq1"][None, :], p["diff_lk1"][None, :],
                       p["diff_lq2"][None, :], p["diff_lk2"][None, :],
                       p["diff_subln_g"][:, None], lambda_init, _tile(s, TILE_ATT_Q), ts)

    e = p["w_router"].shape[1]
    x1, hfp, idx, wts, rnk, cnt = _mergeout(
        o_gla, o_diff, x, g1, sc_a, sh_a, cols["mg"].astype(BF16), cols["md"].astype(BF16),
        p["w_branch_gla"].astype(BF16), p["w_branch_diff"].astype(BF16),
        p["w_out"].astype(BF16), gt_a, p["norm2_g"][None, :], sc_f, sh_f,
        p["w_router"].T, p["router_bias"][:, None], ts)

    counts = cnt[:, 0]
    pcounts = ((counts + MOE_ROWS - 1) // MOE_ROWS) * MOE_ROWS
    pend = jnp.cumsum(pcounts)
    pstart = pend - pcounts
    pos = _positions(idx, rnk, pstart[:, None], ts)
    n_items = (s * TOP_K) // MOE_ROWS + e
    n_valid = (pend[-1] // MOE_ROWS).astype(jnp.int32)
    item_b = jnp.minimum(jnp.arange(n_items, dtype=jnp.int32), n_valid - 1)
    item_e = jnp.minimum(jnp.sum(pend[None, :] <= (item_b * MOE_ROWS)[:, None], axis=1),
                         e - 1).astype(jnp.int32)

    wn = SC_GATHER_WINDOW
    pos_w = pos.reshape(TOP_K, s // wn, wn).transpose(1, 0, 2).reshape(s // wn * TOP_K, wn)
    xs = _sc_scatter_rows(hfp, pos_w, n_items * MOE_ROWS)
    item_rows = jnp.clip(pstart[item_e] + counts[item_e] - item_b * MOE_ROWS, 0,
                         MOE_ROWS).astype(jnp.int32)
    prev_e = jnp.concatenate([jnp.full((1,), -1, jnp.int32), item_e[:-1]])
    item_first = ((jnp.arange(n_items) < n_valid) & (item_e != prev_e)).astype(jnp.int32)
    item_slot = ((jnp.cumsum(item_first) - 1) % MOE_SLOTS).astype(jnp.int32)
    cand = jnp.where(pcounts > 0, jnp.arange(e, dtype=jnp.int32), e)
    nonempty_from = lax.cummin(cand[::-1])[::-1]
    following = jnp.concatenate([nonempty_from[1:], jnp.full((2,), e, jnp.int32)])
    ahead = item_e
    lead = [nonempty_from[0]]
    for _ in range(MOE_SLOTS - 1):
        ahead = following[ahead]
        lead.append(following[lead[-1]])
    item_next = jnp.where(ahead < e, ahead, -1).astype(jnp.int32)
    lead = jnp.stack(lead[:MOE_SLOTS - 1])
    lead = jnp.where(lead < e, lead, -1).astype(jnp.int32)
    ys = _moe(item_e, item_b, item_first, item_slot, item_next, lead, item_rows, n_valid[None],
              xs, p["w_exp_gate"], p["w_exp_up"], p["w_exp_down"])
    gathered = _sc_gather_rows(ys, pos.reshape(1, TOP_K * s)).reshape(TOP_K, s, d // 2)
    return _combine(wts.T, hfp, x1, gt_f, p["w_sh_gate"].astype(BF16),
                    p["w_sh_up"].astype(BF16), p["w_sh_down"].astype(BF16), gathered,
                    _tile(s, TILE_COMBINE))


_LAYER_PARAMS = ("w_ada", "b_ada", "norm1_g", "w_in", "gla_w_a2", "gla_b_a", "gla_onorm_g",
                 "diff_qnorm_g", "diff_knorm_g", "diff_lq1", "diff_lk1", "diff_lq2", "diff_lk2",
                 "diff_subln_g", "w_branch_gla", "w_branch_diff", "w_out", "norm2_g", "w_router",
                 "router_bias", "w_exp_gate", "w_exp_up", "w_exp_down", "w_sh_gate", "w_sh_up",
                 "w_sh_down")


def kernel(x, c, positions, w_ada, b_ada, norm1_g, w_in, gla_w_a2, gla_b_a, gla_onorm_g, diff_qnorm_g, diff_knorm_g, diff_lq1, diff_lk1, diff_lq2, diff_lk2, diff_subln_g, w_branch_gla, w_branch_diff, w_out, norm2_g, w_router, router_bias, w_exp_gate, w_exp_up, w_exp_down, w_sh_gate, w_sh_up, w_sh_down):
    stacked = dict(zip(_LAYER_PARAMS, (
        w_ada, b_ada, norm1_g, w_in, gla_w_a2, gla_b_a, gla_onorm_g, diff_qnorm_g, diff_knorm_g,
        diff_lq1, diff_lk1, diff_lq2, diff_lk2, diff_subln_g, w_branch_gla, w_branch_diff, w_out,
        norm2_g, w_router, router_bias, w_exp_gate, w_exp_up, w_exp_down, w_sh_gate, w_sh_up,
        w_sh_down)))
    b, s, d = x.shape
    assert b == 1, "single-sequence kernel"
    xl = x[0]
    c_col = c[0][:, None]
    pos_row = positions.astype(jnp.int32)
    for l in range(w_ada.shape[0]):
        xl = _layer(l, xl, c_col, pos_row, {k: v[l] for k, v in stacked.items()})
    return xl[None]
```

```python
import functools
import math

import jax
import jax.numpy as jnp
from jax import lax
from jax.experimental import pallas as pl
from jax.experimental.pallas import tpu as pltpu
from jax.experimental.pallas import tpu_sc as plsc

CHUNK = 64
EPS = 1e-6
GLA_HEADS = 4
GLA_DK = 128
GLA_DV = 256
GLA_TAU = 16.0
DIFF_HEADS = 8
DIFF_DH = 64
DIFF_DV = 2 * DIFF_DH
ROPE_THETA = 500000.0
ROT_DIM = DIFF_DH // 4
N_GROUPS = 8
TOPK_GROUPS = 4
TOP_K = 8
ROUTED_SCALE = 2.5

MOE_ROWS = 640
MOE_DMA_CHUNKS = 4
MOE_SLOTS = 3
SC_GATHER_WINDOW = 64
VMEM_LIMIT = 56 * 1024 * 1024
TILE_SEQ = 512
TILE_ATT_Q = 2048
TILE_COMBINE = 512
NEG_BIG = -1e30
LOG2E = 1.4426950408889634
F32 = jnp.float32
BF16 = jnp.bfloat16


def _cparams(sem):
    return pltpu.CompilerParams(dimension_semantics=sem, vmem_limit_bytes=VMEM_LIMIT)


def _nt_dot(a, b):
    return lax.dot_general(a, b, (((1,), (1,)), ((), ())), preferred_element_type=F32)


def _pack_halves(x):
    n = x.shape[1] // 2
    lo = pltpu.bitcast(x[:, :n].astype(BF16).astype(F32), jnp.uint32) >> 16
    hi = pltpu.bitcast(x[:, n:].astype(BF16).astype(F32), jnp.uint32) & jnp.uint32(0xFFFF0000)
    return lo | hi


def _unpack_halves(w):
    return (pltpu.bitcast(w << 16, F32), pltpu.bitcast(w & jnp.uint32(0xFFFF0000), F32))


def _split_bf16(a):
    hi = a.astype(BF16)
    return hi, (a - hi.astype(F32)).astype(BF16)


def _rms_mod(x, g, sc, sh):
    xn = x * lax.rsqrt(jnp.mean(x * x, axis=-1, keepdims=True) + EPS)
    return (xn * g) * (1.0 + sc) + sh


def _ada_kernel(c_ref, w_ref, b_ref, o_ref):
    c = c_ref[...]
    ca = c * jax.nn.sigmoid(c)
    o_ref[...] = jnp.sum(ca * w_ref[...], axis=0, keepdims=True) + b_ref[...]


def _ada(c_col, w, b):
    d, n = w.shape
    tn = min(1024, n)
    return pl.pallas_call(
        _ada_kernel,
        grid=(n // tn,),
        in_specs=[pl.BlockSpec((d, 1), lambda j: (0, 0)),
                  pl.BlockSpec((d, tn), lambda j: (0, j)),
                  pl.BlockSpec((1, tn), lambda j: (0, j))],
        out_specs=pl.BlockSpec((1, tn), lambda j: (0, j)),
        out_shape=jax.ShapeDtypeStruct((1, n), F32),
        compiler_params=_cparams(("arbitrary",)),
        name="ada",
    )(c_col, w, b)


def _gla_kernel(x_ref, g1_ref, sc_ref, sh_ref, wq_ref, wv_ref, wg_ref, wga_ref, kt_ref, wa2t_ref,
                ba_ref, on_ref, o_ref, state_ref, o_scr, q_ref, v_ref, gg_ref):
    tt = x_ref.shape[0]
    nchunk = tt // CHUNK

    @pl.when(pl.program_id(0) == 0)
    def _():
        state_ref[...] = jnp.zeros_like(state_ref)

    h_in = _rms_mod(x_ref[...], g1_ref[...], sc_ref[...], sh_ref[...]).astype(BF16)
    q_ref[...] = jnp.dot(h_in, wq_ref[...], preferred_element_type=F32).astype(BF16)
    v_ref[...] = jnp.dot(h_in, wv_ref[...], preferred_element_type=F32).astype(BF16)
    gg_ref[...] = jnp.dot(h_in, wg_ref[...], preferred_element_type=F32).astype(BF16)
    ga = jnp.dot(h_in, wga_ref[...], preferred_element_type=F32)

    a_hi, a_lo = _split_bf16(wa2t_ref[...])
    g_hi, g_lo = _split_bf16(ga)
    zt = _nt_dot(a_hi, g_hi) + _nt_dot(a_hi, g_lo) + _nt_dot(a_lo, g_hi) + ba_ref[...]
    lat = (jnp.minimum(zt, 0.0) - jnp.log1p(jnp.exp(-jnp.abs(zt)))) * (1.0 / GLA_TAU)
    row = lax.broadcasted_iota(jnp.int32, (tt, tt), 0)
    col = lax.broadcasted_iota(jnp.int32, (tt, tt), 1)
    same = (row // CHUNK) == (col // CHUNK)
    incl = jnp.where(same & (row <= col), 1.0, 0.0).astype(BF16)
    full = jnp.where(same, 1.0, 0.0).astype(BF16)
    lat_hi, lat_lo = _split_bf16(lat)
    cumt = (jnp.dot(lat_hi, incl, preferred_element_type=F32)
            + jnp.dot(lat_lo, incl, preferred_element_type=F32))
    tott = (jnp.dot(lat_hi, full, preferred_element_type=F32)
            + jnp.dot(lat_lo, full, preferred_element_type=F32))
    kdt = kt_ref[...].astype(F32) * jnp.exp(tott - cumt)
    dec = jnp.exp(tott)

    lane = lax.broadcasted_iota(jnp.int32, (GLA_DK, 2 * CHUNK), 1)
    upd = {}
    for c in range(nchunk):
        pair = (c // 2) * 2 * CHUNK
        if nchunk > 1:
            keep = (lane // CHUNK) == (c % 2)
        for h in range(GLA_HEADS):
            rows = slice(h * GLA_DK, (h + 1) * GLA_DK)
            vcols = slice(h * GLA_DV, (h + 1) * GLA_DV)
            if nchunk > 1:
                a = jnp.where(keep, kdt[rows, pair:pair + 2 * CHUNK], 0.0).astype(BF16)
                vp = v_ref[pair:pair + 2 * CHUNK, vcols]
            else:
                a = kdt[rows, :].astype(BF16)
                vp = v_ref[:, vcols]
            upd[c, h] = jnp.dot(a, vp, preferred_element_type=F32)

    for h in range(GLA_HEADS):
        rows = slice(h * GLA_DK, (h + 1) * GLA_DK)
        vcols = slice(h * GLA_DV, (h + 1) * GLA_DV)
        st = state_ref[h]
        states = []
        for c in range(nchunk):
            st = st * dec[rows, c * CHUNK:c * CHUNK + 1] + upd[c, h]
            states.append(st.astype(BF16))
        state_ref[h] = st
        for c in range(nchunk):
            o_scr[c * CHUNK:(c + 1) * CHUNK, vcols] = jnp.dot(
                q_ref[c * CHUNK:(c + 1) * CHUNK, rows], states[c], preferred_element_type=F32)

    for h in range(GLA_HEADS):
        vcols = slice(h * GLA_DV, (h + 1) * GLA_DV)
        o = o_scr[:, vcols] * (GLA_DK ** -0.5)
        o = o * lax.rsqrt(jnp.mean(o * o, axis=-1, keepdims=True) + EPS) * on_ref[...]
        g = gg_ref[:, vcols].astype(F32)
        o_ref[:, vcols] = (o * (g * jax.nn.sigmoid(g))).astype(BF16)


def _gla(x, g1, sc, sh, wq, wv, wg, wga, gkt, wa2t, ba_col, on_g, tt):
    s, d = x.shape
    qk = GLA_HEADS * GLA_DK
    vw = GLA_HEADS * GLA_DV
    vec = pl.BlockSpec((1, d), lambda i: (0, 0))
    whole = lambda a: pl.BlockSpec(a.shape, lambda i: (0, 0))
    return pl.pallas_call(
        _gla_kernel,
        grid=(s // tt,),
        in_specs=[pl.BlockSpec((tt, d), lambda i: (i, 0)), vec, vec, vec,
                  whole(wq), whole(wv), whole(wg), whole(wga),
                  pl.BlockSpec((qk, tt), lambda i: (0, i)),
                  whole(wa2t), whole(ba_col), whole(on_g)],
        out_specs=pl.BlockSpec((tt, vw), lambda i: (i, 0)),
        out_shape=jax.ShapeDtypeStruct((s, vw), BF16),
        scratch_shapes=[pltpu.VMEM((GLA_HEADS, GLA_DK, GLA_DV), F32),
                        pltpu.VMEM((tt, vw), F32),
                        pltpu.VMEM((tt, qk), BF16), pltpu.VMEM((tt, vw), BF16),
                        pltpu.VMEM((tt, vw), BF16)],
        compiler_params=_cparams(("arbitrary",)),
        name="gla",
    )(x, g1, sc, sh, wq, wv, wg, wga, gkt, wa2t, ba_col, on_g)


def _qknorm_rope_t(xt, g_col, cos, sin):
    n, tm = xt.shape
    x3 = xt.reshape(n // DIFF_DH, DIFF_DH, tm)
    r = lax.rsqrt(jnp.mean(x3 * x3, axis=1, keepdims=True) + EPS)
    y = x3 * r * g_col[None]
    half = ROT_DIM // 2
    y1, y2, rest = y[:, :half], y[:, half:ROT_DIM], y[:, ROT_DIM:]
    o1 = y1 * cos[None] - y2 * sin[None]
    o2 = y2 * cos[None] + y1 * sin[None]
    return jnp.concatenate([o1, o2, rest], axis=1)


def _seg_norms(x3):
    return jnp.sqrt(jnp.sum(x3 * x3, axis=1)).reshape(DIFF_HEADS, 2, x3.shape[2])


def _qkvprep_kernel(x_ref, g_ref, sc_ref, sh_ref, wq_ref, wk_ref, wv_ref, wgk_ref, pos_ref,
                    invf_ref, qg_ref, kg_ref, qa_ref, qb_ref, ko_ref, ve_ref, qn_ref, kn_ref,
                    gk_ref):
    tm = x_ref.shape[0]
    h = _rms_mod(x_ref[...], g_ref[...], sc_ref[...], sh_ref[...]).astype(BF16)
    qt = _nt_dot(wq_ref[...], h)
    kt = _nt_dot(wk_ref[...], h)
    vt = _nt_dot(wv_ref[...], h)
    gk_ref[...] = _nt_dot(wgk_ref[...], h).astype(BF16)
    ang = pos_ref[...].astype(F32) * invf_ref[...]
    cos, sin = jnp.cos(ang), jnp.sin(ang)
    q3 = _qknorm_rope_t(qt, qg_ref[...], cos, sin) * (DIFF_DH ** -0.5 * LOG2E)
    qn_ref[...] = _seg_norms(q3)
    seg = lax.broadcasted_iota(jnp.int32, q3.shape, 0)
    qa_ref[...] = jnp.where(seg % 2 == 0, q3, 0.0).reshape(-1, tm).astype(BF16)
    qb_ref[...] = jnp.where(seg % 2 == 1, q3, 0.0).reshape(-1, tm).astype(BF16)
    k3 = _qknorm_rope_t(kt, kg_ref[...], cos, sin)
    ko_ref[...] = k3.reshape(-1, tm).T.astype(BF16)
    kn_ref[...] = _seg_norms(k3)
    v3 = vt.astype(BF16).reshape(DIFF_HEADS, DIFF_DV, tm)
    ones = jnp.ones((DIFF_HEADS, ATT_SUM_ROWS, tm), BF16)
    ve_ref[...] = jnp.concatenate([v3, ones], axis=1).reshape(-1, tm)


def _qkvprep(x, g, sc, sh, wq_t, wk_t, wv_t, wgk_t, pos_row, invf_col, qg_col, kg_col, tm):
    s, d = x.shape
    n = DIFF_HEADS * 2 * DIFF_DH
    ne = DIFF_HEADS * (DIFF_DV + ATT_SUM_ROWS)
    ngk = wgk_t.shape[0]
    col = pl.BlockSpec((DIFF_DH, 1), lambda i: (0, 0))
    vec = pl.BlockSpec((1, d), lambda i: (0, 0))
    wspec = pl.BlockSpec((n, d), lambda i: (0, 0))
    return pl.pallas_call(
        _qkvprep_kernel,
        grid=(s // tm,),
        in_specs=[pl.BlockSpec((tm, d), lambda i: (i, 0)), vec, vec, vec,
                  wspec, wspec, wspec, pl.BlockSpec((ngk, d), lambda i: (0, 0)),
                  pl.BlockSpec((1, tm), lambda i: (0, i)),
                  pl.BlockSpec((ROT_DIM // 2, 1), lambda i: (0, 0)), col, col],
        out_specs=[pl.BlockSpec((n, tm), lambda i: (0, i)),
                   pl.BlockSpec((n, tm), lambda i: (0, i)),
                   pl.BlockSpec((tm, n), lambda i: (i, 0)),
                   pl.BlockSpec((ne, tm), lambda i: (0, i)),
                   pl.BlockSpec((DIFF_HEADS, 2, tm), lambda i: (0, 0, i)),
                   pl.BlockSpec((DIFF_HEADS, 2, tm), lambda i: (0, 0, i)),
                   pl.BlockSpec((ngk, tm), lambda i: (0, i))],
        out_shape=[jax.ShapeDtypeStruct((n, s), BF16), jax.ShapeDtypeStruct((n, s), BF16),
                   jax.ShapeDtypeStruct((s, n), BF16), jax.ShapeDtypeStruct((ne, s), BF16),
                   jax.ShapeDtypeStruct((DIFF_HEADS, 2, s), F32),
                   jax.ShapeDtypeStruct((DIFF_HEADS, 2, s), F32),
                   jax.ShapeDtypeStruct((ngk, s), BF16)],
        compiler_params=_cparams(("arbitrary",)),
        name="qkvprep",
    )(x, g, sc, sh, wq_t, wk_t, wv_t, wgk_t, pos_row, invf_col, qg_col, kg_col)


ATT_COLS = 256
ATT_LOOKAHEAD = {True: 2, False: 4}
ATT_BODY_ITEMS = {True: 64, False: 8}
ATT_SUM_ROWS = 16
ATT_BOUND_SLACK = 1.02
ATT_BOUND_LIMIT = 50.0


def _diffattn_kernel(qa_ref, qb_ref, k_ref, vt_ref, qn_ref, kn_ref, lq1_ref, lk1_ref, lq2_ref,
                     lk2_ref, sg_ref, o_ref, *scr, lambda_init, tk, cols):
    tq = qa_ref.shape[1]
    nblk = 2 * tq // cols
    m_scr, acc_scr = (scr[b * nblk:(b + 1) * nblk] for b in range(2))
    kmax_scr = scr[2 * nblk]
    s_scr = scr[2 * nblk + 1:]
    i = pl.program_id(1)

    @pl.when(i == 0)
    def _():
        kmax_scr[...] = jnp.max(kn_ref[...], axis=1, keepdims=True)

    bound = qn_ref[...] * kmax_scr[...] * ATT_BOUND_SLACK
    bound = jnp.concatenate([bound[0:1], bound[1:2]], axis=1)
    bounded = jnp.max(bound) < ATT_BOUND_LIMIT
    for c in range(nblk):
        m_scr[c][...] = jnp.where(bounded, bound[:, c * cols:(c + 1) * cols], NEG_BIG)
        acc_scr[c][...] = jnp.zeros_like(acc_scr[c])

    def scores(j, c):
        start = pl.multiple_of(j * tk, tk)
        q_ref = qa_ref if c * cols < tq else qb_ref
        off = (c * cols) % tq
        return jnp.dot(k_ref[pl.ds(start, tk), :], q_ref[:, off:off + cols],
                       preferred_element_type=F32)

    def steps(items, next_tile, fixed):
        look = ATT_LOOKAHEAD[fixed]
        pending = []
        for n, (j, c, keys) in enumerate(items):
            s = s_scr[n][...] if n < look else pending.pop(0)
            ahead = n + look
            if ahead < len(items):
                pending.append(scores(*items[ahead][:2]))
            elif next_tile is not None:
                s_scr[ahead - len(items)][...] = scores(next_tile, ahead - len(items))
            start = pl.multiple_of(j * tk, tk)
            if keys is None:
                keys = tk
            else:
                s = s[:keys]
                krow = lax.broadcasted_iota(jnp.int32, (keys, cols), 0)
                qcol = lax.broadcasted_iota(jnp.int32, (keys, cols), 1)
                qpos = i * tq + (c * cols) % tq + qcol
                s = jnp.where((start + krow) // CHUNK <= qpos // CHUNK, s, NEG_BIG)
            vj = vt_ref[:, pl.ds(start, keys)]
            if fixed:
                p = jnp.exp2(s - m_scr[c][...])
                acc_scr[c][:DIFF_DV] += jnp.dot(vj[:DIFF_DV], p.astype(BF16),
                                                preferred_element_type=F32)
                acc_scr[c][DIFF_DV:DIFF_DV + 1] += jnp.sum(p, axis=0, keepdims=True)
            else:
                m_prev = m_scr[c][...]
                m_new = jnp.maximum(m_prev, jnp.max(s, axis=0, keepdims=True))
                alpha = jnp.exp2(m_prev - m_new)
                p = jnp.exp2((s - m_new).astype(BF16))
                acc_scr[c][...] = alpha * acc_scr[c][...] + jnp.dot(
                    vj, p, preferred_element_type=F32)
                m_scr[c][...] = m_new

    def past(first, count):
        return [(first + u, c, None) for u in range(count) for c in range(nblk)]

    def attend(fixed):
        n_past = i * (tq // tk)
        for c in range(ATT_LOOKAHEAD[fixed]):
            s_scr[c][...] = scores(0, c)
        big = max(1, ATT_BODY_ITEMS[fixed] // nblk)
        lax.fori_loop(0, n_past // big, lambda t, c: (steps(
            past(big * t, big), big * t + big, fixed), c)[1], 0)
        size = big // 2
        while size >= 1:
            first = (n_past // (2 * size)) * (2 * size)

            @pl.when(n_past % (2 * size) >= size)
            def _(first=first, size=size):
                steps(past(first, size), first + size, fixed)

            size //= 2
        diagonal = []
        for d in range(tq // tk):
            for c in range(nblk):
                off = (c * cols) % tq
                visible = off + cols - d * tk
                if visible > 0:
                    diagonal.append((n_past + d, c, None if off >= (d + 1) * tk
                                     else min(tk, visible)))
        steps(diagonal, None, fixed)

    @pl.when(bounded)
    def _():
        attend(True)

    @pl.when(jnp.logical_not(bounded))
    def _():
        attend(False)

    o = jnp.concatenate([acc_scr[c][:DIFF_DV] * (1.0 / acc_scr[c][DIFF_DV:DIFF_DV + 1])
                         for c in range(nblk)], axis=1)
    lam = (jnp.exp(jnp.sum(lq1_ref[...] * lk1_ref[...]))
           - jnp.exp(jnp.sum(lq2_ref[...] * lk2_ref[...])) + lambda_init)
    o = o[:, :tq] - lam * o[:, tq:]
    o = o * lax.rsqrt(jnp.mean(o * o, axis=0, keepdims=True) + EPS) * sg_ref[...]
    o_ref[...] = (o * (1.0 - lambda_init)).T.astype(BF16)


def _diffattn(qat, qbt, kr, vte, qn, kn, lq1, lk1, lq2, lk2, sg_col, lambda_init, tq, tk):
    s = kr.shape[0]
    hd = 2 * DIFF_DH
    vec = pl.BlockSpec((1, DIFF_DH), lambda h, i: (0, 0))
    cols = min(ATT_COLS, tq)
    nblk = 2 * tq // cols
    kern = functools.partial(_diffattn_kernel, lambda_init=lambda_init, tk=tk, cols=cols)
    dve = DIFF_DV + ATT_SUM_ROWS
    assert tq % tk == 0 and nblk >= max(ATT_LOOKAHEAD.values())
    scratch = ([pltpu.VMEM((1, cols), F32)] * nblk + [pltpu.VMEM((dve, cols), F32)] * nblk
               + [pltpu.VMEM((2, 1), F32)]
               + [pltpu.VMEM((tk, cols), F32)] * max(ATT_LOOKAHEAD.values()))
    return pl.pallas_call(
        kern,
        grid=(DIFF_HEADS, s // tq),
        in_specs=[pl.BlockSpec((hd, tq), lambda h, i: (h, i)),
                  pl.BlockSpec((hd, tq), lambda h, i: (h, i)),
                  pl.BlockSpec((s, hd), lambda h, i: (0, h)),
                  pl.BlockSpec((dve, s), lambda h, i: (h, 0)),
                  pl.BlockSpec((None, 2, tq), lambda h, i: (h, 0, i)),
                  pl.BlockSpec((None, 2, s), lambda h, i: (h, 0, 0)),
                  vec, vec, vec, vec,
                  pl.BlockSpec((DIFF_DV, 1), lambda h, i: (0, 0))],
        out_specs=pl.BlockSpec((tq, DIFF_DV), lambda h, i: (i, h)),
        out_shape=jax.ShapeDtypeStruct((s, DIFF_HEADS * DIFF_DV), BF16),
        scratch_shapes=scratch,
        compiler_params=_cparams(("arbitrary", "arbitrary")),
        name="diffattn",
    )(qat, qbt, kr, vte, qn, kn, lq1, lk1, lq2, lk2, sg_col)


def _mergeout_kernel(og_ref, od_ref, x_ref, g1_ref, sca_ref, sha_ref, wmg_ref, wmd_ref, wbg_ref,
                     wbd_ref, wo_ref, gt_ref, g2_ref, sc_ref, sh_ref, wrt_ref, bias_ref,
                     x1_ref, hfp_ref, idx_ref, wts_ref, rnk_ref, cnt_ref, run_scr, hf_scr):
    i = pl.program_id(0)

    @pl.when(i == 0)
    def _():
        hf_scr[...] = jnp.zeros_like(hf_scr)
        run_scr[...] = jnp.zeros_like(run_scr)

    w_hi, w_lo = _split_bf16(wrt_ref[...])
    h_hi, h_lo = _split_bf16(hf_scr[...])
    logits = _nt_dot(w_hi, h_hi) + _nt_dot(w_hi, h_lo) + _nt_dot(w_lo, h_hi)
    x = x_ref[...]
    h_in = _rms_mod(x, g1_ref[...], sca_ref[...], sha_ref[...]).astype(BF16)
    mg = jnp.dot(h_in, wmg_ref[...], preferred_element_type=F32)
    md = jnp.dot(h_in, wmd_ref[...], preferred_element_type=F32)
    bg = jnp.dot(og_ref[...], wbg_ref[...], preferred_element_type=F32)
    bd = jnp.dot(od_ref[...], wbd_ref[...], preferred_element_type=F32)
    merged = jax.nn.sigmoid(mg) * bg + jax.nn.sigmoid(md) * bd
    x1 = x + gt_ref[...] * jnp.dot(merged.astype(BF16), wo_ref[...],
                                   preferred_element_type=F32)
    x1_ref[...] = x1
    hf = _rms_mod(x1, g2_ref[...], sc_ref[...], sh_ref[...])
    hfp_ref[...] = _pack_halves(hf)
    hf_scr[...] = hf
    _route_select(logits, i > 0, bias_ref, idx_ref, wts_ref, rnk_ref, cnt_ref, run_scr)


def _mergeout(og, od, x, g1, sca, sha, wmg, wmd, wbg, wbd, wo, gt, g2, sc, sh, wrt, bias_col,
              tm):
    s, d = x.shape
    e = wrt.shape[0]
    vec = pl.BlockSpec((1, d), lambda i: (0, 0))
    wspec = pl.BlockSpec((d, d), lambda i: (0, 0))
    n = s // tm
    row = pl.BlockSpec((tm, d), lambda i: (jnp.minimum(i, n - 1), 0))
    tok = pl.BlockSpec((TOP_K, tm), lambda i: (0, jnp.maximum(i - 1, 0)))
    return pl.pallas_call(
        _mergeout_kernel,
        grid=(n + 1,),
        in_specs=[row, row, row, vec, vec, vec, wspec, wspec, wspec, wspec, wspec,
                  vec, vec, vec, vec,
                  pl.BlockSpec((e, d), lambda i: (0, 0)), pl.BlockSpec((e, 1), lambda i: (0, 0))],
        out_specs=[row, pl.BlockSpec((tm, d // 2), lambda i: (jnp.minimum(i, n - 1), 0)),
                   tok, tok, tok, pl.BlockSpec((e, 128), lambda i: (0, 0))],
        out_shape=[jax.ShapeDtypeStruct((s, d), F32),
                   jax.ShapeDtypeStruct((s, d // 2), jnp.uint32),
                   jax.ShapeDtypeStruct((TOP_K, s), jnp.int32),
                   jax.ShapeDtypeStruct((TOP_K, s), F32),
                   jax.ShapeDtypeStruct((TOP_K, s), jnp.int32),
                   jax.ShapeDtypeStruct((e, 128), jnp.int32)],
        scratch_shapes=[pltpu.VMEM((e, 128), F32), pltpu.VMEM((tm, d), F32)],
        compiler_params=_cparams(("arbitrary",)),
        name="mergeout",
    )(og, od, x, g1, sca, sha, wmg, wmd, wbg, wbd, wo, gt, g2, sc, sh, wrt, bias_col)


def _route_select(logits, live, bias_ref, idx_ref, wts_ref, rnk_ref, cnt_ref, run_scr):
    e, tr = logits.shape
    gsz = e // N_GROUPS
    scores = jax.nn.sigmoid(logits)
    biased = scores + bias_ref[...]
    g3 = biased.reshape(N_GROUPS, gsz, tr)
    m1 = jnp.max(g3, axis=1, keepdims=True)
    n_top = jnp.sum(jnp.where(g3 == m1, 1.0, 0.0), axis=1, keepdims=True)
    m2 = jnp.max(jnp.where(g3 < m1, g3, -jnp.inf), axis=1, keepdims=True)
    gs = (m1 + jnp.where(n_top >= 2.0, m1, m2)).reshape(N_GROUPS, tr)
    gi = lax.broadcasted_iota(jnp.int32, (N_GROUPS, tr), 0)
    beaten = jnp.zeros((N_GROUPS, tr), F32)
    for g in range(N_GROUPS):
        other = gs[g:g + 1, :]
        beaten = beaten + jnp.where((other > gs) | ((other == gs) & (g < gi)), 1.0, 0.0)
    gsel = (beaten < float(TOPK_GROUPS)).reshape(N_GROUPS, 1, tr)
    masked = jnp.where(gsel, g3, -jnp.inf).reshape(e, tr)

    ids = lax.broadcasted_iota(jnp.int32, (e, tr), 0)
    chosen = jnp.zeros((e, tr), F32)
    sel_idx, sel_score = [], []
    for _ in range(TOP_K):
        mx = jnp.max(masked, axis=0, keepdims=True)
        ix = jnp.min(jnp.where(masked == mx, ids, e), axis=0, keepdims=True)
        hit = ids == ix
        sel_idx.append(ix)
        sel_score.append(jnp.sum(jnp.where(hit, scores, 0.0), axis=0, keepdims=True))
        chosen = jnp.where(hit, 1.0, chosen)
        masked = jnp.where(hit, -jnp.inf, masked)
    idx = jnp.concatenate(sel_idx, axis=0)
    sc = jnp.concatenate(sel_score, axis=0)
    idx_ref[...] = idx
    wts_ref[...] = sc / jnp.sum(sc, axis=0, keepdims=True) * ROUTED_SCALE

    row = lax.broadcasted_iota(jnp.int32, (tr, tr), 0)
    col = lax.broadcasted_iota(jnp.int32, (tr, tr), 1)
    before = jnp.where(row < col, 1.0, 0.0).astype(BF16)
    prior = jnp.dot(chosen.astype(BF16), before, preferred_element_type=F32) + run_scr[:, 0:1]
    rnk_ref[...] = jnp.concatenate(
        [jnp.sum(jnp.where(ids == sel_idx[k], prior, 0.0), axis=0, keepdims=True)
         for k in range(TOP_K)], axis=0).astype(jnp.int32)
    run_scr[...] = run_scr[...] + jnp.where(live, jnp.sum(chosen, axis=1, keepdims=True), 0.0)
    cnt_ref[...] = run_scr[...].astype(jnp.int32)


def _positions_kernel(idx_ref, rnk_ref, pstart_ref, pos_ref):
    e = pstart_ref.shape[0]
    ts = idx_ref.shape[1]
    ids = lax.broadcasted_iota(jnp.int32, (e, ts), 0)
    idx = idx_ref[...]
    pos_ref[...] = rnk_ref[...] + jnp.concatenate(
        [jnp.sum(jnp.where(ids == idx[k:k + 1, :], pstart_ref[...], 0), axis=0, keepdims=True)
         for k in range(TOP_K)], axis=0)


def _positions(idx, rnk, pstart_col, ts):
    s = idx.shape[1]
    e = pstart_col.shape[0]
    tok = pl.BlockSpec((TOP_K, ts), lambda i: (0, i))
    return pl.pallas_call(
        _positions_kernel,
        grid=(s // ts,),
        in_specs=[tok, tok, pl.BlockSpec((e, 1), lambda i: (0, 0))],
        out_specs=tok,
        out_shape=jax.ShapeDtypeStruct((TOP_K, s), jnp.int32),
        compiler_params=_cparams(("arbitrary",)),
        name="positions",
    )(idx, rnk, pstart_col)


def _swiglu_packed(xp, wg, wu, wd):
    lo, hi = _unpack_halves(xp)
    lo, hi = lo.astype(BF16), hi.astype(BF16)
    n = lo.shape[1]
    g = (jnp.dot(lo, wg[:n], preferred_element_type=F32)
         + jnp.dot(hi, wg[n:], preferred_element_type=F32))
    u = (jnp.dot(lo, wu[:n], preferred_element_type=F32)
         + jnp.dot(hi, wu[n:], preferred_element_type=F32))
    h = (g * jax.nn.sigmoid(g)) * u
    return jnp.dot(h.astype(BF16), wd[...], preferred_element_type=F32)


def _moe_kernel(ie_ref, ib_ref, first_ref, slot_ref, ne_ref, lead_ref, rows_ref, nv_ref, xs_ref,
                wg_hbm, wu_hbm, wd_hbm, ys_ref, wg_f, wu_f, wd_f, sem):
    del ib_ref
    i = pl.program_id(0)

    def fetch(e, slot):
        copies = []
        for n, (src, dst) in enumerate(((wg_hbm, wg_f), (wu_hbm, wu_f), (wd_hbm, wd_f))):
            rows = src.shape[1] // MOE_DMA_CHUNKS
            for ch in range(MOE_DMA_CHUNKS):
                part = pl.ds(ch * rows, rows)
                copies.append(pltpu.make_async_copy(
                    src.at[e, part], dst.at[slot, part], sem.at[slot, n * MOE_DMA_CHUNKS + ch]))
        return copies

    @pl.when(i == 0)
    def _():
        for s in range(MOE_SLOTS - 1):
            @pl.when(lead_ref[s] >= 0)
            def _(s=s):
                for cp in fetch(lead_ref[s], s):
                    cp.start()

    @pl.when(i < nv_ref[0])
    def _():
        for slot in range(MOE_SLOTS):
            @pl.when((first_ref[i] == 1) & (slot_ref[i] == slot))
            def _(slot=slot):
                for cp in fetch(ie_ref[i], slot):
                    cp.wait()

                @pl.when(ne_ref[i] >= 0)
                def _():
                    for cp in fetch(ne_ref[i], (slot + MOE_SLOTS - 1) % MOE_SLOTS):
                        cp.start()

        slot = slot_ref[i]
        row = lax.broadcasted_iota(jnp.int32, xs_ref.shape, 0)
        xp = jnp.where(row < rows_ref[i], xs_ref[...], jnp.uint32(0))
        ys_ref[...] = _pack_halves(_swiglu_packed(
            xp, wg_f[slot].astype(BF16), wu_f[slot].astype(BF16), wd_f[slot].astype(BF16)))


def _moe(item_e, item_b, item_first, item_slot, item_next, lead, item_rows, n_valid, xs, wg, wu,
         wd):
    m_pad, dh = xs.shape
    _, d, f = wg.shape
    n_items = item_e.shape[0]
    blk = lambda i, ie, ib, fi, sl, ne, ld, nr, nv: (ib[i], 0)
    hbm = pl.BlockSpec(memory_space=pl.ANY)
    return pl.pallas_call(
        _moe_kernel,
        grid_spec=pltpu.PrefetchScalarGridSpec(
            num_scalar_prefetch=8,
            grid=(n_items,),
            in_specs=[pl.BlockSpec((MOE_ROWS, dh), blk), hbm, hbm, hbm],
            out_specs=pl.BlockSpec((MOE_ROWS, dh), blk),
            scratch_shapes=[pltpu.VMEM((MOE_SLOTS, d, f), F32), pltpu.VMEM((MOE_SLOTS, d, f), F32),
                            pltpu.VMEM((MOE_SLOTS, f, d), F32),
                            pltpu.SemaphoreType.DMA((MOE_SLOTS, 3 * MOE_DMA_CHUNKS))],
        ),
        out_shape=jax.ShapeDtypeStruct((m_pad, dh), jnp.uint32),
        compiler_params=_cparams(("arbitrary",)),
        name="moe",
    )(item_e, item_b, item_first, item_slot, item_next, lead, item_rows, n_valid, xs, wg, wu, wd)


def _sc_gather_rows(table, idx_row):
    m = idx_row.shape[1]
    w = table.shape[1]
    idx_row = idx_row.reshape(m // SC_GATHER_WINDOW, SC_GATHER_WINDOW)
    mesh = plsc.VectorSubcoreMesh(core_axis_name="c", subcore_axis_name="s")

    @functools.partial(pl.kernel, mesh=mesh,
                       out_type=jax.ShapeDtypeStruct((m, w), table.dtype))
    def gather(table_hbm, idx_hbm, out_hbm):
        def body(idx_vmem, out_vmem):
            pltpu.sync_copy(table_hbm.at[idx_vmem.at[0]], out_vmem)

        pltpu.emit_pipeline(
            body,
            grid=(m // SC_GATHER_WINDOW,),
            in_specs=[pl.BlockSpec((1, SC_GATHER_WINDOW), lambda i: (i, 0))],
            out_specs=[pl.BlockSpec((SC_GATHER_WINDOW, w), lambda i: (i, 0))],
            core_axis_name=("c", "s"),
            dimension_semantics=(pltpu.PARALLEL,),
        )(idx_hbm, out_hbm)

    return gather(table, idx_row)


def _sc_scatter_rows(rows, idx_blocks, m_out):
    s, w = rows.shape
    mesh = plsc.VectorSubcoreMesh(core_axis_name="c", subcore_axis_name="s")

    @functools.partial(pl.kernel, mesh=mesh,
                       out_type=jax.ShapeDtypeStruct((m_out, w), rows.dtype))
    def scatter(rows_hbm, idx_hbm, out_hbm):
        def body(rows_vmem, idx_vmem):
            for k in range(TOP_K):
                pltpu.sync_copy(rows_vmem, out_hbm.at[idx_vmem.at[k]])

        pltpu.emit_pipeline(
            body,
            grid=(s // SC_GATHER_WINDOW,),
            in_specs=[pl.BlockSpec((SC_GATHER_WINDOW, w), lambda i: (i, 0)),
                      pl.BlockSpec((TOP_K, SC_GATHER_WINDOW), lambda i: (i, 0))],
            out_specs=[],
            core_axis_name=("c", "s"),
            dimension_semantics=(pltpu.PARALLEL,),
        )(rows_hbm, idx_hbm)

    return scatter(rows, idx_blocks)


def _combine_kernel(wt_ref, hf_ref, x1_ref, gt_ref, sg_ref, su_ref, sd_ref, g_ref, o_ref):
    tc = x1_ref.shape[0]
    y = _swiglu_packed(hf_ref[...], sg_ref, su_ref, sd_ref)
    wt = wt_ref[...]
    n = g_ref.shape[2]
    r_lo = jnp.zeros((tc, n), F32)
    r_hi = jnp.zeros((tc, n), F32)
    for k in range(TOP_K):
        lo, hi = _unpack_halves(g_ref[k])
        r_lo = r_lo + lo * wt[:, k:k + 1]
        r_hi = r_hi + hi * wt[:, k:k + 1]
    y = y + jnp.concatenate([r_lo, r_hi], axis=1)
    o_ref[...] = x1_ref[...] + gt_ref[...] * y


def _combine(wts_t, hfp, x1, gt, sg, su, sd, gathered, tc):
    s, d = x1.shape
    f = sg.shape[1]
    row = pl.BlockSpec((tc, d), lambda i: (i, 0))
    return pl.pallas_call(
        _combine_kernel,
        grid=(s // tc,),
        in_specs=[pl.BlockSpec((tc, TOP_K), lambda i: (i, 0)),
                  pl.BlockSpec((tc, d // 2), lambda i: (i, 0)), row,
                  pl.BlockSpec((1, d), lambda i: (0, 0)),
                  pl.BlockSpec((d, f), lambda i: (0, 0)),
                  pl.BlockSpec((d, f), lambda i: (0, 0)),
                  pl.BlockSpec((f, d), lambda i: (0, 0)),
                  pl.BlockSpec((TOP_K, tc, d // 2), lambda i: (0, i, 0))],
        out_specs=row,
        out_shape=jax.ShapeDtypeStruct((s, d), F32),
        compiler_params=_cparams(("arbitrary",)),
        name="combine",
    )(wts_t, hfp, x1, gt, sg, su, sd, gathered)


def _tile(n, want):
    t = min(n, want)
    assert n % t == 0, (n, t)
    return t


def _layer(l, x, c_col, pos_row, p):
    s, d = x.shape
    lambda_init = 0.8 - 0.6 * math.exp(-0.3 * l)
    gqk, gv = GLA_HEADS * GLA_DK, GLA_HEADS * GLA_DV
    dqk, dvw = DIFF_HEADS * 2 * DIFF_DH, DIFF_HEADS * DIFF_DV
    lowrank = p["gla_w_a2"].shape[0]

    mod = _ada(c_col, p["w_ada"], p["b_ada"][None, :])
    sh_a, sc_a, gt_a, sh_f, sc_f, gt_f = [mod[:, j * d:(j + 1) * d] for j in range(6)]

    w_in = p["w_in"]
    o = 0
    cols = {}
    for name, wdt in (("gq", gqk), ("gk", gqk), ("gv", gv), ("ga", lowrank), ("gg", gv),
                      ("dq", dqk), ("dk", dqk), ("dv", dvw), ("mg", d), ("md", d)):
        cols[name] = w_in[:, o:o + wdt]
        o += wdt
    w_ga = jnp.pad(cols["ga"], ((0, 0), (0, 128 - lowrank))).astype(BF16)

    g1 = p["norm1_g"][None, :]
    ts = _tile(s, TILE_SEQ)

    invf = ROPE_THETA ** (-jnp.arange(0, ROT_DIM, 2, dtype=F32) / ROT_DIM)
    qat, qbt, kr, vte, qn, kn, gkt = _qkvprep(
        x, g1, sc_a, sh_a, *(cols[n].T.astype(BF16) for n in ("dq", "dk", "dv", "gk")),
        pos_row, invf[:, None], p["diff_qnorm_g"][:, None], p["diff_knorm_g"][:, None], ts)

    wa2t = jnp.pad(p["gla_w_a2"].T, ((0, 0), (0, 128 - lowrank)))
    o_gla = _gla(x, g1, sc_a, sh_a, cols["gq"].astype(BF16), cols["gv"].astype(BF16),
                 cols["gg"].astype(BF16), w_ga, gkt, wa2t, p["gla_b_a"][:, None],
                 p["gla_onorm_g"][None, :], ts)
    o_diff = _diffattn(qat, qbt, kr, vte, qn, kn, p["diff_lq1"][None, :], p["diff_lk1"][None, :],
                       p["diff_lq2"][None, :], p["diff_lk2"][None, :],
                       p["diff_subln_g"][:, None], lambda_init, _tile(s, TILE_ATT_Q), ts)

    e = p["w_router"].shape[1]
    x1, hfp, idx, wts, rnk, cnt = _mergeout(
        o_gla, o_diff, x, g1, sc_a, sh_a, cols["mg"].astype(BF16), cols["md"].astype(BF16),
        p["w_branch_gla"].astype(BF16), p["w_branch_diff"].astype(BF16),
        p["w_out"].astype(BF16), gt_a, p["norm2_g"][None, :], sc_f, sh_f,
        p["w_router"].T, p["router_bias"][:, None], ts)

    counts = cnt[:, 0]
    pcounts = ((counts + MOE_ROWS - 1) // MOE_ROWS) * MOE_ROWS
    pend = jnp.cumsum(pcounts)
    pstart = pend - pcounts
    pos = _positions(idx, rnk, pstart[:, None], ts)
    n_items = (s * TOP_K) // MOE_ROWS + e
    n_valid = (pend[-1] // MOE_ROWS).astype(jnp.int32)
    item_b = jnp.minimum(jnp.arange(n_items, dtype=jnp.int32), n_valid - 1)
    item_e = jnp.minimum(jnp.sum(pend[None, :] <= (item_b * MOE_ROWS)[:, None], axis=1),
                         e - 1).astype(jnp.int32)

    wn = SC_GATHER_WINDOW
    pos_w = pos.reshape(TOP_K, s // wn, wn).transpose(1, 0, 2).reshape(s // wn * TOP_K, wn)
    xs = _sc_scatter_rows(hfp, pos_w, n_items * MOE_ROWS)
    item_rows = jnp.clip(pstart[item_e] + counts[item_e] - item_b * MOE_ROWS, 0,
                         MOE_ROWS).astype(jnp.int32)
    prev_e = jnp.concatenate([jnp.full((1,), -1, jnp.int32), item_e[:-1]])
    item_first = ((jnp.arange(n_items) < n_valid) & (item_e != prev_e)).astype(jnp.int32)
    item_slot = ((jnp.cumsum(item_first) - 1) % MOE_SLOTS).astype(jnp.int32)
    cand = jnp.where(pcounts > 0, jnp.arange(e, dtype=jnp.int32), e)
    nonempty_from = lax.cummin(cand[::-1])[::-1]
    following = jnp.concatenate([nonempty_from[1:], jnp.full((2,), e, jnp.int32)])
    ahead = item_e
    lead = [nonempty_from[0]]
    for _ in range(MOE_SLOTS - 1):
        ahead = following[ahead]
        lead.append(following[lead[-1]])
    item_next = jnp.where(ahead < e, ahead, -1).astype(jnp.int32)
    lead = jnp.stack(lead[:MOE_SLOTS - 1])
    lead = jnp.where(lead < e, lead, -1).astype(jnp.int32)
    ys = _moe(item_e, item_b, item_first, item_slot, item_next, lead, item_rows, n_valid[None],
              xs, p["w_exp_gate"], p["w_exp_up"], p["w_exp_down"])
    gathered = _sc_gather_rows(ys, pos.reshape(1, TOP_K * s)).reshape(TOP_K, s, d // 2)
    return _combine(wts.T, hfp, x1, gt_f, p["w_sh_gate"].astype(BF16),
                    p["w_sh_up"].astype(BF16), p["w_sh_down"].astype(BF16), gathered,
                    _tile(s, TILE_COMBINE))


_LAYER_PARAMS = ("w_ada", "b_ada", "norm1_g", "w_in", "gla_w_a2", "gla_b_a", "gla_onorm_g",
                 "diff_qnorm_g", "diff_knorm_g", "diff_lq1", "diff_lk1", "diff_lq2", "diff_lk2",
                 "diff_subln_g", "w_branch_gla", "w_branch_diff", "w_out", "norm2_g", "w_router",
                 "router_bias", "w_exp_gate", "w_exp_up", "w_exp_down", "w_sh_gate", "w_sh_up",
                 "w_sh_down")


def kernel(x, c, positions, w_ada, b_ada, norm1_g, w_in, gla_w_a2, gla_b_a, gla_onorm_g, diff_qnorm_g, diff_knorm_g, diff_lq1, diff_lk1, diff_lq2, diff_lk2, diff_subln_g, w_branch_gla, w_branch_diff, w_out, norm2_g, w_router, router_bias, w_exp_gate, w_exp_up, w_exp_down, w_sh_gate, w_sh_up, w_sh_down):
    stacked = dict(zip(_LAYER_PARAMS, (
        w_ada, b_ada, norm1_g, w_in, gla_w_a2, gla_b_a, gla_onorm_g, diff_qnorm_g, diff_knorm_g,
        diff_lq1, diff_lk1, diff_lq2, diff_lk2, diff_subln_g, w_branch_gla, w_branch_diff, w_out,
        norm2_g, w_router, router_bias, w_exp_gate, w_exp_up, w_exp_down, w_sh_gate, w_sh_up,
        w_sh_down)))
    b, s, d = x.shape
    assert b == 1, "single-sequence kernel"
    xl = x[0]
    c_col = c[0][:, None]
    pos_row = positions.astype(jnp.int32)
    for l in range(w_ada.shape[0]):
        xl = _layer(l, xl, c_col, pos_row, {k: v[l] for k, v in stacked.items()})
    return xl[None]
```

```python
import functools
import math

import jax
import jax.numpy as jnp
from jax import lax
from jax.experimental import pallas as pl
from jax.experimental.pallas import tpu as pltpu
from jax.experimental.pallas import tpu_sc as plsc

CHUNK = 64
EPS = 1e-6
GLA_HEADS = 4
GLA_DK = 128
GLA_DV = 256
GLA_TAU = 16.0
DIFF_HEADS = 8
DIFF_DH = 64
DIFF_DV = 2 * DIFF_DH
ROPE_THETA = 500000.0
ROT_DIM = DIFF_DH // 4
N_GROUPS = 8
TOPK_GROUPS = 4
TOP_K = 8
ROUTED_SCALE = 2.5

MOE_ROWS = 640
MOE_SLOTS = 3
SC_GATHER_WINDOW = 64
VMEM_LIMIT = 56 * 1024 * 1024
TILE_SEQ = 512
TILE_ATT_Q = 2048
TILE_COMBINE = 512
NEG_BIG = -1e30
LOG2E = 1.4426950408889634
F32 = jnp.float32
BF16 = jnp.bfloat16


def _cparams(sem):
    return pltpu.CompilerParams(dimension_semantics=sem, vmem_limit_bytes=VMEM_LIMIT)


def _nt_dot(a, b):
    return lax.dot_general(a, b, (((1,), (1,)), ((), ())), preferred_element_type=F32)


def _pack_halves(x):
    n = x.shape[1] // 2
    lo = pltpu.bitcast(x[:, :n].astype(BF16).astype(F32), jnp.uint32) >> 16
    hi = pltpu.bitcast(x[:, n:].astype(BF16).astype(F32), jnp.uint32) & jnp.uint32(0xFFFF0000)
    return lo | hi


def _unpack_halves(w):
    return (pltpu.bitcast(w << 16, F32), pltpu.bitcast(w & jnp.uint32(0xFFFF0000), F32))


def _split_bf16(a):
    hi = a.astype(BF16)
    return hi, (a - hi.astype(F32)).astype(BF16)


def _rms_mod(x, g, sc, sh):
    xn = x * lax.rsqrt(jnp.mean(x * x, axis=-1, keepdims=True) + EPS)
    return (xn * g) * (1.0 + sc) + sh


def _ada_kernel(c_ref, w_ref, b_ref, o_ref):
    c = c_ref[...]
    ca = c * jax.nn.sigmoid(c)
    o_ref[...] = jnp.sum(ca * w_ref[...], axis=0, keepdims=True) + b_ref[...]


def _ada(c_col, w, b):
    d, n = w.shape
    tn = min(1024, n)
    return pl.pallas_call(
        _ada_kernel,
        grid=(n // tn,),
        in_specs=[pl.BlockSpec((d, 1), lambda j: (0, 0)),
                  pl.BlockSpec((d, tn), lambda j: (0, j)),
                  pl.BlockSpec((1, tn), lambda j: (0, j))],
        out_specs=pl.BlockSpec((1, tn), lambda j: (0, j)),
        out_shape=jax.ShapeDtypeStruct((1, n), F32),
        compiler_params=_cparams(("arbitrary",)),
        name="ada",
    )(c_col, w, b)


def _gla_kernel(x_ref, g1_ref, sc_ref, sh_ref, wq_ref, wv_ref, wg_ref, wga_ref, kt_ref, wa2t_ref,
                ba_ref, on_ref, o_ref, state_ref, o_scr, q_ref, v_ref, gg_ref):
    tt = x_ref.shape[0]
    nchunk = tt // CHUNK

    @pl.when(pl.program_id(0) == 0)
    def _():
        state_ref[...] = jnp.zeros_like(state_ref)

    h_in = _rms_mod(x_ref[...], g1_ref[...], sc_ref[...], sh_ref[...]).astype(BF16)
    q_ref[...] = jnp.dot(h_in, wq_ref[...], preferred_element_type=F32).astype(BF16)
    v_ref[...] = jnp.dot(h_in, wv_ref[...], preferred_element_type=F32).astype(BF16)
    gg_ref[...] = jnp.dot(h_in, wg_ref[...], preferred_element_type=F32).astype(BF16)
    ga = jnp.dot(h_in, wga_ref[...], preferred_element_type=F32)

    a_hi, a_lo = _split_bf16(wa2t_ref[...])
    g_hi, g_lo = _split_bf16(ga)
    zt = _nt_dot(a_hi, g_hi) + _nt_dot(a_hi, g_lo) + _nt_dot(a_lo, g_hi) + ba_ref[...]
    lat = (jnp.minimum(zt, 0.0) - jnp.log1p(jnp.exp(-jnp.abs(zt)))) * (1.0 / GLA_TAU)
    row = lax.broadcasted_iota(jnp.int32, (tt, tt), 0)
    col = lax.broadcasted_iota(jnp.int32, (tt, tt), 1)
    same = (row // CHUNK) == (col // CHUNK)
    incl = jnp.where(same & (row <= col), 1.0, 0.0).astype(BF16)
    full = jnp.where(same, 1.0, 0.0).astype(BF16)
    lat_hi, lat_lo = _split_bf16(lat)
    cumt = (jnp.dot(lat_hi, incl, preferred_element_type=F32)
            + jnp.dot(lat_lo, incl, preferred_element_type=F32))
    tott = (jnp.dot(lat_hi, full, preferred_element_type=F32)
            + jnp.dot(lat_lo, full, preferred_element_type=F32))
    kdt = kt_ref[...].astype(F32) * jnp.exp(tott - cumt)
    dec = jnp.exp(tott)

    lane = lax.broadcasted_iota(jnp.int32, (GLA_DK, 2 * CHUNK), 1)
    upd = {}
    for c in range(nchunk):
        pair = (c // 2) * 2 * CHUNK
        if nchunk > 1:
            keep = (lane // CHUNK) == (c % 2)
        for h in range(GLA_HEADS):
            rows = slice(h * GLA_DK, (h + 1) * GLA_DK)
            vcols = slice(h * GLA_DV, (h + 1) * GLA_DV)
            if nchunk > 1:
                a = jnp.where(keep, kdt[rows, pair:pair + 2 * CHUNK], 0.0).astype(BF16)
                vp = v_ref[pair:pair + 2 * CHUNK, vcols]
            else:
                a = kdt[rows, :].astype(BF16)
                vp = v_ref[:, vcols]
            upd[c, h] = jnp.dot(a, vp, preferred_element_type=F32)

    for h in range(GLA_HEADS):
        rows = slice(h * GLA_DK, (h + 1) * GLA_DK)
        vcols = slice(h * GLA_DV, (h + 1) * GLA_DV)
        st = state_ref[h]
        states = []
        for c in range(nchunk):
            st = st * dec[rows, c * CHUNK:c * CHUNK + 1] + upd[c, h]
            states.append(st.astype(BF16))
        state_ref[h] = st
        for c in range(nchunk):
            o_scr[c * CHUNK:(c + 1) * CHUNK, vcols] = jnp.dot(
                q_ref[c * CHUNK:(c + 1) * CHUNK, rows], states[c], preferred_element_type=F32)

    for h in range(GLA_HEADS):
        vcols = slice(h * GLA_DV, (h + 1) * GLA_DV)
        o = o_scr[:, vcols] * (GLA_DK ** -0.5)
        o = o * lax.rsqrt(jnp.mean(o * o, axis=-1, keepdims=True) + EPS) * on_ref[...]
        g = gg_ref[:, vcols].astype(F32)
        o_ref[:, vcols] = (o * (g * jax.nn.sigmoid(g))).astype(BF16)


def _gla(x, g1, sc, sh, wq, wv, wg, wga, gkt, wa2t, ba_col, on_g, tt):
    s, d = x.shape
    qk = GLA_HEADS * GLA_DK
    vw = GLA_HEADS * GLA_DV
    vec = pl.BlockSpec((1, d), lambda i: (0, 0))
    whole = lambda a: pl.BlockSpec(a.shape, lambda i: (0, 0))
    return pl.pallas_call(
        _gla_kernel,
        grid=(s // tt,),
        in_specs=[pl.BlockSpec((tt, d), lambda i: (i, 0)), vec, vec, vec,
                  whole(wq), whole(wv), whole(wg), whole(wga),
                  pl.BlockSpec((qk, tt), lambda i: (0, i)),
                  whole(wa2t), whole(ba_col), whole(on_g)],
        out_specs=pl.BlockSpec((tt, vw), lambda i: (i, 0)),
        out_shape=jax.ShapeDtypeStruct((s, vw), BF16),
        scratch_shapes=[pltpu.VMEM((GLA_HEADS, GLA_DK, GLA_DV), F32),
                        pltpu.VMEM((tt, vw), F32),
                        pltpu.VMEM((tt, qk), BF16), pltpu.VMEM((tt, vw), BF16),
                        pltpu.VMEM((tt, vw), BF16)],
        compiler_params=_cparams(("arbitrary",)),
        name="gla",
    )(x, g1, sc, sh, wq, wv, wg, wga, gkt, wa2t, ba_col, on_g)


def _qknorm_rope_t(xt, g_col, cos, sin):
    n, tm = xt.shape
    x3 = xt.reshape(n // DIFF_DH, DIFF_DH, tm)
    r = lax.rsqrt(jnp.mean(x3 * x3, axis=1, keepdims=True) + EPS)
    y = x3 * r * g_col[None]
    half = ROT_DIM // 2
    y1, y2, rest = y[:, :half], y[:, half:ROT_DIM], y[:, ROT_DIM:]
    o1 = y1 * cos[None] - y2 * sin[None]
    o2 = y2 * cos[None] + y1 * sin[None]
    return jnp.concatenate([o1, o2, rest], axis=1)


def _seg_norms(x3):
    return jnp.sqrt(jnp.sum(x3 * x3, axis=1)).reshape(DIFF_HEADS, 2, x3.shape[2])


def _qkvprep_kernel(x_ref, g_ref, sc_ref, sh_ref, wq_ref, wk_ref, wv_ref, wgk_ref, pos_ref,
                    invf_ref, qg_ref, kg_ref, qa_ref, qb_ref, ko_ref, ve_ref, qn_ref, kn_ref,
                    gk_ref):
    tm = x_ref.shape[0]
    h = _rms_mod(x_ref[...], g_ref[...], sc_ref[...], sh_ref[...]).astype(BF16)
    qt = _nt_dot(wq_ref[...], h)
    kt = _nt_dot(wk_ref[...], h)
    vt = _nt_dot(wv_ref[...], h)
    gk_ref[...] = _nt_dot(wgk_ref[...], h).astype(BF16)
    ang = pos_ref[...].astype(F32) * invf_ref[...]
    cos, sin = jnp.cos(ang), jnp.sin(ang)
    q3 = _qknorm_rope_t(qt, qg_ref[...], cos, sin) * (DIFF_DH ** -0.5 * LOG2E)
    qn_ref[...] = _seg_norms(q3)
    seg = lax.broadcasted_iota(jnp.int32, q3.shape, 0)
    qa_ref[...] = jnp.where(seg % 2 == 0, q3, 0.0).reshape(-1, tm).astype(BF16)
    qb_ref[...] = jnp.where(seg % 2 == 1, q3, 0.0).reshape(-1, tm).astype(BF16)
    k3 = _qknorm_rope_t(kt, kg_ref[...], cos, sin)
    ko_ref[...] = k3.reshape(-1, tm).T.astype(BF16)
    kn_ref[...] = _seg_norms(k3)
    v3 = vt.astype(BF16).reshape(DIFF_HEADS, DIFF_DV, tm)
    ones = jnp.ones((DIFF_HEADS, ATT_SUM_ROWS, tm), BF16)
    ve_ref[...] = jnp.concatenate([v3, ones], axis=1).reshape(-1, tm)


def _qkvprep(x, g, sc, sh, wq_t, wk_t, wv_t, wgk_t, pos_row, invf_col, qg_col, kg_col, tm):
    s, d = x.shape
    n = DIFF_HEADS * 2 * DIFF_DH
    ne = DIFF_HEADS * (DIFF_DV + ATT_SUM_ROWS)
    ngk = wgk_t.shape[0]
    col = pl.BlockSpec((DIFF_DH, 1), lambda i: (0, 0))
    vec = pl.BlockSpec((1, d), lambda i: (0, 0))
    wspec = pl.BlockSpec((n, d), lambda i: (0, 0))
    return pl.pallas_call(
        _qkvprep_kernel,
        grid=(s // tm,),
        in_specs=[pl.BlockSpec((tm, d), lambda i: (i, 0)), vec, vec, vec,
                  wspec, wspec, wspec, pl.BlockSpec((ngk, d), lambda i: (0, 0)),
                  pl.BlockSpec((1, tm), lambda i: (0, i)),
                  pl.BlockSpec((ROT_DIM // 2, 1), lambda i: (0, 0)), col, col],
        out_specs=[pl.BlockSpec((n, tm), lambda i: (0, i)),
                   pl.BlockSpec((n, tm), lambda i: (0, i)),
                   pl.BlockSpec((tm, n), lambda i: (i, 0)),
                   pl.BlockSpec((ne, tm), lambda i: (0, i)),
                   pl.BlockSpec((DIFF_HEADS, 2, tm), lambda i: (0, 0, i)),
                   pl.BlockSpec((DIFF_HEADS, 2, tm), lambda i: (0, 0, i)),
                   pl.BlockSpec((ngk, tm), lambda i: (0, i))],
        out_shape=[jax.ShapeDtypeStruct((n, s), BF16), jax.ShapeDtypeStruct((n, s), BF16),
                   jax.ShapeDtypeStruct((s, n), BF16), jax.ShapeDtypeStruct((ne, s), BF16),
                   jax.ShapeDtypeStruct((DIFF_HEADS, 2, s), F32),
                   jax.ShapeDtypeStruct((DIFF_HEADS, 2, s), F32),
                   jax.ShapeDtypeStruct((ngk, s), BF16)],
        compiler_params=_cparams(("arbitrary",)),
        name="qkvprep",
    )(x, g, sc, sh, wq_t, wk_t, wv_t, wgk_t, pos_row, invf_col, qg_col, kg_col)


ATT_COLS = 256
ATT_LOOKAHEAD = {True: 2, False: 4}
ATT_BODY_ITEMS = {True: 64, False: 8}
ATT_SUM_ROWS = 16
ATT_BOUND_SLACK = 1.02
ATT_BOUND_LIMIT = 50.0


def _diffattn_kernel(qa_ref, qb_ref, k_ref, vt_ref, qn_ref, kn_ref, lq1_ref, lk1_ref, lq2_ref,
                     lk2_ref, sg_ref, o_ref, *scr, lambda_init, tk, cols):
    tq = qa_ref.shape[1]
    nblk = 2 * tq // cols
    m_scr, acc_scr = (scr[b * nblk:(b + 1) * nblk] for b in range(2))
    kmax_scr = scr[2 * nblk]
    s_scr = scr[2 * nblk + 1:]
    i = pl.program_id(1)

    @pl.when(i == 0)
    def _():
        kmax_scr[...] = jnp.max(kn_ref[...], axis=1, keepdims=True)

    bound = qn_ref[...] * kmax_scr[...] * ATT_BOUND_SLACK
    bound = jnp.concatenate([bound[0:1], bound[1:2]], axis=1)
    bounded = jnp.max(bound) < ATT_BOUND_LIMIT
    for c in range(nblk):
        m_scr[c][...] = jnp.where(bounded, bound[:, c * cols:(c + 1) * cols], NEG_BIG)
        acc_scr[c][...] = jnp.zeros_like(acc_scr[c])

    def scores(j, c):
        start = pl.multiple_of(j * tk, tk)
        q_ref = qa_ref if c * cols < tq else qb_ref
        off = (c * cols) % tq
        return jnp.dot(k_ref[pl.ds(start, tk), :], q_ref[:, off:off + cols],
                       preferred_element_type=F32)

    def steps(items, next_tile, fixed):
        look = ATT_LOOKAHEAD[fixed]
        pending = []
        for n, (j, c, keys) in enumerate(items):
            s = s_scr[n][...] if n < look else pending.pop(0)
            ahead = n + look
            if ahead < len(items):
                pending.append(scores(*items[ahead][:2]))
            elif next_tile is not None:
                s_scr[ahead - len(items)][...] = scores(next_tile, ahead - len(items))
            start = pl.multiple_of(j * tk, tk)
            if keys is None:
                keys = tk
            else:
                s = s[:keys]
                krow = lax.broadcasted_iota(jnp.int32, (keys, cols), 0)
                qcol = lax.broadcasted_iota(jnp.int32, (keys, cols), 1)
                qpos = i * tq + (c * cols) % tq + qcol
                s = jnp.where((start + krow) // CHUNK <= qpos // CHUNK, s, NEG_BIG)
            vj = vt_ref[:, pl.ds(start, keys)]
            if fixed:
                p = jnp.exp2(s - m_scr[c][...])
                acc_scr[c][:DIFF_DV] += jnp.dot(vj[:DIFF_DV], p.astype(BF16),
                                                preferred_element_type=F32)
                acc_scr[c][DIFF_DV:DIFF_DV + 1] += jnp.sum(p, axis=0, keepdims=True)
            else:
                m_prev = m_scr[c][...]
                m_new = jnp.maximum(m_prev, jnp.max(s, axis=0, keepdims=True))
                alpha = jnp.exp2(m_prev - m_new)
                p = jnp.exp2((s - m_new).astype(BF16))
                acc_scr[c][...] = alpha * acc_scr[c][...] + jnp.dot(
                    vj, p, preferred_element_type=F32)
                m_scr[c][...] = m_new

    def past(first, count):
        return [(first + u, c, None) for u in range(count) for c in range(nblk)]

    def attend(fixed):
        n_past = i * (tq // tk)
        for c in range(ATT_LOOKAHEAD[fixed]):
            s_scr[c][...] = scores(0, c)
        big = max(1, ATT_BODY_ITEMS[fixed] // nblk)
        lax.fori_loop(0, n_past // big, lambda t, c: (steps(
            past(big * t, big), big * t + big, fixed), c)[1], 0)
        size = big // 2
        while size >= 1:
            first = (n_past // (2 * size)) * (2 * size)

            @pl.when(n_past % (2 * size) >= size)
            def _(first=first, size=size):
                steps(past(first, size), first + size, fixed)

            size //= 2
        diagonal = []
        for d in range(tq // tk):
            for c in range(nblk):
                off = (c * cols) % tq
                visible = off + cols - d * tk
                if visible > 0:
                    diagonal.append((n_past + d, c, None if off >= (d + 1) * tk
                                     else min(tk, visible)))
        steps(diagonal, None, fixed)

    @pl.when(bounded)
    def _():
        attend(True)

    @pl.when(jnp.logical_not(bounded))
    def _():
        attend(False)

    o = jnp.concatenate([acc_scr[c][:DIFF_DV] * (1.0 / acc_scr[c][DIFF_DV:DIFF_DV + 1])
                         for c in range(nblk)], axis=1)
    lam = (jnp.exp(jnp.sum(lq1_ref[...] * lk1_ref[...]))
           - jnp.exp(jnp.sum(lq2_ref[...] * lk2_ref[...])) + lambda_init)
    o = o[:, :tq] - lam * o[:, tq:]
    o = o * lax.rsqrt(jnp.mean(o * o, axis=0, keepdims=True) + EPS) * sg_ref[...]
    o_ref[...] = (o * (1.0 - lambda_init)).T.astype(BF16)


def _diffattn(qat, qbt, kr, vte, qn, kn, lq1, lk1, lq2, lk2, sg_col, lambda_init, tq, tk):
    s = kr.shape[0]
    hd = 2 * DIFF_DH
    vec = pl.BlockSpec((1, DIFF_DH), lambda h, i: (0, 0))
    cols = min(ATT_COLS, tq)
    nblk = 2 * tq // cols
    kern = functools.partial(_diffattn_kernel, lambda_init=lambda_init, tk=tk, cols=cols)
    dve = DIFF_DV + ATT_SUM_ROWS
    assert tq % tk == 0 and nblk >= max(ATT_LOOKAHEAD.values())
    scratch = ([pltpu.VMEM((1, cols), F32)] * nblk + [pltpu.VMEM((dve, cols), F32)] * nblk
               + [pltpu.VMEM((2, 1), F32)]
               + [pltpu.VMEM((tk, cols), F32)] * max(ATT_LOOKAHEAD.values()))
    return pl.pallas_call(
        kern,
        grid=(DIFF_HEADS, s // tq),
        in_specs=[pl.BlockSpec((hd, tq), lambda h, i: (h, i)),
                  pl.BlockSpec((hd, tq), lambda h, i: (h, i)),
                  pl.BlockSpec((s, hd), lambda h, i: (0, h)),
                  pl.BlockSpec((dve, s), lambda h, i: (h, 0)),
                  pl.BlockSpec((None, 2, tq), lambda h, i: (h, 0, i)),
                  pl.BlockSpec((None, 2, s), lambda h, i: (h, 0, 0)),
                  vec, vec, vec, vec,
                  pl.BlockSpec((DIFF_DV, 1), lambda h, i: (0, 0))],
        out_specs=pl.BlockSpec((tq, DIFF_DV), lambda h, i: (i, h)),
        out_shape=jax.ShapeDtypeStruct((s, DIFF_HEADS * DIFF_DV), BF16),
        scratch_shapes=scratch,
        compiler_params=_cparams(("arbitrary", "arbitrary")),
        name="diffattn",
    )(qat, qbt, kr, vte, qn, kn, lq1, lk1, lq2, lk2, sg_col)


def _mergeout_kernel(og_ref, od_ref, x_ref, g1_ref, sca_ref, sha_ref, wmg_ref, wmd_ref, wbg_ref,
                     wbd_ref, wo_ref, gt_ref, g2_ref, sc_ref, sh_ref, wrt_ref, bias_ref,
                     x1_ref, hfp_ref, idx_ref, wts_ref, rnk_ref, cnt_ref, run_scr, hf_scr):
    i = pl.program_id(0)

    @pl.when(i == 0)
    def _():
        hf_scr[...] = jnp.zeros_like(hf_scr)
        run_scr[...] = jnp.zeros_like(run_scr)

    w_hi, w_lo = _split_bf16(wrt_ref[...])
    h_hi, h_lo = _split_bf16(hf_scr[...])
    logits = _nt_dot(w_hi, h_hi) + _nt_dot(w_hi, h_lo) + _nt_dot(w_lo, h_hi)
    x = x_ref[...]
    h_in = _rms_mod(x, g1_ref[...], sca_ref[...], sha_ref[...]).astype(BF16)
    mg = jnp.dot(h_in, wmg_ref[...], preferred_element_type=F32)
    md = jnp.dot(h_in, wmd_ref[...], preferred_element_type=F32)
    bg = jnp.dot(og_ref[...], wbg_ref[...], preferred_element_type=F32)
    bd = jnp.dot(od_ref[...], wbd_ref[...], preferred_element_type=F32)
    merged = jax.nn.sigmoid(mg) * bg + jax.nn.sigmoid(md) * bd
    x1 = x + gt_ref[...] * jnp.dot(merged.astype(BF16), wo_ref[...],
                                   preferred_element_type=F32)
    x1_ref[...] = x1
    hf = _rms_mod(x1, g2_ref[...], sc_ref[...], sh_ref[...])
    hfp_ref[...] = _pack_halves(hf)
    hf_scr[...] = hf
    _route_select(logits, i > 0, bias_ref, idx_ref, wts_ref, rnk_ref, cnt_ref, run_scr)


def _mergeout(og, od, x, g1, sca, sha, wmg, wmd, wbg, wbd, wo, gt, g2, sc, sh, wrt, bias_col,
              tm):
    s, d = x.shape
    e = wrt.shape[0]
    vec = pl.BlockSpec((1, d), lambda i: (0, 0))
    wspec = pl.BlockSpec((d, d), lambda i: (0, 0))
    n = s // tm
    row = pl.BlockSpec((tm, d), lambda i: (jnp.minimum(i, n - 1), 0))
    tok = pl.BlockSpec((TOP_K, tm), lambda i: (0, jnp.maximum(i - 1, 0)))
    return pl.pallas_call(
        _mergeout_kernel,
        grid=(n + 1,),
        in_specs=[row, row, row, vec, vec, vec, wspec, wspec, wspec, wspec, wspec,
                  vec, vec, vec, vec,
                  pl.BlockSpec((e, d), lambda i: (0, 0)), pl.BlockSpec((e, 1), lambda i: (0, 0))],
        out_specs=[row, pl.BlockSpec((tm, d // 2), lambda i: (jnp.minimum(i, n - 1), 0)),
                   tok, tok, tok, pl.BlockSpec((e, 128), lambda i: (0, 0))],
        out_shape=[jax.ShapeDtypeStruct((s, d), F32),
                   jax.ShapeDtypeStruct((s, d // 2), jnp.uint32),
                   jax.ShapeDtypeStruct((TOP_K, s), jnp.int32),
                   jax.ShapeDtypeStruct((TOP_K, s), F32),
                   jax.ShapeDtypeStruct((TOP_K, s), jnp.int32),
                   jax.ShapeDtypeStruct((e, 128), jnp.int32)],
        scratch_shapes=[pltpu.VMEM((e, 128), F32), pltpu.VMEM((tm, d), F32)],
        compiler_params=_cparams(("arbitrary",)),
        name="mergeout",
    )(og, od, x, g1, sca, sha, wmg, wmd, wbg, wbd, wo, gt, g2, sc, sh, wrt, bias_col)


def _route_select(logits, live, bias_ref, idx_ref, wts_ref, rnk_ref, cnt_ref, run_scr):
    e, tr = logits.shape
    gsz = e // N_GROUPS
    scores = jax.nn.sigmoid(logits)
    biased = scores + bias_ref[...]
    g3 = biased.reshape(N_GROUPS, gsz, tr)
    m1 = jnp.max(g3, axis=1, keepdims=True)
    n_top = jnp.sum(jnp.where(g3 == m1, 1.0, 0.0), axis=1, keepdims=True)
    m2 = jnp.max(jnp.where(g3 < m1, g3, -jnp.inf), axis=1, keepdims=True)
    gs = (m1 + jnp.where(n_top >= 2.0, m1, m2)).reshape(N_GROUPS, tr)
    gi = lax.broadcasted_iota(jnp.int32, (N_GROUPS, tr), 0)
    beaten = jnp.zeros((N_GROUPS, tr), F32)
    for g in range(N_GROUPS):
        other = gs[g:g + 1, :]
        beaten = beaten + jnp.where((other > gs) | ((other == gs) & (g < gi)), 1.0, 0.0)
    gsel = (beaten < float(TOPK_GROUPS)).reshape(N_GROUPS, 1, tr)
    masked = jnp.where(gsel, g3, -jnp.inf).reshape(e, tr)

    ids = lax.broadcasted_iota(jnp.int32, (e, tr), 0)
    chosen = jnp.zeros((e, tr), F32)
    sel_idx, sel_score = [], []
    for _ in range(TOP_K):
        mx = jnp.max(masked, axis=0, keepdims=True)
        ix = jnp.min(jnp.where(masked == mx, ids, e), axis=0, keepdims=True)
        hit = ids == ix
        sel_idx.append(ix)
        sel_score.append(jnp.sum(jnp.where(hit, scores, 0.0), axis=0, keepdims=True))
        chosen = jnp.where(hit, 1.0, chosen)
        masked = jnp.where(hit, -jnp.inf, masked)
    idx = jnp.concatenate(sel_idx, axis=0)
    sc = jnp.concatenate(sel_score, axis=0)
    idx_ref[...] = idx
    wts_ref[...] = sc / jnp.sum(sc, axis=0, keepdims=True) * ROUTED_SCALE

    row = lax.broadcasted_iota(jnp.int32, (tr, tr), 0)
    col = lax.broadcasted_iota(jnp.int32, (tr, tr), 1)
    before = jnp.where(row < col, 1.0, 0.0).astype(BF16)
    prior = jnp.dot(chosen.astype(BF16), before, preferred_element_type=F32) + run_scr[:, 0:1]
    rnk_ref[...] = jnp.concatenate(
        [jnp.sum(jnp.where(ids == sel_idx[k], prior, 0.0), axis=0, keepdims=True)
         for k in range(TOP_K)], axis=0).astype(jnp.int32)
    run_scr[...] = run_scr[...] + jnp.where(live, jnp.sum(chosen, axis=1, keepdims=True), 0.0)
    cnt_ref[...] = run_scr[...].astype(jnp.int32)


def _positions_kernel(idx_ref, rnk_ref, pstart_ref, pos_ref):
    e = pstart_ref.shape[0]
    ts = idx_ref.shape[1]
    ids = lax.broadcasted_iota(jnp.int32, (e, ts), 0)
    idx = idx_ref[...]
    pos_ref[...] = rnk_ref[...] + jnp.concatenate(
        [jnp.sum(jnp.where(ids == idx[k:k + 1, :], pstart_ref[...], 0), axis=0, keepdims=True)
         for k in range(TOP_K)], axis=0)


def _positions(idx, rnk, pstart_col, ts):
    s = idx.shape[1]
    e = pstart_col.shape[0]
    tok = pl.BlockSpec((TOP_K, ts), lambda i: (0, i))
    return pl.pallas_call(
        _positions_kernel,
        grid=(s // ts,),
        in_specs=[tok, tok, pl.BlockSpec((e, 1), lambda i: (0, 0))],
        out_specs=tok,
        out_shape=jax.ShapeDtypeStruct((TOP_K, s), jnp.int32),
        compiler_params=_cparams(("arbitrary",)),
        name="positions",
    )(idx, rnk, pstart_col)


def _swiglu_packed(xp, wg, wu, wd):
    lo, hi = _unpack_halves(xp)
    lo, hi = lo.astype(BF16), hi.astype(BF16)
    n = lo.shape[1]
    g = (jnp.dot(lo, wg[:n], preferred_element_type=F32)
         + jnp.dot(hi, wg[n:], preferred_element_type=F32))
    u = (jnp.dot(lo, wu[:n], preferred_element_type=F32)
         + jnp.dot(hi, wu[n:], preferred_element_type=F32))
    h = (g * jax.nn.sigmoid(g)) * u
    return jnp.dot(h.astype(BF16), wd[...], preferred_element_type=F32)


def _moe_kernel(ie_ref, ib_ref, first_ref, slot_ref, ne_ref, lead_ref, rows_ref, nv_ref, xs_ref,
                wg_hbm, wu_hbm, wd_hbm, ys_ref, wg_f, wu_f, wd_f, sem):
    del ib_ref
    i = pl.program_id(0)

    def fetch(e, slot):
        return [pltpu.make_async_copy(src.at[e], dst.at[slot], sem.at[slot, n])
                for n, (src, dst) in enumerate(((wg_hbm, wg_f), (wu_hbm, wu_f), (wd_hbm, wd_f)))]

    @pl.when(i == 0)
    def _():
        for s in range(MOE_SLOTS - 1):
            @pl.when(lead_ref[s] >= 0)
            def _(s=s):
                for cp in fetch(lead_ref[s], s):
                    cp.start()

    @pl.when(i < nv_ref[0])
    def _():
        for slot in range(MOE_SLOTS):
            @pl.when((first_ref[i] == 1) & (slot_ref[i] == slot))
            def _(slot=slot):
                for cp in fetch(ie_ref[i], slot):
                    cp.wait()

                @pl.when(ne_ref[i] >= 0)
                def _():
                    for cp in fetch(ne_ref[i], (slot + MOE_SLOTS - 1) % MOE_SLOTS):
                        cp.start()

        slot = slot_ref[i]
        row = lax.broadcasted_iota(jnp.int32, xs_ref.shape, 0)
        xp = jnp.where(row < rows_ref[i], xs_ref[...], jnp.uint32(0))
        ys_ref[...] = _pack_halves(_swiglu_packed(
            xp, wg_f[slot].astype(BF16), wu_f[slot].astype(BF16), wd_f[slot].astype(BF16)))


def _moe(item_e, item_b, item_first, item_slot, item_next, lead, item_rows, n_valid, xs, wg, wu,
         wd):
    m_pad, dh = xs.shape
    _, d, f = wg.shape
    n_items = item_e.shape[0]
    blk = lambda i, ie, ib, fi, sl, ne, ld, nr, nv: (ib[i], 0)
    hbm = pl.BlockSpec(memory_space=pl.ANY)
    return pl.pallas_call(
        _moe_kernel,
        grid_spec=pltpu.PrefetchScalarGridSpec(
            num_scalar_prefetch=8,
            grid=(n_items,),
            in_specs=[pl.BlockSpec((MOE_ROWS, dh), blk), hbm, hbm, hbm],
            out_specs=pl.BlockSpec((MOE_ROWS, dh), blk),
            scratch_shapes=[pltpu.VMEM((MOE_SLOTS, d, f), F32), pltpu.VMEM((MOE_SLOTS, d, f), F32),
                            pltpu.VMEM((MOE_SLOTS, f, d), F32),
                            pltpu.SemaphoreType.DMA((MOE_SLOTS, 3))],
        ),
        out_shape=jax.ShapeDtypeStruct((m_pad, dh), jnp.uint32),
        compiler_params=_cparams(("arbitrary",)),
        name="moe",
    )(item_e, item_b, item_first, item_slot, item_next, lead, item_rows, n_valid, xs, wg, wu, wd)


def _sc_gather_rows(table, idx_row):
    m = idx_row.shape[1]
    w = table.shape[1]
    idx_row = idx_row.reshape(m // SC_GATHER_WINDOW, SC_GATHER_WINDOW)
    mesh = plsc.VectorSubcoreMesh(core_axis_name="c", subcore_axis_name="s")

    @functools.partial(pl.kernel, mesh=mesh,
                       out_type=jax.ShapeDtypeStruct((m, w), table.dtype))
    def gather(table_hbm, idx_hbm, out_hbm):
        def body(idx_vmem, out_vmem):
            pltpu.sync_copy(table_hbm.at[idx_vmem.at[0]], out_vmem)

        pltpu.emit_pipeline(
            body,
            grid=(m // SC_GATHER_WINDOW,),
            in_specs=[pl.BlockSpec((1, SC_GATHER_WINDOW), lambda i: (i, 0))],
            out_specs=[pl.BlockSpec((SC_GATHER_WINDOW, w), lambda i: (i, 0))],
            core_axis_name=("c", "s"),
            dimension_semantics=(pltpu.PARALLEL,),
        )(idx_hbm, out_hbm)

    return gather(table, idx_row)


def _sc_scatter_rows(rows, idx_blocks, m_out):
    s, w = rows.shape
    mesh = plsc.VectorSubcoreMesh(core_axis_name="c", subcore_axis_name="s")

    @functools.partial(pl.kernel, mesh=mesh,
                       out_type=jax.ShapeDtypeStruct((m_out, w), rows.dtype))
    def scatter(rows_hbm, idx_hbm, out_hbm):
        def body(rows_vmem, idx_vmem):
            for k in range(TOP_K):
                pltpu.sync_copy(rows_vmem, out_hbm.at[idx_vmem.at[k]])

        pltpu.emit_pipeline(
            body,
            grid=(s // SC_GATHER_WINDOW,),
            in_specs=[pl.BlockSpec((SC_GATHER_WINDOW, w), lambda i: (i, 0)),
                      pl.BlockSpec((TOP_K, SC_GATHER_WINDOW), lambda i: (i, 0))],
            out_specs=[],
            core_axis_name=("c", "s"),
            dimension_semantics=(pltpu.PARALLEL,),
        )(rows_hbm, idx_hbm)

    return scatter(rows, idx_blocks)


def _combine_kernel(wt_ref, hf_ref, x1_ref, gt_ref, sg_ref, su_ref, sd_ref, g_ref, o_ref):
    tc = x1_ref.shape[0]
    y = _swiglu_packed(hf_ref[...], sg_ref, su_ref, sd_ref)
    wt = wt_ref[...]
    n = g_ref.shape[2]
    r_lo = jnp.zeros((tc, n), F32)
    r_hi = jnp.zeros((tc, n), F32)
    for k in range(TOP_K):
        lo, hi = _unpack_halves(g_ref[k])
        r_lo = r_lo + lo * wt[:, k:k + 1]
        r_hi = r_hi + hi * wt[:, k:k + 1]
    y = y + jnp.concatenate([r_lo, r_hi], axis=1)
    o_ref[...] = x1_ref[...] + gt_ref[...] * y


def _combine(wts_t, hfp, x1, gt, sg, su, sd, gathered, tc):
    s, d = x1.shape
    f = sg.shape[1]
    row = pl.BlockSpec((tc, d), lambda i: (i, 0))
    return pl.pallas_call(
        _combine_kernel,
        grid=(s // tc,),
        in_specs=[pl.BlockSpec((tc, TOP_K), lambda i: (i, 0)),
                  pl.BlockSpec((tc, d // 2), lambda i: (i, 0)), row,
                  pl.BlockSpec((1, d), lambda i: (0, 0)),
                  pl.BlockSpec((d, f), lambda i: (0, 0)),
                  pl.BlockSpec((d, f), lambda i: (0, 0)),
                  pl.BlockSpec((f, d), lambda i: (0, 0)),
                  pl.BlockSpec((TOP_K, tc, d // 2), lambda i: (0, i, 0))],
        out_specs=row,
        out_shape=jax.ShapeDtypeStruct((s, d), F32),
        compiler_params=_cparams(("arbitrary",)),
        name="combine",
    )(wts_t, hfp, x1, gt, sg, su, sd, gathered)


def _tile(n, want):
    t = min(n, want)
    assert n % t == 0, (n, t)
    return t


def _layer(l, x, c_col, pos_row, p):
    s, d = x.shape
    lambda_init = 0.8 - 0.6 * math.exp(-0.3 * l)
    gqk, gv = GLA_HEADS * GLA_DK, GLA_HEADS * GLA_DV
    dqk, dvw = DIFF_HEADS * 2 * DIFF_DH, DIFF_HEADS * DIFF_DV
    lowrank = p["gla_w_a2"].shape[0]

    mod = _ada(c_col, p["w_ada"], p["b_ada"][None, :])
    sh_a, sc_a, gt_a, sh_f, sc_f, gt_f = [mod[:, j * d:(j + 1) * d] for j in range(6)]

    w_in = p["w_in"]
    o = 0
    cols = {}
    for name, wdt in (("gq", gqk), ("gk", gqk), ("gv", gv), ("ga", lowrank), ("gg", gv),
                      ("dq", dqk), ("dk", dqk), ("dv", dvw), ("mg", d), ("md", d)):
        cols[name] = w_in[:, o:o + wdt]
        o += wdt
    w_ga = jnp.pad(cols["ga"], ((0, 0), (0, 128 - lowrank))).astype(BF16)

    g1 = p["norm1_g"][None, :]
    ts = _tile(s, TILE_SEQ)

    invf = ROPE_THETA ** (-jnp.arange(0, ROT_DIM, 2, dtype=F32) / ROT_DIM)
    qat, qbt, kr, vte, qn, kn, gkt = _qkvprep(
        x, g1, sc_a, sh_a, *(cols[n].T.astype(BF16) for n in ("dq", "dk", "dv", "gk")),
        pos_row, invf[:, None], p["diff_qnorm_g"][:, None], p["diff_knorm_g"][:, None], ts)

    wa2t = jnp.pad(p["gla_w_a2"].T, ((0, 0), (0, 128 - lowrank)))
    o_gla = _gla(x, g1, sc_a, sh_a, cols["gq"].astype(BF16), cols["gv"].astype(BF16),
                 cols["gg"].astype(BF16), w_ga, gkt, wa2t, p["gla_b_a"][:, None],
                 p["gla_onorm_g"][None, :], ts)
    o_diff = _diffattn(qat, qbt, kr, vte, qn, kn, p["diff_lq1"][None, :], p["diff_lk1"][None, :],
                       p["diff_lq2"][None, :], p["diff_lk2"][None, :],
                       p["diff_subln_g"][:, None], lambda_init, _tile(s, TILE_ATT_Q), ts)

    e = p["w_router"].shape[1]
    x1, hfp, idx, wts, rnk, cnt = _mergeout(
        o_gla, o_diff, x, g1, sc_a, sh_a, cols["mg"].astype(BF16), cols["md"].astype(BF16),
        p["w_branch_gla"].astype(BF16), p["w_branch_diff"].astype(BF16),
        p["w_out"].astype(BF16), gt_a, p["norm2_g"][None, :], sc_f, sh_f,
        p["w_router"].T, p["router_bias"][:, None], ts)

    counts = cnt[:, 0]
    pcounts = ((counts + MOE_ROWS - 1) // MOE_ROWS) * MOE_ROWS
    pend = jnp.cumsum(pcounts)
    pstart = pend - pcounts
    pos = _positions(idx, rnk, pstart[:, None], ts)
    n_items = (s * TOP_K) // MOE_ROWS + e
    n_valid = (pend[-1] // MOE_ROWS).astype(jnp.int32)
    item_b = jnp.minimum(jnp.arange(n_items, dtype=jnp.int32), n_valid - 1)
    item_e = jnp.minimum(jnp.sum(pend[None, :] <= (item_b * MOE_ROWS)[:, None], axis=1),
                         e - 1).astype(jnp.int32)

    wn = SC_GATHER_WINDOW
    pos_w = pos.reshape(TOP_K, s // wn, wn).transpose(1, 0, 2).reshape(s // wn * TOP_K, wn)
    xs = _sc_scatter_rows(hfp, pos_w, n_items * MOE_ROWS)
    item_rows = jnp.clip(pstart[item_e] + counts[item_e] - item_b * MOE_ROWS, 0,
                         MOE_ROWS).astype(jnp.int32)
    prev_e = jnp.concatenate([jnp.full((1,), -1, jnp.int32), item_e[:-1]])
    item_first = ((jnp.arange(n_items) < n_valid) & (item_e != prev_e)).astype(jnp.int32)
    item_slot = ((jnp.cumsum(item_first) - 1) % MOE_SLOTS).astype(jnp.int32)
    cand = jnp.where(pcounts > 0, jnp.arange(e, dtype=jnp.int32), e)
    nonempty_from = lax.cummin(cand[::-1])[::-1]
    following = jnp.concatenate([nonempty_from[1:], jnp.full((2,), e, jnp.int32)])
    ahead = item_e
    lead = [nonempty_from[0]]
    for _ in range(MOE_SLOTS - 1):
        ahead = following[ahead]
        lead.append(following[lead[-1]])
    item_next = jnp.where(ahead < e, ahead, -1).astype(jnp.int32)
    lead = jnp.stack(lead[:MOE_SLOTS - 1])
    lead = jnp.where(lead < e, lead, -1).astype(jnp.int32)
    ys = _moe(item_e, item_b, item_first, item_slot, item_next, lead, item_rows, n_valid[None],
              xs, p["w_exp_gate"], p["w_exp_up"], p["w_exp_down"])
    gathered = _sc_gather_rows(ys, pos.reshape(1, TOP_K * s)).reshape(TOP_K, s, d // 2)
    return _combine(wts.T, hfp, x1, gt_f, p["w_sh_gate"].astype(BF16),
                    p["w_sh_up"].astype(BF16), p["w_sh_down"].astype(BF16), gathered,
                    _tile(s, TILE_COMBINE))


_LAYER_PARAMS = ("w_ada", "b_ada", "norm1_g", "w_in", "gla_w_a2", "gla_b_a", "gla_onorm_g",
                 "diff_qnorm_g", "diff_knorm_g", "diff_lq1", "diff_lk1", "diff_lq2", "diff_lk2",
                 "diff_subln_g", "w_branch_gla", "w_branch_diff", "w_out", "norm2_g", "w_router",
                 "router_bias", "w_exp_gate", "w_exp_up", "w_exp_down", "w_sh_gate", "w_sh_up",
                 "w_sh_down")


def kernel(x, c, positions, w_ada, b_ada, norm1_g, w_in, gla_w_a2, gla_b_a, gla_onorm_g, diff_qnorm_g, diff_knorm_g, diff_lq1, diff_lk1, diff_lq2, diff_lk2, diff_subln_g, w_branch_gla, w_branch_diff, w_out, norm2_g, w_router, router_bias, w_exp_gate, w_exp_up, w_exp_down, w_sh_gate, w_sh_up, w_sh_down):
    stacked = dict(zip(_LAYER_PARAMS, (
        w_ada, b_ada, norm1_g, w_in, gla_w_a2, gla_b_a, gla_onorm_g, diff_qnorm_g, diff_knorm_g,
        diff_lq1, diff_lk1, diff_lq2, diff_lk2, diff_subln_g, w_branch_gla, w_branch_diff, w_out,
        norm2_g, w_router, router_bias, w_exp_gate, w_exp_up, w_exp_down, w_sh_gate, w_sh_up,
        w_sh_down)))
    b, s, d = x.shape
    assert b == 1, "single-sequence kernel"
    xl = x[0]
    c_col = c[0][:, None]
    pos_row = positions.astype(jnp.int32)
    for l in range(w_ada.shape[0]):
        xl = _layer(l, xl, c_col, pos_row, {k: v[l] for k, v in stacked.items()})
    return xl[None]
```

```python
import functools
import math

import jax
import jax.numpy as jnp
from jax import lax
from jax.experimental import pallas as pl
from jax.experimental.pallas import tpu as pltpu
from jax.experimental.pallas import tpu_sc as plsc

CHUNK = 64
EPS = 1e-6
GLA_HEADS = 4
GLA_DK = 128
GLA_DV = 256
GLA_TAU = 16.0
DIFF_HEADS = 8
DIFF_DH = 64
DIFF_DV = 2 * DIFF_DH
ROPE_THETA = 500000.0
ROT_DIM = DIFF_DH // 4
N_GROUPS = 8
TOPK_GROUPS = 4
TOP_K = 8
ROUTED_SCALE = 2.5

MOE_ROWS = 640
MOE_SLOTS = 3
SC_GATHER_WINDOW = 64
VMEM_LIMIT = 56 * 1024 * 1024
TILE_SEQ = 512
TILE_ATT_Q = 2048
TILE_COMBINE = 512
NEG_BIG = -1e30
LOG2E = 1.4426950408889634
F32 = jnp.float32
BF16 = jnp.bfloat16


def _cparams(sem):
    return pltpu.CompilerParams(dimension_semantics=sem, vmem_limit_bytes=VMEM_LIMIT)


def _nt_dot(a, b):
    return lax.dot_general(a, b, (((1,), (1,)), ((), ())), preferred_element_type=F32)


def _pack_halves(x):
    n = x.shape[1] // 2
    lo = pltpu.bitcast(x[:, :n].astype(BF16).astype(F32), jnp.uint32) >> 16
    hi = pltpu.bitcast(x[:, n:].astype(BF16).astype(F32), jnp.uint32) & jnp.uint32(0xFFFF0000)
    return lo | hi


def _unpack_halves(w):
    return (pltpu.bitcast(w << 16, F32), pltpu.bitcast(w & jnp.uint32(0xFFFF0000), F32))


def _split_bf16(a):
    hi = a.astype(BF16)
    return hi, (a - hi.astype(F32)).astype(BF16)


def _rms_mod(x, g, sc, sh):
    xn = x * lax.rsqrt(jnp.mean(x * x, axis=-1, keepdims=True) + EPS)
    return (xn * g) * (1.0 + sc) + sh


def _ada_kernel(c_ref, w_ref, b_ref, o_ref):
    c = c_ref[...]
    ca = c * jax.nn.sigmoid(c)
    o_ref[...] = jnp.sum(ca * w_ref[...], axis=0, keepdims=True) + b_ref[...]


def _ada(c_col, w, b):
    d, n = w.shape
    tn = min(1024, n)
    return pl.pallas_call(
        _ada_kernel,
        grid=(n // tn,),
        in_specs=[pl.BlockSpec((d, 1), lambda j: (0, 0)),
                  pl.BlockSpec((d, tn), lambda j: (0, j)),
                  pl.BlockSpec((1, tn), lambda j: (0, j))],
        out_specs=pl.BlockSpec((1, tn), lambda j: (0, j)),
        out_shape=jax.ShapeDtypeStruct((1, n), F32),
        compiler_params=_cparams(("arbitrary",)),
        name="ada",
    )(c_col, w, b)


def _gla_kernel(x_ref, g1_ref, sc_ref, sh_ref, wq_ref, wv_ref, wg_ref, wga_ref, kt_ref, wa2t_ref,
                ba_ref, on_ref, o_ref, state_ref, o_scr, q_ref, v_ref, gg_ref):
    tt = x_ref.shape[0]
    nchunk = tt // CHUNK

    @pl.when(pl.program_id(0) == 0)
    def _():
        state_ref[...] = jnp.zeros_like(state_ref)

    h_in = _rms_mod(x_ref[...], g1_ref[...], sc_ref[...], sh_ref[...]).astype(BF16)
    q_ref[...] = jnp.dot(h_in, wq_ref[...], preferred_element_type=F32).astype(BF16)
    v_ref[...] = jnp.dot(h_in, wv_ref[...], preferred_element_type=F32).astype(BF16)
    gg_ref[...] = jnp.dot(h_in, wg_ref[...], preferred_element_type=F32).astype(BF16)
    ga = jnp.dot(h_in, wga_ref[...], preferred_element_type=F32)

    a_hi, a_lo = _split_bf16(wa2t_ref[...])
    g_hi, g_lo = _split_bf16(ga)
    zt = _nt_dot(a_hi, g_hi) + _nt_dot(a_hi, g_lo) + _nt_dot(a_lo, g_hi) + ba_ref[...]
    lat = (jnp.minimum(zt, 0.0) - jnp.log1p(jnp.exp(-jnp.abs(zt)))) * (1.0 / GLA_TAU)
    row = lax.broadcasted_iota(jnp.int32, (tt, tt), 0)
    col = lax.broadcasted_iota(jnp.int32, (tt, tt), 1)
    same = (row // CHUNK) == (col // CHUNK)
    incl = jnp.where(same & (row <= col), 1.0, 0.0).astype(BF16)
    full = jnp.where(same, 1.0, 0.0).astype(BF16)
    lat_hi, lat_lo = _split_bf16(lat)
    cumt = (jnp.dot(lat_hi, incl, preferred_element_type=F32)
            + jnp.dot(lat_lo, incl, preferred_element_type=F32))
    tott = (jnp.dot(lat_hi, full, preferred_element_type=F32)
            + jnp.dot(lat_lo, full, preferred_element_type=F32))
    kdt = kt_ref[...].astype(F32) * jnp.exp(tott - cumt)
    dec = jnp.exp(tott)

    lane = lax.broadcasted_iota(jnp.int32, (GLA_DK, 2 * CHUNK), 1)
    upd = {}
    for c in range(nchunk):
        pair = (c // 2) * 2 * CHUNK
        if nchunk > 1:
            keep = (lane // CHUNK) == (c % 2)
        for h in range(GLA_HEADS):
            rows = slice(h * GLA_DK, (h + 1) * GLA_DK)
            vcols = slice(h * GLA_DV, (h + 1) * GLA_DV)
            if nchunk > 1:
                a = jnp.where(keep, kdt[rows, pair:pair + 2 * CHUNK], 0.0).astype(BF16)
                vp = v_ref[pair:pair + 2 * CHUNK, vcols]
            else:
                a = kdt[rows, :].astype(BF16)
                vp = v_ref[:, vcols]
            upd[c, h] = jnp.dot(a, vp, preferred_element_type=F32)

    for h in range(GLA_HEADS):
        rows = slice(h * GLA_DK, (h + 1) * GLA_DK)
        vcols = slice(h * GLA_DV, (h + 1) * GLA_DV)
        st = state_ref[h]
        states = []
        for c in range(nchunk):
            st = st * dec[rows, c * CHUNK:c * CHUNK + 1] + upd[c, h]
            states.append(st.astype(BF16))
        state_ref[h] = st
        for c in range(nchunk):
            o_scr[c * CHUNK:(c + 1) * CHUNK, vcols] = jnp.dot(
                q_ref[c * CHUNK:(c + 1) * CHUNK, rows], states[c], preferred_element_type=F32)

    for h in range(GLA_HEADS):
        vcols = slice(h * GLA_DV, (h + 1) * GLA_DV)
        o = o_scr[:, vcols] * (GLA_DK ** -0.5)
        o = o * lax.rsqrt(jnp.mean(o * o, axis=-1, keepdims=True) + EPS) * on_ref[...]
        g = gg_ref[:, vcols].astype(F32)
        o_ref[:, vcols] = (o * (g * jax.nn.sigmoid(g))).astype(BF16)


def _gla(x, g1, sc, sh, wq, wv, wg, wga, gkt, wa2t, ba_col, on_g, tt):
    s, d = x.shape
    qk = GLA_HEADS * GLA_DK
    vw = GLA_HEADS * GLA_DV
    vec = pl.BlockSpec((1, d), lambda i: (0, 0))
    whole = lambda a: pl.BlockSpec(a.shape, lambda i: (0, 0))
    return pl.pallas_call(
        _gla_kernel,
        grid=(s // tt,),
        in_specs=[pl.BlockSpec((tt, d), lambda i: (i, 0)), vec, vec, vec,
                  whole(wq), whole(wv), whole(wg), whole(wga),
                  pl.BlockSpec((qk, tt), lambda i: (0, i)),
                  whole(wa2t), whole(ba_col), whole(on_g)],
        out_specs=pl.BlockSpec((tt, vw), lambda i: (i, 0)),
        out_shape=jax.ShapeDtypeStruct((s, vw), BF16),
        scratch_shapes=[pltpu.VMEM((GLA_HEADS, GLA_DK, GLA_DV), F32),
                        pltpu.VMEM((tt, vw), F32),
                        pltpu.VMEM((tt, qk), BF16), pltpu.VMEM((tt, vw), BF16),
                        pltpu.VMEM((tt, vw), BF16)],
        compiler_params=_cparams(("arbitrary",)),
        name="gla",
    )(x, g1, sc, sh, wq, wv, wg, wga, gkt, wa2t, ba_col, on_g)


def _qknorm_rope_t(xt, g_col, cos, sin):
    n, tm = xt.shape
    x3 = xt.reshape(n // DIFF_DH, DIFF_DH, tm)
    r = lax.rsqrt(jnp.mean(x3 * x3, axis=1, keepdims=True) + EPS)
    y = x3 * r * g_col[None]
    half = ROT_DIM // 2
    y1, y2, rest = y[:, :half], y[:, half:ROT_DIM], y[:, ROT_DIM:]
    o1 = y1 * cos[None] - y2 * sin[None]
    o2 = y2 * cos[None] + y1 * sin[None]
    return jnp.concatenate([o1, o2, rest], axis=1)


def _seg_norms(x3):
    return jnp.sqrt(jnp.sum(x3 * x3, axis=1)).reshape(DIFF_HEADS, 2, x3.shape[2])


def _qkvprep_kernel(x_ref, g_ref, sc_ref, sh_ref, wq_ref, wk_ref, wv_ref, wgk_ref, pos_ref,
                    invf_ref, qg_ref, kg_ref, qa_ref, qb_ref, ko_ref, ve_ref, qn_ref, kn_ref,
                    gk_ref):
    tm = x_ref.shape[0]
    h = _rms_mod(x_ref[...], g_ref[...], sc_ref[...], sh_ref[...]).astype(BF16)
    qt = _nt_dot(wq_ref[...], h)
    kt = _nt_dot(wk_ref[...], h)
    vt = _nt_dot(wv_ref[...], h)
    gk_ref[...] = _nt_dot(wgk_ref[...], h).astype(BF16)
    ang = pos_ref[...].astype(F32) * invf_ref[...]
    cos, sin = jnp.cos(ang), jnp.sin(ang)
    q3 = _qknorm_rope_t(qt, qg_ref[...], cos, sin) * (DIFF_DH ** -0.5 * LOG2E)
    qn_ref[...] = _seg_norms(q3)
    seg = lax.broadcasted_iota(jnp.int32, q3.shape, 0)
    qa_ref[...] = jnp.where(seg % 2 == 0, q3, 0.0).reshape(-1, tm).astype(BF16)
    qb_ref[...] = jnp.where(seg % 2 == 1, q3, 0.0).reshape(-1, tm).astype(BF16)
    k3 = _qknorm_rope_t(kt, kg_ref[...], cos, sin)
    ko_ref[...] = k3.reshape(-1, tm).T.astype(BF16)
    kn_ref[...] = _seg_norms(k3)
    v3 = vt.astype(BF16).reshape(DIFF_HEADS, DIFF_DV, tm)
    ones = jnp.ones((DIFF_HEADS, ATT_SUM_ROWS, tm), BF16)
    ve_ref[...] = jnp.concatenate([v3, ones], axis=1).reshape(-1, tm)


def _qkvprep(x, g, sc, sh, wq_t, wk_t, wv_t, wgk_t, pos_row, invf_col, qg_col, kg_col, tm):
    s, d = x.shape
    n = DIFF_HEADS * 2 * DIFF_DH
    ne = DIFF_HEADS * (DIFF_DV + ATT_SUM_ROWS)
    ngk = wgk_t.shape[0]
    col = pl.BlockSpec((DIFF_DH, 1), lambda i: (0, 0))
    vec = pl.BlockSpec((1, d), lambda i: (0, 0))
    wspec = pl.BlockSpec((n, d), lambda i: (0, 0))
    return pl.pallas_call(
        _qkvprep_kernel,
        grid=(s // tm,),
        in_specs=[pl.BlockSpec((tm, d), lambda i: (i, 0)), vec, vec, vec,
                  wspec, wspec, wspec, pl.BlockSpec((ngk, d), lambda i: (0, 0)),
                  pl.BlockSpec((1, tm), lambda i: (0, i)),
                  pl.BlockSpec((ROT_DIM // 2, 1), lambda i: (0, 0)), col, col],
        out_specs=[pl.BlockSpec((n, tm), lambda i: (0, i)),
                   pl.BlockSpec((n, tm), lambda i: (0, i)),
                   pl.BlockSpec((tm, n), lambda i: (i, 0)),
                   pl.BlockSpec((ne, tm), lambda i: (0, i)),
                   pl.BlockSpec((DIFF_HEADS, 2, tm), lambda i: (0, 0, i)),
                   pl.BlockSpec((DIFF_HEADS, 2, tm), lambda i: (0, 0, i)),
                   pl.BlockSpec((ngk, tm), lambda i: (0, i))],
        out_shape=[jax.ShapeDtypeStruct((n, s), BF16), jax.ShapeDtypeStruct((n, s), BF16),
                   jax.ShapeDtypeStruct((s, n), BF16), jax.ShapeDtypeStruct((ne, s), BF16),
                   jax.ShapeDtypeStruct((DIFF_HEADS, 2, s), F32),
                   jax.ShapeDtypeStruct((DIFF_HEADS, 2, s), F32),
                   jax.ShapeDtypeStruct((ngk, s), BF16)],
        compiler_params=_cparams(("arbitrary",)),
        name="qkvprep",
    )(x, g, sc, sh, wq_t, wk_t, wv_t, wgk_t, pos_row, invf_col, qg_col, kg_col)


ATT_COLS = 256
ATT_LOOKAHEAD = {True: 2, False: 4}
ATT_BODY_ITEMS = {True: 64, False: 8}
ATT_SUM_ROWS = 16
ATT_BOUND_SLACK = 1.02
ATT_BOUND_LIMIT = 50.0


def _diffattn_kernel(qa_ref, qb_ref, k_ref, vt_ref, qn_ref, kn_ref, lq1_ref, lk1_ref, lq2_ref,
                     lk2_ref, sg_ref, o_ref, *scr, lambda_init, tk, cols):
    tq = qa_ref.shape[1]
    nblk = 2 * tq // cols
    m_scr, acc_scr = (scr[b * nblk:(b + 1) * nblk] for b in range(2))
    kmax_scr = scr[2 * nblk]
    s_scr = scr[2 * nblk + 1:]
    i = pl.program_id(1)

    @pl.when(i == 0)
    def _():
        kmax_scr[...] = jnp.max(kn_ref[...], axis=1, keepdims=True)

    bound = qn_ref[...] * kmax_scr[...] * ATT_BOUND_SLACK
    bound = jnp.concatenate([bound[0:1], bound[1:2]], axis=1)
    bounded = jnp.max(bound) < ATT_BOUND_LIMIT
    for c in range(nblk):
        m_scr[c][...] = jnp.where(bounded, bound[:, c * cols:(c + 1) * cols], NEG_BIG)
        acc_scr[c][...] = jnp.zeros_like(acc_scr[c])

    def scores(j, c):
        start = pl.multiple_of(j * tk, tk)
        q_ref = qa_ref if c * cols < tq else qb_ref
        off = (c * cols) % tq
        return jnp.dot(k_ref[pl.ds(start, tk), :], q_ref[:, off:off + cols],
                       preferred_element_type=F32)

    def steps(items, next_tile, fixed):
        look = ATT_LOOKAHEAD[fixed]
        pending = []
        for n, (j, c, keys) in enumerate(items):
            s = s_scr[n][...] if n < look else pending.pop(0)
            ahead = n + look
            if ahead < len(items):
                pending.append(scores(*items[ahead][:2]))
            elif next_tile is not None:
                s_scr[ahead - len(items)][...] = scores(next_tile, ahead - len(items))
            start = pl.multiple_of(j * tk, tk)
            if keys is None:
                keys = tk
            else:
                s = s[:keys]
                krow = lax.broadcasted_iota(jnp.int32, (keys, cols), 0)
                qcol = lax.broadcasted_iota(jnp.int32, (keys, cols), 1)
                qpos = i * tq + (c * cols) % tq + qcol
                s = jnp.where((start + krow) // CHUNK <= qpos // CHUNK, s, NEG_BIG)
            vj = vt_ref[:, pl.ds(start, keys)]
            if fixed:
                p = jnp.exp2(s - m_scr[c][...])
                acc_scr[c][:DIFF_DV] += jnp.dot(vj[:DIFF_DV], p.astype(BF16),
                                                preferred_element_type=F32)
                acc_scr[c][DIFF_DV:DIFF_DV + 1] += jnp.sum(p, axis=0, keepdims=True)
            else:
                m_prev = m_scr[c][...]
                m_new = jnp.maximum(m_prev, jnp.max(s, axis=0, keepdims=True))
                alpha = jnp.exp2(m_prev - m_new)
                p = jnp.exp2((s - m_new).astype(BF16))
                acc_scr[c][...] = alpha * acc_scr[c][...] + jnp.dot(
                    vj, p, preferred_element_type=F32)
                m_scr[c][...] = m_new

    def past(first, count):
        return [(first + u, c, None) for u in range(count) for c in range(nblk)]

    def attend(fixed):
        n_past = i * (tq // tk)
        for c in range(ATT_LOOKAHEAD[fixed]):
            s_scr[c][...] = scores(0, c)
        big = max(1, ATT_BODY_ITEMS[fixed] // nblk)
        lax.fori_loop(0, n_past // big, lambda t, c: (steps(
            past(big * t, big), big * t + big, fixed), c)[1], 0)
        size = big // 2
        while size >= 1:
            first = (n_past // (2 * size)) * (2 * size)

            @pl.when(n_past % (2 * size) >= size)
            def _(first=first, size=size):
                steps(past(first, size), first + size, fixed)

            size //= 2
        diagonal = []
        for d in range(tq // tk):
            for c in range(nblk):
                off = (c * cols) % tq
                visible = off + cols - d * tk
                if visible > 0:
                    diagonal.append((n_past + d, c, None if off >= (d + 1) * tk
                                     else min(tk, visible)))
        steps(diagonal, None, fixed)

    @pl.when(bounded)
    def _():
        attend(True)

    @pl.when(jnp.logical_not(bounded))
    def _():
        attend(False)

    o = jnp.concatenate([acc_scr[c][:DIFF_DV] * (1.0 / acc_scr[c][DIFF_DV:DIFF_DV + 1])
                         for c in range(nblk)], axis=1)
    lam = (jnp.exp(jnp.sum(lq1_ref[...] * lk1_ref[...]))
           - jnp.exp(jnp.sum(lq2_ref[...] * lk2_ref[...])) + lambda_init)
    o = o[:, :tq] - lam * o[:, tq:]
    o = o * lax.rsqrt(jnp.mean(o * o, axis=0, keepdims=True) + EPS) * sg_ref[...]
    o_ref[...] = (o * (1.0 - lambda_init)).T.astype(BF16)


def _diffattn(qat, qbt, kr, vte, qn, kn, lq1, lk1, lq2, lk2, sg_col, lambda_init, tq, tk):
    s = kr.shape[0]
    hd = 2 * DIFF_DH
    vec = pl.BlockSpec((1, DIFF_DH), lambda h, i: (0, 0))
    cols = min(ATT_COLS, tq)
    nblk = 2 * tq // cols
    kern = functools.partial(_diffattn_kernel, lambda_init=lambda_init, tk=tk, cols=cols)
    dve = DIFF_DV + ATT_SUM_ROWS
    assert tq % tk == 0 and nblk >= max(ATT_LOOKAHEAD.values())
    scratch = ([pltpu.VMEM((1, cols), F32)] * nblk + [pltpu.VMEM((dve, cols), F32)] * nblk
               + [pltpu.VMEM((2, 1), F32)]
               + [pltpu.VMEM((tk, cols), F32)] * max(ATT_LOOKAHEAD.values()))
    return pl.pallas_call(
        kern,
        grid=(DIFF_HEADS, s // tq),
        in_specs=[pl.BlockSpec((hd, tq), lambda h, i: (h, i)),
                  pl.BlockSpec((hd, tq), lambda h, i: (h, i)),
                  pl.BlockSpec((s, hd), lambda h, i: (0, h)),
                  pl.BlockSpec((dve, s), lambda h, i: (h, 0)),
                  pl.BlockSpec((None, 2, tq), lambda h, i: (h, 0, i)),
                  pl.BlockSpec((None, 2, s), lambda h, i: (h, 0, 0)),
                  vec, vec, vec, vec,
                  pl.BlockSpec((DIFF_DV, 1), lambda h, i: (0, 0))],
        out_specs=pl.BlockSpec((tq, DIFF_DV), lambda h, i: (i, h)),
        out_shape=jax.ShapeDtypeStruct((s, DIFF_HEADS * DIFF_DV), BF16),
        scratch_shapes=scratch,
        compiler_params=_cparams(("arbitrary", "arbitrary")),
        name="diffattn",
    )(qat, qbt, kr, vte, qn, kn, lq1, lk1, lq2, lk2, sg_col)


def _mergeout_kernel(og_ref, od_ref, x_ref, g1_ref, sca_ref, sha_ref, wmg_ref, wmd_ref, wbg_ref,
                     wbd_ref, wo_ref, gt_ref, g2_ref, sc_ref, sh_ref, wrt_ref, bias_ref,
                     x1_ref, hfp_ref, idx_ref, wts_ref, rnk_ref, cnt_ref, run_scr, hf_scr):
    i = pl.program_id(0)

    @pl.when(i == 0)
    def _():
        hf_scr[...] = jnp.zeros_like(hf_scr)
        run_scr[...] = jnp.zeros_like(run_scr)

    w_hi, w_lo = _split_bf16(wrt_ref[...])
    h_hi, h_lo = _split_bf16(hf_scr[...])
    logits = _nt_dot(w_hi, h_hi) + _nt_dot(w_hi, h_lo) + _nt_dot(w_lo, h_hi)
    x = x_ref[...]
    h_in = _rms_mod(x, g1_ref[...], sca_ref[...], sha_ref[...]).astype(BF16)
    mg = jnp.dot(h_in, wmg_ref[...], preferred_element_type=F32)
    md = jnp.dot(h_in, wmd_ref[...], preferred_element_type=F32)
    bg = jnp.dot(og_ref[...], wbg_ref[...], preferred_element_type=F32)
    bd = jnp.dot(od_ref[...], wbd_ref[...], preferred_element_type=F32)
    merged = jax.nn.sigmoid(mg) * bg + jax.nn.sigmoid(md) * bd
    x1 = x + gt_ref[...] * jnp.dot(merged.astype(BF16), wo_ref[...],
                                   preferred_element_type=F32)
    x1_ref[...] = x1
    hf = _rms_mod(x1, g2_ref[...], sc_ref[...], sh_ref[...])
    hfp_ref[...] = _pack_halves(hf)
    hf_scr[...] = hf
    _route_select(logits, i > 0, bias_ref, idx_ref, wts_ref, rnk_ref, cnt_ref, run_scr)


def _mergeout(og, od, x, g1, sca, sha, wmg, wmd, wbg, wbd, wo, gt, g2, sc, sh, wrt, bias_col,
              tm):
    s, d = x.shape
    e = wrt.shape[0]
    vec = pl.BlockSpec((1, d), lambda i: (0, 0))
    wspec = pl.BlockSpec((d, d), lambda i: (0, 0))
    n = s // tm
    row = pl.BlockSpec((tm, d), lambda i: (jnp.minimum(i, n - 1), 0))
    tok = pl.BlockSpec((TOP_K, tm), lambda i: (0, jnp.maximum(i - 1, 0)))
    return pl.pallas_call(
        _mergeout_kernel,
        grid=(n + 1,),
        in_specs=[row, row, row, vec, vec, vec, wspec, wspec, wspec, wspec, wspec,
                  vec, vec, vec, vec,
                  pl.BlockSpec((e, d), lambda i: (0, 0)), pl.BlockSpec((e, 1), lambda i: (0, 0))],
        out_specs=[row, pl.BlockSpec((tm, d // 2), lambda i: (jnp.minimum(i, n - 1), 0)),
                   tok, tok, tok, pl.BlockSpec((e, 128), lambda i: (0, 0))],
        out_shape=[jax.ShapeDtypeStruct((s, d), F32),
                   jax.ShapeDtypeStruct((s, d // 2), jnp.uint32),
                   jax.ShapeDtypeStruct((TOP_K, s), jnp.int32),
                   jax.ShapeDtypeStruct((TOP_K, s), F32),
                   jax.ShapeDtypeStruct((TOP_K, s), jnp.int32),
                   jax.ShapeDtypeStruct((e, 128), jnp.int32)],
        scratch_shapes=[pltpu.VMEM((e, 128), F32), pltpu.VMEM((tm, d), F32)],
        compiler_params=_cparams(("arbitrary",)),
        name="mergeout",
    )(og, od, x, g1, sca, sha, wmg, wmd, wbg, wbd, wo, gt, g2, sc, sh, wrt, bias_col)


def _route_select(logits, live, bias_ref, idx_ref, wts_ref, rnk_ref, cnt_ref, run_scr):
    e, tr = logits.shape
    gsz = e // N_GROUPS
    scores = jax.nn.sigmoid(logits)
    biased = scores + bias_ref[...]
    g3 = biased.reshape(N_GROUPS, gsz, tr)
    m1 = jnp.max(g3, axis=1, keepdims=True)
    n_top = jnp.sum(jnp.where(g3 == m1, 1.0, 0.0), axis=1, keepdims=True)
    m2 = jnp.max(jnp.where(g3 < m1, g3, -jnp.inf), axis=1, keepdims=True)
    gs = (m1 + jnp.where(n_top >= 2.0, m1, m2)).reshape(N_GROUPS, tr)
    gi = lax.broadcasted_iota(jnp.int32, (N_GROUPS, tr), 0)
    beaten = jnp.zeros((N_GROUPS, tr), F32)
    for g in range(N_GROUPS):
        other = gs[g:g + 1, :]
        beaten = beaten + jnp.where((other > gs) | ((other == gs) & (g < gi)), 1.0, 0.0)
    gsel = (beaten < float(TOPK_GROUPS)).reshape(N_GROUPS, 1, tr)
    masked = jnp.where(gsel, g3, -jnp.inf).reshape(e, tr)

    ids = lax.broadcasted_iota(jnp.int32, (e, tr), 0)
    chosen = jnp.zeros((e, tr), F32)
    sel_idx, sel_score = [], []
    for _ in range(TOP_K):
        mx = jnp.max(masked, axis=0, keepdims=True)
        ix = jnp.min(jnp.where(masked == mx, ids, e), axis=0, keepdims=True)
        hit = ids == ix
        sel_idx.append(ix)
        sel_score.append(jnp.sum(jnp.where(hit, scores, 0.0), axis=0, keepdims=True))
        chosen = jnp.where(hit, 1.0, chosen)
        masked = jnp.where(hit, -jnp.inf, masked)
    idx = jnp.concatenate(sel_idx, axis=0)
    sc = jnp.concatenate(sel_score, axis=0)
    idx_ref[...] = idx
    wts_ref[...] = sc / jnp.sum(sc, axis=0, keepdims=True) * ROUTED_SCALE

    row = lax.broadcasted_iota(jnp.int32, (tr, tr), 0)
    col = lax.broadcasted_iota(jnp.int32, (tr, tr), 1)
    before = jnp.where(row < col, 1.0, 0.0).astype(BF16)
    prior = jnp.dot(chosen.astype(BF16), before, preferred_element_type=F32) + run_scr[:, 0:1]
    rnk_ref[...] = jnp.concatenate(
        [jnp.sum(jnp.where(ids == sel_idx[k], prior, 0.0), axis=0, keepdims=True)
         for k in range(TOP_K)], axis=0).astype(jnp.int32)
    run_scr[...] = run_scr[...] + jnp.where(live, jnp.sum(chosen, axis=1, keepdims=True), 0.0)
    cnt_ref[...] = run_scr[...].astype(jnp.int32)


def _positions_kernel(idx_ref, rnk_ref, pstart_ref, pos_ref):
    e = pstart_ref.shape[0]
    ts = idx_ref.shape[1]
    ids = lax.broadcasted_iota(jnp.int32, (e, ts), 0)
    idx = idx_ref[...]
    pos_ref[...] = rnk_ref[...] + jnp.concatenate(
        [jnp.sum(jnp.where(ids == idx[k:k + 1, :], pstart_ref[...], 0), axis=0, keepdims=True)
         for k in range(TOP_K)], axis=0)


def _positions(idx, rnk, pstart_col, ts):
    s = idx.shape[1]
    e = pstart_col.shape[0]
    tok = pl.BlockSpec((TOP_K, ts), lambda i: (0, i))
    return pl.pallas_call(
        _positions_kernel,
        grid=(s // ts,),
        in_specs=[tok, tok, pl.BlockSpec((e, 1), lambda i: (0, 0))],
        out_specs=tok,
        out_shape=jax.ShapeDtypeStruct((TOP_K, s), jnp.int32),
        compiler_params=_cparams(("arbitrary",)),
        name="positions",
    )(idx, rnk, pstart_col)


def _swiglu_packed(xp, wg, wu, wd):
    lo, hi = _unpack_halves(xp)
    lo, hi = lo.astype(BF16), hi.astype(BF16)
    n = lo.shape[1]
    g = (jnp.dot(lo, wg[:n], preferred_element_type=F32)
         + jnp.dot(hi, wg[n:], preferred_element_type=F32))
    u = (jnp.dot(lo, wu[:n], preferred_element_type=F32)
         + jnp.dot(hi, wu[n:], preferred_element_type=F32))
    h = (g * jax.nn.sigmoid(g)) * u
    return jnp.dot(h.astype(BF16), wd[...], preferred_element_type=F32)


def _moe_kernel(ie_ref, ib_ref, first_ref, slot_ref, ne_ref, lead_ref, rows_ref, nv_ref, xs_ref,
                wg_hbm, wu_hbm, wd_hbm, ys_ref, wg_f, wu_f, wd_f, sem):
    del ib_ref
    i = pl.program_id(0)

    def fetch(e, slot):
        return [pltpu.make_async_copy(src.at[e], dst.at[slot], sem.at[slot, n])
                for n, (src, dst) in enumerate(((wg_hbm, wg_f), (wu_hbm, wu_f), (wd_hbm, wd_f)))]

    @pl.when(i == 0)
    def _():
        for s in range(MOE_SLOTS - 1):
            @pl.when(lead_ref[s] >= 0)
            def _(s=s):
                for cp in fetch(lead_ref[s], s):
                    cp.start()

    @pl.when(i < nv_ref[0])
    def _():
        for slot in range(MOE_SLOTS):
            @pl.when((first_ref[i] == 1) & (slot_ref[i] == slot))
            def _(slot=slot):
                for cp in fetch(ie_ref[i], slot):
                    cp.wait()

                @pl.when(ne_ref[i] >= 0)
                def _():
                    for cp in fetch(ne_ref[i], (slot + MOE_SLOTS - 1) % MOE_SLOTS):
                        cp.start()

        slot = slot_ref[i]
        row = lax.broadcasted_iota(jnp.int32, xs_ref.shape, 0)
        xp = jnp.where(row < rows_ref[i], xs_ref[...], jnp.uint32(0))
        ys_ref[...] = _pack_halves(_swiglu_packed(
            xp, wg_f[slot].astype(BF16), wu_f[slot].astype(BF16), wd_f[slot].astype(BF16)))


def _moe(item_e, item_b, item_first, item_slot, item_next, lead, item_rows, n_valid, xs, wg, wu,
         wd):
    m_pad, dh = xs.shape
    _, d, f = wg.shape
    n_items = item_e.shape[0]
    blk = lambda i, ie, ib, fi, sl, ne, ld, nr, nv: (ib[i], 0)
    hbm = pl.BlockSpec(memory_space=pl.ANY)
    return pl.pallas_call(
        _moe_kernel,
        grid_spec=pltpu.PrefetchScalarGridSpec(
            num_scalar_prefetch=8,
            grid=(n_items,),
            in_specs=[pl.BlockSpec((MOE_ROWS, dh), blk), hbm, hbm, hbm],
            out_specs=pl.BlockSpec((MOE_ROWS, dh), blk),
            scratch_shapes=[pltpu.VMEM((MOE_SLOTS, d, f), F32), pltpu.VMEM((MOE_SLOTS, d, f), F32),
                            pltpu.VMEM((MOE_SLOTS, f, d), F32),
                            pltpu.SemaphoreType.DMA((MOE_SLOTS, 3))],
        ),
        out_shape=jax.ShapeDtypeStruct((m_pad, dh), jnp.uint32),
        compiler_params=_cparams(("arbitrary",)),
        name="moe",
    )(item_e, item_b, item_first, item_slot, item_next, lead, item_rows, n_valid, xs, wg, wu, wd)


def _sc_gather_rows(table, idx_row):
    m = idx_row.shape[1]
    w = table.shape[1]
    idx_row = idx_row.reshape(m // SC_GATHER_WINDOW, SC_GATHER_WINDOW)
    mesh = plsc.VectorSubcoreMesh(core_axis_name="c", subcore_axis_name="s")

    @functools.partial(pl.kernel, mesh=mesh,
                       out_type=jax.ShapeDtypeStruct((m, w), table.dtype))
    def gather(table_hbm, idx_hbm, out_hbm):
        def body(idx_vmem, out_vmem):
            pltpu.sync_copy(table_hbm.at[idx_vmem.at[0]], out_vmem)

        pltpu.emit_pipeline(
            body,
            grid=(m // SC_GATHER_WINDOW,),
            in_specs=[pl.BlockSpec((1, SC_GATHER_WINDOW), lambda i: (i, 0))],
            out_specs=[pl.BlockSpec((SC_GATHER_WINDOW, w), lambda i: (i, 0))],
            core_axis_name=("c", "s"),
            dimension_semantics=(pltpu.PARALLEL,),
        )(idx_hbm, out_hbm)

    return gather(table, idx_row)


def _sc_scatter_rows(rows, idx_blocks, m_out):
    s, w = rows.shape
    mesh = plsc.VectorSubcoreMesh(core_axis_name="c", subcore_axis_name="s")

    @functools.partial(pl.kernel, mesh=mesh,
                       out_type=jax.ShapeDtypeStruct((m_out, w), rows.dtype))
    def scatter(rows_hbm, idx_hbm, out_hbm):
        def body(rows_vmem, idx_vmem):
            for k in range(TOP_K):
                pltpu.sync_copy(rows_vmem, out_hbm.at[idx_vmem.at[k]])

        pltpu.emit_pipeline(
            body,
            grid=(s // SC_GATHER_WINDOW,),
            in_specs=[pl.BlockSpec((SC_GATHER_WINDOW, w), lambda i: (i, 0)),
                      pl.BlockSpec((TOP_K, SC_GATHER_WINDOW), lambda i: (i, 0))],
            out_specs=[],
            core_axis_name=("c", "s"),
            dimension_semantics=(pltpu.PARALLEL,),
        )(rows_hbm, idx_hbm)

    return scatter(rows, idx_blocks)


def _shared_kernel(hf_ref, sg_ref, su_ref, sd_ref, o_ref):
    o_ref[...] = _pack_halves(_swiglu_packed(hf_ref[...], sg_ref, su_ref, sd_ref))


def _shared(hfp, sg, su, sd, tm):
    s, dh = hfp.shape
    d, f = sg.shape
    row = pl.BlockSpec((tm, dh), lambda i: (i, 0))
    return pl.pallas_call(
        _shared_kernel,
        grid=(s // tm,),
        in_specs=[row, pl.BlockSpec((d, f), lambda i: (0, 0)),
                  pl.BlockSpec((d, f), lambda i: (0, 0)), pl.BlockSpec((f, d), lambda i: (0, 0))],
        out_specs=row,
        out_shape=jax.ShapeDtypeStruct((s, dh), jnp.uint32),
        compiler_params=_cparams(("arbitrary",)),
        name="shared",
    )(hfp, sg, su, sd)


def _combine_kernel(wt_ref, sh_ref, x1_ref, gt_ref, g_ref, o_ref):
    tc = x1_ref.shape[0]
    y = jnp.concatenate(_unpack_halves(sh_ref[...]), axis=1)
    wt = wt_ref[...]
    n = g_ref.shape[2]
    r_lo = jnp.zeros((tc, n), F32)
    r_hi = jnp.zeros((tc, n), F32)
    for k in range(TOP_K):
        lo, hi = _unpack_halves(g_ref[k])
        r_lo = r_lo + lo * wt[:, k:k + 1]
        r_hi = r_hi + hi * wt[:, k:k + 1]
    y = y + jnp.concatenate([r_lo, r_hi], axis=1)
    o_ref[...] = x1_ref[...] + gt_ref[...] * y


def _combine(wts_t, shared, x1, gt, gathered, tc):
    s, d = x1.shape
    row = pl.BlockSpec((tc, d), lambda i: (i, 0))
    return pl.pallas_call(
        _combine_kernel,
        grid=(s // tc,),
        in_specs=[pl.BlockSpec((tc, TOP_K), lambda i: (i, 0)),
                  pl.BlockSpec((tc, d // 2), lambda i: (i, 0)), row,
                  pl.BlockSpec((1, d), lambda i: (0, 0)),
                  pl.BlockSpec((TOP_K, tc, d // 2), lambda i: (0, i, 0))],
        out_specs=row,
        out_shape=jax.ShapeDtypeStruct((s, d), F32),
        compiler_params=_cparams(("arbitrary",)),
        name="combine",
    )(wts_t, shared, x1, gt, gathered)


def _tile(n, want):
    t = min(n, want)
    assert n % t == 0, (n, t)
    return t


def _layer(l, x, c_col, pos_row, p):
    s, d = x.shape
    lambda_init = 0.8 - 0.6 * math.exp(-0.3 * l)
    gqk, gv = GLA_HEADS * GLA_DK, GLA_HEADS * GLA_DV
    dqk, dvw = DIFF_HEADS * 2 * DIFF_DH, DIFF_HEADS * DIFF_DV
    lowrank = p["gla_w_a2"].shape[0]

    mod = _ada(c_col, p["w_ada"], p["b_ada"][None, :])
    sh_a, sc_a, gt_a, sh_f, sc_f, gt_f = [mod[:, j * d:(j + 1) * d] for j in range(6)]

    w_in = p["w_in"]
    o = 0
    cols = {}
    for name, wdt in (("gq", gqk), ("gk", gqk), ("gv", gv), ("ga", lowrank), ("gg", gv),
                      ("dq", dqk), ("dk", dqk), ("dv", dvw), ("mg", d), ("md", d)):
        cols[name] = w_in[:, o:o + wdt]
        o += wdt
    w_ga = jnp.pad(cols["ga"], ((0, 0), (0, 128 - lowrank))).astype(BF16)

    g1 = p["norm1_g"][None, :]
    ts = _tile(s, TILE_SEQ)

    invf = ROPE_THETA ** (-jnp.arange(0, ROT_DIM, 2, dtype=F32) / ROT_DIM)
    qat, qbt, kr, vte, qn, kn, gkt = _qkvprep(
        x, g1, sc_a, sh_a, *(cols[n].T.astype(BF16) for n in ("dq", "dk", "dv", "gk")),
        pos_row, invf[:, None], p["diff_qnorm_g"][:, None], p["diff_knorm_g"][:, None], ts)

    wa2t = jnp.pad(p["gla_w_a2"].T, ((0, 0), (0, 128 - lowrank)))
    o_gla = _gla(x, g1, sc_a, sh_a, cols["gq"].astype(BF16), cols["gv"].astype(BF16),
                 cols["gg"].astype(BF16), w_ga, gkt, wa2t, p["gla_b_a"][:, None],
                 p["gla_onorm_g"][None, :], ts)
    o_diff = _diffattn(qat, qbt, kr, vte, qn, kn, p["diff_lq1"][None, :], p["diff_lk1"][None, :],
                       p["diff_lq2"][None, :], p["diff_lk2"][None, :],
                       p["diff_subln_g"][:, None], lambda_init, _tile(s, TILE_ATT_Q), ts)

    e = p["w_router"].shape[1]
    x1, hfp, idx, wts, rnk, cnt = _mergeout(
        o_gla, o_diff, x, g1, sc_a, sh_a, cols["mg"].astype(BF16), cols["md"].astype(BF16),
        p["w_branch_gla"].astype(BF16), p["w_branch_diff"].astype(BF16),
        p["w_out"].astype(BF16), gt_a, p["norm2_g"][None, :], sc_f, sh_f,
        p["w_router"].T, p["router_bias"][:, None], ts)

    counts = cnt[:, 0]
    pcounts = ((counts + MOE_ROWS - 1) // MOE_ROWS) * MOE_ROWS
    pend = jnp.cumsum(pcounts)
    pstart = pend - pcounts
    pos = _positions(idx, rnk, pstart[:, None], ts)
    n_items = (s * TOP_K) // MOE_ROWS + e
    n_valid = (pend[-1] // MOE_ROWS).astype(jnp.int32)
    item_b = jnp.minimum(jnp.arange(n_items, dtype=jnp.int32), n_valid - 1)
    item_e = jnp.minimum(jnp.sum(pend[None, :] <= (item_b * MOE_ROWS)[:, None], axis=1),
                         e - 1).astype(jnp.int32)

    wn = SC_GATHER_WINDOW
    pos_w = pos.reshape(TOP_K, s // wn, wn).transpose(1, 0, 2).reshape(s // wn * TOP_K, wn)
    xs = _sc_scatter_rows(hfp, pos_w, n_items * MOE_ROWS)
    item_rows = jnp.clip(pstart[item_e] + counts[item_e] - item_b * MOE_ROWS, 0,
                         MOE_ROWS).astype(jnp.int32)
    prev_e = jnp.concatenate([jnp.full((1,), -1, jnp.int32), item_e[:-1]])
    item_first = ((jnp.arange(n_items) < n_valid) & (item_e != prev_e)).astype(jnp.int32)
    item_slot = ((jnp.cumsum(item_first) - 1) % MOE_SLOTS).astype(jnp.int32)
    cand = jnp.where(pcounts > 0, jnp.arange(e, dtype=jnp.int32), e)
    nonempty_from = lax.cummin(cand[::-1])[::-1]
    following = jnp.concatenate([nonempty_from[1:], jnp.full((2,), e, jnp.int32)])
    ahead = item_e
    lead = [nonempty_from[0]]
    for _ in range(MOE_SLOTS - 1):
        ahead = following[ahead]
        lead.append(following[lead[-1]])
    item_next = jnp.where(ahead < e, ahead, -1).astype(jnp.int32)
    lead = jnp.stack(lead[:MOE_SLOTS - 1])
    lead = jnp.where(lead < e, lead, -1).astype(jnp.int32)
    ys = _moe(item_e, item_b, item_first, item_slot, item_next, lead, item_rows, n_valid[None],
              xs, p["w_exp_gate"], p["w_exp_up"], p["w_exp_down"])
    gathered = _sc_gather_rows(ys, pos.reshape(1, TOP_K * s)).reshape(TOP_K, s, d // 2)
    shared = _shared(hfp, p["w_sh_gate"].astype(BF16), p["w_sh_up"].astype(BF16),
                     p["w_sh_down"].astype(BF16), ts)
    return _combine(wts.T, shared, x1, gt_f, gathered, _tile(s, TILE_COMBINE))


_LAYER_PARAMS = ("w_ada", "b_ada", "norm1_g", "w_in", "gla_w_a2", "gla_b_a", "gla_onorm_g",
                 "diff_qnorm_g", "diff_knorm_g", "diff_lq1", "diff_lk1", "diff_lq2", "diff_lk2",
                 "diff_subln_g", "w_branch_gla", "w_branch_diff", "w_out", "norm2_g", "w_router",
                 "router_bias", "w_exp_gate", "w_exp_up", "w_exp_down", "w_sh_gate", "w_sh_up",
                 "w_sh_down")


def kernel(x, c, positions, w_ada, b_ada, norm1_g, w_in, gla_w_a2, gla_b_a, gla_onorm_g, diff_qnorm_g, diff_knorm_g, diff_lq1, diff_lk1, diff_lq2, diff_lk2, diff_subln_g, w_branch_gla, w_branch_diff, w_out, norm2_g, w_router, router_bias, w_exp_gate, w_exp_up, w_exp_down, w_sh_gate, w_sh_up, w_sh_down):
    stacked = dict(zip(_LAYER_PARAMS, (
        w_ada, b_ada, norm1_g, w_in, gla_w_a2, gla_b_a, gla_onorm_g, diff_qnorm_g, diff_knorm_g,
        diff_lq1, diff_lk1, diff_lq2, diff_lk2, diff_subln_g, w_branch_gla, w_branch_diff, w_out,
        norm2_g, w_router, router_bias, w_exp_gate, w_exp_up, w_exp_down, w_sh_gate, w_sh_up,
        w_sh_down)))
    b, s, d = x.shape
    assert b == 1, "single-sequence kernel"
    xl = x[0]
    c_col = c[0][:, None]
    pos_row = positions.astype(jnp.int32)
    for l in range(w_ada.shape[0]):
        xl = _layer(l, xl, c_col, pos_row, {k: v[l] for k, v in stacked.items()})
    return xl[None]
```
